```python
import math
import jax, jax.numpy as jnp
from jax import lax
import numpy as np

D_MODEL = 1024
BATCH = 8
SEQ = 16384
DEPTH = 1

HEAD_DIM = 64
N_HEADS_A = 8
N_KV_A = 2
N_HEADS_B = 8
D_A = N_HEADS_A * HEAD_DIM
D_KV_A = N_KV_A * HEAD_DIM
D_B = N_HEADS_B * HEAD_DIM
D_MIX = D_A + D_B
D_IN = D_A + 2 * D_KV_A + 3 * D_B
D_FF = 4 * D_MODEL
D_PLE = 256
GRID_W = 64
ROPE_THETA = 10000.0
ROPE_HALF = HEAD_DIM // 2
Q_BLOCK = 128
DILATED_PATTERNS = ((128, 1), (512, 4), (2048, 16))
N_BUCKETS = 32
MAX_DISTANCE = 1024
EPS = 1e-6
NEG_BIG = -1e30

kernel_name = "hymba_axial_gqa_dilated_swa_encoder_layer"


def rms_norm(x, g):
    xf = x.astype(jnp.float32)
    y = xf * lax.rsqrt(jnp.mean(xf * xf, axis=-1, keepdims=True) + EPS)
    return (y * g.astype(jnp.float32)).astype(x.dtype)


def axial_rope_tables(n_tokens):
    rows = n_tokens // GRID_W
    row = jnp.broadcast_to(jnp.arange(rows)[:, None], (rows, GRID_W)).reshape(-1).astype(jnp.float32)
    col = jnp.broadcast_to(jnp.arange(GRID_W)[None, :], (rows, GRID_W)).reshape(-1).astype(jnp.float32)
    n_axis = ROPE_HALF // 2
    inv_freq = ROPE_THETA ** (-jnp.arange(n_axis, dtype=jnp.float32) / n_axis)
    ang = jnp.concatenate([row[:, None] * inv_freq, col[:, None] * inv_freq], axis=-1)
    return jnp.cos(ang), jnp.sin(ang)


def apply_rope(x, cos, sin):
    xf = x.astype(jnp.float32)
    x1, x2 = xf[..., :ROPE_HALF], xf[..., ROPE_HALF:]
    c, s = cos[None, :, None, :], sin[None, :, None, :]
    return jnp.concatenate([x1 * c - x2 * s, x2 * c + x1 * s], axis=-1).astype(x.dtype)


def mixer_a(q, k, v, g_q, g_k, cos, sin):
    q = apply_rope(rms_norm(q, g_q), cos, sin)
    k = apply_rope(rms_norm(k, g_k), cos, sin)
    b, s_len = q.shape[0], q.shape[1]
    grp = N_HEADS_A // N_KV_A
    nblk = s_len // Q_BLOCK
    qb = q.reshape(b, nblk, Q_BLOCK, N_KV_A, grp, HEAD_DIM).transpose(1, 0, 2, 3, 4, 5)
    scale = HEAD_DIM ** -0.5

    def block(q_blk):
        sc = jnp.einsum('bqkgd,bskd->bkgqs', q_blk, k, preferred_element_type=jnp.float32) * scale
        pr = jax.nn.softmax(sc, axis=-1)
        return jnp.einsum('bkgqs,bskd->bqkgd', pr.astype(v.dtype), v)

    o = lax.map(block, qb)
    return o.transpose(1, 0, 2, 3, 4, 5).reshape(b, s_len, D_A)


def t5_bucket(rel):
    nb = N_BUCKETS // 2
    max_exact = nb // 2
    side = jnp.where(rel > 0, nb, 0)
    n = jnp.abs(rel)
    large = max_exact + (jnp.log(jnp.maximum(n, max_exact).astype(jnp.float32) / max_exact)
                         / math.log(MAX_DISTANCE / max_exact) * (nb - max_exact)).astype(jnp.int32)
    large = jnp.minimum(large, nb - 1)
    return side + jnp.where(n < max_exact, n, large)


def dilated_pattern(q, k, v, rel_bias, window, dilation):
    b, s_len, h, d = q.shape
    length = s_len // dilation
    half = window // (2 * dilation)
    blk = half
    nblk = -(-length // blk)
    lp = nblk * blk

    def to_sub(x):
        x = x.reshape(b, length, dilation, h, d).transpose(0, 2, 1, 3, 4)
        return jnp.pad(x, ((0, 0), (0, 0), (0, lp - length), (0, 0), (0, 0)))

    def band(x):
        xp = jnp.pad(to_sub(x), ((0, 0), (0, 0), (blk, blk), (0, 0), (0, 0)))
        xb = xp.reshape(b, dilation, nblk + 2, blk, h, d)
        return jnp.concatenate([xb[:, :, :-2], xb[:, :, 1:-1], xb[:, :, 2:]], axis=3)

    qs = to_sub(q).reshape(b, dilation, nblk, blk, h, d)
    ks, vs = band(k), band(v)
    qi = jnp.arange(blk)
    kj = jnp.arange(3 * blk)
    rel = kj[None, :] - blk - qi[:, None]
    key_m = jnp.arange(nblk)[:, None] * blk - blk + kj[None, :]
    valid = ((jnp.abs(rel) <= half)[None, :, :]
             & (key_m >= 0)[:, None, :] & (key_m < length)[:, None, :])
    bias = rel_bias[t5_bucket(rel * dilation)].astype(jnp.float32)
    bias = bias.transpose(0, 2, 1)[None, None, None]
    sc = jnp.einsum('brnqhd,brnkhd->brnqhk', qs, ks, preferred_element_type=jnp.float32) * (HEAD_DIM ** -0.5)
    sc = jnp.where(valid[None, None, :, :, None, :], sc + bias, NEG_BIG)
    mx = jnp.max(sc, axis=-1)
    e = jnp.exp(sc - mx[..., None])
    den = jnp.sum(e, axis=-1)
    o = jnp.einsum('brnqhk,brnkhd->brnqhd', e.astype(vs.dtype), vs,
                   preferred_element_type=jnp.float32) / den[..., None]

    def from_sub(y):
        y = y.reshape((b, dilation, lp) + y.shape[4:])[:, :, :length]
        return jnp.moveaxis(y, 1, 2).reshape((b, s_len) + y.shape[3:])

    return from_sub(o), from_sub(mx), from_sub(den)


def mixer_b(q, k, v, rel_bias):
    outs, mxs, dens = [], [], []
    for window, dilation in DILATED_PATTERNS:
        o, mx, den = dilated_pattern(q, k, v, rel_bias, window, dilation)
        outs.append(o); mxs.append(mx); dens.append(den)
    m_all = jnp.maximum(jnp.maximum(mxs[0], mxs[1]), mxs[2])
    w = [den * jnp.exp(mx - m_all) for den, mx in zip(dens, mxs)]
    w_sum = w[0] + w[1] + w[2]
    y = (w[0][..., None] * outs[0] + w[1][..., None] * outs[1] + w[2][..., None] * outs[2]) / w_sum[..., None]
    b, s_len = q.shape[0], q.shape[1]
    return y.reshape(b, s_len, D_B).astype(q.dtype)


def _fwd_setup_inputs(seed: int = 0) -> dict:
    key = jax.random.key(seed)
    ks = jax.random.split(key, 20)
    nrm = lambda k, shape, scale: jax.random.normal(k, shape, jnp.float32) * scale
    gain = lambda k, shape: 1.0 + 0.05 * jax.random.normal(k, shape, jnp.float32)
    return {
        "x": nrm(ks[0], (BATCH, SEQ, D_MODEL), 1.0),
        "p": nrm(ks[1], (DEPTH, BATCH, SEQ, D_PLE), 1.0),
        "w_in": nrm(ks[2], (DEPTH, D_MODEL, D_IN), D_MODEL ** -0.5),
        "g_attn_pre": gain(ks[3], (DEPTH, D_MODEL)),
        "g_q": gain(ks[4], (DEPTH, HEAD_DIM)),
        "g_k": gain(ks[5], (DEPTH, HEAD_DIM)),
        "g_out_a": gain(ks[6], (DEPTH, D_A)),
        "g_out_b": gain(ks[7], (DEPTH, D_B)),
        "w_out": nrm(ks[8], (DEPTH, D_MIX, D_MODEL), D_MIX ** -0.5),
        "g_attn_post": gain(ks[9], (DEPTH, D_MODEL)),
        "rel_bias": nrm(ks[10], (N_BUCKETS, N_HEADS_B), 0.5),
        "g_mlp_pre": gain(ks[11], (DEPTH, D_MODEL)),
        "w_ff1": nrm(ks[12], (DEPTH, D_MODEL, D_FF), D_MODEL ** -0.5),
        "w_ff2": nrm(ks[13], (DEPTH, D_FF, D_MODEL), D_FF ** -0.5),
        "g_mlp_post": gain(ks[14], (DEPTH, D_MODEL)),
        "g_ple": gain(ks[15], (DEPTH, D_MODEL)),
        "w_ple_gate": nrm(ks[16], (DEPTH, D_MODEL, D_MODEL), D_MODEL ** -0.5),
        "w_ple_proj": nrm(ks[17], (DEPTH, D_PLE, D_MODEL), D_PLE ** -0.5),
    }


def _fwd_reference(x, p, w_in, g_attn_pre, g_q, g_k, g_out_a, g_out_b, w_out, g_attn_post, rel_bias,
              g_mlp_pre, w_ff1, w_ff2, g_mlp_post, g_ple, w_ple_gate, w_ple_proj):
    b, s_len = x.shape[0], x.shape[1]
    cos, sin = axial_rope_tables(s_len)
    split_at = [D_A, D_A + D_KV_A, D_A + 2 * D_KV_A, D_A + 2 * D_KV_A + D_B, D_A + 2 * D_KV_A + 2 * D_B]
    h = x
    for i in range(DEPTH):
        xn = rms_norm(h, g_attn_pre[i])
        proj = jnp.einsum('bsd,de->bse', xn, w_in[i])
        qa, ka, va, qb, kb, vb = jnp.split(proj, split_at, axis=-1)
        ya = mixer_a(qa.reshape(b, s_len, N_HEADS_A, HEAD_DIM),
                     ka.reshape(b, s_len, N_KV_A, HEAD_DIM),
                     va.reshape(b, s_len, N_KV_A, HEAD_DIM), g_q[i], g_k[i], cos, sin)
        yb = mixer_b(qb.reshape(b, s_len, N_HEADS_B, HEAD_DIM),
                     kb.reshape(b, s_len, N_HEADS_B, HEAD_DIM),
                     vb.reshape(b, s_len, N_HEADS_B, HEAD_DIM), rel_bias)
        y = jnp.concatenate([rms_norm(ya, g_out_a[i]), rms_norm(yb, g_out_b[i])], axis=-1)
        y = jnp.einsum('bse,ed->bsd', y, w_out[i])
        h = h + rms_norm(y, g_attn_post[i])
        xn = rms_norm(h, g_mlp_pre[i])
        f = jnp.square(jax.nn.relu(jnp.einsum('bsd,df->bsf', xn, w_ff1[i])))
        f = jnp.einsum('bsf,fd->bsd', f, w_ff2[i])
        h = h + rms_norm(f, g_mlp_post[i])
        gate = jax.nn.sigmoid(jnp.einsum('bsd,de->bse', rms_norm(h, g_ple[i]), w_ple_gate[i]))
        h = h + gate * jnp.einsum('bsp,pd->bsd', p[i], w_ple_proj[i])
    return h


import jax as _jax
import jax.numpy as _jnp

TWIN_FORMAT = 'train_step'
FWD_PARAMS = ['x', 'p', 'w_in', 'g_attn_pre', 'g_q', 'g_k', 'g_out_a', 'g_out_b', 'w_out', 'g_attn_post', 'rel_bias', 'g_mlp_pre', 'w_ff1', 'w_ff2', 'g_mlp_post', 'g_ple', 'w_ple_gate', 'w_ple_proj']
TWIN_WEIGHTS = ['w_in', 'g_attn_pre', 'g_q', 'g_k', 'g_out_a', 'g_out_b', 'w_out', 'g_attn_post', 'rel_bias', 'g_mlp_pre', 'w_ff1', 'w_ff2', 'g_mlp_post', 'g_ple', 'w_ple_gate', 'w_ple_proj']
TWIN_DIFF_INPUT = 'x'
TWIN_INPUTS = ['x', 'p', 'w_in', 'g_attn_pre', 'g_q', 'g_k', 'g_out_a', 'g_out_b', 'w_out', 'g_attn_post', 'rel_bias', 'g_mlp_pre', 'w_ff1', 'w_ff2', 'g_mlp_post', 'g_ple', 'w_ple_gate', 'w_ple_proj', 'loss_target', 'm_w_in', 'm_g_attn_pre', 'm_g_q', 'm_g_k', 'm_g_out_a', 'm_g_out_b', 'm_w_out', 'm_g_attn_post', 'm_rel_bias', 'm_g_mlp_pre', 'm_w_ff1', 'm_w_ff2', 'm_g_mlp_post', 'm_g_ple', 'm_w_ple_gate', 'm_w_ple_proj', 'v_w_in', 'v_g_attn_pre', 'v_g_q', 'v_g_k', 'v_g_out_a', 'v_g_out_b', 'v_w_out', 'v_g_attn_post', 'v_rel_bias', 'v_g_mlp_pre', 'v_w_ff1', 'v_w_ff2', 'v_g_mlp_post', 'v_g_ple', 'v_w_ple_gate', 'v_w_ple_proj']
TWIN_OUTPUTS = ['loss', 'grad_x', 'grad_w_in', 'grad_g_attn_pre', 'grad_g_q', 'grad_g_k', 'grad_g_out_a', 'grad_g_out_b', 'grad_w_out', 'grad_g_attn_post', 'grad_rel_bias', 'grad_g_mlp_pre', 'grad_w_ff1', 'grad_w_ff2', 'grad_g_mlp_post', 'grad_g_ple', 'grad_w_ple_gate', 'grad_w_ple_proj', 'delta_w_in', 'delta_g_attn_pre', 'delta_g_q', 'delta_g_k', 'delta_g_out_a', 'delta_g_out_b', 'delta_w_out', 'delta_g_attn_post', 'delta_rel_bias', 'delta_g_mlp_pre', 'delta_w_ff1', 'delta_w_ff2', 'delta_g_mlp_post', 'delta_g_ple', 'delta_w_ple_gate', 'delta_w_ple_proj', 'new_m_w_in', 'new_m_g_attn_pre', 'new_m_g_q', 'new_m_g_k', 'new_m_g_out_a', 'new_m_g_out_b', 'new_m_w_out', 'new_m_g_attn_post', 'new_m_rel_bias', 'new_m_g_mlp_pre', 'new_m_w_ff1', 'new_m_w_ff2', 'new_m_g_mlp_post', 'new_m_g_ple', 'new_m_w_ple_gate', 'new_m_w_ple_proj', 'new_v_w_in', 'new_v_g_attn_pre', 'new_v_g_q', 'new_v_g_k', 'new_v_g_out_a', 'new_v_g_out_b', 'new_v_w_out', 'new_v_g_attn_post', 'new_v_rel_bias', 'new_v_g_mlp_pre', 'new_v_w_ff1', 'new_v_w_ff2', 'new_v_g_mlp_post', 'new_v_g_ple', 'new_v_w_ple_gate', 'new_v_w_ple_proj']
TWIN_LEAF_KINDS = {'loss': 'loss', 'grad_x': 'grad_x', 'grad_w_in': 'grad_w', 'grad_g_attn_pre': 'grad_w', 'grad_g_q': 'grad_w', 'grad_g_k': 'grad_w', 'grad_g_out_a': 'grad_w', 'grad_g_out_b': 'grad_w', 'grad_w_out': 'grad_w', 'grad_g_attn_post': 'grad_w', 'grad_rel_bias': 'grad_w', 'grad_g_mlp_pre': 'grad_w', 'grad_w_ff1': 'grad_w', 'grad_w_ff2': 'grad_w', 'grad_g_mlp_post': 'grad_w', 'grad_g_ple': 'grad_w', 'grad_w_ple_gate': 'grad_w', 'grad_w_ple_proj': 'grad_w', 'delta_w_in': 'delta_w', 'delta_g_attn_pre': 'delta_w', 'delta_g_q': 'delta_w', 'delta_g_k': 'delta_w', 'delta_g_out_a': 'delta_w', 'delta_g_out_b': 'delta_w', 'delta_w_out': 'delta_w', 'delta_g_attn_post': 'delta_w', 'delta_rel_bias': 'delta_w', 'delta_g_mlp_pre': 'delta_w', 'delta_w_ff1': 'delta_w', 'delta_w_ff2': 'delta_w', 'delta_g_mlp_post': 'delta_w', 'delta_g_ple': 'delta_w', 'delta_w_ple_gate': 'delta_w', 'delta_w_ple_proj': 'delta_w', 'new_m_w_in': 'new_m', 'new_m_g_attn_pre': 'new_m', 'new_m_g_q': 'new_m', 'new_m_g_k': 'new_m', 'new_m_g_out_a': 'new_m', 'new_m_g_out_b': 'new_m', 'new_m_w_out': 'new_m', 'new_m_g_attn_post': 'new_m', 'new_m_rel_bias': 'new_m', 'new_m_g_mlp_pre': 'new_m', 'new_m_w_ff1': 'new_m', 'new_m_w_ff2': 'new_m', 'new_m_g_mlp_post': 'new_m', 'new_m_g_ple': 'new_m', 'new_m_w_ple_gate': 'new_m', 'new_m_w_ple_proj': 'new_m', 'new_v_w_in': 'new_v', 'new_v_g_attn_pre': 'new_v', 'new_v_g_q': 'new_v', 'new_v_g_k': 'new_v', 'new_v_g_out_a': 'new_v', 'new_v_g_out_b': 'new_v', 'new_v_w_out': 'new_v', 'new_v_g_attn_post': 'new_v', 'new_v_rel_bias': 'new_v', 'new_v_g_mlp_pre': 'new_v', 'new_v_w_ff1': 'new_v', 'new_v_w_ff2': 'new_v', 'new_v_g_mlp_post': 'new_v', 'new_v_g_ple': 'new_v', 'new_v_w_ple_gate': 'new_v', 'new_v_w_ple_proj': 'new_v'}


def _forward(args):
    return _fwd_reference(*[args[k] for k in FWD_PARAMS])


def _output_shape():
    def fwd():
        inp = _fwd_setup_inputs(0)
        return _fwd_reference(*[inp[k] for k in FWD_PARAMS])
    out = _jax.eval_shape(fwd)
    return out.shape, out.dtype

N_MICROBATCH = 1
ADAM_LR = 0.001
ADAM_B1 = 0.9
ADAM_B2 = 0.999
ADAM_EPS = 1e-08
ADAM_WD = 0.01
ADAM_STEP = 10
PER_EXAMPLE_BATCH_AXIS = {'x': 0, 'p': 1, 'loss_target': 0}
SHARED_INPUTS = []
_WEIGHT_DTYPES = {'w_in': _jnp.float32, 'g_attn_pre': _jnp.float32, 'g_q': _jnp.float32, 'g_k': _jnp.float32, 'g_out_a': _jnp.float32, 'g_out_b': _jnp.float32, 'w_out': _jnp.float32, 'g_attn_post': _jnp.float32, 'rel_bias': _jnp.float32, 'g_mlp_pre': _jnp.float32, 'w_ff1': _jnp.float32, 'w_ff2': _jnp.float32, 'g_mlp_post': _jnp.float32, 'g_ple': _jnp.float32, 'w_ple_gate': _jnp.float32, 'w_ple_proj': _jnp.float32}
MOMENT_SCALE = {'w_in': 3.430270e+00, 'g_attn_pre': 5.846437e+00, 'g_q': 6.147860e+00, 'g_k': 6.952018e+00, 'g_out_a': 7.664204e+00, 'g_out_b': 2.793124e+00, 'w_out': 5.648711e+00, 'g_attn_post': 1.270072e+02, 'rel_bias': 1.226767e+00, 'g_mlp_pre': 3.726178e+00, 'w_ff1': 1.776128e+00, 'w_ff2': 6.642932e+00, 'g_mlp_post': 1.304303e+02, 'g_ple': 4.927707e+00, 'w_ple_gate': 2.830124e+00, 'w_ple_proj': 1.494895e+00}


def _to_microbatches(a, axis):
    t = _jnp.moveaxis(a, axis, 0)
    t = t.reshape((N_MICROBATCH, t.shape[0] // N_MICROBATCH) + t.shape[1:])
    return _jnp.moveaxis(t, 1, axis + 1)


def setup_inputs(seed: int = 0) -> dict:
    inp = _fwd_setup_inputs(seed)
    key = _jax.random.fold_in(_jax.random.key(seed), 7919)
    shape, _ = _output_shape()
    out = dict(inp)
    out["loss_target"] = _jax.random.normal(_jax.random.fold_in(key, 0), shape, _jnp.float32)
    for i, name in enumerate(TWIN_WEIGHTS):
        w = inp[name].astype(_jnp.float32)
        if MOMENT_SCALE is None:
            s = _jnp.sqrt(_jnp.mean(_jnp.square(w)) + 1e-30)
        else:
            s = MOMENT_SCALE[name]
        km, kv = _jax.random.split(_jax.random.fold_in(key, i + 1))
        out[name] = w
        out["m_" + name] = s * _jax.random.normal(km, w.shape, _jnp.float32)
        out["v_" + name] = (s * s) * _jax.random.uniform(kv, w.shape, _jnp.float32, 0.5, 1.5)
    if N_MICROBATCH > 1:
        for name, axis in PER_EXAMPLE_BATCH_AXIS.items():
            out[name] = _to_microbatches(out[name], axis)
    return {'x': out['x'], 'p': out['p'], 'w_in': out['w_in'], 'g_attn_pre': out['g_attn_pre'], 'g_q': out['g_q'], 'g_k': out['g_k'], 'g_out_a': out['g_out_a'], 'g_out_b': out['g_out_b'], 'w_out': out['w_out'], 'g_attn_post': out['g_attn_post'], 'rel_bias': out['rel_bias'], 'g_mlp_pre': out['g_mlp_pre'], 'w_ff1': out['w_ff1'], 'w_ff2': out['w_ff2'], 'g_mlp_post': out['g_mlp_post'], 'g_ple': out['g_ple'], 'w_ple_gate': out['w_ple_gate'], 'w_ple_proj': out['w_ple_proj'], 'loss_target': out['loss_target'], 'm_w_in': out['m_w_in'], 'm_g_attn_pre': out['m_g_attn_pre'], 'm_g_q': out['m_g_q'], 'm_g_k': out['m_g_k'], 'm_g_out_a': out['m_g_out_a'], 'm_g_out_b': out['m_g_out_b'], 'm_w_out': out['m_w_out'], 'm_g_attn_post': out['m_g_attn_post'], 'm_rel_bias': out['m_rel_bias'], 'm_g_mlp_pre': out['m_g_mlp_pre'], 'm_w_ff1': out['m_w_ff1'], 'm_w_ff2': out['m_w_ff2'], 'm_g_mlp_post': out['m_g_mlp_post'], 'm_g_ple': out['m_g_ple'], 'm_w_ple_gate': out['m_w_ple_gate'], 'm_w_ple_proj': out['m_w_ple_proj'], 'v_w_in': out['v_w_in'], 'v_g_attn_pre': out['v_g_attn_pre'], 'v_g_q': out['v_g_q'], 'v_g_k': out['v_g_k'], 'v_g_out_a': out['v_g_out_a'], 'v_g_out_b': out['v_g_out_b'], 'v_w_out': out['v_w_out'], 'v_g_attn_post': out['v_g_attn_post'], 'v_rel_bias': out['v_rel_bias'], 'v_g_mlp_pre': out['v_g_mlp_pre'], 'v_w_ff1': out['v_w_ff1'], 'v_w_ff2': out['v_w_ff2'], 'v_g_mlp_post': out['v_g_mlp_post'], 'v_g_ple': out['v_g_ple'], 'v_w_ple_gate': out['v_w_ple_gate'], 'v_w_ple_proj': out['v_w_ple_proj']}


def _loss(weights, diff, rest, loss_target):
    with _jax.named_scope("forward"):
        args = {**rest, TWIN_DIFF_INPUT: diff, **{k: w.astype(_WEIGHT_DTYPES[k]) for k, w in weights.items()}}
        y = _forward(args)
    with _jax.named_scope("loss_head"):
        err = _jnp.square(y.astype(_jnp.float32) - loss_target)
        return 0.5 * _jnp.sum(_jnp.mean(err, axis=-1)) if err.ndim else 0.5 * err


def _adamw(w, g, m, v):
    m = ADAM_B1 * m + (1.0 - ADAM_B1) * g
    v = ADAM_B2 * v + (1.0 - ADAM_B2) * _jnp.square(g)
    m_hat = m / (1.0 - ADAM_B1 ** ADAM_STEP)
    v_hat = v / (1.0 - ADAM_B2 ** ADAM_STEP)
    delta = -ADAM_LR * (m_hat / (_jnp.sqrt(v_hat) + ADAM_EPS) + ADAM_WD * w)
    return delta, m, v


def reference(x, p, w_in, g_attn_pre, g_q, g_k, g_out_a, g_out_b, w_out, g_attn_post, rel_bias, g_mlp_pre, w_ff1, w_ff2, g_mlp_post, g_ple, w_ple_gate, w_ple_proj, loss_target, m_w_in, m_g_attn_pre, m_g_q, m_g_k, m_g_out_a, m_g_out_b, m_w_out, m_g_attn_post, m_rel_bias, m_g_mlp_pre, m_w_ff1, m_w_ff2, m_g_mlp_post, m_g_ple, m_w_ple_gate, m_w_ple_proj, v_w_in, v_g_attn_pre, v_g_q, v_g_k, v_g_out_a, v_g_out_b, v_w_out, v_g_attn_post, v_rel_bias, v_g_mlp_pre, v_w_ff1, v_w_ff2, v_g_mlp_post, v_g_ple, v_w_ple_gate, v_w_ple_proj):
    given = dict(x=x, p=p, w_in=w_in, g_attn_pre=g_attn_pre, g_q=g_q, g_k=g_k, g_out_a=g_out_a, g_out_b=g_out_b, w_out=w_out, g_attn_post=g_attn_post, rel_bias=rel_bias, g_mlp_pre=g_mlp_pre, w_ff1=w_ff1, w_ff2=w_ff2, g_mlp_post=g_mlp_post, g_ple=g_ple, w_ple_gate=w_ple_gate, w_ple_proj=w_ple_proj, loss_target=loss_target, m_w_in=m_w_in, m_g_attn_pre=m_g_attn_pre, m_g_q=m_g_q, m_g_k=m_g_k, m_g_out_a=m_g_out_a, m_g_out_b=m_g_out_b, m_w_out=m_w_out, m_g_attn_post=m_g_attn_post, m_rel_bias=m_rel_bias, m_g_mlp_pre=m_g_mlp_pre, m_w_ff1=m_w_ff1, m_w_ff2=m_w_ff2, m_g_mlp_post=m_g_mlp_post, m_g_ple=m_g_ple, m_w_ple_gate=m_w_ple_gate, m_w_ple_proj=m_w_ple_proj, v_w_in=v_w_in, v_g_attn_pre=v_g_attn_pre, v_g_q=v_g_q, v_g_k=v_g_k, v_g_out_a=v_g_out_a, v_g_out_b=v_g_out_b, v_w_out=v_w_out, v_g_attn_post=v_g_attn_post, v_rel_bias=v_rel_bias, v_g_mlp_pre=v_g_mlp_pre, v_w_ff1=v_w_ff1, v_w_ff2=v_w_ff2, v_g_mlp_post=v_g_mlp_post, v_g_ple=v_g_ple, v_w_ple_gate=v_w_ple_gate, v_w_ple_proj=v_w_ple_proj)
    weights = {n: given[n] for n in TWIN_WEIGHTS}
    shared = {n: given[n] for n in SHARED_INPUTS}
    per_example = {n: given[n] for n in ['x', 'p']}
    grad_fn = _jax.value_and_grad(_loss, argnums=(0, 1))

    def one_microbatch(ex, loss_target):
        ex = dict(ex)
        diff = ex.pop(TWIN_DIFF_INPUT)
        return grad_fn(weights, diff, {**shared, **ex}, loss_target)

    if N_MICROBATCH == 1:
        loss, (grad_w, grad_x) = one_microbatch(per_example, given["loss_target"])
    else:
        def body(carry, xs):
            loss_sum, grad_sum = carry
            l_k, (gw_k, gx_k) = one_microbatch(xs[0], xs[1])
            with _jax.named_scope("update"):
                return (loss_sum + l_k, _jax.tree.map(_jnp.add, grad_sum, gw_k)), gx_k

        init = (_jnp.zeros((), _jnp.float32), _jax.tree.map(_jnp.zeros_like, weights))
        (loss, grad_w), grad_x = _jax.lax.scan(body, init, (per_example, given["loss_target"]))
    with _jax.named_scope("update"):
        delta_w, new_m, new_v = {}, {}, {}
        for n in TWIN_WEIGHTS:
            delta_w[n], new_m[n], new_v[n] = _adamw(weights[n], grad_w[n], given["m_" + n], given["v_" + n])
    return (loss, grad_x, *[grad_w[n] for n in TWIN_WEIGHTS], *[delta_w[n] for n in TWIN_WEIGHTS],
            *[new_m[n] for n in TWIN_WEIGHTS], *[new_v[n] for n in TWIN_WEIGHTS])
```

```python
import functools
import math

import jax
import jax.numpy as jnp
import numpy as np
from jax import lax
from jax.experimental import pallas as pl
from jax.experimental.pallas import tpu as pltpu

F32 = jnp.float32
BF16 = jnp.bfloat16
SDS = jax.ShapeDtypeStruct

D_MODEL = 1024
HEAD_DIM = 64
N_HEADS_A = 8
N_KV_A = 2
GROUP_A = N_HEADS_A // N_KV_A
N_HEADS_B = 8
D_A = N_HEADS_A * HEAD_DIM
D_KV_A = N_KV_A * HEAD_DIM
D_B = N_HEADS_B * HEAD_DIM
D_IN = D_A + 2 * D_KV_A + 3 * D_B
D_FF = 4 * D_MODEL
D_PLE = 256
GRID_W = 64
ROPE_THETA = 10000.0
ROPE_HALF = HEAD_DIM // 2
DILATED_PATTERNS = ((128, 1), (512, 4), (2048, 16))
BAND_HALF = 64
N_BUCKETS = 32
MAX_DISTANCE = 1024
EPS = 1e-6
NEG_BIG = -1e30
SCORE_SCALE = HEAD_DIM ** -0.5

ADAM_LR = 0.001
ADAM_B1 = 0.9
ADAM_B2 = 0.999
ADAM_EPS = 1e-08
ADAM_WD = 0.01
ADAM_STEP = 10

N_DEV = 8
MESH_AXES = ("x", "y", "c")
V7X_VMEM_LIMIT = 56 * 1024 * 1024

OFF_QA, OFF_KA, OFF_VA, OFF_QB, OFF_KB, OFF_VB = 0, 512, 640, 768, 1280, 1792


def _params(n_axes, vmem=None):
    return pltpu.CompilerParams(dimension_semantics=("arbitrary",) * n_axes, vmem_limit_bytes=vmem)


def _dot(a, b):
    return jnp.dot(a, b, preferred_element_type=F32)


def _dot_nt(a, b):
    return lax.dot_general(a, b, (((1,), (1,)), ((), ())), preferred_element_type=F32)


def _dot_tn(a, b):
    return lax.dot_general(a, b, (((0,), (0,)), ((), ())), preferred_element_type=F32)


def _rms_fwd(x, g):
    r = lax.rsqrt(jnp.mean(x * x, axis=-1, keepdims=True) + EPS)
    return x * r * g, r


def _rms_bwd(x, g, dy):
    r = lax.rsqrt(jnp.mean(x * x, axis=-1, keepdims=True) + EPS)
    xh = x * r
    dxh = dy * g
    dx = r * (dxh - xh * jnp.mean(dxh * xh, axis=-1, keepdims=True))
    return dx, jnp.sum(dy * xh, axis=0, keepdims=True)


def _head_sum(v):
    head = lax.broadcasted_iota(jnp.int32, v.shape, 1) >> 6
    out = jnp.zeros_like(v)
    for h in range(v.shape[1] // HEAD_DIM):
        msk = head == h
        s = jnp.sum(jnp.where(msk, v, 0.0), axis=-1, keepdims=True)
        out = jnp.where(msk, s, out)
    return out


def _swap_halves(v):
    w = v.shape[1]
    lane = lax.broadcasted_iota(jnp.int32, v.shape, 1)
    first_half = (lane & (HEAD_DIM - 1)) < ROPE_HALF
    return jnp.where(first_half, pltpu.roll(v, w - ROPE_HALF, 1), pltpu.roll(v, ROPE_HALF, 1))


def _head_norm_rope(v, g, cos, sin_signed):
    r = lax.rsqrt(_head_sum(v * v) * (1.0 / HEAD_DIM) + EPS)
    y = v * r * g
    return y * cos + _swap_halves(y) * sin_signed


def _head_norm_rope_bwd(v, g, cos, sin_signed, dout):
    dy = dout * cos - _swap_halves(dout) * sin_signed
    r = lax.rsqrt(_head_sum(v * v) * (1.0 / HEAD_DIM) + EPS)
    xh = v * r
    dxh = dy * g
    dv = r * (dxh - xh * (_head_sum(dxh * xh) * (1.0 / HEAD_DIM)))
    return dv, jnp.sum(dy * xh, axis=0, keepdims=True)


def _row_spec(tm, n):
    return pl.BlockSpec((tm, n), lambda i: (i, 0))


def _full_spec(shape):
    nd = len(shape)
    return pl.BlockSpec(shape, lambda *_: (0,) * nd)


def _const_spec(shape):
    nd = len(shape)
    return pl.BlockSpec(shape, lambda *_: (0,) * nd, pipeline_mode=pl.Buffered(1))


def _acc(ref, first, val):
    @pl.when(first)
    def _():
        ref[...] = val

    @pl.when(jnp.logical_not(first))
    def _():
        ref[...] += val


def _inproj_fwd(x, g_pre, w_in, cos, sin, gq, gk, tm):
    s_len = x.shape[0]

    def body(x_ref, g_ref, w_ref, cos_ref, sin_ref, gq_ref, gk_ref,
             qa_raw, ka_raw, qs, kn, va, qb, kb, vb, xn_out):
        xn, _ = _rms_fwd(x_ref[...], g_ref[...])
        xn = xn.astype(BF16)
        xn_out[...] = xn
        qa = _dot(xn, w_ref[:, OFF_QA:OFF_KA])
        qa_raw[...] = qa
        qs[...] = (_head_norm_rope(qa, gq_ref[...], cos_ref[...], sin_ref[...]) * SCORE_SCALE).astype(BF16)
        ka = _dot(xn, w_ref[:, OFF_KA:OFF_VA])
        ka_raw[...] = ka
        kn[...] = _head_norm_rope(ka, gk_ref[...], cos_ref[:, :D_KV_A], sin_ref[:, :D_KV_A]).astype(BF16)
        va[...] = _dot(xn, w_ref[:, OFF_VA:OFF_QB]).astype(BF16)
        qb[...] = (_dot(xn, w_ref[:, OFF_QB:OFF_KB]) * SCORE_SCALE).astype(BF16)
        kb[...] = _dot(xn, w_ref[:, OFF_KB:OFF_VB]).astype(BF16)
        vb[...] = _dot(xn, w_ref[:, OFF_VB:D_IN]).astype(BF16)

    return pl.pallas_call(
        body, name="inproj_fwd", grid=(s_len // tm,),
        in_specs=[_row_spec(tm, D_MODEL), _full_spec((1, D_MODEL)), _const_spec((D_MODEL, D_IN)),
                  _row_spec(tm, D_A), _row_spec(tm, D_A), _full_spec((1, D_A)), _full_spec((1, D_KV_A))],
        out_specs=[_row_spec(tm, D_A), _row_spec(tm, D_KV_A), _row_spec(tm, D_A), _row_spec(tm, D_KV_A),
                   _row_spec(tm, D_KV_A), _row_spec(tm, D_B), _row_spec(tm, D_B), _row_spec(tm, D_B),
                   _row_spec(tm, D_MODEL)],
        out_shape=(SDS((s_len, D_A), F32), SDS((s_len, D_KV_A), F32), SDS((s_len, D_A), BF16),
                   SDS((s_len, D_KV_A), BF16), SDS((s_len, D_KV_A), BF16), SDS((s_len, D_B), BF16),
                   SDS((s_len, D_B), BF16), SDS((s_len, D_B), BF16), SDS((s_len, D_MODEL), BF16)),
        compiler_params=_params(1, V7X_VMEM_LIMIT),
    )(x, g_pre, w_in, cos, sin, gq, gk)


def _stack_heads(ref, tq):
    return jnp.concatenate([ref[:, HEAD_DIM * g:HEAD_DIM * (g + 1)] for g in range(GROUP_A)], axis=0)


def _stack_cols(ref, tq):
    return jnp.concatenate([ref[:, HEAD_DIM * g:HEAD_DIM * g + 1] for g in range(GROUP_A)], axis=0)


def _attn_fwd(qs, kt, v, tq, tk):
    s_len = qs.shape[0]
    nk = s_len // tk
    gw = GROUP_A * HEAD_DIM

    def body(q_ref, kt_ref, v_ref, o_ref, lse_ref):
        q = _stack_heads(q_ref, tq)

        def step(j, carry):
            m, l, acc = carry
            s = _dot(q, kt_ref[j])
            m_new = jnp.maximum(m, jnp.max(s, axis=-1, keepdims=True))
            p = jnp.exp(s - m_new)
            alpha = jnp.exp(m - m_new)
            l = alpha * l + jnp.sum(p, axis=-1, keepdims=True)
            vj = v_ref[pl.ds(pl.multiple_of(j * tk, tk), tk), :]
            acc = alpha * acc + _dot(p.astype(BF16), vj)
            return m_new, l, acc

        rows = GROUP_A * tq
        m, l, acc = lax.fori_loop(0, nk, step, (jnp.full((rows, 1), NEG_BIG, F32), jnp.zeros((rows, 1), F32),
                                                jnp.zeros((rows, HEAD_DIM), F32)))
        o = acc / l
        lse = m + jnp.log(l)
        for g in range(GROUP_A):
            o_ref[:, HEAD_DIM * g:HEAD_DIM * (g + 1)] = o[g * tq:(g + 1) * tq]
            lse_ref[:, HEAD_DIM * g:HEAD_DIM * (g + 1)] = jnp.broadcast_to(lse[g * tq:(g + 1) * tq], (tq, HEAD_DIM))

    return pl.pallas_call(
        body, name="attn_fwd", grid=(N_KV_A, s_len // tq),
        in_specs=[pl.BlockSpec((tq, gw), lambda kv, i: (i, kv)),
                  pl.BlockSpec((None, nk, HEAD_DIM, tk), lambda kv, i: (kv, 0, 0, 0)),
                  pl.BlockSpec((None, s_len, HEAD_DIM), lambda kv, i: (kv, 0, 0))],
        out_specs=[pl.BlockSpec((tq, gw), lambda kv, i: (i, kv)), pl.BlockSpec((tq, gw), lambda kv, i: (i, kv))],
        out_shape=(SDS((s_len, D_A), F32), SDS((s_len, D_A), F32)),
        compiler_params=_params(2, V7X_VMEM_LIMIT),
    )(qs, kt, v)


def _attn_bwd(qs, do, lse, delta, kt, k, vt, tq, tk):
    s_len = qs.shape[0]
    nk = s_len // tk
    nq = s_len // tq
    gw = GROUP_A * HEAD_DIM

    def body(q_ref, do_ref, lse_ref, delta_ref, kt_ref, k_ref, vt_ref, dq_ref, dk_hbm, dv_hbm, dk_acc, dv_acc):
        kv = pl.program_id(0)
        i = pl.program_id(1)

        @pl.when(i == 0)
        def _():
            dk_acc[...] = jnp.zeros_like(dk_acc)
            dv_acc[...] = jnp.zeros_like(dv_acc)

        q = _stack_heads(q_ref, tq)
        dout = _stack_heads(do_ref, tq)
        row_lse = _stack_cols(lse_ref, tq)
        row_delta = _stack_cols(delta_ref, tq)

        def step(j, dq):
            s = _dot(q, kt_ref[j])
            p = jnp.exp(s - row_lse)
            dp = _dot(dout, vt_ref[j])
            ds = (p * (dp - row_delta)).astype(BF16)
            rows = pl.ds(pl.multiple_of(j * tk, tk), tk)
            dv_acc[rows, :] += _dot_tn(p.astype(BF16), dout)
            dk_acc[rows, :] += _dot_tn(ds, q)
            return dq + _dot(ds, k_ref[rows, :])

        dq = lax.fori_loop(0, nk, step, jnp.zeros((GROUP_A * tq, HEAD_DIM), F32)) * SCORE_SCALE
        for g in range(GROUP_A):
            dq_ref[:, HEAD_DIM * g:HEAD_DIM * (g + 1)] = dq[g * tq:(g + 1) * tq]

        @pl.when(i == nq - 1)
        def _():
            pltpu.sync_copy(dk_acc, dk_hbm.at[kv])
            pltpu.sync_copy(dv_acc, dv_hbm.at[kv])

    tile = pl.BlockSpec((tq, gw), lambda kv, i: (i, kv))
    chunks = pl.BlockSpec((None, nk, HEAD_DIM, tk), lambda kv, i: (kv, 0, 0, 0))
    return pl.pallas_call(
        body, name="attn_bwd", grid=(N_KV_A, nq),
        in_specs=[tile, tile, tile, tile, chunks,
                  pl.BlockSpec((None, s_len, HEAD_DIM), lambda kv, i: (kv, 0, 0)), chunks],
        out_specs=[tile, pl.BlockSpec(memory_space=pl.ANY), pl.BlockSpec(memory_space=pl.ANY)],
        out_shape=(SDS((s_len, D_A), F32), SDS((N_KV_A, s_len, HEAD_DIM), F32), SDS((N_KV_A, s_len, HEAD_DIM), F32)),
        scratch_shapes=[pltpu.VMEM((s_len, HEAD_DIM), F32), pltpu.VMEM((s_len, HEAD_DIM), F32)],
        compiler_params=_params(2, V7X_VMEM_LIMIT),
    )(qs, do, lse, delta, kt, k, vt)


def _band_specs(length, t):
    hb = t // BAND_HALF
    last = length // BAND_HALF - 1
    main = pl.BlockSpec((t, D_B), lambda r, i: (i, r))
    prev = pl.BlockSpec((BAND_HALF, D_B), lambda r, i: (jnp.maximum(i * hb - 1, 0), r))
    nxt = pl.BlockSpec((BAND_HALF, D_B), lambda r, i: (jnp.minimum((i + 1) * hb, last), r))
    return main, [prev, main, nxt]


def _window(refs):
    return jnp.concatenate([r[...] for r in refs], axis=0)


def _head(v, h):
    return v[:, HEAD_DIM * h:HEAD_DIM * (h + 1)]


def _band_fwd(q, k, v, bias, dil, t):
    s_len = q.shape[0]
    length = s_len // dil
    w = t + 2 * BAND_HALF
    view = lambda a: a.reshape(length, dil * D_B)
    main, win = _band_specs(length, t)

    def body(q_ref, k0, k1, k2, v0, v1, v2, bias_ref, o_ref, lse_ref):
        i = pl.program_id(1)
        kw = _window((k0, k1, k2))
        vw = _window((v0, v1, v2))
        pos = i * t - BAND_HALF + lax.broadcasted_iota(jnp.int32, (1, w), 1)
        valid = (pos >= 0) & (pos < length)
        for h in range(N_HEADS_B):
            s = _dot_nt(q_ref[:, HEAD_DIM * h:HEAD_DIM * (h + 1)], _head(kw, h)) + bias_ref[h]
            s = jnp.where(valid, s, NEG_BIG)
            m = jnp.max(s, axis=-1, keepdims=True)
            e = jnp.exp(s - m)
            den = jnp.sum(e, axis=-1, keepdims=True)
            o_ref[:, HEAD_DIM * h:HEAD_DIM * (h + 1)] = _dot(e.astype(BF16), _head(vw, h)) / den
            lse_ref[:, HEAD_DIM * h:HEAD_DIM * (h + 1)] = jnp.broadcast_to(m + jnp.log(den), (t, HEAD_DIM))

    o, lse = pl.pallas_call(
        body, name=f"band_fwd_d{dil}", grid=(dil, length // t),
        in_specs=[main] + win + win + [_full_spec((N_HEADS_B, t, w))],
        out_specs=[main, main],
        out_shape=(SDS((length, dil * D_B), F32), SDS((length, dil * D_B), F32)),
        compiler_params=_params(2, V7X_VMEM_LIMIT),
    )(view(q), view(k), view(k), view(k), view(v), view(v), view(v), bias)
    return o.reshape(s_len, D_B), lse.reshape(s_len, D_B)


def _band_dq(q, do, lse, delta, k, v, bias, dq_in, dil, t):
    s_len = q.shape[0]
    length = s_len // dil
    w = t + 2 * BAND_HALF
    view = lambda a: a.reshape(length, dil * D_B)
    main, win = _band_specs(length, t)

    def body(q_ref, do_ref, lse_ref, delta_ref, k0, k1, k2, v0, v1, v2, bias_ref, acc_ref, dq_ref, dbias_ref):
        i = pl.program_id(1)
        first = (pl.program_id(0) == 0) & (i == 0)
        kw = _window((k0, k1, k2))
        vw = _window((v0, v1, v2))
        pos = i * t - BAND_HALF + lax.broadcasted_iota(jnp.int32, (1, w), 1)
        valid = (pos >= 0) & (pos < length)
        for h in range(N_HEADS_B):
            cols = slice(HEAD_DIM * h, HEAD_DIM * (h + 1))
            s = _dot_nt(q_ref[:, cols], _head(kw, h)) + bias_ref[h]
            s = jnp.where(valid, s, NEG_BIG)
            p = jnp.exp(s - lse_ref[:, HEAD_DIM * h:HEAD_DIM * h + 1])
            dp = _dot_nt(do_ref[:, cols], _head(vw, h))
            ds = p * (dp - delta_ref[:, HEAD_DIM * h:HEAD_DIM * h + 1])
            dq_ref[:, cols] = acc_ref[:, cols] + _dot(ds.astype(BF16), _head(kw, h)) * SCORE_SCALE

            @pl.when(first)
            def _():
                dbias_ref[h] = ds

            @pl.when(jnp.logical_not(first))
            def _():
                dbias_ref[h] += ds

    dq, dbias = pl.pallas_call(
        body, name=f"band_dq_d{dil}", grid=(dil, length // t),
        in_specs=[main, main, main, main] + win + win + [_full_spec((N_HEADS_B, t, w)), main],
        out_specs=[main, _full_spec((N_HEADS_B, t, w))],
        out_shape=(SDS((length, dil * D_B), F32), SDS((N_HEADS_B, t, w), F32)),
        compiler_params=_params(2, V7X_VMEM_LIMIT),
    )(view(q), view(do), view(lse), view(delta), view(k), view(k), view(k), view(v), view(v), view(v), bias,
      view(dq_in))
    return dq.reshape(s_len, D_B), dbias


def _band_dkv(k, v, q, do, lse, delta, bias_t, dk_in, dv_in, dil, t):
    s_len = q.shape[0]
    length = s_len // dil
    w = t + 2 * BAND_HALF
    view = lambda a: a.reshape(length, dil * D_B)
    main, win = _band_specs(length, t)

    def body(k_ref, v_ref, q0, q1, q2, d0, d1, d2, l0, l1, l2, e0, e1, e2, bias_ref, dk_acc, dv_acc, dk_ref, dv_ref):
        i = pl.program_id(1)
        qw = _window((q0, q1, q2))
        dow = _window((d0, d1, d2))
        lsew = _window((l0, l1, l2))
        deltaw = _window((e0, e1, e2))
        pos = i * t - BAND_HALF + lax.broadcasted_iota(jnp.int32, (w, 1), 0)
        valid = (pos >= 0) & (pos < length)
        for h in range(N_HEADS_B):
            cols = slice(HEAD_DIM * h, HEAD_DIM * (h + 1))
            qh = _head(qw, h)
            doh = _head(dow, h)
            s = _dot_nt(qh, k_ref[:, cols]) + bias_ref[h]
            s = jnp.where(valid, s, NEG_BIG)
            p = jnp.exp(s - lsew[:, HEAD_DIM * h:HEAD_DIM * h + 1])
            dp = _dot_nt(doh, v_ref[:, cols])
            ds = p * (dp - deltaw[:, HEAD_DIM * h:HEAD_DIM * h + 1])
            dv_ref[:, cols] = dv_acc[:, cols] + _dot_tn(p.astype(BF16), doh)
            dk_ref[:, cols] = dk_acc[:, cols] + _dot_tn(ds.astype(BF16), qh)

    dk, dv = pl.pallas_call(
        body, name=f"band_dkv_d{dil}", grid=(dil, length // t),
        in_specs=[main, main] + win + win + win + win + [_full_spec((N_HEADS_B, w, t)), main, main],
        out_specs=[main, main],
        out_shape=(SDS((length, dil * D_B), F32), SDS((length, dil * D_B), F32)),
        compiler_params=_params(2, V7X_VMEM_LIMIT),
    )(view(k), view(v), view(q), view(q), view(q), view(do), view(do), view(do), view(lse), view(lse), view(lse),
      view(delta), view(delta), view(delta), bias_t, view(dk_in), view(dv_in))
    return dk.reshape(s_len, D_B), dv.reshape(s_len, D_B)


def _t5_bucket_np(rel):
    nb = N_BUCKETS // 2
    max_exact = nb // 2
    side = np.where(rel > 0, nb, 0)
    n = np.abs(rel)
    ratio = np.maximum(n, max_exact).astype(np.float32) / np.float32(max_exact)
    large = max_exact + (np.log(ratio) / np.float32(math.log(MAX_DISTANCE / max_exact))
                         * np.float32(nb - max_exact)).astype(np.int32)
    large = np.minimum(large, nb - 1)
    return (side + np.where(n < max_exact, n, large)).astype(np.int32)


def _band_buckets(dil, t):
    rel = np.arange(t + 2 * BAND_HALF)[None, :] - BAND_HALF - np.arange(t)[:, None]
    bucket = _t5_bucket_np(np.clip(rel, -BAND_HALF, BAND_HALF) * dil)
    return np.where(np.abs(rel) <= BAND_HALF, bucket, -1).astype(np.int32)


def _toeplitz(vals, rows, cols):
    heads = vals.shape[0]
    period = rows + cols
    vec = jnp.concatenate([vals[:, rows - 1:], jnp.zeros((heads, 1), vals.dtype), vals[:, :rows - 1]], axis=1)
    flat = jnp.broadcast_to(vec[:, None, :], (heads, rows, period)).reshape(heads, rows * period)
    return flat[:, :rows * (period - 1)].reshape(heads, rows, period - 1)[:, :, :cols]


def _bias_tiles(rel_bias, dil, t):
    w = t + 2 * BAND_HALF
    rel = np.arange(-BAND_HALF, BAND_HALF + 1)
    bucket = _t5_bucket_np(rel * dil)
    runs, start = [], 0
    for i in range(1, len(bucket) + 1):
        if i == len(bucket) or bucket[i] != bucket[start]:
            b = int(bucket[start])
            runs.append(jnp.broadcast_to(rel_bias[b:b + 1], (i - start, N_HEADS_B)))
            start = i
    per_rel = jnp.concatenate(runs, axis=0).T

    def diagonals(lo, hi):
        left = jnp.full((N_HEADS_B, max(0, -BAND_HALF - lo)), NEG_BIG, F32)
        right = jnp.full((N_HEADS_B, max(0, hi - BAND_HALF)), NEG_BIG, F32)
        return jnp.concatenate([left, per_rel, right], axis=1)

    tile = _toeplitz(diagonals(-(t - 1) - BAND_HALF, w - 1 - BAND_HALF), t, w)
    twin = _toeplitz(diagonals(-(w - 1) + BAND_HALF, t - 1 + BAND_HALF), w, t)
    return tile, twin


def _dbias_reduce(dbias, buckets):
    n = len(dbias)

    def body(*refs):
        db_refs, bk_refs, o_ref = refs[:n], refs[n:2 * n], refs[2 * n]
        row = lax.broadcasted_iota(jnp.int32, (N_BUCKETS, 128), 0)
        lane = lax.broadcasted_iota(jnp.int32, (N_BUCKETS, 128), 1)

        def per_bucket(b, out):
            for pat in range(n):
                msk = bk_refs[pat][...] == b
                for h in range(N_HEADS_B):
                    tot = jnp.sum(jnp.where(msk, db_refs[pat][h], 0.0), axis=-1, keepdims=True)
                    tot = jnp.sum(tot, axis=0, keepdims=True)
                    out = out + jnp.where((row == b) & (lane == h), tot, 0.0)
            return out

        o_ref[...] = lax.fori_loop(0, N_BUCKETS, per_bucket, jnp.zeros((N_BUCKETS, 128), F32))

    return pl.pallas_call(
        body, name="dbias_reduce", out_shape=SDS((N_BUCKETS, 128), F32),
        compiler_params=pltpu.CompilerParams(vmem_limit_bytes=V7X_VMEM_LIMIT),
    )(*dbias, *buckets)


def _attn_out_fwd(ya, ob, lb, x, g_a, g_b, w_out, g_post, tm):
    s_len = ya.shape[0]

    def body(ya_ref, o0, o1, o2, l0, l1, l2, x_ref, ga_ref, gb_ref, w_ref, gp_ref,
             h1_ref, yo_ref, yb_ref, lse_ref, ycat_ref):
        m = jnp.maximum(jnp.maximum(l0[...], l1[...]), l2[...])
        w0, w1, w2 = jnp.exp(l0[...] - m), jnp.exp(l1[...] - m), jnp.exp(l2[...] - m)
        wsum = w0 + w1 + w2
        yb = (w0 * o0[...] + w1 * o1[...] + w2 * o2[...]) / wsum
        yb_ref[...] = yb
        lse_ref[...] = m + jnp.log(wsum)
        yan, _ = _rms_fwd(ya_ref[...], ga_ref[...])
        ybn, _ = _rms_fwd(yb, gb_ref[...])
        yan, ybn = yan.astype(BF16), ybn.astype(BF16)
        ycat_ref[:, :D_A] = yan
        ycat_ref[:, D_A:] = ybn
        yo = _dot(yan, w_ref[:D_A, :]) + _dot(ybn, w_ref[D_A:, :])
        yo_ref[...] = yo
        post, _ = _rms_fwd(yo, gp_ref[...])
        h1_ref[...] = x_ref[...] + post

    half, full = _row_spec(tm, D_A), _row_spec(tm, D_MODEL)
    return pl.pallas_call(
        body, name="attn_out_fwd", grid=(s_len // tm,),
        in_specs=[half] * 7 + [full, _full_spec((1, D_A)), _full_spec((1, D_B)), _full_spec((D_MODEL, D_MODEL)),
                               _full_spec((1, D_MODEL))],
        out_specs=[full, full, half, half, full],
        out_shape=(SDS((s_len, D_MODEL), F32), SDS((s_len, D_MODEL), F32), SDS((s_len, D_B), F32),
                   SDS((s_len, D_B), F32), SDS((s_len, D_MODEL), BF16)),
        compiler_params=_params(1, V7X_VMEM_LIMIT),
    )(ya, ob[0], ob[1], ob[2], lb[0], lb[1], lb[2], x, g_a, g_b, w_out, g_post)


def _attn_out_bwd(dh1, yo, ya, yb, g_a, g_b, w_out, g_post, tm):
    s_len = ya.shape[0]

    def body(dh1_ref, yo_ref, ya_ref, yb_ref, ga_ref, gb_ref, w_ref, gp_ref,
             dyo_ref, dya_ref, dyb_ref, dela_ref, delb_ref, dgp_ref, dga_ref, dgb_ref):
        first = pl.program_id(0) == 0
        dyo, dgp = _rms_bwd(yo_ref[...], gp_ref[...], dh1_ref[...])
        dyo = dyo.astype(BF16)
        dyo_ref[...] = dyo
        _acc(dgp_ref, first, dgp)
        dya_n = _dot_nt(dyo, w_ref[:D_A, :])
        dyb_n = _dot_nt(dyo, w_ref[D_A:, :])
        ya, yb = ya_ref[...], yb_ref[...]
        dya, dga = _rms_bwd(ya, ga_ref[...], dya_n)
        dyb, dgb = _rms_bwd(yb, gb_ref[...], dyb_n)
        _acc(dga_ref, first, dga)
        _acc(dgb_ref, first, dgb)
        dya_ref[...] = dya.astype(BF16)
        dyb_ref[...] = dyb.astype(BF16)
        dela_ref[...] = _head_sum(dya * ya)
        delb_ref[...] = _head_sum(dyb * yb)

    half, full = _row_spec(tm, D_A), _row_spec(tm, D_MODEL)
    return pl.pallas_call(
        body, name="attn_out_bwd", grid=(s_len // tm,),
        in_specs=[full, full, half, half, _full_spec((1, D_A)), _full_spec((1, D_B)),
                  _full_spec((D_MODEL, D_MODEL)), _full_spec((1, D_MODEL))],
        out_specs=[full, half, half, half, half, _full_spec((1, D_MODEL)), _full_spec((1, D_A)), _full_spec((1, D_B))],
        out_shape=(SDS((s_len, D_MODEL), BF16), SDS((s_len, D_A), BF16), SDS((s_len, D_B), BF16),
                   SDS((s_len, D_A), F32), SDS((s_len, D_B), F32), SDS((1, D_MODEL), F32), SDS((1, D_A), F32),
                   SDS((1, D_B), F32)),
        compiler_params=_params(1, V7X_VMEM_LIMIT),
    )(dh1, yo, ya, yb, g_a, g_b, w_out, g_post)


FF_CHUNK = 1024


def _mlp_fwd(h1, g_pre, w1, w2, g_post, tm):
    s_len = h1.shape[0]

    def body(h1_ref, gpre_ref, w1_ref, w2_ref, gpost_ref, h2_ref, fo_ref, xn_ref):
        h1v = h1_ref[...]
        xn, _ = _rms_fwd(h1v, gpre_ref[...])
        xn = xn.astype(BF16)
        xn_ref[...] = xn
        fo = jnp.zeros((tm, D_MODEL), F32)
        for c in range(D_FF // FF_CHUNK):
            cols = slice(c * FF_CHUNK, (c + 1) * FF_CHUNK)
            u = jnp.maximum(_dot(xn, w1_ref[:, cols]), 0.0)
            fo = fo + _dot((u * u).astype(BF16), w2_ref[cols, :])
        fo_ref[...] = fo
        post, _ = _rms_fwd(fo, gpost_ref[...])
        h2_ref[...] = h1v + post

    full = _row_spec(tm, D_MODEL)
    return pl.pallas_call(
        body, name="mlp_fwd", grid=(s_len // tm,),
        in_specs=[full, _full_spec((1, D_MODEL)), _const_spec((D_MODEL, D_FF)), _const_spec((D_FF, D_MODEL)),
                  _full_spec((1, D_MODEL))],
        out_specs=[full, full, full],
        out_shape=(SDS((s_len, D_MODEL), F32), SDS((s_len, D_MODEL), F32), SDS((s_len, D_MODEL), BF16)),
        compiler_params=_params(1, V7X_VMEM_LIMIT),
    )(h1, g_pre, w1, w2, g_post)


def _mlp_bwd(h1, dh2, fo, xn, g_pre, w1, w2, g_post, tm):
    s_len = h1.shape[0]

    def body(h1_ref, dh2_ref, fo_ref, xn_ref, gpre_ref, w1_ref, w2_ref, gpost_ref,
             dh1_ref, dfo_ref, du_ref, f_ref, dgpost_ref, dgpre_ref):
        first = pl.program_id(0) == 0
        dh2 = dh2_ref[...]
        dfo, dgpost = _rms_bwd(fo_ref[...], gpost_ref[...], dh2)
        dfo = dfo.astype(BF16)
        dfo_ref[...] = dfo
        _acc(dgpost_ref, first, dgpost)
        xn = xn_ref[...]
        dxn = jnp.zeros((tm, D_MODEL), F32)
        for c in range(D_FF // FF_CHUNK):
            cols = slice(c * FF_CHUNK, (c + 1) * FF_CHUNK)
            u = jnp.maximum(_dot(xn, w1_ref[:, cols]), 0.0)
            f_ref[:, cols] = (u * u).astype(BF16)
            du = (_dot_nt(dfo, w2_ref[cols, :]) * (2.0 * u)).astype(BF16)
            du_ref[:, cols] = du
            dxn = dxn + _dot_nt(du, w1_ref[:, cols])
        dx, dgpre = _rms_bwd(h1_ref[...], gpre_ref[...], dxn)
        _acc(dgpre_ref, first, dgpre)
        dh1_ref[...] = dh2 + dx

    full, wide = _row_spec(tm, D_MODEL), _row_spec(tm, D_FF)
    return pl.pallas_call(
        body, name="mlp_bwd", grid=(s_len // tm,),
        in_specs=[full, full, full, full, _full_spec((1, D_MODEL)), _const_spec((D_MODEL, D_FF)),
                  _const_spec((D_FF, D_MODEL)), _full_spec((1, D_MODEL))],
        out_specs=[full, full, wide, wide, _full_spec((1, D_MODEL)), _full_spec((1, D_MODEL))],
        out_shape=(SDS((s_len, D_MODEL), F32), SDS((s_len, D_MODEL), BF16), SDS((s_len, D_FF), BF16),
                   SDS((s_len, D_FF), BF16), SDS((1, D_MODEL), F32), SDS((1, D_MODEL), F32)),
        compiler_params=_params(1, V7X_VMEM_LIMIT),
    )(h1, dh2, fo, xn, g_pre, w1, w2, g_post)


def _ple_fwd_bwd(h2, p, target, g_ple, w_gate, w_proj, tm):
    s_len = h2.shape[0]

    def body(h2_ref, p_ref, t_ref, g_ref, wg_ref, wp_ref, dh2_ref, loss_ref, dg_ref, xn_ref, dgl_ref, dpp_ref, pb_ref):
        first = pl.program_id(0) == 0
        h2 = h2_ref[...]
        g = g_ref[...]
        xn, _ = _rms_fwd(h2, g)
        xn = xn.astype(BF16)
        xn_ref[...] = xn
        gate = 1.0 / (1.0 + jnp.exp(-_dot(xn, wg_ref[...])))
        pb = p_ref[...].astype(BF16)
        pb_ref[...] = pb
        pp = _dot(pb, wp_ref[...])
        diff = h2 + gate * pp - t_ref[...]
        _acc(loss_ref, first, jnp.full((8, 128), jnp.sum(diff * diff), F32))
        dh3 = diff * (1.0 / D_MODEL)
        dpp_ref[...] = (dh3 * gate).astype(BF16)
        dgl = (dh3 * pp * gate * (1.0 - gate)).astype(BF16)
        dgl_ref[...] = dgl
        dx, dg = _rms_bwd(h2, g, _dot_nt(dgl, wg_ref[...]))
        _acc(dg_ref, first, dg)
        dh2_ref[...] = dh3 + dx

    full = _row_spec(tm, D_MODEL)
    return pl.pallas_call(
        body, name="ple_fwd_bwd", grid=(s_len // tm,),
        in_specs=[full, _row_spec(tm, D_PLE), full, _full_spec((1, D_MODEL)), _full_spec((D_MODEL, D_MODEL)),
                  _full_spec((D_PLE, D_MODEL))],
        out_specs=[full, _full_spec((8, 128)), _full_spec((1, D_MODEL)), full, full, full, _row_spec(tm, D_PLE)],
        out_shape=(SDS((s_len, D_MODEL), F32), SDS((8, 128), F32), SDS((1, D_MODEL), F32), SDS((s_len, D_MODEL), BF16),
                   SDS((s_len, D_MODEL), BF16), SDS((s_len, D_MODEL), BF16), SDS((s_len, D_PLE), BF16)),
        compiler_params=_params(1, V7X_VMEM_LIMIT),
    )(h2, p, target, g_ple, w_gate, w_proj)


def _inproj_bwd(dqs, dkn, dva, dqb, dkb, dvb, qa_raw, ka_raw, x, dh1, g_pre, w_in, cos, sin, gq, gk, tm):
    s_len = x.shape[0]

    def body(dqs_ref, dkn_ref, dva_ref, dqb_ref, dkb_ref, dvb_ref, qa_ref, ka_ref, x_ref, dh1_ref, g_ref, w_ref,
             cos_ref, sin_ref, gq_ref, gk_ref, dx_ref, dproj_ref, dg_ref, dgq_ref, dgk_ref):
        first = pl.program_id(0) == 0
        dqa, dgq = _head_norm_rope_bwd(qa_ref[...], gq_ref[...], cos_ref[...], sin_ref[...], dqs_ref[...])
        dka, dgk = _head_norm_rope_bwd(ka_ref[...], gk_ref[...], cos_ref[:, :D_KV_A], sin_ref[:, :D_KV_A], dkn_ref[...])
        _acc(dgq_ref, first, dgq)
        _acc(dgk_ref, first, dgk)
        dproj_ref[:, OFF_QA:OFF_KA] = dqa.astype(BF16)
        dproj_ref[:, OFF_KA:OFF_VA] = dka.astype(BF16)
        dproj_ref[:, OFF_VA:OFF_QB] = dva_ref[...].astype(BF16)
        dproj_ref[:, OFF_QB:OFF_KB] = dqb_ref[...].astype(BF16)
        dproj_ref[:, OFF_KB:OFF_VB] = dkb_ref[...].astype(BF16)
        dproj_ref[:, OFF_VB:D_IN] = dvb_ref[...].astype(BF16)
        dxn = _dot_nt(dproj_ref[...], w_ref[...])
        dx, dg = _rms_bwd(x_ref[...], g_ref[...], dxn)
        _acc(dg_ref, first, dg)
        dx_ref[...] = dh1_ref[...] + dx

    half, kvw, full = _row_spec(tm, D_A), _row_spec(tm, D_KV_A), _row_spec(tm, D_MODEL)
    return pl.pallas_call(
        body, name="inproj_bwd", grid=(s_len // tm,),
        in_specs=[half, kvw, kvw, half, half, half, half, kvw, full, full, _full_spec((1, D_MODEL)),
                  _const_spec((D_MODEL, D_IN)), half, half, _full_spec((1, D_A)), _full_spec((1, D_KV_A))],
        out_specs=[full, _row_spec(tm, D_IN), _full_spec((1, D_MODEL)), _full_spec((1, D_A)), _full_spec((1, D_KV_A))],
        out_shape=(SDS((s_len, D_MODEL), F32), SDS((s_len, D_IN), BF16), SDS((1, D_MODEL), F32), SDS((1, D_A), F32),
                   SDS((1, D_KV_A), F32)),
        compiler_params=_params(1, V7X_VMEM_LIMIT),
    )(dqs, dkn, dva, dqb, dkb, dvb, qa_raw, ka_raw, x, dh1, g_pre, w_in, cos, sin, gq, gk)


def _weight_grad(a, b, name, tk1, tn, tm):
    s_len, k1 = a.shape
    n = b.shape[1]

    def body(a_ref, b_ref, o_ref):
        _acc(o_ref, pl.program_id(2) == 0, _dot_tn(a_ref[...], b_ref[...]))

    return pl.pallas_call(
        body, name=name, grid=(k1 // tk1, n // tn, s_len // tm),
        in_specs=[pl.BlockSpec((tm, tk1), lambda i, j, r: (r, i)), pl.BlockSpec((tm, tn), lambda i, j, r: (r, j))],
        out_specs=pl.BlockSpec((tk1, tn), lambda i, j, r: (i, j)),
        out_shape=SDS((k1, n), F32),
        compiler_params=_params(3, V7X_VMEM_LIMIT),
    )(a, b)


def _my_index():
    return 4 * lax.axis_index("x") + 2 * lax.axis_index("y") + lax.axis_index("c")


def _peer(k):
    f = k + 1
    x, y, c = lax.axis_index("x"), lax.axis_index("y"), lax.axis_index("c")
    return (x ^ ((f >> 2) & 1), y ^ ((f >> 1) & 1), c ^ (f & 1))


def _all_gather(blocks, name):
    n = len(blocks)

    def body(*refs):
        ins, outs = refs[:n], refs[n:2 * n]
        send_sems, recv_sems, local_sems = refs[2 * n:]
        me = _my_index()
        copies = []
        for a in range(n):
            own = pltpu.make_async_copy(ins[a], outs[a].at[me], local_sems.at[a])
            own.start()
            copies.append(own)
        remote = []
        for a in range(n):
            for k in range(N_DEV - 1):
                cp = pltpu.make_async_remote_copy(
                    src_ref=ins[a], dst_ref=outs[a].at[me], send_sem=send_sems.at[a, k], recv_sem=recv_sems.at[a, k],
                    device_id=_peer(k), device_id_type=pl.DeviceIdType.MESH)
                cp.start()
                remote.append(cp)
        for cp in copies:
            cp.wait()
        for cp in remote:
            cp.wait_send()
        for cp in remote:
            cp.wait_recv()

    any_spec = pl.BlockSpec(memory_space=pl.ANY)
    return pl.pallas_call(
        body, name=name,
        in_specs=[any_spec] * n, out_specs=[any_spec] * n,
        out_shape=[SDS((N_DEV,) + b.shape, b.dtype) for b in blocks],
        scratch_shapes=[pltpu.SemaphoreType.DMA((n, N_DEV - 1)), pltpu.SemaphoreType.DMA((n, N_DEV - 1)),
                        pltpu.SemaphoreType.DMA((n,))],
    )(*blocks)


def _all_to_all(parts, name):
    n = len(parts)

    def body(*refs):
        ins, outs = refs[:n], refs[n:2 * n]
        send_sems, recv_sems, local_sems = refs[2 * n:]
        me = _my_index()
        copies = []
        for a in range(n):
            own = pltpu.make_async_copy(ins[a].at[me], outs[a].at[me], local_sems.at[a])
            own.start()
            copies.append(own)
        remote = []
        for a in range(n):
            for k in range(N_DEV - 1):
                px, py, pc = _peer(k)
                cp = pltpu.make_async_remote_copy(
                    src_ref=ins[a].at[4 * px + 2 * py + pc], dst_ref=outs[a].at[me],
                    send_sem=send_sems.at[a, k], recv_sem=recv_sems.at[a, k],
                    device_id=(px, py, pc), device_id_type=pl.DeviceIdType.MESH)
                cp.start()
                remote.append(cp)
        for cp in copies:
            cp.wait()
        for cp in remote:
            cp.wait_send()
        for cp in remote:
            cp.wait_recv()

    any_spec = pl.BlockSpec(memory_space=pl.ANY)
    return pl.pallas_call(
        body, name=name,
        in_specs=[any_spec] * n, out_specs=[any_spec] * n,
        out_shape=[SDS(p.shape, p.dtype) for p in parts],
        scratch_shapes=[pltpu.SemaphoreType.DMA((n, N_DEV - 1)), pltpu.SemaphoreType.DMA((n, N_DEV - 1)),
                        pltpu.SemaphoreType.DMA((n,))],
    )(*parts)


def _sum_adamw(parts, w, m, v, name, tr):
    rows, cols = w.shape
    c1 = 1.0 / (1.0 - ADAM_B1 ** ADAM_STEP)
    c2 = 1.0 / (1.0 - ADAM_B2 ** ADAM_STEP)

    def body(p_ref, w_ref, m_ref, v_ref, g_ref, d_ref, nm_ref, nv_ref):
        g = p_ref[0]
        for j in range(1, N_DEV):
            g = g + p_ref[j]
        g_ref[...] = g
        nm = ADAM_B1 * m_ref[...] + (1.0 - ADAM_B1) * g
        nv = ADAM_B2 * v_ref[...] + (1.0 - ADAM_B2) * (g * g)
        nm_ref[...] = nm
        nv_ref[...] = nv
        d_ref[...] = -ADAM_LR * ((nm * c1) / (jnp.sqrt(nv * c2) + ADAM_EPS) + ADAM_WD * w_ref[...])

    blk = pl.BlockSpec((tr, cols), lambda i: (i, 0))
    return pl.pallas_call(
        body, name=name, grid=(rows // tr,),
        in_specs=[pl.BlockSpec((N_DEV, tr, cols), lambda i: (0, i, 0)), blk, blk, blk],
        out_specs=[blk] * 4, out_shape=[SDS((rows, cols), F32)] * 4,
        compiler_params=_params(1, V7X_VMEM_LIMIT),
    )(parts, w, m, v)


_SMALL = (("g_attn_pre", 1024), ("g_q", 64), ("g_k", 64), ("g_out_a", 512), ("g_out_b", 512), ("g_attn_post", 1024),
          ("rel_bias", 256), ("g_mlp_pre", 1024), ("g_mlp_post", 1024), ("g_ple", 1024))
_SLAB_ROWS = 56


def _pack_small(vals):
    rows = []
    for (name, size) in _SMALL:
        flat = vals[name].reshape(-1).astype(F32)
        padded = -(-size // 128) * 128
        rows.append(jnp.pad(flat, (0, padded - size)).reshape(padded // 128, 128))
    slab = jnp.concatenate(rows, axis=0)
    return jnp.pad(slab, ((0, _SLAB_ROWS - slab.shape[0]), (0, 0)))


def _unpack_small(slab, shapes):
    out, row = {}, 0
    for (name, size) in _SMALL:
        nrow = -(-size // 128)
        out[name] = slab[row:row + nrow].reshape(-1)[:size].reshape(shapes[name])
        row += nrow
    return out


def _rope_tables(s_len):
    rows = s_len // GRID_W
    row = jnp.broadcast_to(jnp.arange(rows)[:, None], (rows, GRID_W)).reshape(-1).astype(F32)
    col = jnp.broadcast_to(jnp.arange(GRID_W)[None, :], (rows, GRID_W)).reshape(-1).astype(F32)
    n_axis = ROPE_HALF // 2
    inv_freq = ROPE_THETA ** (-jnp.arange(n_axis, dtype=F32) / n_axis)
    ang = jnp.concatenate([row[:, None] * inv_freq, col[:, None] * inv_freq], axis=-1)
    cos, sin = jnp.cos(ang), jnp.sin(ang)
    cos = jnp.tile(jnp.concatenate([cos, cos], axis=-1), (1, N_HEADS_A))
    sin = jnp.tile(jnp.concatenate([-sin, sin], axis=-1), (1, N_HEADS_A))
    return cos, sin


def _local_step(x, p, target, w_in, w_out, w_ff1, w_ff2, w_gate, w_proj, small):
    s_len = x.shape[0]
    tm = min(256, s_len)
    tq = min(128, s_len)
    tk = min(512, s_len)
    cos, sin = _rope_tables(s_len)
    gq = jnp.tile(small["g_q"], (1, N_HEADS_A))
    gk = jnp.tile(small["g_k"], (1, N_KV_A))

    qa_raw, ka_raw, qs, kn, va, qb, kb, vb, xn1 = _inproj_fwd(x, small["g_attn_pre"], w_in, cos, sin, gq, gk, tm)

    def kv_major(a):
        return a.reshape(s_len, N_KV_A, HEAD_DIM).transpose(1, 0, 2)

    def kv_chunks_t(a):
        return a.reshape(s_len // tk, tk, N_KV_A, HEAD_DIM).transpose(2, 0, 3, 1)

    k_maj, v_maj, kt, vt = kv_major(kn), kv_major(va), kv_chunks_t(kn), kv_chunks_t(va)
    ya, lse_a = _attn_fwd(qs, kt, v_maj, tq, tk)

    ob, lb, tiles = [], [], []
    for (_, dil) in DILATED_PATTERNS:
        t = min(256, s_len // dil)
        bias, bias_t = _bias_tiles(small["rel_bias"], dil, t)
        tiles.append((t, bias, bias_t, _band_buckets(dil, t)))
        o, l = _band_fwd(qb, kb, vb, bias, dil, t)
        ob.append(o)
        lb.append(l)

    h1, yo, yb, lse_b, ycat = _attn_out_fwd(ya, ob, lb, x, small["g_out_a"], small["g_out_b"], w_out,
                                            small["g_attn_post"], tm)
    h2, fo, xn2 = _mlp_fwd(h1, small["g_mlp_pre"], w_ff1, w_ff2, small["g_mlp_post"], tm)
    dh2, loss_part, dg_ple, xn3, dgl, dpp, pb = _ple_fwd_bwd(h2, p, target, small["g_ple"], w_gate, w_proj, tm)
    dh1, dfo, du, f, dg_mlp_post, dg_mlp_pre = _mlp_bwd(h1, dh2, fo, xn2, small["g_mlp_pre"], w_ff1, w_ff2,
                                                          small["g_mlp_post"], tm)
    dyo, dya, dyb, delta_a, delta_b, dg_attn_post, dg_out_a, dg_out_b = _attn_out_bwd(
        dh1, yo, ya, yb, small["g_out_a"], small["g_out_b"], w_out, small["g_attn_post"], tm)

    dqs, dk_maj, dv_maj = _attn_bwd(qs, dya, lse_a, delta_a, kt, k_maj, vt, tq, tk)
    dkn = dk_maj.transpose(1, 0, 2).reshape(s_len, D_KV_A)
    dva = dv_maj.transpose(1, 0, 2).reshape(s_len, D_KV_A)

    dqb = dkb = dvb = jnp.zeros((s_len, D_B), F32)
    dbias = []
    for (_, dil), (t, bias, bias_t, _) in zip(DILATED_PATTERNS, tiles):
        dqb, db = _band_dq(qb, dyb, lse_b, delta_b, kb, vb, bias, dqb, dil, t)
        dkb, dvb = _band_dkv(kb, vb, qb, dyb, lse_b, delta_b, bias_t, dkb, dvb, dil, t)
        dbias.append(db)
    d_rel = _dbias_reduce(dbias, [jnp.asarray(tl[3]) for tl in tiles])[:, :N_HEADS_B]

    dx, dproj, dg_attn_pre, dgq_lanes, dgk_lanes = _inproj_bwd(
        dqs, dkn, dva, dqb, dkb, dvb, qa_raw, ka_raw, x, dh1, small["g_attn_pre"], w_in, cos, sin, gq, gk, tm)

    tg = min(512, s_len)
    grads = {
        "w_in": _weight_grad(xn1, dproj, "grad_w_in", D_MODEL, 768, tg),
        "w_out": _weight_grad(ycat, dyo, "grad_w_out", D_MODEL, D_MODEL, tg),
        "w_ff1": _weight_grad(xn2, du, "grad_w_ff1", D_MODEL, 1024, tg),
        "w_ff2": _weight_grad(f, dfo, "grad_w_ff2", 1024, D_MODEL, tg),
        "w_ple_gate": _weight_grad(xn3, dgl, "grad_w_ple_gate", D_MODEL, D_MODEL, tg),
        "w_ple_proj": _weight_grad(pb, dpp, "grad_w_ple_proj", D_PLE, D_MODEL, tg),
    }
    small_grads = {
        "g_attn_pre": dg_attn_pre, "g_q": dgq_lanes.reshape(N_HEADS_A, HEAD_DIM).sum(0, keepdims=True),
        "g_k": dgk_lanes.reshape(N_KV_A, HEAD_DIM).sum(0, keepdims=True), "g_out_a": dg_out_a, "g_out_b": dg_out_b,
        "g_attn_post": dg_attn_post, "rel_bias": d_rel, "g_mlp_pre": dg_mlp_pre, "g_mlp_post": dg_mlp_post,
        "g_ple": dg_ple,
    }
    return loss_part[0, 0], dx, grads, small_grads


_BIG = (("w_in", 1), ("w_out", 0), ("w_ff1", 1), ("w_ff2", 0), ("w_ple_gate", 0), ("w_ple_proj", 1))


def _assemble(gathered, axis):
    if axis == 0:
        return gathered.reshape(-1, gathered.shape[2])
    return gathered.transpose(1, 0, 2).reshape(gathered.shape[1], -1)


def _cut(full, axis):
    if axis == 0:
        return full.reshape(N_DEV, full.shape[0] // N_DEV, full.shape[1])
    return full.reshape(full.shape[0], N_DEV, full.shape[1] // N_DEV).transpose(1, 0, 2)


def kernel(x, p, w_in, g_attn_pre, g_q, g_k, g_out_a, g_out_b, w_out, g_attn_post, rel_bias, g_mlp_pre, w_ff1, w_ff2, g_mlp_post, g_ple, w_ple_gate, w_ple_proj, loss_target, m_w_in, m_g_attn_pre, m_g_q, m_g_k, m_g_out_a, m_g_out_b, m_w_out, m_g_attn_post, m_rel_bias, m_g_mlp_pre, m_w_ff1, m_w_ff2, m_g_mlp_post, m_g_ple, m_w_ple_gate, m_w_ple_proj, v_w_in, v_g_attn_pre, v_g_q, v_g_k, v_g_out_a, v_g_out_b, v_w_out, v_g_attn_post, v_rel_bias, v_g_mlp_pre, v_w_ff1, v_w_ff2, v_g_mlp_post, v_g_ple, v_w_ple_gate, v_w_ple_proj):
    given = dict(locals())
    small_names = [n for n, _ in _SMALL]
    small = {n: given[n] for n in small_names}
    shards = {n: given[n][0] for n, _ in _BIG}

    gathered = _all_gather([shards[n].astype(BF16) for n, _ in _BIG], "gather_weights")
    full = {n: _assemble(g, axis) for (n, axis), g in zip(_BIG, gathered)}

    loss_part, dx, grads, small_grads = _local_step(
        x[0], p[0, 0], loss_target[0], full["w_in"], full["w_out"], full["w_ff1"], full["w_ff2"], full["w_ple_gate"],
        full["w_ple_proj"], small)

    parts = [_cut(grads[n], axis) for n, axis in _BIG]
    slab = _pack_small(small_grads)
    slab_parts = jnp.broadcast_to(slab[None], (N_DEV,) + slab.shape)
    received = _all_to_all(parts + [slab_parts], "exchange_grads")

    out_g, out_d, out_m, out_v = {}, {}, {}, {}
    for (n, _), rec in zip(_BIG, received[:-1]):
        rows = shards[n].shape[0]
        g, d, nm, nv = _sum_adamw(rec, shards[n], given["m_" + n][0], given["v_" + n][0], "adamw_" + n, min(rows, 128))
        out_g[n], out_d[n], out_m[n], out_v[n] = g[None], d[None], nm[None], nv[None]
    g, d, nm, nv = _sum_adamw(received[-1], _pack_small(small), _pack_small({n: given["m_" + n] for n in small_names}),
                              _pack_small({n: given["v_" + n] for n in small_names}), "adamw_small", _SLAB_ROWS)
    shapes = {n: given[n].shape for n in small_names}
    for dst, slab_out in ((out_g, g), (out_d, d), (out_m, nm), (out_v, nv)):
        dst.update(_unpack_small(slab_out, shapes))

    loss = lax.psum(loss_part * (0.5 / D_MODEL), MESH_AXES)
    order = ["w_in", "g_attn_pre", "g_q", "g_k", "g_out_a", "g_out_b", "w_out", "g_attn_post", "rel_bias", "g_mlp_pre",
             "w_ff1", "w_ff2", "g_mlp_post", "g_ple", "w_ple_gate", "w_ple_proj"]
    return (loss, dx[None], *[out_g[n] for n in order], *[out_d[n] for n in order], *[out_m[n] for n in order],
            *[out_v[n] for n in order])
```

```python
import functools
import math

import jax
import jax.numpy as jnp
import numpy as np
from jax import lax
from jax.experimental import pallas as pl
from jax.experimental.pallas import tpu as pltpu

F32 = jnp.float32
BF16 = jnp.bfloat16
SDS = jax.ShapeDtypeStruct

D_MODEL = 1024
HEAD_DIM = 64
N_HEADS_A = 8
N_KV_A = 2
GROUP_A = N_HEADS_A // N_KV_A
N_HEADS_B = 8
D_A = N_HEADS_A * HEAD_DIM
D_KV_A = N_KV_A * HEAD_DIM
D_B = N_HEADS_B * HEAD_DIM
D_IN = D_A + 2 * D_KV_A + 3 * D_B
D_FF = 4 * D_MODEL
D_PLE = 256
GRID_W = 64
ROPE_THETA = 10000.0
ROPE_HALF = HEAD_DIM // 2
DILATED_PATTERNS = ((128, 1), (512, 4), (2048, 16))
BAND_HALF = 64
N_BUCKETS = 32
MAX_DISTANCE = 1024
EPS = 1e-6
NEG_BIG = -1e30
SCORE_SCALE = HEAD_DIM ** -0.5

ADAM_LR = 0.001
ADAM_B1 = 0.9
ADAM_B2 = 0.999
ADAM_EPS = 1e-08
ADAM_WD = 0.01
ADAM_STEP = 10

N_DEV = 8
MESH_AXES = ("x", "y", "c")
V7X_VMEM_LIMIT = 56 * 1024 * 1024

OFF_QA, OFF_KA, OFF_VA, OFF_QB, OFF_KB, OFF_VB = 0, 512, 640, 768, 1280, 1792


def _params(n_axes, vmem=None):
    return pltpu.CompilerParams(dimension_semantics=("arbitrary",) * n_axes, vmem_limit_bytes=vmem)


def _dot(a, b):
    return jnp.dot(a, b, preferred_element_type=F32)


def _dot_nt(a, b):
    return lax.dot_general(a, b, (((1,), (1,)), ((), ())), preferred_element_type=F32)


def _dot_tn(a, b):
    return lax.dot_general(a, b, (((0,), (0,)), ((), ())), preferred_element_type=F32)


def _rms_fwd(x, g):
    r = lax.rsqrt(jnp.mean(x * x, axis=-1, keepdims=True) + EPS)
    return x * r * g, r


def _rms_bwd(x, g, dy):
    r = lax.rsqrt(jnp.mean(x * x, axis=-1, keepdims=True) + EPS)
    xh = x * r
    dxh = dy * g
    dx = r * (dxh - xh * jnp.mean(dxh * xh, axis=-1, keepdims=True))
    return dx, jnp.sum(dy * xh, axis=0, keepdims=True)


def _head_sum(v):
    head = lax.broadcasted_iota(jnp.int32, v.shape, 1) >> 6
    out = jnp.zeros_like(v)
    for h in range(v.shape[1] // HEAD_DIM):
        msk = head == h
        s = jnp.sum(jnp.where(msk, v, 0.0), axis=-1, keepdims=True)
        out = jnp.where(msk, s, out)
    return out


def _swap_halves(v):
    w = v.shape[1]
    lane = lax.broadcasted_iota(jnp.int32, v.shape, 1)
    first_half = (lane & (HEAD_DIM - 1)) < ROPE_HALF
    return jnp.where(first_half, pltpu.roll(v, w - ROPE_HALF, 1), pltpu.roll(v, ROPE_HALF, 1))


def _head_norm_rope(v, g, cos, sin_signed):
    r = lax.rsqrt(_head_sum(v * v) * (1.0 / HEAD_DIM) + EPS)
    y = v * r * g
    return y * cos + _swap_halves(y) * sin_signed


def _head_norm_rope_bwd(v, g, cos, sin_signed, dout):
    dy = dout * cos - _swap_halves(dout) * sin_signed
    r = lax.rsqrt(_head_sum(v * v) * (1.0 / HEAD_DIM) + EPS)
    xh = v * r
    dxh = dy * g
    dv = r * (dxh - xh * (_head_sum(dxh * xh) * (1.0 / HEAD_DIM)))
    return dv, jnp.sum(dy * xh, axis=0, keepdims=True)


def _row_spec(tm, n):
    return pl.BlockSpec((tm, n), lambda i: (i, 0))


def _full_spec(shape):
    nd = len(shape)
    return pl.BlockSpec(shape, lambda *_: (0,) * nd)


def _const_spec(shape):
    nd = len(shape)
    return pl.BlockSpec(shape, lambda *_: (0,) * nd, pipeline_mode=pl.Buffered(1))


def _acc(ref, first, val):
    @pl.when(first)
    def _():
        ref[...] = val

    @pl.when(jnp.logical_not(first))
    def _():
        ref[...] += val


def _inproj_fwd(x, g_pre, w_in, cos, sin, gq, gk, tm):
    s_len = x.shape[0]

    def body(x_ref, g_ref, w_ref, cos_ref, sin_ref, gq_ref, gk_ref,
             qa_raw, ka_raw, qs, kn, va, qb, kb, vb, xn_out):
        xn, _ = _rms_fwd(x_ref[...], g_ref[...])
        xn = xn.astype(BF16)
        xn_out[...] = xn
        qa = _dot(xn, w_ref[:, OFF_QA:OFF_KA])
        qa_raw[...] = qa
        qs[...] = (_head_norm_rope(qa, gq_ref[...], cos_ref[...], sin_ref[...]) * SCORE_SCALE).astype(BF16)
        ka = _dot(xn, w_ref[:, OFF_KA:OFF_VA])
        ka_raw[...] = ka
        kn[...] = _head_norm_rope(ka, gk_ref[...], cos_ref[:, :D_KV_A], sin_ref[:, :D_KV_A]).astype(BF16)
        va[...] = _dot(xn, w_ref[:, OFF_VA:OFF_QB]).astype(BF16)
        qb[...] = (_dot(xn, w_ref[:, OFF_QB:OFF_KB]) * SCORE_SCALE).astype(BF16)
        kb[...] = _dot(xn, w_ref[:, OFF_KB:OFF_VB]).astype(BF16)
        vb[...] = _dot(xn, w_ref[:, OFF_VB:D_IN]).astype(BF16)

    return pl.pallas_call(
        body, name="inproj_fwd", grid=(s_len // tm,),
        in_specs=[_row_spec(tm, D_MODEL), _full_spec((1, D_MODEL)), _const_spec((D_MODEL, D_IN)),
                  _row_spec(tm, D_A), _row_spec(tm, D_A), _full_spec((1, D_A)), _full_spec((1, D_KV_A))],
        out_specs=[_row_spec(tm, D_A), _row_spec(tm, D_KV_A), _row_spec(tm, D_A), _row_spec(tm, D_KV_A),
                   _row_spec(tm, D_KV_A), _row_spec(tm, D_B), _row_spec(tm, D_B), _row_spec(tm, D_B),
                   _row_spec(tm, D_MODEL)],
        out_shape=(SDS((s_len, D_A), F32), SDS((s_len, D_KV_A), F32), SDS((s_len, D_A), BF16),
                   SDS((s_len, D_KV_A), BF16), SDS((s_len, D_KV_A), BF16), SDS((s_len, D_B), BF16),
                   SDS((s_len, D_B), BF16), SDS((s_len, D_B), BF16), SDS((s_len, D_MODEL), BF16)),
        compiler_params=_params(1, V7X_VMEM_LIMIT),
    )(x, g_pre, w_in, cos, sin, gq, gk)


def _stack_heads(ref, tq):
    return jnp.concatenate([ref[:, HEAD_DIM * g:HEAD_DIM * (g + 1)] for g in range(GROUP_A)], axis=0)


def _stack_cols(ref, tq):
    return jnp.concatenate([ref[:, HEAD_DIM * g:HEAD_DIM * g + 1] for g in range(GROUP_A)], axis=0)


def _attn_fwd(qs, k, vt, tq, tk):
    s_len = qs.shape[0]
    nk = s_len // tk
    assert nk % 2 == 0
    gw = GROUP_A * HEAD_DIM
    rows = GROUP_A * tq

    def body(q_ref, k_ref, vt_ref, o_ref, lse_ref, s_buf):
        qt = _stack_heads(q_ref, tq).T

        def scores(j, slot):
            kj = k_ref[pl.ds(pl.multiple_of(j * tk, tk), tk), :]
            s_buf[slot] = _dot(kj, qt)

        def consume(j, slot, carry):
            m, l, acc = carry
            st = s_buf[slot]
            m_new = jnp.maximum(m, jnp.max(st, axis=0, keepdims=True))
            pt = jnp.exp(st - m_new)
            alpha = jnp.exp(m - m_new)
            l = alpha * l + jnp.sum(pt, axis=0, keepdims=True)
            acc = alpha * acc + _dot(vt_ref[j], pt.astype(BF16))
            return m_new, l, acc

        scores(0, 0)

        def step(jj, carry):
            j = 2 * jj
            scores(j + 1, 1)
            carry = consume(j, 0, carry)
            scores(jnp.minimum(j + 2, nk - 1), 0)
            return consume(j + 1, 1, carry)

        m, l, acc = lax.fori_loop(0, nk // 2, step, (jnp.full((1, rows), NEG_BIG, F32), jnp.zeros((1, rows), F32),
                                                     jnp.zeros((HEAD_DIM, rows), F32)))
        o = (acc / l).T
        lse = jnp.broadcast_to(m + jnp.log(l), (HEAD_DIM, rows)).T
        for g in range(GROUP_A):
            o_ref[:, HEAD_DIM * g:HEAD_DIM * (g + 1)] = o[g * tq:(g + 1) * tq]
            lse_ref[:, HEAD_DIM * g:HEAD_DIM * (g + 1)] = lse[g * tq:(g + 1) * tq]

    tile = pl.BlockSpec((tq, gw), lambda kv, i: (i, kv))
    return pl.pallas_call(
        body, name="attn_fwd", grid=(N_KV_A, s_len // tq),
        in_specs=[tile, pl.BlockSpec((None, s_len, HEAD_DIM), lambda kv, i: (kv, 0, 0)),
                  pl.BlockSpec((None, nk, HEAD_DIM, tk), lambda kv, i: (kv, 0, 0, 0))],
        out_specs=[tile, tile],
        out_shape=(SDS((s_len, D_A), F32), SDS((s_len, D_A), F32)),
        scratch_shapes=[pltpu.VMEM((2, tk, rows), F32)],
        compiler_params=_params(2, V7X_VMEM_LIMIT),
    )(qs, k, vt)


def _attn_bwd(qs, do, lse, delta, kt, k, vt, tq, tk):
    s_len = qs.shape[0]
    nk = s_len // tk
    assert nk % 2 == 0
    nq = s_len // tq
    gw = GROUP_A * HEAD_DIM
    rows = GROUP_A * tq

    def body(q_ref, do_ref, lse_ref, delta_ref, kt_ref, k_ref, vt_ref, dq_ref, dk_hbm, dv_hbm,
             dk_acc, dv_acc, s_buf, dp_buf):
        kv = pl.program_id(0)
        i = pl.program_id(1)

        @pl.when(i == 0)
        def _():
            dk_acc[...] = jnp.zeros_like(dk_acc)
            dv_acc[...] = jnp.zeros_like(dv_acc)

        q = _stack_heads(q_ref, tq)
        dout = _stack_heads(do_ref, tq)
        qt = q.T
        doutt = dout.T
        row_lse = _stack_cols(lse_ref, tq)
        row_delta = _stack_cols(delta_ref, tq)

        def scores(j, slot):
            s_buf[slot] = _dot(q, kt_ref[j])
            dp_buf[slot] = _dot(dout, vt_ref[j])

        def consume(j, slot, dq):
            p = jnp.exp(s_buf[slot] - row_lse)
            ds = (p * (dp_buf[slot] - row_delta)).astype(BF16)
            dv_acc[j] += _dot(doutt, p.astype(BF16))
            dk_acc[j] += _dot(qt, ds)
            return dq + _dot(ds, k_ref[pl.ds(pl.multiple_of(j * tk, tk), tk), :])

        scores(0, 0)

        def step(jj, dq):
            j = 2 * jj
            scores(j + 1, 1)
            dq = consume(j, 0, dq)
            scores(jnp.minimum(j + 2, nk - 1), 0)
            return consume(j + 1, 1, dq)

        dq = lax.fori_loop(0, nk // 2, step, jnp.zeros((rows, HEAD_DIM), F32)) * SCORE_SCALE
        for g in range(GROUP_A):
            dq_ref[:, HEAD_DIM * g:HEAD_DIM * (g + 1)] = dq[g * tq:(g + 1) * tq]

        @pl.when(i == nq - 1)
        def _():
            pltpu.sync_copy(dk_acc, dk_hbm.at[kv])
            pltpu.sync_copy(dv_acc, dv_hbm.at[kv])

    tile = pl.BlockSpec((tq, gw), lambda kv, i: (i, kv))
    chunks = pl.BlockSpec((None, nk, HEAD_DIM, tk), lambda kv, i: (kv, 0, 0, 0))
    grad_t = SDS((N_KV_A, nk, HEAD_DIM, tk), F32)
    return pl.pallas_call(
        body, name="attn_bwd", grid=(N_KV_A, nq),
        in_specs=[tile, tile, tile, tile, chunks,
                  pl.BlockSpec((None, s_len, HEAD_DIM), lambda kv, i: (kv, 0, 0)), chunks],
        out_specs=[tile, pl.BlockSpec(memory_space=pl.ANY), pl.BlockSpec(memory_space=pl.ANY)],
        out_shape=(SDS((s_len, D_A), F32), grad_t, grad_t),
        scratch_shapes=[pltpu.VMEM((nk, HEAD_DIM, tk), F32), pltpu.VMEM((nk, HEAD_DIM, tk), F32),
                        pltpu.VMEM((2, rows, tk), F32), pltpu.VMEM((2, rows, tk), F32)],
        compiler_params=_params(2, V7X_VMEM_LIMIT),
    )(qs, do, lse, delta, kt, k, vt)


def _band_specs(length, t):
    hb = t // BAND_HALF
    last = length // BAND_HALF - 1
    main = pl.BlockSpec((t, D_B), lambda r, i: (i, r))
    prev = pl.BlockSpec((BAND_HALF, D_B), lambda r, i: (jnp.maximum(i * hb - 1, 0), r))
    nxt = pl.BlockSpec((BAND_HALF, D_B), lambda r, i: (jnp.minimum((i + 1) * hb, last), r))
    return main, [prev, main, nxt]


def _window(refs):
    return jnp.concatenate([r[...] for r in refs], axis=0)


def _head(v, h):
    return v[:, HEAD_DIM * h:HEAD_DIM * (h + 1)]


def _band_fwd(q, k, v, bias, dil, t):
    s_len = q.shape[0]
    length = s_len // dil
    w = t + 2 * BAND_HALF
    view = lambda a: a.reshape(length, dil * D_B)
    main, win = _band_specs(length, t)

    def body(q_ref, k0, k1, k2, v0, v1, v2, bias_ref, o_ref, lse_ref):
        i = pl.program_id(1)
        kw = _window((k0, k1, k2))
        vw = _window((v0, v1, v2))
        pos = i * t - BAND_HALF + lax.broadcasted_iota(jnp.int32, (1, w), 1)
        valid = (pos >= 0) & (pos < length)
        for h in range(N_HEADS_B):
            s = _dot_nt(q_ref[:, HEAD_DIM * h:HEAD_DIM * (h + 1)], _head(kw, h)) + bias_ref[h]
            s = jnp.where(valid, s, NEG_BIG)
            m = jnp.max(s, axis=-1, keepdims=True)
            e = jnp.exp(s - m)
            den = jnp.sum(e, axis=-1, keepdims=True)
            o_ref[:, HEAD_DIM * h:HEAD_DIM * (h + 1)] = _dot(e.astype(BF16), _head(vw, h)) / den
            lse_ref[:, HEAD_DIM * h:HEAD_DIM * (h + 1)] = jnp.broadcast_to(m + jnp.log(den), (t, HEAD_DIM))

    o, lse = pl.pallas_call(
        body, name=f"band_fwd_d{dil}", grid=(dil, length // t),
        in_specs=[main] + win + win + [_full_spec((N_HEADS_B, t, w))],
        out_specs=[main, main],
        out_shape=(SDS((length, dil * D_B), F32), SDS((length, dil * D_B), F32)),
        compiler_params=_params(2, V7X_VMEM_LIMIT),
    )(view(q), view(k), view(k), view(k), view(v), view(v), view(v), bias)
    return o.reshape(s_len, D_B), lse.reshape(s_len, D_B)


def _band_dq(q, do, lse, delta, k, v, bias, dq_in, dil, t):
    s_len = q.shape[0]
    length = s_len // dil
    w = t + 2 * BAND_HALF
    view = lambda a: a.reshape(length, dil * D_B)
    main, win = _band_specs(length, t)

    def body(q_ref, do_ref, lse_ref, delta_ref, k0, k1, k2, v0, v1, v2, bias_ref, acc_ref, dq_ref, dbias_ref):
        i = pl.program_id(1)
        first = (pl.program_id(0) == 0) & (i == 0)
        kw = _window((k0, k1, k2))
        vw = _window((v0, v1, v2))
        pos = i * t - BAND_HALF + lax.broadcasted_iota(jnp.int32, (1, w), 1)
        valid = (pos >= 0) & (pos < length)
        for h in range(N_HEADS_B):
            cols = slice(HEAD_DIM * h, HEAD_DIM * (h + 1))
            s = _dot_nt(q_ref[:, cols], _head(kw, h)) + bias_ref[h]
            s = jnp.where(valid, s, NEG_BIG)
            p = jnp.exp(s - lse_ref[:, HEAD_DIM * h:HEAD_DIM * h + 1])
            dp = _dot_nt(do_ref[:, cols], _head(vw, h))
            ds = p * (dp - delta_ref[:, HEAD_DIM * h:HEAD_DIM * h + 1])
            dq_ref[:, cols] = acc_ref[:, cols] + _dot(ds.astype(BF16), _head(kw, h)) * SCORE_SCALE

            @pl.when(first)
            def _():
                dbias_ref[h] = ds

            @pl.when(jnp.logical_not(first))
            def _():
                dbias_ref[h] += ds

    dq, dbias = pl.pallas_call(
        body, name=f"band_dq_d{dil}", grid=(dil, length // t),
        in_specs=[main, main, main, main] + win + win + [_full_spec((N_HEADS_B, t, w)), main],
        out_specs=[main, _full_spec((N_HEADS_B, t, w))],
        out_shape=(SDS((length, dil * D_B), F32), SDS((N_HEADS_B, t, w), F32)),
        compiler_params=_params(2, V7X_VMEM_LIMIT),
    )(view(q), view(do), view(lse), view(delta), view(k), view(k), view(k), view(v), view(v), view(v), bias,
      view(dq_in))
    return dq.reshape(s_len, D_B), dbias


def _band_dkv(k, v, q, do, lse, delta, bias_t, dk_in, dv_in, dil, t):
    s_len = q.shape[0]
    length = s_len // dil
    w = t + 2 * BAND_HALF
    view = lambda a: a.reshape(length, dil * D_B)
    main, win = _band_specs(length, t)

    def body(k_ref, v_ref, q0, q1, q2, d0, d1, d2, l0, l1, l2, e0, e1, e2, bias_ref, dk_acc, dv_acc, dk_ref, dv_ref):
        i = pl.program_id(1)
        qw = _window((q0, q1, q2))
        dow = _window((d0, d1, d2))
        lsew = _window((l0, l1, l2))
        deltaw = _window((e0, e1, e2))
        pos = i * t - BAND_HALF + lax.broadcasted_iota(jnp.int32, (w, 1), 0)
        valid = (pos >= 0) & (pos < length)
        for h in range(N_HEADS_B):
            cols = slice(HEAD_DIM * h, HEAD_DIM * (h + 1))
            qh = _head(qw, h)
            doh = _head(dow, h)
            s = _dot_nt(qh, k_ref[:, cols]) + bias_ref[h]
            s = jnp.where(valid, s, NEG_BIG)
            p = jnp.exp(s - lsew[:, HEAD_DIM * h:HEAD_DIM * h + 1])
            dp = _dot_nt(doh, v_ref[:, cols])
            ds = p * (dp - deltaw[:, HEAD_DIM * h:HEAD_DIM * h + 1])
            dv_ref[:, cols] = dv_acc[:, cols] + _dot_tn(p.astype(BF16), doh)
            dk_ref[:, cols] = dk_acc[:, cols] + _dot_tn(ds.astype(BF16), qh)

    dk, dv = pl.pallas_call(
        body, name=f"band_dkv_d{dil}", grid=(dil, length // t),
        in_specs=[main, main] + win + win + win + win + [_full_spec((N_HEADS_B, w, t)), main, main],
        out_specs=[main, main],
        out_shape=(SDS((length, dil * D_B), F32), SDS((length, dil * D_B), F32)),
        compiler_params=_params(2, V7X_VMEM_LIMIT),
    )(view(k), view(v), view(q), view(q), view(q), view(do), view(do), view(do), view(lse), view(lse), view(lse),
      view(delta), view(delta), view(delta), bias_t, view(dk_in), view(dv_in))
    return dk.reshape(s_len, D_B), dv.reshape(s_len, D_B)


def _t5_bucket_np(rel):
    nb = N_BUCKETS // 2
    max_exact = nb // 2
    side = np.where(rel > 0, nb, 0)
    n = np.abs(rel)
    ratio = np.maximum(n, max_exact).astype(np.float32) / np.float32(max_exact)
    large = max_exact + (np.log(ratio) / np.float32(math.log(MAX_DISTANCE / max_exact))
                         * np.float32(nb - max_exact)).astype(np.int32)
    large = np.minimum(large, nb - 1)
    return (side + np.where(n < max_exact, n, large)).astype(np.int32)


def _band_buckets(dil, t):
    rel = np.arange(t + 2 * BAND_HALF)[None, :] - BAND_HALF - np.arange(t)[:, None]
    bucket = _t5_bucket_np(np.clip(rel, -BAND_HALF, BAND_HALF) * dil)
    return np.where(np.abs(rel) <= BAND_HALF, bucket, -1).astype(np.int32)


def _toeplitz(vals, rows, cols):
    heads = vals.shape[0]
    period = rows + cols
    vec = jnp.concatenate([vals[:, rows - 1:], jnp.zeros((heads, 1), vals.dtype), vals[:, :rows - 1]], axis=1)
    flat = jnp.broadcast_to(vec[:, None, :], (heads, rows, period)).reshape(heads, rows * period)
    return flat[:, :rows * (period - 1)].reshape(heads, rows, period - 1)[:, :, :cols]


def _bias_tiles(rel_bias, dil, t):
    w = t + 2 * BAND_HALF
    rel = np.arange(-BAND_HALF, BAND_HALF + 1)
    bucket = _t5_bucket_np(rel * dil)
    runs, start = [], 0
    for i in range(1, len(bucket) + 1):
        if i == len(bucket) or bucket[i] != bucket[start]:
            b = int(bucket[start])
            runs.append(jnp.broadcast_to(rel_bias[b:b + 1], (i - start, N_HEADS_B)))
            start = i
    per_rel = jnp.concatenate(runs, axis=0).T

    def diagonals(lo, hi):
        left = jnp.full((N_HEADS_B, max(0, -BAND_HALF - lo)), NEG_BIG, F32)
        right = jnp.full((N_HEADS_B, max(0, hi - BAND_HALF)), NEG_BIG, F32)
        return jnp.concatenate([left, per_rel, right], axis=1)

    tile = _toeplitz(diagonals(-(t - 1) - BAND_HALF, w - 1 - BAND_HALF), t, w)
    twin = _toeplitz(diagonals(-(w - 1) + BAND_HALF, t - 1 + BAND_HALF), w, t)
    return tile, twin


def _dbias_reduce(dbias, buckets):
    n = len(dbias)

    def body(*refs):
        db_refs, bk_refs, o_ref = refs[:n], refs[n:2 * n], refs[2 * n]
        row = lax.broadcasted_iota(jnp.int32, (N_BUCKETS, 128), 0)
        lane = lax.broadcasted_iota(jnp.int32, (N_BUCKETS, 128), 1)

        def per_bucket(b, out):
            for pat in range(n):
                msk = bk_refs[pat][...] == b
                for h in range(N_HEADS_B):
                    tot = jnp.sum(jnp.where(msk, db_refs[pat][h], 0.0), axis=-1, keepdims=True)
                    tot = jnp.sum(tot, axis=0, keepdims=True)
                    out = out + jnp.where((row == b) & (lane == h), tot, 0.0)
            return out

        o_ref[...] = lax.fori_loop(0, N_BUCKETS, per_bucket, jnp.zeros((N_BUCKETS, 128), F32))

    return pl.pallas_call(
        body, name="dbias_reduce", out_shape=SDS((N_BUCKETS, 128), F32),
        compiler_params=pltpu.CompilerParams(vmem_limit_bytes=V7X_VMEM_LIMIT),
    )(*dbias, *buckets)


def _attn_out_fwd(ya, ob, lb, x, g_a, g_b, w_out, g_post, tm):
    s_len = ya.shape[0]

    def body(ya_ref, o0, o1, o2, l0, l1, l2, x_ref, ga_ref, gb_ref, w_ref, gp_ref,
             h1_ref, yo_ref, yb_ref, lse_ref, ycat_ref):
        m = jnp.maximum(jnp.maximum(l0[...], l1[...]), l2[...])
        w0, w1, w2 = jnp.exp(l0[...] - m), jnp.exp(l1[...] - m), jnp.exp(l2[...] - m)
        wsum = w0 + w1 + w2
        yb = (w0 * o0[...] + w1 * o1[...] + w2 * o2[...]) / wsum
        yb_ref[...] = yb
        lse_ref[...] = m + jnp.log(wsum)
        yan, _ = _rms_fwd(ya_ref[...], ga_ref[...])
        ybn, _ = _rms_fwd(yb, gb_ref[...])
        yan, ybn = yan.astype(BF16), ybn.astype(BF16)
        ycat_ref[:, :D_A] = yan
        ycat_ref[:, D_A:] = ybn
        yo = _dot(yan, w_ref[:D_A, :]) + _dot(ybn, w_ref[D_A:, :])
        yo_ref[...] = yo
        post, _ = _rms_fwd(yo, gp_ref[...])
        h1_ref[...] = x_ref[...] + post

    half, full = _row_spec(tm, D_A), _row_spec(tm, D_MODEL)
    return pl.pallas_call(
        body, name="attn_out_fwd", grid=(s_len // tm,),
        in_specs=[half] * 7 + [full, _full_spec((1, D_A)), _full_spec((1, D_B)), _full_spec((D_MODEL, D_MODEL)),
                               _full_spec((1, D_MODEL))],
        out_specs=[full, full, half, half, full],
        out_shape=(SDS((s_len, D_MODEL), F32), SDS((s_len, D_MODEL), F32), SDS((s_len, D_B), F32),
                   SDS((s_len, D_B), F32), SDS((s_len, D_MODEL), BF16)),
        compiler_params=_params(1, V7X_VMEM_LIMIT),
    )(ya, ob[0], ob[1], ob[2], lb[0], lb[1], lb[2], x, g_a, g_b, w_out, g_post)


def _attn_out_bwd(dh1, yo, ya, yb, g_a, g_b, w_out, g_post, tm):
    s_len = ya.shape[0]

    def body(dh1_ref, yo_ref, ya_ref, yb_ref, ga_ref, gb_ref, w_ref, gp_ref,
             dyo_ref, dya_ref, dyb_ref, dela_ref, delb_ref, dgp_ref, dga_ref, dgb_ref):
        first = pl.program_id(0) == 0
        dyo, dgp = _rms_bwd(yo_ref[...], gp_ref[...], dh1_ref[...])
        dyo = dyo.astype(BF16)
        dyo_ref[...] = dyo
        _acc(dgp_ref, first, dgp)
        dya_n = _dot_nt(dyo, w_ref[:D_A, :])
        dyb_n = _dot_nt(dyo, w_ref[D_A:, :])
        ya, yb = ya_ref[...], yb_ref[...]
        dya, dga = _rms_bwd(ya, ga_ref[...], dya_n)
        dyb, dgb = _rms_bwd(yb, gb_ref[...], dyb_n)
        _acc(dga_ref, first, dga)
        _acc(dgb_ref, first, dgb)
        dya_ref[...] = dya.astype(BF16)
        dyb_ref[...] = dyb.astype(BF16)
        dela_ref[...] = _head_sum(dya * ya)
        delb_ref[...] = _head_sum(dyb * yb)

    half, full = _row_spec(tm, D_A), _row_spec(tm, D_MODEL)
    return pl.pallas_call(
        body, name="attn_out_bwd", grid=(s_len // tm,),
        in_specs=[full, full, half, half, _full_spec((1, D_A)), _full_spec((1, D_B)),
                  _full_spec((D_MODEL, D_MODEL)), _full_spec((1, D_MODEL))],
        out_specs=[full, half, half, half, half, _full_spec((1, D_MODEL)), _full_spec((1, D_A)), _full_spec((1, D_B))],
        out_shape=(SDS((s_len, D_MODEL), BF16), SDS((s_len, D_A), BF16), SDS((s_len, D_B), BF16),
                   SDS((s_len, D_A), F32), SDS((s_len, D_B), F32), SDS((1, D_MODEL), F32), SDS((1, D_A), F32),
                   SDS((1, D_B), F32)),
        compiler_params=_params(1, V7X_VMEM_LIMIT),
    )(dh1, yo, ya, yb, g_a, g_b, w_out, g_post)


FF_CHUNK = 1024


def _mlp_fwd(h1, g_pre, w1, w2, g_post, tm):
    s_len = h1.shape[0]

    def body(h1_ref, gpre_ref, w1_ref, w2_ref, gpost_ref, h2_ref, fo_ref, xn_ref):
        h1v = h1_ref[...]
        xn, _ = _rms_fwd(h1v, gpre_ref[...])
        xn = xn.astype(BF16)
        xn_ref[...] = xn
        fo = jnp.zeros((tm, D_MODEL), F32)
        for c in range(D_FF // FF_CHUNK):
            cols = slice(c * FF_CHUNK, (c + 1) * FF_CHUNK)
            u = jnp.maximum(_dot(xn, w1_ref[:, cols]), 0.0)
            fo = fo + _dot((u * u).astype(BF16), w2_ref[cols, :])
        fo_ref[...] = fo
        post, _ = _rms_fwd(fo, gpost_ref[...])
        h2_ref[...] = h1v + post

    full = _row_spec(tm, D_MODEL)
    return pl.pallas_call(
        body, name="mlp_fwd", grid=(s_len // tm,),
        in_specs=[full, _full_spec((1, D_MODEL)), _const_spec((D_MODEL, D_FF)), _const_spec((D_FF, D_MODEL)),
                  _full_spec((1, D_MODEL))],
        out_specs=[full, full, full],
        out_shape=(SDS((s_len, D_MODEL), F32), SDS((s_len, D_MODEL), F32), SDS((s_len, D_MODEL), BF16)),
        compiler_params=_params(1, V7X_VMEM_LIMIT),
    )(h1, g_pre, w1, w2, g_post)


def _mlp_bwd(h1, dh2, fo, xn, g_pre, w1, w2, g_post, tm):
    s_len = h1.shape[0]

    def body(h1_ref, dh2_ref, fo_ref, xn_ref, gpre_ref, w1_ref, w2_ref, gpost_ref,
             dh1_ref, dfo_ref, du_ref, f_ref, dgpost_ref, dgpre_ref):
        first = pl.program_id(0) == 0
        dh2 = dh2_ref[...]
        dfo, dgpost = _rms_bwd(fo_ref[...], gpost_ref[...], dh2)
        dfo = dfo.astype(BF16)
        dfo_ref[...] = dfo
        _acc(dgpost_ref, first, dgpost)
        xn = xn_ref[...]
        dxn = jnp.zeros((tm, D_MODEL), F32)
        for c in range(D_FF // FF_CHUNK):
            cols = slice(c * FF_CHUNK, (c + 1) * FF_CHUNK)
            u = jnp.maximum(_dot(xn, w1_ref[:, cols]), 0.0)
            f_ref[:, cols] = (u * u).astype(BF16)
            du = (_dot_nt(dfo, w2_ref[cols, :]) * (2.0 * u)).astype(BF16)
            du_ref[:, cols] = du
            dxn = dxn + _dot_nt(du, w1_ref[:, cols])
        dx, dgpre = _rms_bwd(h1_ref[...], gpre_ref[...], dxn)
        _acc(dgpre_ref, first, dgpre)
        dh1_ref[...] = dh2 + dx

    full, wide = _row_spec(tm, D_MODEL), _row_spec(tm, D_FF)
    return pl.pallas_call(
        body, name="mlp_bwd", grid=(s_len // tm,),
        in_specs=[full, full, full, full, _full_spec((1, D_MODEL)), _const_spec((D_MODEL, D_FF)),
                  _const_spec((D_FF, D_MODEL)), _full_spec((1, D_MODEL))],
        out_specs=[full, full, wide, wide, _full_spec((1, D_MODEL)), _full_spec((1, D_MODEL))],
        out_shape=(SDS((s_len, D_MODEL), F32), SDS((s_len, D_MODEL), BF16), SDS((s_len, D_FF), BF16),
                   SDS((s_len, D_FF), BF16), SDS((1, D_MODEL), F32), SDS((1, D_MODEL), F32)),
        compiler_params=_params(1, V7X_VMEM_LIMIT),
    )(h1, dh2, fo, xn, g_pre, w1, w2, g_post)


def _ple_fwd_bwd(h2, p, target, g_ple, w_gate, w_proj, tm):
    s_len = h2.shape[0]

    def body(h2_ref, p_ref, t_ref, g_ref, wg_ref, wp_ref, dh2_ref, loss_ref, dg_ref, xn_ref, dgl_ref, dpp_ref, pb_ref):
        first = pl.program_id(0) == 0
        h2 = h2_ref[...]
        g = g_ref[...]
        xn, _ = _rms_fwd(h2, g)
        xn = xn.astype(BF16)
        xn_ref[...] = xn
        gate = 1.0 / (1.0 + jnp.exp(-_dot(xn, wg_ref[...])))
        pb = p_ref[...].astype(BF16)
        pb_ref[...] = pb
        pp = _dot(pb, wp_ref[...])
        diff = h2 + gate * pp - t_ref[...]
        _acc(loss_ref, first, jnp.full((8, 128), jnp.sum(diff * diff), F32))
        dh3 = diff * (1.0 / D_MODEL)
        dpp_ref[...] = (dh3 * gate).astype(BF16)
        dgl = (dh3 * pp * gate * (1.0 - gate)).astype(BF16)
        dgl_ref[...] = dgl
        dx, dg = _rms_bwd(h2, g, _dot_nt(dgl, wg_ref[...]))
        _acc(dg_ref, first, dg)
        dh2_ref[...] = dh3 + dx

    full = _row_spec(tm, D_MODEL)
    return pl.pallas_call(
        body, name="ple_fwd_bwd", grid=(s_len // tm,),
        in_specs=[full, _row_spec(tm, D_PLE), full, _full_spec((1, D_MODEL)), _full_spec((D_MODEL, D_MODEL)),
                  _full_spec((D_PLE, D_MODEL))],
        out_specs=[full, _full_spec((8, 128)), _full_spec((1, D_MODEL)), full, full, full, _row_spec(tm, D_PLE)],
        out_shape=(SDS((s_len, D_MODEL), F32), SDS((8, 128), F32), SDS((1, D_MODEL), F32), SDS((s_len, D_MODEL), BF16),
                   SDS((s_len, D_MODEL), BF16), SDS((s_len, D_MODEL), BF16), SDS((s_len, D_PLE), BF16)),
        compiler_params=_params(1, V7X_VMEM_LIMIT),
    )(h2, p, target, g_ple, w_gate, w_proj)


def _inproj_bwd(dqs, dkn, dva, dqb, dkb, dvb, qa_raw, ka_raw, x, dh1, g_pre, w_in, cos, sin, gq, gk, tm):
    s_len = x.shape[0]

    def body(dqs_ref, dkn_ref, dva_ref, dqb_ref, dkb_ref, dvb_ref, qa_ref, ka_ref, x_ref, dh1_ref, g_ref, w_ref,
             cos_ref, sin_ref, gq_ref, gk_ref, dx_ref, dproj_ref, dg_ref, dgq_ref, dgk_ref):
        first = pl.program_id(0) == 0
        dqa, dgq = _head_norm_rope_bwd(qa_ref[...], gq_ref[...], cos_ref[...], sin_ref[...], dqs_ref[...])
        dka, dgk = _head_norm_rope_bwd(ka_ref[...], gk_ref[...], cos_ref[:, :D_KV_A], sin_ref[:, :D_KV_A], dkn_ref[...])
        _acc(dgq_ref, first, dgq)
        _acc(dgk_ref, first, dgk)
        dproj_ref[:, OFF_QA:OFF_KA] = dqa.astype(BF16)
        dproj_ref[:, OFF_KA:OFF_VA] = dka.astype(BF16)
        dproj_ref[:, OFF_VA:OFF_QB] = dva_ref[...].astype(BF16)
        dproj_ref[:, OFF_QB:OFF_KB] = dqb_ref[...].astype(BF16)
        dproj_ref[:, OFF_KB:OFF_VB] = dkb_ref[...].astype(BF16)
        dproj_ref[:, OFF_VB:D_IN] = dvb_ref[...].astype(BF16)
        dxn = _dot_nt(dproj_ref[...], w_ref[...])
        dx, dg = _rms_bwd(x_ref[...], g_ref[...], dxn)
        _acc(dg_ref, first, dg)
        dx_ref[...] = dh1_ref[...] + dx

    half, kvw, full = _row_spec(tm, D_A), _row_spec(tm, D_KV_A), _row_spec(tm, D_MODEL)
    return pl.pallas_call(
        body, name="inproj_bwd", grid=(s_len // tm,),
        in_specs=[half, kvw, kvw, half, half, half, half, kvw, full, full, _full_spec((1, D_MODEL)),
                  _const_spec((D_MODEL, D_IN)), half, half, _full_spec((1, D_A)), _full_spec((1, D_KV_A))],
        out_specs=[full, _row_spec(tm, D_IN), _full_spec((1, D_MODEL)), _full_spec((1, D_A)), _full_spec((1, D_KV_A))],
        out_shape=(SDS((s_len, D_MODEL), F32), SDS((s_len, D_IN), BF16), SDS((1, D_MODEL), F32), SDS((1, D_A), F32),
                   SDS((1, D_KV_A), F32)),
        compiler_params=_params(1, V7X_VMEM_LIMIT),
    )(dqs, dkn, dva, dqb, dkb, dvb, qa_raw, ka_raw, x, dh1, g_pre, w_in, cos, sin, gq, gk)


def _weight_grad(a, b, name, tk1, tn, tm):
    s_len, k1 = a.shape
    n = b.shape[1]

    def body(a_ref, b_ref, o_ref):
        _acc(o_ref, pl.program_id(2) == 0, _dot_tn(a_ref[...], b_ref[...]))

    return pl.pallas_call(
        body, name=name, grid=(k1 // tk1, n // tn, s_len // tm),
        in_specs=[pl.BlockSpec((tm, tk1), lambda i, j, r: (r, i)), pl.BlockSpec((tm, tn), lambda i, j, r: (r, j))],
        out_specs=pl.BlockSpec((tk1, tn), lambda i, j, r: (i, j)),
        out_shape=SDS((k1, n), F32),
        compiler_params=_params(3, V7X_VMEM_LIMIT),
    )(a, b)


def _my_index():
    return 4 * lax.axis_index("x") + 2 * lax.axis_index("y") + lax.axis_index("c")


def _peer(k):
    f = k + 1
    x, y, c = lax.axis_index("x"), lax.axis_index("y"), lax.axis_index("c")
    return (x ^ ((f >> 2) & 1), y ^ ((f >> 1) & 1), c ^ (f & 1))


def _all_gather(blocks, name):
    n = len(blocks)

    def body(*refs):
        ins, outs = refs[:n], refs[n:2 * n]
        send_sems, recv_sems, local_sems = refs[2 * n:]
        me = _my_index()
        copies = []
        for a in range(n):
            own = pltpu.make_async_copy(ins[a], outs[a].at[me], local_sems.at[a])
            own.start()
            copies.append(own)
        remote = []
        for a in range(n):
            for k in range(N_DEV - 1):
                cp = pltpu.make_async_remote_copy(
                    src_ref=ins[a], dst_ref=outs[a].at[me], send_sem=send_sems.at[a, k], recv_sem=recv_sems.at[a, k],
                    device_id=_peer(k), device_id_type=pl.DeviceIdType.MESH)
                cp.start()
                remote.append(cp)
        for cp in copies:
            cp.wait()
        for cp in remote:
            cp.wait_send()
        for cp in remote:
            cp.wait_recv()

    any_spec = pl.BlockSpec(memory_space=pl.ANY)
    return pl.pallas_call(
        body, name=name,
        in_specs=[any_spec] * n, out_specs=[any_spec] * n,
        out_shape=[SDS((N_DEV,) + b.shape, b.dtype) for b in blocks],
        scratch_shapes=[pltpu.SemaphoreType.DMA((n, N_DEV - 1)), pltpu.SemaphoreType.DMA((n, N_DEV - 1)),
                        pltpu.SemaphoreType.DMA((n,))],
    )(*blocks)


def _all_to_all(parts, name):
    n = len(parts)

    def body(*refs):
        ins, outs = refs[:n], refs[n:2 * n]
        send_sems, recv_sems, local_sems = refs[2 * n:]
        me = _my_index()
        copies = []
        for a in range(n):
            own = pltpu.make_async_copy(ins[a].at[me], outs[a].at[me], local_sems.at[a])
            own.start()
            copies.append(own)
        remote = []
        for a in range(n):
            for k in range(N_DEV - 1):
                px, py, pc = _peer(k)
                cp = pltpu.make_async_remote_copy(
                    src_ref=ins[a].at[4 * px + 2 * py + pc], dst_ref=outs[a].at[me],
                    send_sem=send_sems.at[a, k], recv_sem=recv_sems.at[a, k],
                    device_id=(px, py, pc), device_id_type=pl.DeviceIdType.MESH)
                cp.start()
                remote.append(cp)
        for cp in copies:
            cp.wait()
        for cp in remote:
            cp.wait_send()
        for cp in remote:
            cp.wait_recv()

    any_spec = pl.BlockSpec(memory_space=pl.ANY)
    return pl.pallas_call(
        body, name=name,
        in_specs=[any_spec] * n, out_specs=[any_spec] * n,
        out_shape=[SDS(p.shape, p.dtype) for p in parts],
        scratch_shapes=[pltpu.SemaphoreType.DMA((n, N_DEV - 1)), pltpu.SemaphoreType.DMA((n, N_DEV - 1)),
                        pltpu.SemaphoreType.DMA((n,))],
    )(*parts)


def _sum_adamw(parts, w, m, v, name, tr):
    rows, cols = w.shape
    c1 = 1.0 / (1.0 - ADAM_B1 ** ADAM_STEP)
    c2 = 1.0 / (1.0 - ADAM_B2 ** ADAM_STEP)

    def body(p_ref, w_ref, m_ref, v_ref, g_ref, d_ref, nm_ref, nv_ref):
        g = p_ref[0]
        for j in range(1, N_DEV):
            g = g + p_ref[j]
        g_ref[...] = g
        nm = ADAM_B1 * m_ref[...] + (1.0 - ADAM_B1) * g
        nv = ADAM_B2 * v_ref[...] + (1.0 - ADAM_B2) * (g * g)
        nm_ref[...] = nm
        nv_ref[...] = nv
        d_ref[...] = -ADAM_LR * ((nm * c1) / (jnp.sqrt(nv * c2) + ADAM_EPS) + ADAM_WD * w_ref[...])

    blk = pl.BlockSpec((tr, cols), lambda i: (i, 0))
    return pl.pallas_call(
        body, name=name, grid=(rows // tr,),
        in_specs=[pl.BlockSpec((N_DEV, tr, cols), lambda i: (0, i, 0)), blk, blk, blk],
        out_specs=[blk] * 4, out_shape=[SDS((rows, cols), F32)] * 4,
        compiler_params=_params(1, V7X_VMEM_LIMIT),
    )(parts, w, m, v)


_SMALL = (("g_attn_pre", 1024), ("g_q", 64), ("g_k", 64), ("g_out_a", 512), ("g_out_b", 512), ("g_attn_post", 1024),
          ("rel_bias", 256), ("g_mlp_pre", 1024), ("g_mlp_post", 1024), ("g_ple", 1024))
_SLAB_ROWS = 56


def _pack_small(vals):
    rows = []
    for (name, size) in _SMALL:
        flat = vals[name].reshape(-1).astype(F32)
        padded = -(-size // 128) * 128
        rows.append(jnp.pad(flat, (0, padded - size)).reshape(padded // 128, 128))
    slab = jnp.concatenate(rows, axis=0)
    return jnp.pad(slab, ((0, _SLAB_ROWS - slab.shape[0]), (0, 0)))


def _unpack_small(slab, shapes):
    out, row = {}, 0
    for (name, size) in _SMALL:
        nrow = -(-size // 128)
        out[name] = slab[row:row + nrow].reshape(-1)[:size].reshape(shapes[name])
        row += nrow
    return out


def _rope_tables(s_len):
    rows = s_len // GRID_W
    row = jnp.broadcast_to(jnp.arange(rows)[:, None], (rows, GRID_W)).reshape(-1).astype(F32)
    col = jnp.broadcast_to(jnp.arange(GRID_W)[None, :], (rows, GRID_W)).reshape(-1).astype(F32)
    n_axis = ROPE_HALF // 2
    inv_freq = ROPE_THETA ** (-jnp.arange(n_axis, dtype=F32) / n_axis)
    ang = jnp.concatenate([row[:, None] * inv_freq, col[:, None] * inv_freq], axis=-1)
    cos, sin = jnp.cos(ang), jnp.sin(ang)
    cos = jnp.tile(jnp.concatenate([cos, cos], axis=-1), (1, N_HEADS_A))
    sin = jnp.tile(jnp.concatenate([-sin, sin], axis=-1), (1, N_HEADS_A))
    return cos, sin


def _local_step(x, p, target, w_in, w_out, w_ff1, w_ff2, w_gate, w_proj, small):
    s_len = x.shape[0]
    tm = min(256, s_len)
    tq = min(128, s_len)
    tk = min(1024, s_len // 2)
    cos, sin = _rope_tables(s_len)
    gq = jnp.tile(small["g_q"], (1, N_HEADS_A))
    gk = jnp.tile(small["g_k"], (1, N_KV_A))

    qa_raw, ka_raw, qs, kn, va, qb, kb, vb, xn1 = _inproj_fwd(x, small["g_attn_pre"], w_in, cos, sin, gq, gk, tm)

    def kv_major(a):
        return a.reshape(s_len, N_KV_A, HEAD_DIM).transpose(1, 0, 2)

    def kv_chunks_t(a):
        return a.reshape(s_len // tk, tk, N_KV_A, HEAD_DIM).transpose(2, 0, 3, 1)

    def kv_unchunk(a):
        return a.transpose(1, 3, 0, 2).reshape(s_len, D_KV_A)

    k_maj, kt, vt = kv_major(kn), kv_chunks_t(kn), kv_chunks_t(va)
    ya, lse_a = _attn_fwd(qs, k_maj, vt, tq, tk)

    ob, lb, tiles = [], [], []
    for (_, dil) in DILATED_PATTERNS:
        t = min(256, s_len // dil)
        bias, bias_t = _bias_tiles(small["rel_bias"], dil, t)
        tiles.append((t, bias, bias_t, _band_buckets(dil, t)))
        o, l = _band_fwd(qb, kb, vb, bias, dil, t)
        ob.append(o)
        lb.append(l)

    h1, yo, yb, lse_b, ycat = _attn_out_fwd(ya, ob, lb, x, small["g_out_a"], small["g_out_b"], w_out,
                                            small["g_attn_post"], tm)
    h2, fo, xn2 = _mlp_fwd(h1, small["g_mlp_pre"], w_ff1, w_ff2, small["g_mlp_post"], tm)
    dh2, loss_part, dg_ple, xn3, dgl, dpp, pb = _ple_fwd_bwd(h2, p, target, small["g_ple"], w_gate, w_proj, tm)
    dh1, dfo, du, f, dg_mlp_post, dg_mlp_pre = _mlp_bwd(h1, dh2, fo, xn2, small["g_mlp_pre"], w_ff1, w_ff2,
                                                          small["g_mlp_post"], tm)
    dyo, dya, dyb, delta_a, delta_b, dg_attn_post, dg_out_a, dg_out_b = _attn_out_bwd(
        dh1, yo, ya, yb, small["g_out_a"], small["g_out_b"], w_out, small["g_attn_post"], tm)

    dqs, dk_t, dv_t = _attn_bwd(qs, dya, lse_a, delta_a, kt, k_maj, vt, tq, tk)
    dkn, dva = kv_unchunk(dk_t), kv_unchunk(dv_t)

    dqb = dkb = dvb = jnp.zeros((s_len, D_B), F32)
    dbias = []
    for (_, dil), (t, bias, bias_t, _) in zip(DILATED_PATTERNS, tiles):
        dqb, db = _band_dq(qb, dyb, lse_b, delta_b, kb, vb, bias, dqb, dil, t)
        dkb, dvb = _band_dkv(kb, vb, qb, dyb, lse_b, delta_b, bias_t, dkb, dvb, dil, t)
        dbias.append(db)
    d_rel = _dbias_reduce(dbias, [jnp.asarray(tl[3]) for tl in tiles])[:, :N_HEADS_B]

    dx, dproj, dg_attn_pre, dgq_lanes, dgk_lanes = _inproj_bwd(
        dqs, dkn, dva, dqb, dkb, dvb, qa_raw, ka_raw, x, dh1, small["g_attn_pre"], w_in, cos, sin, gq, gk, tm)

    tg = min(512, s_len)
    grads = {
        "w_in": _weight_grad(xn1, dproj, "grad_w_in", D_MODEL, 768, tg),
        "w_out": _weight_grad(ycat, dyo, "grad_w_out", D_MODEL, D_MODEL, tg),
        "w_ff1": _weight_grad(xn2, du, "grad_w_ff1", D_MODEL, 1024, tg),
        "w_ff2": _weight_grad(f, dfo, "grad_w_ff2", 1024, D_MODEL, tg),
        "w_ple_gate": _weight_grad(xn3, dgl, "grad_w_ple_gate", D_MODEL, D_MODEL, tg),
        "w_ple_proj": _weight_grad(pb, dpp, "grad_w_ple_proj", D_PLE, D_MODEL, tg),
    }
    small_grads = {
        "g_attn_pre": dg_attn_pre, "g_q": dgq_lanes.reshape(N_HEADS_A, HEAD_DIM).sum(0, keepdims=True),
        "g_k": dgk_lanes.reshape(N_KV_A, HEAD_DIM).sum(0, keepdims=True), "g_out_a": dg_out_a, "g_out_b": dg_out_b,
        "g_attn_post": dg_attn_post, "rel_bias": d_rel, "g_mlp_pre": dg_mlp_pre, "g_mlp_post": dg_mlp_post,
        "g_ple": dg_ple,
    }
    return loss_part[0, 0], dx, grads, small_grads


_BIG = (("w_in", 1), ("w_out", 0), ("w_ff1", 1), ("w_ff2", 0), ("w_ple_gate", 0), ("w_ple_proj", 1))


def _assemble(gathered, axis):
    if axis == 0:
        return gathered.reshape(-1, gathered.shape[2])
    return gathered.transpose(1, 0, 2).reshape(gathered.shape[1], -1)


def _cut(full, axis):
    if axis == 0:
        return full.reshape(N_DEV, full.shape[0] // N_DEV, full.shape[1])
    return full.reshape(full.shape[0], N_DEV, full.shape[1] // N_DEV).transpose(1, 0, 2)


def kernel(x, p, w_in, g_attn_pre, g_q, g_k, g_out_a, g_out_b, w_out, g_attn_post, rel_bias, g_mlp_pre, w_ff1, w_ff2, g_mlp_post, g_ple, w_ple_gate, w_ple_proj, loss_target, m_w_in, m_g_attn_pre, m_g_q, m_g_k, m_g_out_a, m_g_out_b, m_w_out, m_g_attn_post, m_rel_bias, m_g_mlp_pre, m_w_ff1, m_w_ff2, m_g_mlp_post, m_g_ple, m_w_ple_gate, m_w_ple_proj, v_w_in, v_g_attn_pre, v_g_q, v_g_k, v_g_out_a, v_g_out_b, v_w_out, v_g_attn_post, v_rel_bias, v_g_mlp_pre, v_w_ff1, v_w_ff2, v_g_mlp_post, v_g_ple, v_w_ple_gate, v_w_ple_proj):
    given = dict(locals())
    small_names = [n for n, _ in _SMALL]
    small = {n: given[n] for n in small_names}
    shards = {n: given[n][0] for n, _ in _BIG}

    gathered = _all_gather([shards[n].astype(BF16) for n, _ in _BIG], "gather_weights")
    full = {n: _assemble(g, axis) for (n, axis), g in zip(_BIG, gathered)}

    loss_part, dx, grads, small_grads = _local_step(
        x[0], p[0, 0], loss_target[0], full["w_in"], full["w_out"], full["w_ff1"], full["w_ff2"], full["w_ple_gate"],
        full["w_ple_proj"], small)

    parts = [_cut(grads[n], axis) for n, axis in _BIG]
    slab = _pack_small(small_grads)
    slab_parts = jnp.broadcast_to(slab[None], (N_DEV,) + slab.shape)
    received = _all_to_all(parts + [slab_parts], "exchange_grads")

    out_g, out_d, out_m, out_v = {}, {}, {}, {}
    for (n, _), rec in zip(_BIG, received[:-1]):
        rows = shards[n].shape[0]
        g, d, nm, nv = _sum_adamw(rec, shards[n], given["m_" + n][0], given["v_" + n][0], "adamw_" + n, min(rows, 128))
        out_g[n], out_d[n], out_m[n], out_v[n] = g[None], d[None], nm[None], nv[None]
    g, d, nm, nv = _sum_adamw(received[-1], _pack_small(small), _pack_small({n: given["m_" + n] for n in small_names}),
                              _pack_small({n: given["v_" + n] for n in small_names}), "adamw_small", _SLAB_ROWS)
    shapes = {n: given[n].shape for n in small_names}
    for dst, slab_out in ((out_g, g), (out_d, d), (out_m, nm), (out_v, nv)):
        dst.update(_unpack_small(slab_out, shapes))

    loss = lax.psum(loss_part * (0.5 / D_MODEL), MESH_AXES)
    order = ["w_in", "g_attn_pre", "g_q", "g_k", "g_out_a", "g_out_b", "w_out", "g_attn_post", "rel_bias", "g_mlp_pre",
             "w_ff1", "w_ff2", "g_mlp_post", "g_ple", "w_ple_gate", "w_ple_proj"]
    return (loss, dx[None], *[out_g[n] for n in order], *[out_d[n] for n in order], *[out_m[n] for n in order],
            *[out_v[n] for n in order])
```

```python
import functools
import math

import jax
import jax.numpy as jnp
import numpy as np
from jax import lax
from jax.experimental import pallas as pl
from jax.experimental.pallas import tpu as pltpu

F32 = jnp.float32
BF16 = jnp.bfloat16
SDS = jax.ShapeDtypeStruct

D_MODEL = 1024
HEAD_DIM = 64
N_HEADS_A = 8
N_KV_A = 2
GROUP_A = N_HEADS_A // N_KV_A
N_HEADS_B = 8
D_A = N_HEADS_A * HEAD_DIM
D_KV_A = N_KV_A * HEAD_DIM
D_B = N_HEADS_B * HEAD_DIM
D_IN = D_A + 2 * D_KV_A + 3 * D_B
D_FF = 4 * D_MODEL
D_PLE = 256
GRID_W = 64
ROPE_THETA = 10000.0
ROPE_HALF = HEAD_DIM // 2
DILATED_PATTERNS = ((128, 1), (512, 4), (2048, 16))
BAND_HALF = 64
N_BUCKETS = 32
MAX_DISTANCE = 1024
EPS = 1e-6
NEG_BIG = -1e30
SCORE_SCALE = HEAD_DIM ** -0.5

ADAM_LR = 0.001
ADAM_B1 = 0.9
ADAM_B2 = 0.999
ADAM_EPS = 1e-08
ADAM_WD = 0.01
ADAM_STEP = 10

N_DEV = 8
MESH_AXES = ("x", "y", "c")
V7X_VMEM_LIMIT = 56 * 1024 * 1024

OFF_QA, OFF_KA, OFF_VA, OFF_QB, OFF_KB, OFF_VB = 0, 512, 640, 768, 1280, 1792


def _params(n_axes, vmem=None):
    return pltpu.CompilerParams(dimension_semantics=("arbitrary",) * n_axes, vmem_limit_bytes=vmem)


def _dot(a, b):
    return jnp.dot(a, b, preferred_element_type=F32)


def _dot_nt(a, b):
    return lax.dot_general(a, b, (((1,), (1,)), ((), ())), preferred_element_type=F32)


def _dot_tn(a, b):
    return lax.dot_general(a, b, (((0,), (0,)), ((), ())), preferred_element_type=F32)


def _rms_fwd(x, g):
    r = lax.rsqrt(jnp.mean(x * x, axis=-1, keepdims=True) + EPS)
    return x * r * g, r


def _rms_bwd(x, g, dy):
    r = lax.rsqrt(jnp.mean(x * x, axis=-1, keepdims=True) + EPS)
    xh = x * r
    dxh = dy * g
    dx = r * (dxh - xh * jnp.mean(dxh * xh, axis=-1, keepdims=True))
    return dx, jnp.sum(dy * xh, axis=0, keepdims=True)


def _head_sum(v):
    head = lax.broadcasted_iota(jnp.int32, v.shape, 1) >> 6
    out = jnp.zeros_like(v)
    for h in range(v.shape[1] // HEAD_DIM):
        msk = head == h
        s = jnp.sum(jnp.where(msk, v, 0.0), axis=-1, keepdims=True)
        out = jnp.where(msk, s, out)
    return out


def _swap_halves(v):
    w = v.shape[1]
    lane = lax.broadcasted_iota(jnp.int32, v.shape, 1)
    first_half = (lane & (HEAD_DIM - 1)) < ROPE_HALF
    return jnp.where(first_half, pltpu.roll(v, w - ROPE_HALF, 1), pltpu.roll(v, ROPE_HALF, 1))


def _head_norm_rope(v, g, cos, sin_signed):
    r = lax.rsqrt(_head_sum(v * v) * (1.0 / HEAD_DIM) + EPS)
    y = v * r * g
    return y * cos + _swap_halves(y) * sin_signed


def _head_norm_rope_bwd(v, g, cos, sin_signed, dout):
    dy = dout * cos - _swap_halves(dout) * sin_signed
    r = lax.rsqrt(_head_sum(v * v) * (1.0 / HEAD_DIM) + EPS)
    xh = v * r
    dxh = dy * g
    dv = r * (dxh - xh * (_head_sum(dxh * xh) * (1.0 / HEAD_DIM)))
    return dv, jnp.sum(dy * xh, axis=0, keepdims=True)


def _row_spec(tm, n):
    return pl.BlockSpec((tm, n), lambda i: (i, 0))


def _full_spec(shape):
    nd = len(shape)
    return pl.BlockSpec(shape, lambda *_: (0,) * nd)


def _const_spec(shape):
    nd = len(shape)
    return pl.BlockSpec(shape, lambda *_: (0,) * nd, pipeline_mode=pl.Buffered(1))


def _acc(ref, first, val):
    @pl.when(first)
    def _():
        ref[...] = val

    @pl.when(jnp.logical_not(first))
    def _():
        ref[...] += val


def _inproj_fwd(x, g_pre, w_in, cos, sin, gq, gk, tm):
    s_len = x.shape[0]

    def body(x_ref, g_ref, w_ref, cos_ref, sin_ref, gq_ref, gk_ref,
             qa_raw, ka_raw, qs, kn, va, qb, kb, vb, xn_out):
        xn, _ = _rms_fwd(x_ref[...], g_ref[...])
        xn = xn.astype(BF16)
        xn_out[...] = xn
        qa = _dot(xn, w_ref[:, OFF_QA:OFF_KA])
        qa_raw[...] = qa
        qs[...] = (_head_norm_rope(qa, gq_ref[...], cos_ref[...], sin_ref[...]) * SCORE_SCALE).astype(BF16)
        ka = _dot(xn, w_ref[:, OFF_KA:OFF_VA])
        ka_raw[...] = ka
        kn[...] = _head_norm_rope(ka, gk_ref[...], cos_ref[:, :D_KV_A], sin_ref[:, :D_KV_A]).astype(BF16)
        va[...] = _dot(xn, w_ref[:, OFF_VA:OFF_QB]).astype(BF16)
        qb[...] = (_dot(xn, w_ref[:, OFF_QB:OFF_KB]) * SCORE_SCALE).astype(BF16)
        kb[...] = _dot(xn, w_ref[:, OFF_KB:OFF_VB]).astype(BF16)
        vb[...] = _dot(xn, w_ref[:, OFF_VB:D_IN]).astype(BF16)

    return pl.pallas_call(
        body, name="inproj_fwd", grid=(s_len // tm,),
        in_specs=[_row_spec(tm, D_MODEL), _full_spec((1, D_MODEL)), _const_spec((D_MODEL, D_IN)),
                  _row_spec(tm, D_A), _row_spec(tm, D_A), _full_spec((1, D_A)), _full_spec((1, D_KV_A))],
        out_specs=[_row_spec(tm, D_A), _row_spec(tm, D_KV_A), _row_spec(tm, D_A), _row_spec(tm, D_KV_A),
                   _row_spec(tm, D_KV_A), _row_spec(tm, D_B), _row_spec(tm, D_B), _row_spec(tm, D_B),
                   _row_spec(tm, D_MODEL)],
        out_shape=(SDS((s_len, D_A), F32), SDS((s_len, D_KV_A), F32), SDS((s_len, D_A), BF16),
                   SDS((s_len, D_KV_A), BF16), SDS((s_len, D_KV_A), BF16), SDS((s_len, D_B), BF16),
                   SDS((s_len, D_B), BF16), SDS((s_len, D_B), BF16), SDS((s_len, D_MODEL), BF16)),
        compiler_params=_params(1, V7X_VMEM_LIMIT),
    )(x, g_pre, w_in, cos, sin, gq, gk)


def _stack_heads(ref, tq):
    return jnp.concatenate([ref[:, HEAD_DIM * g:HEAD_DIM * (g + 1)] for g in range(GROUP_A)], axis=0)


def _stack_cols(ref, tq):
    return jnp.concatenate([ref[:, HEAD_DIM * g:HEAD_DIM * g + 1] for g in range(GROUP_A)], axis=0)


def _attn_fwd(qs, k, vt, tq, tk):
    s_len = qs.shape[0]
    nk = s_len // tk
    assert nk % 2 == 0
    gw = GROUP_A * HEAD_DIM
    rows = GROUP_A * tq
    vrows = vt.shape[2]

    def body(q_ref, k_ref, vt_ref, o_ref, lse_ref, s_buf):
        qt = _stack_heads(q_ref, tq).T

        def scores(j, slot):
            kj = k_ref[pl.ds(pl.multiple_of(j * tk, tk), tk), :]
            s_buf[slot] = _dot(kj, qt)

        def consume(j, slot, carry):
            m, acc = carry
            st = s_buf[slot]
            m_new = jnp.maximum(m, jnp.max(st, axis=0, keepdims=True))
            pt = jnp.exp(st - m_new)
            acc = jnp.exp(m - m_new) * acc + _dot(vt_ref[j], pt.astype(BF16))
            return m_new, acc

        scores(0, 0)

        def step(jj, carry):
            j = 2 * jj
            scores(j + 1, 1)
            carry = consume(j, 0, carry)
            scores(jnp.minimum(j + 2, nk - 1), 0)
            return consume(j + 1, 1, carry)

        m, acc = lax.fori_loop(0, nk // 2, step, (jnp.full((1, rows), NEG_BIG, F32), jnp.zeros((vrows, rows), F32)))
        l = acc[HEAD_DIM:HEAD_DIM + 1]
        o = (acc[:HEAD_DIM] / l).T
        lse = jnp.broadcast_to(m + jnp.log(l), (HEAD_DIM, rows)).T
        for g in range(GROUP_A):
            o_ref[:, HEAD_DIM * g:HEAD_DIM * (g + 1)] = o[g * tq:(g + 1) * tq]
            lse_ref[:, HEAD_DIM * g:HEAD_DIM * (g + 1)] = lse[g * tq:(g + 1) * tq]

    tile = pl.BlockSpec((tq, gw), lambda kv, i: (i, kv))
    return pl.pallas_call(
        body, name="attn_fwd", grid=(N_KV_A, s_len // tq),
        in_specs=[tile, pl.BlockSpec((None, s_len, HEAD_DIM), lambda kv, i: (kv, 0, 0)),
                  pl.BlockSpec((None, nk, vrows, tk), lambda kv, i: (kv, 0, 0, 0))],
        out_specs=[tile, tile],
        out_shape=(SDS((s_len, D_A), F32), SDS((s_len, D_A), F32)),
        scratch_shapes=[pltpu.VMEM((2, tk, rows), F32)],
        compiler_params=_params(2, V7X_VMEM_LIMIT),
    )(qs, k, vt)


def _attn_bwd(qs, do, lse, delta, kt, k, vt, tq, tk):
    s_len = qs.shape[0]
    nk = s_len // tk
    assert nk % 2 == 0
    nq = s_len // tq
    gw = GROUP_A * HEAD_DIM
    rows = GROUP_A * tq

    def body(q_ref, do_ref, lse_ref, delta_ref, kt_ref, k_ref, vt_ref, dq_ref, dk_hbm, dv_hbm,
             dk_acc, dv_acc, s_buf, dp_buf):
        kv = pl.program_id(0)
        i = pl.program_id(1)

        @pl.when(i == 0)
        def _():
            dk_acc[...] = jnp.zeros_like(dk_acc)
            dv_acc[...] = jnp.zeros_like(dv_acc)

        q = _stack_heads(q_ref, tq)
        dout = _stack_heads(do_ref, tq)
        qt = q.T
        doutt = dout.T
        row_lse = _stack_cols(lse_ref, tq)
        row_delta = _stack_cols(delta_ref, tq)

        def scores(j, slot):
            s_buf[slot] = _dot(q, kt_ref[j])
            dp_buf[slot] = _dot(dout, vt_ref[j])

        def consume(j, slot, dq):
            p = jnp.exp(s_buf[slot] - row_lse)
            ds = (p * (dp_buf[slot] - row_delta)).astype(BF16)
            dv_acc[j] += _dot(doutt, p.astype(BF16))
            dk_acc[j] += _dot(qt, ds)
            return dq + _dot(ds, k_ref[pl.ds(pl.multiple_of(j * tk, tk), tk), :])

        scores(0, 0)

        def step(jj, dq):
            j = 2 * jj
            scores(j + 1, 1)
            dq = consume(j, 0, dq)
            scores(jnp.minimum(j + 2, nk - 1), 0)
            return consume(j + 1, 1, dq)

        dq = lax.fori_loop(0, nk // 2, step, jnp.zeros((rows, HEAD_DIM), F32)) * SCORE_SCALE
        for g in range(GROUP_A):
            dq_ref[:, HEAD_DIM * g:HEAD_DIM * (g + 1)] = dq[g * tq:(g + 1) * tq]

        @pl.when(i == nq - 1)
        def _():
            pltpu.sync_copy(dk_acc, dk_hbm.at[kv])
            pltpu.sync_copy(dv_acc, dv_hbm.at[kv])

    tile = pl.BlockSpec((tq, gw), lambda kv, i: (i, kv))
    chunks = pl.BlockSpec((None, nk, HEAD_DIM, tk), lambda kv, i: (kv, 0, 0, 0))
    grad_t = SDS((N_KV_A, nk, HEAD_DIM, tk), F32)
    return pl.pallas_call(
        body, name="attn_bwd", grid=(N_KV_A, nq),
        in_specs=[tile, tile, tile, tile, chunks,
                  pl.BlockSpec((None, s_len, HEAD_DIM), lambda kv, i: (kv, 0, 0)), chunks],
        out_specs=[tile, pl.BlockSpec(memory_space=pl.ANY), pl.BlockSpec(memory_space=pl.ANY)],
        out_shape=(SDS((s_len, D_A), F32), grad_t, grad_t),
        scratch_shapes=[pltpu.VMEM((nk, HEAD_DIM, tk), F32), pltpu.VMEM((nk, HEAD_DIM, tk), F32),
                        pltpu.VMEM((2, rows, tk), F32), pltpu.VMEM((2, rows, tk), F32)],
        compiler_params=_params(2, V7X_VMEM_LIMIT),
    )(qs, do, lse, delta, kt, k, vt)


def _band_specs(length, t):
    hb = t // BAND_HALF
    last = length // BAND_HALF - 1
    main = pl.BlockSpec((t, D_B), lambda r, i: (i, r))
    prev = pl.BlockSpec((BAND_HALF, D_B), lambda r, i: (jnp.maximum(i * hb - 1, 0), r))
    nxt = pl.BlockSpec((BAND_HALF, D_B), lambda r, i: (jnp.minimum((i + 1) * hb, last), r))
    return main, [prev, main, nxt]


def _window(refs):
    return jnp.concatenate([r[...] for r in refs], axis=0)


def _head(v, h):
    return v[:, HEAD_DIM * h:HEAD_DIM * (h + 1)]


def _band_fwd(q, k, v, bias, dil, t):
    s_len = q.shape[0]
    length = s_len // dil
    w = t + 2 * BAND_HALF
    view = lambda a: a.reshape(length, dil * D_B)
    main, win = _band_specs(length, t)

    def body(q_ref, k0, k1, k2, v0, v1, v2, bias_ref, o_ref, lse_ref, s_buf, kt_buf):
        i = pl.program_id(1)
        kt_buf[...] = _window((k0, k1, k2)).T
        vw = _window((v0, v1, v2))
        pos = i * t - BAND_HALF + lax.broadcasted_iota(jnp.int32, (1, w), 1)
        valid = (pos >= 0) & (pos < length)
        for h in range(N_HEADS_B):
            s_buf[h] = _dot(q_ref[:, HEAD_DIM * h:HEAD_DIM * (h + 1)], kt_buf[HEAD_DIM * h:HEAD_DIM * (h + 1), :])
        for h in range(N_HEADS_B):
            s = jnp.where(valid, s_buf[h] + bias_ref[h], NEG_BIG)
            m = jnp.max(s, axis=-1, keepdims=True)
            e = jnp.exp(s - m)
            den = jnp.sum(e, axis=-1, keepdims=True)
            o_ref[:, HEAD_DIM * h:HEAD_DIM * (h + 1)] = _dot(e.astype(BF16), _head(vw, h)) / den
            lse_ref[:, HEAD_DIM * h:HEAD_DIM * (h + 1)] = jnp.broadcast_to(m + jnp.log(den), (t, HEAD_DIM))

    o, lse = pl.pallas_call(
        body, name=f"band_fwd_d{dil}", grid=(dil, length // t),
        in_specs=[main] + win + win + [_full_spec((N_HEADS_B, t, w))],
        out_specs=[main, main],
        out_shape=(SDS((length, dil * D_B), F32), SDS((length, dil * D_B), F32)),
        scratch_shapes=[pltpu.VMEM((N_HEADS_B, t, w), F32), pltpu.VMEM((D_B, w), BF16)],
        compiler_params=_params(2, V7X_VMEM_LIMIT),
    )(view(q), view(k), view(k), view(k), view(v), view(v), view(v), bias)
    return o.reshape(s_len, D_B), lse.reshape(s_len, D_B)


def _band_dq(q, do, lse, delta, k, v, bias, dq_in, dil, t):
    s_len = q.shape[0]
    length = s_len // dil
    w = t + 2 * BAND_HALF
    view = lambda a: a.reshape(length, dil * D_B)
    main, win = _band_specs(length, t)

    def body(q_ref, do_ref, lse_ref, delta_ref, k0, k1, k2, v0, v1, v2, bias_ref, acc_ref, dq_ref, dbias_ref,
             s_buf, dp_buf, kt_buf, vt_buf):
        i = pl.program_id(1)
        first = (pl.program_id(0) == 0) & (i == 0)
        kw = _window((k0, k1, k2))
        kt_buf[...] = kw.T
        vt_buf[...] = _window((v0, v1, v2)).T
        pos = i * t - BAND_HALF + lax.broadcasted_iota(jnp.int32, (1, w), 1)
        valid = (pos >= 0) & (pos < length)
        for h in range(N_HEADS_B):
            cols = slice(HEAD_DIM * h, HEAD_DIM * (h + 1))
            s_buf[h] = _dot(q_ref[:, cols], kt_buf[cols, :])
            dp_buf[h] = _dot(do_ref[:, cols], vt_buf[cols, :])
        for h in range(N_HEADS_B):
            cols = slice(HEAD_DIM * h, HEAD_DIM * (h + 1))
            s = jnp.where(valid, s_buf[h] + bias_ref[h], NEG_BIG)
            p = jnp.exp(s - lse_ref[:, HEAD_DIM * h:HEAD_DIM * h + 1])
            ds = p * (dp_buf[h] - delta_ref[:, HEAD_DIM * h:HEAD_DIM * h + 1])
            dq_ref[:, cols] = acc_ref[:, cols] + _dot(ds.astype(BF16), _head(kw, h)) * SCORE_SCALE

            @pl.when(first)
            def _():
                dbias_ref[h] = ds

            @pl.when(jnp.logical_not(first))
            def _():
                dbias_ref[h] += ds

    dq, dbias = pl.pallas_call(
        body, name=f"band_dq_d{dil}", grid=(dil, length // t),
        in_specs=[main, main, main, main] + win + win + [_full_spec((N_HEADS_B, t, w)), main],
        out_specs=[main, _full_spec((N_HEADS_B, t, w))],
        out_shape=(SDS((length, dil * D_B), F32), SDS((N_HEADS_B, t, w), F32)),
        scratch_shapes=[pltpu.VMEM((N_HEADS_B, t, w), F32), pltpu.VMEM((N_HEADS_B, t, w), F32),
                        pltpu.VMEM((D_B, w), BF16), pltpu.VMEM((D_B, w), BF16)],
        compiler_params=_params(2, V7X_VMEM_LIMIT),
    )(view(q), view(do), view(lse), view(delta), view(k), view(k), view(k), view(v), view(v), view(v), bias,
      view(dq_in))
    return dq.reshape(s_len, D_B), dbias


def _band_dkv(k, v, q, do, lse, delta, bias_t, dk_in, dv_in, dil, t):
    s_len = q.shape[0]
    length = s_len // dil
    w = t + 2 * BAND_HALF
    view = lambda a: a.reshape(length, dil * D_B)
    main, win = _band_specs(length, t)

    def body(k_ref, v_ref, q0, q1, q2, d0, d1, d2, l0, l1, l2, e0, e1, e2, bias_ref, dk_acc, dv_acc, dk_ref, dv_ref,
             s_buf, dp_buf, kt_buf, vt_buf):
        i = pl.program_id(1)
        qw = _window((q0, q1, q2))
        dow = _window((d0, d1, d2))
        lsew = _window((l0, l1, l2))
        deltaw = _window((e0, e1, e2))
        pos = i * t - BAND_HALF + lax.broadcasted_iota(jnp.int32, (w, 1), 0)
        valid = (pos >= 0) & (pos < length)
        kt_buf[...] = k_ref[...].T
        vt_buf[...] = v_ref[...].T
        for h in range(N_HEADS_B):
            cols = slice(HEAD_DIM * h, HEAD_DIM * (h + 1))
            s_buf[h] = _dot(_head(qw, h), kt_buf[cols, :])
            dp_buf[h] = _dot(_head(dow, h), vt_buf[cols, :])
        qwt = qw.T
        dowt = dow.T
        dkt, dvt = [], []
        for h in range(N_HEADS_B):
            rows = slice(HEAD_DIM * h, HEAD_DIM * (h + 1))
            s = jnp.where(valid, s_buf[h] + bias_ref[h], NEG_BIG)
            p = jnp.exp(s - lsew[:, HEAD_DIM * h:HEAD_DIM * h + 1])
            ds = p * (dp_buf[h] - deltaw[:, HEAD_DIM * h:HEAD_DIM * h + 1])
            dvt.append(_dot(dowt[rows, :], p.astype(BF16)))
            dkt.append(_dot(qwt[rows, :], ds.astype(BF16)))
        dv_ref[...] = dv_acc[...] + jnp.concatenate(dvt, axis=0).T
        dk_ref[...] = dk_acc[...] + jnp.concatenate(dkt, axis=0).T

    dk, dv = pl.pallas_call(
        body, name=f"band_dkv_d{dil}", grid=(dil, length // t),
        in_specs=[main, main] + win + win + win + win + [_full_spec((N_HEADS_B, w, t)), main, main],
        out_specs=[main, main],
        out_shape=(SDS((length, dil * D_B), F32), SDS((length, dil * D_B), F32)),
        scratch_shapes=[pltpu.VMEM((N_HEADS_B, w, t), F32), pltpu.VMEM((N_HEADS_B, w, t), F32),
                        pltpu.VMEM((D_B, t), BF16), pltpu.VMEM((D_B, t), BF16)],
        compiler_params=_params(2, V7X_VMEM_LIMIT),
    )(view(k), view(v), view(q), view(q), view(q), view(do), view(do), view(do), view(lse), view(lse), view(lse),
      view(delta), view(delta), view(delta), bias_t, view(dk_in), view(dv_in))
    return dk.reshape(s_len, D_B), dv.reshape(s_len, D_B)


def _t5_bucket_np(rel):
    nb = N_BUCKETS // 2
    max_exact = nb // 2
    side = np.where(rel > 0, nb, 0)
    n = np.abs(rel)
    ratio = np.maximum(n, max_exact).astype(np.float32) / np.float32(max_exact)
    large = max_exact + (np.log(ratio) / np.float32(math.log(MAX_DISTANCE / max_exact))
                         * np.float32(nb - max_exact)).astype(np.int32)
    large = np.minimum(large, nb - 1)
    return (side + np.where(n < max_exact, n, large)).astype(np.int32)


def _band_buckets(dil, t):
    rel = np.arange(t + 2 * BAND_HALF)[None, :] - BAND_HALF - np.arange(t)[:, None]
    bucket = _t5_bucket_np(np.clip(rel, -BAND_HALF, BAND_HALF) * dil)
    return np.where(np.abs(rel) <= BAND_HALF, bucket, -1).astype(np.int32)


def _toeplitz(vals, rows, cols):
    heads = vals.shape[0]
    period = rows + cols
    vec = jnp.concatenate([vals[:, rows - 1:], jnp.zeros((heads, 1), vals.dtype), vals[:, :rows - 1]], axis=1)
    flat = jnp.broadcast_to(vec[:, None, :], (heads, rows, period)).reshape(heads, rows * period)
    return flat[:, :rows * (period - 1)].reshape(heads, rows, period - 1)[:, :, :cols]


def _bias_tiles(rel_bias, dil, t):
    w = t + 2 * BAND_HALF
    rel = np.arange(-BAND_HALF, BAND_HALF + 1)
    bucket = _t5_bucket_np(rel * dil)
    runs, start = [], 0
    for i in range(1, len(bucket) + 1):
        if i == len(bucket) or bucket[i] != bucket[start]:
            b = int(bucket[start])
            runs.append(jnp.broadcast_to(rel_bias[b:b + 1], (i - start, N_HEADS_B)))
            start = i
    per_rel = jnp.concatenate(runs, axis=0).T

    def diagonals(lo, hi):
        left = jnp.full((N_HEADS_B, max(0, -BAND_HALF - lo)), NEG_BIG, F32)
        right = jnp.full((N_HEADS_B, max(0, hi - BAND_HALF)), NEG_BIG, F32)
        return jnp.concatenate([left, per_rel, right], axis=1)

    tile = _toeplitz(diagonals(-(t - 1) - BAND_HALF, w - 1 - BAND_HALF), t, w)
    twin = _toeplitz(diagonals(-(w - 1) + BAND_HALF, t - 1 + BAND_HALF), w, t)
    return tile, twin


def _dbias_reduce(dbias, buckets):
    n = len(dbias)

    def body(*refs):
        db_refs, bk_refs, o_ref = refs[:n], refs[n:2 * n], refs[2 * n]
        row = lax.broadcasted_iota(jnp.int32, (N_BUCKETS, 128), 0)
        lane = lax.broadcasted_iota(jnp.int32, (N_BUCKETS, 128), 1)

        def per_bucket(b, out):
            for pat in range(n):
                msk = bk_refs[pat][...] == b
                for h in range(N_HEADS_B):
                    tot = jnp.sum(jnp.where(msk, db_refs[pat][h], 0.0), axis=-1, keepdims=True)
                    tot = jnp.sum(tot, axis=0, keepdims=True)
                    out = out + jnp.where((row == b) & (lane == h), tot, 0.0)
            return out

        o_ref[...] = lax.fori_loop(0, N_BUCKETS, per_bucket, jnp.zeros((N_BUCKETS, 128), F32))

    return pl.pallas_call(
        body, name="dbias_reduce", out_shape=SDS((N_BUCKETS, 128), F32),
        compiler_params=pltpu.CompilerParams(vmem_limit_bytes=V7X_VMEM_LIMIT),
    )(*dbias, *buckets)


def _attn_out_fwd(ya, ob, lb, x, g_a, g_b, w_out, g_post, tm):
    s_len = ya.shape[0]

    def body(ya_ref, o0, o1, o2, l0, l1, l2, x_ref, ga_ref, gb_ref, w_ref, gp_ref,
             h1_ref, yo_ref, yb_ref, lse_ref, ycat_ref):
        m = jnp.maximum(jnp.maximum(l0[...], l1[...]), l2[...])
        w0, w1, w2 = jnp.exp(l0[...] - m), jnp.exp(l1[...] - m), jnp.exp(l2[...] - m)
        wsum = w0 + w1 + w2
        yb = (w0 * o0[...] + w1 * o1[...] + w2 * o2[...]) / wsum
        yb_ref[...] = yb
        lse_ref[...] = m + jnp.log(wsum)
        yan, _ = _rms_fwd(ya_ref[...], ga_ref[...])
        ybn, _ = _rms_fwd(yb, gb_ref[...])
        yan, ybn = yan.astype(BF16), ybn.astype(BF16)
        ycat_ref[:, :D_A] = yan
        ycat_ref[:, D_A:] = ybn
        yo = _dot(yan, w_ref[:D_A, :]) + _dot(ybn, w_ref[D_A:, :])
        yo_ref[...] = yo
        post, _ = _rms_fwd(yo, gp_ref[...])
        h1_ref[...] = x_ref[...] + post

    half, full = _row_spec(tm, D_A), _row_spec(tm, D_MODEL)
    return pl.pallas_call(
        body, name="attn_out_fwd", grid=(s_len // tm,),
        in_specs=[half] * 7 + [full, _full_spec((1, D_A)), _full_spec((1, D_B)), _full_spec((D_MODEL, D_MODEL)),
                               _full_spec((1, D_MODEL))],
        out_specs=[full, full, half, half, full],
        out_shape=(SDS((s_len, D_MODEL), F32), SDS((s_len, D_MODEL), F32), SDS((s_len, D_B), F32),
                   SDS((s_len, D_B), F32), SDS((s_len, D_MODEL), BF16)),
        compiler_params=_params(1, V7X_VMEM_LIMIT),
    )(ya, ob[0], ob[1], ob[2], lb[0], lb[1], lb[2], x, g_a, g_b, w_out, g_post)


def _attn_out_bwd(dh1, yo, ya, yb, g_a, g_b, w_out, g_post, tm):
    s_len = ya.shape[0]

    def body(dh1_ref, yo_ref, ya_ref, yb_ref, ga_ref, gb_ref, w_ref, gp_ref,
             dyo_ref, dya_ref, dyb_ref, dela_ref, delb_ref, dgp_ref, dga_ref, dgb_ref):
        first = pl.program_id(0) == 0
        dyo, dgp = _rms_bwd(yo_ref[...], gp_ref[...], dh1_ref[...])
        dyo = dyo.astype(BF16)
        dyo_ref[...] = dyo
        _acc(dgp_ref, first, dgp)
        dya_n = _dot_nt(dyo, w_ref[:D_A, :])
        dyb_n = _dot_nt(dyo, w_ref[D_A:, :])
        ya, yb = ya_ref[...], yb_ref[...]
        dya, dga = _rms_bwd(ya, ga_ref[...], dya_n)
        dyb, dgb = _rms_bwd(yb, gb_ref[...], dyb_n)
        _acc(dga_ref, first, dga)
        _acc(dgb_ref, first, dgb)
        dya_ref[...] = dya.astype(BF16)
        dyb_ref[...] = dyb.astype(BF16)
        dela_ref[...] = _head_sum(dya * ya)
        delb_ref[...] = _head_sum(dyb * yb)

    half, full = _row_spec(tm, D_A), _row_spec(tm, D_MODEL)
    return pl.pallas_call(
        body, name="attn_out_bwd", grid=(s_len // tm,),
        in_specs=[full, full, half, half, _full_spec((1, D_A)), _full_spec((1, D_B)),
                  _full_spec((D_MODEL, D_MODEL)), _full_spec((1, D_MODEL))],
        out_specs=[full, half, half, half, half, _full_spec((1, D_MODEL)), _full_spec((1, D_A)), _full_spec((1, D_B))],
        out_shape=(SDS((s_len, D_MODEL), BF16), SDS((s_len, D_A), BF16), SDS((s_len, D_B), BF16),
                   SDS((s_len, D_A), F32), SDS((s_len, D_B), F32), SDS((1, D_MODEL), F32), SDS((1, D_A), F32),
                   SDS((1, D_B), F32)),
        compiler_params=_params(1, V7X_VMEM_LIMIT),
    )(dh1, yo, ya, yb, g_a, g_b, w_out, g_post)


FF_CHUNK = 1024


def _mlp_fwd(h1, g_pre, w1, w2, g_post, tm):
    s_len = h1.shape[0]

    def body(h1_ref, gpre_ref, w1_ref, w2_ref, gpost_ref, h2_ref, fo_ref, xn_ref):
        h1v = h1_ref[...]
        xn, _ = _rms_fwd(h1v, gpre_ref[...])
        xn = xn.astype(BF16)
        xn_ref[...] = xn
        fo = jnp.zeros((tm, D_MODEL), F32)
        for c in range(D_FF // FF_CHUNK):
            cols = slice(c * FF_CHUNK, (c + 1) * FF_CHUNK)
            u = jnp.maximum(_dot(xn, w1_ref[:, cols]), 0.0)
            fo = fo + _dot((u * u).astype(BF16), w2_ref[cols, :])
        fo_ref[...] = fo
        post, _ = _rms_fwd(fo, gpost_ref[...])
        h2_ref[...] = h1v + post

    full = _row_spec(tm, D_MODEL)
    return pl.pallas_call(
        body, name="mlp_fwd", grid=(s_len // tm,),
        in_specs=[full, _full_spec((1, D_MODEL)), _const_spec((D_MODEL, D_FF)), _const_spec((D_FF, D_MODEL)),
                  _full_spec((1, D_MODEL))],
        out_specs=[full, full, full],
        out_shape=(SDS((s_len, D_MODEL), F32), SDS((s_len, D_MODEL), F32), SDS((s_len, D_MODEL), BF16)),
        compiler_params=_params(1, V7X_VMEM_LIMIT),
    )(h1, g_pre, w1, w2, g_post)


def _mlp_bwd(h1, dh2, fo, xn, g_pre, w1, w2, g_post, tm):
    s_len = h1.shape[0]

    def body(h1_ref, dh2_ref, fo_ref, xn_ref, gpre_ref, w1_ref, w2_ref, gpost_ref,
             dh1_ref, dfo_ref, du_ref, f_ref, dgpost_ref, dgpre_ref):
        first = pl.program_id(0) == 0
        dh2 = dh2_ref[...]
        dfo, dgpost = _rms_bwd(fo_ref[...], gpost_ref[...], dh2)
        dfo = dfo.astype(BF16)
        dfo_ref[...] = dfo
        _acc(dgpost_ref, first, dgpost)
        xn = xn_ref[...]
        dxn = jnp.zeros((tm, D_MODEL), F32)
        for c in range(D_FF // FF_CHUNK):
            cols = slice(c * FF_CHUNK, (c + 1) * FF_CHUNK)
            u = jnp.maximum(_dot(xn, w1_ref[:, cols]), 0.0)
            f_ref[:, cols] = (u * u).astype(BF16)
            du = (_dot_nt(dfo, w2_ref[cols, :]) * (2.0 * u)).astype(BF16)
            du_ref[:, cols] = du
            dxn = dxn + _dot_nt(du, w1_ref[:, cols])
        dx, dgpre = _rms_bwd(h1_ref[...], gpre_ref[...], dxn)
        _acc(dgpre_ref, first, dgpre)
        dh1_ref[...] = dh2 + dx

    full, wide = _row_spec(tm, D_MODEL), _row_spec(tm, D_FF)
    return pl.pallas_call(
        body, name="mlp_bwd", grid=(s_len // tm,),
        in_specs=[full, full, full, full, _full_spec((1, D_MODEL)), _const_spec((D_MODEL, D_FF)),
                  _const_spec((D_FF, D_MODEL)), _full_spec((1, D_MODEL))],
        out_specs=[full, full, wide, wide, _full_spec((1, D_MODEL)), _full_spec((1, D_MODEL))],
        out_shape=(SDS((s_len, D_MODEL), F32), SDS((s_len, D_MODEL), BF16), SDS((s_len, D_FF), BF16),
                   SDS((s_len, D_FF), BF16), SDS((1, D_MODEL), F32), SDS((1, D_MODEL), F32)),
        compiler_params=_params(1, V7X_VMEM_LIMIT),
    )(h1, dh2, fo, xn, g_pre, w1, w2, g_post)


def _ple_fwd_bwd(h2, p, target, g_ple, w_gate, w_proj, tm):
    s_len = h2.shape[0]

    def body(h2_ref, p_ref, t_ref, g_ref, wg_ref, wp_ref, dh2_ref, loss_ref, dg_ref, xn_ref, dgl_ref, dpp_ref, pb_ref):
        first = pl.program_id(0) == 0
        h2 = h2_ref[...]
        g = g_ref[...]
        xn, _ = _rms_fwd(h2, g)
        xn = xn.astype(BF16)
        xn_ref[...] = xn
        gate = 1.0 / (1.0 + jnp.exp(-_dot(xn, wg_ref[...])))
        pb = p_ref[...].astype(BF16)
        pb_ref[...] = pb
        pp = _dot(pb, wp_ref[...])
        diff = h2 + gate * pp - t_ref[...]
        _acc(loss_ref, first, jnp.full((8, 128), jnp.sum(diff * diff), F32))
        dh3 = diff * (1.0 / D_MODEL)
        dpp_ref[...] = (dh3 * gate).astype(BF16)
        dgl = (dh3 * pp * gate * (1.0 - gate)).astype(BF16)
        dgl_ref[...] = dgl
        dx, dg = _rms_bwd(h2, g, _dot_nt(dgl, wg_ref[...]))
        _acc(dg_ref, first, dg)
        dh2_ref[...] = dh3 + dx

    full = _row_spec(tm, D_MODEL)
    return pl.pallas_call(
        body, name="ple_fwd_bwd", grid=(s_len // tm,),
        in_specs=[full, _row_spec(tm, D_PLE), full, _full_spec((1, D_MODEL)), _full_spec((D_MODEL, D_MODEL)),
                  _full_spec((D_PLE, D_MODEL))],
        out_specs=[full, _full_spec((8, 128)), _full_spec((1, D_MODEL)), full, full, full, _row_spec(tm, D_PLE)],
        out_shape=(SDS((s_len, D_MODEL), F32), SDS((8, 128), F32), SDS((1, D_MODEL), F32), SDS((s_len, D_MODEL), BF16),
                   SDS((s_len, D_MODEL), BF16), SDS((s_len, D_MODEL), BF16), SDS((s_len, D_PLE), BF16)),
        compiler_params=_params(1, V7X_VMEM_LIMIT),
    )(h2, p, target, g_ple, w_gate, w_proj)


def _inproj_bwd(dqs, dkn, dva, dqb, dkb, dvb, qa_raw, ka_raw, x, dh1, g_pre, w_in, cos, sin, gq, gk, tm):
    s_len = x.shape[0]

    def body(dqs_ref, dkn_ref, dva_ref, dqb_ref, dkb_ref, dvb_ref, qa_ref, ka_ref, x_ref, dh1_ref, g_ref, w_ref,
             cos_ref, sin_ref, gq_ref, gk_ref, dx_ref, dproj_ref, dg_ref, dgq_ref, dgk_ref):
        first = pl.program_id(0) == 0
        dqa, dgq = _head_norm_rope_bwd(qa_ref[...], gq_ref[...], cos_ref[...], sin_ref[...], dqs_ref[...])
        dka, dgk = _head_norm_rope_bwd(ka_ref[...], gk_ref[...], cos_ref[:, :D_KV_A], sin_ref[:, :D_KV_A], dkn_ref[...])
        _acc(dgq_ref, first, dgq)
        _acc(dgk_ref, first, dgk)
        dproj_ref[:, OFF_QA:OFF_KA] = dqa.astype(BF16)
        dproj_ref[:, OFF_KA:OFF_VA] = dka.astype(BF16)
        dproj_ref[:, OFF_VA:OFF_QB] = dva_ref[...].astype(BF16)
        dproj_ref[:, OFF_QB:OFF_KB] = dqb_ref[...].astype(BF16)
        dproj_ref[:, OFF_KB:OFF_VB] = dkb_ref[...].astype(BF16)
        dproj_ref[:, OFF_VB:D_IN] = dvb_ref[...].astype(BF16)
        dxn = _dot_nt(dproj_ref[...], w_ref[...])
        dx, dg = _rms_bwd(x_ref[...], g_ref[...], dxn)
        _acc(dg_ref, first, dg)
        dx_ref[...] = dh1_ref[...] + dx

    half, kvw, full = _row_spec(tm, D_A), _row_spec(tm, D_KV_A), _row_spec(tm, D_MODEL)
    return pl.pallas_call(
        body, name="inproj_bwd", grid=(s_len // tm,),
        in_specs=[half, kvw, kvw, half, half, half, half, kvw, full, full, _full_spec((1, D_MODEL)),
                  _const_spec((D_MODEL, D_IN)), half, half, _full_spec((1, D_A)), _full_spec((1, D_KV_A))],
        out_specs=[full, _row_spec(tm, D_IN), _full_spec((1, D_MODEL)), _full_spec((1, D_A)), _full_spec((1, D_KV_A))],
        out_shape=(SDS((s_len, D_MODEL), F32), SDS((s_len, D_IN), BF16), SDS((1, D_MODEL), F32), SDS((1, D_A), F32),
                   SDS((1, D_KV_A), F32)),
        compiler_params=_params(1, V7X_VMEM_LIMIT),
    )(dqs, dkn, dva, dqb, dkb, dvb, qa_raw, ka_raw, x, dh1, g_pre, w_in, cos, sin, gq, gk)


def _weight_grad(a, b, name, tk1, tn, tm):
    s_len, k1 = a.shape
    n = b.shape[1]
    steps = s_len // tm

    def body(a_ref, b_ref, o_ref, acc):
        r = pl.program_id(2)
        _acc(acc, r == 0, _dot_tn(a_ref[...], b_ref[...]))

        @pl.when(r == steps - 1)
        def _():
            o_ref[...] = acc[...].astype(BF16)

    return pl.pallas_call(
        body, name=name, grid=(k1 // tk1, n // tn, steps),
        in_specs=[pl.BlockSpec((tm, tk1), lambda i, j, r: (r, i)), pl.BlockSpec((tm, tn), lambda i, j, r: (r, j))],
        out_specs=pl.BlockSpec((tk1, tn), lambda i, j, r: (i, j)),
        out_shape=SDS((k1, n), BF16),
        scratch_shapes=[pltpu.VMEM((tk1, tn), F32)],
        compiler_params=_params(3, V7X_VMEM_LIMIT),
    )(a, b)


def _my_index():
    return 4 * lax.axis_index("x") + 2 * lax.axis_index("y") + lax.axis_index("c")


def _peer(k):
    f = k + 1
    x, y, c = lax.axis_index("x"), lax.axis_index("y"), lax.axis_index("c")
    return (x ^ ((f >> 2) & 1), y ^ ((f >> 1) & 1), c ^ (f & 1))


def _all_gather(blocks, name):
    n = len(blocks)

    def body(*refs):
        ins, outs = refs[:n], refs[n:2 * n]
        send_sems, recv_sems, local_sems = refs[2 * n:]
        me = _my_index()
        copies = []
        for a in range(n):
            own = pltpu.make_async_copy(ins[a], outs[a].at[me], local_sems.at[a])
            own.start()
            copies.append(own)
        remote = []
        for a in range(n):
            for k in range(N_DEV - 1):
                cp = pltpu.make_async_remote_copy(
                    src_ref=ins[a], dst_ref=outs[a].at[me], send_sem=send_sems.at[a, k], recv_sem=recv_sems.at[a, k],
                    device_id=_peer(k), device_id_type=pl.DeviceIdType.MESH)
                cp.start()
                remote.append(cp)
        for cp in copies:
            cp.wait()
        for cp in remote:
            cp.wait_send()
        for cp in remote:
            cp.wait_recv()

    any_spec = pl.BlockSpec(memory_space=pl.ANY)
    return pl.pallas_call(
        body, name=name,
        in_specs=[any_spec] * n, out_specs=[any_spec] * n,
        out_shape=[SDS((N_DEV,) + b.shape, b.dtype) for b in blocks],
        scratch_shapes=[pltpu.SemaphoreType.DMA((n, N_DEV - 1)), pltpu.SemaphoreType.DMA((n, N_DEV - 1)),
                        pltpu.SemaphoreType.DMA((n,))],
    )(*blocks)


def _all_to_all(parts, name):
    n = len(parts)

    def body(*refs):
        ins, outs = refs[:n], refs[n:2 * n]
        send_sems, recv_sems, local_sems = refs[2 * n:]
        me = _my_index()
        copies = []
        for a in range(n):
            own = pltpu.make_async_copy(ins[a].at[me], outs[a].at[me], local_sems.at[a])
            own.start()
            copies.append(own)
        remote = []
        for a in range(n):
            for k in range(N_DEV - 1):
                px, py, pc = _peer(k)
                cp = pltpu.make_async_remote_copy(
                    src_ref=ins[a].at[4 * px + 2 * py + pc], dst_ref=outs[a].at[me],
                    send_sem=send_sems.at[a, k], recv_sem=recv_sems.at[a, k],
                    device_id=(px, py, pc), device_id_type=pl.DeviceIdType.MESH)
                cp.start()
                remote.append(cp)
        for cp in copies:
            cp.wait()
        for cp in remote:
            cp.wait_send()
        for cp in remote:
            cp.wait_recv()

    any_spec = pl.BlockSpec(memory_space=pl.ANY)
    return pl.pallas_call(
        body, name=name,
        in_specs=[any_spec] * n, out_specs=[any_spec] * n,
        out_shape=[SDS(p.shape, p.dtype) for p in parts],
        scratch_shapes=[pltpu.SemaphoreType.DMA((n, N_DEV - 1)), pltpu.SemaphoreType.DMA((n, N_DEV - 1)),
                        pltpu.SemaphoreType.DMA((n,))],
    )(*parts)


def _sum_adamw(parts, w, m, v, name, tr):
    rows, cols = w.shape
    c1 = 1.0 / (1.0 - ADAM_B1 ** ADAM_STEP)
    c2 = 1.0 / (1.0 - ADAM_B2 ** ADAM_STEP)

    def body(p_ref, w_ref, m_ref, v_ref, g_ref, d_ref, nm_ref, nv_ref):
        g = p_ref[0].astype(F32)
        for j in range(1, N_DEV):
            g = g + p_ref[j].astype(F32)
        g_ref[...] = g
        nm = ADAM_B1 * m_ref[...] + (1.0 - ADAM_B1) * g
        nv = ADAM_B2 * v_ref[...] + (1.0 - ADAM_B2) * (g * g)
        nm_ref[...] = nm
        nv_ref[...] = nv
        d_ref[...] = -ADAM_LR * ((nm * c1) / (jnp.sqrt(nv * c2) + ADAM_EPS) + ADAM_WD * w_ref[...])

    blk = pl.BlockSpec((tr, cols), lambda i: (i, 0))
    return pl.pallas_call(
        body, name=name, grid=(rows // tr,),
        in_specs=[pl.BlockSpec((N_DEV, tr, cols), lambda i: (0, i, 0)), blk, blk, blk],
        out_specs=[blk] * 4, out_shape=[SDS((rows, cols), F32)] * 4,
        compiler_params=_params(1, V7X_VMEM_LIMIT),
    )(parts, w, m, v)


_SMALL = (("g_attn_pre", 1024), ("g_q", 64), ("g_k", 64), ("g_out_a", 512), ("g_out_b", 512), ("g_attn_post", 1024),
          ("rel_bias", 256), ("g_mlp_pre", 1024), ("g_mlp_post", 1024), ("g_ple", 1024))
_SLAB_ROWS = 56


def _pack_small(vals):
    rows = []
    for (name, size) in _SMALL:
        flat = vals[name].reshape(-1).astype(F32)
        padded = -(-size // 128) * 128
        rows.append(jnp.pad(flat, (0, padded - size)).reshape(padded // 128, 128))
    slab = jnp.concatenate(rows, axis=0)
    return jnp.pad(slab, ((0, _SLAB_ROWS - slab.shape[0]), (0, 0)))


def _unpack_small(slab, shapes):
    out, row = {}, 0
    for (name, size) in _SMALL:
        nrow = -(-size // 128)
        out[name] = slab[row:row + nrow].reshape(-1)[:size].reshape(shapes[name])
        row += nrow
    return out


def _rope_tables(s_len):
    rows = s_len // GRID_W
    row = jnp.broadcast_to(jnp.arange(rows)[:, None], (rows, GRID_W)).reshape(-1).astype(F32)
    col = jnp.broadcast_to(jnp.arange(GRID_W)[None, :], (rows, GRID_W)).reshape(-1).astype(F32)
    n_axis = ROPE_HALF // 2
    inv_freq = ROPE_THETA ** (-jnp.arange(n_axis, dtype=F32) / n_axis)
    ang = jnp.concatenate([row[:, None] * inv_freq, col[:, None] * inv_freq], axis=-1)
    cos, sin = jnp.cos(ang), jnp.sin(ang)
    cos = jnp.tile(jnp.concatenate([cos, cos], axis=-1), (1, N_HEADS_A))
    sin = jnp.tile(jnp.concatenate([-sin, sin], axis=-1), (1, N_HEADS_A))
    return cos, sin


def _local_step(x, p, target, w_in, w_out, w_ff1, w_ff2, w_gate, w_proj, small):
    s_len = x.shape[0]
    tm = min(256, s_len)
    tq = min(128, s_len)
    tk = min(2048, s_len // 2)
    cos, sin = _rope_tables(s_len)
    gq = jnp.tile(small["g_q"], (1, N_HEADS_A))
    gk = jnp.tile(small["g_k"], (1, N_KV_A))

    qa_raw, ka_raw, qs, kn, va, qb, kb, vb, xn1 = _inproj_fwd(x, small["g_attn_pre"], w_in, cos, sin, gq, gk, tm)

    def kv_major(a):
        return a.reshape(s_len, N_KV_A, HEAD_DIM).transpose(1, 0, 2)

    def kv_chunks_t(a):
        return a.reshape(s_len // tk, tk, N_KV_A, HEAD_DIM).transpose(2, 0, 3, 1)

    def kv_unchunk(a):
        return a.transpose(1, 3, 0, 2).reshape(s_len, D_KV_A)

    k_maj, kt, vt = kv_major(kn), kv_chunks_t(kn), kv_chunks_t(va)
    vt_ones = jnp.concatenate([vt, jnp.ones((N_KV_A, s_len // tk, 16, tk), BF16)], axis=2)
    ya, lse_a = _attn_fwd(qs, k_maj, vt_ones, tq, tk)

    ob, lb, tiles = [], [], []
    for (_, dil) in DILATED_PATTERNS:
        t = min(256, s_len // dil)
        bias, bias_t = _bias_tiles(small["rel_bias"], dil, t)
        tiles.append((t, bias, bias_t, _band_buckets(dil, t)))
        o, l = _band_fwd(qb, kb, vb, bias, dil, t)
        ob.append(o)
        lb.append(l)

    h1, yo, yb, lse_b, ycat = _attn_out_fwd(ya, ob, lb, x, small["g_out_a"], small["g_out_b"], w_out,
                                            small["g_attn_post"], tm)
    h2, fo, xn2 = _mlp_fwd(h1, small["g_mlp_pre"], w_ff1, w_ff2, small["g_mlp_post"], tm)
    dh2, loss_part, dg_ple, xn3, dgl, dpp, pb = _ple_fwd_bwd(h2, p, target, small["g_ple"], w_gate, w_proj, tm)
    dh1, dfo, du, f, dg_mlp_post, dg_mlp_pre = _mlp_bwd(h1, dh2, fo, xn2, small["g_mlp_pre"], w_ff1, w_ff2,
                                                          small["g_mlp_post"], tm)
    dyo, dya, dyb, delta_a, delta_b, dg_attn_post, dg_out_a, dg_out_b = _attn_out_bwd(
        dh1, yo, ya, yb, small["g_out_a"], small["g_out_b"], w_out, small["g_attn_post"], tm)

    dqs, dk_t, dv_t = _attn_bwd(qs, dya, lse_a, delta_a, kt, k_maj, vt, tq, tk)
    dkn, dva = kv_unchunk(dk_t), kv_unchunk(dv_t)

    dqb = dkb = dvb = jnp.zeros((s_len, D_B), F32)
    dbias = []
    for (_, dil), (t, bias, bias_t, _) in zip(DILATED_PATTERNS, tiles):
        dqb, db = _band_dq(qb, dyb, lse_b, delta_b, kb, vb, bias, dqb, dil, t)
        dkb, dvb = _band_dkv(kb, vb, qb, dyb, lse_b, delta_b, bias_t, dkb, dvb, dil, t)
        dbias.append(db)
    d_rel = _dbias_reduce(dbias, [jnp.asarray(tl[3]) for tl in tiles])[:, :N_HEADS_B]

    dx, dproj, dg_attn_pre, dgq_lanes, dgk_lanes = _inproj_bwd(
        dqs, dkn, dva, dqb, dkb, dvb, qa_raw, ka_raw, x, dh1, small["g_attn_pre"], w_in, cos, sin, gq, gk, tm)

    tg = min(512, s_len)
    grads = {
        "w_in": _weight_grad(xn1, dproj, "grad_w_in", D_MODEL, 768, tg),
        "w_out": _weight_grad(ycat, dyo, "grad_w_out", D_MODEL, D_MODEL, tg),
        "w_ff1": _weight_grad(xn2, du, "grad_w_ff1", D_MODEL, 1024, tg),
        "w_ff2": _weight_grad(f, dfo, "grad_w_ff2", 1024, D_MODEL, tg),
        "w_ple_gate": _weight_grad(xn3, dgl, "grad_w_ple_gate", D_MODEL, D_MODEL, tg),
        "w_ple_proj": _weight_grad(pb, dpp, "grad_w_ple_proj", D_PLE, D_MODEL, tg),
    }
    small_grads = {
        "g_attn_pre": dg_attn_pre, "g_q": dgq_lanes.reshape(N_HEADS_A, HEAD_DIM).sum(0, keepdims=True),
        "g_k": dgk_lanes.reshape(N_KV_A, HEAD_DIM).sum(0, keepdims=True), "g_out_a": dg_out_a, "g_out_b": dg_out_b,
        "g_attn_post": dg_attn_post, "rel_bias": d_rel, "g_mlp_pre": dg_mlp_pre, "g_mlp_post": dg_mlp_post,
        "g_ple": dg_ple,
    }
    return loss_part[0, 0], dx, grads, small_grads


_BIG = (("w_in", 1), ("w_out", 0), ("w_ff1", 1), ("w_ff2", 0), ("w_ple_gate", 0), ("w_ple_proj", 1))


def _assemble(gathered, axis):
    if axis == 0:
        return gathered.reshape(-1, gathered.shape[2])
    return gathered.transpose(1, 0, 2).reshape(gathered.shape[1], -1)


def _cut(full, axis):
    if axis == 0:
        return full.reshape(N_DEV, full.shape[0] // N_DEV, full.shape[1])
    return full.reshape(full.shape[0], N_DEV, full.shape[1] // N_DEV).transpose(1, 0, 2)


def kernel(x, p, w_in, g_attn_pre, g_q, g_k, g_out_a, g_out_b, w_out, g_attn_post, rel_bias, g_mlp_pre, w_ff1, w_ff2, g_mlp_post, g_ple, w_ple_gate, w_ple_proj, loss_target, m_w_in, m_g_attn_pre, m_g_q, m_g_k, m_g_out_a, m_g_out_b, m_w_out, m_g_attn_post, m_rel_bias, m_g_mlp_pre, m_w_ff1, m_w_ff2, m_g_mlp_post, m_g_ple, m_w_ple_gate, m_w_ple_proj, v_w_in, v_g_attn_pre, v_g_q, v_g_k, v_g_out_a, v_g_out_b, v_w_out, v_g_attn_post, v_rel_bias, v_g_mlp_pre, v_w_ff1, v_w_ff2, v_g_mlp_post, v_g_ple, v_w_ple_gate, v_w_ple_proj):
    given = dict(locals())
    small_names = [n for n, _ in _SMALL]
    small = {n: given[n] for n in small_names}
    shards = {n: given[n][0] for n, _ in _BIG}

    gathered = _all_gather([shards[n].astype(BF16) for n, _ in _BIG], "gather_weights")
    full = {n: _assemble(g, axis) for (n, axis), g in zip(_BIG, gathered)}

    loss_part, dx, grads, small_grads = _local_step(
        x[0], p[0, 0], loss_target[0], full["w_in"], full["w_out"], full["w_ff1"], full["w_ff2"], full["w_ple_gate"],
        full["w_ple_proj"], small)

    parts = [_cut(grads[n], axis) for n, axis in _BIG]
    slab = _pack_small(small_grads)
    slab_parts = jnp.broadcast_to(slab[None], (N_DEV,) + slab.shape)
    received = _all_to_all(parts + [slab_parts], "exchange_grads")

    out_g, out_d, out_m, out_v = {}, {}, {}, {}
    for (n, _), rec in zip(_BIG, received[:-1]):
        rows = shards[n].shape[0]
        g, d, nm, nv = _sum_adamw(rec, shards[n], given["m_" + n][0], given["v_" + n][0], "adamw_" + n, min(rows, 128))
        out_g[n], out_d[n], out_m[n], out_v[n] = g[None], d[None], nm[None], nv[None]
    g, d, nm, nv = _sum_adamw(received[-1], _pack_small(small), _pack_small({n: given["m_" + n] for n in small_names}),
                              _pack_small({n: given["v_" + n] for n in small_names}), "adamw_small", _SLAB_ROWS)
    shapes = {n: given[n].shape for n in small_names}
    for dst, slab_out in ((out_g, g), (out_d, d), (out_m, nm), (out_v, nv)):
        dst.update(_unpack_small(slab_out, shapes))

    loss = lax.psum(loss_part * (0.5 / D_MODEL), MESH_AXES)
    order = ["w_in", "g_attn_pre", "g_q", "g_k", "g_out_a", "g_out_b", "w_out", "g_attn_post", "rel_bias", "g_mlp_pre",
             "w_ff1", "w_ff2", "g_mlp_post", "g_ple", "w_ple_gate", "w_ple_proj"]
    return (loss, dx[None], *[out_g[n] for n in order], *[out_d[n] for n in order], *[out_m[n] for n in order],
            *[out_v[n] for n in order])
```

```python
import functools
import math

import jax
import jax.numpy as jnp
import numpy as np
from jax import lax
from jax.experimental import pallas as pl
from jax.experimental.pallas import tpu as pltpu

F32 = jnp.float32
BF16 = jnp.bfloat16
SDS = jax.ShapeDtypeStruct

D_MODEL = 1024
HEAD_DIM = 64
N_HEADS_A = 8
N_KV_A = 2
GROUP_A = N_HEADS_A // N_KV_A
N_HEADS_B = 8
D_A = N_HEADS_A * HEAD_DIM
D_KV_A = N_KV_A * HEAD_DIM
D_B = N_HEADS_B * HEAD_DIM
D_IN = D_A + 2 * D_KV_A + 3 * D_B
D_FF = 4 * D_MODEL
D_PLE = 256
GRID_W = 64
ROPE_THETA = 10000.0
ROPE_HALF = HEAD_DIM // 2
DILATED_PATTERNS = ((128, 1), (512, 4), (2048, 16))
BAND_HALF = 64
N_BUCKETS = 32
MAX_DISTANCE = 1024
EPS = 1e-6
NEG_BIG = -1e30
SCORE_SCALE = HEAD_DIM ** -0.5

ADAM_LR = 0.001
ADAM_B1 = 0.9
ADAM_B2 = 0.999
ADAM_EPS = 1e-08
ADAM_WD = 0.01
ADAM_STEP = 10

N_DEV = 8
MESH_AXES = ("x", "y", "c")
V7X_VMEM_LIMIT = 56 * 1024 * 1024

OFF_QA, OFF_KA, OFF_VA, OFF_QB, OFF_KB, OFF_VB = 0, 512, 640, 768, 1280, 1792


def _params(n_axes, vmem=None):
    return pltpu.CompilerParams(dimension_semantics=("arbitrary",) * n_axes, vmem_limit_bytes=vmem)


def _dot(a, b):
    return jnp.dot(a, b, preferred_element_type=F32)


def _dot_nt(a, b):
    return lax.dot_general(a, b, (((1,), (1,)), ((), ())), preferred_element_type=F32)


def _dot_tn(a, b):
    return lax.dot_general(a, b, (((0,), (0,)), ((), ())), preferred_element_type=F32)


def _rms_fwd(x, g):
    r = lax.rsqrt(jnp.mean(x * x, axis=-1, keepdims=True) + EPS)
    return x * r * g, r


def _rms_bwd(x, g, dy):
    r = lax.rsqrt(jnp.mean(x * x, axis=-1, keepdims=True) + EPS)
    xh = x * r
    dxh = dy * g
    dx = r * (dxh - xh * jnp.mean(dxh * xh, axis=-1, keepdims=True))
    return dx, jnp.sum(dy * xh, axis=0, keepdims=True)


def _head_sum(v):
    head = lax.broadcasted_iota(jnp.int32, v.shape, 1) >> 6
    out = jnp.zeros_like(v)
    for h in range(v.shape[1] // HEAD_DIM):
        msk = head == h
        s = jnp.sum(jnp.where(msk, v, 0.0), axis=-1, keepdims=True)
        out = jnp.where(msk, s, out)
    return out


def _swap_halves(v):
    w = v.shape[1]
    lane = lax.broadcasted_iota(jnp.int32, v.shape, 1)
    first_half = (lane & (HEAD_DIM - 1)) < ROPE_HALF
    return jnp.where(first_half, pltpu.roll(v, w - ROPE_HALF, 1), pltpu.roll(v, ROPE_HALF, 1))


def _head_norm_rope(v, g, cos, sin_signed):
    r = lax.rsqrt(_head_sum(v * v) * (1.0 / HEAD_DIM) + EPS)
    y = v * r * g
    return y * cos + _swap_halves(y) * sin_signed


def _head_norm_rope_bwd(v, g, cos, sin_signed, dout):
    dy = dout * cos - _swap_halves(dout) * sin_signed
    r = lax.rsqrt(_head_sum(v * v) * (1.0 / HEAD_DIM) + EPS)
    xh = v * r
    dxh = dy * g
    dv = r * (dxh - xh * (_head_sum(dxh * xh) * (1.0 / HEAD_DIM)))
    return dv, jnp.sum(dy * xh, axis=0, keepdims=True)


def _row_spec(tm, n):
    return pl.BlockSpec((tm, n), lambda i: (i, 0))


def _full_spec(shape):
    nd = len(shape)
    return pl.BlockSpec(shape, lambda *_: (0,) * nd)


def _const_spec(shape):
    nd = len(shape)
    return pl.BlockSpec(shape, lambda *_: (0,) * nd, pipeline_mode=pl.Buffered(1))


def _zero_at_first(first, *refs):
    @pl.when(first)
    def _():
        for ref in refs:
            ref[...] = jnp.zeros_like(ref)


def _inproj_fwd(x, g_pre, w_in, cos, sin, gq, gk, tm):
    s_len = x.shape[0]

    def body(x_ref, g_ref, w_ref, cos_ref, sin_ref, gq_ref, gk_ref,
             qa_raw, ka_raw, qs, kn, va, qb, kb, vb, xn_out):
        xn, _ = _rms_fwd(x_ref[...], g_ref[...])
        xn = xn.astype(BF16)
        xn_out[...] = xn
        qa = _dot(xn, w_ref[:, OFF_QA:OFF_KA])
        qa_raw[...] = qa
        qs[...] = (_head_norm_rope(qa, gq_ref[...], cos_ref[...], sin_ref[...]) * SCORE_SCALE).astype(BF16)
        ka = _dot(xn, w_ref[:, OFF_KA:OFF_VA])
        ka_raw[...] = ka
        kn[...] = _head_norm_rope(ka, gk_ref[...], cos_ref[:, :D_KV_A], sin_ref[:, :D_KV_A]).astype(BF16)
        va[...] = _dot(xn, w_ref[:, OFF_VA:OFF_QB]).astype(BF16)
        qb[...] = (_dot(xn, w_ref[:, OFF_QB:OFF_KB]) * SCORE_SCALE).astype(BF16)
        kb[...] = _dot(xn, w_ref[:, OFF_KB:OFF_VB]).astype(BF16)
        vb[...] = _dot(xn, w_ref[:, OFF_VB:D_IN]).astype(BF16)

    return pl.pallas_call(
        body, name="inproj_fwd", grid=(s_len // tm,),
        in_specs=[_row_spec(tm, D_MODEL), _full_spec((1, D_MODEL)), _const_spec((D_MODEL, D_IN)),
                  _row_spec(tm, D_A), _row_spec(tm, D_A), _full_spec((1, D_A)), _full_spec((1, D_KV_A))],
        out_specs=[_row_spec(tm, D_A), _row_spec(tm, D_KV_A), _row_spec(tm, D_A), _row_spec(tm, D_KV_A),
                   _row_spec(tm, D_KV_A), _row_spec(tm, D_B), _row_spec(tm, D_B), _row_spec(tm, D_B),
                   _row_spec(tm, D_MODEL)],
        out_shape=(SDS((s_len, D_A), F32), SDS((s_len, D_KV_A), F32), SDS((s_len, D_A), BF16),
                   SDS((s_len, D_KV_A), BF16), SDS((s_len, D_KV_A), BF16), SDS((s_len, D_B), BF16),
                   SDS((s_len, D_B), BF16), SDS((s_len, D_B), BF16), SDS((s_len, D_MODEL), BF16)),
        compiler_params=_params(1, V7X_VMEM_LIMIT),
    )(x, g_pre, w_in, cos, sin, gq, gk)


def _stack_heads(ref, tq):
    return jnp.concatenate([ref[:, HEAD_DIM * g:HEAD_DIM * (g + 1)] for g in range(GROUP_A)], axis=0)


def _stack_cols(ref, tq):
    return jnp.concatenate([ref[:, HEAD_DIM * g:HEAD_DIM * g + 1] for g in range(GROUP_A)], axis=0)


def _attn_fwd(qs, k, vt, tq, tk):
    s_len = qs.shape[0]
    nk = s_len // tk
    assert nk % 2 == 0
    gw = GROUP_A * HEAD_DIM
    rows = GROUP_A * tq
    vrows = vt.shape[2]

    def body(q_ref, k_ref, vt_ref, o_ref, lse_ref, s_buf):
        qt = _stack_heads(q_ref, tq).T

        def scores(j, slot):
            kj = k_ref[pl.ds(pl.multiple_of(j * tk, tk), tk), :]
            s_buf[slot] = _dot(kj, qt)

        def consume(j, slot, carry):
            m, acc = carry
            st = s_buf[slot]
            m_new = jnp.maximum(m, jnp.max(st, axis=0, keepdims=True))
            pt = jnp.exp(st - m_new)
            acc = jnp.exp(m - m_new) * acc + _dot(vt_ref[j], pt.astype(BF16))
            return m_new, acc

        scores(0, 0)

        def pair(j, carry, more):
            scores(j + 1, 1)
            carry = consume(j, 0, carry)
            if more:
                scores(j + 2, 0)
            return consume(j + 1, 1, carry)

        carry = (jnp.full((1, rows), NEG_BIG, F32), jnp.zeros((vrows, rows), F32))
        carry = lax.fori_loop(0, nk // 2 - 1, lambda jj, c: pair(2 * jj, c, True), carry)
        m, acc = pair(nk - 2, carry, False)
        l = acc[HEAD_DIM:HEAD_DIM + 1]
        o = (acc[:HEAD_DIM] / l).T
        lse = jnp.broadcast_to(m + jnp.log(l), (HEAD_DIM, rows)).T
        for g in range(GROUP_A):
            o_ref[:, HEAD_DIM * g:HEAD_DIM * (g + 1)] = o[g * tq:(g + 1) * tq]
            lse_ref[:, HEAD_DIM * g:HEAD_DIM * (g + 1)] = lse[g * tq:(g + 1) * tq]

    tile = pl.BlockSpec((tq, gw), lambda kv, i: (i, kv))
    return pl.pallas_call(
        body, name="attn_fwd", grid=(N_KV_A, s_len // tq),
        in_specs=[tile, pl.BlockSpec((None, s_len, HEAD_DIM), lambda kv, i: (kv, 0, 0)),
                  pl.BlockSpec((None, nk, vrows, tk), lambda kv, i: (kv, 0, 0, 0))],
        out_specs=[tile, tile],
        out_shape=(SDS((s_len, D_A), F32), SDS((s_len, D_A), F32)),
        scratch_shapes=[pltpu.VMEM((2, tk, rows), F32)],
        compiler_params=_params(2, V7X_VMEM_LIMIT),
    )(qs, k, vt)


def _attn_bwd(qs, do, lse, delta, kt, k, vt, tq, tk):
    s_len = qs.shape[0]
    nk = s_len // tk
    assert nk % 2 == 0
    nq = s_len // tq
    gw = GROUP_A * HEAD_DIM
    rows = GROUP_A * tq

    def body(q_ref, do_ref, lse_ref, delta_ref, kt_ref, k_ref, vt_ref, dq_ref, dk_hbm, dv_hbm,
             dk_acc, dv_acc, s_buf, dp_buf):
        kv = pl.program_id(0)
        i = pl.program_id(1)

        @pl.when(i == 0)
        def _():
            dk_acc[...] = jnp.zeros_like(dk_acc)
            dv_acc[...] = jnp.zeros_like(dv_acc)

        q = _stack_heads(q_ref, tq)
        dout = _stack_heads(do_ref, tq)
        qt = q.T
        doutt = dout.T
        row_lse = _stack_cols(lse_ref, tq)
        row_delta = _stack_cols(delta_ref, tq)

        def scores(j, slot):
            s_buf[slot] = _dot(q, kt_ref[j])
            dp_buf[slot] = _dot(dout, vt_ref[j])

        def consume(j, slot, dq):
            p = jnp.exp(s_buf[slot] - row_lse)
            ds = (p * (dp_buf[slot] - row_delta)).astype(BF16)
            dv_acc[j] += _dot(doutt, p.astype(BF16))
            dk_acc[j] += _dot(qt, ds)
            return dq + _dot(ds, k_ref[pl.ds(pl.multiple_of(j * tk, tk), tk), :])

        scores(0, 0)

        def pair(j, dq, more):
            scores(j + 1, 1)
            dq = consume(j, 0, dq)
            if more:
                scores(j + 2, 0)
            return consume(j + 1, 1, dq)

        dq = lax.fori_loop(0, nk // 2 - 1, lambda jj, c: pair(2 * jj, c, True), jnp.zeros((rows, HEAD_DIM), F32))
        dq = pair(nk - 2, dq, False) * SCORE_SCALE
        for g in range(GROUP_A):
            dq_ref[:, HEAD_DIM * g:HEAD_DIM * (g + 1)] = dq[g * tq:(g + 1) * tq]

        @pl.when(i == nq - 1)
        def _():
            pltpu.sync_copy(dk_acc, dk_hbm.at[kv])
            pltpu.sync_copy(dv_acc, dv_hbm.at[kv])

    tile = pl.BlockSpec((tq, gw), lambda kv, i: (i, kv))
    chunks = pl.BlockSpec((None, nk, HEAD_DIM, tk), lambda kv, i: (kv, 0, 0, 0))
    grad_t = SDS((N_KV_A, nk, HEAD_DIM, tk), F32)
    return pl.pallas_call(
        body, name="attn_bwd", grid=(N_KV_A, nq),
        in_specs=[tile, tile, tile, tile, chunks,
                  pl.BlockSpec((None, s_len, HEAD_DIM), lambda kv, i: (kv, 0, 0)), chunks],
        out_specs=[tile, pl.BlockSpec(memory_space=pl.ANY), pl.BlockSpec(memory_space=pl.ANY)],
        out_shape=(SDS((s_len, D_A), F32), grad_t, grad_t),
        scratch_shapes=[pltpu.VMEM((nk, HEAD_DIM, tk), F32), pltpu.VMEM((nk, HEAD_DIM, tk), F32),
                        pltpu.VMEM((2, rows, tk), F32), pltpu.VMEM((2, rows, tk), F32)],
        compiler_params=_params(2, V7X_VMEM_LIMIT),
    )(qs, do, lse, delta, kt, k, vt)


def _band_specs(length, t):
    hb = t // BAND_HALF
    last = length // BAND_HALF - 1
    main = pl.BlockSpec((t, D_B), lambda r, i: (i, r))
    prev = pl.BlockSpec((BAND_HALF, D_B), lambda r, i: (jnp.maximum(i * hb - 1, 0), r))
    nxt = pl.BlockSpec((BAND_HALF, D_B), lambda r, i: (jnp.minimum((i + 1) * hb, last), r))
    return main, [prev, main, nxt]


def _window(refs):
    return jnp.concatenate([r[...] for r in refs], axis=0)


def _head(v, h):
    return v[:, HEAD_DIM * h:HEAD_DIM * (h + 1)]


def _band_fwd(q, k, v, bias, dil, t):
    s_len = q.shape[0]
    length = s_len // dil
    w = t + 2 * BAND_HALF
    view = lambda a: a.reshape(length, dil * D_B)
    main, win = _band_specs(length, t)

    def body(q_ref, k0, k1, k2, v0, v1, v2, bias_ref, o_ref, lse_ref, s_buf, kt_buf):
        i = pl.program_id(1)
        kt_buf[...] = _window((k0, k1, k2)).T
        vw = _window((v0, v1, v2))
        pos = i * t - BAND_HALF + lax.broadcasted_iota(jnp.int32, (1, w), 1)
        valid = (pos >= 0) & (pos < length)
        for h in range(N_HEADS_B):
            s_buf[h] = _dot(q_ref[:, HEAD_DIM * h:HEAD_DIM * (h + 1)], kt_buf[HEAD_DIM * h:HEAD_DIM * (h + 1), :])
        for h in range(N_HEADS_B):
            s = jnp.where(valid, s_buf[h] + bias_ref[h], NEG_BIG)
            m = jnp.max(s, axis=-1, keepdims=True)
            e = jnp.exp(s - m)
            den = jnp.sum(e, axis=-1, keepdims=True)
            o_ref[:, HEAD_DIM * h:HEAD_DIM * (h + 1)] = _dot(e.astype(BF16), _head(vw, h)) / den
            lse_ref[:, HEAD_DIM * h:HEAD_DIM * (h + 1)] = jnp.broadcast_to(m + jnp.log(den), (t, HEAD_DIM))

    o, lse = pl.pallas_call(
        body, name=f"band_fwd_d{dil}", grid=(dil, length // t),
        in_specs=[main] + win + win + [_full_spec((N_HEADS_B, t, w))],
        out_specs=[main, main],
        out_shape=(SDS((length, dil * D_B), F32), SDS((length, dil * D_B), F32)),
        scratch_shapes=[pltpu.VMEM((N_HEADS_B, t, w), F32), pltpu.VMEM((D_B, w), BF16)],
        compiler_params=_params(2, V7X_VMEM_LIMIT),
    )(view(q), view(k), view(k), view(k), view(v), view(v), view(v), bias)
    return o.reshape(s_len, D_B), lse.reshape(s_len, D_B)


def _band_dq(q, do, lse, delta, k, v, bias, dq_in, dil, t):
    s_len = q.shape[0]
    length = s_len // dil
    w = t + 2 * BAND_HALF
    view = lambda a: a.reshape(length, dil * D_B)
    main, win = _band_specs(length, t)

    def body(q_ref, do_ref, lse_ref, delta_ref, k0, k1, k2, v0, v1, v2, bias_ref, acc_ref, dq_ref, dbias_ref,
             s_buf, dp_buf, kt_buf, vt_buf):
        i = pl.program_id(1)
        @pl.when((pl.program_id(0) == 0) & (i == 0))
        def _():
            dbias_ref[...] = jnp.zeros_like(dbias_ref)

        kw = _window((k0, k1, k2))
        kt_buf[...] = kw.T
        vt_buf[...] = _window((v0, v1, v2)).T
        pos = i * t - BAND_HALF + lax.broadcasted_iota(jnp.int32, (1, w), 1)
        valid = (pos >= 0) & (pos < length)
        for h in range(N_HEADS_B):
            cols = slice(HEAD_DIM * h, HEAD_DIM * (h + 1))
            s_buf[h] = _dot(q_ref[:, cols], kt_buf[cols, :])
            dp_buf[h] = _dot(do_ref[:, cols], vt_buf[cols, :])
        for h in range(N_HEADS_B):
            cols = slice(HEAD_DIM * h, HEAD_DIM * (h + 1))
            s = jnp.where(valid, s_buf[h] + bias_ref[h], NEG_BIG)
            p = jnp.exp(s - lse_ref[:, HEAD_DIM * h:HEAD_DIM * h + 1])
            ds = p * (dp_buf[h] - delta_ref[:, HEAD_DIM * h:HEAD_DIM * h + 1])
            dq_ref[:, cols] = acc_ref[:, cols] + _dot(ds.astype(BF16), _head(kw, h)) * SCORE_SCALE
            dbias_ref[h] += ds

    dq, dbias = pl.pallas_call(
        body, name=f"band_dq_d{dil}", grid=(dil, length // t),
        in_specs=[main, main, main, main] + win + win + [_full_spec((N_HEADS_B, t, w)), main],
        out_specs=[main, _full_spec((N_HEADS_B, t, w))],
        out_shape=(SDS((length, dil * D_B), F32), SDS((N_HEADS_B, t, w), F32)),
        scratch_shapes=[pltpu.VMEM((N_HEADS_B, t, w), F32), pltpu.VMEM((N_HEADS_B, t, w), F32),
                        pltpu.VMEM((D_B, w), BF16), pltpu.VMEM((D_B, w), BF16)],
        compiler_params=_params(2, V7X_VMEM_LIMIT),
    )(view(q), view(do), view(lse), view(delta), view(k), view(k), view(k), view(v), view(v), view(v), bias,
      view(dq_in))
    return dq.reshape(s_len, D_B), dbias


def _band_dkv(k, v, q, do, lse, delta, bias_t, dk_in, dv_in, dil, t):
    s_len = q.shape[0]
    length = s_len // dil
    w = t + 2 * BAND_HALF
    view = lambda a: a.reshape(length, dil * D_B)
    main, win = _band_specs(length, t)

    def body(k_ref, v_ref, q0, q1, q2, d0, d1, d2, l0, l1, l2, e0, e1, e2, bias_ref, dk_acc, dv_acc, dk_ref, dv_ref,
             s_buf, dp_buf, kt_buf, vt_buf):
        i = pl.program_id(1)
        qw = _window((q0, q1, q2))
        dow = _window((d0, d1, d2))
        lsew = _window((l0, l1, l2))
        deltaw = _window((e0, e1, e2))
        pos = i * t - BAND_HALF + lax.broadcasted_iota(jnp.int32, (w, 1), 0)
        valid = (pos >= 0) & (pos < length)
        kt_buf[...] = k_ref[...].T
        vt_buf[...] = v_ref[...].T
        for h in range(N_HEADS_B):
            cols = slice(HEAD_DIM * h, HEAD_DIM * (h + 1))
            s_buf[h] = _dot(_head(qw, h), kt_buf[cols, :])
            dp_buf[h] = _dot(_head(dow, h), vt_buf[cols, :])
        qwt = qw.T
        dowt = dow.T
        dkt, dvt = [], []
        for h in range(N_HEADS_B):
            rows = slice(HEAD_DIM * h, HEAD_DIM * (h + 1))
            s = jnp.where(valid, s_buf[h] + bias_ref[h], NEG_BIG)
            p = jnp.exp(s - lsew[:, HEAD_DIM * h:HEAD_DIM * h + 1])
            ds = p * (dp_buf[h] - deltaw[:, HEAD_DIM * h:HEAD_DIM * h + 1])
            dvt.append(_dot(dowt[rows, :], p.astype(BF16)))
            dkt.append(_dot(qwt[rows, :], ds.astype(BF16)))
        dv_ref[...] = dv_acc[...] + jnp.concatenate(dvt, axis=0).T
        dk_ref[...] = dk_acc[...] + jnp.concatenate(dkt, axis=0).T

    dk, dv = pl.pallas_call(
        body, name=f"band_dkv_d{dil}", grid=(dil, length // t),
        in_specs=[main, main] + win + win + win + win + [_full_spec((N_HEADS_B, w, t)), main, main],
        out_specs=[main, main],
        out_shape=(SDS((length, dil * D_B), F32), SDS((length, dil * D_B), F32)),
        scratch_shapes=[pltpu.VMEM((N_HEADS_B, w, t), F32), pltpu.VMEM((N_HEADS_B, w, t), F32),
                        pltpu.VMEM((D_B, t), BF16), pltpu.VMEM((D_B, t), BF16)],
        compiler_params=_params(2, V7X_VMEM_LIMIT),
    )(view(k), view(v), view(q), view(q), view(q), view(do), view(do), view(do), view(lse), view(lse), view(lse),
      view(delta), view(delta), view(delta), bias_t, view(dk_in), view(dv_in))
    return dk.reshape(s_len, D_B), dv.reshape(s_len, D_B)


def _t5_bucket_np(rel):
    nb = N_BUCKETS // 2
    max_exact = nb // 2
    side = np.where(rel > 0, nb, 0)
    n = np.abs(rel)
    ratio = np.maximum(n, max_exact).astype(np.float32) / np.float32(max_exact)
    large = max_exact + (np.log(ratio) / np.float32(math.log(MAX_DISTANCE / max_exact))
                         * np.float32(nb - max_exact)).astype(np.int32)
    large = np.minimum(large, nb - 1)
    return (side + np.where(n < max_exact, n, large)).astype(np.int32)


def _band_buckets(dil, t):
    rel = np.arange(t + 2 * BAND_HALF)[None, :] - BAND_HALF - np.arange(t)[:, None]
    bucket = _t5_bucket_np(np.clip(rel, -BAND_HALF, BAND_HALF) * dil)
    return np.where(np.abs(rel) <= BAND_HALF, bucket, -1).astype(np.int32)


def _toeplitz(vals, rows, cols):
    heads = vals.shape[0]
    period = rows + cols
    vec = jnp.concatenate([vals[:, rows - 1:], jnp.zeros((heads, 1), vals.dtype), vals[:, :rows - 1]], axis=1)
    flat = jnp.broadcast_to(vec[:, None, :], (heads, rows, period)).reshape(heads, rows * period)
    return flat[:, :rows * (period - 1)].reshape(heads, rows, period - 1)[:, :, :cols]


def _bias_tiles(rel_bias, dil, t):
    w = t + 2 * BAND_HALF
    rel = np.arange(-BAND_HALF, BAND_HALF + 1)
    bucket = _t5_bucket_np(rel * dil)
    runs, start = [], 0
    for i in range(1, len(bucket) + 1):
        if i == len(bucket) or bucket[i] != bucket[start]:
            b = int(bucket[start])
            runs.append(jnp.broadcast_to(rel_bias[b:b + 1], (i - start, N_HEADS_B)))
            start = i
    per_rel = jnp.concatenate(runs, axis=0).T

    def diagonals(lo, hi):
        left = jnp.full((N_HEADS_B, max(0, -BAND_HALF - lo)), NEG_BIG, F32)
        right = jnp.full((N_HEADS_B, max(0, hi - BAND_HALF)), NEG_BIG, F32)
        return jnp.concatenate([left, per_rel, right], axis=1)

    tile = _toeplitz(diagonals(-(t - 1) - BAND_HALF, w - 1 - BAND_HALF), t, w)
    twin = _toeplitz(diagonals(-(w - 1) + BAND_HALF, t - 1 + BAND_HALF), w, t)
    return tile, twin


def _dbias_reduce(dbias, buckets):
    n = len(dbias)

    def body(*refs):
        db_refs, bk_refs, o_ref = refs[:n], refs[n:2 * n], refs[2 * n]
        row = lax.broadcasted_iota(jnp.int32, (N_BUCKETS, 128), 0)
        lane = lax.broadcasted_iota(jnp.int32, (N_BUCKETS, 128), 1)

        def per_bucket(b, out):
            for pat in range(n):
                msk = bk_refs[pat][...] == b
                for h in range(N_HEADS_B):
                    tot = jnp.sum(jnp.where(msk, db_refs[pat][h], 0.0), axis=-1, keepdims=True)
                    tot = jnp.sum(tot, axis=0, keepdims=True)
                    out = out + jnp.where((row == b) & (lane == h), tot, 0.0)
            return out

        o_ref[...] = lax.fori_loop(0, N_BUCKETS, per_bucket, jnp.zeros((N_BUCKETS, 128), F32))

    return pl.pallas_call(
        body, name="dbias_reduce", out_shape=SDS((N_BUCKETS, 128), F32),
        compiler_params=pltpu.CompilerParams(vmem_limit_bytes=V7X_VMEM_LIMIT),
    )(*dbias, *buckets)


def _attn_out_fwd(ya, ob, lb, x, g_a, g_b, w_out, g_post, tm):
    s_len = ya.shape[0]

    def body(ya_ref, o0, o1, o2, l0, l1, l2, x_ref, ga_ref, gb_ref, w_ref, gp_ref,
             h1_ref, yo_ref, yb_ref, lse_ref, ycat_ref):
        m = jnp.maximum(jnp.maximum(l0[...], l1[...]), l2[...])
        w0, w1, w2 = jnp.exp(l0[...] - m), jnp.exp(l1[...] - m), jnp.exp(l2[...] - m)
        wsum = w0 + w1 + w2
        yb = (w0 * o0[...] + w1 * o1[...] + w2 * o2[...]) / wsum
        yb_ref[...] = yb
        lse_ref[...] = m + jnp.log(wsum)
        yan, _ = _rms_fwd(ya_ref[...], ga_ref[...])
        ybn, _ = _rms_fwd(yb, gb_ref[...])
        yan, ybn = yan.astype(BF16), ybn.astype(BF16)
        ycat_ref[:, :D_A] = yan
        ycat_ref[:, D_A:] = ybn
        yo = _dot(yan, w_ref[:D_A, :]) + _dot(ybn, w_ref[D_A:, :])
        yo_ref[...] = yo
        post, _ = _rms_fwd(yo, gp_ref[...])
        h1_ref[...] = x_ref[...] + post

    half, full = _row_spec(tm, D_A), _row_spec(tm, D_MODEL)
    return pl.pallas_call(
        body, name="attn_out_fwd", grid=(s_len // tm,),
        in_specs=[half] * 7 + [full, _full_spec((1, D_A)), _full_spec((1, D_B)), _full_spec((D_MODEL, D_MODEL)),
                               _full_spec((1, D_MODEL))],
        out_specs=[full, full, half, half, full],
        out_shape=(SDS((s_len, D_MODEL), F32), SDS((s_len, D_MODEL), F32), SDS((s_len, D_B), F32),
                   SDS((s_len, D_B), F32), SDS((s_len, D_MODEL), BF16)),
        compiler_params=_params(1, V7X_VMEM_LIMIT),
    )(ya, ob[0], ob[1], ob[2], lb[0], lb[1], lb[2], x, g_a, g_b, w_out, g_post)


def _attn_out_bwd(dh1, yo, ya, yb, g_a, g_b, w_out, g_post, tm):
    s_len = ya.shape[0]

    def body(dh1_ref, yo_ref, ya_ref, yb_ref, ga_ref, gb_ref, w_ref, gp_ref,
             dyo_ref, dya_ref, dyb_ref, dela_ref, delb_ref, dgp_ref, dga_ref, dgb_ref):
        _zero_at_first(pl.program_id(0) == 0, dgp_ref, dga_ref, dgb_ref)
        dyo, dgp = _rms_bwd(yo_ref[...], gp_ref[...], dh1_ref[...])
        dyo = dyo.astype(BF16)
        dyo_ref[...] = dyo
        dgp_ref[...] += dgp
        dya_n = _dot_nt(dyo, w_ref[:D_A, :])
        dyb_n = _dot_nt(dyo, w_ref[D_A:, :])
        ya, yb = ya_ref[...], yb_ref[...]
        dya, dga = _rms_bwd(ya, ga_ref[...], dya_n)
        dyb, dgb = _rms_bwd(yb, gb_ref[...], dyb_n)
        dga_ref[...] += dga
        dgb_ref[...] += dgb
        dya_ref[...] = dya.astype(BF16)
        dyb_ref[...] = dyb.astype(BF16)
        dela_ref[...] = _head_sum(dya * ya)
        delb_ref[...] = _head_sum(dyb * yb)

    half, full = _row_spec(tm, D_A), _row_spec(tm, D_MODEL)
    return pl.pallas_call(
        body, name="attn_out_bwd", grid=(s_len // tm,),
        in_specs=[full, full, half, half, _full_spec((1, D_A)), _full_spec((1, D_B)),
                  _full_spec((D_MODEL, D_MODEL)), _full_spec((1, D_MODEL))],
        out_specs=[full, half, half, half, half, _full_spec((1, D_MODEL)), _full_spec((1, D_A)), _full_spec((1, D_B))],
        out_shape=(SDS((s_len, D_MODEL), BF16), SDS((s_len, D_A), BF16), SDS((s_len, D_B), BF16),
                   SDS((s_len, D_A), F32), SDS((s_len, D_B), F32), SDS((1, D_MODEL), F32), SDS((1, D_A), F32),
                   SDS((1, D_B), F32)),
        compiler_params=_params(1, V7X_VMEM_LIMIT),
    )(dh1, yo, ya, yb, g_a, g_b, w_out, g_post)


FF_CHUNK = 1024


def _mlp_fwd(h1, g_pre, w1, w2, g_post, tm):
    s_len = h1.shape[0]

    def body(h1_ref, gpre_ref, w1_ref, w2_ref, gpost_ref, h2_ref, fo_ref, xn_ref):
        h1v = h1_ref[...]
        xn, _ = _rms_fwd(h1v, gpre_ref[...])
        xn = xn.astype(BF16)
        xn_ref[...] = xn
        fo = jnp.zeros((tm, D_MODEL), F32)
        for c in range(D_FF // FF_CHUNK):
            cols = slice(c * FF_CHUNK, (c + 1) * FF_CHUNK)
            u = jnp.maximum(_dot(xn, w1_ref[:, cols]), 0.0)
            fo = fo + _dot((u * u).astype(BF16), w2_ref[cols, :])
        fo_ref[...] = fo
        post, _ = _rms_fwd(fo, gpost_ref[...])
        h2_ref[...] = h1v + post

    full = _row_spec(tm, D_MODEL)
    return pl.pallas_call(
        body, name="mlp_fwd", grid=(s_len // tm,),
        in_specs=[full, _full_spec((1, D_MODEL)), _const_spec((D_MODEL, D_FF)), _const_spec((D_FF, D_MODEL)),
                  _full_spec((1, D_MODEL))],
        out_specs=[full, full, full],
        out_shape=(SDS((s_len, D_MODEL), F32), SDS((s_len, D_MODEL), F32), SDS((s_len, D_MODEL), BF16)),
        compiler_params=_params(1, V7X_VMEM_LIMIT),
    )(h1, g_pre, w1, w2, g_post)


def _mlp_bwd(h1, dh2, fo, xn, g_pre, w1, w2, g_post, tm):
    s_len = h1.shape[0]

    def body(h1_ref, dh2_ref, fo_ref, xn_ref, gpre_ref, w1_ref, w2_ref, gpost_ref,
             dh1_ref, dfo_ref, du_ref, f_ref, dgpost_ref, dgpre_ref):
        _zero_at_first(pl.program_id(0) == 0, dgpost_ref, dgpre_ref)
        dh2 = dh2_ref[...]
        dfo, dgpost = _rms_bwd(fo_ref[...], gpost_ref[...], dh2)
        dfo = dfo.astype(BF16)
        dfo_ref[...] = dfo
        dgpost_ref[...] += dgpost
        xn = xn_ref[...]
        dxn = jnp.zeros((tm, D_MODEL), F32)
        for c in range(D_FF // FF_CHUNK):
            cols = slice(c * FF_CHUNK, (c + 1) * FF_CHUNK)
            u = jnp.maximum(_dot(xn, w1_ref[:, cols]), 0.0)
            f_ref[:, cols] = (u * u).astype(BF16)
            du = (_dot_nt(dfo, w2_ref[cols, :]) * (2.0 * u)).astype(BF16)
            du_ref[:, cols] = du
            dxn = dxn + _dot_nt(du, w1_ref[:, cols])
        dx, dgpre = _rms_bwd(h1_ref[...], gpre_ref[...], dxn)
        dgpre_ref[...] += dgpre
        dh1_ref[...] = dh2 + dx

    full, wide = _row_spec(tm, D_MODEL), _row_spec(tm, D_FF)
    return pl.pallas_call(
        body, name="mlp_bwd", grid=(s_len // tm,),
        in_specs=[full, full, full, full, _full_spec((1, D_MODEL)), _const_spec((D_MODEL, D_FF)),
                  _const_spec((D_FF, D_MODEL)), _full_spec((1, D_MODEL))],
        out_specs=[full, full, wide, wide, _full_spec((1, D_MODEL)), _full_spec((1, D_MODEL))],
        out_shape=(SDS((s_len, D_MODEL), F32), SDS((s_len, D_MODEL), BF16), SDS((s_len, D_FF), BF16),
                   SDS((s_len, D_FF), BF16), SDS((1, D_MODEL), F32), SDS((1, D_MODEL), F32)),
        compiler_params=_params(1, V7X_VMEM_LIMIT),
    )(h1, dh2, fo, xn, g_pre, w1, w2, g_post)


def _ple_fwd_bwd(h2, p, target, g_ple, w_gate, w_proj, tm):
    s_len = h2.shape[0]

    def body(h2_ref, p_ref, t_ref, g_ref, wg_ref, wp_ref, dh2_ref, loss_ref, dg_ref, xn_ref, dgl_ref, dpp_ref, pb_ref):
        _zero_at_first(pl.program_id(0) == 0, loss_ref, dg_ref)
        h2 = h2_ref[...]
        g = g_ref[...]
        xn, _ = _rms_fwd(h2, g)
        xn = xn.astype(BF16)
        xn_ref[...] = xn
        gate = 1.0 / (1.0 + jnp.exp(-_dot(xn, wg_ref[...])))
        pb = p_ref[...].astype(BF16)
        pb_ref[...] = pb
        pp = _dot(pb, wp_ref[...])
        diff = h2 + gate * pp - t_ref[...]
        loss_ref[...] += jnp.full((8, 128), jnp.sum(diff * diff), F32)
        dh3 = diff * (1.0 / D_MODEL)
        dpp_ref[...] = (dh3 * gate).astype(BF16)
        dgl = (dh3 * pp * gate * (1.0 - gate)).astype(BF16)
        dgl_ref[...] = dgl
        dx, dg = _rms_bwd(h2, g, _dot_nt(dgl, wg_ref[...]))
        dg_ref[...] += dg
        dh2_ref[...] = dh3 + dx

    full = _row_spec(tm, D_MODEL)
    return pl.pallas_call(
        body, name="ple_fwd_bwd", grid=(s_len // tm,),
        in_specs=[full, _row_spec(tm, D_PLE), full, _full_spec((1, D_MODEL)), _full_spec((D_MODEL, D_MODEL)),
                  _full_spec((D_PLE, D_MODEL))],
        out_specs=[full, _full_spec((8, 128)), _full_spec((1, D_MODEL)), full, full, full, _row_spec(tm, D_PLE)],
        out_shape=(SDS((s_len, D_MODEL), F32), SDS((8, 128), F32), SDS((1, D_MODEL), F32), SDS((s_len, D_MODEL), BF16),
                   SDS((s_len, D_MODEL), BF16), SDS((s_len, D_MODEL), BF16), SDS((s_len, D_PLE), BF16)),
        compiler_params=_params(1, V7X_VMEM_LIMIT),
    )(h2, p, target, g_ple, w_gate, w_proj)


def _inproj_bwd(dqs, dkn, dva, dqb, dkb, dvb, qa_raw, ka_raw, x, dh1, g_pre, w_in, cos, sin, gq, gk, tm):
    s_len = x.shape[0]

    def body(dqs_ref, dkn_ref, dva_ref, dqb_ref, dkb_ref, dvb_ref, qa_ref, ka_ref, x_ref, dh1_ref, g_ref, w_ref,
             cos_ref, sin_ref, gq_ref, gk_ref, dx_ref, dproj_ref, dg_ref, dgq_ref, dgk_ref):
        _zero_at_first(pl.program_id(0) == 0, dg_ref, dgq_ref, dgk_ref)
        dqa, dgq = _head_norm_rope_bwd(qa_ref[...], gq_ref[...], cos_ref[...], sin_ref[...], dqs_ref[...])
        dka, dgk = _head_norm_rope_bwd(ka_ref[...], gk_ref[...], cos_ref[:, :D_KV_A], sin_ref[:, :D_KV_A], dkn_ref[...])
        dgq_ref[...] += dgq
        dgk_ref[...] += dgk
        dproj_ref[:, OFF_QA:OFF_KA] = dqa.astype(BF16)
        dproj_ref[:, OFF_KA:OFF_VA] = dka.astype(BF16)
        dproj_ref[:, OFF_VA:OFF_QB] = dva_ref[...].astype(BF16)
        dproj_ref[:, OFF_QB:OFF_KB] = dqb_ref[...].astype(BF16)
        dproj_ref[:, OFF_KB:OFF_VB] = dkb_ref[...].astype(BF16)
        dproj_ref[:, OFF_VB:D_IN] = dvb_ref[...].astype(BF16)
        dxn = _dot_nt(dproj_ref[...], w_ref[...])
        dx, dg = _rms_bwd(x_ref[...], g_ref[...], dxn)
        dg_ref[...] += dg
        dx_ref[...] = dh1_ref[...] + dx

    half, kvw, full = _row_spec(tm, D_A), _row_spec(tm, D_KV_A), _row_spec(tm, D_MODEL)
    return pl.pallas_call(
        body, name="inproj_bwd", grid=(s_len // tm,),
        in_specs=[half, kvw, kvw, half, half, half, half, kvw, full, full, _full_spec((1, D_MODEL)),
                  _const_spec((D_MODEL, D_IN)), half, half, _full_spec((1, D_A)), _full_spec((1, D_KV_A))],
        out_specs=[full, _row_spec(tm, D_IN), _full_spec((1, D_MODEL)), _full_spec((1, D_A)), _full_spec((1, D_KV_A))],
        out_shape=(SDS((s_len, D_MODEL), F32), SDS((s_len, D_IN), BF16), SDS((1, D_MODEL), F32), SDS((1, D_A), F32),
                   SDS((1, D_KV_A), F32)),
        compiler_params=_params(1, V7X_VMEM_LIMIT),
    )(dqs, dkn, dva, dqb, dkb, dvb, qa_raw, ka_raw, x, dh1, g_pre, w_in, cos, sin, gq, gk)


def _weight_grad(a, b, name, tk1, tn, tm):
    s_len, k1 = a.shape
    n = b.shape[1]
    steps = s_len // tm

    def body(a_ref, b_ref, o_ref, acc):
        r = pl.program_id(2)
        _zero_at_first(r == 0, acc)
        acc[...] += _dot_tn(a_ref[...], b_ref[...])

        @pl.when(r == steps - 1)
        def _():
            o_ref[...] = acc[...].astype(BF16)

    return pl.pallas_call(
        body, name=name, grid=(k1 // tk1, n // tn, steps),
        in_specs=[pl.BlockSpec((tm, tk1), lambda i, j, r: (r, i)), pl.BlockSpec((tm, tn), lambda i, j, r: (r, j))],
        out_specs=pl.BlockSpec((tk1, tn), lambda i, j, r: (i, j)),
        out_shape=SDS((k1, n), BF16),
        scratch_shapes=[pltpu.VMEM((tk1, tn), F32)],
        compiler_params=_params(3, V7X_VMEM_LIMIT),
    )(a, b)


def _my_index():
    return 4 * lax.axis_index("x") + 2 * lax.axis_index("y") + lax.axis_index("c")


def _peer(k):
    f = k + 1
    x, y, c = lax.axis_index("x"), lax.axis_index("y"), lax.axis_index("c")
    return (x ^ ((f >> 2) & 1), y ^ ((f >> 1) & 1), c ^ (f & 1))


def _all_gather(blocks, name):
    n = len(blocks)

    def body(*refs):
        ins, outs = refs[:n], refs[n:2 * n]
        send_sems, recv_sems, local_sems = refs[2 * n:]
        me = _my_index()
        copies = []
        for a in range(n):
            own = pltpu.make_async_copy(ins[a], outs[a].at[me], local_sems.at[a])
            own.start()
            copies.append(own)
        remote = []
        for a in range(n):
            for k in range(N_DEV - 1):
                cp = pltpu.make_async_remote_copy(
                    src_ref=ins[a], dst_ref=outs[a].at[me], send_sem=send_sems.at[a, k], recv_sem=recv_sems.at[a, k],
                    device_id=_peer(k), device_id_type=pl.DeviceIdType.MESH)
                cp.start()
                remote.append(cp)
        for cp in copies:
            cp.wait()
        for cp in remote:
            cp.wait_send()
        for cp in remote:
            cp.wait_recv()

    any_spec = pl.BlockSpec(memory_space=pl.ANY)
    return pl.pallas_call(
        body, name=name,
        in_specs=[any_spec] * n, out_specs=[any_spec] * n,
        out_shape=[SDS((N_DEV,) + b.shape, b.dtype) for b in blocks],
        scratch_shapes=[pltpu.SemaphoreType.DMA((n, N_DEV - 1)), pltpu.SemaphoreType.DMA((n, N_DEV - 1)),
                        pltpu.SemaphoreType.DMA((n,))],
    )(*blocks)


def _all_to_all(parts, name):
    n = len(parts)

    def body(*refs):
        ins, outs = refs[:n], refs[n:2 * n]
        send_sems, recv_sems, local_sems = refs[2 * n:]
        me = _my_index()
        copies = []
        for a in range(n):
            own = pltpu.make_async_copy(ins[a].at[me], outs[a].at[me], local_sems.at[a])
            own.start()
            copies.append(own)
        remote = []
        for a in range(n):
            for k in range(N_DEV - 1):
                px, py, pc = _peer(k)
                cp = pltpu.make_async_remote_copy(
                    src_ref=ins[a].at[4 * px + 2 * py + pc], dst_ref=outs[a].at[me],
                    send_sem=send_sems.at[a, k], recv_sem=recv_sems.at[a, k],
                    device_id=(px, py, pc), device_id_type=pl.DeviceIdType.MESH)
                cp.start()
                remote.append(cp)
        for cp in copies:
            cp.wait()
        for cp in remote:
            cp.wait_send()
        for cp in remote:
            cp.wait_recv()

    any_spec = pl.BlockSpec(memory_space=pl.ANY)
    return pl.pallas_call(
        body, name=name,
        in_specs=[any_spec] * n, out_specs=[any_spec] * n,
        out_shape=[SDS(p.shape, p.dtype) for p in parts],
        scratch_shapes=[pltpu.SemaphoreType.DMA((n, N_DEV - 1)), pltpu.SemaphoreType.DMA((n, N_DEV - 1)),
                        pltpu.SemaphoreType.DMA((n,))],
    )(*parts)


def _sum_adamw(parts, w, m, v, name, tr):
    rows, cols = w.shape
    c1 = 1.0 / (1.0 - ADAM_B1 ** ADAM_STEP)
    c2 = 1.0 / (1.0 - ADAM_B2 ** ADAM_STEP)

    def body(p_ref, w_ref, m_ref, v_ref, g_ref, d_ref, nm_ref, nv_ref):
        g = p_ref[0].astype(F32)
        for j in range(1, N_DEV):
            g = g + p_ref[j].astype(F32)
        g_ref[...] = g
        nm = ADAM_B1 * m_ref[...] + (1.0 - ADAM_B1) * g
        nv = ADAM_B2 * v_ref[...] + (1.0 - ADAM_B2) * (g * g)
        nm_ref[...] = nm
        nv_ref[...] = nv
        d_ref[...] = -ADAM_LR * ((nm * c1) / (jnp.sqrt(nv * c2) + ADAM_EPS) + ADAM_WD * w_ref[...])

    blk = pl.BlockSpec((tr, cols), lambda i: (i, 0))
    return pl.pallas_call(
        body, name=name, grid=(rows // tr,),
        in_specs=[pl.BlockSpec((N_DEV, tr, cols), lambda i: (0, i, 0)), blk, blk, blk],
        out_specs=[blk] * 4, out_shape=[SDS((rows, cols), F32)] * 4,
        compiler_params=_params(1, V7X_VMEM_LIMIT),
    )(parts, w, m, v)


_SMALL = (("g_attn_pre", 1024), ("g_q", 64), ("g_k", 64), ("g_out_a", 512), ("g_out_b", 512), ("g_attn_post", 1024),
          ("rel_bias", 256), ("g_mlp_pre", 1024), ("g_mlp_post", 1024), ("g_ple", 1024))
_SLAB_ROWS = 56


def _pack_small(vals):
    rows = []
    for (name, size) in _SMALL:
        flat = vals[name].reshape(-1).astype(F32)
        padded = -(-size // 128) * 128
        rows.append(jnp.pad(flat, (0, padded - size)).reshape(padded // 128, 128))
    slab = jnp.concatenate(rows, axis=0)
    return jnp.pad(slab, ((0, _SLAB_ROWS - slab.shape[0]), (0, 0)))


def _unpack_small(slab, shapes):
    out, row = {}, 0
    for (name, size) in _SMALL:
        nrow = -(-size // 128)
        out[name] = slab[row:row + nrow].reshape(-1)[:size].reshape(shapes[name])
        row += nrow
    return out


def _rope_tables(s_len):
    rows = s_len // GRID_W
    row = jnp.broadcast_to(jnp.arange(rows)[:, None], (rows, GRID_W)).reshape(-1).astype(F32)
    col = jnp.broadcast_to(jnp.arange(GRID_W)[None, :], (rows, GRID_W)).reshape(-1).astype(F32)
    n_axis = ROPE_HALF // 2
    inv_freq = ROPE_THETA ** (-jnp.arange(n_axis, dtype=F32) / n_axis)
    ang = jnp.concatenate([row[:, None] * inv_freq, col[:, None] * inv_freq], axis=-1)
    cos, sin = jnp.cos(ang), jnp.sin(ang)
    cos = jnp.tile(jnp.concatenate([cos, cos], axis=-1), (1, N_HEADS_A))
    sin = jnp.tile(jnp.concatenate([-sin, sin], axis=-1), (1, N_HEADS_A))
    return cos, sin


def _local_step(x, p, target, w_in, w_out, w_ff1, w_ff2, w_gate, w_proj, small):
    s_len = x.shape[0]
    tm = min(256, s_len)
    tq = min(128, s_len)
    tk = min(2048, s_len // 2)
    cos, sin = _rope_tables(s_len)
    gq = jnp.tile(small["g_q"], (1, N_HEADS_A))
    gk = jnp.tile(small["g_k"], (1, N_KV_A))

    qa_raw, ka_raw, qs, kn, va, qb, kb, vb, xn1 = _inproj_fwd(x, small["g_attn_pre"], w_in, cos, sin, gq, gk, tm)

    def kv_major(a):
        return a.reshape(s_len, N_KV_A, HEAD_DIM).transpose(1, 0, 2)

    def kv_chunks_t(a):
        return a.reshape(s_len // tk, tk, N_KV_A, HEAD_DIM).transpose(2, 0, 3, 1)

    def kv_unchunk(a):
        return a.transpose(1, 3, 0, 2).reshape(s_len, D_KV_A)

    k_maj, kt, vt = kv_major(kn), kv_chunks_t(kn), kv_chunks_t(va)
    vt_ones = jnp.concatenate([vt, jnp.ones((N_KV_A, s_len // tk, 16, tk), BF16)], axis=2)
    ya, lse_a = _attn_fwd(qs, k_maj, vt_ones, tq, tk)

    ob, lb, tiles = [], [], []
    for (_, dil) in DILATED_PATTERNS:
        t = min(256, s_len // dil)
        bias, bias_t = _bias_tiles(small["rel_bias"], dil, t)
        tiles.append((t, bias, bias_t, _band_buckets(dil, t)))
        o, l = _band_fwd(qb, kb, vb, bias, dil, t)
        ob.append(o)
        lb.append(l)

    h1, yo, yb, lse_b, ycat = _attn_out_fwd(ya, ob, lb, x, small["g_out_a"], small["g_out_b"], w_out,
                                            small["g_attn_post"], tm)
    h2, fo, xn2 = _mlp_fwd(h1, small["g_mlp_pre"], w_ff1, w_ff2, small["g_mlp_post"], tm)
    dh2, loss_part, dg_ple, xn3, dgl, dpp, pb = _ple_fwd_bwd(h2, p, target, small["g_ple"], w_gate, w_proj, tm)
    dh1, dfo, du, f, dg_mlp_post, dg_mlp_pre = _mlp_bwd(h1, dh2, fo, xn2, small["g_mlp_pre"], w_ff1, w_ff2,
                                                          small["g_mlp_post"], tm)
    dyo, dya, dyb, delta_a, delta_b, dg_attn_post, dg_out_a, dg_out_b = _attn_out_bwd(
        dh1, yo, ya, yb, small["g_out_a"], small["g_out_b"], w_out, small["g_attn_post"], tm)

    dqs, dk_t, dv_t = _attn_bwd(qs, dya, lse_a, delta_a, kt, k_maj, vt, tq, tk)
    dkn, dva = kv_unchunk(dk_t), kv_unchunk(dv_t)

    dqb = dkb = dvb = jnp.zeros((s_len, D_B), F32)
    dbias = []
    for (_, dil), (t, bias, bias_t, _) in zip(DILATED_PATTERNS, tiles):
        dqb, db = _band_dq(qb, dyb, lse_b, delta_b, kb, vb, bias, dqb, dil, t)
        dkb, dvb = _band_dkv(kb, vb, qb, dyb, lse_b, delta_b, bias_t, dkb, dvb, dil, t)
        dbias.append(db)
    d_rel = _dbias_reduce(dbias, [jnp.asarray(tl[3]) for tl in tiles])[:, :N_HEADS_B]

    dx, dproj, dg_attn_pre, dgq_lanes, dgk_lanes = _inproj_bwd(
        dqs, dkn, dva, dqb, dkb, dvb, qa_raw, ka_raw, x, dh1, small["g_attn_pre"], w_in, cos, sin, gq, gk, tm)

    tg = min(512, s_len)
    grads = {
        "w_in": _weight_grad(xn1, dproj, "grad_w_in", D_MODEL, 768, tg),
        "w_out": _weight_grad(ycat, dyo, "grad_w_out", D_MODEL, D_MODEL, tg),
        "w_ff1": _weight_grad(xn2, du, "grad_w_ff1", D_MODEL, 1024, tg),
        "w_ff2": _weight_grad(f, dfo, "grad_w_ff2", 1024, D_MODEL, tg),
        "w_ple_gate": _weight_grad(xn3, dgl, "grad_w_ple_gate", D_MODEL, D_MODEL, tg),
        "w_ple_proj": _weight_grad(pb, dpp, "grad_w_ple_proj", D_PLE, D_MODEL, tg),
    }
    small_grads = {
        "g_attn_pre": dg_attn_pre, "g_q": dgq_lanes.reshape(N_HEADS_A, HEAD_DIM).sum(0, keepdims=True),
        "g_k": dgk_lanes.reshape(N_KV_A, HEAD_DIM).sum(0, keepdims=True), "g_out_a": dg_out_a, "g_out_b": dg_out_b,
        "g_attn_post": dg_attn_post, "rel_bias": d_rel, "g_mlp_pre": dg_mlp_pre, "g_mlp_post": dg_mlp_post,
        "g_ple": dg_ple,
    }
    return loss_part[0, 0], dx, grads, small_grads


_BIG = (("w_in", 1), ("w_out", 0), ("w_ff1", 1), ("w_ff2", 0), ("w_ple_gate", 0), ("w_ple_proj", 1))


def _assemble(gathered, axis):
    if axis == 0:
        return gathered.reshape(-1, gathered.shape[2])
    return gathered.transpose(1, 0, 2).reshape(gathered.shape[1], -1)


def _cut(full, axis):
    if axis == 0:
        return full.reshape(N_DEV, full.shape[0] // N_DEV, full.shape[1])
    return full.reshape(full.shape[0], N_DEV, full.shape[1] // N_DEV).transpose(1, 0, 2)


def kernel(x, p, w_in, g_attn_pre, g_q, g_k, g_out_a, g_out_b, w_out, g_attn_post, rel_bias, g_mlp_pre, w_ff1, w_ff2, g_mlp_post, g_ple, w_ple_gate, w_ple_proj, loss_target, m_w_in, m_g_attn_pre, m_g_q, m_g_k, m_g_out_a, m_g_out_b, m_w_out, m_g_attn_post, m_rel_bias, m_g_mlp_pre, m_w_ff1, m_w_ff2, m_g_mlp_post, m_g_ple, m_w_ple_gate, m_w_ple_proj, v_w_in, v_g_attn_pre, v_g_q, v_g_k, v_g_out_a, v_g_out_b, v_w_out, v_g_attn_post, v_rel_bias, v_g_mlp_pre, v_w_ff1, v_w_ff2, v_g_mlp_post, v_g_ple, v_w_ple_gate, v_w_ple_proj):
    given = dict(locals())
    small_names = [n for n, _ in _SMALL]
    small = {n: given[n] for n in small_names}
    shards = {n: given[n][0] for n, _ in _BIG}

    gathered = _all_gather([shards[n].astype(BF16) for n, _ in _BIG], "gather_weights")
    full = {n: _assemble(g, axis) for (n, axis), g in zip(_BIG, gathered)}

    loss_part, dx, grads, small_grads = _local_step(
        x[0], p[0, 0], loss_target[0], full["w_in"], full["w_out"], full["w_ff1"], full["w_ff2"], full["w_ple_gate"],
        full["w_ple_proj"], small)

    parts = [_cut(grads[n], axis) for n, axis in _BIG]
    slab = _pack_small(small_grads)
    slab_parts = jnp.broadcast_to(slab[None], (N_DEV,) + slab.shape)
    received = _all_to_all(parts + [slab_parts], "exchange_grads")

    out_g, out_d, out_m, out_v = {}, {}, {}, {}
    for (n, _), rec in zip(_BIG, received[:-1]):
        rows = shards[n].shape[0]
        g, d, nm, nv = _sum_adamw(rec, shards[n], given["m_" + n][0], given["v_" + n][0], "adamw_" + n, min(rows, 128))
        out_g[n], out_d[n], out_m[n], out_v[n] = g[None], d[None], nm[None], nv[None]
    g, d, nm, nv = _sum_adamw(received[-1], _pack_small(small), _pack_small({n: given["m_" + n] for n in small_names}),
                              _pack_small({n: given["v_" + n] for n in small_names}), "adamw_small", _SLAB_ROWS)
    shapes = {n: given[n].shape for n in small_names}
    for dst, slab_out in ((out_g, g), (out_d, d), (out_m, nm), (out_v, nv)):
        dst.update(_unpack_small(slab_out, shapes))

    loss = lax.psum(loss_part * (0.5 / D_MODEL), MESH_AXES)
    order = ["w_in", "g_attn_pre", "g_q", "g_k", "g_out_a", "g_out_b", "w_out", "g_attn_post", "rel_bias", "g_mlp_pre",
             "w_ff1", "w_ff2", "g_mlp_post", "g_ple", "w_ple_gate", "w_ple_proj"]
    return (loss, dx[None], *[out_g[n] for n in order], *[out_d[n] for n in order], *[out_m[n] for n in order],
            *[out_v[n] for n in order])
```

```python
import functools
import math

import jax
import jax.numpy as jnp
import numpy as np
from jax import lax
from jax.experimental import pallas as pl
from jax.experimental.pallas import tpu as pltpu

F32 = jnp.float32
BF16 = jnp.bfloat16
SDS = jax.ShapeDtypeStruct

D_MODEL = 1024
HEAD_DIM = 64
N_HEADS_A = 8
N_KV_A = 2
GROUP_A = N_HEADS_A // N_KV_A
N_HEADS_B = 8
D_A = N_HEADS_A * HEAD_DIM
D_KV_A = N_KV_A * HEAD_DIM
D_B = N_HEADS_B * HEAD_DIM
D_IN = D_A + 2 * D_KV_A + 3 * D_B
D_FF = 4 * D_MODEL
D_PLE = 256
GRID_W = 64
ROPE_THETA = 10000.0
ROPE_HALF = HEAD_DIM // 2
DILATED_PATTERNS = ((128, 1), (512, 4), (2048, 16))
BAND_HALF = 64
N_BUCKETS = 32
MAX_DISTANCE = 1024
EPS = 1e-6
NEG_BIG = -1e30
SCORE_SCALE = HEAD_DIM ** -0.5

ADAM_LR = 0.001
ADAM_B1 = 0.9
ADAM_B2 = 0.999
ADAM_EPS = 1e-08
ADAM_WD = 0.01
ADAM_STEP = 10

N_DEV = 8
MESH_AXES = ("x", "y", "c")
V7X_VMEM_LIMIT = 56 * 1024 * 1024

OFF_QA, OFF_KA, OFF_VA, OFF_QB, OFF_KB, OFF_VB = 0, 512, 640, 768, 1280, 1792


def _params(n_axes, vmem=None):
    return pltpu.CompilerParams(dimension_semantics=("arbitrary",) * n_axes, vmem_limit_bytes=vmem)


def _dot(a, b):
    return jnp.dot(a, b, preferred_element_type=F32)


def _dot_nt(a, b):
    return lax.dot_general(a, b, (((1,), (1,)), ((), ())), preferred_element_type=F32)


def _dot_tn(a, b):
    return lax.dot_general(a, b, (((0,), (0,)), ((), ())), preferred_element_type=F32)


def _rms_fwd(x, g):
    r = lax.rsqrt(jnp.mean(x * x, axis=-1, keepdims=True) + EPS)
    return x * r * g, r


def _rms_bwd(x, g, dy):
    r = lax.rsqrt(jnp.mean(x * x, axis=-1, keepdims=True) + EPS)
    xh = x * r
    dxh = dy * g
    dx = r * (dxh - xh * jnp.mean(dxh * xh, axis=-1, keepdims=True))
    return dx, jnp.sum(dy * xh, axis=0, keepdims=True)


def _head_sum(v):
    head = lax.broadcasted_iota(jnp.int32, v.shape, 1) >> 6
    out = jnp.zeros_like(v)
    for h in range(v.shape[1] // HEAD_DIM):
        msk = head == h
        s = jnp.sum(jnp.where(msk, v, 0.0), axis=-1, keepdims=True)
        out = jnp.where(msk, s, out)
    return out


def _swap_halves(v):
    w = v.shape[1]
    lane = lax.broadcasted_iota(jnp.int32, v.shape, 1)
    first_half = (lane & (HEAD_DIM - 1)) < ROPE_HALF
    return jnp.where(first_half, pltpu.roll(v, w - ROPE_HALF, 1), pltpu.roll(v, ROPE_HALF, 1))


def _head_norm_rope(v, g, cos, sin_signed):
    r = lax.rsqrt(_head_sum(v * v) * (1.0 / HEAD_DIM) + EPS)
    y = v * r * g
    return y * cos + _swap_halves(y) * sin_signed


def _head_norm_rope_bwd(v, g, cos, sin_signed, dout):
    dy = dout * cos - _swap_halves(dout) * sin_signed
    r = lax.rsqrt(_head_sum(v * v) * (1.0 / HEAD_DIM) + EPS)
    xh = v * r
    dxh = dy * g
    dv = r * (dxh - xh * (_head_sum(dxh * xh) * (1.0 / HEAD_DIM)))
    return dv, jnp.sum(dy * xh, axis=0, keepdims=True)


def _row_spec(tm, n):
    return pl.BlockSpec((tm, n), lambda i: (i, 0))


def _full_spec(shape):
    nd = len(shape)
    return pl.BlockSpec(shape, lambda *_: (0,) * nd)


def _const_spec(shape):
    nd = len(shape)
    return pl.BlockSpec(shape, lambda *_: (0,) * nd, pipeline_mode=pl.Buffered(1))


def _zero_at_first(first, *refs):
    @pl.when(first)
    def _():
        for ref in refs:
            ref[...] = jnp.zeros_like(ref)


def _inproj_fwd(x, g_pre, w_in, cos, sin, gq, gk, tm):
    s_len = x.shape[0]

    def body(x_ref, g_ref, w_ref, cos_ref, sin_ref, gq_ref, gk_ref,
             qa_raw, ka_raw, qs, kn, va, qb, kb, vb, xn_out):
        xn, _ = _rms_fwd(x_ref[...], g_ref[...])
        xn = xn.astype(BF16)
        xn_out[...] = xn
        qa = _dot(xn, w_ref[:, OFF_QA:OFF_KA])
        qa_raw[...] = qa
        qs[...] = (_head_norm_rope(qa, gq_ref[...], cos_ref[...], sin_ref[...]) * SCORE_SCALE).astype(BF16)
        ka = _dot(xn, w_ref[:, OFF_KA:OFF_VA])
        ka_raw[...] = ka
        kn[...] = _head_norm_rope(ka, gk_ref[...], cos_ref[:, :D_KV_A], sin_ref[:, :D_KV_A]).astype(BF16)
        va[...] = _dot(xn, w_ref[:, OFF_VA:OFF_QB]).astype(BF16)
        qb[...] = (_dot(xn, w_ref[:, OFF_QB:OFF_KB]) * SCORE_SCALE).astype(BF16)
        kb[...] = _dot(xn, w_ref[:, OFF_KB:OFF_VB]).astype(BF16)
        vb[...] = _dot(xn, w_ref[:, OFF_VB:D_IN]).astype(BF16)

    return pl.pallas_call(
        body, name="inproj_fwd", grid=(s_len // tm,),
        in_specs=[_row_spec(tm, D_MODEL), _full_spec((1, D_MODEL)), _const_spec((D_MODEL, D_IN)),
                  _row_spec(tm, D_A), _row_spec(tm, D_A), _full_spec((1, D_A)), _full_spec((1, D_KV_A))],
        out_specs=[_row_spec(tm, D_A), _row_spec(tm, D_KV_A), _row_spec(tm, D_A), _row_spec(tm, D_KV_A),
                   _row_spec(tm, D_KV_A), _row_spec(tm, D_B), _row_spec(tm, D_B), _row_spec(tm, D_B),
                   _row_spec(tm, D_MODEL)],
        out_shape=(SDS((s_len, D_A), F32), SDS((s_len, D_KV_A), F32), SDS((s_len, D_A), BF16),
                   SDS((s_len, D_KV_A), BF16), SDS((s_len, D_KV_A), BF16), SDS((s_len, D_B), BF16),
                   SDS((s_len, D_B), BF16), SDS((s_len, D_B), BF16), SDS((s_len, D_MODEL), BF16)),
        compiler_params=_params(1, V7X_VMEM_LIMIT),
    )(x, g_pre, w_in, cos, sin, gq, gk)


def _stack_heads(ref, tq):
    return jnp.concatenate([ref[:, HEAD_DIM * g:HEAD_DIM * (g + 1)] for g in range(GROUP_A)], axis=0)


def _stack_cols(ref, tq):
    return jnp.concatenate([ref[:, HEAD_DIM * g:HEAD_DIM * g + 1] for g in range(GROUP_A)], axis=0)


def _attn_fwd(qs, k, vt, tq, tk):
    s_len = qs.shape[0]
    nk = s_len // tk
    assert nk % 2 == 0
    gw = GROUP_A * HEAD_DIM
    rows = GROUP_A * tq
    vrows = vt.shape[2]

    def body(q_ref, k_ref, vt_ref, o_ref, lse_ref, s_buf):
        qt = _stack_heads(q_ref, tq).T

        def scores(j, slot):
            kj = k_ref[pl.ds(pl.multiple_of(j * tk, tk), tk), :]
            s_buf[slot] = _dot(kj, qt)

        def consume(j, slot, carry):
            m, acc = carry
            st = s_buf[slot]
            m_new = jnp.maximum(m, jnp.max(st, axis=0, keepdims=True))
            pt = jnp.exp(st - m_new)
            acc = jnp.exp(m - m_new) * acc + _dot(vt_ref[j], pt.astype(BF16))
            return m_new, acc

        scores(0, 0)

        def pair(j, carry, more):
            scores(j + 1, 1)
            carry = consume(j, 0, carry)
            if more:
                scores(j + 2, 0)
            return consume(j + 1, 1, carry)

        carry = (jnp.full((1, rows), NEG_BIG, F32), jnp.zeros((vrows, rows), F32))
        carry = lax.fori_loop(0, nk // 2 - 1, lambda jj, c: pair(2 * jj, c, True), carry)
        m, acc = pair(nk - 2, carry, False)
        l = acc[HEAD_DIM:HEAD_DIM + 1]
        o = (acc[:HEAD_DIM] / l).T
        lse = jnp.broadcast_to(m + jnp.log(l), (HEAD_DIM, rows)).T
        for g in range(GROUP_A):
            o_ref[:, HEAD_DIM * g:HEAD_DIM * (g + 1)] = o[g * tq:(g + 1) * tq]
            lse_ref[:, HEAD_DIM * g:HEAD_DIM * (g + 1)] = lse[g * tq:(g + 1) * tq]

    tile = pl.BlockSpec((tq, gw), lambda kv, i: (i, kv))
    return pl.pallas_call(
        body, name="attn_fwd", grid=(N_KV_A, s_len // tq),
        in_specs=[tile, pl.BlockSpec((None, s_len, HEAD_DIM), lambda kv, i: (kv, 0, 0)),
                  pl.BlockSpec((None, nk, vrows, tk), lambda kv, i: (kv, 0, 0, 0))],
        out_specs=[tile, tile],
        out_shape=(SDS((s_len, D_A), F32), SDS((s_len, D_A), F32)),
        scratch_shapes=[pltpu.VMEM((2, tk, rows), F32)],
        compiler_params=_params(2, V7X_VMEM_LIMIT),
    )(qs, k, vt)


def _attn_bwd(qs, do, lse, delta, kt, k, vt, tq, tk):
    s_len = qs.shape[0]
    nk = s_len // tk
    assert nk % 2 == 0
    nq = s_len // tq
    gw = GROUP_A * HEAD_DIM
    rows = GROUP_A * tq

    def body(q_ref, do_ref, lse_ref, delta_ref, kt_ref, k_ref, vt_ref, dq_ref, dk_hbm, dv_hbm,
             dk_acc, dv_acc, s_buf, dp_buf):
        kv = pl.program_id(0)
        i = pl.program_id(1)

        @pl.when(i == 0)
        def _():
            dk_acc[...] = jnp.zeros_like(dk_acc)
            dv_acc[...] = jnp.zeros_like(dv_acc)

        q = _stack_heads(q_ref, tq)
        dout = _stack_heads(do_ref, tq)
        qt = q.T
        doutt = dout.T
        row_lse = _stack_cols(lse_ref, tq)
        row_delta = _stack_cols(delta_ref, tq)

        def scores(j, slot):
            s_buf[slot] = _dot(q, kt_ref[j])
            dp_buf[slot] = _dot(dout, vt_ref[j])

        def consume(j, slot, dq):
            p = jnp.exp(s_buf[slot] - row_lse)
            ds = (p * (dp_buf[slot] - row_delta)).astype(BF16)
            dv_acc[j] += _dot(doutt, p.astype(BF16))
            dk_acc[j] += _dot(qt, ds)
            return dq + _dot(ds, k_ref[pl.ds(pl.multiple_of(j * tk, tk), tk), :])

        scores(0, 0)

        def pair(j, dq, more):
            scores(j + 1, 1)
            dq = consume(j, 0, dq)
            if more:
                scores(j + 2, 0)
            return consume(j + 1, 1, dq)

        dq = lax.fori_loop(0, nk // 2 - 1, lambda jj, c: pair(2 * jj, c, True), jnp.zeros((rows, HEAD_DIM), F32))
        dq = pair(nk - 2, dq, False) * SCORE_SCALE
        for g in range(GROUP_A):
            dq_ref[:, HEAD_DIM * g:HEAD_DIM * (g + 1)] = dq[g * tq:(g + 1) * tq]

        @pl.when(i == nq - 1)
        def _():
            pltpu.sync_copy(dk_acc, dk_hbm.at[kv])
            pltpu.sync_copy(dv_acc, dv_hbm.at[kv])

    tile = pl.BlockSpec((tq, gw), lambda kv, i: (i, kv))
    chunks = pl.BlockSpec((None, nk, HEAD_DIM, tk), lambda kv, i: (kv, 0, 0, 0))
    grad_t = SDS((N_KV_A, nk, HEAD_DIM, tk), F32)
    return pl.pallas_call(
        body, name="attn_bwd", grid=(N_KV_A, nq),
        in_specs=[tile, tile, tile, tile, chunks,
                  pl.BlockSpec((None, s_len, HEAD_DIM), lambda kv, i: (kv, 0, 0)), chunks],
        out_specs=[tile, pl.BlockSpec(memory_space=pl.ANY), pl.BlockSpec(memory_space=pl.ANY)],
        out_shape=(SDS((s_len, D_A), F32), grad_t, grad_t),
        scratch_shapes=[pltpu.VMEM((nk, HEAD_DIM, tk), F32), pltpu.VMEM((nk, HEAD_DIM, tk), F32),
                        pltpu.VMEM((2, rows, tk), F32), pltpu.VMEM((2, rows, tk), F32)],
        compiler_params=_params(2, V7X_VMEM_LIMIT),
    )(qs, do, lse, delta, kt, k, vt)


def _band_specs(length, t):
    hb = t // BAND_HALF
    last = length // BAND_HALF - 1
    main = pl.BlockSpec((t, D_B), lambda r, i: (i, r))
    prev = pl.BlockSpec((BAND_HALF, D_B), lambda r, i: (jnp.maximum(i * hb - 1, 0), r))
    nxt = pl.BlockSpec((BAND_HALF, D_B), lambda r, i: (jnp.minimum((i + 1) * hb, last), r))
    return main, [prev, main, nxt]


def _window(refs):
    return jnp.concatenate([r[...] for r in refs], axis=0)


def _head(v, h):
    return v[:, HEAD_DIM * h:HEAD_DIM * (h + 1)]


def _call_with_rider(body, name, grid, in_specs, out_specs, out_shape, scratch_shapes, args, rider):
    n_in, n_out = len(in_specs), len(out_specs)
    if rider is None:
        res = pl.pallas_call(body, name=name, grid=grid, in_specs=in_specs, out_specs=out_specs, out_shape=out_shape,
                             scratch_shapes=scratch_shapes, compiler_params=_params(len(grid), V7X_VMEM_LIMIT))(*args)
        return res, []
    kind, arrays = rider
    n = len(arrays)

    def with_rider(*refs):
        ins, c_ins = refs[:n_in], refs[n_in:n_in + n]
        outs, c_outs = refs[n_in + n:n_in + n + n_out], refs[n_in + n + n_out:n_in + 2 * n + n_out]
        rest = refs[n_in + 2 * n + n_out:]
        scratch, sems = rest[:-3], rest[-3:]
        ids = [pl.program_id(a) for a in range(len(grid))]
        first = functools.reduce(jnp.logical_and, [i == 0 for i in ids])
        last = functools.reduce(jnp.logical_and, [i == g - 1 for i, g in zip(ids, grid)])

        @pl.when(first)
        def _():
            _comm_start(kind, c_ins, c_outs, sems)

        body(*ins, *outs, *scratch)

        @pl.when(last)
        def _():
            _comm_wait(kind, c_ins, c_outs, sems)

    res = pl.pallas_call(
        with_rider, name=name, grid=grid, in_specs=list(in_specs) + [ANY_SPEC] * n,
        out_specs=list(out_specs) + [ANY_SPEC] * n, out_shape=list(out_shape) + _comm_out_shapes(kind, arrays),
        scratch_shapes=list(scratch_shapes) + _comm_sems(n), compiler_params=_params(len(grid), V7X_VMEM_LIMIT),
    )(*args, *arrays)
    return res[:n_out], res[n_out:]


def _band_fwd(q, k, v, bias, dil, t, rider=None):
    s_len = q.shape[0]
    length = s_len // dil
    w = t + 2 * BAND_HALF
    view = lambda a: a.reshape(length, dil * D_B)
    main, win = _band_specs(length, t)

    def body(q_ref, k0, k1, k2, v0, v1, v2, bias_ref, o_ref, lse_ref, s_buf, kt_buf):
        i = pl.program_id(1)
        kt_buf[...] = _window((k0, k1, k2)).T
        vw = _window((v0, v1, v2))
        pos = i * t - BAND_HALF + lax.broadcasted_iota(jnp.int32, (1, w), 1)
        valid = (pos >= 0) & (pos < length)
        for h in range(N_HEADS_B):
            s_buf[h] = _dot(q_ref[:, HEAD_DIM * h:HEAD_DIM * (h + 1)], kt_buf[HEAD_DIM * h:HEAD_DIM * (h + 1), :])
        for h in range(N_HEADS_B):
            s = jnp.where(valid, s_buf[h] + bias_ref[h], NEG_BIG)
            m = jnp.max(s, axis=-1, keepdims=True)
            e = jnp.exp(s - m)
            den = jnp.sum(e, axis=-1, keepdims=True)
            o_ref[:, HEAD_DIM * h:HEAD_DIM * (h + 1)] = _dot(e.astype(BF16), _head(vw, h)) / den
            lse_ref[:, HEAD_DIM * h:HEAD_DIM * (h + 1)] = jnp.broadcast_to(m + jnp.log(den), (t, HEAD_DIM))

    (o, lse), rode = _call_with_rider(
        body, f"band_fwd_d{dil}", (dil, length // t),
        [main] + win + win + [_full_spec((N_HEADS_B, t, w))], [main, main],
        [SDS((length, dil * D_B), F32), SDS((length, dil * D_B), F32)],
        [pltpu.VMEM((N_HEADS_B, t, w), F32), pltpu.VMEM((D_B, w), BF16)],
        (view(q), view(k), view(k), view(k), view(v), view(v), view(v), bias), rider)
    return o.reshape(s_len, D_B), lse.reshape(s_len, D_B), rode


def _band_dq(q, do, lse, delta, k, v, bias, dq_in, dil, t, rider=None):
    s_len = q.shape[0]
    length = s_len // dil
    w = t + 2 * BAND_HALF
    view = lambda a: a.reshape(length, dil * D_B)
    main, win = _band_specs(length, t)

    def body(q_ref, do_ref, lse_ref, delta_ref, k0, k1, k2, v0, v1, v2, bias_ref, acc_ref, dq_ref, dbias_ref,
             s_buf, dp_buf, kt_buf, vt_buf):
        i = pl.program_id(1)
        @pl.when((pl.program_id(0) == 0) & (i == 0))
        def _():
            dbias_ref[...] = jnp.zeros_like(dbias_ref)

        kw = _window((k0, k1, k2))
        kt_buf[...] = kw.T
        vt_buf[...] = _window((v0, v1, v2)).T
        pos = i * t - BAND_HALF + lax.broadcasted_iota(jnp.int32, (1, w), 1)
        valid = (pos >= 0) & (pos < length)
        for h in range(N_HEADS_B):
            cols = slice(HEAD_DIM * h, HEAD_DIM * (h + 1))
            s_buf[h] = _dot(q_ref[:, cols], kt_buf[cols, :])
            dp_buf[h] = _dot(do_ref[:, cols], vt_buf[cols, :])
        for h in range(N_HEADS_B):
            cols = slice(HEAD_DIM * h, HEAD_DIM * (h + 1))
            s = jnp.where(valid, s_buf[h] + bias_ref[h], NEG_BIG)
            p = jnp.exp(s - lse_ref[:, HEAD_DIM * h:HEAD_DIM * h + 1])
            ds = p * (dp_buf[h] - delta_ref[:, HEAD_DIM * h:HEAD_DIM * h + 1])
            dq_ref[:, cols] = acc_ref[:, cols] + _dot(ds.astype(BF16), _head(kw, h)) * SCORE_SCALE
            dbias_ref[h] += ds

    (dq, dbias), rode = _call_with_rider(
        body, f"band_dq_d{dil}", (dil, length // t),
        [main, main, main, main] + win + win + [_full_spec((N_HEADS_B, t, w)), main],
        [main, _full_spec((N_HEADS_B, t, w))],
        [SDS((length, dil * D_B), F32), SDS((N_HEADS_B, t, w), F32)],
        [pltpu.VMEM((N_HEADS_B, t, w), F32), pltpu.VMEM((N_HEADS_B, t, w), F32),
         pltpu.VMEM((D_B, w), BF16), pltpu.VMEM((D_B, w), BF16)],
        (view(q), view(do), view(lse), view(delta), view(k), view(k), view(k), view(v), view(v), view(v), bias,
         view(dq_in)), rider)
    return dq.reshape(s_len, D_B), dbias, rode


def _band_dkv(k, v, q, do, lse, delta, bias_t, dk_in, dv_in, dil, t):
    s_len = q.shape[0]
    length = s_len // dil
    w = t + 2 * BAND_HALF
    view = lambda a: a.reshape(length, dil * D_B)
    main, win = _band_specs(length, t)

    def body(k_ref, v_ref, q0, q1, q2, d0, d1, d2, l0, l1, l2, e0, e1, e2, bias_ref, dk_acc, dv_acc, dk_ref, dv_ref,
             s_buf, dp_buf, kt_buf, vt_buf):
        i = pl.program_id(1)
        qw = _window((q0, q1, q2))
        dow = _window((d0, d1, d2))
        lsew = _window((l0, l1, l2))
        deltaw = _window((e0, e1, e2))
        pos = i * t - BAND_HALF + lax.broadcasted_iota(jnp.int32, (w, 1), 0)
        valid = (pos >= 0) & (pos < length)
        kt_buf[...] = k_ref[...].T
        vt_buf[...] = v_ref[...].T
        for h in range(N_HEADS_B):
            cols = slice(HEAD_DIM * h, HEAD_DIM * (h + 1))
            s_buf[h] = _dot(_head(qw, h), kt_buf[cols, :])
            dp_buf[h] = _dot(_head(dow, h), vt_buf[cols, :])
        qwt = qw.T
        dowt = dow.T
        dkt, dvt = [], []
        for h in range(N_HEADS_B):
            rows = slice(HEAD_DIM * h, HEAD_DIM * (h + 1))
            s = jnp.where(valid, s_buf[h] + bias_ref[h], NEG_BIG)
            p = jnp.exp(s - lsew[:, HEAD_DIM * h:HEAD_DIM * h + 1])
            ds = p * (dp_buf[h] - deltaw[:, HEAD_DIM * h:HEAD_DIM * h + 1])
            dvt.append(_dot(dowt[rows, :], p.astype(BF16)))
            dkt.append(_dot(qwt[rows, :], ds.astype(BF16)))
        dv_ref[...] = dv_acc[...] + jnp.concatenate(dvt, axis=0).T
        dk_ref[...] = dk_acc[...] + jnp.concatenate(dkt, axis=0).T

    dk, dv = pl.pallas_call(
        body, name=f"band_dkv_d{dil}", grid=(dil, length // t),
        in_specs=[main, main] + win + win + win + win + [_full_spec((N_HEADS_B, w, t)), main, main],
        out_specs=[main, main],
        out_shape=(SDS((length, dil * D_B), F32), SDS((length, dil * D_B), F32)),
        scratch_shapes=[pltpu.VMEM((N_HEADS_B, w, t), F32), pltpu.VMEM((N_HEADS_B, w, t), F32),
                        pltpu.VMEM((D_B, t), BF16), pltpu.VMEM((D_B, t), BF16)],
        compiler_params=_params(2, V7X_VMEM_LIMIT),
    )(view(k), view(v), view(q), view(q), view(q), view(do), view(do), view(do), view(lse), view(lse), view(lse),
      view(delta), view(delta), view(delta), bias_t, view(dk_in), view(dv_in))
    return dk.reshape(s_len, D_B), dv.reshape(s_len, D_B)


def _t5_bucket_np(rel):
    nb = N_BUCKETS // 2
    max_exact = nb // 2
    side = np.where(rel > 0, nb, 0)
    n = np.abs(rel)
    ratio = np.maximum(n, max_exact).astype(np.float32) / np.float32(max_exact)
    large = max_exact + (np.log(ratio) / np.float32(math.log(MAX_DISTANCE / max_exact))
                         * np.float32(nb - max_exact)).astype(np.int32)
    large = np.minimum(large, nb - 1)
    return (side + np.where(n < max_exact, n, large)).astype(np.int32)


def _band_buckets(dil, t):
    rel = np.arange(t + 2 * BAND_HALF)[None, :] - BAND_HALF - np.arange(t)[:, None]
    bucket = _t5_bucket_np(np.clip(rel, -BAND_HALF, BAND_HALF) * dil)
    return np.where(np.abs(rel) <= BAND_HALF, bucket, -1).astype(np.int32)


def _toeplitz(vals, rows, cols):
    heads = vals.shape[0]
    period = rows + cols
    vec = jnp.concatenate([vals[:, rows - 1:], jnp.zeros((heads, 1), vals.dtype), vals[:, :rows - 1]], axis=1)
    flat = jnp.broadcast_to(vec[:, None, :], (heads, rows, period)).reshape(heads, rows * period)
    return flat[:, :rows * (period - 1)].reshape(heads, rows, period - 1)[:, :, :cols]


def _bias_tiles(rel_bias, dil, t):
    w = t + 2 * BAND_HALF
    rel = np.arange(-BAND_HALF, BAND_HALF + 1)
    bucket = _t5_bucket_np(rel * dil)
    runs, start = [], 0
    for i in range(1, len(bucket) + 1):
        if i == len(bucket) or bucket[i] != bucket[start]:
            b = int(bucket[start])
            runs.append(jnp.broadcast_to(rel_bias[b:b + 1], (i - start, N_HEADS_B)))
            start = i
    per_rel = jnp.concatenate(runs, axis=0).T

    def diagonals(lo, hi):
        left = jnp.full((N_HEADS_B, max(0, -BAND_HALF - lo)), NEG_BIG, F32)
        right = jnp.full((N_HEADS_B, max(0, hi - BAND_HALF)), NEG_BIG, F32)
        return jnp.concatenate([left, per_rel, right], axis=1)

    tile = _toeplitz(diagonals(-(t - 1) - BAND_HALF, w - 1 - BAND_HALF), t, w)
    twin = _toeplitz(diagonals(-(w - 1) + BAND_HALF, t - 1 + BAND_HALF), w, t)
    return tile, twin


def _dbias_reduce(dbias, buckets):
    n = len(dbias)

    def body(*refs):
        db_refs, bk_refs, o_ref = refs[:n], refs[n:2 * n], refs[2 * n]
        row = lax.broadcasted_iota(jnp.int32, (N_BUCKETS, 128), 0)
        lane = lax.broadcasted_iota(jnp.int32, (N_BUCKETS, 128), 1)

        def per_bucket(b, out):
            for pat in range(n):
                msk = bk_refs[pat][...] == b
                for h in range(N_HEADS_B):
                    tot = jnp.sum(jnp.where(msk, db_refs[pat][h], 0.0), axis=-1, keepdims=True)
                    tot = jnp.sum(tot, axis=0, keepdims=True)
                    out = out + jnp.where((row == b) & (lane == h), tot, 0.0)
            return out

        o_ref[...] = lax.fori_loop(0, N_BUCKETS, per_bucket, jnp.zeros((N_BUCKETS, 128), F32))

    return pl.pallas_call(
        body, name="dbias_reduce", out_shape=SDS((N_BUCKETS, 128), F32),
        compiler_params=pltpu.CompilerParams(vmem_limit_bytes=V7X_VMEM_LIMIT),
    )(*dbias, *buckets)


def _attn_out_fwd(ya, ob, lb, x, g_a, g_b, w_out, g_post, tm):
    s_len = ya.shape[0]

    def body(ya_ref, o0, o1, o2, l0, l1, l2, x_ref, ga_ref, gb_ref, w_ref, gp_ref,
             h1_ref, yo_ref, yb_ref, lse_ref, ycat_ref):
        m = jnp.maximum(jnp.maximum(l0[...], l1[...]), l2[...])
        w0, w1, w2 = jnp.exp(l0[...] - m), jnp.exp(l1[...] - m), jnp.exp(l2[...] - m)
        wsum = w0 + w1 + w2
        yb = (w0 * o0[...] + w1 * o1[...] + w2 * o2[...]) / wsum
        yb_ref[...] = yb
        lse_ref[...] = m + jnp.log(wsum)
        yan, _ = _rms_fwd(ya_ref[...], ga_ref[...])
        ybn, _ = _rms_fwd(yb, gb_ref[...])
        yan, ybn = yan.astype(BF16), ybn.astype(BF16)
        ycat_ref[:, :D_A] = yan
        ycat_ref[:, D_A:] = ybn
        yo = _dot(yan, w_ref[:D_A, :]) + _dot(ybn, w_ref[D_A:, :])
        yo_ref[...] = yo
        post, _ = _rms_fwd(yo, gp_ref[...])
        h1_ref[...] = x_ref[...] + post

    half, full = _row_spec(tm, D_A), _row_spec(tm, D_MODEL)
    return pl.pallas_call(
        body, name="attn_out_fwd", grid=(s_len // tm,),
        in_specs=[half] * 7 + [full, _full_spec((1, D_A)), _full_spec((1, D_B)), _full_spec((D_MODEL, D_MODEL)),
                               _full_spec((1, D_MODEL))],
        out_specs=[full, full, half, half, full],
        out_shape=(SDS((s_len, D_MODEL), F32), SDS((s_len, D_MODEL), F32), SDS((s_len, D_B), F32),
                   SDS((s_len, D_B), F32), SDS((s_len, D_MODEL), BF16)),
        compiler_params=_params(1, V7X_VMEM_LIMIT),
    )(ya, ob[0], ob[1], ob[2], lb[0], lb[1], lb[2], x, g_a, g_b, w_out, g_post)


def _attn_out_bwd(dh1, yo, ya, yb, g_a, g_b, w_out, g_post, tm):
    s_len = ya.shape[0]

    def body(dh1_ref, yo_ref, ya_ref, yb_ref, ga_ref, gb_ref, w_ref, gp_ref,
             dyo_ref, dya_ref, dyb_ref, dela_ref, delb_ref, dgp_ref, dga_ref, dgb_ref):
        _zero_at_first(pl.program_id(0) == 0, dgp_ref, dga_ref, dgb_ref)
        dyo, dgp = _rms_bwd(yo_ref[...], gp_ref[...], dh1_ref[...])
        dyo = dyo.astype(BF16)
        dyo_ref[...] = dyo
        dgp_ref[...] += dgp
        dya_n = _dot_nt(dyo, w_ref[:D_A, :])
        dyb_n = _dot_nt(dyo, w_ref[D_A:, :])
        ya, yb = ya_ref[...], yb_ref[...]
        dya, dga = _rms_bwd(ya, ga_ref[...], dya_n)
        dyb, dgb = _rms_bwd(yb, gb_ref[...], dyb_n)
        dga_ref[...] += dga
        dgb_ref[...] += dgb
        dya_ref[...] = dya.astype(BF16)
        dyb_ref[...] = dyb.astype(BF16)
        dela_ref[...] = _head_sum(dya * ya)
        delb_ref[...] = _head_sum(dyb * yb)

    half, full = _row_spec(tm, D_A), _row_spec(tm, D_MODEL)
    return pl.pallas_call(
        body, name="attn_out_bwd", grid=(s_len // tm,),
        in_specs=[full, full, half, half, _full_spec((1, D_A)), _full_spec((1, D_B)),
                  _full_spec((D_MODEL, D_MODEL)), _full_spec((1, D_MODEL))],
        out_specs=[full, half, half, half, half, _full_spec((1, D_MODEL)), _full_spec((1, D_A)), _full_spec((1, D_B))],
        out_shape=(SDS((s_len, D_MODEL), BF16), SDS((s_len, D_A), BF16), SDS((s_len, D_B), BF16),
                   SDS((s_len, D_A), F32), SDS((s_len, D_B), F32), SDS((1, D_MODEL), F32), SDS((1, D_A), F32),
                   SDS((1, D_B), F32)),
        compiler_params=_params(1, V7X_VMEM_LIMIT),
    )(dh1, yo, ya, yb, g_a, g_b, w_out, g_post)


FF_CHUNK = 1024


def _mlp_fwd(h1, g_pre, w1, w2, g_post, tm):
    s_len = h1.shape[0]

    def body(h1_ref, gpre_ref, w1_ref, w2_ref, gpost_ref, h2_ref, fo_ref, xn_ref):
        h1v = h1_ref[...]
        xn, _ = _rms_fwd(h1v, gpre_ref[...])
        xn = xn.astype(BF16)
        xn_ref[...] = xn
        fo = jnp.zeros((tm, D_MODEL), F32)
        for c in range(D_FF // FF_CHUNK):
            cols = slice(c * FF_CHUNK, (c + 1) * FF_CHUNK)
            u = jnp.maximum(_dot(xn, w1_ref[:, cols]), 0.0)
            fo = fo + _dot((u * u).astype(BF16), w2_ref[cols, :])
        fo_ref[...] = fo
        post, _ = _rms_fwd(fo, gpost_ref[...])
        h2_ref[...] = h1v + post

    full = _row_spec(tm, D_MODEL)
    return pl.pallas_call(
        body, name="mlp_fwd", grid=(s_len // tm,),
        in_specs=[full, _full_spec((1, D_MODEL)), _const_spec((D_MODEL, D_FF)), _const_spec((D_FF, D_MODEL)),
                  _full_spec((1, D_MODEL))],
        out_specs=[full, full, full],
        out_shape=(SDS((s_len, D_MODEL), F32), SDS((s_len, D_MODEL), F32), SDS((s_len, D_MODEL), BF16)),
        compiler_params=_params(1, V7X_VMEM_LIMIT),
    )(h1, g_pre, w1, w2, g_post)


def _mlp_bwd(h1, dh2, fo, xn, g_pre, w1, w2, g_post, tm):
    s_len = h1.shape[0]

    def body(h1_ref, dh2_ref, fo_ref, xn_ref, gpre_ref, w1_ref, w2_ref, gpost_ref,
             dh1_ref, dfo_ref, du_ref, f_ref, dgpost_ref, dgpre_ref):
        _zero_at_first(pl.program_id(0) == 0, dgpost_ref, dgpre_ref)
        dh2 = dh2_ref[...]
        dfo, dgpost = _rms_bwd(fo_ref[...], gpost_ref[...], dh2)
        dfo = dfo.astype(BF16)
        dfo_ref[...] = dfo
        dgpost_ref[...] += dgpost
        xn = xn_ref[...]
        dxn = jnp.zeros((tm, D_MODEL), F32)
        for c in range(D_FF // FF_CHUNK):
            cols = slice(c * FF_CHUNK, (c + 1) * FF_CHUNK)
            u = jnp.maximum(_dot(xn, w1_ref[:, cols]), 0.0)
            f_ref[:, cols] = (u * u).astype(BF16)
            du = (_dot_nt(dfo, w2_ref[cols, :]) * (2.0 * u)).astype(BF16)
            du_ref[:, cols] = du
            dxn = dxn + _dot_nt(du, w1_ref[:, cols])
        dx, dgpre = _rms_bwd(h1_ref[...], gpre_ref[...], dxn)
        dgpre_ref[...] += dgpre
        dh1_ref[...] = dh2 + dx

    full, wide = _row_spec(tm, D_MODEL), _row_spec(tm, D_FF)
    return pl.pallas_call(
        body, name="mlp_bwd", grid=(s_len // tm,),
        in_specs=[full, full, full, full, _full_spec((1, D_MODEL)), _const_spec((D_MODEL, D_FF)),
                  _const_spec((D_FF, D_MODEL)), _full_spec((1, D_MODEL))],
        out_specs=[full, full, wide, wide, _full_spec((1, D_MODEL)), _full_spec((1, D_MODEL))],
        out_shape=(SDS((s_len, D_MODEL), F32), SDS((s_len, D_MODEL), BF16), SDS((s_len, D_FF), BF16),
                   SDS((s_len, D_FF), BF16), SDS((1, D_MODEL), F32), SDS((1, D_MODEL), F32)),
        compiler_params=_params(1, V7X_VMEM_LIMIT),
    )(h1, dh2, fo, xn, g_pre, w1, w2, g_post)


def _ple_fwd_bwd(h2, p, target, g_ple, w_gate, w_proj, tm):
    s_len = h2.shape[0]

    def body(h2_ref, p_ref, t_ref, g_ref, wg_ref, wp_ref, dh2_ref, loss_ref, dg_ref, xn_ref, dgl_ref, dpp_ref, pb_ref):
        _zero_at_first(pl.program_id(0) == 0, loss_ref, dg_ref)
        h2 = h2_ref[...]
        g = g_ref[...]
        xn, _ = _rms_fwd(h2, g)
        xn = xn.astype(BF16)
        xn_ref[...] = xn
        gate = 1.0 / (1.0 + jnp.exp(-_dot(xn, wg_ref[...])))
        pb = p_ref[...].astype(BF16)
        pb_ref[...] = pb
        pp = _dot(pb, wp_ref[...])
        diff = h2 + gate * pp - t_ref[...]
        loss_ref[...] += jnp.full((8, 128), jnp.sum(diff * diff), F32)
        dh3 = diff * (1.0 / D_MODEL)
        dpp_ref[...] = (dh3 * gate).astype(BF16)
        dgl = (dh3 * pp * gate * (1.0 - gate)).astype(BF16)
        dgl_ref[...] = dgl
        dx, dg = _rms_bwd(h2, g, _dot_nt(dgl, wg_ref[...]))
        dg_ref[...] += dg
        dh2_ref[...] = dh3 + dx

    full = _row_spec(tm, D_MODEL)
    return pl.pallas_call(
        body, name="ple_fwd_bwd", grid=(s_len // tm,),
        in_specs=[full, _row_spec(tm, D_PLE), full, _full_spec((1, D_MODEL)), _full_spec((D_MODEL, D_MODEL)),
                  _full_spec((D_PLE, D_MODEL))],
        out_specs=[full, _full_spec((8, 128)), _full_spec((1, D_MODEL)), full, full, full, _row_spec(tm, D_PLE)],
        out_shape=(SDS((s_len, D_MODEL), F32), SDS((8, 128), F32), SDS((1, D_MODEL), F32), SDS((s_len, D_MODEL), BF16),
                   SDS((s_len, D_MODEL), BF16), SDS((s_len, D_MODEL), BF16), SDS((s_len, D_PLE), BF16)),
        compiler_params=_params(1, V7X_VMEM_LIMIT),
    )(h2, p, target, g_ple, w_gate, w_proj)


def _inproj_bwd(dqs, dkn, dva, dqb, dkb, dvb, qa_raw, ka_raw, x, dh1, g_pre, w_in, cos, sin, gq, gk, tm):
    s_len = x.shape[0]

    def body(dqs_ref, dkn_ref, dva_ref, dqb_ref, dkb_ref, dvb_ref, qa_ref, ka_ref, x_ref, dh1_ref, g_ref, w_ref,
             cos_ref, sin_ref, gq_ref, gk_ref, dx_ref, dproj_ref, dg_ref, dgq_ref, dgk_ref):
        _zero_at_first(pl.program_id(0) == 0, dg_ref, dgq_ref, dgk_ref)
        dqa, dgq = _head_norm_rope_bwd(qa_ref[...], gq_ref[...], cos_ref[...], sin_ref[...], dqs_ref[...])
        dka, dgk = _head_norm_rope_bwd(ka_ref[...], gk_ref[...], cos_ref[:, :D_KV_A], sin_ref[:, :D_KV_A], dkn_ref[...])
        dgq_ref[...] += dgq
        dgk_ref[...] += dgk
        dproj_ref[:, OFF_QA:OFF_KA] = dqa.astype(BF16)
        dproj_ref[:, OFF_KA:OFF_VA] = dka.astype(BF16)
        dproj_ref[:, OFF_VA:OFF_QB] = dva_ref[...].astype(BF16)
        dproj_ref[:, OFF_QB:OFF_KB] = dqb_ref[...].astype(BF16)
        dproj_ref[:, OFF_KB:OFF_VB] = dkb_ref[...].astype(BF16)
        dproj_ref[:, OFF_VB:D_IN] = dvb_ref[...].astype(BF16)
        dxn = _dot_nt(dproj_ref[...], w_ref[...])
        dx, dg = _rms_bwd(x_ref[...], g_ref[...], dxn)
        dg_ref[...] += dg
        dx_ref[...] = dh1_ref[...] + dx

    half, kvw, full = _row_spec(tm, D_A), _row_spec(tm, D_KV_A), _row_spec(tm, D_MODEL)
    return pl.pallas_call(
        body, name="inproj_bwd", grid=(s_len // tm,),
        in_specs=[half, kvw, kvw, half, half, half, half, kvw, full, full, _full_spec((1, D_MODEL)),
                  _const_spec((D_MODEL, D_IN)), half, half, _full_spec((1, D_A)), _full_spec((1, D_KV_A))],
        out_specs=[full, _row_spec(tm, D_IN), _full_spec((1, D_MODEL)), _full_spec((1, D_A)), _full_spec((1, D_KV_A))],
        out_shape=(SDS((s_len, D_MODEL), F32), SDS((s_len, D_IN), BF16), SDS((1, D_MODEL), F32), SDS((1, D_A), F32),
                   SDS((1, D_KV_A), F32)),
        compiler_params=_params(1, V7X_VMEM_LIMIT),
    )(dqs, dkn, dva, dqb, dkb, dvb, qa_raw, ka_raw, x, dh1, g_pre, w_in, cos, sin, gq, gk)


def _weight_grad(a, b, name, tk1, tn, tm):
    s_len, k1 = a.shape
    n = b.shape[1]
    steps = s_len // tm

    def body(a_ref, b_ref, o_ref, acc):
        r = pl.program_id(2)
        _zero_at_first(r == 0, acc)
        acc[...] += _dot_tn(a_ref[...], b_ref[...])

        @pl.when(r == steps - 1)
        def _():
            o_ref[...] = acc[...].astype(BF16)

    return pl.pallas_call(
        body, name=name, grid=(k1 // tk1, n // tn, steps),
        in_specs=[pl.BlockSpec((tm, tk1), lambda i, j, r: (r, i)), pl.BlockSpec((tm, tn), lambda i, j, r: (r, j))],
        out_specs=pl.BlockSpec((tk1, tn), lambda i, j, r: (i, j)),
        out_shape=SDS((k1, n), BF16),
        scratch_shapes=[pltpu.VMEM((tk1, tn), F32)],
        compiler_params=_params(3, V7X_VMEM_LIMIT),
    )(a, b)


def _my_index():
    return 4 * lax.axis_index("x") + 2 * lax.axis_index("y") + lax.axis_index("c")


def _peer(k):
    f = k + 1
    x, y, c = lax.axis_index("x"), lax.axis_index("y"), lax.axis_index("c")
    return (x ^ ((f >> 2) & 1), y ^ ((f >> 1) & 1), c ^ (f & 1))


GATHER, EXCHANGE = "gather", "exchange"
ANY_SPEC = pl.BlockSpec(memory_space=pl.ANY)


def _comm_out_shapes(kind, arrays):
    return [SDS((N_DEV,) + a.shape if kind == GATHER else a.shape, a.dtype) for a in arrays]


def _comm_sems(n):
    return [pltpu.SemaphoreType.DMA((n, N_DEV - 1)), pltpu.SemaphoreType.DMA((n, N_DEV - 1)), pltpu.SemaphoreType.DMA((n,))]


def _comm_copies(kind, ins, outs, send_sems, recv_sems, local_sems):
    me = _my_index()
    local, remote = [], []
    for a, (src, dst) in enumerate(zip(ins, outs)):
        local.append(pltpu.make_async_copy(src if kind == GATHER else src.at[me], dst.at[me], local_sems.at[a]))
        for k in range(N_DEV - 1):
            px, py, pc = _peer(k)
            remote.append(pltpu.make_async_remote_copy(
                src_ref=src if kind == GATHER else src.at[4 * px + 2 * py + pc], dst_ref=dst.at[me],
                send_sem=send_sems.at[a, k], recv_sem=recv_sems.at[a, k],
                device_id=(px, py, pc), device_id_type=pl.DeviceIdType.MESH))
    return local, remote


def _comm_start(kind, ins, outs, sems):
    local, remote = _comm_copies(kind, ins, outs, *sems)
    for cp in local + remote:
        cp.start()


def _comm_wait(kind, ins, outs, sems):
    local, remote = _comm_copies(kind, ins, outs, *sems)
    for cp in local:
        cp.wait()
    for cp in remote:
        cp.wait_send()
    for cp in remote:
        cp.wait_recv()


def _collective(kind, arrays, name):
    n = len(arrays)

    def body(*refs):
        ins, outs, sems = refs[:n], refs[n:2 * n], refs[2 * n:]
        _comm_start(kind, ins, outs, sems)
        _comm_wait(kind, ins, outs, sems)

    return pl.pallas_call(
        body, name=name, in_specs=[ANY_SPEC] * n, out_specs=[ANY_SPEC] * n,
        out_shape=_comm_out_shapes(kind, arrays), scratch_shapes=_comm_sems(n),
    )(*arrays)


def _sum_adamw(parts, w, m, v, name, tr):
    rows, cols = w.shape
    c1 = 1.0 / (1.0 - ADAM_B1 ** ADAM_STEP)
    c2 = 1.0 / (1.0 - ADAM_B2 ** ADAM_STEP)

    def body(p_ref, w_ref, m_ref, v_ref, g_ref, d_ref, nm_ref, nv_ref):
        g = p_ref[0].astype(F32)
        for j in range(1, N_DEV):
            g = g + p_ref[j].astype(F32)
        g_ref[...] = g
        nm = ADAM_B1 * m_ref[...] + (1.0 - ADAM_B1) * g
        nv = ADAM_B2 * v_ref[...] + (1.0 - ADAM_B2) * (g * g)
        nm_ref[...] = nm
        nv_ref[...] = nv
        d_ref[...] = -ADAM_LR * ((nm * c1) / (jnp.sqrt(nv * c2) + ADAM_EPS) + ADAM_WD * w_ref[...])

    blk = pl.BlockSpec((tr, cols), lambda i: (i, 0))
    return pl.pallas_call(
        body, name=name, grid=(rows // tr,),
        in_specs=[pl.BlockSpec((N_DEV, tr, cols), lambda i: (0, i, 0)), blk, blk, blk],
        out_specs=[blk] * 4, out_shape=[SDS((rows, cols), F32)] * 4,
        compiler_params=_params(1, V7X_VMEM_LIMIT),
    )(parts, w, m, v)


_SMALL = (("g_attn_pre", 1024), ("g_q", 64), ("g_k", 64), ("g_out_a", 512), ("g_out_b", 512), ("g_attn_post", 1024),
          ("rel_bias", 256), ("g_mlp_pre", 1024), ("g_mlp_post", 1024), ("g_ple", 1024))
_SLAB_ROWS = 56


def _pack_small(vals):
    rows = []
    for (name, size) in _SMALL:
        flat = vals[name].reshape(-1).astype(F32)
        padded = -(-size // 128) * 128
        rows.append(jnp.pad(flat, (0, padded - size)).reshape(padded // 128, 128))
    slab = jnp.concatenate(rows, axis=0)
    return jnp.pad(slab, ((0, _SLAB_ROWS - slab.shape[0]), (0, 0)))


def _unpack_small(slab, shapes):
    out, row = {}, 0
    for (name, size) in _SMALL:
        nrow = -(-size // 128)
        out[name] = slab[row:row + nrow].reshape(-1)[:size].reshape(shapes[name])
        row += nrow
    return out


def _rope_tables(s_len):
    rows = s_len // GRID_W
    row = jnp.broadcast_to(jnp.arange(rows)[:, None], (rows, GRID_W)).reshape(-1).astype(F32)
    col = jnp.broadcast_to(jnp.arange(GRID_W)[None, :], (rows, GRID_W)).reshape(-1).astype(F32)
    n_axis = ROPE_HALF // 2
    inv_freq = ROPE_THETA ** (-jnp.arange(n_axis, dtype=F32) / n_axis)
    ang = jnp.concatenate([row[:, None] * inv_freq, col[:, None] * inv_freq], axis=-1)
    cos, sin = jnp.cos(ang), jnp.sin(ang)
    cos = jnp.tile(jnp.concatenate([cos, cos], axis=-1), (1, N_HEADS_A))
    sin = jnp.tile(jnp.concatenate([-sin, sin], axis=-1), (1, N_HEADS_A))
    return cos, sin


_RIDERS = {1: ("w_out", "w_ff1"), 4: ("w_ff2",), 16: ("w_ple_gate", "w_ple_proj")}
_GRAD_RIDERS = {1: ("w_ff1",), 4: ("w_ff2",), 16: ("w_ple_gate", "w_ple_proj", "w_out")}


def _local_step(x, p, target, w_in, shards, small):
    axis_of = dict(_BIG)
    s_len = x.shape[0]
    tm = min(256, s_len)
    tq = min(128, s_len)
    tk = min(2048, s_len // 2)
    cos, sin = _rope_tables(s_len)
    gq = jnp.tile(small["g_q"], (1, N_HEADS_A))
    gk = jnp.tile(small["g_k"], (1, N_KV_A))

    qa_raw, ka_raw, qs, kn, va, qb, kb, vb, xn1 = _inproj_fwd(x, small["g_attn_pre"], w_in, cos, sin, gq, gk, tm)

    def kv_major(a):
        return a.reshape(s_len, N_KV_A, HEAD_DIM).transpose(1, 0, 2)

    def kv_chunks_t(a):
        return a.reshape(s_len // tk, tk, N_KV_A, HEAD_DIM).transpose(2, 0, 3, 1)

    def kv_unchunk(a):
        return a.transpose(1, 3, 0, 2).reshape(s_len, D_KV_A)

    k_maj, kt, vt = kv_major(kn), kv_chunks_t(kn), kv_chunks_t(va)
    vt_ones = jnp.concatenate([vt, jnp.ones((N_KV_A, s_len // tk, 16, tk), BF16)], axis=2)
    ya, lse_a = _attn_fwd(qs, k_maj, vt_ones, tq, tk)

    ob, lb, tiles, full = [], [], [], {}
    for (_, dil) in DILATED_PATTERNS:
        t = min(256, s_len // dil)
        bias, bias_t = _bias_tiles(small["rel_bias"], dil, t)
        tiles.append((t, bias, bias_t, _band_buckets(dil, t)))
        o, l, gathered = _band_fwd(qb, kb, vb, bias, dil, t, (GATHER, [shards[n] for n in _RIDERS[dil]]))
        ob.append(o)
        lb.append(l)
        full.update({n: _assemble(g, axis_of[n]) for n, g in zip(_RIDERS[dil], gathered)})
    w_out, w_ff1, w_ff2, w_gate, w_proj = (full[n] for n in ("w_out", "w_ff1", "w_ff2", "w_ple_gate", "w_ple_proj"))

    h1, yo, yb, lse_b, ycat = _attn_out_fwd(ya, ob, lb, x, small["g_out_a"], small["g_out_b"], w_out,
                                            small["g_attn_post"], tm)
    h2, fo, xn2 = _mlp_fwd(h1, small["g_mlp_pre"], w_ff1, w_ff2, small["g_mlp_post"], tm)
    dh2, loss_part, dg_ple, xn3, dgl, dpp, pb = _ple_fwd_bwd(h2, p, target, small["g_ple"], w_gate, w_proj, tm)
    dh1, dfo, du, f, dg_mlp_post, dg_mlp_pre = _mlp_bwd(h1, dh2, fo, xn2, small["g_mlp_pre"], w_ff1, w_ff2,
                                                          small["g_mlp_post"], tm)
    dyo, dya, dyb, delta_a, delta_b, dg_attn_post, dg_out_a, dg_out_b = _attn_out_bwd(
        dh1, yo, ya, yb, small["g_out_a"], small["g_out_b"], w_out, small["g_attn_post"], tm)

    dqs, dk_t, dv_t = _attn_bwd(qs, dya, lse_a, delta_a, kt, k_maj, vt, tq, tk)
    dkn, dva = kv_unchunk(dk_t), kv_unchunk(dv_t)

    tg = min(512, s_len)
    grads = {
        "w_out": _weight_grad(ycat, dyo, "grad_w_out", D_MODEL, D_MODEL, tg),
        "w_ff1": _weight_grad(xn2, du, "grad_w_ff1", D_MODEL, 1024, tg),
        "w_ff2": _weight_grad(f, dfo, "grad_w_ff2", 1024, D_MODEL, tg),
        "w_ple_gate": _weight_grad(xn3, dgl, "grad_w_ple_gate", D_MODEL, D_MODEL, tg),
        "w_ple_proj": _weight_grad(pb, dpp, "grad_w_ple_proj", D_PLE, D_MODEL, tg),
    }

    dqb = dkb = dvb = jnp.zeros((s_len, D_B), F32)
    dbias, received = [], {}
    for (_, dil), (t, bias, bias_t, _) in zip(DILATED_PATTERNS, tiles):
        rider = (EXCHANGE, [_cut(grads[n], axis_of[n]) for n in _GRAD_RIDERS[dil]])
        dqb, db, got = _band_dq(qb, dyb, lse_b, delta_b, kb, vb, bias, dqb, dil, t, rider)
        dkb, dvb = _band_dkv(kb, vb, qb, dyb, lse_b, delta_b, bias_t, dkb, dvb, dil, t)
        dbias.append(db)
        received.update(zip(_GRAD_RIDERS[dil], got))
    d_rel = _dbias_reduce(dbias, [jnp.asarray(tl[3]) for tl in tiles])[:, :N_HEADS_B]

    dx, dproj, dg_attn_pre, dgq_lanes, dgk_lanes = _inproj_bwd(
        dqs, dkn, dva, dqb, dkb, dvb, qa_raw, ka_raw, x, dh1, small["g_attn_pre"], w_in, cos, sin, gq, gk, tm)
    grad_w_in = _weight_grad(xn1, dproj, "grad_w_in", D_MODEL, 768, tg)
    small_grads = {
        "g_attn_pre": dg_attn_pre, "g_q": dgq_lanes.reshape(N_HEADS_A, HEAD_DIM).sum(0, keepdims=True),
        "g_k": dgk_lanes.reshape(N_KV_A, HEAD_DIM).sum(0, keepdims=True), "g_out_a": dg_out_a, "g_out_b": dg_out_b,
        "g_attn_post": dg_attn_post, "rel_bias": d_rel, "g_mlp_pre": dg_mlp_pre, "g_mlp_post": dg_mlp_post,
        "g_ple": dg_ple,
    }
    return loss_part[0, 0], dx, grad_w_in, received, small_grads


_BIG = (("w_in", 1), ("w_out", 0), ("w_ff1", 1), ("w_ff2", 0), ("w_ple_gate", 0), ("w_ple_proj", 1))


def _assemble(gathered, axis):
    if axis == 0:
        return gathered.reshape(-1, gathered.shape[2])
    return gathered.transpose(1, 0, 2).reshape(gathered.shape[1], -1)


def _cut(full, axis):
    if axis == 0:
        return full.reshape(N_DEV, full.shape[0] // N_DEV, full.shape[1])
    return full.reshape(full.shape[0], N_DEV, full.shape[1] // N_DEV).transpose(1, 0, 2)


def kernel(x, p, w_in, g_attn_pre, g_q, g_k, g_out_a, g_out_b, w_out, g_attn_post, rel_bias, g_mlp_pre, w_ff1, w_ff2, g_mlp_post, g_ple, w_ple_gate, w_ple_proj, loss_target, m_w_in, m_g_attn_pre, m_g_q, m_g_k, m_g_out_a, m_g_out_b, m_w_out, m_g_attn_post, m_rel_bias, m_g_mlp_pre, m_w_ff1, m_w_ff2, m_g_mlp_post, m_g_ple, m_w_ple_gate, m_w_ple_proj, v_w_in, v_g_attn_pre, v_g_q, v_g_k, v_g_out_a, v_g_out_b, v_w_out, v_g_attn_post, v_rel_bias, v_g_mlp_pre, v_w_ff1, v_w_ff2, v_g_mlp_post, v_g_ple, v_w_ple_gate, v_w_ple_proj):
    given = dict(locals())
    small_names = [n for n, _ in _SMALL]
    small = {n: given[n] for n in small_names}
    shards = {n: given[n][0] for n, _ in _BIG}

    bf16_shards = {n: shards[n].astype(BF16) for n, _ in _BIG}
    (gathered_w_in,) = _collective(GATHER, [bf16_shards["w_in"]], "gather_w_in")

    loss_part, dx, grad_w_in, received, small_grads = _local_step(
        x[0], p[0, 0], loss_target[0], _assemble(gathered_w_in, 1), bf16_shards, small)

    slab = _pack_small(small_grads)
    slab_parts = jnp.broadcast_to(slab[None], (N_DEV,) + slab.shape)
    received["w_in"], slabs = _collective(EXCHANGE, [_cut(grad_w_in, 1), slab_parts], "exchange_w_in_grads")

    out_g, out_d, out_m, out_v = {}, {}, {}, {}
    for n, _ in _BIG:
        rows = shards[n].shape[0]
        g, d, nm, nv = _sum_adamw(received[n], shards[n], given["m_" + n][0], given["v_" + n][0], "adamw_" + n,
                                  min(rows, 128))
        out_g[n], out_d[n], out_m[n], out_v[n] = g[None], d[None], nm[None], nv[None]
    g, d, nm, nv = _sum_adamw(slabs, _pack_small(small), _pack_small({n: given["m_" + n] for n in small_names}),
                              _pack_small({n: given["v_" + n] for n in small_names}), "adamw_small", _SLAB_ROWS)
    shapes = {n: given[n].shape for n in small_names}
    for dst, slab_out in ((out_g, g), (out_d, d), (out_m, nm), (out_v, nv)):
        dst.update(_unpack_small(slab_out, shapes))

    loss = lax.psum(loss_part * (0.5 / D_MODEL), MESH_AXES)
    order = ["w_in", "g_attn_pre", "g_q", "g_k", "g_out_a", "g_out_b", "w_out", "g_attn_post", "rel_bias", "g_mlp_pre",
             "w_ff1", "w_ff2", "g_mlp_post", "g_ple", "w_ple_gate", "w_ple_proj"]
    return (loss, dx[None], *[out_g[n] for n in order], *[out_d[n] for n in order], *[out_m[n] for n in order],
            *[out_v[n] for n in order])
```

```python
import functools
import math

import jax
import jax.numpy as jnp
import numpy as np
from jax import lax
from jax.experimental import pallas as pl
from jax.experimental.pallas import tpu as pltpu

F32 = jnp.float32
BF16 = jnp.bfloat16
SDS = jax.ShapeDtypeStruct

D_MODEL = 1024
HEAD_DIM = 64
N_HEADS_A = 8
N_KV_A = 2
GROUP_A = N_HEADS_A // N_KV_A
N_HEADS_B = 8
D_A = N_HEADS_A * HEAD_DIM
D_KV_A = N_KV_A * HEAD_DIM
D_B = N_HEADS_B * HEAD_DIM
D_IN = D_A + 2 * D_KV_A + 3 * D_B
D_FF = 4 * D_MODEL
D_PLE = 256
GRID_W = 64
ROPE_THETA = 10000.0
ROPE_HALF = HEAD_DIM // 2
DILATED_PATTERNS = ((128, 1), (512, 4), (2048, 16))
BAND_HALF = 64
N_BUCKETS = 32
MAX_DISTANCE = 1024
EPS = 1e-6
NEG_BIG = -1e30
SCORE_SCALE = HEAD_DIM ** -0.5

ADAM_LR = 0.001
ADAM_B1 = 0.9
ADAM_B2 = 0.999
ADAM_EPS = 1e-08
ADAM_WD = 0.01
ADAM_STEP = 10

N_DEV = 8
MESH_AXES = ("x", "y", "c")
V7X_VMEM_LIMIT = 56 * 1024 * 1024

OFF_QA, OFF_KA, OFF_VA, OFF_QB, OFF_KB, OFF_VB = 0, 512, 640, 768, 1280, 1792


def _params(n_axes, vmem=None):
    return pltpu.CompilerParams(dimension_semantics=("arbitrary",) * n_axes, vmem_limit_bytes=vmem)


def _dot(a, b):
    return jnp.dot(a, b, preferred_element_type=F32)


def _dot_nt(a, b):
    return lax.dot_general(a, b, (((1,), (1,)), ((), ())), preferred_element_type=F32)


def _dot_tn(a, b):
    return lax.dot_general(a, b, (((0,), (0,)), ((), ())), preferred_element_type=F32)


def _rms_fwd(x, g):
    r = lax.rsqrt(jnp.mean(x * x, axis=-1, keepdims=True) + EPS)
    return x * r * g, r


def _rms_bwd(x, g, dy):
    r = lax.rsqrt(jnp.mean(x * x, axis=-1, keepdims=True) + EPS)
    xh = x * r
    dxh = dy * g
    dx = r * (dxh - xh * jnp.mean(dxh * xh, axis=-1, keepdims=True))
    return dx, jnp.sum(dy * xh, axis=0, keepdims=True)


def _head_sum(v):
    head = lax.broadcasted_iota(jnp.int32, v.shape, 1) >> 6
    out = jnp.zeros_like(v)
    for h in range(v.shape[1] // HEAD_DIM):
        msk = head == h
        s = jnp.sum(jnp.where(msk, v, 0.0), axis=-1, keepdims=True)
        out = jnp.where(msk, s, out)
    return out


def _swap_halves(v):
    w = v.shape[1]
    lane = lax.broadcasted_iota(jnp.int32, v.shape, 1)
    first_half = (lane & (HEAD_DIM - 1)) < ROPE_HALF
    return jnp.where(first_half, pltpu.roll(v, w - ROPE_HALF, 1), pltpu.roll(v, ROPE_HALF, 1))


def _head_norm_rope(v, g, cos, sin_signed):
    r = lax.rsqrt(_head_sum(v * v) * (1.0 / HEAD_DIM) + EPS)
    y = v * r * g
    return y * cos + _swap_halves(y) * sin_signed


def _head_norm_rope_bwd(v, g, cos, sin_signed, dout):
    dy = dout * cos - _swap_halves(dout) * sin_signed
    r = lax.rsqrt(_head_sum(v * v) * (1.0 / HEAD_DIM) + EPS)
    xh = v * r
    dxh = dy * g
    dv = r * (dxh - xh * (_head_sum(dxh * xh) * (1.0 / HEAD_DIM)))
    return dv, jnp.sum(dy * xh, axis=0, keepdims=True)


def _row_spec(tm, n):
    return pl.BlockSpec((tm, n), lambda i: (i, 0))


def _full_spec(shape):
    nd = len(shape)
    return pl.BlockSpec(shape, lambda *_: (0,) * nd)


def _const_spec(shape):
    nd = len(shape)
    return pl.BlockSpec(shape, lambda *_: (0,) * nd, pipeline_mode=pl.Buffered(1))


def _zero_at_first(first, *refs):
    @pl.when(first)
    def _():
        for ref in refs:
            ref[...] = jnp.zeros_like(ref)


def _inproj_fwd(x, g_pre, w_in, cos, sin, gq, gk, tm):
    s_len = x.shape[0]

    def body(x_ref, g_ref, w_ref, cos_ref, sin_ref, gq_ref, gk_ref,
             qa_raw, ka_raw, qs, kn, va, qb, kb, vb, xn_out):
        xn, _ = _rms_fwd(x_ref[...], g_ref[...])
        xn = xn.astype(BF16)
        xn_out[...] = xn
        qa = _dot(xn, w_ref[:, OFF_QA:OFF_KA])
        qa_raw[...] = qa
        cos, sin = cos_ref[...], sin_ref[...]
        qs[...] = (_head_norm_rope(qa, gq_ref[...], jnp.tile(cos, (1, D_A // D_KV_A)), jnp.tile(sin, (1, D_A // D_KV_A)))
                   * SCORE_SCALE).astype(BF16)
        ka = _dot(xn, w_ref[:, OFF_KA:OFF_VA])
        ka_raw[...] = ka
        kn[...] = _head_norm_rope(ka, gk_ref[...], cos, sin).astype(BF16)
        va[...] = _dot(xn, w_ref[:, OFF_VA:OFF_QB]).astype(BF16)
        qb[...] = (_dot(xn, w_ref[:, OFF_QB:OFF_KB]) * SCORE_SCALE).astype(BF16)
        kb[...] = _dot(xn, w_ref[:, OFF_KB:OFF_VB]).astype(BF16)
        vb[...] = _dot(xn, w_ref[:, OFF_VB:D_IN]).astype(BF16)

    return pl.pallas_call(
        body, name="inproj_fwd", grid=(s_len // tm,),
        in_specs=[_row_spec(tm, D_MODEL), _full_spec((1, D_MODEL)), _const_spec((D_MODEL, D_IN)),
                  _row_spec(tm, D_KV_A), _row_spec(tm, D_KV_A), _full_spec((1, D_A)), _full_spec((1, D_KV_A))],
        out_specs=[_row_spec(tm, D_A), _row_spec(tm, D_KV_A), _row_spec(tm, D_A), _row_spec(tm, D_KV_A),
                   _row_spec(tm, D_KV_A), _row_spec(tm, D_B), _row_spec(tm, D_B), _row_spec(tm, D_B),
                   _row_spec(tm, D_MODEL)],
        out_shape=(SDS((s_len, D_A), F32), SDS((s_len, D_KV_A), F32), SDS((s_len, D_A), BF16),
                   SDS((s_len, D_KV_A), BF16), SDS((s_len, D_KV_A), BF16), SDS((s_len, D_B), BF16),
                   SDS((s_len, D_B), BF16), SDS((s_len, D_B), BF16), SDS((s_len, D_MODEL), BF16)),
        compiler_params=_params(1, V7X_VMEM_LIMIT),
    )(x, g_pre, w_in, cos, sin, gq, gk)


def _stack_heads(ref, tq):
    return jnp.concatenate([ref[:, HEAD_DIM * g:HEAD_DIM * (g + 1)] for g in range(GROUP_A)], axis=0)


def _stack_cols(ref, tq):
    return jnp.concatenate([ref[:, HEAD_DIM * g:HEAD_DIM * g + 1] for g in range(GROUP_A)], axis=0)


def _attn_fwd(qs, k, vt, tq, tk):
    s_len = qs.shape[0]
    nk = s_len // tk
    assert nk % 2 == 0
    gw = GROUP_A * HEAD_DIM
    rows = GROUP_A * tq
    vrows = vt.shape[2]

    def body(q_ref, k_ref, vt_ref, o_ref, lse_ref, s_buf):
        qt = _stack_heads(q_ref, tq).T

        def scores(j, slot):
            kj = k_ref[pl.ds(pl.multiple_of(j * tk, tk), tk), :]
            s_buf[slot] = _dot(kj, qt)

        def consume(j, slot, carry):
            m, acc = carry
            st = s_buf[slot]
            m_new = jnp.maximum(m, jnp.max(st, axis=0, keepdims=True))
            pt = jnp.exp(st - m_new)
            acc = jnp.exp(m - m_new) * acc + _dot(vt_ref[j], pt.astype(BF16))
            return m_new, acc

        scores(0, 0)

        def pair(j, carry, more):
            scores(j + 1, 1)
            carry = consume(j, 0, carry)
            if more:
                scores(j + 2, 0)
            return consume(j + 1, 1, carry)

        carry = (jnp.full((1, rows), NEG_BIG, F32), jnp.zeros((vrows, rows), F32))
        carry = lax.fori_loop(0, nk // 2 - 1, lambda jj, c: pair(2 * jj, c, True), carry)
        m, acc = pair(nk - 2, carry, False)
        l = acc[HEAD_DIM:HEAD_DIM + 1]
        o = (acc[:HEAD_DIM] / l).T
        lse = jnp.broadcast_to(m + jnp.log(l), (HEAD_DIM, rows)).T
        for g in range(GROUP_A):
            o_ref[:, HEAD_DIM * g:HEAD_DIM * (g + 1)] = o[g * tq:(g + 1) * tq]
            lse_ref[:, HEAD_DIM * g:HEAD_DIM * (g + 1)] = lse[g * tq:(g + 1) * tq]

    tile = pl.BlockSpec((tq, gw), lambda kv, i: (i, kv))
    return pl.pallas_call(
        body, name="attn_fwd", grid=(N_KV_A, s_len // tq),
        in_specs=[tile, pl.BlockSpec((None, s_len, HEAD_DIM), lambda kv, i: (kv, 0, 0)),
                  pl.BlockSpec((None, nk, vrows, tk), lambda kv, i: (kv, 0, 0, 0))],
        out_specs=[tile, tile],
        out_shape=(SDS((s_len, D_A), F32), SDS((s_len, D_A), F32)),
        scratch_shapes=[pltpu.VMEM((2, tk, rows), F32)],
        compiler_params=_params(2, V7X_VMEM_LIMIT),
    )(qs, k, vt)


def _attn_bwd(qs, do, lse, delta, kt, k, vt, tq, tk):
    s_len = qs.shape[0]
    nk = s_len // tk
    assert nk % 2 == 0
    nq = s_len // tq
    gw = GROUP_A * HEAD_DIM
    rows = GROUP_A * tq

    def body(q_ref, do_ref, lse_ref, delta_ref, kt_ref, k_ref, vt_ref, dq_ref, dk_hbm, dv_hbm,
             dk_acc, dv_acc, s_buf, dp_buf):
        kv = pl.program_id(0)
        i = pl.program_id(1)

        @pl.when(i == 0)
        def _():
            dk_acc[...] = jnp.zeros_like(dk_acc)
            dv_acc[...] = jnp.zeros_like(dv_acc)

        q = _stack_heads(q_ref, tq)
        dout = _stack_heads(do_ref, tq)
        qt = q.T
        doutt = dout.T
        row_lse = _stack_cols(lse_ref, tq)
        row_delta = _stack_cols(delta_ref, tq)

        def scores(j, slot):
            s_buf[slot] = _dot(q, kt_ref[j])
            dp_buf[slot] = _dot(dout, vt_ref[j])

        def consume(j, slot, dq):
            p = jnp.exp(s_buf[slot] - row_lse)
            ds = (p * (dp_buf[slot] - row_delta)).astype(BF16)
            dv_acc[j] += _dot(doutt, p.astype(BF16))
            dk_acc[j] += _dot(qt, ds)
            return dq + _dot(ds, k_ref[pl.ds(pl.multiple_of(j * tk, tk), tk), :])

        scores(0, 0)

        def pair(j, dq, more):
            scores(j + 1, 1)
            dq = consume(j, 0, dq)
            if more:
                scores(j + 2, 0)
            return consume(j + 1, 1, dq)

        dq = lax.fori_loop(0, nk // 2 - 1, lambda jj, c: pair(2 * jj, c, True), jnp.zeros((rows, HEAD_DIM), F32))
        dq = pair(nk - 2, dq, False) * SCORE_SCALE
        for g in range(GROUP_A):
            dq_ref[:, HEAD_DIM * g:HEAD_DIM * (g + 1)] = dq[g * tq:(g + 1) * tq]

        @pl.when(i == nq - 1)
        def _():
            pltpu.sync_copy(dk_acc, dk_hbm.at[kv])
            pltpu.sync_copy(dv_acc, dv_hbm.at[kv])

    tile = pl.BlockSpec((tq, gw), lambda kv, i: (i, kv))
    chunks = pl.BlockSpec((None, nk, HEAD_DIM, tk), lambda kv, i: (kv, 0, 0, 0))
    grad_t = SDS((N_KV_A, nk, HEAD_DIM, tk), F32)
    return pl.pallas_call(
        body, name="attn_bwd", grid=(N_KV_A, nq),
        in_specs=[tile, tile, tile, tile, chunks,
                  pl.BlockSpec((None, s_len, HEAD_DIM), lambda kv, i: (kv, 0, 0)), chunks],
        out_specs=[tile, pl.BlockSpec(memory_space=pl.ANY), pl.BlockSpec(memory_space=pl.ANY)],
        out_shape=(SDS((s_len, D_A), F32), grad_t, grad_t),
        scratch_shapes=[pltpu.VMEM((nk, HEAD_DIM, tk), F32), pltpu.VMEM((nk, HEAD_DIM, tk), F32),
                        pltpu.VMEM((2, rows, tk), F32), pltpu.VMEM((2, rows, tk), F32)],
        compiler_params=_params(2, V7X_VMEM_LIMIT),
    )(qs, do, lse, delta, kt, k, vt)


STAT_W = 128


def _band_specs(length, t, width=D_B):
    hb = t // BAND_HALF
    last = length // BAND_HALF - 1
    main = pl.BlockSpec((t, width), lambda r, i: (i, r))
    prev = pl.BlockSpec((BAND_HALF, width), lambda r, i: (jnp.maximum(i * hb - 1, 0), r))
    nxt = pl.BlockSpec((BAND_HALF, width), lambda r, i: (jnp.minimum((i + 1) * hb, last), r))
    return main, [prev, main, nxt]


def _pack_heads(cols):
    lane = lax.broadcasted_iota(jnp.int32, (cols[0].shape[0], STAT_W), 1)
    out = jnp.zeros((cols[0].shape[0], STAT_W), F32)
    for h, c in enumerate(cols):
        out = jnp.where(lane == h, c, out)
    return out


def _spread_heads(stat):
    head = lax.broadcasted_iota(jnp.int32, (stat.shape[0], D_B), 1) >> 6
    out = jnp.zeros((stat.shape[0], D_B), F32)
    for h in range(N_HEADS_B):
        out = jnp.where(head == h, stat[:, h:h + 1], out)
    return out


def _window(refs):
    return jnp.concatenate([r[...] for r in refs], axis=0)


def _head(v, h):
    return v[:, HEAD_DIM * h:HEAD_DIM * (h + 1)]


def _call_with_rider(body, name, grid, in_specs, out_specs, out_shape, scratch_shapes, args, rider):
    n_in, n_out = len(in_specs), len(out_specs)
    if rider is None:
        res = pl.pallas_call(body, name=name, grid=grid, in_specs=in_specs, out_specs=out_specs, out_shape=out_shape,
                             scratch_shapes=scratch_shapes, compiler_params=_params(len(grid), V7X_VMEM_LIMIT))(*args)
        return res, []
    kind, arrays = rider
    n = len(arrays)

    def with_rider(*refs):
        ins, c_ins = refs[:n_in], refs[n_in:n_in + n]
        outs, c_outs = refs[n_in + n:n_in + n + n_out], refs[n_in + n + n_out:n_in + 2 * n + n_out]
        rest = refs[n_in + 2 * n + n_out:]
        scratch, sems = rest[:-3], rest[-3:]
        ids = [pl.program_id(a) for a in range(len(grid))]
        first = functools.reduce(jnp.logical_and, [i == 0 for i in ids])
        last = functools.reduce(jnp.logical_and, [i == g - 1 for i, g in zip(ids, grid)])

        @pl.when(first)
        def _():
            _comm_start(kind, c_ins, c_outs, sems)

        body(*ins, *outs, *scratch)

        @pl.when(last)
        def _():
            _comm_wait(kind, c_ins, c_outs, sems)

    res = pl.pallas_call(
        with_rider, name=name, grid=grid, in_specs=list(in_specs) + [ANY_SPEC] * n,
        out_specs=list(out_specs) + [ANY_SPEC] * n, out_shape=list(out_shape) + _comm_out_shapes(kind, arrays),
        scratch_shapes=list(scratch_shapes) + _comm_sems(n), compiler_params=_params(len(grid), V7X_VMEM_LIMIT),
    )(*args, *arrays)
    return res[:n_out], res[n_out:]


def _band_fwd(q, k, v, bias, dil, t, rider=None):
    s_len = q.shape[0]
    length = s_len // dil
    w = t + 2 * BAND_HALF
    view = lambda a: a.reshape(length, dil * D_B)
    main, win = _band_specs(length, t)
    stat, _ = _band_specs(length, t, STAT_W)

    def body(q_ref, k0, k1, k2, v0, v1, v2, bias_ref, o_ref, lse_ref, s_buf, kt_buf):
        lses = []
        i = pl.program_id(1)
        kt_buf[...] = _window((k0, k1, k2)).T
        vw = _window((v0, v1, v2))
        pos = i * t - BAND_HALF + lax.broadcasted_iota(jnp.int32, (1, w), 1)
        valid = (pos >= 0) & (pos < length)
        for h in range(N_HEADS_B):
            s_buf[h] = _dot(q_ref[:, HEAD_DIM * h:HEAD_DIM * (h + 1)], kt_buf[HEAD_DIM * h:HEAD_DIM * (h + 1), :])
        for h in range(N_HEADS_B):
            s = jnp.where(valid, s_buf[h] + bias_ref[h], NEG_BIG)
            m = jnp.max(s, axis=-1, keepdims=True)
            e = jnp.exp(s - m)
            den = jnp.sum(e, axis=-1, keepdims=True)
            o_ref[:, HEAD_DIM * h:HEAD_DIM * (h + 1)] = _dot(e.astype(BF16), _head(vw, h)) / den
            lses.append(m + jnp.log(den))
        lse_ref[...] = _pack_heads(lses)

    (o, lse), rode = _call_with_rider(
        body, f"band_fwd_d{dil}", (dil, length // t),
        [main] + win + win + [_full_spec((N_HEADS_B, t, w))], [main, stat],
        [SDS((length, dil * D_B), F32), SDS((length, dil * STAT_W), F32)],
        [pltpu.VMEM((N_HEADS_B, t, w), F32), pltpu.VMEM((D_B, w), BF16)],
        (view(q), view(k), view(k), view(k), view(v), view(v), view(v), bias), rider)
    return o.reshape(s_len, D_B), lse.reshape(s_len, STAT_W), rode


def _band_dq(q, do, lse, delta, k, v, bias, dq_in, dil, t, rider=None):
    s_len = q.shape[0]
    length = s_len // dil
    w = t + 2 * BAND_HALF
    view = lambda a: a.reshape(length, dil * D_B)
    main, win = _band_specs(length, t)

    def body(q_ref, do_ref, lse_ref, delta_ref, k0, k1, k2, v0, v1, v2, bias_ref, acc_ref, dq_ref, dbias_ref,
             s_buf, dp_buf, kt_buf, vt_buf):
        i = pl.program_id(1)
        @pl.when((pl.program_id(0) == 0) & (i == 0))
        def _():
            dbias_ref[...] = jnp.zeros_like(dbias_ref)

        kw = _window((k0, k1, k2))
        kt_buf[...] = kw.T
        vt_buf[...] = _window((v0, v1, v2)).T
        pos = i * t - BAND_HALF + lax.broadcasted_iota(jnp.int32, (1, w), 1)
        valid = (pos >= 0) & (pos < length)
        for h in range(N_HEADS_B):
            cols = slice(HEAD_DIM * h, HEAD_DIM * (h + 1))
            s_buf[h] = _dot(q_ref[:, cols], kt_buf[cols, :])
            dp_buf[h] = _dot(do_ref[:, cols], vt_buf[cols, :])
        for h in range(N_HEADS_B):
            cols = slice(HEAD_DIM * h, HEAD_DIM * (h + 1))
            s = jnp.where(valid, s_buf[h] + bias_ref[h], NEG_BIG)
            p = jnp.exp(s - lse_ref[:, h:h + 1])
            ds = p * (dp_buf[h] - delta_ref[:, h:h + 1])
            dq_ref[:, cols] = acc_ref[:, cols] + _dot(ds.astype(BF16), _head(kw, h)) * SCORE_SCALE
            dbias_ref[h] += ds

    stat, _ = _band_specs(length, t, STAT_W)
    sview = lambda a: a.reshape(length, dil * STAT_W)
    (dq, dbias), rode = _call_with_rider(
        body, f"band_dq_d{dil}", (dil, length // t),
        [main, main, stat, stat] + win + win + [_full_spec((N_HEADS_B, t, w)), main],
        [main, _full_spec((N_HEADS_B, t, w))],
        [SDS((length, dil * D_B), F32), SDS((N_HEADS_B, t, w), F32)],
        [pltpu.VMEM((N_HEADS_B, t, w), F32), pltpu.VMEM((N_HEADS_B, t, w), F32),
         pltpu.VMEM((D_B, w), BF16), pltpu.VMEM((D_B, w), BF16)],
        (view(q), view(do), sview(lse), sview(delta), view(k), view(k), view(k), view(v), view(v), view(v), bias,
         view(dq_in)), rider)
    return dq.reshape(s_len, D_B), dbias, rode


def _band_dkv(k, v, q, do, lse, delta, bias_t, dk_in, dv_in, dil, t):
    s_len = q.shape[0]
    length = s_len // dil
    w = t + 2 * BAND_HALF
    view = lambda a: a.reshape(length, dil * D_B)
    main, win = _band_specs(length, t)

    def body(k_ref, v_ref, q0, q1, q2, d0, d1, d2, l0, l1, l2, e0, e1, e2, bias_ref, dk_acc, dv_acc, dk_ref, dv_ref,
             s_buf, dp_buf, kt_buf, vt_buf):
        i = pl.program_id(1)
        qw = _window((q0, q1, q2))
        dow = _window((d0, d1, d2))
        lsew = _window((l0, l1, l2))
        deltaw = _window((e0, e1, e2))
        pos = i * t - BAND_HALF + lax.broadcasted_iota(jnp.int32, (w, 1), 0)
        valid = (pos >= 0) & (pos < length)
        kt_buf[...] = k_ref[...].T
        vt_buf[...] = v_ref[...].T
        for h in range(N_HEADS_B):
            cols = slice(HEAD_DIM * h, HEAD_DIM * (h + 1))
            s_buf[h] = _dot(_head(qw, h), kt_buf[cols, :])
            dp_buf[h] = _dot(_head(dow, h), vt_buf[cols, :])
        qwt = qw.T
        dowt = dow.T
        dkt, dvt = [], []
        for h in range(N_HEADS_B):
            rows = slice(HEAD_DIM * h, HEAD_DIM * (h + 1))
            s = jnp.where(valid, s_buf[h] + bias_ref[h], NEG_BIG)
            p = jnp.exp(s - lsew[:, h:h + 1])
            ds = p * (dp_buf[h] - deltaw[:, h:h + 1])
            dvt.append(_dot(dowt[rows, :], p.astype(BF16)))
            dkt.append(_dot(qwt[rows, :], ds.astype(BF16)))
        dv_ref[...] = dv_acc[...] + jnp.concatenate(dvt, axis=0).T
        dk_ref[...] = dk_acc[...] + jnp.concatenate(dkt, axis=0).T

    _, swin = _band_specs(length, t, STAT_W)
    sview = lambda a: a.reshape(length, dil * STAT_W)
    dk, dv = pl.pallas_call(
        body, name=f"band_dkv_d{dil}", grid=(dil, length // t),
        in_specs=[main, main] + win + win + swin + swin + [_full_spec((N_HEADS_B, w, t)), main, main],
        out_specs=[main, main],
        out_shape=(SDS((length, dil * D_B), F32), SDS((length, dil * D_B), F32)),
        scratch_shapes=[pltpu.VMEM((N_HEADS_B, w, t), F32), pltpu.VMEM((N_HEADS_B, w, t), F32),
                        pltpu.VMEM((D_B, t), BF16), pltpu.VMEM((D_B, t), BF16)],
        compiler_params=_params(2, V7X_VMEM_LIMIT),
    )(view(k), view(v), view(q), view(q), view(q), view(do), view(do), view(do), sview(lse), sview(lse), sview(lse),
      sview(delta), sview(delta), sview(delta), bias_t, view(dk_in), view(dv_in))
    return dk.reshape(s_len, D_B), dv.reshape(s_len, D_B)


def _t5_bucket_np(rel):
    nb = N_BUCKETS // 2
    max_exact = nb // 2
    side = np.where(rel > 0, nb, 0)
    n = np.abs(rel)
    ratio = np.maximum(n, max_exact).astype(np.float32) / np.float32(max_exact)
    large = max_exact + (np.log(ratio) / np.float32(math.log(MAX_DISTANCE / max_exact))
                         * np.float32(nb - max_exact)).astype(np.int32)
    large = np.minimum(large, nb - 1)
    return (side + np.where(n < max_exact, n, large)).astype(np.int32)


def _band_buckets(dil, t):
    rel = np.arange(t + 2 * BAND_HALF)[None, :] - BAND_HALF - np.arange(t)[:, None]
    bucket = _t5_bucket_np(np.clip(rel, -BAND_HALF, BAND_HALF) * dil)
    return np.where(np.abs(rel) <= BAND_HALF, bucket, -1).astype(np.int32)


def _toeplitz(vals, rows, cols):
    heads = vals.shape[0]
    period = rows + cols
    vec = jnp.concatenate([vals[:, rows - 1:], jnp.zeros((heads, 1), vals.dtype), vals[:, :rows - 1]], axis=1)
    flat = jnp.broadcast_to(vec[:, None, :], (heads, rows, period)).reshape(heads, rows * period)
    return flat[:, :rows * (period - 1)].reshape(heads, rows, period - 1)[:, :, :cols]


def _bias_tiles(rel_bias, dil, t):
    w = t + 2 * BAND_HALF
    rel = np.arange(-BAND_HALF, BAND_HALF + 1)
    bucket = _t5_bucket_np(rel * dil)
    runs, start = [], 0
    for i in range(1, len(bucket) + 1):
        if i == len(bucket) or bucket[i] != bucket[start]:
            b = int(bucket[start])
            runs.append(jnp.broadcast_to(rel_bias[b:b + 1], (i - start, N_HEADS_B)))
            start = i
    per_rel = jnp.concatenate(runs, axis=0).T

    def diagonals(lo, hi):
        left = jnp.full((N_HEADS_B, max(0, -BAND_HALF - lo)), NEG_BIG, F32)
        right = jnp.full((N_HEADS_B, max(0, hi - BAND_HALF)), NEG_BIG, F32)
        return jnp.concatenate([left, per_rel, right], axis=1)

    tile = _toeplitz(diagonals(-(t - 1) - BAND_HALF, w - 1 - BAND_HALF), t, w)
    twin = _toeplitz(diagonals(-(w - 1) + BAND_HALF, t - 1 + BAND_HALF), w, t)
    return tile, twin


def _dbias_reduce(dbias, buckets):
    n = len(dbias)

    def body(*refs):
        db_refs, bk_refs, o_ref = refs[:n], refs[n:2 * n], refs[2 * n]
        row = lax.broadcasted_iota(jnp.int32, (N_BUCKETS, 128), 0)
        lane = lax.broadcasted_iota(jnp.int32, (N_BUCKETS, 128), 1)

        def per_bucket(b, out):
            for pat in range(n):
                msk = bk_refs[pat][...] == b
                for h in range(N_HEADS_B):
                    tot = jnp.sum(jnp.where(msk, db_refs[pat][h], 0.0), axis=-1, keepdims=True)
                    tot = jnp.sum(tot, axis=0, keepdims=True)
                    out = out + jnp.where((row == b) & (lane == h), tot, 0.0)
            return out

        o_ref[...] = lax.fori_loop(0, N_BUCKETS, per_bucket, jnp.zeros((N_BUCKETS, 128), F32))

    return pl.pallas_call(
        body, name="dbias_reduce", out_shape=SDS((N_BUCKETS, 128), F32),
        compiler_params=pltpu.CompilerParams(vmem_limit_bytes=V7X_VMEM_LIMIT),
    )(*dbias, *buckets)


def _attn_out_fwd(ya, ob, lb, x, g_a, g_b, w_out, g_post, tm):
    s_len = ya.shape[0]

    def body(ya_ref, o0, o1, o2, l0, l1, l2, x_ref, ga_ref, gb_ref, w_ref, gp_ref,
             h1_ref, yo_ref, yb_ref, lse_ref, ycat_ref):
        m = jnp.maximum(jnp.maximum(l0[...], l1[...]), l2[...])
        w0, w1, w2 = jnp.exp(l0[...] - m), jnp.exp(l1[...] - m), jnp.exp(l2[...] - m)
        wsum = w0 + w1 + w2
        yb = (_spread_heads(w0 / wsum) * o0[...] + _spread_heads(w1 / wsum) * o1[...]
              + _spread_heads(w2 / wsum) * o2[...])
        yb_ref[...] = yb
        lse_ref[...] = m + jnp.log(wsum)
        yan, _ = _rms_fwd(ya_ref[...], ga_ref[...])
        ybn, _ = _rms_fwd(yb, gb_ref[...])
        yan, ybn = yan.astype(BF16), ybn.astype(BF16)
        ycat_ref[:, :D_A] = yan
        ycat_ref[:, D_A:] = ybn
        yo = _dot(yan, w_ref[:D_A, :]) + _dot(ybn, w_ref[D_A:, :])
        yo_ref[...] = yo
        post, _ = _rms_fwd(yo, gp_ref[...])
        h1_ref[...] = x_ref[...] + post

    half, full, stat = _row_spec(tm, D_A), _row_spec(tm, D_MODEL), _row_spec(tm, STAT_W)
    return pl.pallas_call(
        body, name="attn_out_fwd", grid=(s_len // tm,),
        in_specs=[half] * 4 + [stat] * 3 + [full, _full_spec((1, D_A)), _full_spec((1, D_B)),
                                            _full_spec((D_MODEL, D_MODEL)), _full_spec((1, D_MODEL))],
        out_specs=[full, full, half, stat, full],
        out_shape=(SDS((s_len, D_MODEL), F32), SDS((s_len, D_MODEL), F32), SDS((s_len, D_B), F32),
                   SDS((s_len, STAT_W), F32), SDS((s_len, D_MODEL), BF16)),
        compiler_params=_params(1, V7X_VMEM_LIMIT),
    )(ya, ob[0], ob[1], ob[2], lb[0], lb[1], lb[2], x, g_a, g_b, w_out, g_post)


def _attn_out_bwd(dh1, yo, ya, yb, g_a, g_b, w_out, g_post, tm):
    s_len = ya.shape[0]

    def body(dh1_ref, yo_ref, ya_ref, yb_ref, ga_ref, gb_ref, w_ref, gp_ref,
             dyo_ref, dya_ref, dyb_ref, dela_ref, delb_ref, dgp_ref, dga_ref, dgb_ref):
        _zero_at_first(pl.program_id(0) == 0, dgp_ref, dga_ref, dgb_ref)
        dyo, dgp = _rms_bwd(yo_ref[...], gp_ref[...], dh1_ref[...])
        dyo = dyo.astype(BF16)
        dyo_ref[...] = dyo
        dgp_ref[...] += dgp
        dya_n = _dot_nt(dyo, w_ref[:D_A, :])
        dyb_n = _dot_nt(dyo, w_ref[D_A:, :])
        ya, yb = ya_ref[...], yb_ref[...]
        dya, dga = _rms_bwd(ya, ga_ref[...], dya_n)
        dyb, dgb = _rms_bwd(yb, gb_ref[...], dyb_n)
        dga_ref[...] += dga
        dgb_ref[...] += dgb
        dya_ref[...] = dya.astype(BF16)
        dyb_ref[...] = dyb.astype(BF16)
        dela_ref[...] = _head_sum(dya * ya)
        prod = dyb * yb
        head = lax.broadcasted_iota(jnp.int32, prod.shape, 1) >> 6
        delb_ref[...] = _pack_heads([jnp.sum(jnp.where(head == h, prod, 0.0), axis=-1, keepdims=True)
                                     for h in range(N_HEADS_B)])

    half, full = _row_spec(tm, D_A), _row_spec(tm, D_MODEL)
    return pl.pallas_call(
        body, name="attn_out_bwd", grid=(s_len // tm,),
        in_specs=[full, full, half, half, _full_spec((1, D_A)), _full_spec((1, D_B)),
                  _full_spec((D_MODEL, D_MODEL)), _full_spec((1, D_MODEL))],
        out_specs=[full, half, half, half, _row_spec(tm, STAT_W), _full_spec((1, D_MODEL)), _full_spec((1, D_A)),
                   _full_spec((1, D_B))],
        out_shape=(SDS((s_len, D_MODEL), BF16), SDS((s_len, D_A), BF16), SDS((s_len, D_B), BF16),
                   SDS((s_len, D_A), F32), SDS((s_len, STAT_W), F32), SDS((1, D_MODEL), F32), SDS((1, D_A), F32),
                   SDS((1, D_B), F32)),
        compiler_params=_params(1, V7X_VMEM_LIMIT),
    )(dh1, yo, ya, yb, g_a, g_b, w_out, g_post)


FF_CHUNK = 1024


def _mlp_fwd(h1, g_pre, w1, w2, g_post, tm):
    s_len = h1.shape[0]

    def body(h1_ref, gpre_ref, w1_ref, w2_ref, gpost_ref, h2_ref, fo_ref, xn_ref):
        h1v = h1_ref[...]
        xn, _ = _rms_fwd(h1v, gpre_ref[...])
        xn = xn.astype(BF16)
        xn_ref[...] = xn
        fo = jnp.zeros((tm, D_MODEL), F32)
        for c in range(D_FF // FF_CHUNK):
            cols = slice(c * FF_CHUNK, (c + 1) * FF_CHUNK)
            u = jnp.maximum(_dot(xn, w1_ref[:, cols]), 0.0)
            fo = fo + _dot((u * u).astype(BF16), w2_ref[cols, :])
        fo_ref[...] = fo
        post, _ = _rms_fwd(fo, gpost_ref[...])
        h2_ref[...] = h1v + post

    full = _row_spec(tm, D_MODEL)
    return pl.pallas_call(
        body, name="mlp_fwd", grid=(s_len // tm,),
        in_specs=[full, _full_spec((1, D_MODEL)), _const_spec((D_MODEL, D_FF)), _const_spec((D_FF, D_MODEL)),
                  _full_spec((1, D_MODEL))],
        out_specs=[full, full, full],
        out_shape=(SDS((s_len, D_MODEL), F32), SDS((s_len, D_MODEL), F32), SDS((s_len, D_MODEL), BF16)),
        compiler_params=_params(1, V7X_VMEM_LIMIT),
    )(h1, g_pre, w1, w2, g_post)


def _mlp_bwd(h1, dh2, fo, xn, g_pre, w1, w2, g_post, tm):
    s_len = h1.shape[0]

    def body(h1_ref, dh2_ref, fo_ref, xn_ref, gpre_ref, w1_ref, w2_ref, gpost_ref,
             dh1_ref, dfo_ref, du_ref, f_ref, dgpost_ref, dgpre_ref):
        _zero_at_first(pl.program_id(0) == 0, dgpost_ref, dgpre_ref)
        dh2 = dh2_ref[...]
        dfo, dgpost = _rms_bwd(fo_ref[...], gpost_ref[...], dh2)
        dfo = dfo.astype(BF16)
        dfo_ref[...] = dfo
        dgpost_ref[...] += dgpost
        xn = xn_ref[...]
        dxn = jnp.zeros((tm, D_MODEL), F32)
        for c in range(D_FF // FF_CHUNK):
            cols = slice(c * FF_CHUNK, (c + 1) * FF_CHUNK)
            u = jnp.maximum(_dot(xn, w1_ref[:, cols]), 0.0)
            f_ref[:, cols] = (u * u).astype(BF16)
            du = (_dot_nt(dfo, w2_ref[cols, :]) * (2.0 * u)).astype(BF16)
            du_ref[:, cols] = du
            dxn = dxn + _dot_nt(du, w1_ref[:, cols])
        dx, dgpre = _rms_bwd(h1_ref[...], gpre_ref[...], dxn)
        dgpre_ref[...] += dgpre
        dh1_ref[...] = dh2 + dx

    full, wide = _row_spec(tm, D_MODEL), _row_spec(tm, D_FF)
    return pl.pallas_call(
        body, name="mlp_bwd", grid=(s_len // tm,),
        in_specs=[full, full, full, full, _full_spec((1, D_MODEL)), _const_spec((D_MODEL, D_FF)),
                  _const_spec((D_FF, D_MODEL)), _full_spec((1, D_MODEL))],
        out_specs=[full, full, wide, wide, _full_spec((1, D_MODEL)), _full_spec((1, D_MODEL))],
        out_shape=(SDS((s_len, D_MODEL), F32), SDS((s_len, D_MODEL), BF16), SDS((s_len, D_FF), BF16),
                   SDS((s_len, D_FF), BF16), SDS((1, D_MODEL), F32), SDS((1, D_MODEL), F32)),
        compiler_params=_params(1, V7X_VMEM_LIMIT),
    )(h1, dh2, fo, xn, g_pre, w1, w2, g_post)


def _ple_fwd_bwd(h2, p, target, g_ple, w_gate, w_proj, tm):
    s_len = h2.shape[0]

    def body(h2_ref, p_ref, t_ref, g_ref, wg_ref, wp_ref, dh2_ref, loss_ref, dg_ref, xn_ref, dgl_ref, dpp_ref, pb_ref):
        _zero_at_first(pl.program_id(0) == 0, loss_ref, dg_ref)
        h2 = h2_ref[...]
        g = g_ref[...]
        xn, _ = _rms_fwd(h2, g)
        xn = xn.astype(BF16)
        xn_ref[...] = xn
        gate = 1.0 / (1.0 + jnp.exp(-_dot(xn, wg_ref[...])))
        pb = p_ref[...].astype(BF16)
        pb_ref[...] = pb
        pp = _dot(pb, wp_ref[...])
        diff = h2 + gate * pp - t_ref[...]
        loss_ref[...] += jnp.full((8, 128), jnp.sum(diff * diff), F32)
        dh3 = diff * (1.0 / D_MODEL)
        dpp_ref[...] = (dh3 * gate).astype(BF16)
        dgl = (dh3 * pp * gate * (1.0 - gate)).astype(BF16)
        dgl_ref[...] = dgl
        dx, dg = _rms_bwd(h2, g, _dot_nt(dgl, wg_ref[...]))
        dg_ref[...] += dg
        dh2_ref[...] = dh3 + dx

    full = _row_spec(tm, D_MODEL)
    return pl.pallas_call(
        body, name="ple_fwd_bwd", grid=(s_len // tm,),
        in_specs=[full, _row_spec(tm, D_PLE), full, _full_spec((1, D_MODEL)), _full_spec((D_MODEL, D_MODEL)),
                  _full_spec((D_PLE, D_MODEL))],
        out_specs=[full, _full_spec((8, 128)), _full_spec((1, D_MODEL)), full, full, full, _row_spec(tm, D_PLE)],
        out_shape=(SDS((s_len, D_MODEL), F32), SDS((8, 128), F32), SDS((1, D_MODEL), F32), SDS((s_len, D_MODEL), BF16),
                   SDS((s_len, D_MODEL), BF16), SDS((s_len, D_MODEL), BF16), SDS((s_len, D_PLE), BF16)),
        compiler_params=_params(1, V7X_VMEM_LIMIT),
    )(h2, p, target, g_ple, w_gate, w_proj)


def _inproj_bwd(dqs, dkn, dva, dqb, dkb, dvb, qa_raw, ka_raw, x, dh1, g_pre, w_in, cos, sin, gq, gk, tm):
    s_len = x.shape[0]

    def body(dqs_ref, dkn_ref, dva_ref, dqb_ref, dkb_ref, dvb_ref, qa_ref, ka_ref, x_ref, dh1_ref, g_ref, w_ref,
             cos_ref, sin_ref, gq_ref, gk_ref, dx_ref, dproj_ref, dg_ref, dgq_ref, dgk_ref):
        _zero_at_first(pl.program_id(0) == 0, dg_ref, dgq_ref, dgk_ref)
        cos, sin = cos_ref[...], sin_ref[...]
        dqa, dgq = _head_norm_rope_bwd(qa_ref[...], gq_ref[...], jnp.tile(cos, (1, D_A // D_KV_A)),
                                       jnp.tile(sin, (1, D_A // D_KV_A)), dqs_ref[...])
        dka, dgk = _head_norm_rope_bwd(ka_ref[...], gk_ref[...], cos, sin, dkn_ref[...])
        dgq_ref[...] += dgq
        dgk_ref[...] += dgk
        dproj_ref[:, OFF_QA:OFF_KA] = dqa.astype(BF16)
        dproj_ref[:, OFF_KA:OFF_VA] = dka.astype(BF16)
        dproj_ref[:, OFF_VA:OFF_QB] = dva_ref[...].astype(BF16)
        dproj_ref[:, OFF_QB:OFF_KB] = dqb_ref[...].astype(BF16)
        dproj_ref[:, OFF_KB:OFF_VB] = dkb_ref[...].astype(BF16)
        dproj_ref[:, OFF_VB:D_IN] = dvb_ref[...].astype(BF16)
        dxn = _dot_nt(dproj_ref[...], w_ref[...])
        dx, dg = _rms_bwd(x_ref[...], g_ref[...], dxn)
        dg_ref[...] += dg
        dx_ref[...] = dh1_ref[...] + dx

    half, kvw, full = _row_spec(tm, D_A), _row_spec(tm, D_KV_A), _row_spec(tm, D_MODEL)
    return pl.pallas_call(
        body, name="inproj_bwd", grid=(s_len // tm,),
        in_specs=[half, kvw, kvw, half, half, half, half, kvw, full, full, _full_spec((1, D_MODEL)),
                  _const_spec((D_MODEL, D_IN)), kvw, kvw, _full_spec((1, D_A)), _full_spec((1, D_KV_A))],
        out_specs=[full, _row_spec(tm, D_IN), _full_spec((1, D_MODEL)), _full_spec((1, D_A)), _full_spec((1, D_KV_A))],
        out_shape=(SDS((s_len, D_MODEL), F32), SDS((s_len, D_IN), BF16), SDS((1, D_MODEL), F32), SDS((1, D_A), F32),
                   SDS((1, D_KV_A), F32)),
        compiler_params=_params(1, V7X_VMEM_LIMIT),
    )(dqs, dkn, dva, dqb, dkb, dvb, qa_raw, ka_raw, x, dh1, g_pre, w_in, cos, sin, gq, gk)


def _weight_grad(a, b, name, tk1, tn, tm):
    s_len, k1 = a.shape
    n = b.shape[1]
    steps = s_len // tm

    def body(a_ref, b_ref, o_ref, acc):
        r = pl.program_id(2)
        _zero_at_first(r == 0, acc)
        acc[...] += _dot_tn(a_ref[...], b_ref[...])

        @pl.when(r == steps - 1)
        def _():
            o_ref[...] = acc[...].astype(BF16)

    return pl.pallas_call(
        body, name=name, grid=(k1 // tk1, n // tn, steps),
        in_specs=[pl.BlockSpec((tm, tk1), lambda i, j, r: (r, i)), pl.BlockSpec((tm, tn), lambda i, j, r: (r, j))],
        out_specs=pl.BlockSpec((tk1, tn), lambda i, j, r: (i, j)),
        out_shape=SDS((k1, n), BF16),
        scratch_shapes=[pltpu.VMEM((tk1, tn), F32)],
        compiler_params=_params(3, V7X_VMEM_LIMIT),
    )(a, b)


def _my_index():
    return 4 * lax.axis_index("x") + 2 * lax.axis_index("y") + lax.axis_index("c")


def _peer(k):
    f = k + 1
    x, y, c = lax.axis_index("x"), lax.axis_index("y"), lax.axis_index("c")
    return (x ^ ((f >> 2) & 1), y ^ ((f >> 1) & 1), c ^ (f & 1))


GATHER, EXCHANGE = "gather", "exchange"
ANY_SPEC = pl.BlockSpec(memory_space=pl.ANY)


def _comm_out_shapes(kind, arrays):
    return [SDS((N_DEV,) + a.shape if kind == GATHER else a.shape, a.dtype) for a in arrays]


def _comm_sems(n):
    return [pltpu.SemaphoreType.DMA((n, N_DEV - 1)), pltpu.SemaphoreType.DMA((n, N_DEV - 1)), pltpu.SemaphoreType.DMA((n,))]


def _comm_copies(kind, ins, outs, send_sems, recv_sems, local_sems):
    me = _my_index()
    local, remote = [], []
    for a, (src, dst) in enumerate(zip(ins, outs)):
        local.append(pltpu.make_async_copy(src if kind == GATHER else src.at[me], dst.at[me], local_sems.at[a]))
        for k in range(N_DEV - 1):
            px, py, pc = _peer(k)
            remote.append(pltpu.make_async_remote_copy(
                src_ref=src if kind == GATHER else src.at[4 * px + 2 * py + pc], dst_ref=dst.at[me],
                send_sem=send_sems.at[a, k], recv_sem=recv_sems.at[a, k],
                device_id=(px, py, pc), device_id_type=pl.DeviceIdType.MESH))
    return local, remote


def _comm_start(kind, ins, outs, sems):
    local, remote = _comm_copies(kind, ins, outs, *sems)
    for cp in local + remote:
        cp.start()


def _comm_wait(kind, ins, outs, sems):
    local, remote = _comm_copies(kind, ins, outs, *sems)
    for cp in local:
        cp.wait()
    for cp in remote:
        cp.wait_send()
    for cp in remote:
        cp.wait_recv()


def _collective(kind, arrays, name):
    n = len(arrays)

    def body(*refs):
        ins, outs, sems = refs[:n], refs[n:2 * n], refs[2 * n:]
        _comm_start(kind, ins, outs, sems)
        _comm_wait(kind, ins, outs, sems)

    return pl.pallas_call(
        body, name=name, in_specs=[ANY_SPEC] * n, out_specs=[ANY_SPEC] * n,
        out_shape=_comm_out_shapes(kind, arrays), scratch_shapes=_comm_sems(n),
    )(*arrays)


def _sum_adamw(parts, w, m, v, name, tr):
    rows, cols = w.shape
    c1 = 1.0 / (1.0 - ADAM_B1 ** ADAM_STEP)
    c2 = 1.0 / (1.0 - ADAM_B2 ** ADAM_STEP)

    def body(p_ref, w_ref, m_ref, v_ref, g_ref, d_ref, nm_ref, nv_ref):
        g = p_ref[0].astype(F32)
        for j in range(1, N_DEV):
            g = g + p_ref[j].astype(F32)
        g_ref[...] = g
        nm = ADAM_B1 * m_ref[...] + (1.0 - ADAM_B1) * g
        nv = ADAM_B2 * v_ref[...] + (1.0 - ADAM_B2) * (g * g)
        nm_ref[...] = nm
        nv_ref[...] = nv
        d_ref[...] = -ADAM_LR * ((nm * c1) / (jnp.sqrt(nv * c2) + ADAM_EPS) + ADAM_WD * w_ref[...])

    blk = pl.BlockSpec((tr, cols), lambda i: (i, 0))
    return pl.pallas_call(
        body, name=name, grid=(rows // tr,),
        in_specs=[pl.BlockSpec((N_DEV, tr, cols), lambda i: (0, i, 0)), blk, blk, blk],
        out_specs=[blk] * 4, out_shape=[SDS((rows, cols), F32)] * 4,
        compiler_params=_params(1, V7X_VMEM_LIMIT),
    )(parts, w, m, v)


_SMALL = (("g_attn_pre", 1024), ("g_q", 64), ("g_k", 64), ("g_out_a", 512), ("g_out_b", 512), ("g_attn_post", 1024),
          ("rel_bias", 256), ("g_mlp_pre", 1024), ("g_mlp_post", 1024), ("g_ple", 1024))
_SLAB_ROWS = 56


def _pack_small(vals):
    rows = []
    for (name, size) in _SMALL:
        flat = vals[name].reshape(-1).astype(F32)
        padded = -(-size // 128) * 128
        rows.append(jnp.pad(flat, (0, padded - size)).reshape(padded // 128, 128))
    slab = jnp.concatenate(rows, axis=0)
    return jnp.pad(slab, ((0, _SLAB_ROWS - slab.shape[0]), (0, 0)))


def _unpack_small(slab, shapes):
    out, row = {}, 0
    for (name, size) in _SMALL:
        nrow = -(-size // 128)
        out[name] = slab[row:row + nrow].reshape(-1)[:size].reshape(shapes[name])
        row += nrow
    return out


def _rope_tables(s_len):
    rows = s_len // GRID_W
    row = jnp.broadcast_to(jnp.arange(rows)[:, None], (rows, GRID_W)).reshape(-1).astype(F32)
    col = jnp.broadcast_to(jnp.arange(GRID_W)[None, :], (rows, GRID_W)).reshape(-1).astype(F32)
    n_axis = ROPE_HALF // 2
    inv_freq = ROPE_THETA ** (-jnp.arange(n_axis, dtype=F32) / n_axis)
    ang = jnp.concatenate([row[:, None] * inv_freq, col[:, None] * inv_freq], axis=-1)
    cos, sin = jnp.cos(ang), jnp.sin(ang)
    cos = jnp.tile(jnp.concatenate([cos, cos], axis=-1), (1, N_KV_A))
    sin = jnp.tile(jnp.concatenate([-sin, sin], axis=-1), (1, N_KV_A))
    return cos, sin


_RIDERS = {1: ("w_out", "w_ff1"), 4: ("w_ff2",), 16: ("w_ple_gate", "w_ple_proj")}
_GRAD_RIDERS = {1: ("w_ff1",), 4: ("w_ff2",), 16: ("w_ple_gate", "w_ple_proj", "w_out")}


def _local_step(x, p, target, w_in, shards, small):
    axis_of = dict(_BIG)
    s_len = x.shape[0]
    tm = min(256, s_len)
    tq = min(128, s_len)
    tk = min(2048, s_len // 2)
    cos, sin = _rope_tables(s_len)
    gq = jnp.tile(small["g_q"], (1, N_HEADS_A))
    gk = jnp.tile(small["g_k"], (1, N_KV_A))

    qa_raw, ka_raw, qs, kn, va, qb, kb, vb, xn1 = _inproj_fwd(x, small["g_attn_pre"], w_in, cos, sin, gq, gk, tm)

    def kv_major(a):
        return a.reshape(s_len, N_KV_A, HEAD_DIM).transpose(1, 0, 2)

    def kv_chunks_t(a):
        return a.reshape(s_len // tk, tk, N_KV_A, HEAD_DIM).transpose(2, 0, 3, 1)

    def kv_unchunk(a):
        return a.transpose(1, 3, 0, 2).reshape(s_len, D_KV_A)

    k_maj, kt, vt = kv_major(kn), kv_chunks_t(kn), kv_chunks_t(va)
    vt_ones = jnp.concatenate([vt, jnp.ones((N_KV_A, s_len // tk, 16, tk), BF16)], axis=2)
    ya, lse_a = _attn_fwd(qs, k_maj, vt_ones, tq, tk)

    ob, lb, tiles, full = [], [], [], {}
    for (_, dil) in DILATED_PATTERNS:
        t = min(256, s_len // dil)
        bias, bias_t = _bias_tiles(small["rel_bias"], dil, t)
        tiles.append((t, bias, bias_t, _band_buckets(dil, t)))
        o, l, gathered = _band_fwd(qb, kb, vb, bias, dil, t, (GATHER, [shards[n] for n in _RIDERS[dil]]))
        ob.append(o)
        lb.append(l)
        full.update({n: _assemble(g, axis_of[n]) for n, g in zip(_RIDERS[dil], gathered)})
    w_out, w_ff1, w_ff2, w_gate, w_proj = (full[n] for n in ("w_out", "w_ff1", "w_ff2", "w_ple_gate", "w_ple_proj"))

    h1, yo, yb, lse_b, ycat = _attn_out_fwd(ya, ob, lb, x, small["g_out_a"], small["g_out_b"], w_out,
                                            small["g_attn_post"], tm)
    h2, fo, xn2 = _mlp_fwd(h1, small["g_mlp_pre"], w_ff1, w_ff2, small["g_mlp_post"], tm)
    dh2, loss_part, dg_ple, xn3, dgl, dpp, pb = _ple_fwd_bwd(h2, p, target, small["g_ple"], w_gate, w_proj, tm)
    dh1, dfo, du, f, dg_mlp_post, dg_mlp_pre = _mlp_bwd(h1, dh2, fo, xn2, small["g_mlp_pre"], w_ff1, w_ff2,
                                                          small["g_mlp_post"], tm)
    dyo, dya, dyb, delta_a, delta_b, dg_attn_post, dg_out_a, dg_out_b = _attn_out_bwd(
        dh1, yo, ya, yb, small["g_out_a"], small["g_out_b"], w_out, small["g_attn_post"], tm)

    dqs, dk_t, dv_t = _attn_bwd(qs, dya, lse_a, delta_a, kt, k_maj, vt, tq, tk)
    dkn, dva = kv_unchunk(dk_t), kv_unchunk(dv_t)

    tg = min(512, s_len)
    grads = {
        "w_out": _weight_grad(ycat, dyo, "grad_w_out", D_MODEL, D_MODEL, tg),
        "w_ff1": _weight_grad(xn2, du, "grad_w_ff1", D_MODEL, 1024, tg),
        "w_ff2": _weight_grad(f, dfo, "grad_w_ff2", 1024, D_MODEL, tg),
        "w_ple_gate": _weight_grad(xn3, dgl, "grad_w_ple_gate", D_MODEL, D_MODEL, tg),
        "w_ple_proj": _weight_grad(pb, dpp, "grad_w_ple_proj", D_PLE, D_MODEL, tg),
    }

    dqb = dkb = dvb = jnp.zeros((s_len, D_B), F32)
    dbias, received = [], {}
    for (_, dil), (t, bias, bias_t, _) in zip(DILATED_PATTERNS, tiles):
        rider = (EXCHANGE, [_cut(grads[n], axis_of[n]) for n in _GRAD_RIDERS[dil]])
        dqb, db, got = _band_dq(qb, dyb, lse_b, delta_b, kb, vb, bias, dqb, dil, t, rider)
        dkb, dvb = _band_dkv(kb, vb, qb, dyb, lse_b, delta_b, bias_t, dkb, dvb, dil, t)
        dbias.append(db)
        received.update(zip(_GRAD_RIDERS[dil], got))
    d_rel = _dbias_reduce(dbias, [jnp.asarray(tl[3]) for tl in tiles])[:, :N_HEADS_B]

    dx, dproj, dg_attn_pre, dgq_lanes, dgk_lanes = _inproj_bwd(
        dqs, dkn, dva, dqb, dkb, dvb, qa_raw, ka_raw, x, dh1, small["g_attn_pre"], w_in, cos, sin, gq, gk, tm)
    grad_w_in = _weight_grad(xn1, dproj, "grad_w_in", D_MODEL, 768, tg)
    small_grads = {
        "g_attn_pre": dg_attn_pre, "g_q": dgq_lanes.reshape(N_HEADS_A, HEAD_DIM).sum(0, keepdims=True),
        "g_k": dgk_lanes.reshape(N_KV_A, HEAD_DIM).sum(0, keepdims=True), "g_out_a": dg_out_a, "g_out_b": dg_out_b,
        "g_attn_post": dg_attn_post, "rel_bias": d_rel, "g_mlp_pre": dg_mlp_pre, "g_mlp_post": dg_mlp_post,
        "g_ple": dg_ple,
    }
    return loss_part[0, 0], dx, grad_w_in, received, small_grads


_BIG = (("w_in", 1), ("w_out", 0), ("w_ff1", 1), ("w_ff2", 0), ("w_ple_gate", 0), ("w_ple_proj", 1))


def _assemble(gathered, axis):
    if axis == 0:
        return gathered.reshape(-1, gathered.shape[2])
    return gathered.transpose(1, 0, 2).reshape(gathered.shape[1], -1)


def _cut(full, axis):
    if axis == 0:
        return full.reshape(N_DEV, full.shape[0] // N_DEV, full.shape[1])
    return full.reshape(full.shape[0], N_DEV, full.shape[1] // N_DEV).transpose(1, 0, 2)


def kernel(x, p, w_in, g_attn_pre, g_q, g_k, g_out_a, g_out_b, w_out, g_attn_post, rel_bias, g_mlp_pre, w_ff1, w_ff2, g_mlp_post, g_ple, w_ple_gate, w_ple_proj, loss_target, m_w_in, m_g_attn_pre, m_g_q, m_g_k, m_g_out_a, m_g_out_b, m_w_out, m_g_attn_post, m_rel_bias, m_g_mlp_pre, m_w_ff1, m_w_ff2, m_g_mlp_post, m_g_ple, m_w_ple_gate, m_w_ple_proj, v_w_in, v_g_attn_pre, v_g_q, v_g_k, v_g_out_a, v_g_out_b, v_w_out, v_g_attn_post, v_rel_bias, v_g_mlp_pre, v_w_ff1, v_w_ff2, v_g_mlp_post, v_g_ple, v_w_ple_gate, v_w_ple_proj):
    given = dict(locals())
    small_names = [n for n, _ in _SMALL]
    small = {n: given[n] for n in small_names}
    shards = {n: given[n][0] for n, _ in _BIG}

    bf16_shards = {n: shards[n].astype(BF16) for n, _ in _BIG}
    (gathered_w_in,) = _collective(GATHER, [bf16_shards["w_in"]], "gather_w_in")

    loss_part, dx, grad_w_in, received, small_grads = _local_step(
        x[0], p[0, 0], loss_target[0], _assemble(gathered_w_in, 1), bf16_shards, small)

    slab = _pack_small(small_grads)
    slab_parts = jnp.broadcast_to(slab[None], (N_DEV,) + slab.shape)
    received["w_in"], slabs = _collective(EXCHANGE, [_cut(grad_w_in, 1), slab_parts], "exchange_w_in_grads")

    out_g, out_d, out_m, out_v = {}, {}, {}, {}
    for n, _ in _BIG:
        rows = shards[n].shape[0]
        g, d, nm, nv = _sum_adamw(received[n], shards[n], given["m_" + n][0], given["v_" + n][0], "adamw_" + n,
                                  min(rows, 128))
        out_g[n], out_d[n], out_m[n], out_v[n] = g[None], d[None], nm[None], nv[None]
    g, d, nm, nv = _sum_adamw(slabs, _pack_small(small), _pack_small({n: given["m_" + n] for n in small_names}),
                              _pack_small({n: given["v_" + n] for n in small_names}), "adamw_small", _SLAB_ROWS)
    shapes = {n: given[n].shape for n in small_names}
    for dst, slab_out in ((out_g, g), (out_d, d), (out_m, nm), (out_v, nv)):
        dst.update(_unpack_small(slab_out, shapes))

    loss = lax.psum(loss_part * (0.5 / D_MODEL), MESH_AXES)
    order = ["w_in", "g_attn_pre", "g_q", "g_k", "g_out_a", "g_out_b", "w_out", "g_attn_post", "rel_bias", "g_mlp_pre",
             "w_ff1", "w_ff2", "g_mlp_post", "g_ple", "w_ple_gate", "w_ple_proj"]
    return (loss, dx[None], *[out_g[n] for n in order], *[out_d[n] for n in order], *[out_m[n] for n in order],
            *[out_v[n] for n in order])
```

```python
import functools
import math

import jax
import jax.numpy as jnp
import numpy as np
from jax import lax
from jax.experimental import pallas as pl
from jax.experimental.pallas import tpu as pltpu

F32 = jnp.float32
BF16 = jnp.bfloat16
SDS = jax.ShapeDtypeStruct

D_MODEL = 1024
HEAD_DIM = 64
N_HEADS_A = 8
N_KV_A = 2
GROUP_A = N_HEADS_A // N_KV_A
N_HEADS_B = 8
D_A = N_HEADS_A * HEAD_DIM
D_KV_A = N_KV_A * HEAD_DIM
D_B = N_HEADS_B * HEAD_DIM
D_IN = D_A + 2 * D_KV_A + 3 * D_B
D_FF = 4 * D_MODEL
D_PLE = 256
GRID_W = 64
ROPE_THETA = 10000.0
ROPE_HALF = HEAD_DIM // 2
DILATED_PATTERNS = ((128, 1), (512, 4), (2048, 16))
BAND_HALF = 64
N_BUCKETS = 32
MAX_DISTANCE = 1024
EPS = 1e-6
NEG_BIG = -1e30
SCORE_SCALE = HEAD_DIM ** -0.5

ADAM_LR = 0.001
ADAM_B1 = 0.9
ADAM_B2 = 0.999
ADAM_EPS = 1e-08
ADAM_WD = 0.01
ADAM_STEP = 10

N_DEV = 8
V7X_VMEM_LIMIT = 56 * 1024 * 1024

OFF_QA, OFF_KA, OFF_VA, OFF_QB, OFF_KB, OFF_VB = 0, 512, 640, 768, 1280, 1792


def _params(n_axes, vmem=None):
    return pltpu.CompilerParams(dimension_semantics=("arbitrary",) * n_axes, vmem_limit_bytes=vmem)


def _dot(a, b):
    return jnp.dot(a, b, preferred_element_type=F32)


def _dot_nt(a, b):
    return lax.dot_general(a, b, (((1,), (1,)), ((), ())), preferred_element_type=F32)


def _dot_tn(a, b):
    return lax.dot_general(a, b, (((0,), (0,)), ((), ())), preferred_element_type=F32)


def _rms_fwd(x, g):
    r = lax.rsqrt(jnp.mean(x * x, axis=-1, keepdims=True) + EPS)
    return x * r * g, r


def _rms_bwd(x, g, dy):
    r = lax.rsqrt(jnp.mean(x * x, axis=-1, keepdims=True) + EPS)
    xh = x * r
    dxh = dy * g
    dx = r * (dxh - xh * jnp.mean(dxh * xh, axis=-1, keepdims=True))
    return dx, jnp.sum(dy * xh, axis=0, keepdims=True)


def _head_sum(v):
    head = lax.broadcasted_iota(jnp.int32, v.shape, 1) >> 6
    out = jnp.zeros_like(v)
    for h in range(v.shape[1] // HEAD_DIM):
        msk = head == h
        s = jnp.sum(jnp.where(msk, v, 0.0), axis=-1, keepdims=True)
        out = jnp.where(msk, s, out)
    return out


def _swap_halves(v):
    w = v.shape[1]
    lane = lax.broadcasted_iota(jnp.int32, v.shape, 1)
    first_half = (lane & (HEAD_DIM - 1)) < ROPE_HALF
    return jnp.where(first_half, pltpu.roll(v, w - ROPE_HALF, 1), pltpu.roll(v, ROPE_HALF, 1))


def _head_norm_rope(v, g, cos, sin_signed):
    r = lax.rsqrt(_head_sum(v * v) * (1.0 / HEAD_DIM) + EPS)
    y = v * r * g
    return y * cos + _swap_halves(y) * sin_signed


def _head_norm_rope_bwd(v, g, cos, sin_signed, dout):
    dy = dout * cos - _swap_halves(dout) * sin_signed
    r = lax.rsqrt(_head_sum(v * v) * (1.0 / HEAD_DIM) + EPS)
    xh = v * r
    dxh = dy * g
    dv = r * (dxh - xh * (_head_sum(dxh * xh) * (1.0 / HEAD_DIM)))
    return dv, jnp.sum(dy * xh, axis=0, keepdims=True)


def _row_spec(tm, n):
    return pl.BlockSpec((tm, n), lambda i: (i, 0))


def _full_spec(shape):
    nd = len(shape)
    return pl.BlockSpec(shape, lambda *_: (0,) * nd)


def _const_spec(shape):
    nd = len(shape)
    return pl.BlockSpec(shape, lambda *_: (0,) * nd, pipeline_mode=pl.Buffered(1))


def _zero_at_first(first, *refs):
    @pl.when(first)
    def _():
        for ref in refs:
            ref[...] = jnp.zeros_like(ref)


def _inproj_fwd(x, g_pre, w_in, cos, sin, gq, gk, tm):
    s_len = x.shape[0]

    def body(x_ref, g_ref, w_ref, cos_ref, sin_ref, gq_ref, gk_ref,
             qa_raw, ka_raw, qs, kn, va, qb, kb, vb, xn_out):
        xn, _ = _rms_fwd(x_ref[...], g_ref[...])
        xn = xn.astype(BF16)
        xn_out[...] = xn
        qa = _dot(xn, w_ref[:, OFF_QA:OFF_KA])
        qa_raw[...] = qa
        cos, sin = cos_ref[...], sin_ref[...]
        qs[...] = (_head_norm_rope(qa, gq_ref[...], jnp.tile(cos, (1, D_A // D_KV_A)), jnp.tile(sin, (1, D_A // D_KV_A)))
                   * SCORE_SCALE).astype(BF16)
        ka = _dot(xn, w_ref[:, OFF_KA:OFF_VA])
        ka_raw[...] = ka
        kn[...] = _head_norm_rope(ka, gk_ref[...], cos, sin).astype(BF16)
        va[...] = _dot(xn, w_ref[:, OFF_VA:OFF_QB]).astype(BF16)
        qb[...] = (_dot(xn, w_ref[:, OFF_QB:OFF_KB]) * SCORE_SCALE).astype(BF16)
        kb[...] = _dot(xn, w_ref[:, OFF_KB:OFF_VB]).astype(BF16)
        vb[...] = _dot(xn, w_ref[:, OFF_VB:D_IN]).astype(BF16)

    return pl.pallas_call(
        body, name="inproj_fwd", grid=(s_len // tm,),
        in_specs=[_row_spec(tm, D_MODEL), _full_spec((1, D_MODEL)), _const_spec((D_MODEL, D_IN)),
                  _row_spec(tm, D_KV_A), _row_spec(tm, D_KV_A), _full_spec((1, D_A)), _full_spec((1, D_KV_A))],
        out_specs=[_row_spec(tm, D_A), _row_spec(tm, D_KV_A), _row_spec(tm, D_A), _row_spec(tm, D_KV_A),
                   _row_spec(tm, D_KV_A), _row_spec(tm, D_B), _row_spec(tm, D_B), _row_spec(tm, D_B),
                   _row_spec(tm, D_MODEL)],
        out_shape=(SDS((s_len, D_A), F32), SDS((s_len, D_KV_A), F32), SDS((s_len, D_A), BF16),
                   SDS((s_len, D_KV_A), BF16), SDS((s_len, D_KV_A), BF16), SDS((s_len, D_B), BF16),
                   SDS((s_len, D_B), BF16), SDS((s_len, D_B), BF16), SDS((s_len, D_MODEL), BF16)),
        compiler_params=_params(1, V7X_VMEM_LIMIT),
    )(x, g_pre, w_in, cos, sin, gq, gk)


def _stack_heads(ref, tq):
    return jnp.concatenate([ref[:, HEAD_DIM * g:HEAD_DIM * (g + 1)] for g in range(GROUP_A)], axis=0)


def _stack_cols(ref, tq):
    return jnp.concatenate([ref[:, HEAD_DIM * g:HEAD_DIM * g + 1] for g in range(GROUP_A)], axis=0)


def _attn_fwd(qs, k, vt, tq, tk):
    s_len = qs.shape[0]
    nk = s_len // tk
    assert nk % 2 == 0
    gw = GROUP_A * HEAD_DIM
    rows = GROUP_A * tq
    vrows = vt.shape[2]

    def body(q_ref, k_ref, vt_ref, o_ref, lse_ref, s_buf):
        qt = _stack_heads(q_ref, tq).T

        def scores(j, slot):
            kj = k_ref[pl.ds(pl.multiple_of(j * tk, tk), tk), :]
            s_buf[slot] = _dot(kj, qt)

        def consume(j, slot, carry):
            m, acc = carry
            st = s_buf[slot]
            m_new = jnp.maximum(m, jnp.max(st, axis=0, keepdims=True))
            pt = jnp.exp(st - m_new)
            acc = jnp.exp(m - m_new) * acc + _dot(vt_ref[j], pt.astype(BF16))
            return m_new, acc

        scores(0, 0)

        def pair(j, carry, more):
            scores(j + 1, 1)
            carry = consume(j, 0, carry)
            if more:
                scores(j + 2, 0)
            return consume(j + 1, 1, carry)

        carry = (jnp.full((1, rows), NEG_BIG, F32), jnp.zeros((vrows, rows), F32))
        carry = lax.fori_loop(0, nk // 2 - 1, lambda jj, c: pair(2 * jj, c, True), carry)
        m, acc = pair(nk - 2, carry, False)
        l = acc[HEAD_DIM:HEAD_DIM + 1]
        o = (acc[:HEAD_DIM] / l).T
        lse = jnp.broadcast_to(m + jnp.log(l), (HEAD_DIM, rows)).T
        for g in range(GROUP_A):
            o_ref[:, HEAD_DIM * g:HEAD_DIM * (g + 1)] = o[g * tq:(g + 1) * tq]
            lse_ref[:, HEAD_DIM * g:HEAD_DIM * (g + 1)] = lse[g * tq:(g + 1) * tq]

    tile = pl.BlockSpec((tq, gw), lambda kv, i: (i, kv))
    return pl.pallas_call(
        body, name="attn_fwd", grid=(N_KV_A, s_len // tq),
        in_specs=[tile, pl.BlockSpec((None, s_len, HEAD_DIM), lambda kv, i: (kv, 0, 0)),
                  pl.BlockSpec((None, nk, vrows, tk), lambda kv, i: (kv, 0, 0, 0))],
        out_specs=[tile, tile],
        out_shape=(SDS((s_len, D_A), F32), SDS((s_len, D_A), F32)),
        scratch_shapes=[pltpu.VMEM((2, tk, rows), F32)],
        compiler_params=_params(2, V7X_VMEM_LIMIT),
    )(qs, k, vt)


def _attn_bwd(qs, do, lse, delta, kt, k, vt, tq, tk):
    s_len = qs.shape[0]
    nk = s_len // tk
    assert nk % 2 == 0
    nq = s_len // tq
    gw = GROUP_A * HEAD_DIM
    rows = GROUP_A * tq

    def body(q_ref, do_ref, lse_ref, delta_ref, kt_ref, k_ref, vt_ref, dq_ref, dk_hbm, dv_hbm,
             dk_acc, dv_acc, s_buf, dp_buf):
        kv = pl.program_id(0)
        i = pl.program_id(1)

        @pl.when(i == 0)
        def _():
            dk_acc[...] = jnp.zeros_like(dk_acc)
            dv_acc[...] = jnp.zeros_like(dv_acc)

        q = _stack_heads(q_ref, tq)
        dout = _stack_heads(do_ref, tq)
        qt = q.T
        doutt = dout.T
        row_lse = _stack_cols(lse_ref, tq)
        row_delta = _stack_cols(delta_ref, tq)

        def scores(j, slot):
            s_buf[slot] = _dot(q, kt_ref[j])
            dp_buf[slot] = _dot(dout, vt_ref[j])

        def consume(j, slot, dq):
            p = jnp.exp(s_buf[slot] - row_lse)
            ds = (p * (dp_buf[slot] - row_delta)).astype(BF16)
            dv_acc[j] += _dot(doutt, p.astype(BF16))
            dk_acc[j] += _dot(qt, ds)
            return dq + _dot(ds, k_ref[pl.ds(pl.multiple_of(j * tk, tk), tk), :])

        scores(0, 0)

        def pair(j, dq, more):
            scores(j + 1, 1)
            dq = consume(j, 0, dq)
            if more:
                scores(j + 2, 0)
            return consume(j + 1, 1, dq)

        dq = lax.fori_loop(0, nk // 2 - 1, lambda jj, c: pair(2 * jj, c, True), jnp.zeros((rows, HEAD_DIM), F32))
        dq = pair(nk - 2, dq, False) * SCORE_SCALE
        for g in range(GROUP_A):
            dq_ref[:, HEAD_DIM * g:HEAD_DIM * (g + 1)] = dq[g * tq:(g + 1) * tq]

        @pl.when(i == nq - 1)
        def _():
            pltpu.sync_copy(dk_acc, dk_hbm.at[kv])
            pltpu.sync_copy(dv_acc, dv_hbm.at[kv])

    tile = pl.BlockSpec((tq, gw), lambda kv, i: (i, kv))
    chunks = pl.BlockSpec((None, nk, HEAD_DIM, tk), lambda kv, i: (kv, 0, 0, 0))
    grad_t = SDS((N_KV_A, nk, HEAD_DIM, tk), F32)
    return pl.pallas_call(
        body, name="attn_bwd", grid=(N_KV_A, nq),
        in_specs=[tile, tile, tile, tile, chunks,
                  pl.BlockSpec((None, s_len, HEAD_DIM), lambda kv, i: (kv, 0, 0)), chunks],
        out_specs=[tile, pl.BlockSpec(memory_space=pl.ANY), pl.BlockSpec(memory_space=pl.ANY)],
        out_shape=(SDS((s_len, D_A), F32), grad_t, grad_t),
        scratch_shapes=[pltpu.VMEM((nk, HEAD_DIM, tk), F32), pltpu.VMEM((nk, HEAD_DIM, tk), F32),
                        pltpu.VMEM((2, rows, tk), F32), pltpu.VMEM((2, rows, tk), F32)],
        compiler_params=_params(2, V7X_VMEM_LIMIT),
    )(qs, do, lse, delta, kt, k, vt)


STAT_W = 128


def _band_specs(length, t, width=D_B):
    hb = t // BAND_HALF
    last = length // BAND_HALF - 1
    main = pl.BlockSpec((t, width), lambda r, i: (i, r))
    prev = pl.BlockSpec((BAND_HALF, width), lambda r, i: (jnp.maximum(i * hb - 1, 0), r))
    nxt = pl.BlockSpec((BAND_HALF, width), lambda r, i: (jnp.minimum((i + 1) * hb, last), r))
    return main, [prev, main, nxt]


def _pack_heads(cols):
    lane = lax.broadcasted_iota(jnp.int32, (cols[0].shape[0], STAT_W), 1)
    out = jnp.zeros((cols[0].shape[0], STAT_W), F32)
    for h, c in enumerate(cols):
        out = jnp.where(lane == h, c, out)
    return out


def _spread_heads(stat):
    head = lax.broadcasted_iota(jnp.int32, (stat.shape[0], D_B), 1) >> 6
    out = jnp.zeros((stat.shape[0], D_B), F32)
    for h in range(N_HEADS_B):
        out = jnp.where(head == h, stat[:, h:h + 1], out)
    return out


def _window(refs):
    return jnp.concatenate([r[...] for r in refs], axis=0)


def _head(v, h):
    return v[:, HEAD_DIM * h:HEAD_DIM * (h + 1)]


def _call_with_rider(body, name, grid, in_specs, out_specs, out_shape, scratch_shapes, args, rider):
    n_in, n_out = len(in_specs), len(out_specs)
    if rider is None:
        res = pl.pallas_call(body, name=name, grid=grid, in_specs=in_specs, out_specs=out_specs, out_shape=out_shape,
                             scratch_shapes=scratch_shapes, compiler_params=_params(len(grid), V7X_VMEM_LIMIT))(*args)
        return res, []
    kind, arrays = rider
    n = len(arrays)

    def with_rider(*refs):
        ins, c_ins = refs[:n_in], refs[n_in:n_in + n]
        outs, c_outs = refs[n_in + n:n_in + n + n_out], refs[n_in + n + n_out:n_in + 2 * n + n_out]
        rest = refs[n_in + 2 * n + n_out:]
        scratch, sems = rest[:-3], rest[-3:]
        ids = [pl.program_id(a) for a in range(len(grid))]
        first = functools.reduce(jnp.logical_and, [i == 0 for i in ids])
        last = functools.reduce(jnp.logical_and, [i == g - 1 for i, g in zip(ids, grid)])

        @pl.when(first)
        def _():
            _comm_start(kind, c_ins, c_outs, sems)

        body(*ins, *outs, *scratch)

        @pl.when(last)
        def _():
            _comm_wait(kind, c_ins, c_outs, sems)

    res = pl.pallas_call(
        with_rider, name=name, grid=grid, in_specs=list(in_specs) + [ANY_SPEC] * n,
        out_specs=list(out_specs) + [ANY_SPEC] * n, out_shape=list(out_shape) + _comm_out_shapes(kind, arrays),
        scratch_shapes=list(scratch_shapes) + _comm_sems(n), compiler_params=_params(len(grid), V7X_VMEM_LIMIT),
    )(*args, *arrays)
    return res[:n_out], res[n_out:]


def _band_fwd(q, k, v, bias, dil, t, rider=None):
    s_len = q.shape[0]
    length = s_len // dil
    w = t + 2 * BAND_HALF
    view = lambda a: a.reshape(length, dil * D_B)
    main, win = _band_specs(length, t)
    stat, _ = _band_specs(length, t, STAT_W)

    def body(q_ref, k0, k1, k2, v0, v1, v2, bias_ref, o_ref, lse_ref, s_buf, kt_buf):
        lses = []
        i = pl.program_id(1)
        kt_buf[...] = _window((k0, k1, k2)).T
        vw = _window((v0, v1, v2))
        pos = i * t - BAND_HALF + lax.broadcasted_iota(jnp.int32, (1, w), 1)
        valid = (pos >= 0) & (pos < length)
        for h in range(N_HEADS_B):
            s_buf[h] = _dot(q_ref[:, HEAD_DIM * h:HEAD_DIM * (h + 1)], kt_buf[HEAD_DIM * h:HEAD_DIM * (h + 1), :])
        for h in range(N_HEADS_B):
            s = jnp.where(valid, s_buf[h] + bias_ref[h], NEG_BIG)
            m = jnp.max(s, axis=-1, keepdims=True)
            e = jnp.exp(s - m)
            den = jnp.sum(e, axis=-1, keepdims=True)
            o_ref[:, HEAD_DIM * h:HEAD_DIM * (h + 1)] = _dot(e.astype(BF16), _head(vw, h)) / den
            lses.append(m + jnp.log(den))
        lse_ref[...] = _pack_heads(lses)

    (o, lse), rode = _call_with_rider(
        body, f"band_fwd_d{dil}", (dil, length // t),
        [main] + win + win + [_full_spec((N_HEADS_B, t, w))], [main, stat],
        [SDS((length, dil * D_B), F32), SDS((length, dil * STAT_W), F32)],
        [pltpu.VMEM((N_HEADS_B, t, w), F32), pltpu.VMEM((D_B, w), BF16)],
        (view(q), view(k), view(k), view(k), view(v), view(v), view(v), bias), rider)
    return o.reshape(s_len, D_B), lse.reshape(s_len, STAT_W), rode


def _band_dq(q, do, lse, delta, k, v, bias, dq_in, dil, t, rider=None):
    s_len = q.shape[0]
    length = s_len // dil
    w = t + 2 * BAND_HALF
    view = lambda a: a.reshape(length, dil * D_B)
    main, win = _band_specs(length, t)

    def body(q_ref, do_ref, lse_ref, delta_ref, k0, k1, k2, v0, v1, v2, bias_ref, acc_ref, dq_ref, dbias_ref,
             s_buf, dp_buf, kt_buf, vt_buf):
        i = pl.program_id(1)
        @pl.when((pl.program_id(0) == 0) & (i == 0))
        def _():
            dbias_ref[...] = jnp.zeros_like(dbias_ref)

        kw = _window((k0, k1, k2))
        kt_buf[...] = kw.T
        vt_buf[...] = _window((v0, v1, v2)).T
        pos = i * t - BAND_HALF + lax.broadcasted_iota(jnp.int32, (1, w), 1)
        valid = (pos >= 0) & (pos < length)
        for h in range(N_HEADS_B):
            cols = slice(HEAD_DIM * h, HEAD_DIM * (h + 1))
            s_buf[h] = _dot(q_ref[:, cols], kt_buf[cols, :])
            dp_buf[h] = _dot(do_ref[:, cols], vt_buf[cols, :])
        for h in range(N_HEADS_B):
            cols = slice(HEAD_DIM * h, HEAD_DIM * (h + 1))
            s = jnp.where(valid, s_buf[h] + bias_ref[h], NEG_BIG)
            p = jnp.exp(s - lse_ref[:, h:h + 1])
            ds = p * (dp_buf[h] - delta_ref[:, h:h + 1])
            dq_ref[:, cols] = acc_ref[:, cols] + _dot(ds.astype(BF16), _head(kw, h)) * SCORE_SCALE
            dbias_ref[h] += ds

    stat, _ = _band_specs(length, t, STAT_W)
    sview = lambda a: a.reshape(length, dil * STAT_W)
    (dq, dbias), rode = _call_with_rider(
        body, f"band_dq_d{dil}", (dil, length // t),
        [main, main, stat, stat] + win + win + [_full_spec((N_HEADS_B, t, w)), main],
        [main, _full_spec((N_HEADS_B, t, w))],
        [SDS((length, dil * D_B), F32), SDS((N_HEADS_B, t, w), F32)],
        [pltpu.VMEM((N_HEADS_B, t, w), F32), pltpu.VMEM((N_HEADS_B, t, w), F32),
         pltpu.VMEM((D_B, w), BF16), pltpu.VMEM((D_B, w), BF16)],
        (view(q), view(do), sview(lse), sview(delta), view(k), view(k), view(k), view(v), view(v), view(v), bias,
         view(dq_in)), rider)
    return dq.reshape(s_len, D_B), dbias, rode


def _band_dkv(k, v, q, do, lse, delta, bias_t, dk_in, dv_in, dil, t):
    s_len = q.shape[0]
    length = s_len // dil
    w = t + 2 * BAND_HALF
    view = lambda a: a.reshape(length, dil * D_B)
    main, win = _band_specs(length, t)

    def body(k_ref, v_ref, q0, q1, q2, d0, d1, d2, l0, l1, l2, e0, e1, e2, bias_ref, dk_acc, dv_acc, dk_ref, dv_ref,
             s_buf, dp_buf, kt_buf, vt_buf):
        i = pl.program_id(1)
        qw = _window((q0, q1, q2))
        dow = _window((d0, d1, d2))
        lsew = _window((l0, l1, l2))
        deltaw = _window((e0, e1, e2))
        pos = i * t - BAND_HALF + lax.broadcasted_iota(jnp.int32, (w, 1), 0)
        valid = (pos >= 0) & (pos < length)
        kt_buf[...] = k_ref[...].T
        vt_buf[...] = v_ref[...].T
        for h in range(N_HEADS_B):
            cols = slice(HEAD_DIM * h, HEAD_DIM * (h + 1))
            s_buf[h] = _dot(_head(qw, h), kt_buf[cols, :])
            dp_buf[h] = _dot(_head(dow, h), vt_buf[cols, :])
        qwt = qw.T
        dowt = dow.T
        dkt, dvt = [], []
        for h in range(N_HEADS_B):
            rows = slice(HEAD_DIM * h, HEAD_DIM * (h + 1))
            s = jnp.where(valid, s_buf[h] + bias_ref[h], NEG_BIG)
            p = jnp.exp(s - lsew[:, h:h + 1])
            ds = p * (dp_buf[h] - deltaw[:, h:h + 1])
            dvt.append(_dot(dowt[rows, :], p.astype(BF16)))
            dkt.append(_dot(qwt[rows, :], ds.astype(BF16)))
        dv_ref[...] = dv_acc[...] + jnp.concatenate(dvt, axis=0).T
        dk_ref[...] = dk_acc[...] + jnp.concatenate(dkt, axis=0).T

    _, swin = _band_specs(length, t, STAT_W)
    sview = lambda a: a.reshape(length, dil * STAT_W)
    dk, dv = pl.pallas_call(
        body, name=f"band_dkv_d{dil}", grid=(dil, length // t),
        in_specs=[main, main] + win + win + swin + swin + [_full_spec((N_HEADS_B, w, t)), main, main],
        out_specs=[main, main],
        out_shape=(SDS((length, dil * D_B), F32), SDS((length, dil * D_B), F32)),
        scratch_shapes=[pltpu.VMEM((N_HEADS_B, w, t), F32), pltpu.VMEM((N_HEADS_B, w, t), F32),
                        pltpu.VMEM((D_B, t), BF16), pltpu.VMEM((D_B, t), BF16)],
        compiler_params=_params(2, V7X_VMEM_LIMIT),
    )(view(k), view(v), view(q), view(q), view(q), view(do), view(do), view(do), sview(lse), sview(lse), sview(lse),
      sview(delta), sview(delta), sview(delta), bias_t, view(dk_in), view(dv_in))
    return dk.reshape(s_len, D_B), dv.reshape(s_len, D_B)


def _t5_bucket_np(rel):
    nb = N_BUCKETS // 2
    max_exact = nb // 2
    side = np.where(rel > 0, nb, 0)
    n = np.abs(rel)
    ratio = np.maximum(n, max_exact).astype(np.float32) / np.float32(max_exact)
    large = max_exact + (np.log(ratio) / np.float32(math.log(MAX_DISTANCE / max_exact))
                         * np.float32(nb - max_exact)).astype(np.int32)
    large = np.minimum(large, nb - 1)
    return (side + np.where(n < max_exact, n, large)).astype(np.int32)


def _band_buckets(dil, t):
    rel = np.arange(t + 2 * BAND_HALF)[None, :] - BAND_HALF - np.arange(t)[:, None]
    bucket = _t5_bucket_np(np.clip(rel, -BAND_HALF, BAND_HALF) * dil)
    return np.where(np.abs(rel) <= BAND_HALF, bucket, -1).astype(np.int32)


def _toeplitz(vals, rows, cols):
    heads = vals.shape[0]
    period = rows + cols
    vec = jnp.concatenate([vals[:, rows - 1:], jnp.zeros((heads, 1), vals.dtype), vals[:, :rows - 1]], axis=1)
    flat = jnp.broadcast_to(vec[:, None, :], (heads, rows, period)).reshape(heads, rows * period)
    return flat[:, :rows * (period - 1)].reshape(heads, rows, period - 1)[:, :, :cols]


def _bias_tiles(rel_bias, dil, t):
    w = t + 2 * BAND_HALF
    rel = np.arange(-BAND_HALF, BAND_HALF + 1)
    bucket = _t5_bucket_np(rel * dil)
    runs, start = [], 0
    for i in range(1, len(bucket) + 1):
        if i == len(bucket) or bucket[i] != bucket[start]:
            b = int(bucket[start])
            runs.append(jnp.broadcast_to(rel_bias[b:b + 1], (i - start, N_HEADS_B)))
            start = i
    per_rel = jnp.concatenate(runs, axis=0).T

    def diagonals(lo, hi):
        left = jnp.full((N_HEADS_B, max(0, -BAND_HALF - lo)), NEG_BIG, F32)
        right = jnp.full((N_HEADS_B, max(0, hi - BAND_HALF)), NEG_BIG, F32)
        return jnp.concatenate([left, per_rel, right], axis=1)

    tile = _toeplitz(diagonals(-(t - 1) - BAND_HALF, w - 1 - BAND_HALF), t, w)
    twin = _toeplitz(diagonals(-(w - 1) + BAND_HALF, t - 1 + BAND_HALF), w, t)
    return tile, twin


def _dbias_reduce(dbias, buckets):
    n = len(dbias)

    def body(*refs):
        db_refs, bk_refs, o_ref = refs[:n], refs[n:2 * n], refs[2 * n]
        row = lax.broadcasted_iota(jnp.int32, (N_BUCKETS, 128), 0)
        lane = lax.broadcasted_iota(jnp.int32, (N_BUCKETS, 128), 1)

        def per_bucket(b, out):
            for pat in range(n):
                msk = bk_refs[pat][...] == b
                for h in range(N_HEADS_B):
                    tot = jnp.sum(jnp.where(msk, db_refs[pat][h], 0.0), axis=-1, keepdims=True)
                    tot = jnp.sum(tot, axis=0, keepdims=True)
                    out = out + jnp.where((row == b) & (lane == h), tot, 0.0)
            return out

        o_ref[...] = lax.fori_loop(0, N_BUCKETS, per_bucket, jnp.zeros((N_BUCKETS, 128), F32))

    return pl.pallas_call(
        body, name="dbias_reduce", out_shape=SDS((N_BUCKETS, 128), F32),
        compiler_params=pltpu.CompilerParams(vmem_limit_bytes=V7X_VMEM_LIMIT),
    )(*dbias, *buckets)


def _attn_out_fwd(ya, ob, lb, x, g_a, g_b, w_out, g_post, tm):
    s_len = ya.shape[0]

    def body(ya_ref, o0, o1, o2, l0, l1, l2, x_ref, ga_ref, gb_ref, w_ref, gp_ref,
             h1_ref, yo_ref, yb_ref, lse_ref, ycat_ref):
        m = jnp.maximum(jnp.maximum(l0[...], l1[...]), l2[...])
        w0, w1, w2 = jnp.exp(l0[...] - m), jnp.exp(l1[...] - m), jnp.exp(l2[...] - m)
        wsum = w0 + w1 + w2
        yb = (_spread_heads(w0 / wsum) * o0[...] + _spread_heads(w1 / wsum) * o1[...]
              + _spread_heads(w2 / wsum) * o2[...])
        yb_ref[...] = yb
        lse_ref[...] = m + jnp.log(wsum)
        yan, _ = _rms_fwd(ya_ref[...], ga_ref[...])
        ybn, _ = _rms_fwd(yb, gb_ref[...])
        yan, ybn = yan.astype(BF16), ybn.astype(BF16)
        ycat_ref[:, :D_A] = yan
        ycat_ref[:, D_A:] = ybn
        yo = _dot(yan, w_ref[:D_A, :]) + _dot(ybn, w_ref[D_A:, :])
        yo_ref[...] = yo
        post, _ = _rms_fwd(yo, gp_ref[...])
        h1_ref[...] = x_ref[...] + post

    half, full, stat = _row_spec(tm, D_A), _row_spec(tm, D_MODEL), _row_spec(tm, STAT_W)
    return pl.pallas_call(
        body, name="attn_out_fwd", grid=(s_len // tm,),
        in_specs=[half] * 4 + [stat] * 3 + [full, _full_spec((1, D_A)), _full_spec((1, D_B)),
                                            _full_spec((D_MODEL, D_MODEL)), _full_spec((1, D_MODEL))],
        out_specs=[full, full, half, stat, full],
        out_shape=(SDS((s_len, D_MODEL), F32), SDS((s_len, D_MODEL), F32), SDS((s_len, D_B), F32),
                   SDS((s_len, STAT_W), F32), SDS((s_len, D_MODEL), BF16)),
        compiler_params=_params(1, V7X_VMEM_LIMIT),
    )(ya, ob[0], ob[1], ob[2], lb[0], lb[1], lb[2], x, g_a, g_b, w_out, g_post)


def _attn_out_bwd(dh1, yo, ya, yb, g_a, g_b, w_out, g_post, tm):
    s_len = ya.shape[0]

    def body(dh1_ref, yo_ref, ya_ref, yb_ref, ga_ref, gb_ref, w_ref, gp_ref,
             dyo_ref, dya_ref, dyb_ref, dela_ref, delb_ref, dgp_ref, dga_ref, dgb_ref):
        _zero_at_first(pl.program_id(0) == 0, dgp_ref, dga_ref, dgb_ref)
        dyo, dgp = _rms_bwd(yo_ref[...], gp_ref[...], dh1_ref[...])
        dyo = dyo.astype(BF16)
        dyo_ref[...] = dyo
        dgp_ref[...] += dgp
        dya_n = _dot_nt(dyo, w_ref[:D_A, :])
        dyb_n = _dot_nt(dyo, w_ref[D_A:, :])
        ya, yb = ya_ref[...], yb_ref[...]
        dya, dga = _rms_bwd(ya, ga_ref[...], dya_n)
        dyb, dgb = _rms_bwd(yb, gb_ref[...], dyb_n)
        dga_ref[...] += dga
        dgb_ref[...] += dgb
        dya_ref[...] = dya.astype(BF16)
        dyb_ref[...] = dyb.astype(BF16)
        dela_ref[...] = _head_sum(dya * ya)
        prod = dyb * yb
        head = lax.broadcasted_iota(jnp.int32, prod.shape, 1) >> 6
        delb_ref[...] = _pack_heads([jnp.sum(jnp.where(head == h, prod, 0.0), axis=-1, keepdims=True)
                                     for h in range(N_HEADS_B)])

    half, full = _row_spec(tm, D_A), _row_spec(tm, D_MODEL)
    return pl.pallas_call(
        body, name="attn_out_bwd", grid=(s_len // tm,),
        in_specs=[full, full, half, half, _full_spec((1, D_A)), _full_spec((1, D_B)),
                  _full_spec((D_MODEL, D_MODEL)), _full_spec((1, D_MODEL))],
        out_specs=[full, half, half, half, _row_spec(tm, STAT_W), _full_spec((1, D_MODEL)), _full_spec((1, D_A)),
                   _full_spec((1, D_B))],
        out_shape=(SDS((s_len, D_MODEL), BF16), SDS((s_len, D_A), BF16), SDS((s_len, D_B), BF16),
                   SDS((s_len, D_A), F32), SDS((s_len, STAT_W), F32), SDS((1, D_MODEL), F32), SDS((1, D_A), F32),
                   SDS((1, D_B), F32)),
        compiler_params=_params(1, V7X_VMEM_LIMIT),
    )(dh1, yo, ya, yb, g_a, g_b, w_out, g_post)


FF_CHUNK = 1024


def _mlp_fwd(h1, g_pre, w1, w2, g_post, tm):
    s_len = h1.shape[0]

    def body(h1_ref, gpre_ref, w1_ref, w2_ref, gpost_ref, h2_ref, fo_ref, xn_ref):
        h1v = h1_ref[...]
        xn, _ = _rms_fwd(h1v, gpre_ref[...])
        xn = xn.astype(BF16)
        xn_ref[...] = xn
        fo = jnp.zeros((tm, D_MODEL), F32)
        for c in range(D_FF // FF_CHUNK):
            cols = slice(c * FF_CHUNK, (c + 1) * FF_CHUNK)
            u = jnp.maximum(_dot(xn, w1_ref[:, cols]), 0.0)
            fo = fo + _dot((u * u).astype(BF16), w2_ref[cols, :])
        fo_ref[...] = fo
        post, _ = _rms_fwd(fo, gpost_ref[...])
        h2_ref[...] = h1v + post

    full = _row_spec(tm, D_MODEL)
    return pl.pallas_call(
        body, name="mlp_fwd", grid=(s_len // tm,),
        in_specs=[full, _full_spec((1, D_MODEL)), _const_spec((D_MODEL, D_FF)), _const_spec((D_FF, D_MODEL)),
                  _full_spec((1, D_MODEL))],
        out_specs=[full, full, full],
        out_shape=(SDS((s_len, D_MODEL), F32), SDS((s_len, D_MODEL), F32), SDS((s_len, D_MODEL), BF16)),
        compiler_params=_params(1, V7X_VMEM_LIMIT),
    )(h1, g_pre, w1, w2, g_post)


def _mlp_bwd(h1, dh2, fo, xn, g_pre, w1, w2, g_post, tm):
    s_len = h1.shape[0]

    def body(h1_ref, dh2_ref, fo_ref, xn_ref, gpre_ref, w1_ref, w2_ref, gpost_ref,
             dh1_ref, dfo_ref, du_ref, f_ref, dgpost_ref, dgpre_ref):
        _zero_at_first(pl.program_id(0) == 0, dgpost_ref, dgpre_ref)
        dh2 = dh2_ref[...]
        dfo, dgpost = _rms_bwd(fo_ref[...], gpost_ref[...], dh2)
        dfo = dfo.astype(BF16)
        dfo_ref[...] = dfo
        dgpost_ref[...] += dgpost
        xn = xn_ref[...]
        dxn = jnp.zeros((tm, D_MODEL), F32)
        for c in range(D_FF // FF_CHUNK):
            cols = slice(c * FF_CHUNK, (c + 1) * FF_CHUNK)
            u = jnp.maximum(_dot(xn, w1_ref[:, cols]), 0.0)
            f_ref[:, cols] = (u * u).astype(BF16)
            du = (_dot_nt(dfo, w2_ref[cols, :]) * (2.0 * u)).astype(BF16)
            du_ref[:, cols] = du
            dxn = dxn + _dot_nt(du, w1_ref[:, cols])
        dx, dgpre = _rms_bwd(h1_ref[...], gpre_ref[...], dxn)
        dgpre_ref[...] += dgpre
        dh1_ref[...] = dh2 + dx

    full, wide = _row_spec(tm, D_MODEL), _row_spec(tm, D_FF)
    return pl.pallas_call(
        body, name="mlp_bwd", grid=(s_len // tm,),
        in_specs=[full, full, full, full, _full_spec((1, D_MODEL)), _const_spec((D_MODEL, D_FF)),
                  _const_spec((D_FF, D_MODEL)), _full_spec((1, D_MODEL))],
        out_specs=[full, full, wide, wide, _full_spec((1, D_MODEL)), _full_spec((1, D_MODEL))],
        out_shape=(SDS((s_len, D_MODEL), F32), SDS((s_len, D_MODEL), BF16), SDS((s_len, D_FF), BF16),
                   SDS((s_len, D_FF), BF16), SDS((1, D_MODEL), F32), SDS((1, D_MODEL), F32)),
        compiler_params=_params(1, V7X_VMEM_LIMIT),
    )(h1, dh2, fo, xn, g_pre, w1, w2, g_post)


def _ple_fwd_bwd(h2, p, target, g_ple, w_gate, w_proj, tm):
    s_len = h2.shape[0]

    def body(h2_ref, p_ref, t_ref, g_ref, wg_ref, wp_ref, dh2_ref, loss_ref, dg_ref, xn_ref, dgl_ref, dpp_ref, pb_ref):
        _zero_at_first(pl.program_id(0) == 0, loss_ref, dg_ref)
        h2 = h2_ref[...]
        g = g_ref[...]
        xn, _ = _rms_fwd(h2, g)
        xn = xn.astype(BF16)
        xn_ref[...] = xn
        gate = 1.0 / (1.0 + jnp.exp(-_dot(xn, wg_ref[...])))
        pb = p_ref[...].astype(BF16)
        pb_ref[...] = pb
        pp = _dot(pb, wp_ref[...])
        diff = h2 + gate * pp - t_ref[...]
        loss_ref[...] += jnp.full((8, 128), jnp.sum(diff * diff), F32)
        dh3 = diff * (1.0 / D_MODEL)
        dpp_ref[...] = (dh3 * gate).astype(BF16)
        dgl = (dh3 * pp * gate * (1.0 - gate)).astype(BF16)
        dgl_ref[...] = dgl
        dx, dg = _rms_bwd(h2, g, _dot_nt(dgl, wg_ref[...]))
        dg_ref[...] += dg
        dh2_ref[...] = dh3 + dx

    full = _row_spec(tm, D_MODEL)
    return pl.pallas_call(
        body, name="ple_fwd_bwd", grid=(s_len // tm,),
        in_specs=[full, _row_spec(tm, D_PLE), full, _full_spec((1, D_MODEL)), _full_spec((D_MODEL, D_MODEL)),
                  _full_spec((D_PLE, D_MODEL))],
        out_specs=[full, _full_spec((8, 128)), _full_spec((1, D_MODEL)), full, full, full, _row_spec(tm, D_PLE)],
        out_shape=(SDS((s_len, D_MODEL), F32), SDS((8, 128), F32), SDS((1, D_MODEL), F32), SDS((s_len, D_MODEL), BF16),
                   SDS((s_len, D_MODEL), BF16), SDS((s_len, D_MODEL), BF16), SDS((s_len, D_PLE), BF16)),
        compiler_params=_params(1, V7X_VMEM_LIMIT),
    )(h2, p, target, g_ple, w_gate, w_proj)


def _inproj_bwd(dqs, dk_t, dv_t, dqb, dkb, dvb, qa_raw, ka_raw, x, dh1, g_pre, w_in, cos, sin, gq, gk, tm):
    s_len = x.shape[0]
    tk = dk_t.shape[3]

    def body(dqs_ref, dkt_ref, dvt_ref, dqb_ref, dkb_ref, dvb_ref, qa_ref, ka_ref, x_ref, dh1_ref, g_ref, w_ref,
             cos_ref, sin_ref, gq_ref, gk_ref, dx_ref, dproj_ref, dg_ref, dgq_ref, dgk_ref):
        _zero_at_first(pl.program_id(0) == 0, dg_ref, dgq_ref, dgk_ref)
        cos, sin = cos_ref[...], sin_ref[...]
        dkn = dkt_ref[...].reshape(D_KV_A, tm).T
        dva = dvt_ref[...].reshape(D_KV_A, tm).T
        dqa, dgq = _head_norm_rope_bwd(qa_ref[...], gq_ref[...], jnp.tile(cos, (1, D_A // D_KV_A)),
                                       jnp.tile(sin, (1, D_A // D_KV_A)), dqs_ref[...])
        dka, dgk = _head_norm_rope_bwd(ka_ref[...], gk_ref[...], cos, sin, dkn)
        dgq_ref[...] += dgq
        dgk_ref[...] += dgk
        dproj_ref[:, OFF_QA:OFF_KA] = dqa.astype(BF16)
        dproj_ref[:, OFF_KA:OFF_VA] = dka.astype(BF16)
        dproj_ref[:, OFF_VA:OFF_QB] = dva.astype(BF16)
        dproj_ref[:, OFF_QB:OFF_KB] = dqb_ref[...].astype(BF16)
        dproj_ref[:, OFF_KB:OFF_VB] = dkb_ref[...].astype(BF16)
        dproj_ref[:, OFF_VB:D_IN] = dvb_ref[...].astype(BF16)
        dxn = _dot_nt(dproj_ref[...], w_ref[...])
        dx, dg = _rms_bwd(x_ref[...], g_ref[...], dxn)
        dg_ref[...] += dg
        dx_ref[...] = dh1_ref[...] + dx

    half, kvw, full = _row_spec(tm, D_A), _row_spec(tm, D_KV_A), _row_spec(tm, D_MODEL)
    per_chunk = tk // tm
    chunk_t = pl.BlockSpec((N_KV_A, None, HEAD_DIM, tm), lambda i: (0, i // per_chunk, 0, i % per_chunk))
    return pl.pallas_call(
        body, name="inproj_bwd", grid=(s_len // tm,),
        in_specs=[half, chunk_t, chunk_t, half, half, half, half, kvw, full, full, _full_spec((1, D_MODEL)),
                  _const_spec((D_MODEL, D_IN)), kvw, kvw, _full_spec((1, D_A)), _full_spec((1, D_KV_A))],
        out_specs=[full, _row_spec(tm, D_IN), _full_spec((1, D_MODEL)), _full_spec((1, D_A)), _full_spec((1, D_KV_A))],
        out_shape=(SDS((s_len, D_MODEL), F32), SDS((s_len, D_IN), BF16), SDS((1, D_MODEL), F32), SDS((1, D_A), F32),
                   SDS((1, D_KV_A), F32)),
        compiler_params=_params(1, V7X_VMEM_LIMIT),
    )(dqs, dk_t, dv_t, dqb, dkb, dvb, qa_raw, ka_raw, x, dh1, g_pre, w_in, cos, sin, gq, gk)


def _weight_grad(a, b, name, tk1, tn, tm):
    s_len, k1 = a.shape
    n = b.shape[1]
    steps = s_len // tm

    def body(a_ref, b_ref, o_ref, acc):
        r = pl.program_id(2)
        _zero_at_first(r == 0, acc)
        acc[...] += _dot_tn(a_ref[...], b_ref[...])

        @pl.when(r == steps - 1)
        def _():
            o_ref[...] = acc[...].astype(BF16)

    return pl.pallas_call(
        body, name=name, grid=(k1 // tk1, n // tn, steps),
        in_specs=[pl.BlockSpec((tm, tk1), lambda i, j, r: (r, i)), pl.BlockSpec((tm, tn), lambda i, j, r: (r, j))],
        out_specs=pl.BlockSpec((tk1, tn), lambda i, j, r: (i, j)),
        out_shape=SDS((k1, n), BF16),
        scratch_shapes=[pltpu.VMEM((tk1, tn), F32)],
        compiler_params=_params(3, V7X_VMEM_LIMIT),
    )(a, b)


def _my_index():
    return 4 * lax.axis_index("x") + 2 * lax.axis_index("y") + lax.axis_index("c")


def _peer(k):
    f = k + 1
    x, y, c = lax.axis_index("x"), lax.axis_index("y"), lax.axis_index("c")
    return (x ^ ((f >> 2) & 1), y ^ ((f >> 1) & 1), c ^ (f & 1))


GATHER, EXCHANGE = "gather", "exchange"
ANY_SPEC = pl.BlockSpec(memory_space=pl.ANY)


def _comm_out_shapes(kind, arrays):
    return [SDS((N_DEV,) + a.shape if kind == GATHER else a.shape, a.dtype) for a in arrays]


def _comm_sems(n):
    return [pltpu.SemaphoreType.DMA((n, N_DEV - 1)), pltpu.SemaphoreType.DMA((n, N_DEV - 1)), pltpu.SemaphoreType.DMA((n,))]


def _comm_copies(kind, ins, outs, send_sems, recv_sems, local_sems):
    me = _my_index()
    local, remote = [], []
    for a, (src, dst) in enumerate(zip(ins, outs)):
        local.append(pltpu.make_async_copy(src if kind == GATHER else src.at[me], dst.at[me], local_sems.at[a]))
        for k in range(N_DEV - 1):
            px, py, pc = _peer(k)
            remote.append(pltpu.make_async_remote_copy(
                src_ref=src if kind == GATHER else src.at[4 * px + 2 * py + pc], dst_ref=dst.at[me],
                send_sem=send_sems.at[a, k], recv_sem=recv_sems.at[a, k],
                device_id=(px, py, pc), device_id_type=pl.DeviceIdType.MESH))
    return local, remote


def _comm_start(kind, ins, outs, sems):
    local, remote = _comm_copies(kind, ins, outs, *sems)
    for cp in local + remote:
        cp.start()


def _comm_wait(kind, ins, outs, sems):
    local, remote = _comm_copies(kind, ins, outs, *sems)
    for cp in local:
        cp.wait()
    for cp in remote:
        cp.wait_send()
    for cp in remote:
        cp.wait_recv()


def _collective(kind, arrays, name):
    n = len(arrays)

    def body(*refs):
        ins, outs, sems = refs[:n], refs[n:2 * n], refs[2 * n:]
        _comm_start(kind, ins, outs, sems)
        _comm_wait(kind, ins, outs, sems)

    return pl.pallas_call(
        body, name=name, in_specs=[ANY_SPEC] * n, out_specs=[ANY_SPEC] * n,
        out_shape=_comm_out_shapes(kind, arrays), scratch_shapes=_comm_sems(n),
    )(*arrays)


def _sum_adamw(parts, w, m, v, name, tr):
    rows, cols = w.shape
    c1 = 1.0 / (1.0 - ADAM_B1 ** ADAM_STEP)
    c2 = 1.0 / (1.0 - ADAM_B2 ** ADAM_STEP)

    def body(p_ref, w_ref, m_ref, v_ref, g_ref, d_ref, nm_ref, nv_ref):
        g = p_ref[0].astype(F32)
        for j in range(1, N_DEV):
            g = g + p_ref[j].astype(F32)
        g_ref[...] = g
        nm = ADAM_B1 * m_ref[...] + (1.0 - ADAM_B1) * g
        nv = ADAM_B2 * v_ref[...] + (1.0 - ADAM_B2) * (g * g)
        nm_ref[...] = nm
        nv_ref[...] = nv
        d_ref[...] = -ADAM_LR * ((nm * c1) / (jnp.sqrt(nv * c2) + ADAM_EPS) + ADAM_WD * w_ref[...])

    blk = pl.BlockSpec((tr, cols), lambda i: (i, 0))
    return pl.pallas_call(
        body, name=name, grid=(rows // tr,),
        in_specs=[pl.BlockSpec((N_DEV, tr, cols), lambda i: (0, i, 0)), blk, blk, blk],
        out_specs=[blk] * 4, out_shape=[SDS((rows, cols), F32)] * 4,
        compiler_params=_params(1, V7X_VMEM_LIMIT),
    )(parts, w, m, v)


_SMALL = (("g_attn_pre", 1024), ("g_q", 64), ("g_k", 64), ("g_out_a", 512), ("g_out_b", 512), ("g_attn_post", 1024),
          ("rel_bias", 256), ("g_mlp_pre", 1024), ("g_mlp_post", 1024), ("g_ple", 1024))
_SLAB_ROWS = 56


def _pack_small(vals):
    rows = []
    for (name, size) in _SMALL:
        flat = vals[name].reshape(-1).astype(F32)
        padded = -(-size // 128) * 128
        rows.append(jnp.pad(flat, (0, padded - size)).reshape(padded // 128, 128))
    slab = jnp.concatenate(rows, axis=0)
    return jnp.pad(slab, ((0, _SLAB_ROWS - slab.shape[0]), (0, 0)))


def _unpack_small(slab, shapes):
    out, row = {}, 0
    for (name, size) in _SMALL:
        nrow = -(-size // 128)
        out[name] = slab[row:row + nrow].reshape(-1)[:size].reshape(shapes[name])
        row += nrow
    return out


def _rope_tables(s_len):
    rows = s_len // GRID_W
    row = jnp.broadcast_to(jnp.arange(rows)[:, None], (rows, GRID_W)).reshape(-1).astype(F32)
    col = jnp.broadcast_to(jnp.arange(GRID_W)[None, :], (rows, GRID_W)).reshape(-1).astype(F32)
    n_axis = ROPE_HALF // 2
    inv_freq = ROPE_THETA ** (-jnp.arange(n_axis, dtype=F32) / n_axis)
    ang = jnp.concatenate([row[:, None] * inv_freq, col[:, None] * inv_freq], axis=-1)
    cos, sin = jnp.cos(ang), jnp.sin(ang)
    cos = jnp.tile(jnp.concatenate([cos, cos], axis=-1), (1, N_KV_A))
    sin = jnp.tile(jnp.concatenate([-sin, sin], axis=-1), (1, N_KV_A))
    return cos, sin


_RIDERS = {1: ("w_out", "w_ff1"), 4: ("w_ff2",), 16: ("w_ple_gate", "w_ple_proj")}
_GRAD_RIDERS = {1: ("w_ff1",), 4: ("w_ff2",), 16: ("w_ple_gate", "w_ple_proj", "w_out")}


def _local_step(x, p, target, w_in, shards, small):
    axis_of = dict(_BIG)
    s_len = x.shape[0]
    tm = min(256, s_len)
    tq = min(128, s_len)
    tk = min(2048, s_len // 2)
    cos, sin = _rope_tables(s_len)
    gq = jnp.tile(small["g_q"], (1, N_HEADS_A))
    gk = jnp.tile(small["g_k"], (1, N_KV_A))

    qa_raw, ka_raw, qs, kn, va, qb, kb, vb, xn1 = _inproj_fwd(x, small["g_attn_pre"], w_in, cos, sin, gq, gk, tm)

    def kv_major(a):
        return a.reshape(s_len, N_KV_A, HEAD_DIM).transpose(1, 0, 2)

    def kv_chunks_t(a):
        return a.reshape(s_len // tk, tk, N_KV_A, HEAD_DIM).transpose(2, 0, 3, 1)

    k_maj, kt, vt = kv_major(kn), kv_chunks_t(kn), kv_chunks_t(va)
    vt_ones = jnp.concatenate([vt, jnp.ones((N_KV_A, s_len // tk, 16, tk), BF16)], axis=2)
    ya, lse_a = _attn_fwd(qs, k_maj, vt_ones, tq, tk)

    ob, lb, tiles, full = [], [], [], {}
    for (_, dil) in DILATED_PATTERNS:
        t = min(256, s_len // dil)
        bias, bias_t = _bias_tiles(small["rel_bias"], dil, t)
        tiles.append((t, bias, bias_t, _band_buckets(dil, t)))
        o, l, gathered = _band_fwd(qb, kb, vb, bias, dil, t, (GATHER, [shards[n] for n in _RIDERS[dil]]))
        ob.append(o)
        lb.append(l)
        full.update({n: _assemble(g, axis_of[n]) for n, g in zip(_RIDERS[dil], gathered)})
    w_out, w_ff1, w_ff2, w_gate, w_proj = (full[n] for n in ("w_out", "w_ff1", "w_ff2", "w_ple_gate", "w_ple_proj"))

    h1, yo, yb, lse_b, ycat = _attn_out_fwd(ya, ob, lb, x, small["g_out_a"], small["g_out_b"], w_out,
                                            small["g_attn_post"], tm)
    h2, fo, xn2 = _mlp_fwd(h1, small["g_mlp_pre"], w_ff1, w_ff2, small["g_mlp_post"], tm)
    dh2, loss_part, dg_ple, xn3, dgl, dpp, pb = _ple_fwd_bwd(h2, p, target, small["g_ple"], w_gate, w_proj, tm)
    dh1, dfo, du, f, dg_mlp_post, dg_mlp_pre = _mlp_bwd(h1, dh2, fo, xn2, small["g_mlp_pre"], w_ff1, w_ff2,
                                                          small["g_mlp_post"], tm)
    dyo, dya, dyb, delta_a, delta_b, dg_attn_post, dg_out_a, dg_out_b = _attn_out_bwd(
        dh1, yo, ya, yb, small["g_out_a"], small["g_out_b"], w_out, small["g_attn_post"], tm)

    dqs, dk_t, dv_t = _attn_bwd(qs, dya, lse_a, delta_a, kt, k_maj, vt, tq, tk)

    tg = min(512, s_len)
    grads = {
        "w_out": _weight_grad(ycat, dyo, "grad_w_out", D_MODEL, D_MODEL, tg),
        "w_ff1": _weight_grad(xn2, du, "grad_w_ff1", D_MODEL, 1024, tg),
        "w_ff2": _weight_grad(f, dfo, "grad_w_ff2", 1024, D_MODEL, tg),
        "w_ple_gate": _weight_grad(xn3, dgl, "grad_w_ple_gate", D_MODEL, D_MODEL, tg),
        "w_ple_proj": _weight_grad(pb, dpp, "grad_w_ple_proj", D_PLE, D_MODEL, tg),
    }

    dqb = dkb = dvb = jnp.zeros((s_len, D_B), F32)
    dbias, received = [], {}
    for (_, dil), (t, bias, bias_t, _) in reversed(list(zip(DILATED_PATTERNS, tiles))):
        rider = (EXCHANGE, [_cut(grads[n], axis_of[n]) for n in _GRAD_RIDERS[dil]])
        dqb, db, got = _band_dq(qb, dyb, lse_b, delta_b, kb, vb, bias, dqb, dil, t, rider)
        dkb, dvb = _band_dkv(kb, vb, qb, dyb, lse_b, delta_b, bias_t, dkb, dvb, dil, t)
        dbias.insert(0, db)
        received.update(zip(_GRAD_RIDERS[dil], got))
    d_rel = _dbias_reduce(dbias, [jnp.asarray(tl[3]) for tl in tiles])[:, :N_HEADS_B]

    dx, dproj, dg_attn_pre, dgq_lanes, dgk_lanes = _inproj_bwd(
        dqs, dk_t, dv_t, dqb, dkb, dvb, qa_raw, ka_raw, x, dh1, small["g_attn_pre"], w_in, cos, sin, gq, gk, tm)
    grad_w_in = _weight_grad(xn1, dproj, "grad_w_in", D_MODEL, 768, tg)
    small_grads = {
        "g_attn_pre": dg_attn_pre, "g_q": dgq_lanes.reshape(N_HEADS_A, HEAD_DIM).sum(0, keepdims=True),
        "g_k": dgk_lanes.reshape(N_KV_A, HEAD_DIM).sum(0, keepdims=True), "g_out_a": dg_out_a, "g_out_b": dg_out_b,
        "g_attn_post": dg_attn_post, "rel_bias": d_rel, "g_mlp_pre": dg_mlp_pre, "g_mlp_post": dg_mlp_post,
        "g_ple": dg_ple,
    }
    return loss_part[0, 0], dx, grad_w_in, received, small_grads


_BIG = (("w_in", 1), ("w_out", 0), ("w_ff1", 1), ("w_ff2", 0), ("w_ple_gate", 0), ("w_ple_proj", 1))


def _assemble(gathered, axis):
    if axis == 0:
        return gathered.reshape(-1, gathered.shape[2])
    return gathered.transpose(1, 0, 2).reshape(gathered.shape[1], -1)


def _cut(full, axis):
    if axis == 0:
        return full.reshape(N_DEV, full.shape[0] // N_DEV, full.shape[1])
    return full.reshape(full.shape[0], N_DEV, full.shape[1] // N_DEV).transpose(1, 0, 2)


def kernel(x, p, w_in, g_attn_pre, g_q, g_k, g_out_a, g_out_b, w_out, g_attn_post, rel_bias, g_mlp_pre, w_ff1, w_ff2, g_mlp_post, g_ple, w_ple_gate, w_ple_proj, loss_target, m_w_in, m_g_attn_pre, m_g_q, m_g_k, m_g_out_a, m_g_out_b, m_w_out, m_g_attn_post, m_rel_bias, m_g_mlp_pre, m_w_ff1, m_w_ff2, m_g_mlp_post, m_g_ple, m_w_ple_gate, m_w_ple_proj, v_w_in, v_g_attn_pre, v_g_q, v_g_k, v_g_out_a, v_g_out_b, v_w_out, v_g_attn_post, v_rel_bias, v_g_mlp_pre, v_w_ff1, v_w_ff2, v_g_mlp_post, v_g_ple, v_w_ple_gate, v_w_ple_proj):
    given = dict(locals())
    small_names = [n for n, _ in _SMALL]
    small = {n: given[n] for n in small_names}
    shards = {n: given[n][0] for n, _ in _BIG}

    bf16_shards = {n: shards[n].astype(BF16) for n, _ in _BIG}
    (gathered_w_in,) = _collective(GATHER, [bf16_shards["w_in"]], "gather_w_in")

    loss_part, dx, grad_w_in, received, small_grads = _local_step(
        x[0], p[0, 0], loss_target[0], _assemble(gathered_w_in, 1), bf16_shards, small)

    loss_at = np.zeros((_SLAB_ROWS, 128), bool)
    loss_at[-1, 0] = True
    slab = jnp.where(loss_at, loss_part * (0.5 / D_MODEL), _pack_small(small_grads))
    slab_parts = jnp.broadcast_to(slab[None], (N_DEV,) + slab.shape)
    received["w_in"], slabs = _collective(EXCHANGE, [_cut(grad_w_in, 1), slab_parts], "exchange_w_in_grads")

    out_g, out_d, out_m, out_v = {}, {}, {}, {}
    for n, _ in _BIG:
        rows = shards[n].shape[0]
        g, d, nm, nv = _sum_adamw(received[n], shards[n], given["m_" + n][0], given["v_" + n][0], "adamw_" + n,
                                  min(rows, 128))
        out_g[n], out_d[n], out_m[n], out_v[n] = g[None], d[None], nm[None], nv[None]
    g, d, nm, nv = _sum_adamw(slabs, _pack_small(small), _pack_small({n: given["m_" + n] for n in small_names}),
                              _pack_small({n: given["v_" + n] for n in small_names}), "adamw_small", _SLAB_ROWS)
    shapes = {n: given[n].shape for n in small_names}
    for dst, slab_out in ((out_g, g), (out_d, d), (out_m, nm), (out_v, nv)):
        dst.update(_unpack_small(slab_out, shapes))

    loss = g[-1, 0]
    order = ["w_in", "g_attn_pre", "g_q", "g_k", "g_out_a", "g_out_b", "w_out", "g_attn_post", "rel_bias", "g_mlp_pre",
             "w_ff1", "w_ff2", "g_mlp_post", "g_ple", "w_ple_gate", "w_ple_proj"]
    return (loss, dx[None], *[out_g[n] for n in order], *[out_d[n] for n in order], *[out_m[n] for n in order],
            *[out_v[n] for n in order])
```

```python
import functools
import math

import jax
import jax.numpy as jnp
import numpy as np
from jax import lax
from jax.experimental import pallas as pl
from jax.experimental.pallas import tpu as pltpu

F32 = jnp.float32
BF16 = jnp.bfloat16
SDS = jax.ShapeDtypeStruct

D_MODEL = 1024
HEAD_DIM = 64
N_HEADS_A = 8
N_KV_A = 2
GROUP_A = N_HEADS_A // N_KV_A
N_HEADS_B = 8
D_A = N_HEADS_A * HEAD_DIM
D_KV_A = N_KV_A * HEAD_DIM
D_B = N_HEADS_B * HEAD_DIM
D_IN = D_A + 2 * D_KV_A + 3 * D_B
D_FF = 4 * D_MODEL
D_PLE = 256
GRID_W = 64
ROPE_THETA = 10000.0
ROPE_HALF = HEAD_DIM // 2
DILATED_PATTERNS = ((128, 1), (512, 4), (2048, 16))
BAND_HALF = 64
N_BUCKETS = 32
MAX_DISTANCE = 1024
EPS = 1e-6
NEG_BIG = -1e30
SCORE_SCALE = HEAD_DIM ** -0.5

ADAM_LR = 0.001
ADAM_B1 = 0.9
ADAM_B2 = 0.999
ADAM_EPS = 1e-08
ADAM_WD = 0.01
ADAM_STEP = 10

N_DEV = 8
V7X_VMEM_LIMIT = 56 * 1024 * 1024

OFF_QA, OFF_KA, OFF_VA, OFF_QB, OFF_KB, OFF_VB = 0, 512, 640, 768, 1280, 1792


def _params(n_axes, vmem=None):
    return pltpu.CompilerParams(dimension_semantics=("arbitrary",) * n_axes, vmem_limit_bytes=vmem)


def _dot(a, b):
    return jnp.dot(a, b, preferred_element_type=F32)


def _dot_nt(a, b):
    return lax.dot_general(a, b, (((1,), (1,)), ((), ())), preferred_element_type=F32)


def _dot_tn(a, b):
    return lax.dot_general(a, b, (((0,), (0,)), ((), ())), preferred_element_type=F32)


def _rms_fwd(x, g):
    r = lax.rsqrt(jnp.mean(x * x, axis=-1, keepdims=True) + EPS)
    return x * r * g, r


def _rms_bwd(x, g, dy):
    r = lax.rsqrt(jnp.mean(x * x, axis=-1, keepdims=True) + EPS)
    xh = x * r
    dxh = dy * g
    dx = r * (dxh - xh * jnp.mean(dxh * xh, axis=-1, keepdims=True))
    return dx, jnp.sum(dy * xh, axis=0, keepdims=True)


def _head_sum(v):
    head = lax.broadcasted_iota(jnp.int32, v.shape, 1) >> 6
    out = jnp.zeros_like(v)
    for h in range(v.shape[1] // HEAD_DIM):
        msk = head == h
        s = jnp.sum(jnp.where(msk, v, 0.0), axis=-1, keepdims=True)
        out = jnp.where(msk, s, out)
    return out


def _swap_halves(v):
    w = v.shape[1]
    lane = lax.broadcasted_iota(jnp.int32, v.shape, 1)
    first_half = (lane & (HEAD_DIM - 1)) < ROPE_HALF
    return jnp.where(first_half, pltpu.roll(v, w - ROPE_HALF, 1), pltpu.roll(v, ROPE_HALF, 1))


def _head_norm_rope(v, g, cos, sin_signed):
    r = lax.rsqrt(_head_sum(v * v) * (1.0 / HEAD_DIM) + EPS)
    y = v * r * g
    return y * cos + _swap_halves(y) * sin_signed


def _head_norm_rope_bwd(v, g, cos, sin_signed, dout):
    dy = dout * cos - _swap_halves(dout) * sin_signed
    r = lax.rsqrt(_head_sum(v * v) * (1.0 / HEAD_DIM) + EPS)
    xh = v * r
    dxh = dy * g
    dv = r * (dxh - xh * (_head_sum(dxh * xh) * (1.0 / HEAD_DIM)))
    return dv, jnp.sum(dy * xh, axis=0, keepdims=True)


def _row_spec(tm, n):
    return pl.BlockSpec((tm, n), lambda i: (i, 0))


def _full_spec(shape):
    nd = len(shape)
    return pl.BlockSpec(shape, lambda *_: (0,) * nd)


def _const_spec(shape):
    nd = len(shape)
    return pl.BlockSpec(shape, lambda *_: (0,) * nd, pipeline_mode=pl.Buffered(1))


def _zero_at_first(first, *refs):
    @pl.when(first)
    def _():
        for ref in refs:
            ref[...] = jnp.zeros_like(ref)


DILATIONS = tuple(d for _, d in DILATED_PATTERNS if d > 1)


def _dilation_perm(tm, dil):
    rows = np.arange(tm)
    perm = np.zeros((tm, tm), np.float32)
    perm[(rows % dil) * (tm // dil) + rows // dil, rows] = 1.0
    return jnp.asarray(perm, BF16)


def _store_dilated(ref, perm_ref, val, dil):
    per = val.shape[0] // dil
    sorted_rows = _dot(perm_ref[...], val).astype(BF16)
    for r in range(dil):
        ref[:, r * D_B:(r + 1) * D_B] = sorted_rows[r * per:(r + 1) * per, :]


def _dilated_specs(s_len, tm, dtype):
    specs = [pl.BlockSpec((tm // d, d * D_B), lambda i: (i, 0)) for d in DILATIONS]
    return specs, [SDS((s_len // d, d * D_B), dtype) for d in DILATIONS]


def _inproj_fwd(x, g_pre, w_in, cos, sin, gq, gk, tm):
    s_len = x.shape[0]
    n_d = len(DILATIONS)

    def body(x_ref, g_ref, w_ref, cos_ref, sin_ref, gq_ref, gk_ref, *rest):
        perms, (qa_raw, ka_raw, qs, kn, va, qb, kb, vb, xn_out), views = rest[:n_d], rest[n_d:n_d + 9], rest[n_d + 9:]
        xn, _ = _rms_fwd(x_ref[...], g_ref[...])
        xn = xn.astype(BF16)
        xn_out[...] = xn
        qa = _dot(xn, w_ref[:, OFF_QA:OFF_KA])
        qa_raw[...] = qa
        cos, sin = cos_ref[...], sin_ref[...]
        qs[...] = (_head_norm_rope(qa, gq_ref[...], jnp.tile(cos, (1, D_A // D_KV_A)), jnp.tile(sin, (1, D_A // D_KV_A)))
                   * SCORE_SCALE).astype(BF16)
        ka = _dot(xn, w_ref[:, OFF_KA:OFF_VA])
        ka_raw[...] = ka
        kn[...] = _head_norm_rope(ka, gk_ref[...], cos, sin).astype(BF16)
        va[...] = _dot(xn, w_ref[:, OFF_VA:OFF_QB]).astype(BF16)
        mixer_b = ((_dot(xn, w_ref[:, OFF_QB:OFF_KB]) * SCORE_SCALE).astype(BF16),
                   _dot(xn, w_ref[:, OFF_KB:OFF_VB]).astype(BF16), _dot(xn, w_ref[:, OFF_VB:D_IN]).astype(BF16))
        for a, (val, plain) in enumerate(zip(mixer_b, (qb, kb, vb))):
            plain[...] = val
            for j, d in enumerate(DILATIONS):
                _store_dilated(views[a * n_d + j], perms[j], val, d)

    view_specs, view_shapes = _dilated_specs(s_len, tm, BF16)
    return pl.pallas_call(
        body, name="inproj_fwd", grid=(s_len // tm,),
        in_specs=[_row_spec(tm, D_MODEL), _full_spec((1, D_MODEL)), _const_spec((D_MODEL, D_IN)),
                  _row_spec(tm, D_KV_A), _row_spec(tm, D_KV_A), _full_spec((1, D_A)), _full_spec((1, D_KV_A))]
                 + [_full_spec((tm, tm))] * n_d,
        out_specs=[_row_spec(tm, D_A), _row_spec(tm, D_KV_A), _row_spec(tm, D_A), _row_spec(tm, D_KV_A),
                   _row_spec(tm, D_KV_A), _row_spec(tm, D_B), _row_spec(tm, D_B), _row_spec(tm, D_B),
                   _row_spec(tm, D_MODEL)] + view_specs * 3,
        out_shape=[SDS((s_len, D_A), F32), SDS((s_len, D_KV_A), F32), SDS((s_len, D_A), BF16),
                   SDS((s_len, D_KV_A), BF16), SDS((s_len, D_KV_A), BF16), SDS((s_len, D_B), BF16),
                   SDS((s_len, D_B), BF16), SDS((s_len, D_B), BF16), SDS((s_len, D_MODEL), BF16)] + view_shapes * 3,
        compiler_params=_params(1, V7X_VMEM_LIMIT),
    )(x, g_pre, w_in, cos, sin, gq, gk, *[_dilation_perm(tm, d) for d in DILATIONS])


def _stack_heads(ref, tq):
    return jnp.concatenate([ref[:, HEAD_DIM * g:HEAD_DIM * (g + 1)] for g in range(GROUP_A)], axis=0)


def _stack_cols(ref, tq):
    return jnp.concatenate([ref[:, HEAD_DIM * g:HEAD_DIM * g + 1] for g in range(GROUP_A)], axis=0)


def _attn_fwd(qs, k, vt, tq, tk):
    s_len = qs.shape[0]
    nk = s_len // tk
    assert nk % 2 == 0
    gw = GROUP_A * HEAD_DIM
    rows = GROUP_A * tq
    vrows = vt.shape[2]

    def body(q_ref, k_ref, vt_ref, o_ref, lse_ref, s_buf):
        qt = _stack_heads(q_ref, tq).T

        def scores(j, slot):
            kj = k_ref[pl.ds(pl.multiple_of(j * tk, tk), tk), :]
            s_buf[slot] = _dot(kj, qt)

        def consume(j, slot, carry):
            m, acc = carry
            st = s_buf[slot]
            m_new = jnp.maximum(m, jnp.max(st, axis=0, keepdims=True))
            pt = jnp.exp(st - m_new)
            acc = jnp.exp(m - m_new) * acc + _dot(vt_ref[j], pt.astype(BF16))
            return m_new, acc

        scores(0, 0)

        def pair(j, carry, more):
            scores(j + 1, 1)
            carry = consume(j, 0, carry)
            if more:
                scores(j + 2, 0)
            return consume(j + 1, 1, carry)

        carry = (jnp.full((1, rows), NEG_BIG, F32), jnp.zeros((vrows, rows), F32))
        carry = lax.fori_loop(0, nk // 2 - 1, lambda jj, c: pair(2 * jj, c, True), carry)
        m, acc = pair(nk - 2, carry, False)
        l = acc[HEAD_DIM:HEAD_DIM + 1]
        o = (acc[:HEAD_DIM] / l).T
        lse = jnp.broadcast_to(m + jnp.log(l), (HEAD_DIM, rows)).T
        for g in range(GROUP_A):
            o_ref[:, HEAD_DIM * g:HEAD_DIM * (g + 1)] = o[g * tq:(g + 1) * tq]
            lse_ref[:, HEAD_DIM * g:HEAD_DIM * (g + 1)] = lse[g * tq:(g + 1) * tq]

    tile = pl.BlockSpec((tq, gw), lambda kv, i: (i, kv))
    return pl.pallas_call(
        body, name="attn_fwd", grid=(N_KV_A, s_len // tq),
        in_specs=[tile, pl.BlockSpec((None, s_len, HEAD_DIM), lambda kv, i: (kv, 0, 0)),
                  pl.BlockSpec((None, nk, vrows, tk), lambda kv, i: (kv, 0, 0, 0))],
        out_specs=[tile, tile],
        out_shape=(SDS((s_len, D_A), F32), SDS((s_len, D_A), F32)),
        scratch_shapes=[pltpu.VMEM((2, tk, rows), F32)],
        compiler_params=_params(2, V7X_VMEM_LIMIT),
    )(qs, k, vt)


def _attn_bwd(qs, do, lse, delta, kt, k, vt, tq, tk):
    s_len = qs.shape[0]
    nk = s_len // tk
    assert nk % 2 == 0
    nq = s_len // tq
    gw = GROUP_A * HEAD_DIM
    rows = GROUP_A * tq

    def body(q_ref, do_ref, lse_ref, delta_ref, kt_ref, k_ref, vt_ref, dq_ref, dk_hbm, dv_hbm,
             dk_acc, dv_acc, s_buf, dp_buf):
        kv = pl.program_id(0)
        i = pl.program_id(1)

        @pl.when(i == 0)
        def _():
            dk_acc[...] = jnp.zeros_like(dk_acc)
            dv_acc[...] = jnp.zeros_like(dv_acc)

        q = _stack_heads(q_ref, tq)
        dout = _stack_heads(do_ref, tq)
        qt = q.T
        doutt = dout.T
        row_lse = _stack_cols(lse_ref, tq)
        row_delta = _stack_cols(delta_ref, tq)

        def scores(j, slot):
            s_buf[slot] = _dot(q, kt_ref[j])
            dp_buf[slot] = _dot(dout, vt_ref[j])

        def consume(j, slot, dq):
            p = jnp.exp(s_buf[slot] - row_lse)
            ds = (p * (dp_buf[slot] - row_delta)).astype(BF16)
            dv_acc[j] += _dot(doutt, p.astype(BF16))
            dk_acc[j] += _dot(qt, ds)
            return dq + _dot(ds, k_ref[pl.ds(pl.multiple_of(j * tk, tk), tk), :])

        scores(0, 0)

        def pair(j, dq, more):
            scores(j + 1, 1)
            dq = consume(j, 0, dq)
            if more:
                scores(j + 2, 0)
            return consume(j + 1, 1, dq)

        dq = lax.fori_loop(0, nk // 2 - 1, lambda jj, c: pair(2 * jj, c, True), jnp.zeros((rows, HEAD_DIM), F32))
        dq = pair(nk - 2, dq, False) * SCORE_SCALE
        for g in range(GROUP_A):
            dq_ref[:, HEAD_DIM * g:HEAD_DIM * (g + 1)] = dq[g * tq:(g + 1) * tq]

        @pl.when(i == nq - 1)
        def _():
            pltpu.sync_copy(dk_acc, dk_hbm.at[kv])
            pltpu.sync_copy(dv_acc, dv_hbm.at[kv])

    tile = pl.BlockSpec((tq, gw), lambda kv, i: (i, kv))
    chunks = pl.BlockSpec((None, nk, HEAD_DIM, tk), lambda kv, i: (kv, 0, 0, 0))
    grad_t = SDS((N_KV_A, nk, HEAD_DIM, tk), F32)
    return pl.pallas_call(
        body, name="attn_bwd", grid=(N_KV_A, nq),
        in_specs=[tile, tile, tile, tile, chunks,
                  pl.BlockSpec((None, s_len, HEAD_DIM), lambda kv, i: (kv, 0, 0)), chunks],
        out_specs=[tile, pl.BlockSpec(memory_space=pl.ANY), pl.BlockSpec(memory_space=pl.ANY)],
        out_shape=(SDS((s_len, D_A), F32), grad_t, grad_t),
        scratch_shapes=[pltpu.VMEM((nk, HEAD_DIM, tk), F32), pltpu.VMEM((nk, HEAD_DIM, tk), F32),
                        pltpu.VMEM((2, rows, tk), F32), pltpu.VMEM((2, rows, tk), F32)],
        compiler_params=_params(2, V7X_VMEM_LIMIT),
    )(qs, do, lse, delta, kt, k, vt)


STAT_W = 128


def _band_specs(length, t, width=D_B):
    hb = t // BAND_HALF
    last = length // BAND_HALF - 1
    main = pl.BlockSpec((t, width), lambda r, i: (i, r))
    prev = pl.BlockSpec((BAND_HALF, width), lambda r, i: (jnp.maximum(i * hb - 1, 0), r))
    nxt = pl.BlockSpec((BAND_HALF, width), lambda r, i: (jnp.minimum((i + 1) * hb, last), r))
    return main, [prev, main, nxt]


def _pack_heads(cols):
    lane = lax.broadcasted_iota(jnp.int32, (cols[0].shape[0], STAT_W), 1)
    out = jnp.zeros((cols[0].shape[0], STAT_W), F32)
    for h, c in enumerate(cols):
        out = jnp.where(lane == h, c, out)
    return out


def _spread_heads(stat):
    head = lax.broadcasted_iota(jnp.int32, (stat.shape[0], D_B), 1) >> 6
    out = jnp.zeros((stat.shape[0], D_B), F32)
    for h in range(N_HEADS_B):
        out = jnp.where(head == h, stat[:, h:h + 1], out)
    return out


def _window(refs):
    return jnp.concatenate([r[...] for r in refs], axis=0)


def _head(v, h):
    return v[:, HEAD_DIM * h:HEAD_DIM * (h + 1)]


def _call_with_rider(body, name, grid, in_specs, out_specs, out_shape, scratch_shapes, args, rider):
    n_in, n_out = len(in_specs), len(out_specs)
    if rider is None:
        res = pl.pallas_call(body, name=name, grid=grid, in_specs=in_specs, out_specs=out_specs, out_shape=out_shape,
                             scratch_shapes=scratch_shapes, compiler_params=_params(len(grid), V7X_VMEM_LIMIT))(*args)
        return res, []
    kind, arrays = rider
    n = len(arrays)

    def with_rider(*refs):
        ins, c_ins = refs[:n_in], refs[n_in:n_in + n]
        outs, c_outs = refs[n_in + n:n_in + n + n_out], refs[n_in + n + n_out:n_in + 2 * n + n_out]
        rest = refs[n_in + 2 * n + n_out:]
        scratch, sems = rest[:-3], rest[-3:]
        ids = [pl.program_id(a) for a in range(len(grid))]
        first = functools.reduce(jnp.logical_and, [i == 0 for i in ids])
        last = functools.reduce(jnp.logical_and, [i == g - 1 for i, g in zip(ids, grid)])

        @pl.when(first)
        def _():
            _comm_start(kind, c_ins, c_outs, sems)

        body(*ins, *outs, *scratch)

        @pl.when(last)
        def _():
            _comm_wait(kind, c_ins, c_outs, sems)

    res = pl.pallas_call(
        with_rider, name=name, grid=grid, in_specs=list(in_specs) + [ANY_SPEC] * n,
        out_specs=list(out_specs) + [ANY_SPEC] * n, out_shape=list(out_shape) + _comm_out_shapes(kind, arrays),
        scratch_shapes=list(scratch_shapes) + _comm_sems(n), compiler_params=_params(len(grid), V7X_VMEM_LIMIT),
    )(*args, *arrays)
    return res[:n_out], res[n_out:]


def _band_fwd(q, k, v, bias, dil, t, rider=None):
    s_len = q.size // D_B
    length = s_len // dil
    w = t + 2 * BAND_HALF
    view = lambda a: a.reshape(length, dil * D_B)
    main, win = _band_specs(length, t)
    stat, _ = _band_specs(length, t, STAT_W)

    def body(q_ref, k0, k1, k2, v0, v1, v2, bias_ref, o_ref, lse_ref, s_buf, kt_buf):
        lses = []
        i = pl.program_id(1)
        kt_buf[...] = _window((k0, k1, k2)).T
        vw = _window((v0, v1, v2))
        pos = i * t - BAND_HALF + lax.broadcasted_iota(jnp.int32, (1, w), 1)
        valid = (pos >= 0) & (pos < length)
        for h in range(N_HEADS_B):
            s_buf[h] = _dot(q_ref[:, HEAD_DIM * h:HEAD_DIM * (h + 1)], kt_buf[HEAD_DIM * h:HEAD_DIM * (h + 1), :])
        for h in range(N_HEADS_B):
            s = jnp.where(valid, s_buf[h] + bias_ref[h], NEG_BIG)
            m = jnp.max(s, axis=-1, keepdims=True)
            e = jnp.exp(s - m)
            den = jnp.sum(e, axis=-1, keepdims=True)
            o_ref[:, HEAD_DIM * h:HEAD_DIM * (h + 1)] = _dot(e.astype(BF16), _head(vw, h)) / den
            lses.append(m + jnp.log(den))
        lse_ref[...] = _pack_heads(lses)

    (o, lse), rode = _call_with_rider(
        body, f"band_fwd_d{dil}", (dil, length // t),
        [main] + win + win + [_full_spec((N_HEADS_B, t, w))], [main, stat],
        [SDS((length, dil * D_B), F32), SDS((length, dil * STAT_W), F32)],
        [pltpu.VMEM((N_HEADS_B, t, w), F32), pltpu.VMEM((D_B, w), BF16)],
        (view(q), view(k), view(k), view(k), view(v), view(v), view(v), bias), rider)
    return o.reshape(s_len, D_B), lse.reshape(s_len, STAT_W), rode


def _band_dq(q, do, lse, delta, k, v, bias, dq_in, dil, t, rider=None):
    s_len = q.size // D_B
    length = s_len // dil
    w = t + 2 * BAND_HALF
    view = lambda a: a.reshape(length, dil * D_B)
    main, win = _band_specs(length, t)

    def body(q_ref, do_ref, lse_ref, delta_ref, k0, k1, k2, v0, v1, v2, bias_ref, acc_ref, dq_ref, dbias_ref,
             s_buf, dp_buf, kt_buf, vt_buf):
        i = pl.program_id(1)
        @pl.when((pl.program_id(0) == 0) & (i == 0))
        def _():
            dbias_ref[...] = jnp.zeros_like(dbias_ref)

        kw = _window((k0, k1, k2))
        kt_buf[...] = kw.T
        vt_buf[...] = _window((v0, v1, v2)).T
        pos = i * t - BAND_HALF + lax.broadcasted_iota(jnp.int32, (1, w), 1)
        valid = (pos >= 0) & (pos < length)
        for h in range(N_HEADS_B):
            cols = slice(HEAD_DIM * h, HEAD_DIM * (h + 1))
            s_buf[h] = _dot(q_ref[:, cols], kt_buf[cols, :])
            dp_buf[h] = _dot(do_ref[:, cols], vt_buf[cols, :])
        for h in range(N_HEADS_B):
            cols = slice(HEAD_DIM * h, HEAD_DIM * (h + 1))
            s = jnp.where(valid, s_buf[h] + bias_ref[h], NEG_BIG)
            p = jnp.exp(s - lse_ref[:, h:h + 1])
            ds = p * (dp_buf[h] - delta_ref[:, h:h + 1])
            dq_ref[:, cols] = acc_ref[:, cols] + _dot(ds.astype(BF16), _head(kw, h)) * SCORE_SCALE
            dbias_ref[h] += ds

    stat, _ = _band_specs(length, t, STAT_W)
    sview = lambda a: a.reshape(length, dil * STAT_W)
    (dq, dbias), rode = _call_with_rider(
        body, f"band_dq_d{dil}", (dil, length // t),
        [main, main, stat, stat] + win + win + [_full_spec((N_HEADS_B, t, w)), main],
        [main, _full_spec((N_HEADS_B, t, w))],
        [SDS((length, dil * D_B), F32), SDS((N_HEADS_B, t, w), F32)],
        [pltpu.VMEM((N_HEADS_B, t, w), F32), pltpu.VMEM((N_HEADS_B, t, w), F32),
         pltpu.VMEM((D_B, w), BF16), pltpu.VMEM((D_B, w), BF16)],
        (view(q), view(do), sview(lse), sview(delta), view(k), view(k), view(k), view(v), view(v), view(v), bias,
         view(dq_in)), rider)
    return dq.reshape(s_len, D_B), dbias, rode


def _band_dkv(k, v, q, do, lse, delta, bias_t, dk_in, dv_in, dil, t):
    s_len = q.size // D_B
    length = s_len // dil
    w = t + 2 * BAND_HALF
    view = lambda a: a.reshape(length, dil * D_B)
    main, win = _band_specs(length, t)

    def body(k_ref, v_ref, q0, q1, q2, d0, d1, d2, l0, l1, l2, e0, e1, e2, bias_ref, dk_acc, dv_acc, dk_ref, dv_ref,
             s_buf, dp_buf, kt_buf, vt_buf):
        i = pl.program_id(1)
        qw = _window((q0, q1, q2))
        dow = _window((d0, d1, d2))
        lsew = _window((l0, l1, l2))
        deltaw = _window((e0, e1, e2))
        pos = i * t - BAND_HALF + lax.broadcasted_iota(jnp.int32, (w, 1), 0)
        valid = (pos >= 0) & (pos < length)
        kt_buf[...] = k_ref[...].T
        vt_buf[...] = v_ref[...].T
        for h in range(N_HEADS_B):
            cols = slice(HEAD_DIM * h, HEAD_DIM * (h + 1))
            s_buf[h] = _dot(_head(qw, h), kt_buf[cols, :])
            dp_buf[h] = _dot(_head(dow, h), vt_buf[cols, :])
        qwt = qw.T
        dowt = dow.T
        dkt, dvt = [], []
        for h in range(N_HEADS_B):
            rows = slice(HEAD_DIM * h, HEAD_DIM * (h + 1))
            s = jnp.where(valid, s_buf[h] + bias_ref[h], NEG_BIG)
            p = jnp.exp(s - lsew[:, h:h + 1])
            ds = p * (dp_buf[h] - deltaw[:, h:h + 1])
            dvt.append(_dot(dowt[rows, :], p.astype(BF16)))
            dkt.append(_dot(qwt[rows, :], ds.astype(BF16)))
        dv_ref[...] = dv_acc[...] + jnp.concatenate(dvt, axis=0).T
        dk_ref[...] = dk_acc[...] + jnp.concatenate(dkt, axis=0).T

    _, swin = _band_specs(length, t, STAT_W)
    sview = lambda a: a.reshape(length, dil * STAT_W)
    dk, dv = pl.pallas_call(
        body, name=f"band_dkv_d{dil}", grid=(dil, length // t),
        in_specs=[main, main] + win + win + swin + swin + [_full_spec((N_HEADS_B, w, t)), main, main],
        out_specs=[main, main],
        out_shape=(SDS((length, dil * D_B), F32), SDS((length, dil * D_B), F32)),
        scratch_shapes=[pltpu.VMEM((N_HEADS_B, w, t), F32), pltpu.VMEM((N_HEADS_B, w, t), F32),
                        pltpu.VMEM((D_B, t), BF16), pltpu.VMEM((D_B, t), BF16)],
        compiler_params=_params(2, V7X_VMEM_LIMIT),
    )(view(k), view(v), view(q), view(q), view(q), view(do), view(do), view(do), sview(lse), sview(lse), sview(lse),
      sview(delta), sview(delta), sview(delta), bias_t, view(dk_in), view(dv_in))
    return dk.reshape(s_len, D_B), dv.reshape(s_len, D_B)


def _t5_bucket_np(rel):
    nb = N_BUCKETS // 2
    max_exact = nb // 2
    side = np.where(rel > 0, nb, 0)
    n = np.abs(rel)
    ratio = np.maximum(n, max_exact).astype(np.float32) / np.float32(max_exact)
    large = max_exact + (np.log(ratio) / np.float32(math.log(MAX_DISTANCE / max_exact))
                         * np.float32(nb - max_exact)).astype(np.int32)
    large = np.minimum(large, nb - 1)
    return (side + np.where(n < max_exact, n, large)).astype(np.int32)


def _band_buckets(dil, t):
    rel = np.arange(t + 2 * BAND_HALF)[None, :] - BAND_HALF - np.arange(t)[:, None]
    bucket = _t5_bucket_np(np.clip(rel, -BAND_HALF, BAND_HALF) * dil)
    return np.where(np.abs(rel) <= BAND_HALF, bucket, -1).astype(np.int32)


def _toeplitz(vals, rows, cols):
    heads = vals.shape[0]
    period = rows + cols
    vec = jnp.concatenate([vals[:, rows - 1:], jnp.zeros((heads, 1), vals.dtype), vals[:, :rows - 1]], axis=1)
    flat = jnp.broadcast_to(vec[:, None, :], (heads, rows, period)).reshape(heads, rows * period)
    return flat[:, :rows * (period - 1)].reshape(heads, rows, period - 1)[:, :, :cols]


def _bias_tiles(rel_bias, dil, t):
    w = t + 2 * BAND_HALF
    rel = np.arange(-BAND_HALF, BAND_HALF + 1)
    bucket = _t5_bucket_np(rel * dil)
    runs, start = [], 0
    for i in range(1, len(bucket) + 1):
        if i == len(bucket) or bucket[i] != bucket[start]:
            b = int(bucket[start])
            runs.append(jnp.broadcast_to(rel_bias[b:b + 1], (i - start, N_HEADS_B)))
            start = i
    per_rel = jnp.concatenate(runs, axis=0).T

    def diagonals(lo, hi):
        left = jnp.full((N_HEADS_B, max(0, -BAND_HALF - lo)), NEG_BIG, F32)
        right = jnp.full((N_HEADS_B, max(0, hi - BAND_HALF)), NEG_BIG, F32)
        return jnp.concatenate([left, per_rel, right], axis=1)

    tile = _toeplitz(diagonals(-(t - 1) - BAND_HALF, w - 1 - BAND_HALF), t, w)
    twin = _toeplitz(diagonals(-(w - 1) + BAND_HALF, t - 1 + BAND_HALF), w, t)
    return tile, twin


def _dbias_reduce(dbias, buckets):
    n = len(dbias)

    def body(*refs):
        db_refs, bk_refs, o_ref = refs[:n], refs[n:2 * n], refs[2 * n]
        row = lax.broadcasted_iota(jnp.int32, (N_BUCKETS, 128), 0)
        lane = lax.broadcasted_iota(jnp.int32, (N_BUCKETS, 128), 1)

        def per_bucket(b, out):
            for pat in range(n):
                msk = bk_refs[pat][...] == b
                for h in range(N_HEADS_B):
                    tot = jnp.sum(jnp.where(msk, db_refs[pat][h], 0.0), axis=-1, keepdims=True)
                    tot = jnp.sum(tot, axis=0, keepdims=True)
                    out = out + jnp.where((row == b) & (lane == h), tot, 0.0)
            return out

        o_ref[...] = lax.fori_loop(0, N_BUCKETS, per_bucket, jnp.zeros((N_BUCKETS, 128), F32))

    return pl.pallas_call(
        body, name="dbias_reduce", out_shape=SDS((N_BUCKETS, 128), F32),
        compiler_params=pltpu.CompilerParams(vmem_limit_bytes=V7X_VMEM_LIMIT),
    )(*dbias, *buckets)


def _attn_out_fwd(ya, ob, lb, x, g_a, g_b, w_out, g_post, tm):
    s_len = ya.shape[0]

    def body(ya_ref, o0, o1, o2, l0, l1, l2, x_ref, ga_ref, gb_ref, w_ref, gp_ref,
             h1_ref, yo_ref, yb_ref, lse_ref, ycat_ref):
        m = jnp.maximum(jnp.maximum(l0[...], l1[...]), l2[...])
        w0, w1, w2 = jnp.exp(l0[...] - m), jnp.exp(l1[...] - m), jnp.exp(l2[...] - m)
        wsum = w0 + w1 + w2
        yb = (_spread_heads(w0 / wsum) * o0[...] + _spread_heads(w1 / wsum) * o1[...]
              + _spread_heads(w2 / wsum) * o2[...])
        yb_ref[...] = yb
        lse_ref[...] = m + jnp.log(wsum)
        yan, _ = _rms_fwd(ya_ref[...], ga_ref[...])
        ybn, _ = _rms_fwd(yb, gb_ref[...])
        yan, ybn = yan.astype(BF16), ybn.astype(BF16)
        ycat_ref[:, :D_A] = yan
        ycat_ref[:, D_A:] = ybn
        yo = _dot(yan, w_ref[:D_A, :]) + _dot(ybn, w_ref[D_A:, :])
        yo_ref[...] = yo
        post, _ = _rms_fwd(yo, gp_ref[...])
        h1_ref[...] = x_ref[...] + post

    half, full, stat = _row_spec(tm, D_A), _row_spec(tm, D_MODEL), _row_spec(tm, STAT_W)
    return pl.pallas_call(
        body, name="attn_out_fwd", grid=(s_len // tm,),
        in_specs=[half] * 4 + [stat] * 3 + [full, _full_spec((1, D_A)), _full_spec((1, D_B)),
                                            _full_spec((D_MODEL, D_MODEL)), _full_spec((1, D_MODEL))],
        out_specs=[full, full, half, stat, full],
        out_shape=(SDS((s_len, D_MODEL), F32), SDS((s_len, D_MODEL), F32), SDS((s_len, D_B), F32),
                   SDS((s_len, STAT_W), F32), SDS((s_len, D_MODEL), BF16)),
        compiler_params=_params(1, V7X_VMEM_LIMIT),
    )(ya, ob[0], ob[1], ob[2], lb[0], lb[1], lb[2], x, g_a, g_b, w_out, g_post)


def _attn_out_bwd(dh1, yo, ya, yb, g_a, g_b, w_out, g_post, tm):
    s_len = ya.shape[0]
    n_d = len(DILATIONS)

    def body(dh1_ref, yo_ref, ya_ref, yb_ref, ga_ref, gb_ref, w_ref, gp_ref, *rest):
        perms, views = rest[:n_d], rest[n_d + 8:]
        dyo_ref, dya_ref, dyb_ref, dela_ref, delb_ref, dgp_ref, dga_ref, dgb_ref = rest[n_d:n_d + 8]
        _zero_at_first(pl.program_id(0) == 0, dgp_ref, dga_ref, dgb_ref)
        dyo, dgp = _rms_bwd(yo_ref[...], gp_ref[...], dh1_ref[...])
        dyo = dyo.astype(BF16)
        dyo_ref[...] = dyo
        dgp_ref[...] += dgp
        dya_n = _dot_nt(dyo, w_ref[:D_A, :])
        dyb_n = _dot_nt(dyo, w_ref[D_A:, :])
        ya, yb = ya_ref[...], yb_ref[...]
        dya, dga = _rms_bwd(ya, ga_ref[...], dya_n)
        dyb, dgb = _rms_bwd(yb, gb_ref[...], dyb_n)
        dga_ref[...] += dga
        dgb_ref[...] += dgb
        dya_ref[...] = dya.astype(BF16)
        dyb_ref[...] = dyb.astype(BF16)
        for j, d in enumerate(DILATIONS):
            _store_dilated(views[j], perms[j], dyb.astype(BF16), d)
        dela_ref[...] = _head_sum(dya * ya)
        prod = dyb * yb
        head = lax.broadcasted_iota(jnp.int32, prod.shape, 1) >> 6
        delb_ref[...] = _pack_heads([jnp.sum(jnp.where(head == h, prod, 0.0), axis=-1, keepdims=True)
                                     for h in range(N_HEADS_B)])

    half, full = _row_spec(tm, D_A), _row_spec(tm, D_MODEL)
    view_specs, view_shapes = _dilated_specs(s_len, tm, BF16)
    return pl.pallas_call(
        body, name="attn_out_bwd", grid=(s_len // tm,),
        in_specs=[full, full, half, half, _full_spec((1, D_A)), _full_spec((1, D_B)),
                  _full_spec((D_MODEL, D_MODEL)), _full_spec((1, D_MODEL))] + [_full_spec((tm, tm))] * n_d,
        out_specs=[full, half, half, half, _row_spec(tm, STAT_W), _full_spec((1, D_MODEL)), _full_spec((1, D_A)),
                   _full_spec((1, D_B))] + view_specs,
        out_shape=[SDS((s_len, D_MODEL), BF16), SDS((s_len, D_A), BF16), SDS((s_len, D_B), BF16),
                   SDS((s_len, D_A), F32), SDS((s_len, STAT_W), F32), SDS((1, D_MODEL), F32), SDS((1, D_A), F32),
                   SDS((1, D_B), F32)] + view_shapes,
        compiler_params=_params(1, V7X_VMEM_LIMIT),
    )(dh1, yo, ya, yb, g_a, g_b, w_out, g_post, *[_dilation_perm(tm, d) for d in DILATIONS])


FF_CHUNK = 1024


def _mlp_fwd(h1, g_pre, w1, w2, g_post, tm):
    s_len = h1.shape[0]

    def body(h1_ref, gpre_ref, w1_ref, w2_ref, gpost_ref, h2_ref, fo_ref, xn_ref):
        h1v = h1_ref[...]
        xn, _ = _rms_fwd(h1v, gpre_ref[...])
        xn = xn.astype(BF16)
        xn_ref[...] = xn
        fo = jnp.zeros((tm, D_MODEL), F32)
        for c in range(D_FF // FF_CHUNK):
            cols = slice(c * FF_CHUNK, (c + 1) * FF_CHUNK)
            u = jnp.maximum(_dot(xn, w1_ref[:, cols]), 0.0)
            fo = fo + _dot((u * u).astype(BF16), w2_ref[cols, :])
        fo_ref[...] = fo
        post, _ = _rms_fwd(fo, gpost_ref[...])
        h2_ref[...] = h1v + post

    full = _row_spec(tm, D_MODEL)
    return pl.pallas_call(
        body, name="mlp_fwd", grid=(s_len // tm,),
        in_specs=[full, _full_spec((1, D_MODEL)), _const_spec((D_MODEL, D_FF)), _const_spec((D_FF, D_MODEL)),
                  _full_spec((1, D_MODEL))],
        out_specs=[full, full, full],
        out_shape=(SDS((s_len, D_MODEL), F32), SDS((s_len, D_MODEL), F32), SDS((s_len, D_MODEL), BF16)),
        compiler_params=_params(1, V7X_VMEM_LIMIT),
    )(h1, g_pre, w1, w2, g_post)


def _mlp_bwd(h1, dh2, fo, xn, g_pre, w1, w2, g_post, tm):
    s_len = h1.shape[0]

    def body(h1_ref, dh2_ref, fo_ref, xn_ref, gpre_ref, w1_ref, w2_ref, gpost_ref,
             dh1_ref, dfo_ref, du_ref, f_ref, dgpost_ref, dgpre_ref):
        _zero_at_first(pl.program_id(0) == 0, dgpost_ref, dgpre_ref)
        dh2 = dh2_ref[...]
        dfo, dgpost = _rms_bwd(fo_ref[...], gpost_ref[...], dh2)
        dfo = dfo.astype(BF16)
        dfo_ref[...] = dfo
        dgpost_ref[...] += dgpost
        xn = xn_ref[...]
        dxn = jnp.zeros((tm, D_MODEL), F32)
        for c in range(D_FF // FF_CHUNK):
            cols = slice(c * FF_CHUNK, (c + 1) * FF_CHUNK)
            u = jnp.maximum(_dot(xn, w1_ref[:, cols]), 0.0)
            f_ref[:, cols] = (u * u).astype(BF16)
            du = (_dot_nt(dfo, w2_ref[cols, :]) * (2.0 * u)).astype(BF16)
            du_ref[:, cols] = du
            dxn = dxn + _dot_nt(du, w1_ref[:, cols])
        dx, dgpre = _rms_bwd(h1_ref[...], gpre_ref[...], dxn)
        dgpre_ref[...] += dgpre
        dh1_ref[...] = dh2 + dx

    full, wide = _row_spec(tm, D_MODEL), _row_spec(tm, D_FF)
    return pl.pallas_call(
        body, name="mlp_bwd", grid=(s_len // tm,),
        in_specs=[full, full, full, full, _full_spec((1, D_MODEL)), _const_spec((D_MODEL, D_FF)),
                  _const_spec((D_FF, D_MODEL)), _full_spec((1, D_MODEL))],
        out_specs=[full, full, wide, wide, _full_spec((1, D_MODEL)), _full_spec((1, D_MODEL))],
        out_shape=(SDS((s_len, D_MODEL), F32), SDS((s_len, D_MODEL), BF16), SDS((s_len, D_FF), BF16),
                   SDS((s_len, D_FF), BF16), SDS((1, D_MODEL), F32), SDS((1, D_MODEL), F32)),
        compiler_params=_params(1, V7X_VMEM_LIMIT),
    )(h1, dh2, fo, xn, g_pre, w1, w2, g_post)


def _ple_fwd_bwd(h2, p, target, g_ple, w_gate, w_proj, tm):
    s_len = h2.shape[0]

    def body(h2_ref, p_ref, t_ref, g_ref, wg_ref, wp_ref, dh2_ref, loss_ref, dg_ref, xn_ref, dgl_ref, dpp_ref, pb_ref):
        _zero_at_first(pl.program_id(0) == 0, loss_ref, dg_ref)
        h2 = h2_ref[...]
        g = g_ref[...]
        xn, _ = _rms_fwd(h2, g)
        xn = xn.astype(BF16)
        xn_ref[...] = xn
        gate = 1.0 / (1.0 + jnp.exp(-_dot(xn, wg_ref[...])))
        pb = p_ref[...].astype(BF16)
        pb_ref[...] = pb
        pp = _dot(pb, wp_ref[...])
        diff = h2 + gate * pp - t_ref[...]
        loss_ref[...] += jnp.full((8, 128), jnp.sum(diff * diff), F32)
        dh3 = diff * (1.0 / D_MODEL)
        dpp_ref[...] = (dh3 * gate).astype(BF16)
        dgl = (dh3 * pp * gate * (1.0 - gate)).astype(BF16)
        dgl_ref[...] = dgl
        dx, dg = _rms_bwd(h2, g, _dot_nt(dgl, wg_ref[...]))
        dg_ref[...] += dg
        dh2_ref[...] = dh3 + dx

    full = _row_spec(tm, D_MODEL)
    return pl.pallas_call(
        body, name="ple_fwd_bwd", grid=(s_len // tm,),
        in_specs=[full, _row_spec(tm, D_PLE), full, _full_spec((1, D_MODEL)), _full_spec((D_MODEL, D_MODEL)),
                  _full_spec((D_PLE, D_MODEL))],
        out_specs=[full, _full_spec((8, 128)), _full_spec((1, D_MODEL)), full, full, full, _row_spec(tm, D_PLE)],
        out_shape=(SDS((s_len, D_MODEL), F32), SDS((8, 128), F32), SDS((1, D_MODEL), F32), SDS((s_len, D_MODEL), BF16),
                   SDS((s_len, D_MODEL), BF16), SDS((s_len, D_MODEL), BF16), SDS((s_len, D_PLE), BF16)),
        compiler_params=_params(1, V7X_VMEM_LIMIT),
    )(h2, p, target, g_ple, w_gate, w_proj)


def _inproj_bwd(dqs, dk_t, dv_t, dqb, dkb, dvb, qa_raw, ka_raw, x, dh1, g_pre, w_in, cos, sin, gq, gk, tm):
    s_len = x.shape[0]
    tk = dk_t.shape[3]

    def body(dqs_ref, dkt_ref, dvt_ref, dqb_ref, dkb_ref, dvb_ref, qa_ref, ka_ref, x_ref, dh1_ref, g_ref, w_ref,
             cos_ref, sin_ref, gq_ref, gk_ref, dx_ref, dproj_ref, dg_ref, dgq_ref, dgk_ref):
        _zero_at_first(pl.program_id(0) == 0, dg_ref, dgq_ref, dgk_ref)
        cos, sin = cos_ref[...], sin_ref[...]
        dkn = dkt_ref[...].reshape(D_KV_A, tm).T
        dva = dvt_ref[...].reshape(D_KV_A, tm).T
        dqa, dgq = _head_norm_rope_bwd(qa_ref[...], gq_ref[...], jnp.tile(cos, (1, D_A // D_KV_A)),
                                       jnp.tile(sin, (1, D_A // D_KV_A)), dqs_ref[...])
        dka, dgk = _head_norm_rope_bwd(ka_ref[...], gk_ref[...], cos, sin, dkn)
        dgq_ref[...] += dgq
        dgk_ref[...] += dgk
        dproj_ref[:, OFF_QA:OFF_KA] = dqa.astype(BF16)
        dproj_ref[:, OFF_KA:OFF_VA] = dka.astype(BF16)
        dproj_ref[:, OFF_VA:OFF_QB] = dva.astype(BF16)
        dproj_ref[:, OFF_QB:OFF_KB] = dqb_ref[...].astype(BF16)
        dproj_ref[:, OFF_KB:OFF_VB] = dkb_ref[...].astype(BF16)
        dproj_ref[:, OFF_VB:D_IN] = dvb_ref[...].astype(BF16)
        dxn = _dot_nt(dproj_ref[...], w_ref[...])
        dx, dg = _rms_bwd(x_ref[...], g_ref[...], dxn)
        dg_ref[...] += dg
        dx_ref[...] = dh1_ref[...] + dx

    half, kvw, full = _row_spec(tm, D_A), _row_spec(tm, D_KV_A), _row_spec(tm, D_MODEL)
    per_chunk = tk // tm
    chunk_t = pl.BlockSpec((N_KV_A, None, HEAD_DIM, tm), lambda i: (0, i // per_chunk, 0, i % per_chunk))
    return pl.pallas_call(
        body, name="inproj_bwd", grid=(s_len // tm,),
        in_specs=[half, chunk_t, chunk_t, half, half, half, half, kvw, full, full, _full_spec((1, D_MODEL)),
                  _const_spec((D_MODEL, D_IN)), kvw, kvw, _full_spec((1, D_A)), _full_spec((1, D_KV_A))],
        out_specs=[full, _row_spec(tm, D_IN), _full_spec((1, D_MODEL)), _full_spec((1, D_A)), _full_spec((1, D_KV_A))],
        out_shape=(SDS((s_len, D_MODEL), F32), SDS((s_len, D_IN), BF16), SDS((1, D_MODEL), F32), SDS((1, D_A), F32),
                   SDS((1, D_KV_A), F32)),
        compiler_params=_params(1, V7X_VMEM_LIMIT),
    )(dqs, dk_t, dv_t, dqb, dkb, dvb, qa_raw, ka_raw, x, dh1, g_pre, w_in, cos, sin, gq, gk)


def _weight_grad(a, b, name, tk1, tn, tm):
    s_len, k1 = a.shape
    n = b.shape[1]
    steps = s_len // tm

    def body(a_ref, b_ref, o_ref, acc):
        r = pl.program_id(2)
        _zero_at_first(r == 0, acc)
        acc[...] += _dot_tn(a_ref[...], b_ref[...])

        @pl.when(r == steps - 1)
        def _():
            o_ref[...] = acc[...].astype(BF16)

    return pl.pallas_call(
        body, name=name, grid=(k1 // tk1, n // tn, steps),
        in_specs=[pl.BlockSpec((tm, tk1), lambda i, j, r: (r, i)), pl.BlockSpec((tm, tn), lambda i, j, r: (r, j))],
        out_specs=pl.BlockSpec((tk1, tn), lambda i, j, r: (i, j)),
        out_shape=SDS((k1, n), BF16),
        scratch_shapes=[pltpu.VMEM((tk1, tn), F32)],
        compiler_params=_params(3, V7X_VMEM_LIMIT),
    )(a, b)


def _my_index():
    return 4 * lax.axis_index("x") + 2 * lax.axis_index("y") + lax.axis_index("c")


def _peer(k):
    f = k + 1
    x, y, c = lax.axis_index("x"), lax.axis_index("y"), lax.axis_index("c")
    return (x ^ ((f >> 2) & 1), y ^ ((f >> 1) & 1), c ^ (f & 1))


GATHER, EXCHANGE = "gather", "exchange"
ANY_SPEC = pl.BlockSpec(memory_space=pl.ANY)


def _comm_out_shapes(kind, arrays):
    return [SDS((N_DEV,) + a.shape if kind == GATHER else a.shape, a.dtype) for a in arrays]


def _comm_sems(n):
    return [pltpu.SemaphoreType.DMA((n, N_DEV - 1)), pltpu.SemaphoreType.DMA((n, N_DEV - 1)), pltpu.SemaphoreType.DMA((n,))]


def _comm_copies(kind, ins, outs, send_sems, recv_sems, local_sems):
    me = _my_index()
    local, remote = [], []
    for a, (src, dst) in enumerate(zip(ins, outs)):
        local.append(pltpu.make_async_copy(src if kind == GATHER else src.at[me], dst.at[me], local_sems.at[a]))
        for k in range(N_DEV - 1):
            px, py, pc = _peer(k)
            remote.append(pltpu.make_async_remote_copy(
                src_ref=src if kind == GATHER else src.at[4 * px + 2 * py + pc], dst_ref=dst.at[me],
                send_sem=send_sems.at[a, k], recv_sem=recv_sems.at[a, k],
                device_id=(px, py, pc), device_id_type=pl.DeviceIdType.MESH))
    return local, remote


def _comm_start(kind, ins, outs, sems):
    local, remote = _comm_copies(kind, ins, outs, *sems)
    for cp in local + remote:
        cp.start()


def _comm_wait(kind, ins, outs, sems):
    local, remote = _comm_copies(kind, ins, outs, *sems)
    for cp in local:
        cp.wait()
    for cp in remote:
        cp.wait_send()
    for cp in remote:
        cp.wait_recv()


def _collective(kind, arrays, name):
    n = len(arrays)

    def body(*refs):
        ins, outs, sems = refs[:n], refs[n:2 * n], refs[2 * n:]
        _comm_start(kind, ins, outs, sems)
        _comm_wait(kind, ins, outs, sems)

    return pl.pallas_call(
        body, name=name, in_specs=[ANY_SPEC] * n, out_specs=[ANY_SPEC] * n,
        out_shape=_comm_out_shapes(kind, arrays), scratch_shapes=_comm_sems(n),
    )(*arrays)


def _sum_adamw(parts, w, m, v, name, tr):
    rows, cols = w.shape
    c1 = 1.0 / (1.0 - ADAM_B1 ** ADAM_STEP)
    c2 = 1.0 / (1.0 - ADAM_B2 ** ADAM_STEP)

    def body(p_ref, w_ref, m_ref, v_ref, g_ref, d_ref, nm_ref, nv_ref):
        g = p_ref[0].astype(F32)
        for j in range(1, N_DEV):
            g = g + p_ref[j].astype(F32)
        g_ref[...] = g
        nm = ADAM_B1 * m_ref[...] + (1.0 - ADAM_B1) * g
        nv = ADAM_B2 * v_ref[...] + (1.0 - ADAM_B2) * (g * g)
        nm_ref[...] = nm
        nv_ref[...] = nv
        d_ref[...] = -ADAM_LR * ((nm * c1) / (jnp.sqrt(nv * c2) + ADAM_EPS) + ADAM_WD * w_ref[...])

    blk = pl.BlockSpec((tr, cols), lambda i: (i, 0))
    return pl.pallas_call(
        body, name=name, grid=(rows // tr,),
        in_specs=[pl.BlockSpec((N_DEV, tr, cols), lambda i: (0, i, 0)), blk, blk, blk],
        out_specs=[blk] * 4, out_shape=[SDS((rows, cols), F32)] * 4,
        compiler_params=_params(1, V7X_VMEM_LIMIT),
    )(parts, w, m, v)


_SMALL = (("g_attn_pre", 1024), ("g_q", 64), ("g_k", 64), ("g_out_a", 512), ("g_out_b", 512), ("g_attn_post", 1024),
          ("rel_bias", 256), ("g_mlp_pre", 1024), ("g_mlp_post", 1024), ("g_ple", 1024))
_SLAB_ROWS = 56


def _pack_small(vals):
    rows = []
    for (name, size) in _SMALL:
        flat = vals[name].reshape(-1).astype(F32)
        padded = -(-size // 128) * 128
        rows.append(jnp.pad(flat, (0, padded - size)).reshape(padded // 128, 128))
    slab = jnp.concatenate(rows, axis=0)
    return jnp.pad(slab, ((0, _SLAB_ROWS - slab.shape[0]), (0, 0)))


def _unpack_small(slab, shapes):
    out, row = {}, 0
    for (name, size) in _SMALL:
        nrow = -(-size // 128)
        out[name] = slab[row:row + nrow].reshape(-1)[:size].reshape(shapes[name])
        row += nrow
    return out


def _rope_tables(s_len):
    rows = s_len // GRID_W
    row = jnp.broadcast_to(jnp.arange(rows)[:, None], (rows, GRID_W)).reshape(-1).astype(F32)
    col = jnp.broadcast_to(jnp.arange(GRID_W)[None, :], (rows, GRID_W)).reshape(-1).astype(F32)
    n_axis = ROPE_HALF // 2
    inv_freq = ROPE_THETA ** (-jnp.arange(n_axis, dtype=F32) / n_axis)
    ang = jnp.concatenate([row[:, None] * inv_freq, col[:, None] * inv_freq], axis=-1)
    cos, sin = jnp.cos(ang), jnp.sin(ang)
    cos = jnp.tile(jnp.concatenate([cos, cos], axis=-1), (1, N_KV_A))
    sin = jnp.tile(jnp.concatenate([-sin, sin], axis=-1), (1, N_KV_A))
    return cos, sin


_RIDERS = {1: ("w_out", "w_ff1"), 4: ("w_ff2",), 16: ("w_ple_gate", "w_ple_proj")}
_GRAD_RIDERS = {1: ("w_ff1",), 4: ("w_ff2",), 16: ("w_ple_gate", "w_ple_proj", "w_out")}


def _local_step(x, p, target, w_in, shards, small):
    axis_of = dict(_BIG)
    s_len = x.shape[0]
    tm = min(256, s_len)
    tq = min(128, s_len)
    tk = min(2048, s_len // 2)
    cos, sin = _rope_tables(s_len)
    gq = jnp.tile(small["g_q"], (1, N_HEADS_A))
    gk = jnp.tile(small["g_k"], (1, N_KV_A))

    qa_raw, ka_raw, qs, kn, va, qb, kb, vb, xn1, *views = _inproj_fwd(x, small["g_attn_pre"], w_in, cos, sin, gq, gk, tm)
    n_d = len(DILATIONS)
    qb, kb, vb = ({1: plain, **dict(zip(DILATIONS, views[a * n_d:(a + 1) * n_d]))} for a, plain in enumerate((qb, kb, vb)))

    def kv_major(a):
        return a.reshape(s_len, N_KV_A, HEAD_DIM).transpose(1, 0, 2)

    def kv_chunks_t(a):
        return a.reshape(s_len // tk, tk, N_KV_A, HEAD_DIM).transpose(2, 0, 3, 1)

    k_maj, kt, vt = kv_major(kn), kv_chunks_t(kn), kv_chunks_t(va)
    vt_ones = jnp.concatenate([vt, jnp.ones((N_KV_A, s_len // tk, 16, tk), BF16)], axis=2)
    ya, lse_a = _attn_fwd(qs, k_maj, vt_ones, tq, tk)

    ob, lb, tiles, full = [], [], [], {}
    for (_, dil) in DILATED_PATTERNS:
        t = min(256, s_len // dil)
        bias, bias_t = _bias_tiles(small["rel_bias"], dil, t)
        tiles.append((t, bias, bias_t, _band_buckets(dil, t)))
        o, l, gathered = _band_fwd(qb[dil], kb[dil], vb[dil], bias, dil, t, (GATHER, [shards[n] for n in _RIDERS[dil]]))
        ob.append(o)
        lb.append(l)
        full.update({n: _assemble(g, axis_of[n]) for n, g in zip(_RIDERS[dil], gathered)})
    w_out, w_ff1, w_ff2, w_gate, w_proj = (full[n] for n in ("w_out", "w_ff1", "w_ff2", "w_ple_gate", "w_ple_proj"))

    h1, yo, yb, lse_b, ycat = _attn_out_fwd(ya, ob, lb, x, small["g_out_a"], small["g_out_b"], w_out,
                                            small["g_attn_post"], tm)
    h2, fo, xn2 = _mlp_fwd(h1, small["g_mlp_pre"], w_ff1, w_ff2, small["g_mlp_post"], tm)
    dh2, loss_part, dg_ple, xn3, dgl, dpp, pb = _ple_fwd_bwd(h2, p, target, small["g_ple"], w_gate, w_proj, tm)
    dh1, dfo, du, f, dg_mlp_post, dg_mlp_pre = _mlp_bwd(h1, dh2, fo, xn2, small["g_mlp_pre"], w_ff1, w_ff2,
                                                          small["g_mlp_post"], tm)
    dyo, dya, dyb, delta_a, delta_b, dg_attn_post, dg_out_a, dg_out_b, *dyb_views = _attn_out_bwd(
        dh1, yo, ya, yb, small["g_out_a"], small["g_out_b"], w_out, small["g_attn_post"], tm)
    dyb = {1: dyb, **dict(zip(DILATIONS, dyb_views))}

    dqs, dk_t, dv_t = _attn_bwd(qs, dya, lse_a, delta_a, kt, k_maj, vt, tq, tk)

    tg = min(1024, s_len)
    grads = {
        "w_out": _weight_grad(ycat, dyo, "grad_w_out", D_MODEL, D_MODEL, tg),
        "w_ff1": _weight_grad(xn2, du, "grad_w_ff1", D_MODEL, 1024, tg),
        "w_ff2": _weight_grad(f, dfo, "grad_w_ff2", 1024, D_MODEL, tg),
        "w_ple_gate": _weight_grad(xn3, dgl, "grad_w_ple_gate", D_MODEL, D_MODEL, tg),
        "w_ple_proj": _weight_grad(pb, dpp, "grad_w_ple_proj", D_PLE, D_MODEL, tg),
    }

    dqb = dkb = dvb = jnp.zeros((s_len, D_B), F32)
    dbias, received = [], {}
    for (_, dil), (t, bias, bias_t, _) in reversed(list(zip(DILATED_PATTERNS, tiles))):
        rider = (EXCHANGE, [_cut(grads[n], axis_of[n]) for n in _GRAD_RIDERS[dil]])
        dqb, db, got = _band_dq(qb[dil], dyb[dil], lse_b, delta_b, kb[dil], vb[dil], bias, dqb, dil, t, rider)
        dkb, dvb = _band_dkv(kb[dil], vb[dil], qb[dil], dyb[dil], lse_b, delta_b, bias_t, dkb, dvb, dil, t)
        dbias.insert(0, db)
        received.update(zip(_GRAD_RIDERS[dil], got))
    d_rel = _dbias_reduce(dbias, [jnp.asarray(tl[3]) for tl in tiles])[:, :N_HEADS_B]

    dx, dproj, dg_attn_pre, dgq_lanes, dgk_lanes = _inproj_bwd(
        dqs, dk_t, dv_t, dqb, dkb, dvb, qa_raw, ka_raw, x, dh1, small["g_attn_pre"], w_in, cos, sin, gq, gk, tm)
    grad_w_in = _weight_grad(xn1, dproj, "grad_w_in", D_MODEL, 768, tg)
    small_grads = {
        "g_attn_pre": dg_attn_pre, "g_q": dgq_lanes.reshape(N_HEADS_A, HEAD_DIM).sum(0, keepdims=True),
        "g_k": dgk_lanes.reshape(N_KV_A, HEAD_DIM).sum(0, keepdims=True), "g_out_a": dg_out_a, "g_out_b": dg_out_b,
        "g_attn_post": dg_attn_post, "rel_bias": d_rel, "g_mlp_pre": dg_mlp_pre, "g_mlp_post": dg_mlp_post,
        "g_ple": dg_ple,
    }
    return loss_part[0, 0], dx, grad_w_in, received, small_grads


_BIG = (("w_in", 1), ("w_out", 0), ("w_ff1", 1), ("w_ff2", 0), ("w_ple_gate", 0), ("w_ple_proj", 1))


def _assemble(gathered, axis):
    if axis == 0:
        return gathered.reshape(-1, gathered.shape[2])
    return gathered.transpose(1, 0, 2).reshape(gathered.shape[1], -1)


def _cut(full, axis):
    if axis == 0:
        return full.reshape(N_DEV, full.shape[0] // N_DEV, full.shape[1])
    return full.reshape(full.shape[0], N_DEV, full.shape[1] // N_DEV).transpose(1, 0, 2)


def kernel(x, p, w_in, g_attn_pre, g_q, g_k, g_out_a, g_out_b, w_out, g_attn_post, rel_bias, g_mlp_pre, w_ff1, w_ff2, g_mlp_post, g_ple, w_ple_gate, w_ple_proj, loss_target, m_w_in, m_g_attn_pre, m_g_q, m_g_k, m_g_out_a, m_g_out_b, m_w_out, m_g_attn_post, m_rel_bias, m_g_mlp_pre, m_w_ff1, m_w_ff2, m_g_mlp_post, m_g_ple, m_w_ple_gate, m_w_ple_proj, v_w_in, v_g_attn_pre, v_g_q, v_g_k, v_g_out_a, v_g_out_b, v_w_out, v_g_attn_post, v_rel_bias, v_g_mlp_pre, v_w_ff1, v_w_ff2, v_g_mlp_post, v_g_ple, v_w_ple_gate, v_w_ple_proj):
    given = dict(locals())
    small_names = [n for n, _ in _SMALL]
    small = {n: given[n] for n in small_names}
    shards = {n: given[n][0] for n, _ in _BIG}

    bf16_shards = {n: shards[n].astype(BF16) for n, _ in _BIG}
    (gathered_w_in,) = _collective(GATHER, [bf16_shards["w_in"]], "gather_w_in")

    loss_part, dx, grad_w_in, received, small_grads = _local_step(
        x[0], p[0, 0], loss_target[0], _assemble(gathered_w_in, 1), bf16_shards, small)

    loss_at = np.zeros((_SLAB_ROWS, 128), bool)
    loss_at[-1, 0] = True
    slab = jnp.where(loss_at, loss_part * (0.5 / D_MODEL), _pack_small(small_grads))
    slab_parts = jnp.broadcast_to(slab[None], (N_DEV,) + slab.shape)
    received["w_in"], slabs = _collective(EXCHANGE, [_cut(grad_w_in, 1), slab_parts], "exchange_w_in_grads")

    out_g, out_d, out_m, out_v = {}, {}, {}, {}
    for n, _ in _BIG:
        rows = shards[n].shape[0]
        g, d, nm, nv = _sum_adamw(received[n], shards[n], given["m_" + n][0], given["v_" + n][0], "adamw_" + n,
                                  min(rows, 128))
        out_g[n], out_d[n], out_m[n], out_v[n] = g[None], d[None], nm[None], nv[None]
    g, d, nm, nv = _sum_adamw(slabs, _pack_small(small), _pack_small({n: given["m_" + n] for n in small_names}),
                              _pack_small({n: given["v_" + n] for n in small_names}), "adamw_small", _SLAB_ROWS)
    shapes = {n: given[n].shape for n in small_names}
    for dst, slab_out in ((out_g, g), (out_d, d), (out_m, nm), (out_v, nv)):
        dst.update(_unpack_small(slab_out, shapes))

    loss = g[-1, 0]
    order = ["w_in", "g_attn_pre", "g_q", "g_k", "g_out_a", "g_out_b", "w_out", "g_attn_post", "rel_bias", "g_mlp_pre",
             "w_ff1", "w_ff2", "g_mlp_post", "g_ple", "w_ple_gate", "w_ple_proj"]
    return (loss, dx[None], *[out_g[n] for n in order], *[out_d[n] for n in order], *[out_m[n] for n in order],
            *[out_v[n] for n in order])
```

```python
import functools
import math

import jax
import jax.numpy as jnp
import numpy as np
from jax import lax
from jax.experimental import pallas as pl
from jax.experimental.pallas import tpu as pltpu

F32 = jnp.float32
BF16 = jnp.bfloat16
SDS = jax.ShapeDtypeStruct

D_MODEL = 1024
HEAD_DIM = 64
N_HEADS_A = 8
N_KV_A = 2
GROUP_A = N_HEADS_A // N_KV_A
N_HEADS_B = 8
D_A = N_HEADS_A * HEAD_DIM
D_KV_A = N_KV_A * HEAD_DIM
D_B = N_HEADS_B * HEAD_DIM
D_IN = D_A + 2 * D_KV_A + 3 * D_B
D_FF = 4 * D_MODEL
D_PLE = 256
GRID_W = 64
ROPE_THETA = 10000.0
ROPE_HALF = HEAD_DIM // 2
DILATED_PATTERNS = ((128, 1), (512, 4), (2048, 16))
BAND_HALF = 64
N_BUCKETS = 32
MAX_DISTANCE = 1024
EPS = 1e-6
NEG_BIG = -1e30
SCORE_SCALE = HEAD_DIM ** -0.5

ADAM_LR = 0.001
ADAM_B1 = 0.9
ADAM_B2 = 0.999
ADAM_EPS = 1e-08
ADAM_WD = 0.01
ADAM_STEP = 10

N_DEV = 8
V7X_VMEM_LIMIT = 56 * 1024 * 1024

OFF_QA, OFF_KA, OFF_VA, OFF_QB, OFF_KB, OFF_VB = 0, 512, 640, 768, 1280, 1792


def _params(n_axes, vmem=None):
    return pltpu.CompilerParams(dimension_semantics=("arbitrary",) * n_axes, vmem_limit_bytes=vmem)


def _dot(a, b):
    return jnp.dot(a, b, preferred_element_type=F32)


def _dot_nt(a, b):
    return lax.dot_general(a, b, (((1,), (1,)), ((), ())), preferred_element_type=F32)


def _dot_tn(a, b):
    return lax.dot_general(a, b, (((0,), (0,)), ((), ())), preferred_element_type=F32)


def _rms_fwd(x, g):
    r = lax.rsqrt(jnp.mean(x * x, axis=-1, keepdims=True) + EPS)
    return x * r * g, r


def _rms_bwd(x, g, dy):
    r = lax.rsqrt(jnp.mean(x * x, axis=-1, keepdims=True) + EPS)
    xh = x * r
    dxh = dy * g
    dx = r * (dxh - xh * jnp.mean(dxh * xh, axis=-1, keepdims=True))
    return dx, jnp.sum(dy * xh, axis=0, keepdims=True)


def _head_sum(v):
    head = lax.broadcasted_iota(jnp.int32, v.shape, 1) >> 6
    out = jnp.zeros_like(v)
    for h in range(v.shape[1] // HEAD_DIM):
        msk = head == h
        s = jnp.sum(jnp.where(msk, v, 0.0), axis=-1, keepdims=True)
        out = jnp.where(msk, s, out)
    return out


def _swap_halves(v):
    w = v.shape[1]
    lane = lax.broadcasted_iota(jnp.int32, v.shape, 1)
    first_half = (lane & (HEAD_DIM - 1)) < ROPE_HALF
    return jnp.where(first_half, pltpu.roll(v, w - ROPE_HALF, 1), pltpu.roll(v, ROPE_HALF, 1))


def _head_norm_rope(v, g, cos, sin_signed):
    r = lax.rsqrt(_head_sum(v * v) * (1.0 / HEAD_DIM) + EPS)
    y = v * r * g
    return y * cos + _swap_halves(y) * sin_signed


def _head_norm_rope_bwd(v, g, cos, sin_signed, dout):
    dy = dout * cos - _swap_halves(dout) * sin_signed
    r = lax.rsqrt(_head_sum(v * v) * (1.0 / HEAD_DIM) + EPS)
    xh = v * r
    dxh = dy * g
    dv = r * (dxh - xh * (_head_sum(dxh * xh) * (1.0 / HEAD_DIM)))
    return dv, jnp.sum(dy * xh, axis=0, keepdims=True)


def _row_spec(tm, n):
    return pl.BlockSpec((tm, n), lambda i: (i, 0))


def _full_spec(shape):
    nd = len(shape)
    return pl.BlockSpec(shape, lambda *_: (0,) * nd)


def _const_spec(shape):
    nd = len(shape)
    return pl.BlockSpec(shape, lambda *_: (0,) * nd, pipeline_mode=pl.Buffered(1))


def _zero_at_first(first, *refs):
    @pl.when(first)
    def _():
        for ref in refs:
            ref[...] = jnp.zeros_like(ref)


DILATIONS = tuple(d for _, d in DILATED_PATTERNS if d > 1)


def _dilation_perm(tm, dil):
    rows = np.arange(tm)
    perm = np.zeros((tm, tm), np.float32)
    perm[(rows % dil) * (tm // dil) + rows // dil, rows] = 1.0
    return jnp.asarray(perm, BF16)


def _store_dilated(ref, perm_ref, val, dil):
    per = val.shape[0] // dil
    sorted_rows = _dot(perm_ref[...], val).astype(BF16)
    for r in range(dil):
        ref[:, r * D_B:(r + 1) * D_B] = sorted_rows[r * per:(r + 1) * per, :]


def _dilated_specs(s_len, tm, dtype):
    specs = [pl.BlockSpec((tm // d, d * D_B), lambda i: (i, 0)) for d in DILATIONS]
    return specs, [SDS((s_len // d, d * D_B), dtype) for d in DILATIONS]


def _inproj_fwd(x, g_pre, w_in, cos, sin, gq, gk, tm):
    s_len = x.shape[0]
    n_d = len(DILATIONS)

    def body(x_ref, g_ref, w_ref, cos_ref, sin_ref, gq_ref, gk_ref, *rest):
        perms, (qa_raw, ka_raw, qs, kn, va, qb, kb, vb, xn_out), views = rest[:n_d], rest[n_d:n_d + 9], rest[n_d + 9:]
        xn, _ = _rms_fwd(x_ref[...], g_ref[...])
        xn = xn.astype(BF16)
        xn_out[...] = xn
        qa = _dot(xn, w_ref[:, OFF_QA:OFF_KA])
        qa_raw[...] = qa
        cos, sin = cos_ref[...], sin_ref[...]
        qs[...] = (_head_norm_rope(qa, gq_ref[...], jnp.tile(cos, (1, D_A // D_KV_A)), jnp.tile(sin, (1, D_A // D_KV_A)))
                   * SCORE_SCALE).astype(BF16)
        ka = _dot(xn, w_ref[:, OFF_KA:OFF_VA])
        ka_raw[...] = ka
        kn[...] = _head_norm_rope(ka, gk_ref[...], cos, sin).astype(BF16)
        va[...] = _dot(xn, w_ref[:, OFF_VA:OFF_QB]).astype(BF16)
        mixer_b = ((_dot(xn, w_ref[:, OFF_QB:OFF_KB]) * SCORE_SCALE).astype(BF16),
                   _dot(xn, w_ref[:, OFF_KB:OFF_VB]).astype(BF16), _dot(xn, w_ref[:, OFF_VB:D_IN]).astype(BF16))
        for a, (val, plain) in enumerate(zip(mixer_b, (qb, kb, vb))):
            plain[...] = val
            for j, d in enumerate(DILATIONS):
                _store_dilated(views[a * n_d + j], perms[j], val, d)

    view_specs, view_shapes = _dilated_specs(s_len, tm, BF16)
    return pl.pallas_call(
        body, name="inproj_fwd", grid=(s_len // tm,),
        in_specs=[_row_spec(tm, D_MODEL), _full_spec((1, D_MODEL)), _const_spec((D_MODEL, D_IN)),
                  _row_spec(tm, D_KV_A), _row_spec(tm, D_KV_A), _full_spec((1, D_A)), _full_spec((1, D_KV_A))]
                 + [_full_spec((tm, tm))] * n_d,
        out_specs=[_row_spec(tm, D_A), _row_spec(tm, D_KV_A), _row_spec(tm, D_A), _row_spec(tm, D_KV_A),
                   _row_spec(tm, D_KV_A), _row_spec(tm, D_B), _row_spec(tm, D_B), _row_spec(tm, D_B),
                   _row_spec(tm, D_MODEL)] + view_specs * 3,
        out_shape=[SDS((s_len, D_A), F32), SDS((s_len, D_KV_A), F32), SDS((s_len, D_A), BF16),
                   SDS((s_len, D_KV_A), BF16), SDS((s_len, D_KV_A), BF16), SDS((s_len, D_B), BF16),
                   SDS((s_len, D_B), BF16), SDS((s_len, D_B), BF16), SDS((s_len, D_MODEL), BF16)] + view_shapes * 3,
        compiler_params=_params(1, V7X_VMEM_LIMIT),
    )(x, g_pre, w_in, cos, sin, gq, gk, *[_dilation_perm(tm, d) for d in DILATIONS])


def _stack_heads(ref, tq):
    return jnp.concatenate([ref[:, HEAD_DIM * g:HEAD_DIM * (g + 1)] for g in range(GROUP_A)], axis=0)


def _stack_cols(ref, tq):
    return jnp.concatenate([ref[:, HEAD_DIM * g:HEAD_DIM * g + 1] for g in range(GROUP_A)], axis=0)


def _attn_fwd(qs, k, vt, tq, tk):
    s_len = qs.shape[0]
    nk = s_len // tk
    assert nk % 2 == 0
    gw = GROUP_A * HEAD_DIM
    rows = GROUP_A * tq
    vrows = vt.shape[2]

    def body(q_ref, k_ref, vt_ref, o_ref, lse_ref, s_buf):
        qt = _stack_heads(q_ref, tq).T

        def scores(j, slot):
            kj = k_ref[pl.ds(pl.multiple_of(j * tk, tk), tk), :]
            s_buf[slot] = _dot(kj, qt)

        def consume(j, slot, carry):
            m, acc = carry
            st = s_buf[slot]
            m_new = jnp.maximum(m, jnp.max(st, axis=0, keepdims=True))
            pt = jnp.exp(st - m_new)
            acc = jnp.exp(m - m_new) * acc + _dot(vt_ref[j], pt.astype(BF16))
            return m_new, acc

        scores(0, 0)

        def pair(j, carry, more):
            scores(j + 1, 1)
            carry = consume(j, 0, carry)
            if more:
                scores(j + 2, 0)
            return consume(j + 1, 1, carry)

        carry = (jnp.full((1, rows), NEG_BIG, F32), jnp.zeros((vrows, rows), F32))
        carry = lax.fori_loop(0, nk // 2 - 1, lambda jj, c: pair(2 * jj, c, True), carry)
        m, acc = pair(nk - 2, carry, False)
        l = acc[HEAD_DIM:HEAD_DIM + 1]
        o = (acc[:HEAD_DIM] / l).T
        lse = jnp.broadcast_to(m + jnp.log(l), (HEAD_DIM, rows)).T
        for g in range(GROUP_A):
            o_ref[:, HEAD_DIM * g:HEAD_DIM * (g + 1)] = o[g * tq:(g + 1) * tq]
            lse_ref[:, HEAD_DIM * g:HEAD_DIM * (g + 1)] = lse[g * tq:(g + 1) * tq]

    tile = pl.BlockSpec((tq, gw), lambda kv, i: (i, kv))
    return pl.pallas_call(
        body, name="attn_fwd", grid=(N_KV_A, s_len // tq),
        in_specs=[tile, pl.BlockSpec((None, s_len, HEAD_DIM), lambda kv, i: (kv, 0, 0)),
                  pl.BlockSpec((None, nk, vrows, tk), lambda kv, i: (kv, 0, 0, 0))],
        out_specs=[tile, tile],
        out_shape=(SDS((s_len, D_A), F32), SDS((s_len, D_A), F32)),
        scratch_shapes=[pltpu.VMEM((2, tk, rows), F32)],
        compiler_params=_params(2, V7X_VMEM_LIMIT),
    )(qs, k, vt)


def _attn_bwd(qs, do, lse, delta, kt, k, vt, tq, tk):
    s_len = qs.shape[0]
    nk = s_len // tk
    assert nk % 2 == 0
    nq = s_len // tq
    gw = GROUP_A * HEAD_DIM
    rows = GROUP_A * tq

    def body(q_ref, do_ref, lse_ref, delta_ref, kt_ref, k_ref, vt_ref, dq_ref, dk_hbm, dv_hbm,
             dk_acc, dv_acc, s_buf, dp_buf):
        kv = pl.program_id(0)
        i = pl.program_id(1)

        @pl.when(i == 0)
        def _():
            dk_acc[...] = jnp.zeros_like(dk_acc)
            dv_acc[...] = jnp.zeros_like(dv_acc)

        q = _stack_heads(q_ref, tq)
        dout = _stack_heads(do_ref, tq)
        qt = q.T
        doutt = dout.T
        row_lse = _stack_cols(lse_ref, tq)
        row_delta = _stack_cols(delta_ref, tq)

        def scores(j, slot):
            s_buf[slot] = _dot(q, kt_ref[j])
            dp_buf[slot] = _dot(dout, vt_ref[j])

        def consume(j, slot, dq):
            p = jnp.exp(s_buf[slot] - row_lse)
            ds = (p * (dp_buf[slot] - row_delta)).astype(BF16)
            dv_acc[j] += _dot(doutt, p.astype(BF16))
            dk_acc[j] += _dot(qt, ds)
            return dq + _dot(ds, k_ref[pl.ds(pl.multiple_of(j * tk, tk), tk), :])

        scores(0, 0)

        def pair(j, dq, more):
            scores(j + 1, 1)
            dq = consume(j, 0, dq)
            if more:
                scores(j + 2, 0)
            return consume(j + 1, 1, dq)

        dq = lax.fori_loop(0, nk // 2 - 1, lambda jj, c: pair(2 * jj, c, True), jnp.zeros((rows, HEAD_DIM), F32))
        dq = pair(nk - 2, dq, False) * SCORE_SCALE
        for g in range(GROUP_A):
            dq_ref[:, HEAD_DIM * g:HEAD_DIM * (g + 1)] = dq[g * tq:(g + 1) * tq]

        @pl.when(i == nq - 1)
        def _():
            pltpu.sync_copy(dk_acc, dk_hbm.at[kv])
            pltpu.sync_copy(dv_acc, dv_hbm.at[kv])

    tile = pl.BlockSpec((tq, gw), lambda kv, i: (i, kv))
    chunks = pl.BlockSpec((None, nk, HEAD_DIM, tk), lambda kv, i: (kv, 0, 0, 0))
    grad_t = SDS((N_KV_A, nk, HEAD_DIM, tk), F32)
    return pl.pallas_call(
        body, name="attn_bwd", grid=(N_KV_A, nq),
        in_specs=[tile, tile, tile, tile, chunks,
                  pl.BlockSpec((None, s_len, HEAD_DIM), lambda kv, i: (kv, 0, 0)), chunks],
        out_specs=[tile, pl.BlockSpec(memory_space=pl.ANY), pl.BlockSpec(memory_space=pl.ANY)],
        out_shape=(SDS((s_len, D_A), F32), grad_t, grad_t),
        scratch_shapes=[pltpu.VMEM((nk, HEAD_DIM, tk), F32), pltpu.VMEM((nk, HEAD_DIM, tk), F32),
                        pltpu.VMEM((2, rows, tk), F32), pltpu.VMEM((2, rows, tk), F32)],
        compiler_params=_params(2, V7X_VMEM_LIMIT),
    )(qs, do, lse, delta, kt, k, vt)


STAT_W = 128


def _band_specs(length, t, width=D_B):
    hb = t // BAND_HALF
    last = length // BAND_HALF - 1
    main = pl.BlockSpec((t, width), lambda r, i: (i, r))
    prev = pl.BlockSpec((BAND_HALF, width), lambda r, i: (jnp.maximum(i * hb - 1, 0), r))
    nxt = pl.BlockSpec((BAND_HALF, width), lambda r, i: (jnp.minimum((i + 1) * hb, last), r))
    return main, [prev, main, nxt]


def _pack_heads(cols):
    lane = lax.broadcasted_iota(jnp.int32, (cols[0].shape[0], STAT_W), 1)
    out = jnp.zeros((cols[0].shape[0], STAT_W), F32)
    for h, c in enumerate(cols):
        out = jnp.where(lane == h, c, out)
    return out


def _spread_heads(stat):
    head = lax.broadcasted_iota(jnp.int32, (stat.shape[0], D_B), 1) >> 6
    out = jnp.zeros((stat.shape[0], D_B), F32)
    for h in range(N_HEADS_B):
        out = jnp.where(head == h, stat[:, h:h + 1], out)
    return out


def _window(refs):
    return jnp.concatenate([r[...] for r in refs], axis=0)


def _head(v, h):
    return v[:, HEAD_DIM * h:HEAD_DIM * (h + 1)]


def _call_with_rider(body, name, grid, in_specs, out_specs, out_shape, scratch_shapes, args, rider):
    n_in, n_out = len(in_specs), len(out_specs)
    if rider is None:
        res = pl.pallas_call(body, name=name, grid=grid, in_specs=in_specs, out_specs=out_specs, out_shape=out_shape,
                             scratch_shapes=scratch_shapes, compiler_params=_params(len(grid), V7X_VMEM_LIMIT))(*args)
        return res, []
    kind, arrays = rider
    n = len(arrays)

    def with_rider(*refs):
        ins, c_ins = refs[:n_in], refs[n_in:n_in + n]
        outs, c_outs = refs[n_in + n:n_in + n + n_out], refs[n_in + n + n_out:n_in + 2 * n + n_out]
        rest = refs[n_in + 2 * n + n_out:]
        scratch, sems = rest[:-3], rest[-3:]
        ids = [pl.program_id(a) for a in range(len(grid))]
        first = functools.reduce(jnp.logical_and, [i == 0 for i in ids])
        last = functools.reduce(jnp.logical_and, [i == g - 1 for i, g in zip(ids, grid)])

        @pl.when(first)
        def _():
            _comm_start(kind, c_ins, c_outs, sems)

        body(*ins, *outs, *scratch)

        @pl.when(last)
        def _():
            _comm_wait(kind, c_ins, c_outs, sems)

    res = pl.pallas_call(
        with_rider, name=name, grid=grid, in_specs=list(in_specs) + [ANY_SPEC] * n,
        out_specs=list(out_specs) + [ANY_SPEC] * n, out_shape=list(out_shape) + _comm_out_shapes(kind, arrays),
        scratch_shapes=list(scratch_shapes) + _comm_sems(n), compiler_params=_params(len(grid), V7X_VMEM_LIMIT),
    )(*args, *arrays)
    return res[:n_out], res[n_out:]


def _band_fwd(q, k, v, bias, dil, t, rider=None):
    s_len = q.size // D_B
    length = s_len // dil
    w = t + 2 * BAND_HALF
    view = lambda a: a.reshape(length, dil * D_B)
    main, win = _band_specs(length, t)
    stat, _ = _band_specs(length, t, STAT_W)

    def body(q_ref, k0, k1, k2, v0, v1, v2, bias_ref, o_ref, lse_ref, s_buf, kt_buf):
        lses = []
        i = pl.program_id(1)
        kt_buf[...] = _window((k0, k1, k2)).T
        vw = _window((v0, v1, v2))
        pos = i * t - BAND_HALF + lax.broadcasted_iota(jnp.int32, (1, w), 1)
        valid = (pos >= 0) & (pos < length)
        for h in range(N_HEADS_B):
            s_buf[h] = _dot(q_ref[:, HEAD_DIM * h:HEAD_DIM * (h + 1)], kt_buf[HEAD_DIM * h:HEAD_DIM * (h + 1), :])
        for h in range(N_HEADS_B):
            s = jnp.where(valid, s_buf[h] + bias_ref[h], NEG_BIG)
            m = jnp.max(s, axis=-1, keepdims=True)
            e = jnp.exp(s - m)
            den = jnp.sum(e, axis=-1, keepdims=True)
            o_ref[:, HEAD_DIM * h:HEAD_DIM * (h + 1)] = _dot(e.astype(BF16), _head(vw, h)) / den
            lses.append(m + jnp.log(den))
        lse_ref[...] = _pack_heads(lses)

    (o, lse), rode = _call_with_rider(
        body, f"band_fwd_d{dil}", (dil, length // t),
        [main] + win + win + [_full_spec((N_HEADS_B, t, w))], [main, stat],
        [SDS((length, dil * D_B), F32), SDS((length, dil * STAT_W), F32)],
        [pltpu.VMEM((N_HEADS_B, t, w), F32), pltpu.VMEM((D_B, w), BF16)],
        (view(q), view(k), view(k), view(k), view(v), view(v), view(v), bias), rider)
    return o.reshape(s_len, D_B), lse.reshape(s_len, STAT_W), rode


def _band_dq(q, do, lse, delta, k, v, bias, dil, t, rider=None):
    s_len = q.size // D_B
    length = s_len // dil
    w = t + 2 * BAND_HALF
    view = lambda a: a.reshape(length, dil * D_B)
    main, win = _band_specs(length, t)

    def body(q_ref, do_ref, lse_ref, delta_ref, k0, k1, k2, v0, v1, v2, bias_ref, dq_ref, dbias_ref,
             s_buf, dp_buf, kt_buf, vt_buf, dq_buf):
        i = pl.program_id(1)
        @pl.when((pl.program_id(0) == 0) & (i == 0))
        def _():
            dbias_ref[...] = jnp.zeros_like(dbias_ref)

        kw = _window((k0, k1, k2))
        kt_buf[...] = kw.T
        vt_buf[...] = _window((v0, v1, v2)).T
        pos = i * t - BAND_HALF + lax.broadcasted_iota(jnp.int32, (1, w), 1)
        valid = (pos >= 0) & (pos < length)
        for h in range(N_HEADS_B):
            cols = slice(HEAD_DIM * h, HEAD_DIM * (h + 1))
            s_buf[h] = _dot(q_ref[:, cols], kt_buf[cols, :])
            dp_buf[h] = _dot(do_ref[:, cols], vt_buf[cols, :])
        for h in range(N_HEADS_B):
            cols = slice(HEAD_DIM * h, HEAD_DIM * (h + 1))
            s = jnp.where(valid, s_buf[h] + bias_ref[h], NEG_BIG)
            p = jnp.exp(s - lse_ref[:, h:h + 1])
            ds = p * (dp_buf[h] - delta_ref[:, h:h + 1])
            dq_buf[:, cols] = _dot(ds.astype(BF16), _head(kw, h)) * SCORE_SCALE
            dbias_ref[h] += ds
        dq_ref[...] = dq_buf[...].astype(BF16)

    stat, _ = _band_specs(length, t, STAT_W)
    sview = lambda a: a.reshape(length, dil * STAT_W)
    (dq, dbias), rode = _call_with_rider(
        body, f"band_dq_d{dil}", (dil, length // t),
        [main, main, stat, stat] + win + win + [_full_spec((N_HEADS_B, t, w))],
        [main, _full_spec((N_HEADS_B, t, w))],
        [SDS((length, dil * D_B), BF16), SDS((N_HEADS_B, t, w), F32)],
        [pltpu.VMEM((N_HEADS_B, t, w), F32), pltpu.VMEM((N_HEADS_B, t, w), F32),
         pltpu.VMEM((D_B, w), BF16), pltpu.VMEM((D_B, w), BF16), pltpu.VMEM((t, D_B), F32)],
        (view(q), view(do), sview(lse), sview(delta), view(k), view(k), view(k), view(v), view(v), view(v), bias),
        rider)
    return dq, dbias, rode


def _band_dkv(k, v, q, do, lse, delta, bias_t, dil, t):
    s_len = q.size // D_B
    length = s_len // dil
    w = t + 2 * BAND_HALF
    view = lambda a: a.reshape(length, dil * D_B)
    main, win = _band_specs(length, t)

    def body(k_ref, v_ref, q0, q1, q2, d0, d1, d2, l0, l1, l2, e0, e1, e2, bias_ref, dk_ref, dv_ref,
             s_buf, dp_buf, kt_buf, vt_buf):
        i = pl.program_id(1)
        qw = _window((q0, q1, q2))
        dow = _window((d0, d1, d2))
        lsew = _window((l0, l1, l2))
        deltaw = _window((e0, e1, e2))
        pos = i * t - BAND_HALF + lax.broadcasted_iota(jnp.int32, (w, 1), 0)
        valid = (pos >= 0) & (pos < length)
        kt_buf[...] = k_ref[...].T
        vt_buf[...] = v_ref[...].T
        for h in range(N_HEADS_B):
            cols = slice(HEAD_DIM * h, HEAD_DIM * (h + 1))
            s_buf[h] = _dot(_head(qw, h), kt_buf[cols, :])
            dp_buf[h] = _dot(_head(dow, h), vt_buf[cols, :])
        qwt = qw.T
        dowt = dow.T
        dkt, dvt = [], []
        for h in range(N_HEADS_B):
            rows = slice(HEAD_DIM * h, HEAD_DIM * (h + 1))
            s = jnp.where(valid, s_buf[h] + bias_ref[h], NEG_BIG)
            p = jnp.exp(s - lsew[:, h:h + 1])
            ds = p * (dp_buf[h] - deltaw[:, h:h + 1])
            dvt.append(_dot(dowt[rows, :], p.astype(BF16)))
            dkt.append(_dot(qwt[rows, :], ds.astype(BF16)))
        dv_ref[...] = jnp.concatenate(dvt, axis=0).T.astype(BF16)
        dk_ref[...] = jnp.concatenate(dkt, axis=0).T.astype(BF16)

    _, swin = _band_specs(length, t, STAT_W)
    sview = lambda a: a.reshape(length, dil * STAT_W)
    dk, dv = pl.pallas_call(
        body, name=f"band_dkv_d{dil}", grid=(dil, length // t),
        in_specs=[main, main] + win + win + swin + swin + [_full_spec((N_HEADS_B, w, t))],
        out_specs=[main, main],
        out_shape=(SDS((length, dil * D_B), BF16), SDS((length, dil * D_B), BF16)),
        scratch_shapes=[pltpu.VMEM((N_HEADS_B, w, t), F32), pltpu.VMEM((N_HEADS_B, w, t), F32),
                        pltpu.VMEM((D_B, t), BF16), pltpu.VMEM((D_B, t), BF16)],
        compiler_params=_params(2, V7X_VMEM_LIMIT),
    )(view(k), view(v), view(q), view(q), view(q), view(do), view(do), view(do), sview(lse), sview(lse), sview(lse),
      sview(delta), sview(delta), sview(delta), bias_t)
    return dk, dv


def _t5_bucket_np(rel):
    nb = N_BUCKETS // 2
    max_exact = nb // 2
    side = np.where(rel > 0, nb, 0)
    n = np.abs(rel)
    ratio = np.maximum(n, max_exact).astype(np.float32) / np.float32(max_exact)
    large = max_exact + (np.log(ratio) / np.float32(math.log(MAX_DISTANCE / max_exact))
                         * np.float32(nb - max_exact)).astype(np.int32)
    large = np.minimum(large, nb - 1)
    return (side + np.where(n < max_exact, n, large)).astype(np.int32)


def _band_buckets(dil, t):
    rel = np.arange(t + 2 * BAND_HALF)[None, :] - BAND_HALF - np.arange(t)[:, None]
    bucket = _t5_bucket_np(np.clip(rel, -BAND_HALF, BAND_HALF) * dil)
    return np.where(np.abs(rel) <= BAND_HALF, bucket, -1).astype(np.int32)


def _toeplitz(vals, rows, cols):
    heads = vals.shape[0]
    period = rows + cols
    vec = jnp.concatenate([vals[:, rows - 1:], jnp.zeros((heads, 1), vals.dtype), vals[:, :rows - 1]], axis=1)
    flat = jnp.broadcast_to(vec[:, None, :], (heads, rows, period)).reshape(heads, rows * period)
    return flat[:, :rows * (period - 1)].reshape(heads, rows, period - 1)[:, :, :cols]


def _bias_tiles(rel_bias, dil, t):
    w = t + 2 * BAND_HALF
    rel = np.arange(-BAND_HALF, BAND_HALF + 1)
    bucket = _t5_bucket_np(rel * dil)
    runs, start = [], 0
    for i in range(1, len(bucket) + 1):
        if i == len(bucket) or bucket[i] != bucket[start]:
            b = int(bucket[start])
            runs.append(jnp.broadcast_to(rel_bias[b:b + 1], (i - start, N_HEADS_B)))
            start = i
    per_rel = jnp.concatenate(runs, axis=0).T

    def diagonals(lo, hi):
        left = jnp.full((N_HEADS_B, max(0, -BAND_HALF - lo)), NEG_BIG, F32)
        right = jnp.full((N_HEADS_B, max(0, hi - BAND_HALF)), NEG_BIG, F32)
        return jnp.concatenate([left, per_rel, right], axis=1)

    tile = _toeplitz(diagonals(-(t - 1) - BAND_HALF, w - 1 - BAND_HALF), t, w)
    twin = _toeplitz(diagonals(-(w - 1) + BAND_HALF, t - 1 + BAND_HALF), w, t)
    return tile, twin


def _dbias_reduce(dbias, buckets):
    n = len(dbias)

    def body(*refs):
        db_refs, bk_refs, o_ref = refs[:n], refs[n:2 * n], refs[2 * n]
        row = lax.broadcasted_iota(jnp.int32, (N_BUCKETS, 128), 0)
        lane = lax.broadcasted_iota(jnp.int32, (N_BUCKETS, 128), 1)

        def per_bucket(b, out):
            for pat in range(n):
                msk = bk_refs[pat][...] == b
                for h in range(N_HEADS_B):
                    tot = jnp.sum(jnp.where(msk, db_refs[pat][h], 0.0), axis=-1, keepdims=True)
                    tot = jnp.sum(tot, axis=0, keepdims=True)
                    out = out + jnp.where((row == b) & (lane == h), tot, 0.0)
            return out

        o_ref[...] = lax.fori_loop(0, N_BUCKETS, per_bucket, jnp.zeros((N_BUCKETS, 128), F32))

    return pl.pallas_call(
        body, name="dbias_reduce", out_shape=SDS((N_BUCKETS, 128), F32),
        compiler_params=pltpu.CompilerParams(vmem_limit_bytes=V7X_VMEM_LIMIT),
    )(*dbias, *buckets)


def _attn_out_fwd(ya, ob, lb, x, g_a, g_b, w_out, g_post, tm):
    s_len = ya.shape[0]

    def body(ya_ref, o0, o1, o2, l0, l1, l2, x_ref, ga_ref, gb_ref, w_ref, gp_ref,
             h1_ref, yo_ref, yb_ref, lse_ref, ycat_ref):
        m = jnp.maximum(jnp.maximum(l0[...], l1[...]), l2[...])
        w0, w1, w2 = jnp.exp(l0[...] - m), jnp.exp(l1[...] - m), jnp.exp(l2[...] - m)
        wsum = w0 + w1 + w2
        yb = (_spread_heads(w0 / wsum) * o0[...] + _spread_heads(w1 / wsum) * o1[...]
              + _spread_heads(w2 / wsum) * o2[...])
        yb_ref[...] = yb
        lse_ref[...] = m + jnp.log(wsum)
        yan, _ = _rms_fwd(ya_ref[...], ga_ref[...])
        ybn, _ = _rms_fwd(yb, gb_ref[...])
        yan, ybn = yan.astype(BF16), ybn.astype(BF16)
        ycat_ref[:, :D_A] = yan
        ycat_ref[:, D_A:] = ybn
        yo = _dot(yan, w_ref[:D_A, :]) + _dot(ybn, w_ref[D_A:, :])
        yo_ref[...] = yo
        post, _ = _rms_fwd(yo, gp_ref[...])
        h1_ref[...] = x_ref[...] + post

    half, full, stat = _row_spec(tm, D_A), _row_spec(tm, D_MODEL), _row_spec(tm, STAT_W)
    return pl.pallas_call(
        body, name="attn_out_fwd", grid=(s_len // tm,),
        in_specs=[half] * 4 + [stat] * 3 + [full, _full_spec((1, D_A)), _full_spec((1, D_B)),
                                            _full_spec((D_MODEL, D_MODEL)), _full_spec((1, D_MODEL))],
        out_specs=[full, full, half, stat, full],
        out_shape=(SDS((s_len, D_MODEL), F32), SDS((s_len, D_MODEL), F32), SDS((s_len, D_B), F32),
                   SDS((s_len, STAT_W), F32), SDS((s_len, D_MODEL), BF16)),
        compiler_params=_params(1, V7X_VMEM_LIMIT),
    )(ya, ob[0], ob[1], ob[2], lb[0], lb[1], lb[2], x, g_a, g_b, w_out, g_post)


def _attn_out_bwd(dh1, yo, ya, yb, g_a, g_b, w_out, g_post, tm):
    s_len = ya.shape[0]
    n_d = len(DILATIONS)

    def body(dh1_ref, yo_ref, ya_ref, yb_ref, ga_ref, gb_ref, w_ref, gp_ref, *rest):
        perms, views = rest[:n_d], rest[n_d + 8:]
        dyo_ref, dya_ref, dyb_ref, dela_ref, delb_ref, dgp_ref, dga_ref, dgb_ref = rest[n_d:n_d + 8]
        _zero_at_first(pl.program_id(0) == 0, dgp_ref, dga_ref, dgb_ref)
        dyo, dgp = _rms_bwd(yo_ref[...], gp_ref[...], dh1_ref[...])
        dyo = dyo.astype(BF16)
        dyo_ref[...] = dyo
        dgp_ref[...] += dgp
        dya_n = _dot_nt(dyo, w_ref[:D_A, :])
        dyb_n = _dot_nt(dyo, w_ref[D_A:, :])
        ya, yb = ya_ref[...], yb_ref[...]
        dya, dga = _rms_bwd(ya, ga_ref[...], dya_n)
        dyb, dgb = _rms_bwd(yb, gb_ref[...], dyb_n)
        dga_ref[...] += dga
        dgb_ref[...] += dgb
        dya_ref[...] = dya.astype(BF16)
        dyb_ref[...] = dyb.astype(BF16)
        for j, d in enumerate(DILATIONS):
            _store_dilated(views[j], perms[j], dyb.astype(BF16), d)
        dela_ref[...] = _head_sum(dya * ya)
        prod = dyb * yb
        head = lax.broadcasted_iota(jnp.int32, prod.shape, 1) >> 6
        delb_ref[...] = _pack_heads([jnp.sum(jnp.where(head == h, prod, 0.0), axis=-1, keepdims=True)
                                     for h in range(N_HEADS_B)])

    half, full = _row_spec(tm, D_A), _row_spec(tm, D_MODEL)
    view_specs, view_shapes = _dilated_specs(s_len, tm, BF16)
    return pl.pallas_call(
        body, name="attn_out_bwd", grid=(s_len // tm,),
        in_specs=[full, full, half, half, _full_spec((1, D_A)), _full_spec((1, D_B)),
                  _full_spec((D_MODEL, D_MODEL)), _full_spec((1, D_MODEL))] + [_full_spec((tm, tm))] * n_d,
        out_specs=[full, half, half, half, _row_spec(tm, STAT_W), _full_spec((1, D_MODEL)), _full_spec((1, D_A)),
                   _full_spec((1, D_B))] + view_specs,
        out_shape=[SDS((s_len, D_MODEL), BF16), SDS((s_len, D_A), BF16), SDS((s_len, D_B), BF16),
                   SDS((s_len, D_A), F32), SDS((s_len, STAT_W), F32), SDS((1, D_MODEL), F32), SDS((1, D_A), F32),
                   SDS((1, D_B), F32)] + view_shapes,
        compiler_params=_params(1, V7X_VMEM_LIMIT),
    )(dh1, yo, ya, yb, g_a, g_b, w_out, g_post, *[_dilation_perm(tm, d) for d in DILATIONS])


FF_CHUNK = 1024


def _mlp_fwd(h1, g_pre, w1, w2, g_post, tm):
    s_len = h1.shape[0]

    def body(h1_ref, gpre_ref, w1_ref, w2_ref, gpost_ref, h2_ref, fo_ref, xn_ref):
        h1v = h1_ref[...]
        xn, _ = _rms_fwd(h1v, gpre_ref[...])
        xn = xn.astype(BF16)
        xn_ref[...] = xn
        fo = jnp.zeros((tm, D_MODEL), F32)
        for c in range(D_FF // FF_CHUNK):
            cols = slice(c * FF_CHUNK, (c + 1) * FF_CHUNK)
            u = jnp.maximum(_dot(xn, w1_ref[:, cols]), 0.0)
            fo = fo + _dot((u * u).astype(BF16), w2_ref[cols, :])
        fo_ref[...] = fo
        post, _ = _rms_fwd(fo, gpost_ref[...])
        h2_ref[...] = h1v + post

    full = _row_spec(tm, D_MODEL)
    return pl.pallas_call(
        body, name="mlp_fwd", grid=(s_len // tm,),
        in_specs=[full, _full_spec((1, D_MODEL)), _const_spec((D_MODEL, D_FF)), _const_spec((D_FF, D_MODEL)),
                  _full_spec((1, D_MODEL))],
        out_specs=[full, full, full],
        out_shape=(SDS((s_len, D_MODEL), F32), SDS((s_len, D_MODEL), F32), SDS((s_len, D_MODEL), BF16)),
        compiler_params=_params(1, V7X_VMEM_LIMIT),
    )(h1, g_pre, w1, w2, g_post)


def _mlp_bwd(h1, dh2, fo, xn, g_pre, w1, w2, g_post, tm):
    s_len = h1.shape[0]

    def body(h1_ref, dh2_ref, fo_ref, xn_ref, gpre_ref, w1_ref, w2_ref, gpost_ref,
             dh1_ref, dfo_ref, du_ref, f_ref, dgpost_ref, dgpre_ref):
        _zero_at_first(pl.program_id(0) == 0, dgpost_ref, dgpre_ref)
        dh2 = dh2_ref[...]
        dfo, dgpost = _rms_bwd(fo_ref[...], gpost_ref[...], dh2)
        dfo = dfo.astype(BF16)
        dfo_ref[...] = dfo
        dgpost_ref[...] += dgpost
        xn = xn_ref[...]
        dxn = jnp.zeros((tm, D_MODEL), F32)
        for c in range(D_FF // FF_CHUNK):
            cols = slice(c * FF_CHUNK, (c + 1) * FF_CHUNK)
            u = jnp.maximum(_dot(xn, w1_ref[:, cols]), 0.0)
            f_ref[:, cols] = (u * u).astype(BF16)
            du = (_dot_nt(dfo, w2_ref[cols, :]) * (2.0 * u)).astype(BF16)
            du_ref[:, cols] = du
            dxn = dxn + _dot_nt(du, w1_ref[:, cols])
        dx, dgpre = _rms_bwd(h1_ref[...], gpre_ref[...], dxn)
        dgpre_ref[...] += dgpre
        dh1_ref[...] = dh2 + dx

    full, wide = _row_spec(tm, D_MODEL), _row_spec(tm, D_FF)
    return pl.pallas_call(
        body, name="mlp_bwd", grid=(s_len // tm,),
        in_specs=[full, full, full, full, _full_spec((1, D_MODEL)), _const_spec((D_MODEL, D_FF)),
                  _const_spec((D_FF, D_MODEL)), _full_spec((1, D_MODEL))],
        out_specs=[full, full, wide, wide, _full_spec((1, D_MODEL)), _full_spec((1, D_MODEL))],
        out_shape=(SDS((s_len, D_MODEL), F32), SDS((s_len, D_MODEL), BF16), SDS((s_len, D_FF), BF16),
                   SDS((s_len, D_FF), BF16), SDS((1, D_MODEL), F32), SDS((1, D_MODEL), F32)),
        compiler_params=_params(1, V7X_VMEM_LIMIT),
    )(h1, dh2, fo, xn, g_pre, w1, w2, g_post)


def _ple_fwd_bwd(h2, p, target, g_ple, w_gate, w_proj, tm):
    s_len = h2.shape[0]

    def body(h2_ref, p_ref, t_ref, g_ref, wg_ref, wp_ref, dh2_ref, loss_ref, dg_ref, xn_ref, dgl_ref, dpp_ref, pb_ref):
        _zero_at_first(pl.program_id(0) == 0, loss_ref, dg_ref)
        h2 = h2_ref[...]
        g = g_ref[...]
        xn, _ = _rms_fwd(h2, g)
        xn = xn.astype(BF16)
        xn_ref[...] = xn
        gate = 1.0 / (1.0 + jnp.exp(-_dot(xn, wg_ref[...])))
        pb = p_ref[...].astype(BF16)
        pb_ref[...] = pb
        pp = _dot(pb, wp_ref[...])
        diff = h2 + gate * pp - t_ref[...]
        loss_ref[...] += jnp.full((8, 128), jnp.sum(diff * diff), F32)
        dh3 = diff * (1.0 / D_MODEL)
        dpp_ref[...] = (dh3 * gate).astype(BF16)
        dgl = (dh3 * pp * gate * (1.0 - gate)).astype(BF16)
        dgl_ref[...] = dgl
        dx, dg = _rms_bwd(h2, g, _dot_nt(dgl, wg_ref[...]))
        dg_ref[...] += dg
        dh2_ref[...] = dh3 + dx

    full = _row_spec(tm, D_MODEL)
    return pl.pallas_call(
        body, name="ple_fwd_bwd", grid=(s_len // tm,),
        in_specs=[full, _row_spec(tm, D_PLE), full, _full_spec((1, D_MODEL)), _full_spec((D_MODEL, D_MODEL)),
                  _full_spec((D_PLE, D_MODEL))],
        out_specs=[full, _full_spec((8, 128)), _full_spec((1, D_MODEL)), full, full, full, _row_spec(tm, D_PLE)],
        out_shape=(SDS((s_len, D_MODEL), F32), SDS((8, 128), F32), SDS((1, D_MODEL), F32), SDS((s_len, D_MODEL), BF16),
                   SDS((s_len, D_MODEL), BF16), SDS((s_len, D_MODEL), BF16), SDS((s_len, D_PLE), BF16)),
        compiler_params=_params(1, V7X_VMEM_LIMIT),
    )(h2, p, target, g_ple, w_gate, w_proj)


def _inproj_bwd(dqs, dk_t, dv_t, parts_b, qa_raw, ka_raw, x, dh1, g_pre, w_in, cos, sin, gq, gk, tm):
    s_len = x.shape[0]
    tk = dk_t.shape[3]

    n_d = len(DILATIONS)
    n_parts = 3 * (1 + n_d)

    def body(dqs_ref, dkt_ref, dvt_ref, *rest):
        parts, rest = rest[:n_parts], rest[n_parts:]
        (qa_ref, ka_ref, x_ref, dh1_ref, g_ref, w_ref, cos_ref, sin_ref, gq_ref, gk_ref), rest = rest[:10], rest[10:]
        unsort, (dx_ref, dproj_ref, dg_ref, dgq_ref, dgk_ref) = rest[:n_d], rest[n_d:]

        def mixer_b_grad(a):
            own = parts[a * (1 + n_d):(a + 1) * (1 + n_d)]
            total = own[0][...].astype(F32)
            for j, d in enumerate(DILATIONS):
                sorted_rows = jnp.concatenate([own[1 + j][:, r * D_B:(r + 1) * D_B] for r in range(d)], axis=0)
                total = total + _dot(unsort[j][...], sorted_rows)
            return total.astype(BF16)

        _zero_at_first(pl.program_id(0) == 0, dg_ref, dgq_ref, dgk_ref)
        cos, sin = cos_ref[...], sin_ref[...]
        dkn = dkt_ref[...].reshape(D_KV_A, tm).T
        dva = dvt_ref[...].reshape(D_KV_A, tm).T
        dqa, dgq = _head_norm_rope_bwd(qa_ref[...], gq_ref[...], jnp.tile(cos, (1, D_A // D_KV_A)),
                                       jnp.tile(sin, (1, D_A // D_KV_A)), dqs_ref[...])
        dka, dgk = _head_norm_rope_bwd(ka_ref[...], gk_ref[...], cos, sin, dkn)
        dgq_ref[...] += dgq
        dgk_ref[...] += dgk
        dproj_ref[:, OFF_QA:OFF_KA] = dqa.astype(BF16)
        dproj_ref[:, OFF_KA:OFF_VA] = dka.astype(BF16)
        dproj_ref[:, OFF_VA:OFF_QB] = dva.astype(BF16)
        dproj_ref[:, OFF_QB:OFF_KB] = mixer_b_grad(0)
        dproj_ref[:, OFF_KB:OFF_VB] = mixer_b_grad(1)
        dproj_ref[:, OFF_VB:D_IN] = mixer_b_grad(2)
        dxn = _dot_nt(dproj_ref[...], w_ref[...])
        dx, dg = _rms_bwd(x_ref[...], g_ref[...], dxn)
        dg_ref[...] += dg
        dx_ref[...] = dh1_ref[...] + dx

    half, kvw, full = _row_spec(tm, D_A), _row_spec(tm, D_KV_A), _row_spec(tm, D_MODEL)
    per_chunk = tk // tm
    chunk_t = pl.BlockSpec((N_KV_A, None, HEAD_DIM, tm), lambda i: (0, i // per_chunk, 0, i % per_chunk))
    view_specs, _ = _dilated_specs(s_len, tm, BF16)
    return pl.pallas_call(
        body, name="inproj_bwd", grid=(s_len // tm,),
        in_specs=[half, chunk_t, chunk_t] + ([half] + view_specs) * 3
                 + [half, kvw, full, full, _full_spec((1, D_MODEL)), _const_spec((D_MODEL, D_IN)), kvw, kvw,
                    _full_spec((1, D_A)), _full_spec((1, D_KV_A))] + [_full_spec((tm, tm))] * n_d,
        out_specs=[full, _row_spec(tm, D_IN), _full_spec((1, D_MODEL)), _full_spec((1, D_A)), _full_spec((1, D_KV_A))],
        out_shape=(SDS((s_len, D_MODEL), F32), SDS((s_len, D_IN), BF16), SDS((1, D_MODEL), F32), SDS((1, D_A), F32),
                   SDS((1, D_KV_A), F32)),
        compiler_params=_params(1, V7X_VMEM_LIMIT),
    )(dqs, dk_t, dv_t, *parts_b, qa_raw, ka_raw, x, dh1, g_pre, w_in, cos, sin, gq, gk,
      *[_dilation_perm(tm, d).T for d in DILATIONS])


def _weight_grad(a, b, name, tk1, tn, tm):
    s_len, k1 = a.shape
    n = b.shape[1]
    steps = s_len // tm

    def body(a_ref, b_ref, o_ref, acc):
        r = pl.program_id(2)
        _zero_at_first(r == 0, acc)
        acc[...] += _dot_tn(a_ref[...], b_ref[...])

        @pl.when(r == steps - 1)
        def _():
            o_ref[...] = acc[...].astype(BF16)

    return pl.pallas_call(
        body, name=name, grid=(k1 // tk1, n // tn, steps),
        in_specs=[pl.BlockSpec((tm, tk1), lambda i, j, r: (r, i)), pl.BlockSpec((tm, tn), lambda i, j, r: (r, j))],
        out_specs=pl.BlockSpec((tk1, tn), lambda i, j, r: (i, j)),
        out_shape=SDS((k1, n), BF16),
        scratch_shapes=[pltpu.VMEM((tk1, tn), F32)],
        compiler_params=_params(3, V7X_VMEM_LIMIT),
    )(a, b)


def _my_index():
    return 4 * lax.axis_index("x") + 2 * lax.axis_index("y") + lax.axis_index("c")


def _peer(k):
    f = k + 1
    x, y, c = lax.axis_index("x"), lax.axis_index("y"), lax.axis_index("c")
    return (x ^ ((f >> 2) & 1), y ^ ((f >> 1) & 1), c ^ (f & 1))


GATHER, EXCHANGE = "gather", "exchange"
ANY_SPEC = pl.BlockSpec(memory_space=pl.ANY)


def _comm_out_shapes(kind, arrays):
    return [SDS((N_DEV,) + a.shape if kind == GATHER else a.shape, a.dtype) for a in arrays]


def _comm_sems(n):
    return [pltpu.SemaphoreType.DMA((n, N_DEV - 1)), pltpu.SemaphoreType.DMA((n, N_DEV - 1)), pltpu.SemaphoreType.DMA((n,))]


def _comm_copies(kind, ins, outs, send_sems, recv_sems, local_sems):
    me = _my_index()
    local, remote = [], []
    for a, (src, dst) in enumerate(zip(ins, outs)):
        local.append(pltpu.make_async_copy(src if kind == GATHER else src.at[me], dst.at[me], local_sems.at[a]))
        for k in range(N_DEV - 1):
            px, py, pc = _peer(k)
            remote.append(pltpu.make_async_remote_copy(
                src_ref=src if kind == GATHER else src.at[4 * px + 2 * py + pc], dst_ref=dst.at[me],
                send_sem=send_sems.at[a, k], recv_sem=recv_sems.at[a, k],
                device_id=(px, py, pc), device_id_type=pl.DeviceIdType.MESH))
    return local, remote


def _comm_start(kind, ins, outs, sems):
    local, remote = _comm_copies(kind, ins, outs, *sems)
    for cp in local + remote:
        cp.start()


def _comm_wait(kind, ins, outs, sems):
    local, remote = _comm_copies(kind, ins, outs, *sems)
    for cp in local:
        cp.wait()
    for cp in remote:
        cp.wait_send()
    for cp in remote:
        cp.wait_recv()


def _collective(kind, arrays, name):
    n = len(arrays)

    def body(*refs):
        ins, outs, sems = refs[:n], refs[n:2 * n], refs[2 * n:]
        _comm_start(kind, ins, outs, sems)
        _comm_wait(kind, ins, outs, sems)

    return pl.pallas_call(
        body, name=name, in_specs=[ANY_SPEC] * n, out_specs=[ANY_SPEC] * n,
        out_shape=_comm_out_shapes(kind, arrays), scratch_shapes=_comm_sems(n),
    )(*arrays)


def _sum_adamw(parts, w, m, v, name, tr):
    rows, cols = w.shape
    c1 = 1.0 / (1.0 - ADAM_B1 ** ADAM_STEP)
    c2 = 1.0 / (1.0 - ADAM_B2 ** ADAM_STEP)

    def body(p_ref, w_ref, m_ref, v_ref, g_ref, d_ref, nm_ref, nv_ref):
        g = p_ref[0].astype(F32)
        for j in range(1, N_DEV):
            g = g + p_ref[j].astype(F32)
        g_ref[...] = g
        nm = ADAM_B1 * m_ref[...] + (1.0 - ADAM_B1) * g
        nv = ADAM_B2 * v_ref[...] + (1.0 - ADAM_B2) * (g * g)
        nm_ref[...] = nm
        nv_ref[...] = nv
        d_ref[...] = -ADAM_LR * ((nm * c1) / (jnp.sqrt(nv * c2) + ADAM_EPS) + ADAM_WD * w_ref[...])

    blk = pl.BlockSpec((tr, cols), lambda i: (i, 0))
    return pl.pallas_call(
        body, name=name, grid=(rows // tr,),
        in_specs=[pl.BlockSpec((N_DEV, tr, cols), lambda i: (0, i, 0)), blk, blk, blk],
        out_specs=[blk] * 4, out_shape=[SDS((rows, cols), F32)] * 4,
        compiler_params=_params(1, V7X_VMEM_LIMIT),
    )(parts, w, m, v)


_SMALL = (("g_attn_pre", 1024), ("g_q", 64), ("g_k", 64), ("g_out_a", 512), ("g_out_b", 512), ("g_attn_post", 1024),
          ("rel_bias", 256), ("g_mlp_pre", 1024), ("g_mlp_post", 1024), ("g_ple", 1024))
_SLAB_ROWS = 56


def _pack_small(vals):
    rows = []
    for (name, size) in _SMALL:
        flat = vals[name].reshape(-1).astype(F32)
        padded = -(-size // 128) * 128
        rows.append(jnp.pad(flat, (0, padded - size)).reshape(padded // 128, 128))
    slab = jnp.concatenate(rows, axis=0)
    return jnp.pad(slab, ((0, _SLAB_ROWS - slab.shape[0]), (0, 0)))


def _unpack_small(slab, shapes):
    out, row = {}, 0
    for (name, size) in _SMALL:
        nrow = -(-size // 128)
        out[name] = slab[row:row + nrow].reshape(-1)[:size].reshape(shapes[name])
        row += nrow
    return out


def _rope_tables(s_len):
    rows = s_len // GRID_W
    row = jnp.broadcast_to(jnp.arange(rows)[:, None], (rows, GRID_W)).reshape(-1).astype(F32)
    col = jnp.broadcast_to(jnp.arange(GRID_W)[None, :], (rows, GRID_W)).reshape(-1).astype(F32)
    n_axis = ROPE_HALF // 2
    inv_freq = ROPE_THETA ** (-jnp.arange(n_axis, dtype=F32) / n_axis)
    ang = jnp.concatenate([row[:, None] * inv_freq, col[:, None] * inv_freq], axis=-1)
    cos, sin = jnp.cos(ang), jnp.sin(ang)
    cos = jnp.tile(jnp.concatenate([cos, cos], axis=-1), (1, N_KV_A))
    sin = jnp.tile(jnp.concatenate([-sin, sin], axis=-1), (1, N_KV_A))
    return cos, sin


_RIDERS = {1: ("w_out", "w_ff1"), 4: ("w_ff2",), 16: ("w_ple_gate", "w_ple_proj")}
_GRAD_RIDERS = {1: ("w_ff1",), 4: ("w_ff2",), 16: ("w_ple_gate", "w_ple_proj", "w_out")}


def _local_step(x, p, target, w_in, shards, small):
    axis_of = dict(_BIG)
    s_len = x.shape[0]
    tm = min(256, s_len)
    tq = min(128, s_len)
    tk = min(2048, s_len // 2)
    cos, sin = _rope_tables(s_len)
    gq = jnp.tile(small["g_q"], (1, N_HEADS_A))
    gk = jnp.tile(small["g_k"], (1, N_KV_A))

    qa_raw, ka_raw, qs, kn, va, qb, kb, vb, xn1, *views = _inproj_fwd(x, small["g_attn_pre"], w_in, cos, sin, gq, gk, tm)
    n_d = len(DILATIONS)
    qb, kb, vb = ({1: plain, **dict(zip(DILATIONS, views[a * n_d:(a + 1) * n_d]))} for a, plain in enumerate((qb, kb, vb)))

    def kv_major(a):
        return a.reshape(s_len, N_KV_A, HEAD_DIM).transpose(1, 0, 2)

    def kv_chunks_t(a):
        return a.reshape(s_len // tk, tk, N_KV_A, HEAD_DIM).transpose(2, 0, 3, 1)

    k_maj, kt, vt = kv_major(kn), kv_chunks_t(kn), kv_chunks_t(va)
    vt_ones = jnp.concatenate([vt, jnp.ones((N_KV_A, s_len // tk, 16, tk), BF16)], axis=2)
    ya, lse_a = _attn_fwd(qs, k_maj, vt_ones, tq, tk)

    ob, lb, tiles, full = [], [], [], {}
    for (_, dil) in DILATED_PATTERNS:
        t = min(256, s_len // dil)
        bias, bias_t = _bias_tiles(small["rel_bias"], dil, t)
        tiles.append((t, bias, bias_t, _band_buckets(dil, t)))
        o, l, gathered = _band_fwd(qb[dil], kb[dil], vb[dil], bias, dil, t, (GATHER, [shards[n] for n in _RIDERS[dil]]))
        ob.append(o)
        lb.append(l)
        full.update({n: _assemble(g, axis_of[n]) for n, g in zip(_RIDERS[dil], gathered)})
    w_out, w_ff1, w_ff2, w_gate, w_proj = (full[n] for n in ("w_out", "w_ff1", "w_ff2", "w_ple_gate", "w_ple_proj"))

    h1, yo, yb, lse_b, ycat = _attn_out_fwd(ya, ob, lb, x, small["g_out_a"], small["g_out_b"], w_out,
                                            small["g_attn_post"], tm)
    h2, fo, xn2 = _mlp_fwd(h1, small["g_mlp_pre"], w_ff1, w_ff2, small["g_mlp_post"], tm)
    dh2, loss_part, dg_ple, xn3, dgl, dpp, pb = _ple_fwd_bwd(h2, p, target, small["g_ple"], w_gate, w_proj, tm)
    dh1, dfo, du, f, dg_mlp_post, dg_mlp_pre = _mlp_bwd(h1, dh2, fo, xn2, small["g_mlp_pre"], w_ff1, w_ff2,
                                                          small["g_mlp_post"], tm)
    dyo, dya, dyb, delta_a, delta_b, dg_attn_post, dg_out_a, dg_out_b, *dyb_views = _attn_out_bwd(
        dh1, yo, ya, yb, small["g_out_a"], small["g_out_b"], w_out, small["g_attn_post"], tm)
    dyb = {1: dyb, **dict(zip(DILATIONS, dyb_views))}

    dqs, dk_t, dv_t = _attn_bwd(qs, dya, lse_a, delta_a, kt, k_maj, vt, tq, tk)

    tg = min(1024, s_len)
    grads = {
        "w_out": _weight_grad(ycat, dyo, "grad_w_out", D_MODEL, D_MODEL, tg),
        "w_ff1": _weight_grad(xn2, du, "grad_w_ff1", D_MODEL, 1024, tg),
        "w_ff2": _weight_grad(f, dfo, "grad_w_ff2", 1024, D_MODEL, tg),
        "w_ple_gate": _weight_grad(xn3, dgl, "grad_w_ple_gate", D_MODEL, D_MODEL, tg),
        "w_ple_proj": _weight_grad(pb, dpp, "grad_w_ple_proj", D_PLE, D_MODEL, tg),
    }

    dqb, dkb, dvb, dbias, received = {}, {}, {}, [], {}
    for (_, dil), (t, bias, bias_t, _) in zip(DILATED_PATTERNS, tiles):
        rider = (EXCHANGE, [_cut(grads[n], axis_of[n]) for n in _GRAD_RIDERS[dil]])
        dqb[dil], db, got = _band_dq(qb[dil], dyb[dil], lse_b, delta_b, kb[dil], vb[dil], bias, dil, t, rider)
        dkb[dil], dvb[dil] = _band_dkv(kb[dil], vb[dil], qb[dil], dyb[dil], lse_b, delta_b, bias_t, dil, t)
        dbias.append(db)
        received.update(zip(_GRAD_RIDERS[dil], got))
    d_rel = _dbias_reduce(dbias, [jnp.asarray(tl[3]) for tl in tiles])[:, :N_HEADS_B]

    parts_b = [part[d] for part in (dqb, dkb, dvb) for d in (1,) + DILATIONS]
    dx, dproj, dg_attn_pre, dgq_lanes, dgk_lanes = _inproj_bwd(
        dqs, dk_t, dv_t, parts_b, qa_raw, ka_raw, x, dh1, small["g_attn_pre"], w_in, cos, sin, gq, gk, tm)
    grad_w_in = _weight_grad(xn1, dproj, "grad_w_in", D_MODEL, 768, tg)
    small_grads = {
        "g_attn_pre": dg_attn_pre, "g_q": dgq_lanes.reshape(N_HEADS_A, HEAD_DIM).sum(0, keepdims=True),
        "g_k": dgk_lanes.reshape(N_KV_A, HEAD_DIM).sum(0, keepdims=True), "g_out_a": dg_out_a, "g_out_b": dg_out_b,
        "g_attn_post": dg_attn_post, "rel_bias": d_rel, "g_mlp_pre": dg_mlp_pre, "g_mlp_post": dg_mlp_post,
        "g_ple": dg_ple,
    }
    return loss_part[0, 0], dx, grad_w_in, received, small_grads


_BIG = (("w_in", 1), ("w_out", 0), ("w_ff1", 1), ("w_ff2", 0), ("w_ple_gate", 0), ("w_ple_proj", 1))


def _assemble(gathered, axis):
    if axis == 0:
        return gathered.reshape(-1, gathered.shape[2])
    return gathered.transpose(1, 0, 2).reshape(gathered.shape[1], -1)


def _cut(full, axis):
    if axis == 0:
        return full.reshape(N_DEV, full.shape[0] // N_DEV, full.shape[1])
    return full.reshape(full.shape[0], N_DEV, full.shape[1] // N_DEV).transpose(1, 0, 2)


def kernel(x, p, w_in, g_attn_pre, g_q, g_k, g_out_a, g_out_b, w_out, g_attn_post, rel_bias, g_mlp_pre, w_ff1, w_ff2, g_mlp_post, g_ple, w_ple_gate, w_ple_proj, loss_target, m_w_in, m_g_attn_pre, m_g_q, m_g_k, m_g_out_a, m_g_out_b, m_w_out, m_g_attn_post, m_rel_bias, m_g_mlp_pre, m_w_ff1, m_w_ff2, m_g_mlp_post, m_g_ple, m_w_ple_gate, m_w_ple_proj, v_w_in, v_g_attn_pre, v_g_q, v_g_k, v_g_out_a, v_g_out_b, v_w_out, v_g_attn_post, v_rel_bias, v_g_mlp_pre, v_w_ff1, v_w_ff2, v_g_mlp_post, v_g_ple, v_w_ple_gate, v_w_ple_proj):
    given = dict(locals())
    small_names = [n for n, _ in _SMALL]
    small = {n: given[n] for n in small_names}
    shards = {n: given[n][0] for n, _ in _BIG}

    bf16_shards = {n: shards[n].astype(BF16) for n, _ in _BIG}
    (gathered_w_in,) = _collective(GATHER, [bf16_shards["w_in"]], "gather_w_in")

    loss_part, dx, grad_w_in, received, small_grads = _local_step(
        x[0], p[0, 0], loss_target[0], _assemble(gathered_w_in, 1), bf16_shards, small)

    loss_at = np.zeros((_SLAB_ROWS, 128), bool)
    loss_at[-1, 0] = True
    slab = jnp.where(loss_at, loss_part * (0.5 / D_MODEL), _pack_small(small_grads))
    slab_parts = jnp.broadcast_to(slab[None], (N_DEV,) + slab.shape)
    received["w_in"], slabs = _collective(EXCHANGE, [_cut(grad_w_in, 1), slab_parts], "exchange_w_in_grads")

    out_g, out_d, out_m, out_v = {}, {}, {}, {}
    for n, _ in _BIG:
        rows = shards[n].shape[0]
        g, d, nm, nv = _sum_adamw(received[n], shards[n], given["m_" + n][0], given["v_" + n][0], "adamw_" + n,
                                  min(rows, 128))
        out_g[n], out_d[n], out_m[n], out_v[n] = g[None], d[None], nm[None], nv[None]
    g, d, nm, nv = _sum_adamw(slabs, _pack_small(small), _pack_small({n: given["m_" + n] for n in small_names}),
                              _pack_small({n: given["v_" + n] for n in small_names}), "adamw_small", _SLAB_ROWS)
    shapes = {n: given[n].shape for n in small_names}
    for dst, slab_out in ((out_g, g), (out_d, d), (out_m, nm), (out_v, nv)):
        dst.update(_unpack_small(slab_out, shapes))

    loss = g[-1, 0]
    order = ["w_in", "g_attn_pre", "g_q", "g_k", "g_out_a", "g_out_b", "w_out", "g_attn_post", "rel_bias", "g_mlp_pre",
             "w_ff1", "w_ff2", "g_mlp_post", "g_ple", "w_ple_gate", "w_ple_proj"]
    return (loss, dx[None], *[out_g[n] for n in order], *[out_d[n] for n in order], *[out_m[n] for n in order],
            *[out_v[n] for n in order])
```

```python
import functools
import math

import jax
import jax.numpy as jnp
import numpy as np
from jax import lax
from jax.experimental import pallas as pl
from jax.experimental.pallas import tpu as pltpu

F32 = jnp.float32
BF16 = jnp.bfloat16
SDS = jax.ShapeDtypeStruct

D_MODEL = 1024
HEAD_DIM = 64
N_HEADS_A = 8
N_KV_A = 2
GROUP_A = N_HEADS_A // N_KV_A
N_HEADS_B = 8
D_A = N_HEADS_A * HEAD_DIM
D_KV_A = N_KV_A * HEAD_DIM
D_B = N_HEADS_B * HEAD_DIM
D_IN = D_A + 2 * D_KV_A + 3 * D_B
D_FF = 4 * D_MODEL
D_PLE = 256
GRID_W = 64
ROPE_THETA = 10000.0
ROPE_HALF = HEAD_DIM // 2
DILATED_PATTERNS = ((128, 1), (512, 4), (2048, 16))
BAND_HALF = 64
N_BUCKETS = 32
MAX_DISTANCE = 1024
EPS = 1e-6
NEG_BIG = -1e30
SCORE_SCALE = HEAD_DIM ** -0.5

ADAM_LR = 0.001
ADAM_B1 = 0.9
ADAM_B2 = 0.999
ADAM_EPS = 1e-08
ADAM_WD = 0.01
ADAM_STEP = 10

N_DEV = 8
V7X_VMEM_LIMIT = 56 * 1024 * 1024

OFF_QA, OFF_KA, OFF_VA, OFF_QB, OFF_KB, OFF_VB = 0, 512, 640, 768, 1280, 1792


def _params(n_axes, vmem=None):
    return pltpu.CompilerParams(dimension_semantics=("arbitrary",) * n_axes, vmem_limit_bytes=vmem)


def _dot(a, b):
    return jnp.dot(a, b, preferred_element_type=F32)


def _dot_nt(a, b):
    return lax.dot_general(a, b, (((1,), (1,)), ((), ())), preferred_element_type=F32)


def _dot_tn(a, b):
    return lax.dot_general(a, b, (((0,), (0,)), ((), ())), preferred_element_type=F32)


def _rms_fwd(x, g):
    r = lax.rsqrt(jnp.mean(x * x, axis=-1, keepdims=True) + EPS)
    return x * r * g, r


def _rms_bwd(x, g, dy):
    r = lax.rsqrt(jnp.mean(x * x, axis=-1, keepdims=True) + EPS)
    xh = x * r
    dxh = dy * g
    dx = r * (dxh - xh * jnp.mean(dxh * xh, axis=-1, keepdims=True))
    return dx, jnp.sum(dy * xh, axis=0, keepdims=True)


def _head_sum(v):
    head = lax.broadcasted_iota(jnp.int32, v.shape, 1) >> 6
    out = jnp.zeros_like(v)
    for h in range(v.shape[1] // HEAD_DIM):
        msk = head == h
        s = jnp.sum(jnp.where(msk, v, 0.0), axis=-1, keepdims=True)
        out = jnp.where(msk, s, out)
    return out


def _swap_halves(v):
    w = v.shape[1]
    lane = lax.broadcasted_iota(jnp.int32, v.shape, 1)
    first_half = (lane & (HEAD_DIM - 1)) < ROPE_HALF
    return jnp.where(first_half, pltpu.roll(v, w - ROPE_HALF, 1), pltpu.roll(v, ROPE_HALF, 1))


def _head_norm_rope(v, g, cos, sin_signed):
    r = lax.rsqrt(_head_sum(v * v) * (1.0 / HEAD_DIM) + EPS)
    y = v * r * g
    return y * cos + _swap_halves(y) * sin_signed


def _head_norm_rope_bwd(v, g, cos, sin_signed, dout):
    dy = dout * cos - _swap_halves(dout) * sin_signed
    r = lax.rsqrt(_head_sum(v * v) * (1.0 / HEAD_DIM) + EPS)
    xh = v * r
    dxh = dy * g
    dv = r * (dxh - xh * (_head_sum(dxh * xh) * (1.0 / HEAD_DIM)))
    return dv, jnp.sum(dy * xh, axis=0, keepdims=True)


def _row_spec(tm, n):
    return pl.BlockSpec((tm, n), lambda i: (i, 0))


def _full_spec(shape):
    nd = len(shape)
    return pl.BlockSpec(shape, lambda *_: (0,) * nd)


def _const_spec(shape):
    nd = len(shape)
    return pl.BlockSpec(shape, lambda *_: (0,) * nd, pipeline_mode=pl.Buffered(1))


def _zero_at_first(first, *refs):
    @pl.when(first)
    def _():
        for ref in refs:
            ref[...] = jnp.zeros_like(ref)


DILATIONS = tuple(d for _, d in DILATED_PATTERNS if d > 1)


def _dilation_perm(tm, dil):
    rows = np.arange(tm)
    perm = np.zeros((tm, tm), np.float32)
    perm[(rows % dil) * (tm // dil) + rows // dil, rows] = 1.0
    return jnp.asarray(perm, BF16)


def _store_dilated(ref, perm_ref, val, dil):
    per = val.shape[0] // dil
    sorted_rows = _dot(perm_ref[...], val).astype(BF16)
    for r in range(dil):
        ref[:, r * D_B:(r + 1) * D_B] = sorted_rows[r * per:(r + 1) * per, :]


def _dilated_specs(s_len, tm, dtype):
    specs = [pl.BlockSpec((tm // d, d * D_B), lambda i: (i, 0)) for d in DILATIONS]
    return specs, [SDS((s_len // d, d * D_B), dtype) for d in DILATIONS]


ONES_ROWS = 16


def _inproj_fwd(x, g_pre, w_in, cos, sin, gq, gk, tm, tk):
    s_len = x.shape[0]
    n_d = len(DILATIONS)

    def body(x_ref, g_ref, w_ref, cos_ref, sin_ref, gq_ref, gk_ref, *rest):
        perms, views = rest[:n_d], rest[n_d + 10:]
        qa_raw, ka_raw, qs, k_maj, kt, vt, qb, kb, vb, xn_out = rest[n_d:n_d + 10]
        xn, _ = _rms_fwd(x_ref[...], g_ref[...])
        xn = xn.astype(BF16)
        xn_out[...] = xn
        qa = _dot(xn, w_ref[:, OFF_QA:OFF_KA])
        qa_raw[...] = qa
        cos, sin = cos_ref[...], sin_ref[...]
        qs[...] = (_head_norm_rope(qa, gq_ref[...], jnp.tile(cos, (1, D_A // D_KV_A)), jnp.tile(sin, (1, D_A // D_KV_A)))
                   * SCORE_SCALE).astype(BF16)
        ka = _dot(xn, w_ref[:, OFF_KA:OFF_VA])
        ka_raw[...] = ka
        kn = _head_norm_rope(ka, gk_ref[...], cos, sin).astype(BF16)
        for kv in range(N_KV_A):
            k_maj[kv] = kn[:, HEAD_DIM * kv:HEAD_DIM * (kv + 1)]
        kt[...] = kn.T.reshape(N_KV_A, HEAD_DIM, tm)
        va = _dot(xn, w_ref[:, OFF_VA:OFF_QB]).astype(BF16)
        vt[:, :HEAD_DIM, :] = va.T.reshape(N_KV_A, HEAD_DIM, tm)
        vt[:, HEAD_DIM:, :] = jnp.ones((N_KV_A, ONES_ROWS, tm), BF16)
        mixer_b = ((_dot(xn, w_ref[:, OFF_QB:OFF_KB]) * SCORE_SCALE).astype(BF16),
                   _dot(xn, w_ref[:, OFF_KB:OFF_VB]).astype(BF16), _dot(xn, w_ref[:, OFF_VB:D_IN]).astype(BF16))
        for a, (val, plain) in enumerate(zip(mixer_b, (qb, kb, vb))):
            plain[...] = val
            for j, d in enumerate(DILATIONS):
                _store_dilated(views[a * n_d + j], perms[j], val, d)

    view_specs, view_shapes = _dilated_specs(s_len, tm, BF16)
    per_chunk = tk // tm

    def chunk_t(rows):
        return pl.BlockSpec((N_KV_A, None, rows, tm), lambda i: (0, i // per_chunk, 0, i % per_chunk))

    return pl.pallas_call(
        body, name="inproj_fwd", grid=(s_len // tm,),
        in_specs=[_row_spec(tm, D_MODEL), _full_spec((1, D_MODEL)), _const_spec((D_MODEL, D_IN)),
                  _row_spec(tm, D_KV_A), _row_spec(tm, D_KV_A), _full_spec((1, D_A)), _full_spec((1, D_KV_A))]
                 + [_full_spec((tm, tm))] * n_d,
        out_specs=[_row_spec(tm, D_A), _row_spec(tm, D_KV_A), _row_spec(tm, D_A),
                   pl.BlockSpec((N_KV_A, tm, HEAD_DIM), lambda i: (0, i, 0)), chunk_t(HEAD_DIM),
                   chunk_t(HEAD_DIM + ONES_ROWS), _row_spec(tm, D_B), _row_spec(tm, D_B), _row_spec(tm, D_B),
                   _row_spec(tm, D_MODEL)] + view_specs * 3,
        out_shape=[SDS((s_len, D_A), F32), SDS((s_len, D_KV_A), F32), SDS((s_len, D_A), BF16),
                   SDS((N_KV_A, s_len, HEAD_DIM), BF16), SDS((N_KV_A, s_len // tk, HEAD_DIM, tk), BF16),
                   SDS((N_KV_A, s_len // tk, HEAD_DIM + ONES_ROWS, tk), BF16), SDS((s_len, D_B), BF16),
                   SDS((s_len, D_B), BF16), SDS((s_len, D_B), BF16), SDS((s_len, D_MODEL), BF16)] + view_shapes * 3,
        compiler_params=_params(1, V7X_VMEM_LIMIT),
    )(x, g_pre, w_in, cos, sin, gq, gk, *[_dilation_perm(tm, d) for d in DILATIONS])


def _stack_heads(ref, tq):
    return jnp.concatenate([ref[:, HEAD_DIM * g:HEAD_DIM * (g + 1)] for g in range(GROUP_A)], axis=0)


def _stack_cols(ref, tq):
    return jnp.concatenate([ref[:, HEAD_DIM * g:HEAD_DIM * g + 1] for g in range(GROUP_A)], axis=0)


def _attn_fwd(qs, k, vt, tq, tk):
    s_len = qs.shape[0]
    nk = s_len // tk
    assert nk % 2 == 0
    gw = GROUP_A * HEAD_DIM
    rows = GROUP_A * tq
    vrows = vt.shape[2]

    def body(q_ref, k_ref, vt_ref, o_ref, lse_ref, s_buf):
        qt = _stack_heads(q_ref, tq).T

        def scores(j, slot):
            kj = k_ref[pl.ds(pl.multiple_of(j * tk, tk), tk), :]
            s_buf[slot] = _dot(kj, qt)

        def consume(j, slot, carry):
            m, acc = carry
            st = s_buf[slot]
            m_new = jnp.maximum(m, jnp.max(st, axis=0, keepdims=True))
            pt = jnp.exp(st - m_new)
            acc = jnp.exp(m - m_new) * acc + _dot(vt_ref[j], pt.astype(BF16))
            return m_new, acc

        scores(0, 0)

        def pair(j, carry, more):
            scores(j + 1, 1)
            carry = consume(j, 0, carry)
            if more:
                scores(j + 2, 0)
            return consume(j + 1, 1, carry)

        carry = (jnp.full((1, rows), NEG_BIG, F32), jnp.zeros((vrows, rows), F32))
        carry = lax.fori_loop(0, nk // 2 - 1, lambda jj, c: pair(2 * jj, c, True), carry)
        m, acc = pair(nk - 2, carry, False)
        l = acc[HEAD_DIM:HEAD_DIM + 1]
        o = (acc[:HEAD_DIM] / l).T
        lse = jnp.broadcast_to(m + jnp.log(l), (HEAD_DIM, rows)).T
        for g in range(GROUP_A):
            o_ref[:, HEAD_DIM * g:HEAD_DIM * (g + 1)] = o[g * tq:(g + 1) * tq]
            lse_ref[:, HEAD_DIM * g:HEAD_DIM * (g + 1)] = lse[g * tq:(g + 1) * tq]

    tile = pl.BlockSpec((tq, gw), lambda kv, i: (i, kv))
    return pl.pallas_call(
        body, name="attn_fwd", grid=(N_KV_A, s_len // tq),
        in_specs=[tile, pl.BlockSpec((None, s_len, HEAD_DIM), lambda kv, i: (kv, 0, 0)),
                  pl.BlockSpec((None, nk, vrows, tk), lambda kv, i: (kv, 0, 0, 0))],
        out_specs=[tile, tile],
        out_shape=(SDS((s_len, D_A), F32), SDS((s_len, D_A), F32)),
        scratch_shapes=[pltpu.VMEM((2, tk, rows), F32)],
        compiler_params=_params(2, V7X_VMEM_LIMIT),
    )(qs, k, vt)


def _attn_bwd(qs, do, lse, delta, kt, k, vt, tq, tk):
    s_len = qs.shape[0]
    nk = s_len // tk
    assert nk % 2 == 0
    nq = s_len // tq
    gw = GROUP_A * HEAD_DIM
    rows = GROUP_A * tq

    def body(q_ref, do_ref, lse_ref, delta_ref, kt_ref, k_ref, vt_ref, dq_ref, dk_hbm, dv_hbm,
             dk_acc, dv_acc, s_buf, dp_buf):
        kv = pl.program_id(0)
        i = pl.program_id(1)

        @pl.when(i == 0)
        def _():
            dk_acc[...] = jnp.zeros_like(dk_acc)
            dv_acc[...] = jnp.zeros_like(dv_acc)

        q = _stack_heads(q_ref, tq)
        dout = _stack_heads(do_ref, tq)
        qt = q.T
        doutt = dout.T
        row_lse = _stack_cols(lse_ref, tq)
        row_delta = _stack_cols(delta_ref, tq)

        def scores(j, slot):
            s_buf[slot] = _dot(q, kt_ref[j])
            dp_buf[slot] = _dot(dout, vt_ref[j, :HEAD_DIM, :])

        def consume(j, slot, dq):
            p = jnp.exp(s_buf[slot] - row_lse)
            ds = (p * (dp_buf[slot] - row_delta)).astype(BF16)
            dv_acc[j] += _dot(doutt, p.astype(BF16))
            dk_acc[j] += _dot(qt, ds)
            return dq + _dot(ds, k_ref[pl.ds(pl.multiple_of(j * tk, tk), tk), :])

        scores(0, 0)

        def pair(j, dq, more):
            scores(j + 1, 1)
            dq = consume(j, 0, dq)
            if more:
                scores(j + 2, 0)
            return consume(j + 1, 1, dq)

        dq = lax.fori_loop(0, nk // 2 - 1, lambda jj, c: pair(2 * jj, c, True), jnp.zeros((rows, HEAD_DIM), F32))
        dq = pair(nk - 2, dq, False) * SCORE_SCALE
        for g in range(GROUP_A):
            dq_ref[:, HEAD_DIM * g:HEAD_DIM * (g + 1)] = dq[g * tq:(g + 1) * tq]

        @pl.when(i == nq - 1)
        def _():
            pltpu.sync_copy(dk_acc, dk_hbm.at[kv])
            pltpu.sync_copy(dv_acc, dv_hbm.at[kv])

    tile = pl.BlockSpec((tq, gw), lambda kv, i: (i, kv))
    chunks = pl.BlockSpec((None, nk, HEAD_DIM, tk), lambda kv, i: (kv, 0, 0, 0))
    grad_t = SDS((N_KV_A, nk, HEAD_DIM, tk), F32)
    return pl.pallas_call(
        body, name="attn_bwd", grid=(N_KV_A, nq),
        in_specs=[tile, tile, tile, tile, chunks,
                  pl.BlockSpec((None, s_len, HEAD_DIM), lambda kv, i: (kv, 0, 0)),
                  pl.BlockSpec((None, nk, vt.shape[2], tk), lambda kv, i: (kv, 0, 0, 0))],
        out_specs=[tile, pl.BlockSpec(memory_space=pl.ANY), pl.BlockSpec(memory_space=pl.ANY)],
        out_shape=(SDS((s_len, D_A), F32), grad_t, grad_t),
        scratch_shapes=[pltpu.VMEM((nk, HEAD_DIM, tk), F32), pltpu.VMEM((nk, HEAD_DIM, tk), F32),
                        pltpu.VMEM((2, rows, tk), F32), pltpu.VMEM((2, rows, tk), F32)],
        compiler_params=_params(2, V7X_VMEM_LIMIT),
    )(qs, do, lse, delta, kt, k, vt)


STAT_W = 128


def _band_specs(length, t, width=D_B):
    hb = t // BAND_HALF
    last = length // BAND_HALF - 1
    main = pl.BlockSpec((t, width), lambda r, i: (i, r))
    prev = pl.BlockSpec((BAND_HALF, width), lambda r, i: (jnp.maximum(i * hb - 1, 0), r))
    nxt = pl.BlockSpec((BAND_HALF, width), lambda r, i: (jnp.minimum((i + 1) * hb, last), r))
    return main, [prev, main, nxt]


def _pack_heads(cols):
    lane = lax.broadcasted_iota(jnp.int32, (cols[0].shape[0], STAT_W), 1)
    out = jnp.zeros((cols[0].shape[0], STAT_W), F32)
    for h, c in enumerate(cols):
        out = jnp.where(lane == h, c, out)
    return out


def _spread_heads(stat):
    head = lax.broadcasted_iota(jnp.int32, (stat.shape[0], D_B), 1) >> 6
    out = jnp.zeros((stat.shape[0], D_B), F32)
    for h in range(N_HEADS_B):
        out = jnp.where(head == h, stat[:, h:h + 1], out)
    return out


def _window(refs):
    return jnp.concatenate([r[...] for r in refs], axis=0)


def _head(v, h):
    return v[:, HEAD_DIM * h:HEAD_DIM * (h + 1)]


def _call_with_rider(body, name, grid, in_specs, out_specs, out_shape, scratch_shapes, args, rider):
    n_in, n_out = len(in_specs), len(out_specs)
    if rider is None:
        res = pl.pallas_call(body, name=name, grid=grid, in_specs=in_specs, out_specs=out_specs, out_shape=out_shape,
                             scratch_shapes=scratch_shapes, compiler_params=_params(len(grid), V7X_VMEM_LIMIT))(*args)
        return res, []
    kind, arrays = rider
    n = len(arrays)

    def with_rider(*refs):
        ins, c_ins = refs[:n_in], refs[n_in:n_in + n]
        outs, c_outs = refs[n_in + n:n_in + n + n_out], refs[n_in + n + n_out:n_in + 2 * n + n_out]
        rest = refs[n_in + 2 * n + n_out:]
        scratch, sems = rest[:-3], rest[-3:]
        ids = [pl.program_id(a) for a in range(len(grid))]
        first = functools.reduce(jnp.logical_and, [i == 0 for i in ids])
        last = functools.reduce(jnp.logical_and, [i == g - 1 for i, g in zip(ids, grid)])

        @pl.when(first)
        def _():
            _comm_start(kind, c_ins, c_outs, sems)

        body(*ins, *outs, *scratch)

        @pl.when(last)
        def _():
            _comm_wait(kind, c_ins, c_outs, sems)

    res = pl.pallas_call(
        with_rider, name=name, grid=grid, in_specs=list(in_specs) + [ANY_SPEC] * n,
        out_specs=list(out_specs) + [ANY_SPEC] * n, out_shape=list(out_shape) + _comm_out_shapes(kind, arrays),
        scratch_shapes=list(scratch_shapes) + _comm_sems(n), compiler_params=_params(len(grid), V7X_VMEM_LIMIT),
    )(*args, *arrays)
    return res[:n_out], res[n_out:]


def _band_fwd(q, k, v, bias, dil, t, rider=None):
    s_len = q.size // D_B
    length = s_len // dil
    w = t + 2 * BAND_HALF
    view = lambda a: a.reshape(length, dil * D_B)
    main, win = _band_specs(length, t)
    stat, _ = _band_specs(length, t, STAT_W)

    def body(q_ref, k0, k1, k2, v0, v1, v2, bias_ref, o_ref, lse_ref, s_buf, kt_buf):
        lses = []
        i = pl.program_id(1)
        kt_buf[...] = _window((k0, k1, k2)).T
        vw = _window((v0, v1, v2))
        pos = i * t - BAND_HALF + lax.broadcasted_iota(jnp.int32, (1, w), 1)
        valid = (pos >= 0) & (pos < length)
        for h in range(N_HEADS_B):
            s_buf[h] = _dot(q_ref[:, HEAD_DIM * h:HEAD_DIM * (h + 1)], kt_buf[HEAD_DIM * h:HEAD_DIM * (h + 1), :])
        for h in range(N_HEADS_B):
            s = jnp.where(valid, s_buf[h] + bias_ref[h], NEG_BIG)
            m = jnp.max(s, axis=-1, keepdims=True)
            e = jnp.exp(s - m)
            den = jnp.sum(e, axis=-1, keepdims=True)
            o_ref[:, HEAD_DIM * h:HEAD_DIM * (h + 1)] = _dot(e.astype(BF16), _head(vw, h)) / den
            lses.append(m + jnp.log(den))
        lse_ref[...] = _pack_heads(lses)

    (o, lse), rode = _call_with_rider(
        body, f"band_fwd_d{dil}", (dil, length // t),
        [main] + win + win + [_full_spec((N_HEADS_B, t, w))], [main, stat],
        [SDS((length, dil * D_B), F32), SDS((length, dil * STAT_W), F32)],
        [pltpu.VMEM((N_HEADS_B, t, w), F32), pltpu.VMEM((D_B, w), BF16)],
        (view(q), view(k), view(k), view(k), view(v), view(v), view(v), bias), rider)
    return o.reshape(s_len, D_B), lse.reshape(s_len, STAT_W), rode


def _band_dq(q, do, lse, delta, k, v, bias, dil, t, rider=None):
    s_len = q.size // D_B
    length = s_len // dil
    w = t + 2 * BAND_HALF
    view = lambda a: a.reshape(length, dil * D_B)
    main, win = _band_specs(length, t)

    def body(q_ref, do_ref, lse_ref, delta_ref, k0, k1, k2, v0, v1, v2, bias_ref, dq_ref, dbias_ref,
             s_buf, dp_buf, kt_buf, vt_buf, dq_buf):
        i = pl.program_id(1)
        @pl.when((pl.program_id(0) == 0) & (i == 0))
        def _():
            dbias_ref[...] = jnp.zeros_like(dbias_ref)

        kw = _window((k0, k1, k2))
        kt_buf[...] = kw.T
        vt_buf[...] = _window((v0, v1, v2)).T
        pos = i * t - BAND_HALF + lax.broadcasted_iota(jnp.int32, (1, w), 1)
        valid = (pos >= 0) & (pos < length)
        for h in range(N_HEADS_B):
            cols = slice(HEAD_DIM * h, HEAD_DIM * (h + 1))
            s_buf[h] = _dot(q_ref[:, cols], kt_buf[cols, :])
            dp_buf[h] = _dot(do_ref[:, cols], vt_buf[cols, :])
        for h in range(N_HEADS_B):
            cols = slice(HEAD_DIM * h, HEAD_DIM * (h + 1))
            s = jnp.where(valid, s_buf[h] + bias_ref[h], NEG_BIG)
            p = jnp.exp(s - lse_ref[:, h:h + 1])
            ds = p * (dp_buf[h] - delta_ref[:, h:h + 1])
            dq_buf[:, cols] = _dot(ds.astype(BF16), _head(kw, h)) * SCORE_SCALE
            dbias_ref[h] += ds
        dq_ref[...] = dq_buf[...].astype(BF16)

    stat, _ = _band_specs(length, t, STAT_W)
    sview = lambda a: a.reshape(length, dil * STAT_W)
    (dq, dbias), rode = _call_with_rider(
        body, f"band_dq_d{dil}", (dil, length // t),
        [main, main, stat, stat] + win + win + [_full_spec((N_HEADS_B, t, w))],
        [main, _full_spec((N_HEADS_B, t, w))],
        [SDS((length, dil * D_B), BF16), SDS((N_HEADS_B, t, w), F32)],
        [pltpu.VMEM((N_HEADS_B, t, w), F32), pltpu.VMEM((N_HEADS_B, t, w), F32),
         pltpu.VMEM((D_B, w), BF16), pltpu.VMEM((D_B, w), BF16), pltpu.VMEM((t, D_B), F32)],
        (view(q), view(do), sview(lse), sview(delta), view(k), view(k), view(k), view(v), view(v), view(v), bias),
        rider)
    return dq, dbias, rode


def _band_dkv(k, v, q, do, lse, delta, bias_t, dil, t):
    s_len = q.size // D_B
    length = s_len // dil
    w = t + 2 * BAND_HALF
    view = lambda a: a.reshape(length, dil * D_B)
    main, win = _band_specs(length, t)

    def body(k_ref, v_ref, q0, q1, q2, d0, d1, d2, l0, l1, l2, e0, e1, e2, bias_ref, dk_ref, dv_ref,
             s_buf, dp_buf, kt_buf, vt_buf):
        i = pl.program_id(1)
        qw = _window((q0, q1, q2))
        dow = _window((d0, d1, d2))
        lsew = _window((l0, l1, l2))
        deltaw = _window((e0, e1, e2))
        pos = i * t - BAND_HALF + lax.broadcasted_iota(jnp.int32, (w, 1), 0)
        valid = (pos >= 0) & (pos < length)
        kt_buf[...] = k_ref[...].T
        vt_buf[...] = v_ref[...].T
        for h in range(N_HEADS_B):
            cols = slice(HEAD_DIM * h, HEAD_DIM * (h + 1))
            s_buf[h] = _dot(_head(qw, h), kt_buf[cols, :])
            dp_buf[h] = _dot(_head(dow, h), vt_buf[cols, :])
        qwt = qw.T
        dowt = dow.T
        dkt, dvt = [], []
        for h in range(N_HEADS_B):
            rows = slice(HEAD_DIM * h, HEAD_DIM * (h + 1))
            s = jnp.where(valid, s_buf[h] + bias_ref[h], NEG_BIG)
            p = jnp.exp(s - lsew[:, h:h + 1])
            ds = p * (dp_buf[h] - deltaw[:, h:h + 1])
            dvt.append(_dot(dowt[rows, :], p.astype(BF16)))
            dkt.append(_dot(qwt[rows, :], ds.astype(BF16)))
        dv_ref[...] = jnp.concatenate(dvt, axis=0).T.astype(BF16)
        dk_ref[...] = jnp.concatenate(dkt, axis=0).T.astype(BF16)

    _, swin = _band_specs(length, t, STAT_W)
    sview = lambda a: a.reshape(length, dil * STAT_W)
    dk, dv = pl.pallas_call(
        body, name=f"band_dkv_d{dil}", grid=(dil, length // t),
        in_specs=[main, main] + win + win + swin + swin + [_full_spec((N_HEADS_B, w, t))],
        out_specs=[main, main],
        out_shape=(SDS((length, dil * D_B), BF16), SDS((length, dil * D_B), BF16)),
        scratch_shapes=[pltpu.VMEM((N_HEADS_B, w, t), F32), pltpu.VMEM((N_HEADS_B, w, t), F32),
                        pltpu.VMEM((D_B, t), BF16), pltpu.VMEM((D_B, t), BF16)],
        compiler_params=_params(2, V7X_VMEM_LIMIT),
    )(view(k), view(v), view(q), view(q), view(q), view(do), view(do), view(do), sview(lse), sview(lse), sview(lse),
      sview(delta), sview(delta), sview(delta), bias_t)
    return dk, dv


def _t5_bucket_np(rel):
    nb = N_BUCKETS // 2
    max_exact = nb // 2
    side = np.where(rel > 0, nb, 0)
    n = np.abs(rel)
    ratio = np.maximum(n, max_exact).astype(np.float32) / np.float32(max_exact)
    large = max_exact + (np.log(ratio) / np.float32(math.log(MAX_DISTANCE / max_exact))
                         * np.float32(nb - max_exact)).astype(np.int32)
    large = np.minimum(large, nb - 1)
    return (side + np.where(n < max_exact, n, large)).astype(np.int32)


def _band_buckets(dil, t):
    rel = np.arange(t + 2 * BAND_HALF)[None, :] - BAND_HALF - np.arange(t)[:, None]
    bucket = _t5_bucket_np(np.clip(rel, -BAND_HALF, BAND_HALF) * dil)
    return np.where(np.abs(rel) <= BAND_HALF, bucket, -1).astype(np.int32)


def _toeplitz(vals, rows, cols):
    heads = vals.shape[0]
    period = rows + cols
    vec = jnp.concatenate([vals[:, rows - 1:], jnp.zeros((heads, 1), vals.dtype), vals[:, :rows - 1]], axis=1)
    flat = jnp.broadcast_to(vec[:, None, :], (heads, rows, period)).reshape(heads, rows * period)
    return flat[:, :rows * (period - 1)].reshape(heads, rows, period - 1)[:, :, :cols]


def _bias_tiles(rel_bias, dil, t):
    w = t + 2 * BAND_HALF
    rel = np.arange(-BAND_HALF, BAND_HALF + 1)
    bucket = _t5_bucket_np(rel * dil)
    runs, start = [], 0
    for i in range(1, len(bucket) + 1):
        if i == len(bucket) or bucket[i] != bucket[start]:
            b = int(bucket[start])
            runs.append(jnp.broadcast_to(rel_bias[b:b + 1], (i - start, N_HEADS_B)))
            start = i
    per_rel = jnp.concatenate(runs, axis=0).T

    def diagonals(lo, hi):
        left = jnp.full((N_HEADS_B, max(0, -BAND_HALF - lo)), NEG_BIG, F32)
        right = jnp.full((N_HEADS_B, max(0, hi - BAND_HALF)), NEG_BIG, F32)
        return jnp.concatenate([left, per_rel, right], axis=1)

    tile = _toeplitz(diagonals(-(t - 1) - BAND_HALF, w - 1 - BAND_HALF), t, w)
    twin = _toeplitz(diagonals(-(w - 1) + BAND_HALF, t - 1 + BAND_HALF), w, t)
    return tile, twin


def _dbias_reduce(dbias, buckets):
    n = len(dbias)

    def body(*refs):
        db_refs, bk_refs, o_ref = refs[:n], refs[n:2 * n], refs[2 * n]
        row = lax.broadcasted_iota(jnp.int32, (N_BUCKETS, 128), 0)
        lane = lax.broadcasted_iota(jnp.int32, (N_BUCKETS, 128), 1)

        def per_bucket(b, out):
            for pat in range(n):
                msk = bk_refs[pat][...] == b
                for h in range(N_HEADS_B):
                    tot = jnp.sum(jnp.where(msk, db_refs[pat][h], 0.0), axis=-1, keepdims=True)
                    tot = jnp.sum(tot, axis=0, keepdims=True)
                    out = out + jnp.where((row == b) & (lane == h), tot, 0.0)
            return out

        o_ref[...] = lax.fori_loop(0, N_BUCKETS, per_bucket, jnp.zeros((N_BUCKETS, 128), F32))

    return pl.pallas_call(
        body, name="dbias_reduce", out_shape=SDS((N_BUCKETS, 128), F32),
        compiler_params=pltpu.CompilerParams(vmem_limit_bytes=V7X_VMEM_LIMIT),
    )(*dbias, *buckets)


def _attn_out_fwd(ya, ob, lb, x, g_a, g_b, w_out, g_post, tm):
    s_len = ya.shape[0]

    def body(ya_ref, o0, o1, o2, l0, l1, l2, x_ref, ga_ref, gb_ref, w_ref, gp_ref,
             h1_ref, yo_ref, yb_ref, lse_ref, ycat_ref):
        m = jnp.maximum(jnp.maximum(l0[...], l1[...]), l2[...])
        w0, w1, w2 = jnp.exp(l0[...] - m), jnp.exp(l1[...] - m), jnp.exp(l2[...] - m)
        wsum = w0 + w1 + w2
        yb = (_spread_heads(w0 / wsum) * o0[...] + _spread_heads(w1 / wsum) * o1[...]
              + _spread_heads(w2 / wsum) * o2[...])
        yb_ref[...] = yb
        lse_ref[...] = m + jnp.log(wsum)
        yan, _ = _rms_fwd(ya_ref[...], ga_ref[...])
        ybn, _ = _rms_fwd(yb, gb_ref[...])
        yan, ybn = yan.astype(BF16), ybn.astype(BF16)
        ycat_ref[:, :D_A] = yan
        ycat_ref[:, D_A:] = ybn
        yo = _dot(yan, w_ref[:D_A, :]) + _dot(ybn, w_ref[D_A:, :])
        yo_ref[...] = yo
        post, _ = _rms_fwd(yo, gp_ref[...])
        h1_ref[...] = x_ref[...] + post

    half, full, stat = _row_spec(tm, D_A), _row_spec(tm, D_MODEL), _row_spec(tm, STAT_W)
    return pl.pallas_call(
        body, name="attn_out_fwd", grid=(s_len // tm,),
        in_specs=[half] * 4 + [stat] * 3 + [full, _full_spec((1, D_A)), _full_spec((1, D_B)),
                                            _full_spec((D_MODEL, D_MODEL)), _full_spec((1, D_MODEL))],
        out_specs=[full, full, half, stat, full],
        out_shape=(SDS((s_len, D_MODEL), F32), SDS((s_len, D_MODEL), F32), SDS((s_len, D_B), F32),
                   SDS((s_len, STAT_W), F32), SDS((s_len, D_MODEL), BF16)),
        compiler_params=_params(1, V7X_VMEM_LIMIT),
    )(ya, ob[0], ob[1], ob[2], lb[0], lb[1], lb[2], x, g_a, g_b, w_out, g_post)


def _attn_out_bwd(dh1, yo, ya, yb, g_a, g_b, w_out, g_post, tm):
    s_len = ya.shape[0]
    n_d = len(DILATIONS)

    def body(dh1_ref, yo_ref, ya_ref, yb_ref, ga_ref, gb_ref, w_ref, gp_ref, *rest):
        perms, views = rest[:n_d], rest[n_d + 8:]
        dyo_ref, dya_ref, dyb_ref, dela_ref, delb_ref, dgp_ref, dga_ref, dgb_ref = rest[n_d:n_d + 8]
        _zero_at_first(pl.program_id(0) == 0, dgp_ref, dga_ref, dgb_ref)
        dyo, dgp = _rms_bwd(yo_ref[...], gp_ref[...], dh1_ref[...])
        dyo = dyo.astype(BF16)
        dyo_ref[...] = dyo
        dgp_ref[...] += dgp
        dya_n = _dot_nt(dyo, w_ref[:D_A, :])
        dyb_n = _dot_nt(dyo, w_ref[D_A:, :])
        ya, yb = ya_ref[...], yb_ref[...]
        dya, dga = _rms_bwd(ya, ga_ref[...], dya_n)
        dyb, dgb = _rms_bwd(yb, gb_ref[...], dyb_n)
        dga_ref[...] += dga
        dgb_ref[...] += dgb
        dya_ref[...] = dya.astype(BF16)
        dyb_ref[...] = dyb.astype(BF16)
        for j, d in enumerate(DILATIONS):
            _store_dilated(views[j], perms[j], dyb.astype(BF16), d)
        dela_ref[...] = _head_sum(dya * ya)
        prod = dyb * yb
        head = lax.broadcasted_iota(jnp.int32, prod.shape, 1) >> 6
        delb_ref[...] = _pack_heads([jnp.sum(jnp.where(head == h, prod, 0.0), axis=-1, keepdims=True)
                                     for h in range(N_HEADS_B)])

    half, full = _row_spec(tm, D_A), _row_spec(tm, D_MODEL)
    view_specs, view_shapes = _dilated_specs(s_len, tm, BF16)
    return pl.pallas_call(
        body, name="attn_out_bwd", grid=(s_len // tm,),
        in_specs=[full, full, half, half, _full_spec((1, D_A)), _full_spec((1, D_B)),
                  _full_spec((D_MODEL, D_MODEL)), _full_spec((1, D_MODEL))] + [_full_spec((tm, tm))] * n_d,
        out_specs=[full, half, half, half, _row_spec(tm, STAT_W), _full_spec((1, D_MODEL)), _full_spec((1, D_A)),
                   _full_spec((1, D_B))] + view_specs,
        out_shape=[SDS((s_len, D_MODEL), BF16), SDS((s_len, D_A), BF16), SDS((s_len, D_B), BF16),
                   SDS((s_len, D_A), F32), SDS((s_len, STAT_W), F32), SDS((1, D_MODEL), F32), SDS((1, D_A), F32),
                   SDS((1, D_B), F32)] + view_shapes,
        compiler_params=_params(1, V7X_VMEM_LIMIT),
    )(dh1, yo, ya, yb, g_a, g_b, w_out, g_post, *[_dilation_perm(tm, d) for d in DILATIONS])


FF_CHUNK = 1024


def _mlp_fwd(h1, g_pre, w1, w2, g_post, tm):
    s_len = h1.shape[0]

    def body(h1_ref, gpre_ref, w1_ref, w2_ref, gpost_ref, h2_ref, fo_ref, xn_ref):
        h1v = h1_ref[...]
        xn, _ = _rms_fwd(h1v, gpre_ref[...])
        xn = xn.astype(BF16)
        xn_ref[...] = xn
        fo = jnp.zeros((tm, D_MODEL), F32)
        for c in range(D_FF // FF_CHUNK):
            cols = slice(c * FF_CHUNK, (c + 1) * FF_CHUNK)
            u = jnp.maximum(_dot(xn, w1_ref[:, cols]), 0.0)
            fo = fo + _dot((u * u).astype(BF16), w2_ref[cols, :])
        fo_ref[...] = fo
        post, _ = _rms_fwd(fo, gpost_ref[...])
        h2_ref[...] = h1v + post

    full = _row_spec(tm, D_MODEL)
    return pl.pallas_call(
        body, name="mlp_fwd", grid=(s_len // tm,),
        in_specs=[full, _full_spec((1, D_MODEL)), _const_spec((D_MODEL, D_FF)), _const_spec((D_FF, D_MODEL)),
                  _full_spec((1, D_MODEL))],
        out_specs=[full, full, full],
        out_shape=(SDS((s_len, D_MODEL), F32), SDS((s_len, D_MODEL), F32), SDS((s_len, D_MODEL), BF16)),
        compiler_params=_params(1, V7X_VMEM_LIMIT),
    )(h1, g_pre, w1, w2, g_post)


def _mlp_bwd(h1, dh2, fo, xn, g_pre, w1, w2, g_post, tm):
    s_len = h1.shape[0]

    def body(h1_ref, dh2_ref, fo_ref, xn_ref, gpre_ref, w1_ref, w2_ref, gpost_ref,
             dh1_ref, dfo_ref, du_ref, f_ref, dgpost_ref, dgpre_ref):
        _zero_at_first(pl.program_id(0) == 0, dgpost_ref, dgpre_ref)
        dh2 = dh2_ref[...]
        dfo, dgpost = _rms_bwd(fo_ref[...], gpost_ref[...], dh2)
        dfo = dfo.astype(BF16)
        dfo_ref[...] = dfo
        dgpost_ref[...] += dgpost
        xn = xn_ref[...]
        dxn = jnp.zeros((tm, D_MODEL), F32)
        for c in range(D_FF // FF_CHUNK):
            cols = slice(c * FF_CHUNK, (c + 1) * FF_CHUNK)
            u = jnp.maximum(_dot(xn, w1_ref[:, cols]), 0.0)
            f_ref[:, cols] = (u * u).astype(BF16)
            du = (_dot_nt(dfo, w2_ref[cols, :]) * (2.0 * u)).astype(BF16)
            du_ref[:, cols] = du
            dxn = dxn + _dot_nt(du, w1_ref[:, cols])
        dx, dgpre = _rms_bwd(h1_ref[...], gpre_ref[...], dxn)
        dgpre_ref[...] += dgpre
        dh1_ref[...] = dh2 + dx

    full, wide = _row_spec(tm, D_MODEL), _row_spec(tm, D_FF)
    return pl.pallas_call(
        body, name="mlp_bwd", grid=(s_len // tm,),
        in_specs=[full, full, full, full, _full_spec((1, D_MODEL)), _const_spec((D_MODEL, D_FF)),
                  _const_spec((D_FF, D_MODEL)), _full_spec((1, D_MODEL))],
        out_specs=[full, full, wide, wide, _full_spec((1, D_MODEL)), _full_spec((1, D_MODEL))],
        out_shape=(SDS((s_len, D_MODEL), F32), SDS((s_len, D_MODEL), BF16), SDS((s_len, D_FF), BF16),
                   SDS((s_len, D_FF), BF16), SDS((1, D_MODEL), F32), SDS((1, D_MODEL), F32)),
        compiler_params=_params(1, V7X_VMEM_LIMIT),
    )(h1, dh2, fo, xn, g_pre, w1, w2, g_post)


def _ple_fwd_bwd(h2, p, target, g_ple, w_gate, w_proj, tm):
    s_len = h2.shape[0]

    def body(h2_ref, p_ref, t_ref, g_ref, wg_ref, wp_ref, dh2_ref, loss_ref, dg_ref, xn_ref, dgl_ref, dpp_ref, pb_ref):
        _zero_at_first(pl.program_id(0) == 0, loss_ref, dg_ref)
        h2 = h2_ref[...]
        g = g_ref[...]
        xn, _ = _rms_fwd(h2, g)
        xn = xn.astype(BF16)
        xn_ref[...] = xn
        gate = 1.0 / (1.0 + jnp.exp(-_dot(xn, wg_ref[...])))
        pb = p_ref[...].astype(BF16)
        pb_ref[...] = pb
        pp = _dot(pb, wp_ref[...])
        diff = h2 + gate * pp - t_ref[...]
        loss_ref[...] += jnp.full((8, 128), jnp.sum(diff * diff), F32)
        dh3 = diff * (1.0 / D_MODEL)
        dpp_ref[...] = (dh3 * gate).astype(BF16)
        dgl = (dh3 * pp * gate * (1.0 - gate)).astype(BF16)
        dgl_ref[...] = dgl
        dx, dg = _rms_bwd(h2, g, _dot_nt(dgl, wg_ref[...]))
        dg_ref[...] += dg
        dh2_ref[...] = dh3 + dx

    full = _row_spec(tm, D_MODEL)
    return pl.pallas_call(
        body, name="ple_fwd_bwd", grid=(s_len // tm,),
        in_specs=[full, _row_spec(tm, D_PLE), full, _full_spec((1, D_MODEL)), _full_spec((D_MODEL, D_MODEL)),
                  _full_spec((D_PLE, D_MODEL))],
        out_specs=[full, _full_spec((8, 128)), _full_spec((1, D_MODEL)), full, full, full, _row_spec(tm, D_PLE)],
        out_shape=(SDS((s_len, D_MODEL), F32), SDS((8, 128), F32), SDS((1, D_MODEL), F32), SDS((s_len, D_MODEL), BF16),
                   SDS((s_len, D_MODEL), BF16), SDS((s_len, D_MODEL), BF16), SDS((s_len, D_PLE), BF16)),
        compiler_params=_params(1, V7X_VMEM_LIMIT),
    )(h2, p, target, g_ple, w_gate, w_proj)


def _inproj_bwd(dqs, dk_t, dv_t, parts_b, qa_raw, ka_raw, x, dh1, g_pre, w_in, cos, sin, gq, gk, tm):
    s_len = x.shape[0]
    tk = dk_t.shape[3]

    n_d = len(DILATIONS)
    n_parts = 3 * (1 + n_d)

    def body(dqs_ref, dkt_ref, dvt_ref, *rest):
        parts, rest = rest[:n_parts], rest[n_parts:]
        (qa_ref, ka_ref, x_ref, dh1_ref, g_ref, w_ref, cos_ref, sin_ref, gq_ref, gk_ref), rest = rest[:10], rest[10:]
        unsort, (dx_ref, dproj_ref, dg_ref, dgq_ref, dgk_ref) = rest[:n_d], rest[n_d:]

        def mixer_b_grad(a):
            own = parts[a * (1 + n_d):(a + 1) * (1 + n_d)]
            total = own[0][...].astype(F32)
            for j, d in enumerate(DILATIONS):
                sorted_rows = jnp.concatenate([own[1 + j][:, r * D_B:(r + 1) * D_B] for r in range(d)], axis=0)
                total = total + _dot(unsort[j][...], sorted_rows)
            return total.astype(BF16)

        _zero_at_first(pl.program_id(0) == 0, dg_ref, dgq_ref, dgk_ref)
        cos, sin = cos_ref[...], sin_ref[...]
        dkn = dkt_ref[...].reshape(D_KV_A, tm).T
        dva = dvt_ref[...].reshape(D_KV_A, tm).T
        dqa, dgq = _head_norm_rope_bwd(qa_ref[...], gq_ref[...], jnp.tile(cos, (1, D_A // D_KV_A)),
                                       jnp.tile(sin, (1, D_A // D_KV_A)), dqs_ref[...])
        dka, dgk = _head_norm_rope_bwd(ka_ref[...], gk_ref[...], cos, sin, dkn)
        dgq_ref[...] += dgq
        dgk_ref[...] += dgk
        dproj_ref[:, OFF_QA:OFF_KA] = dqa.astype(BF16)
        dproj_ref[:, OFF_KA:OFF_VA] = dka.astype(BF16)
        dproj_ref[:, OFF_VA:OFF_QB] = dva.astype(BF16)
        dproj_ref[:, OFF_QB:OFF_KB] = mixer_b_grad(0)
        dproj_ref[:, OFF_KB:OFF_VB] = mixer_b_grad(1)
        dproj_ref[:, OFF_VB:D_IN] = mixer_b_grad(2)
        dxn = _dot_nt(dproj_ref[...], w_ref[...])
        dx, dg = _rms_bwd(x_ref[...], g_ref[...], dxn)
        dg_ref[...] += dg
        dx_ref[...] = dh1_ref[...] + dx

    half, kvw, full = _row_spec(tm, D_A), _row_spec(tm, D_KV_A), _row_spec(tm, D_MODEL)
    per_chunk = tk // tm
    chunk_t = pl.BlockSpec((N_KV_A, None, HEAD_DIM, tm), lambda i: (0, i // per_chunk, 0, i % per_chunk))
    view_specs, _ = _dilated_specs(s_len, tm, BF16)
    return pl.pallas_call(
        body, name="inproj_bwd", grid=(s_len // tm,),
        in_specs=[half, chunk_t, chunk_t] + ([half] + view_specs) * 3
                 + [half, kvw, full, full, _full_spec((1, D_MODEL)), _const_spec((D_MODEL, D_IN)), kvw, kvw,
                    _full_spec((1, D_A)), _full_spec((1, D_KV_A))] + [_full_spec((tm, tm))] * n_d,
        out_specs=[full, _row_spec(tm, D_IN), _full_spec((1, D_MODEL)), _full_spec((1, D_A)), _full_spec((1, D_KV_A))],
        out_shape=(SDS((s_len, D_MODEL), F32), SDS((s_len, D_IN), BF16), SDS((1, D_MODEL), F32), SDS((1, D_A), F32),
                   SDS((1, D_KV_A), F32)),
        compiler_params=_params(1, V7X_VMEM_LIMIT),
    )(dqs, dk_t, dv_t, *parts_b, qa_raw, ka_raw, x, dh1, g_pre, w_in, cos, sin, gq, gk,
      *[_dilation_perm(tm, d).T for d in DILATIONS])


def _weight_grad(a, b, name, tk1, tn, tm):
    s_len, k1 = a.shape
    n = b.shape[1]
    steps = s_len // tm

    def body(a_ref, b_ref, o_ref, acc):
        r = pl.program_id(2)
        _zero_at_first(r == 0, acc)
        acc[...] += _dot_tn(a_ref[...], b_ref[...])

        @pl.when(r == steps - 1)
        def _():
            o_ref[...] = acc[...].astype(BF16)

    return pl.pallas_call(
        body, name=name, grid=(k1 // tk1, n // tn, steps),
        in_specs=[pl.BlockSpec((tm, tk1), lambda i, j, r: (r, i)), pl.BlockSpec((tm, tn), lambda i, j, r: (r, j))],
        out_specs=pl.BlockSpec((tk1, tn), lambda i, j, r: (i, j)),
        out_shape=SDS((k1, n), BF16),
        scratch_shapes=[pltpu.VMEM((tk1, tn), F32)],
        compiler_params=_params(3, V7X_VMEM_LIMIT),
    )(a, b)


def _my_index():
    return 4 * lax.axis_index("x") + 2 * lax.axis_index("y") + lax.axis_index("c")


def _peer(k):
    f = k + 1
    x, y, c = lax.axis_index("x"), lax.axis_index("y"), lax.axis_index("c")
    return (x ^ ((f >> 2) & 1), y ^ ((f >> 1) & 1), c ^ (f & 1))


GATHER, EXCHANGE = "gather", "exchange"
ANY_SPEC = pl.BlockSpec(memory_space=pl.ANY)


def _comm_out_shapes(kind, arrays):
    return [SDS((N_DEV,) + a.shape if kind == GATHER else a.shape, a.dtype) for a in arrays]


def _comm_sems(n):
    return [pltpu.SemaphoreType.DMA((n, N_DEV - 1)), pltpu.SemaphoreType.DMA((n, N_DEV - 1)), pltpu.SemaphoreType.DMA((n,))]


def _comm_copies(kind, ins, outs, send_sems, recv_sems, local_sems):
    me = _my_index()
    local, remote = [], []
    for a, (src, dst) in enumerate(zip(ins, outs)):
        local.append(pltpu.make_async_copy(src if kind == GATHER else src.at[me], dst.at[me], local_sems.at[a]))
        for k in range(N_DEV - 1):
            px, py, pc = _peer(k)
            remote.append(pltpu.make_async_remote_copy(
                src_ref=src if kind == GATHER else src.at[4 * px + 2 * py + pc], dst_ref=dst.at[me],
                send_sem=send_sems.at[a, k], recv_sem=recv_sems.at[a, k],
                device_id=(px, py, pc), device_id_type=pl.DeviceIdType.MESH))
    return local, remote


def _comm_start(kind, ins, outs, sems):
    local, remote = _comm_copies(kind, ins, outs, *sems)
    for cp in local + remote:
        cp.start()


def _comm_wait(kind, ins, outs, sems):
    local, remote = _comm_copies(kind, ins, outs, *sems)
    for cp in local:
        cp.wait()
    for cp in remote:
        cp.wait_send()
    for cp in remote:
        cp.wait_recv()


def _collective(kind, arrays, name):
    n = len(arrays)

    def body(*refs):
        ins, outs, sems = refs[:n], refs[n:2 * n], refs[2 * n:]
        _comm_start(kind, ins, outs, sems)
        _comm_wait(kind, ins, outs, sems)

    return pl.pallas_call(
        body, name=name, in_specs=[ANY_SPEC] * n, out_specs=[ANY_SPEC] * n,
        out_shape=_comm_out_shapes(kind, arrays), scratch_shapes=_comm_sems(n),
    )(*arrays)


def _sum_adamw(parts, w, m, v, name, tr):
    rows, cols = w.shape
    c1 = 1.0 / (1.0 - ADAM_B1 ** ADAM_STEP)
    c2 = 1.0 / (1.0 - ADAM_B2 ** ADAM_STEP)

    def body(p_ref, w_ref, m_ref, v_ref, g_ref, d_ref, nm_ref, nv_ref):
        g = p_ref[0].astype(F32)
        for j in range(1, N_DEV):
            g = g + p_ref[j].astype(F32)
        g_ref[...] = g
        nm = ADAM_B1 * m_ref[...] + (1.0 - ADAM_B1) * g
        nv = ADAM_B2 * v_ref[...] + (1.0 - ADAM_B2) * (g * g)
        nm_ref[...] = nm
        nv_ref[...] = nv
        d_ref[...] = -ADAM_LR * ((nm * c1) / (jnp.sqrt(nv * c2) + ADAM_EPS) + ADAM_WD * w_ref[...])

    blk = pl.BlockSpec((tr, cols), lambda i: (i, 0))
    return pl.pallas_call(
        body, name=name, grid=(rows // tr,),
        in_specs=[pl.BlockSpec((N_DEV, tr, cols), lambda i: (0, i, 0)), blk, blk, blk],
        out_specs=[blk] * 4, out_shape=[SDS((rows, cols), F32)] * 4,
        compiler_params=_params(1, V7X_VMEM_LIMIT),
    )(parts, w, m, v)


_SMALL = (("g_attn_pre", 1024), ("g_q", 64), ("g_k", 64), ("g_out_a", 512), ("g_out_b", 512), ("g_attn_post", 1024),
          ("rel_bias", 256), ("g_mlp_pre", 1024), ("g_mlp_post", 1024), ("g_ple", 1024))
_SLAB_ROWS = 56


def _pack_small(vals):
    rows = []
    for (name, size) in _SMALL:
        flat = vals[name].reshape(-1).astype(F32)
        padded = -(-size // 128) * 128
        rows.append(jnp.pad(flat, (0, padded - size)).reshape(padded // 128, 128))
    slab = jnp.concatenate(rows, axis=0)
    return jnp.pad(slab, ((0, _SLAB_ROWS - slab.shape[0]), (0, 0)))


def _unpack_small(slab, shapes):
    out, row = {}, 0
    for (name, size) in _SMALL:
        nrow = -(-size // 128)
        out[name] = slab[row:row + nrow].reshape(-1)[:size].reshape(shapes[name])
        row += nrow
    return out


def _rope_tables(s_len):
    rows = s_len // GRID_W
    row = jnp.broadcast_to(jnp.arange(rows)[:, None], (rows, GRID_W)).reshape(-1).astype(F32)
    col = jnp.broadcast_to(jnp.arange(GRID_W)[None, :], (rows, GRID_W)).reshape(-1).astype(F32)
    n_axis = ROPE_HALF // 2
    inv_freq = ROPE_THETA ** (-jnp.arange(n_axis, dtype=F32) / n_axis)
    ang = jnp.concatenate([row[:, None] * inv_freq, col[:, None] * inv_freq], axis=-1)
    cos, sin = jnp.cos(ang), jnp.sin(ang)
    cos = jnp.tile(jnp.concatenate([cos, cos], axis=-1), (1, N_KV_A))
    sin = jnp.tile(jnp.concatenate([-sin, sin], axis=-1), (1, N_KV_A))
    return cos, sin


_RIDERS = {1: ("w_out", "w_ff1"), 4: ("w_ff2",), 16: ("w_ple_gate", "w_ple_proj")}
_GRAD_RIDERS = {1: ("w_ff1",), 4: ("w_ff2",), 16: ("w_ple_gate", "w_ple_proj", "w_out")}


ROW_TILE = 256
ATTN_Q_TILE = 128
ATTN_K_CHUNK = 2048
BAND_TILE = 256
GRAD_ROW_TILE = 1024


def _local_step(x, p, target, w_in, shards, small):
    axis_of = dict(_BIG)
    s_len = x.shape[0]
    tm = min(ROW_TILE, s_len)
    tq = min(ATTN_Q_TILE, s_len)
    tk = min(ATTN_K_CHUNK, s_len // 2)
    cos, sin = _rope_tables(s_len)
    gq = jnp.tile(small["g_q"], (1, N_HEADS_A))
    gk = jnp.tile(small["g_k"], (1, N_KV_A))

    qa_raw, ka_raw, qs, k_maj, kt, vt, qb, kb, vb, xn1, *views = _inproj_fwd(
        x, small["g_attn_pre"], w_in, cos, sin, gq, gk, tm, tk)
    n_d = len(DILATIONS)
    qb, kb, vb = ({1: plain, **dict(zip(DILATIONS, views[a * n_d:(a + 1) * n_d]))} for a, plain in enumerate((qb, kb, vb)))

    ya, lse_a = _attn_fwd(qs, k_maj, vt, tq, tk)

    ob, lb, tiles, full = [], [], [], {}
    for (_, dil) in DILATED_PATTERNS:
        t = min(BAND_TILE, s_len // dil)
        bias, bias_t = _bias_tiles(small["rel_bias"], dil, t)
        tiles.append((t, bias, bias_t, _band_buckets(dil, t)))
        o, l, gathered = _band_fwd(qb[dil], kb[dil], vb[dil], bias, dil, t, (GATHER, [shards[n] for n in _RIDERS[dil]]))
        ob.append(o)
        lb.append(l)
        full.update({n: _assemble(g, axis_of[n]) for n, g in zip(_RIDERS[dil], gathered)})
    w_out, w_ff1, w_ff2, w_gate, w_proj = (full[n] for n in ("w_out", "w_ff1", "w_ff2", "w_ple_gate", "w_ple_proj"))

    h1, yo, yb, lse_b, ycat = _attn_out_fwd(ya, ob, lb, x, small["g_out_a"], small["g_out_b"], w_out,
                                            small["g_attn_post"], tm)
    h2, fo, xn2 = _mlp_fwd(h1, small["g_mlp_pre"], w_ff1, w_ff2, small["g_mlp_post"], tm)
    dh2, loss_part, dg_ple, xn3, dgl, dpp, pb = _ple_fwd_bwd(h2, p, target, small["g_ple"], w_gate, w_proj, tm)
    dh1, dfo, du, f, dg_mlp_post, dg_mlp_pre = _mlp_bwd(h1, dh2, fo, xn2, small["g_mlp_pre"], w_ff1, w_ff2,
                                                          small["g_mlp_post"], tm)
    dyo, dya, dyb, delta_a, delta_b, dg_attn_post, dg_out_a, dg_out_b, *dyb_views = _attn_out_bwd(
        dh1, yo, ya, yb, small["g_out_a"], small["g_out_b"], w_out, small["g_attn_post"], tm)
    dyb = {1: dyb, **dict(zip(DILATIONS, dyb_views))}

    dqs, dk_t, dv_t = _attn_bwd(qs, dya, lse_a, delta_a, kt, k_maj, vt, tq, tk)

    tg = min(GRAD_ROW_TILE, s_len)
    grads = {
        "w_out": _weight_grad(ycat, dyo, "grad_w_out", D_MODEL, D_MODEL, tg),
        "w_ff1": _weight_grad(xn2, du, "grad_w_ff1", D_MODEL, 1024, tg),
        "w_ff2": _weight_grad(f, dfo, "grad_w_ff2", 1024, D_MODEL, tg),
        "w_ple_gate": _weight_grad(xn3, dgl, "grad_w_ple_gate", D_MODEL, D_MODEL, tg),
        "w_ple_proj": _weight_grad(pb, dpp, "grad_w_ple_proj", D_PLE, D_MODEL, tg),
    }

    dqb, dkb, dvb, dbias, received = {}, {}, {}, [], {}
    for (_, dil), (t, bias, bias_t, _) in zip(DILATED_PATTERNS, tiles):
        rider = (EXCHANGE, [_cut(grads[n], axis_of[n]) for n in _GRAD_RIDERS[dil]])
        dqb[dil], db, got = _band_dq(qb[dil], dyb[dil], lse_b, delta_b, kb[dil], vb[dil], bias, dil, t, rider)
        dkb[dil], dvb[dil] = _band_dkv(kb[dil], vb[dil], qb[dil], dyb[dil], lse_b, delta_b, bias_t, dil, t)
        dbias.append(db)
        received.update(zip(_GRAD_RIDERS[dil], got))
    d_rel = _dbias_reduce(dbias, [jnp.asarray(tl[3]) for tl in tiles])[:, :N_HEADS_B]

    parts_b = [part[d] for part in (dqb, dkb, dvb) for d in (1,) + DILATIONS]
    dx, dproj, dg_attn_pre, dgq_lanes, dgk_lanes = _inproj_bwd(
        dqs, dk_t, dv_t, parts_b, qa_raw, ka_raw, x, dh1, small["g_attn_pre"], w_in, cos, sin, gq, gk, tm)
    grad_w_in = _weight_grad(xn1, dproj, "grad_w_in", D_MODEL, 768, tg)
    small_grads = {
        "g_attn_pre": dg_attn_pre, "g_q": dgq_lanes.reshape(N_HEADS_A, HEAD_DIM).sum(0, keepdims=True),
        "g_k": dgk_lanes.reshape(N_KV_A, HEAD_DIM).sum(0, keepdims=True), "g_out_a": dg_out_a, "g_out_b": dg_out_b,
        "g_attn_post": dg_attn_post, "rel_bias": d_rel, "g_mlp_pre": dg_mlp_pre, "g_mlp_post": dg_mlp_post,
        "g_ple": dg_ple,
    }
    return loss_part[0, 0], dx, grad_w_in, received, small_grads


_BIG = (("w_in", 1), ("w_out", 0), ("w_ff1", 1), ("w_ff2", 0), ("w_ple_gate", 0), ("w_ple_proj", 1))


def _assemble(gathered, axis):
    if axis == 0:
        return gathered.reshape(-1, gathered.shape[2])
    return gathered.transpose(1, 0, 2).reshape(gathered.shape[1], -1)


def _cut(full, axis):
    if axis == 0:
        return full.reshape(N_DEV, full.shape[0] // N_DEV, full.shape[1])
    return full.reshape(full.shape[0], N_DEV, full.shape[1] // N_DEV).transpose(1, 0, 2)


def kernel(x, p, w_in, g_attn_pre, g_q, g_k, g_out_a, g_out_b, w_out, g_attn_post, rel_bias, g_mlp_pre, w_ff1, w_ff2, g_mlp_post, g_ple, w_ple_gate, w_ple_proj, loss_target, m_w_in, m_g_attn_pre, m_g_q, m_g_k, m_g_out_a, m_g_out_b, m_w_out, m_g_attn_post, m_rel_bias, m_g_mlp_pre, m_w_ff1, m_w_ff2, m_g_mlp_post, m_g_ple, m_w_ple_gate, m_w_ple_proj, v_w_in, v_g_attn_pre, v_g_q, v_g_k, v_g_out_a, v_g_out_b, v_w_out, v_g_attn_post, v_rel_bias, v_g_mlp_pre, v_w_ff1, v_w_ff2, v_g_mlp_post, v_g_ple, v_w_ple_gate, v_w_ple_proj):
    given = dict(locals())
    small_names = [n for n, _ in _SMALL]
    small = {n: given[n] for n in small_names}
    shards = {n: given[n][0] for n, _ in _BIG}

    bf16_shards = {n: shards[n].astype(BF16) for n, _ in _BIG}
    (gathered_w_in,) = _collective(GATHER, [bf16_shards["w_in"]], "gather_w_in")

    loss_part, dx, grad_w_in, received, small_grads = _local_step(
        x[0], p[0, 0], loss_target[0], _assemble(gathered_w_in, 1), bf16_shards, small)

    loss_at = np.zeros((_SLAB_ROWS, 128), bool)
    loss_at[-1, 0] = True
    slab = jnp.where(loss_at, loss_part * (0.5 / D_MODEL), _pack_small(small_grads))
    slab_parts = jnp.broadcast_to(slab[None], (N_DEV,) + slab.shape)
    received["w_in"], slabs = _collective(EXCHANGE, [_cut(grad_w_in, 1), slab_parts], "exchange_w_in_grads")

    out_g, out_d, out_m, out_v = {}, {}, {}, {}
    for n, _ in _BIG:
        rows = shards[n].shape[0]
        g, d, nm, nv = _sum_adamw(received[n], shards[n], given["m_" + n][0], given["v_" + n][0], "adamw_" + n,
                                  min(rows, 128))
        out_g[n], out_d[n], out_m[n], out_v[n] = g[None], d[None], nm[None], nv[None]
    g, d, nm, nv = _sum_adamw(slabs, _pack_small(small), _pack_small({n: given["m_" + n] for n in small_names}),
                              _pack_small({n: given["v_" + n] for n in small_names}), "adamw_small", _SLAB_ROWS)
    shapes = {n: given[n].shape for n in small_names}
    for dst, slab_out in ((out_g, g), (out_d, d), (out_m, nm), (out_v, nv)):
        dst.update(_unpack_small(slab_out, shapes))

    loss = g[-1, 0]
    order = ["w_in", "g_attn_pre", "g_q", "g_k", "g_out_a", "g_out_b", "w_out", "g_attn_post", "rel_bias", "g_mlp_pre",
             "w_ff1", "w_ff2", "g_mlp_post", "g_ple", "w_ple_gate", "w_ple_proj"]
    return (loss, dx[None], *[out_g[n] for n in order], *[out_d[n] for n in order], *[out_m[n] for n in order],
            *[out_v[n] for n in order])
```

```python
import functools
import math

import jax
import jax.numpy as jnp
import numpy as np
from jax import lax
from jax.experimental import pallas as pl
from jax.experimental.pallas import tpu as pltpu

F32 = jnp.float32
BF16 = jnp.bfloat16
SDS = jax.ShapeDtypeStruct

D_MODEL = 1024
HEAD_DIM = 64
N_HEADS_A = 8
N_KV_A = 2
GROUP_A = N_HEADS_A // N_KV_A
N_HEADS_B = 8
D_A = N_HEADS_A * HEAD_DIM
D_KV_A = N_KV_A * HEAD_DIM
D_B = N_HEADS_B * HEAD_DIM
D_IN = D_A + 2 * D_KV_A + 3 * D_B
D_FF = 4 * D_MODEL
D_PLE = 256
GRID_W = 64
ROPE_THETA = 10000.0
ROPE_HALF = HEAD_DIM // 2
DILATED_PATTERNS = ((128, 1), (512, 4), (2048, 16))
BAND_HALF = 64
N_BUCKETS = 32
MAX_DISTANCE = 1024
EPS = 1e-6
NEG_BIG = -1e30
SCORE_SCALE = HEAD_DIM ** -0.5

ADAM_LR = 0.001
ADAM_B1 = 0.9
ADAM_B2 = 0.999
ADAM_EPS = 1e-08
ADAM_WD = 0.01
ADAM_STEP = 10

N_DEV = 8
V7X_VMEM_LIMIT = 56 * 1024 * 1024

OFF_QA, OFF_KA, OFF_VA, OFF_QB, OFF_KB, OFF_VB = 0, 512, 640, 768, 1280, 1792


def _params(n_axes, vmem=None):
    return pltpu.CompilerParams(dimension_semantics=("arbitrary",) * n_axes, vmem_limit_bytes=vmem)


def _dot(a, b):
    return jnp.dot(a, b, preferred_element_type=F32)


def _dot_nt(a, b):
    return lax.dot_general(a, b, (((1,), (1,)), ((), ())), preferred_element_type=F32)


def _dot_tn(a, b):
    return lax.dot_general(a, b, (((0,), (0,)), ((), ())), preferred_element_type=F32)


def _rms_fwd(x, g):
    r = lax.rsqrt(jnp.mean(x * x, axis=-1, keepdims=True) + EPS)
    return x * r * g, r


def _rms_bwd(x, g, dy):
    r = lax.rsqrt(jnp.mean(x * x, axis=-1, keepdims=True) + EPS)
    xh = x * r
    dxh = dy * g
    dx = r * (dxh - xh * jnp.mean(dxh * xh, axis=-1, keepdims=True))
    return dx, jnp.sum(dy * xh, axis=0, keepdims=True)


def _head_sum(v):
    head = lax.broadcasted_iota(jnp.int32, v.shape, 1) >> 6
    out = jnp.zeros_like(v)
    for h in range(v.shape[1] // HEAD_DIM):
        msk = head == h
        s = jnp.sum(jnp.where(msk, v, 0.0), axis=-1, keepdims=True)
        out = jnp.where(msk, s, out)
    return out


def _swap_halves(v):
    w = v.shape[1]
    lane = lax.broadcasted_iota(jnp.int32, v.shape, 1)
    first_half = (lane & (HEAD_DIM - 1)) < ROPE_HALF
    return jnp.where(first_half, pltpu.roll(v, w - ROPE_HALF, 1), pltpu.roll(v, ROPE_HALF, 1))


def _head_norm_rope(v, g, cos, sin_signed):
    r = lax.rsqrt(_head_sum(v * v) * (1.0 / HEAD_DIM) + EPS)
    y = v * r * g
    return y * cos + _swap_halves(y) * sin_signed


def _head_norm_rope_bwd(v, g, cos, sin_signed, dout):
    dy = dout * cos - _swap_halves(dout) * sin_signed
    r = lax.rsqrt(_head_sum(v * v) * (1.0 / HEAD_DIM) + EPS)
    xh = v * r
    dxh = dy * g
    dv = r * (dxh - xh * (_head_sum(dxh * xh) * (1.0 / HEAD_DIM)))
    return dv, jnp.sum(dy * xh, axis=0, keepdims=True)


def _row_spec(tm, n):
    return pl.BlockSpec((tm, n), lambda i: (i, 0))


def _full_spec(shape):
    nd = len(shape)
    return pl.BlockSpec(shape, lambda *_: (0,) * nd)


def _const_spec(shape):
    nd = len(shape)
    return pl.BlockSpec(shape, lambda *_: (0,) * nd, pipeline_mode=pl.Buffered(1))


def _zero_at_first(first, *refs):
    @pl.when(first)
    def _():
        for ref in refs:
            ref[...] = jnp.zeros_like(ref)


DILATIONS = tuple(d for _, d in DILATED_PATTERNS if d > 1)


def _dilation_perm(tm, dil):
    rows = np.arange(tm)
    perm = np.zeros((tm, tm), np.float32)
    perm[(rows % dil) * (tm // dil) + rows // dil, rows] = 1.0
    return jnp.asarray(perm, BF16)


def _store_dilated(ref, perm_ref, val, dil):
    per = val.shape[0] // dil
    sorted_rows = _dot(perm_ref[...], val).astype(BF16)
    for r in range(dil):
        ref[:, r * D_B:(r + 1) * D_B] = sorted_rows[r * per:(r + 1) * per, :]


def _dilated_specs(s_len, tm, dtype):
    specs = [pl.BlockSpec((tm // d, d * D_B), lambda i: (i, 0)) for d in DILATIONS]
    return specs, [SDS((s_len // d, d * D_B), dtype) for d in DILATIONS]


ONES_ROWS = 16


def _inproj_fwd(x, g_pre, w_in, cos, sin, gq, gk, tm, tk):
    s_len = x.shape[0]
    n_d = len(DILATIONS)

    def body(x_ref, g_ref, w_ref, cos_ref, sin_ref, gq_ref, gk_ref, *rest):
        perms, views = rest[:n_d], rest[n_d + 10:]
        qa_raw, ka_raw, qs, k_maj, kt, vt, qb, kb, vb, xn_out = rest[n_d:n_d + 10]
        xn, _ = _rms_fwd(x_ref[...], g_ref[...])
        xn = xn.astype(BF16)
        xn_out[...] = xn
        qa = _dot(xn, w_ref[:, OFF_QA:OFF_KA])
        qa_raw[...] = qa
        cos, sin = cos_ref[...], sin_ref[...]
        qs[...] = (_head_norm_rope(qa, gq_ref[...], jnp.tile(cos, (1, D_A // D_KV_A)), jnp.tile(sin, (1, D_A // D_KV_A)))
                   * SCORE_SCALE).astype(BF16)
        ka = _dot(xn, w_ref[:, OFF_KA:OFF_VA])
        ka_raw[...] = ka
        kn = _head_norm_rope(ka, gk_ref[...], cos, sin).astype(BF16)
        for kv in range(N_KV_A):
            k_maj[kv] = kn[:, HEAD_DIM * kv:HEAD_DIM * (kv + 1)]
        kt[...] = kn.T.reshape(N_KV_A, HEAD_DIM, tm)
        va = _dot(xn, w_ref[:, OFF_VA:OFF_QB]).astype(BF16)
        vt[:, :HEAD_DIM, :] = va.T.reshape(N_KV_A, HEAD_DIM, tm)
        vt[:, HEAD_DIM:, :] = jnp.ones((N_KV_A, ONES_ROWS, tm), BF16)
        mixer_b = ((_dot(xn, w_ref[:, OFF_QB:OFF_KB]) * SCORE_SCALE).astype(BF16),
                   _dot(xn, w_ref[:, OFF_KB:OFF_VB]).astype(BF16), _dot(xn, w_ref[:, OFF_VB:D_IN]).astype(BF16))
        for a, (val, plain) in enumerate(zip(mixer_b, (qb, kb, vb))):
            plain[...] = val
            for j, d in enumerate(DILATIONS):
                _store_dilated(views[a * n_d + j], perms[j], val, d)

    view_specs, view_shapes = _dilated_specs(s_len, tm, BF16)
    per_chunk = tk // tm

    def chunk_t(rows):
        return pl.BlockSpec((N_KV_A, None, rows, tm), lambda i: (0, i // per_chunk, 0, i % per_chunk))

    return pl.pallas_call(
        body, name="inproj_fwd", grid=(s_len // tm,),
        in_specs=[_row_spec(tm, D_MODEL), _full_spec((1, D_MODEL)), _const_spec((D_MODEL, D_IN)),
                  _row_spec(tm, D_KV_A), _row_spec(tm, D_KV_A), _full_spec((1, D_A)), _full_spec((1, D_KV_A))]
                 + [_full_spec((tm, tm))] * n_d,
        out_specs=[_row_spec(tm, D_A), _row_spec(tm, D_KV_A), _row_spec(tm, D_A),
                   pl.BlockSpec((N_KV_A, tm, HEAD_DIM), lambda i: (0, i, 0)), chunk_t(HEAD_DIM),
                   chunk_t(HEAD_DIM + ONES_ROWS), _row_spec(tm, D_B), _row_spec(tm, D_B), _row_spec(tm, D_B),
                   _row_spec(tm, D_MODEL)] + view_specs * 3,
        out_shape=[SDS((s_len, D_A), F32), SDS((s_len, D_KV_A), F32), SDS((s_len, D_A), BF16),
                   SDS((N_KV_A, s_len, HEAD_DIM), BF16), SDS((N_KV_A, s_len // tk, HEAD_DIM, tk), BF16),
                   SDS((N_KV_A, s_len // tk, HEAD_DIM + ONES_ROWS, tk), BF16), SDS((s_len, D_B), BF16),
                   SDS((s_len, D_B), BF16), SDS((s_len, D_B), BF16), SDS((s_len, D_MODEL), BF16)] + view_shapes * 3,
        compiler_params=_params(1, V7X_VMEM_LIMIT),
    )(x, g_pre, w_in, cos, sin, gq, gk, *[_dilation_perm(tm, d) for d in DILATIONS])


def _stack_heads(ref, tq):
    return jnp.concatenate([ref[:, HEAD_DIM * g:HEAD_DIM * (g + 1)] for g in range(GROUP_A)], axis=0)


def _stack_cols(ref, tq):
    return jnp.concatenate([ref[:, HEAD_DIM * g:HEAD_DIM * g + 1] for g in range(GROUP_A)], axis=0)


def _attn_fwd(qs, k, vt, tq, tk):
    s_len = qs.shape[0]
    nk = s_len // tk
    assert nk % 2 == 0
    gw = GROUP_A * HEAD_DIM
    rows = GROUP_A * tq
    vrows = vt.shape[2]

    def body(q_ref, k_ref, vt_ref, o_ref, lse_ref, s_buf):
        qt = _stack_heads(q_ref, tq).T

        def scores(j, slot):
            kj = k_ref[pl.ds(pl.multiple_of(j * tk, tk), tk), :]
            s_buf[slot] = _dot(kj, qt)

        def consume(j, slot, carry):
            m, acc = carry
            st = s_buf[slot]
            m_new = jnp.maximum(m, jnp.max(st, axis=0, keepdims=True))
            pt = jnp.exp(st - m_new)
            acc = jnp.exp(m - m_new) * acc + _dot(vt_ref[j], pt.astype(BF16))
            return m_new, acc

        scores(0, 0)

        def pair(j, carry, more):
            scores(j + 1, 1)
            carry = consume(j, 0, carry)
            if more:
                scores(j + 2, 0)
            return consume(j + 1, 1, carry)

        carry = (jnp.full((1, rows), NEG_BIG, F32), jnp.zeros((vrows, rows), F32))
        carry = lax.fori_loop(0, nk // 2 - 1, lambda jj, c: pair(2 * jj, c, True), carry)
        m, acc = pair(nk - 2, carry, False)
        l = acc[HEAD_DIM:HEAD_DIM + 1]
        o = (acc[:HEAD_DIM] / l).T
        lse = jnp.broadcast_to(m + jnp.log(l), (HEAD_DIM, rows)).T
        for g in range(GROUP_A):
            o_ref[:, HEAD_DIM * g:HEAD_DIM * (g + 1)] = o[g * tq:(g + 1) * tq]
            lse_ref[:, HEAD_DIM * g:HEAD_DIM * (g + 1)] = lse[g * tq:(g + 1) * tq]

    tile = pl.BlockSpec((tq, gw), lambda kv, i: (i, kv))
    return pl.pallas_call(
        body, name="attn_fwd", grid=(N_KV_A, s_len // tq),
        in_specs=[tile, pl.BlockSpec((None, s_len, HEAD_DIM), lambda kv, i: (kv, 0, 0)),
                  pl.BlockSpec((None, nk, vrows, tk), lambda kv, i: (kv, 0, 0, 0))],
        out_specs=[tile, tile],
        out_shape=(SDS((s_len, D_A), F32), SDS((s_len, D_A), F32)),
        scratch_shapes=[pltpu.VMEM((2, tk, rows), F32)],
        compiler_params=_params(2, V7X_VMEM_LIMIT),
    )(qs, k, vt)


def _attn_bwd(qs, do, lse, delta, kt, k, vt, tq, tk):
    s_len = qs.shape[0]
    nk = s_len // tk
    assert nk % 2 == 0
    nq = s_len // tq
    gw = GROUP_A * HEAD_DIM
    rows = GROUP_A * tq

    def body(q_ref, do_ref, lse_ref, delta_ref, kt_ref, k_ref, vt_ref, dq_ref, dk_hbm, dv_hbm,
             dk_acc, dv_acc, s_buf, dp_buf):
        kv = pl.program_id(0)
        i = pl.program_id(1)

        @pl.when(i == 0)
        def _():
            dk_acc[...] = jnp.zeros_like(dk_acc)
            dv_acc[...] = jnp.zeros_like(dv_acc)

        q = _stack_heads(q_ref, tq)
        dout = _stack_heads(do_ref, tq)
        qt = q.T
        doutt = dout.T
        row_lse = _stack_cols(lse_ref, tq)
        row_delta = _stack_cols(delta_ref, tq)

        def scores(j, slot):
            s_buf[slot] = _dot(q, kt_ref[j])
            dp_buf[slot] = _dot(dout, vt_ref[j, :HEAD_DIM, :])

        def consume(j, slot, dq):
            p = jnp.exp(s_buf[slot] - row_lse)
            ds = (p * (dp_buf[slot] - row_delta)).astype(BF16)
            dv_acc[j] += _dot(doutt, p.astype(BF16))
            dk_acc[j] += _dot(qt, ds)
            return dq + _dot(ds, k_ref[pl.ds(pl.multiple_of(j * tk, tk), tk), :])

        scores(0, 0)

        def pair(j, dq, more):
            scores(j + 1, 1)
            dq = consume(j, 0, dq)
            if more:
                scores(j + 2, 0)
            return consume(j + 1, 1, dq)

        dq = lax.fori_loop(0, nk // 2 - 1, lambda jj, c: pair(2 * jj, c, True), jnp.zeros((rows, HEAD_DIM), F32))
        dq = pair(nk - 2, dq, False) * SCORE_SCALE
        for g in range(GROUP_A):
            dq_ref[:, HEAD_DIM * g:HEAD_DIM * (g + 1)] = dq[g * tq:(g + 1) * tq]

        @pl.when(i == nq - 1)
        def _():
            pltpu.sync_copy(dk_acc, dk_hbm.at[kv])
            pltpu.sync_copy(dv_acc, dv_hbm.at[kv])

    tile = pl.BlockSpec((tq, gw), lambda kv, i: (i, kv))
    chunks = pl.BlockSpec((None, nk, HEAD_DIM, tk), lambda kv, i: (kv, 0, 0, 0))
    grad_t = SDS((N_KV_A, nk, HEAD_DIM, tk), F32)
    return pl.pallas_call(
        body, name="attn_bwd", grid=(N_KV_A, nq),
        in_specs=[tile, tile, tile, tile, chunks,
                  pl.BlockSpec((None, s_len, HEAD_DIM), lambda kv, i: (kv, 0, 0)),
                  pl.BlockSpec((None, nk, vt.shape[2], tk), lambda kv, i: (kv, 0, 0, 0))],
        out_specs=[tile, pl.BlockSpec(memory_space=pl.ANY), pl.BlockSpec(memory_space=pl.ANY)],
        out_shape=(SDS((s_len, D_A), F32), grad_t, grad_t),
        scratch_shapes=[pltpu.VMEM((nk, HEAD_DIM, tk), F32), pltpu.VMEM((nk, HEAD_DIM, tk), F32),
                        pltpu.VMEM((2, rows, tk), F32), pltpu.VMEM((2, rows, tk), F32)],
        compiler_params=_params(2, V7X_VMEM_LIMIT),
    )(qs, do, lse, delta, kt, k, vt)


STAT_W = 128


def _band_specs(length, t, width=D_B):
    hb = t // BAND_HALF
    last = length // BAND_HALF - 1
    main = pl.BlockSpec((t, width), lambda r, i: (i, r))
    prev = pl.BlockSpec((BAND_HALF, width), lambda r, i: (jnp.maximum(i * hb - 1, 0), r))
    nxt = pl.BlockSpec((BAND_HALF, width), lambda r, i: (jnp.minimum((i + 1) * hb, last), r))
    return main, [prev, main, nxt]


def _pack_heads(cols):
    lane = lax.broadcasted_iota(jnp.int32, (cols[0].shape[0], STAT_W), 1)
    out = jnp.zeros((cols[0].shape[0], STAT_W), F32)
    for h, c in enumerate(cols):
        out = jnp.where(lane == h, c, out)
    return out


def _spread_heads(stat):
    head = lax.broadcasted_iota(jnp.int32, (stat.shape[0], D_B), 1) >> 6
    out = jnp.zeros((stat.shape[0], D_B), F32)
    for h in range(N_HEADS_B):
        out = jnp.where(head == h, stat[:, h:h + 1], out)
    return out


def _window(refs):
    return jnp.concatenate([r[...] for r in refs], axis=0)


def _head(v, h):
    return v[:, HEAD_DIM * h:HEAD_DIM * (h + 1)]


def _call_with_rider(body, name, grid, in_specs, out_specs, out_shape, scratch_shapes, args, rider):
    n_in, n_out = len(in_specs), len(out_specs)
    if rider is None:
        res = pl.pallas_call(body, name=name, grid=grid, in_specs=in_specs, out_specs=out_specs, out_shape=out_shape,
                             scratch_shapes=scratch_shapes, compiler_params=_params(len(grid), V7X_VMEM_LIMIT))(*args)
        return res, []
    kind, arrays = rider
    n = len(arrays)

    def with_rider(*refs):
        ins, c_ins = refs[:n_in], refs[n_in:n_in + n]
        outs, c_outs = refs[n_in + n:n_in + n + n_out], refs[n_in + n + n_out:n_in + 2 * n + n_out]
        rest = refs[n_in + 2 * n + n_out:]
        scratch, sems = rest[:-3], rest[-3:]
        ids = [pl.program_id(a) for a in range(len(grid))]
        first = functools.reduce(jnp.logical_and, [i == 0 for i in ids])
        last = functools.reduce(jnp.logical_and, [i == g - 1 for i, g in zip(ids, grid)])

        @pl.when(first)
        def _():
            _comm_start(kind, c_ins, c_outs, sems)

        body(*ins, *outs, *scratch)

        @pl.when(last)
        def _():
            _comm_wait(kind, c_ins, c_outs, sems)

    res = pl.pallas_call(
        with_rider, name=name, grid=grid, in_specs=list(in_specs) + [ANY_SPEC] * n,
        out_specs=list(out_specs) + [ANY_SPEC] * n, out_shape=list(out_shape) + _comm_out_shapes(kind, arrays),
        scratch_shapes=list(scratch_shapes) + _comm_sems(n), compiler_params=_params(len(grid), V7X_VMEM_LIMIT),
    )(*args, *arrays)
    return res[:n_out], res[n_out:]


def _band_fwd(q, k, v, bias, dil, t, rider=None):
    s_len = q.size // D_B
    length = s_len // dil
    w = t + 2 * BAND_HALF
    view = lambda a: a.reshape(length, dil * D_B)
    main, win = _band_specs(length, t)
    stat, _ = _band_specs(length, t, STAT_W)

    def body(q_ref, k0, k1, k2, v0, v1, v2, bias_ref, o_ref, lse_ref, s_buf, kt_buf):
        lses = []
        i = pl.program_id(1)
        kt_buf[...] = _window((k0, k1, k2)).T
        vw = _window((v0, v1, v2))
        pos = i * t - BAND_HALF + lax.broadcasted_iota(jnp.int32, (1, w), 1)
        valid = (pos >= 0) & (pos < length)
        for h in range(N_HEADS_B):
            s_buf[h] = _dot(q_ref[:, HEAD_DIM * h:HEAD_DIM * (h + 1)], kt_buf[HEAD_DIM * h:HEAD_DIM * (h + 1), :])
        for h in range(N_HEADS_B):
            s = jnp.where(valid, s_buf[h] + bias_ref[h], NEG_BIG)
            m = jnp.max(s, axis=-1, keepdims=True)
            e = jnp.exp(s - m)
            den = jnp.sum(e, axis=-1, keepdims=True)
            o_ref[:, HEAD_DIM * h:HEAD_DIM * (h + 1)] = _dot(e.astype(BF16), _head(vw, h)) / den
            lses.append(m + jnp.log(den))
        lse_ref[...] = _pack_heads(lses)

    (o, lse), rode = _call_with_rider(
        body, f"band_fwd_d{dil}", (dil, length // t),
        [main] + win + win + [_full_spec((N_HEADS_B, t, w))], [main, stat],
        [SDS((length, dil * D_B), F32), SDS((length, dil * STAT_W), F32)],
        [pltpu.VMEM((N_HEADS_B, t, w), F32), pltpu.VMEM((D_B, w), BF16)],
        (view(q), view(k), view(k), view(k), view(v), view(v), view(v), bias), rider)
    return o.reshape(s_len, D_B), lse.reshape(s_len, STAT_W), rode


def _band_dq(q, do, lse, delta, k, v, bias, dil, t, rider=None):
    s_len = q.size // D_B
    length = s_len // dil
    w = t + 2 * BAND_HALF
    view = lambda a: a.reshape(length, dil * D_B)
    main, win = _band_specs(length, t)

    def body(q_ref, do_ref, lse_ref, delta_ref, k0, k1, k2, v0, v1, v2, bias_ref, dq_ref, dbias_ref,
             s_buf, dp_buf, kt_buf, vt_buf, dq_buf):
        i = pl.program_id(1)
        @pl.when((pl.program_id(0) == 0) & (i == 0))
        def _():
            dbias_ref[...] = jnp.zeros_like(dbias_ref)

        kw = _window((k0, k1, k2))
        kt_buf[...] = kw.T
        vt_buf[...] = _window((v0, v1, v2)).T
        pos = i * t - BAND_HALF + lax.broadcasted_iota(jnp.int32, (1, w), 1)
        valid = (pos >= 0) & (pos < length)
        for h in range(N_HEADS_B):
            cols = slice(HEAD_DIM * h, HEAD_DIM * (h + 1))
            s_buf[h] = _dot(q_ref[:, cols], kt_buf[cols, :])
            dp_buf[h] = _dot(do_ref[:, cols], vt_buf[cols, :])
        for h in range(N_HEADS_B):
            cols = slice(HEAD_DIM * h, HEAD_DIM * (h + 1))
            s = jnp.where(valid, s_buf[h] + bias_ref[h], NEG_BIG)
            p = jnp.exp(s - lse_ref[:, h:h + 1])
            ds = p * (dp_buf[h] - delta_ref[:, h:h + 1])
            dq_buf[:, cols] = _dot(ds.astype(BF16), _head(kw, h)) * SCORE_SCALE
            dbias_ref[h] += ds
        dq_ref[...] = dq_buf[...].astype(BF16)

    stat, _ = _band_specs(length, t, STAT_W)
    sview = lambda a: a.reshape(length, dil * STAT_W)
    (dq, dbias), rode = _call_with_rider(
        body, f"band_dq_d{dil}", (dil, length // t),
        [main, main, stat, stat] + win + win + [_full_spec((N_HEADS_B, t, w))],
        [main, _full_spec((N_HEADS_B, t, w))],
        [SDS((length, dil * D_B), BF16), SDS((N_HEADS_B, t, w), F32)],
        [pltpu.VMEM((N_HEADS_B, t, w), F32), pltpu.VMEM((N_HEADS_B, t, w), F32),
         pltpu.VMEM((D_B, w), BF16), pltpu.VMEM((D_B, w), BF16), pltpu.VMEM((t, D_B), F32)],
        (view(q), view(do), sview(lse), sview(delta), view(k), view(k), view(k), view(v), view(v), view(v), bias),
        rider)
    return dq, dbias, rode


def _band_dkv(k, v, q, do, lse, delta, bias_t, dil, t):
    s_len = q.size // D_B
    length = s_len // dil
    w = t + 2 * BAND_HALF
    view = lambda a: a.reshape(length, dil * D_B)
    main, win = _band_specs(length, t)

    def body(k_ref, v_ref, q0, q1, q2, d0, d1, d2, l0, l1, l2, e0, e1, e2, bias_ref, dk_ref, dv_ref,
             s_buf, dp_buf, kt_buf, vt_buf):
        i = pl.program_id(1)
        qw = _window((q0, q1, q2))
        dow = _window((d0, d1, d2))
        lsew = _window((l0, l1, l2))
        deltaw = _window((e0, e1, e2))
        pos = i * t - BAND_HALF + lax.broadcasted_iota(jnp.int32, (w, 1), 0)
        valid = (pos >= 0) & (pos < length)
        kt_buf[...] = k_ref[...].T
        vt_buf[...] = v_ref[...].T
        for h in range(N_HEADS_B):
            cols = slice(HEAD_DIM * h, HEAD_DIM * (h + 1))
            s_buf[h] = _dot(_head(qw, h), kt_buf[cols, :])
            dp_buf[h] = _dot(_head(dow, h), vt_buf[cols, :])
        qwt = qw.T
        dowt = dow.T
        dkt, dvt = [], []
        for h in range(N_HEADS_B):
            rows = slice(HEAD_DIM * h, HEAD_DIM * (h + 1))
            s = jnp.where(valid, s_buf[h] + bias_ref[h], NEG_BIG)
            p = jnp.exp(s - lsew[:, h:h + 1])
            ds = p * (dp_buf[h] - deltaw[:, h:h + 1])
            dvt.append(_dot(dowt[rows, :], p.astype(BF16)))
            dkt.append(_dot(qwt[rows, :], ds.astype(BF16)))
        dv_ref[...] = jnp.concatenate(dvt, axis=0).T.astype(BF16)
        dk_ref[...] = jnp.concatenate(dkt, axis=0).T.astype(BF16)

    _, swin = _band_specs(length, t, STAT_W)
    sview = lambda a: a.reshape(length, dil * STAT_W)
    dk, dv = pl.pallas_call(
        body, name=f"band_dkv_d{dil}", grid=(dil, length // t),
        in_specs=[main, main] + win + win + swin + swin + [_full_spec((N_HEADS_B, w, t))],
        out_specs=[main, main],
        out_shape=(SDS((length, dil * D_B), BF16), SDS((length, dil * D_B), BF16)),
        scratch_shapes=[pltpu.VMEM((N_HEADS_B, w, t), F32), pltpu.VMEM((N_HEADS_B, w, t), F32),
                        pltpu.VMEM((D_B, t), BF16), pltpu.VMEM((D_B, t), BF16)],
        compiler_params=_params(2, V7X_VMEM_LIMIT),
    )(view(k), view(v), view(q), view(q), view(q), view(do), view(do), view(do), sview(lse), sview(lse), sview(lse),
      sview(delta), sview(delta), sview(delta), bias_t)
    return dk, dv


def _t5_bucket_np(rel):
    nb = N_BUCKETS // 2
    max_exact = nb // 2
    side = np.where(rel > 0, nb, 0)
    n = np.abs(rel)
    ratio = np.maximum(n, max_exact).astype(np.float32) / np.float32(max_exact)
    large = max_exact + (np.log(ratio) / np.float32(math.log(MAX_DISTANCE / max_exact))
                         * np.float32(nb - max_exact)).astype(np.int32)
    large = np.minimum(large, nb - 1)
    return (side + np.where(n < max_exact, n, large)).astype(np.int32)


def _band_buckets(dil, t):
    rel = np.arange(t + 2 * BAND_HALF)[None, :] - BAND_HALF - np.arange(t)[:, None]
    bucket = _t5_bucket_np(np.clip(rel, -BAND_HALF, BAND_HALF) * dil)
    return np.where(np.abs(rel) <= BAND_HALF, bucket, -1).astype(np.int32)


def _toeplitz(vals, rows, cols):
    heads = vals.shape[0]
    period = -(-(rows + cols) // 128) * 128 + 1
    gap = period - (rows + cols - 1)
    vec = jnp.concatenate([vals[:, rows - 1:], jnp.zeros((heads, gap), vals.dtype), vals[:, :rows - 1]], axis=1)
    flat = jnp.broadcast_to(vec[:, None, :], (heads, rows, period)).reshape(heads, rows * period)
    return flat[:, :rows * (period - 1)].reshape(heads, rows, period - 1)[:, :, :cols]


def _bias_tiles(rel_bias, dil, t):
    w = t + 2 * BAND_HALF
    rel = np.arange(-BAND_HALF, BAND_HALF + 1)
    bucket = _t5_bucket_np(rel * dil)
    runs, start = [], 0
    for i in range(1, len(bucket) + 1):
        if i == len(bucket) or bucket[i] != bucket[start]:
            b = int(bucket[start])
            runs.append(jnp.broadcast_to(rel_bias[b:b + 1], (i - start, N_HEADS_B)))
            start = i
    per_rel = jnp.concatenate(runs, axis=0).T

    def diagonals(lo, hi):
        left = jnp.full((N_HEADS_B, max(0, -BAND_HALF - lo)), NEG_BIG, F32)
        right = jnp.full((N_HEADS_B, max(0, hi - BAND_HALF)), NEG_BIG, F32)
        return jnp.concatenate([left, per_rel, right], axis=1)

    tile = _toeplitz(diagonals(-(t - 1) - BAND_HALF, w - 1 - BAND_HALF), t, w)
    twin = _toeplitz(diagonals(-(w - 1) + BAND_HALF, t - 1 + BAND_HALF), w, t)
    return tile, twin


def _dbias_reduce(dbias, buckets):
    n = len(dbias)

    def body(*refs):
        db_refs, bk_refs, o_ref = refs[:n], refs[n:2 * n], refs[2 * n]
        row = lax.broadcasted_iota(jnp.int32, (N_BUCKETS, 128), 0)
        lane = lax.broadcasted_iota(jnp.int32, (N_BUCKETS, 128), 1)

        def per_bucket(b, out):
            for pat in range(n):
                msk = bk_refs[pat][...] == b
                for h in range(N_HEADS_B):
                    tot = jnp.sum(jnp.where(msk, db_refs[pat][h], 0.0), axis=-1, keepdims=True)
                    tot = jnp.sum(tot, axis=0, keepdims=True)
                    out = out + jnp.where((row == b) & (lane == h), tot, 0.0)
            return out

        o_ref[...] = lax.fori_loop(0, N_BUCKETS, per_bucket, jnp.zeros((N_BUCKETS, 128), F32))

    return pl.pallas_call(
        body, name="dbias_reduce", out_shape=SDS((N_BUCKETS, 128), F32),
        compiler_params=pltpu.CompilerParams(vmem_limit_bytes=V7X_VMEM_LIMIT),
    )(*dbias, *buckets)


def _attn_out_fwd(ya, ob, lb, x, g_a, g_b, w_out, g_post, tm):
    s_len = ya.shape[0]

    def body(ya_ref, o0, o1, o2, l0, l1, l2, x_ref, ga_ref, gb_ref, w_ref, gp_ref,
             h1_ref, yo_ref, yb_ref, lse_ref, ycat_ref):
        m = jnp.maximum(jnp.maximum(l0[...], l1[...]), l2[...])
        w0, w1, w2 = jnp.exp(l0[...] - m), jnp.exp(l1[...] - m), jnp.exp(l2[...] - m)
        wsum = w0 + w1 + w2
        yb = (_spread_heads(w0 / wsum) * o0[...] + _spread_heads(w1 / wsum) * o1[...]
              + _spread_heads(w2 / wsum) * o2[...])
        yb_ref[...] = yb
        lse_ref[...] = m + jnp.log(wsum)
        yan, _ = _rms_fwd(ya_ref[...], ga_ref[...])
        ybn, _ = _rms_fwd(yb, gb_ref[...])
        yan, ybn = yan.astype(BF16), ybn.astype(BF16)
        ycat_ref[:, :D_A] = yan
        ycat_ref[:, D_A:] = ybn
        yo = _dot(yan, w_ref[:D_A, :]) + _dot(ybn, w_ref[D_A:, :])
        yo_ref[...] = yo
        post, _ = _rms_fwd(yo, gp_ref[...])
        h1_ref[...] = x_ref[...] + post

    half, full, stat = _row_spec(tm, D_A), _row_spec(tm, D_MODEL), _row_spec(tm, STAT_W)
    return pl.pallas_call(
        body, name="attn_out_fwd", grid=(s_len // tm,),
        in_specs=[half] * 4 + [stat] * 3 + [full, _full_spec((1, D_A)), _full_spec((1, D_B)),
                                            _full_spec((D_MODEL, D_MODEL)), _full_spec((1, D_MODEL))],
        out_specs=[full, full, half, stat, full],
        out_shape=(SDS((s_len, D_MODEL), F32), SDS((s_len, D_MODEL), F32), SDS((s_len, D_B), F32),
                   SDS((s_len, STAT_W), F32), SDS((s_len, D_MODEL), BF16)),
        compiler_params=_params(1, V7X_VMEM_LIMIT),
    )(ya, ob[0], ob[1], ob[2], lb[0], lb[1], lb[2], x, g_a, g_b, w_out, g_post)


def _attn_out_bwd(dh1, yo, ya, yb, g_a, g_b, w_out, g_post, tm):
    s_len = ya.shape[0]
    n_d = len(DILATIONS)

    def body(dh1_ref, yo_ref, ya_ref, yb_ref, ga_ref, gb_ref, w_ref, gp_ref, *rest):
        perms, views = rest[:n_d], rest[n_d + 8:]
        dyo_ref, dya_ref, dyb_ref, dela_ref, delb_ref, dgp_ref, dga_ref, dgb_ref = rest[n_d:n_d + 8]
        _zero_at_first(pl.program_id(0) == 0, dgp_ref, dga_ref, dgb_ref)
        dyo, dgp = _rms_bwd(yo_ref[...], gp_ref[...], dh1_ref[...])
        dyo = dyo.astype(BF16)
        dyo_ref[...] = dyo
        dgp_ref[...] += dgp
        dya_n = _dot_nt(dyo, w_ref[:D_A, :])
        dyb_n = _dot_nt(dyo, w_ref[D_A:, :])
        ya, yb = ya_ref[...], yb_ref[...]
        dya, dga = _rms_bwd(ya, ga_ref[...], dya_n)
        dyb, dgb = _rms_bwd(yb, gb_ref[...], dyb_n)
        dga_ref[...] += dga
        dgb_ref[...] += dgb
        dya_ref[...] = dya.astype(BF16)
        dyb_ref[...] = dyb.astype(BF16)
        for j, d in enumerate(DILATIONS):
            _store_dilated(views[j], perms[j], dyb.astype(BF16), d)
        dela_ref[...] = _head_sum(dya * ya)
        prod = dyb * yb
        head = lax.broadcasted_iota(jnp.int32, prod.shape, 1) >> 6
        delb_ref[...] = _pack_heads([jnp.sum(jnp.where(head == h, prod, 0.0), axis=-1, keepdims=True)
                                     for h in range(N_HEADS_B)])

    half, full = _row_spec(tm, D_A), _row_spec(tm, D_MODEL)
    view_specs, view_shapes = _dilated_specs(s_len, tm, BF16)
    return pl.pallas_call(
        body, name="attn_out_bwd", grid=(s_len // tm,),
        in_specs=[full, full, half, half, _full_spec((1, D_A)), _full_spec((1, D_B)),
                  _full_spec((D_MODEL, D_MODEL)), _full_spec((1, D_MODEL))] + [_full_spec((tm, tm))] * n_d,
        out_specs=[full, half, half, half, _row_spec(tm, STAT_W), _full_spec((1, D_MODEL)), _full_spec((1, D_A)),
                   _full_spec((1, D_B))] + view_specs,
        out_shape=[SDS((s_len, D_MODEL), BF16), SDS((s_len, D_A), BF16), SDS((s_len, D_B), BF16),
                   SDS((s_len, D_A), F32), SDS((s_len, STAT_W), F32), SDS((1, D_MODEL), F32), SDS((1, D_A), F32),
                   SDS((1, D_B), F32)] + view_shapes,
        compiler_params=_params(1, V7X_VMEM_LIMIT),
    )(dh1, yo, ya, yb, g_a, g_b, w_out, g_post, *[_dilation_perm(tm, d) for d in DILATIONS])


FF_CHUNK = 1024


def _mlp_fwd(h1, g_pre, w1, w2, g_post, tm):
    s_len = h1.shape[0]

    def body(h1_ref, gpre_ref, w1_ref, w2_ref, gpost_ref, h2_ref, fo_ref, xn_ref):
        h1v = h1_ref[...]
        xn, _ = _rms_fwd(h1v, gpre_ref[...])
        xn = xn.astype(BF16)
        xn_ref[...] = xn
        fo = jnp.zeros((tm, D_MODEL), F32)
        for c in range(D_FF // FF_CHUNK):
            cols = slice(c * FF_CHUNK, (c + 1) * FF_CHUNK)
            u = jnp.maximum(_dot(xn, w1_ref[:, cols]), 0.0)
            fo = fo + _dot((u * u).astype(BF16), w2_ref[cols, :])
        fo_ref[...] = fo
        post, _ = _rms_fwd(fo, gpost_ref[...])
        h2_ref[...] = h1v + post

    full = _row_spec(tm, D_MODEL)
    return pl.pallas_call(
        body, name="mlp_fwd", grid=(s_len // tm,),
        in_specs=[full, _full_spec((1, D_MODEL)), _const_spec((D_MODEL, D_FF)), _const_spec((D_FF, D_MODEL)),
                  _full_spec((1, D_MODEL))],
        out_specs=[full, full, full],
        out_shape=(SDS((s_len, D_MODEL), F32), SDS((s_len, D_MODEL), F32), SDS((s_len, D_MODEL), BF16)),
        compiler_params=_params(1, V7X_VMEM_LIMIT),
    )(h1, g_pre, w1, w2, g_post)


def _mlp_bwd(h1, dh2, fo, xn, g_pre, w1, w2, g_post, tm):
    s_len = h1.shape[0]

    def body(h1_ref, dh2_ref, fo_ref, xn_ref, gpre_ref, w1_ref, w2_ref, gpost_ref,
             dh1_ref, dfo_ref, du_ref, f_ref, dgpost_ref, dgpre_ref):
        _zero_at_first(pl.program_id(0) == 0, dgpost_ref, dgpre_ref)
        dh2 = dh2_ref[...]
        dfo, dgpost = _rms_bwd(fo_ref[...], gpost_ref[...], dh2)
        dfo = dfo.astype(BF16)
        dfo_ref[...] = dfo
        dgpost_ref[...] += dgpost
        xn = xn_ref[...]
        dxn = jnp.zeros((tm, D_MODEL), F32)
        for c in range(D_FF // FF_CHUNK):
            cols = slice(c * FF_CHUNK, (c + 1) * FF_CHUNK)
            u = jnp.maximum(_dot(xn, w1_ref[:, cols]), 0.0)
            f_ref[:, cols] = (u * u).astype(BF16)
            du = (_dot_nt(dfo, w2_ref[cols, :]) * (2.0 * u)).astype(BF16)
            du_ref[:, cols] = du
            dxn = dxn + _dot_nt(du, w1_ref[:, cols])
        dx, dgpre = _rms_bwd(h1_ref[...], gpre_ref[...], dxn)
        dgpre_ref[...] += dgpre
        dh1_ref[...] = dh2 + dx

    full, wide = _row_spec(tm, D_MODEL), _row_spec(tm, D_FF)
    return pl.pallas_call(
        body, name="mlp_bwd", grid=(s_len // tm,),
        in_specs=[full, full, full, full, _full_spec((1, D_MODEL)), _const_spec((D_MODEL, D_FF)),
                  _const_spec((D_FF, D_MODEL)), _full_spec((1, D_MODEL))],
        out_specs=[full, full, wide, wide, _full_spec((1, D_MODEL)), _full_spec((1, D_MODEL))],
        out_shape=(SDS((s_len, D_MODEL), F32), SDS((s_len, D_MODEL), BF16), SDS((s_len, D_FF), BF16),
                   SDS((s_len, D_FF), BF16), SDS((1, D_MODEL), F32), SDS((1, D_MODEL), F32)),
        compiler_params=_params(1, V7X_VMEM_LIMIT),
    )(h1, dh2, fo, xn, g_pre, w1, w2, g_post)


def _ple_fwd_bwd(h2, p, target, g_ple, w_gate, w_proj, tm):
    s_len = h2.shape[0]

    def body(h2_ref, p_ref, t_ref, g_ref, wg_ref, wp_ref, dh2_ref, loss_ref, dg_ref, xn_ref, dgl_ref, dpp_ref, pb_ref):
        _zero_at_first(pl.program_id(0) == 0, loss_ref, dg_ref)
        h2 = h2_ref[...]
        g = g_ref[...]
        xn, _ = _rms_fwd(h2, g)
        xn = xn.astype(BF16)
        xn_ref[...] = xn
        gate = 1.0 / (1.0 + jnp.exp(-_dot(xn, wg_ref[...])))
        pb = p_ref[...].astype(BF16)
        pb_ref[...] = pb
        pp = _dot(pb, wp_ref[...])
        diff = h2 + gate * pp - t_ref[...]
        loss_ref[...] += jnp.full((8, 128), jnp.sum(diff * diff), F32)
        dh3 = diff * (1.0 / D_MODEL)
        dpp_ref[...] = (dh3 * gate).astype(BF16)
        dgl = (dh3 * pp * gate * (1.0 - gate)).astype(BF16)
        dgl_ref[...] = dgl
        dx, dg = _rms_bwd(h2, g, _dot_nt(dgl, wg_ref[...]))
        dg_ref[...] += dg
        dh2_ref[...] = dh3 + dx

    full = _row_spec(tm, D_MODEL)
    return pl.pallas_call(
        body, name="ple_fwd_bwd", grid=(s_len // tm,),
        in_specs=[full, _row_spec(tm, D_PLE), full, _full_spec((1, D_MODEL)), _full_spec((D_MODEL, D_MODEL)),
                  _full_spec((D_PLE, D_MODEL))],
        out_specs=[full, _full_spec((8, 128)), _full_spec((1, D_MODEL)), full, full, full, _row_spec(tm, D_PLE)],
        out_shape=(SDS((s_len, D_MODEL), F32), SDS((8, 128), F32), SDS((1, D_MODEL), F32), SDS((s_len, D_MODEL), BF16),
                   SDS((s_len, D_MODEL), BF16), SDS((s_len, D_MODEL), BF16), SDS((s_len, D_PLE), BF16)),
        compiler_params=_params(1, V7X_VMEM_LIMIT),
    )(h2, p, target, g_ple, w_gate, w_proj)


def _inproj_bwd(dqs, dk_t, dv_t, parts_b, qa_raw, ka_raw, x, dh1, g_pre, w_in, cos, sin, gq, gk, tm):
    s_len = x.shape[0]
    tk = dk_t.shape[3]

    n_d = len(DILATIONS)
    n_parts = 3 * (1 + n_d)

    def body(dqs_ref, dkt_ref, dvt_ref, *rest):
        parts, rest = rest[:n_parts], rest[n_parts:]
        (qa_ref, ka_ref, x_ref, dh1_ref, g_ref, w_ref, cos_ref, sin_ref, gq_ref, gk_ref), rest = rest[:10], rest[10:]
        unsort, (dx_ref, dproj_ref, dg_ref, dgq_ref, dgk_ref) = rest[:n_d], rest[n_d:]

        def mixer_b_grad(a):
            own = parts[a * (1 + n_d):(a + 1) * (1 + n_d)]
            total = own[0][...].astype(F32)
            for j, d in enumerate(DILATIONS):
                sorted_rows = jnp.concatenate([own[1 + j][:, r * D_B:(r + 1) * D_B] for r in range(d)], axis=0)
                total = total + _dot(unsort[j][...], sorted_rows)
            return total.astype(BF16)

        _zero_at_first(pl.program_id(0) == 0, dg_ref, dgq_ref, dgk_ref)
        cos, sin = cos_ref[...], sin_ref[...]
        dkn = dkt_ref[...].reshape(D_KV_A, tm).T
        dva = dvt_ref[...].reshape(D_KV_A, tm).T
        dqa, dgq = _head_norm_rope_bwd(qa_ref[...], gq_ref[...], jnp.tile(cos, (1, D_A // D_KV_A)),
                                       jnp.tile(sin, (1, D_A // D_KV_A)), dqs_ref[...])
        dka, dgk = _head_norm_rope_bwd(ka_ref[...], gk_ref[...], cos, sin, dkn)
        dgq_ref[...] += dgq
        dgk_ref[...] += dgk
        dproj_ref[:, OFF_QA:OFF_KA] = dqa.astype(BF16)
        dproj_ref[:, OFF_KA:OFF_VA] = dka.astype(BF16)
        dproj_ref[:, OFF_VA:OFF_QB] = dva.astype(BF16)
        dproj_ref[:, OFF_QB:OFF_KB] = mixer_b_grad(0)
        dproj_ref[:, OFF_KB:OFF_VB] = mixer_b_grad(1)
        dproj_ref[:, OFF_VB:D_IN] = mixer_b_grad(2)
        dxn = _dot_nt(dproj_ref[...], w_ref[...])
        dx, dg = _rms_bwd(x_ref[...], g_ref[...], dxn)
        dg_ref[...] += dg
        dx_ref[...] = dh1_ref[...] + dx

    half, kvw, full = _row_spec(tm, D_A), _row_spec(tm, D_KV_A), _row_spec(tm, D_MODEL)
    per_chunk = tk // tm
    chunk_t = pl.BlockSpec((N_KV_A, None, HEAD_DIM, tm), lambda i: (0, i // per_chunk, 0, i % per_chunk))
    view_specs, _ = _dilated_specs(s_len, tm, BF16)
    return pl.pallas_call(
        body, name="inproj_bwd", grid=(s_len // tm,),
        in_specs=[half, chunk_t, chunk_t] + ([half] + view_specs) * 3
                 + [half, kvw, full, full, _full_spec((1, D_MODEL)), _const_spec((D_MODEL, D_IN)), kvw, kvw,
                    _full_spec((1, D_A)), _full_spec((1, D_KV_A))] + [_full_spec((tm, tm))] * n_d,
        out_specs=[full, _row_spec(tm, D_IN), _full_spec((1, D_MODEL)), _full_spec((1, D_A)), _full_spec((1, D_KV_A))],
        out_shape=(SDS((s_len, D_MODEL), F32), SDS((s_len, D_IN), BF16), SDS((1, D_MODEL), F32), SDS((1, D_A), F32),
                   SDS((1, D_KV_A), F32)),
        compiler_params=_params(1, V7X_VMEM_LIMIT),
    )(dqs, dk_t, dv_t, *parts_b, qa_raw, ka_raw, x, dh1, g_pre, w_in, cos, sin, gq, gk,
      *[_dilation_perm(tm, d).T for d in DILATIONS])


def _weight_grad(a, b, name, tk1, tn, tm):
    s_len, k1 = a.shape
    n = b.shape[1]
    steps = s_len // tm

    def body(a_ref, b_ref, o_ref, acc):
        r = pl.program_id(2)
        _zero_at_first(r == 0, acc)
        acc[...] += _dot_tn(a_ref[...], b_ref[...])

        @pl.when(r == steps - 1)
        def _():
            o_ref[...] = acc[...].astype(BF16)

    return pl.pallas_call(
        body, name=name, grid=(k1 // tk1, n // tn, steps),
        in_specs=[pl.BlockSpec((tm, tk1), lambda i, j, r: (r, i)), pl.BlockSpec((tm, tn), lambda i, j, r: (r, j))],
        out_specs=pl.BlockSpec((tk1, tn), lambda i, j, r: (i, j)),
        out_shape=SDS((k1, n), BF16),
        scratch_shapes=[pltpu.VMEM((tk1, tn), F32)],
        compiler_params=_params(3, V7X_VMEM_LIMIT),
    )(a, b)


def _my_index():
    return 4 * lax.axis_index("x") + 2 * lax.axis_index("y") + lax.axis_index("c")


def _peer(k):
    f = k + 1
    x, y, c = lax.axis_index("x"), lax.axis_index("y"), lax.axis_index("c")
    return (x ^ ((f >> 2) & 1), y ^ ((f >> 1) & 1), c ^ (f & 1))


GATHER, EXCHANGE = "gather", "exchange"
ANY_SPEC = pl.BlockSpec(memory_space=pl.ANY)


def _comm_out_shapes(kind, arrays):
    return [SDS((N_DEV,) + a.shape if kind == GATHER else a.shape, a.dtype) for a in arrays]


def _comm_sems(n):
    return [pltpu.SemaphoreType.DMA((n, N_DEV - 1)), pltpu.SemaphoreType.DMA((n, N_DEV - 1)), pltpu.SemaphoreType.DMA((n,))]


def _comm_copies(kind, ins, outs, send_sems, recv_sems, local_sems):
    me = _my_index()
    local, remote = [], []
    for a, (src, dst) in enumerate(zip(ins, outs)):
        local.append(pltpu.make_async_copy(src if kind == GATHER else src.at[me], dst.at[me], local_sems.at[a]))
        for k in range(N_DEV - 1):
            px, py, pc = _peer(k)
            remote.append(pltpu.make_async_remote_copy(
                src_ref=src if kind == GATHER else src.at[4 * px + 2 * py + pc], dst_ref=dst.at[me],
                send_sem=send_sems.at[a, k], recv_sem=recv_sems.at[a, k],
                device_id=(px, py, pc), device_id_type=pl.DeviceIdType.MESH))
    return local, remote


def _comm_start(kind, ins, outs, sems):
    local, remote = _comm_copies(kind, ins, outs, *sems)
    for cp in local + remote:
        cp.start()


def _comm_wait(kind, ins, outs, sems):
    local, remote = _comm_copies(kind, ins, outs, *sems)
    for cp in local:
        cp.wait()
    for cp in remote:
        cp.wait_send()
    for cp in remote:
        cp.wait_recv()


def _collective(kind, arrays, name):
    n = len(arrays)

    def body(*refs):
        ins, outs, sems = refs[:n], refs[n:2 * n], refs[2 * n:]
        _comm_start(kind, ins, outs, sems)
        _comm_wait(kind, ins, outs, sems)

    return pl.pallas_call(
        body, name=name, in_specs=[ANY_SPEC] * n, out_specs=[ANY_SPEC] * n,
        out_shape=_comm_out_shapes(kind, arrays), scratch_shapes=_comm_sems(n),
    )(*arrays)


def _sum_adamw(parts, w, m, v, name, tr):
    rows, cols = w.shape
    c1 = 1.0 / (1.0 - ADAM_B1 ** ADAM_STEP)
    c2 = 1.0 / (1.0 - ADAM_B2 ** ADAM_STEP)

    def body(p_ref, w_ref, m_ref, v_ref, g_ref, d_ref, nm_ref, nv_ref):
        g = p_ref[0].astype(F32)
        for j in range(1, N_DEV):
            g = g + p_ref[j].astype(F32)
        g_ref[...] = g
        nm = ADAM_B1 * m_ref[...] + (1.0 - ADAM_B1) * g
        nv = ADAM_B2 * v_ref[...] + (1.0 - ADAM_B2) * (g * g)
        nm_ref[...] = nm
        nv_ref[...] = nv
        d_ref[...] = -ADAM_LR * ((nm * c1) / (jnp.sqrt(nv * c2) + ADAM_EPS) + ADAM_WD * w_ref[...])

    blk = pl.BlockSpec((tr, cols), lambda i: (i, 0))
    return pl.pallas_call(
        body, name=name, grid=(rows // tr,),
        in_specs=[pl.BlockSpec((N_DEV, tr, cols), lambda i: (0, i, 0)), blk, blk, blk],
        out_specs=[blk] * 4, out_shape=[SDS((rows, cols), F32)] * 4,
        compiler_params=_params(1, V7X_VMEM_LIMIT),
    )(parts, w, m, v)


_SMALL = (("g_attn_pre", 1024), ("g_q", 64), ("g_k", 64), ("g_out_a", 512), ("g_out_b", 512), ("g_attn_post", 1024),
          ("rel_bias", 256), ("g_mlp_pre", 1024), ("g_mlp_post", 1024), ("g_ple", 1024))
_SLAB_ROWS = 56


def _pack_small(vals):
    rows = []
    for (name, size) in _SMALL:
        flat = vals[name].reshape(-1).astype(F32)
        padded = -(-size // 128) * 128
        rows.append(jnp.pad(flat, (0, padded - size)).reshape(padded // 128, 128))
    slab = jnp.concatenate(rows, axis=0)
    return jnp.pad(slab, ((0, _SLAB_ROWS - slab.shape[0]), (0, 0)))


def _unpack_small(slab, shapes):
    out, row = {}, 0
    for (name, size) in _SMALL:
        nrow = -(-size // 128)
        out[name] = slab[row:row + nrow].reshape(-1)[:size].reshape(shapes[name])
        row += nrow
    return out


def _rope_tables(s_len):
    rows = s_len // GRID_W
    row = jnp.broadcast_to(jnp.arange(rows)[:, None], (rows, GRID_W)).reshape(-1).astype(F32)
    col = jnp.broadcast_to(jnp.arange(GRID_W)[None, :], (rows, GRID_W)).reshape(-1).astype(F32)
    n_axis = ROPE_HALF // 2
    inv_freq = ROPE_THETA ** (-jnp.arange(n_axis, dtype=F32) / n_axis)
    ang = jnp.concatenate([row[:, None] * inv_freq, col[:, None] * inv_freq], axis=-1)
    cos, sin = jnp.cos(ang), jnp.sin(ang)
    cos = jnp.tile(jnp.concatenate([cos, cos], axis=-1), (1, N_KV_A))
    sin = jnp.tile(jnp.concatenate([-sin, sin], axis=-1), (1, N_KV_A))
    return cos, sin


_RIDERS = {1: ("w_out", "w_ff1"), 4: ("w_ff2",), 16: ("w_ple_gate", "w_ple_proj")}
_GRAD_RIDERS = {1: ("w_ff1",), 4: ("w_ff2",), 16: ("w_ple_gate", "w_ple_proj", "w_out")}


ROW_TILE = 256
ATTN_Q_TILE = 128
ATTN_K_CHUNK = 2048
BAND_TILE = 256
GRAD_ROW_TILE = 1024


def _local_step(x, p, target, w_in, shards, small):
    axis_of = dict(_BIG)
    s_len = x.shape[0]
    tm = min(ROW_TILE, s_len)
    tq = min(ATTN_Q_TILE, s_len)
    tk = min(ATTN_K_CHUNK, s_len // 2)
    cos, sin = _rope_tables(s_len)
    gq = jnp.tile(small["g_q"], (1, N_HEADS_A))
    gk = jnp.tile(small["g_k"], (1, N_KV_A))

    qa_raw, ka_raw, qs, k_maj, kt, vt, qb, kb, vb, xn1, *views = _inproj_fwd(
        x, small["g_attn_pre"], w_in, cos, sin, gq, gk, tm, tk)
    n_d = len(DILATIONS)
    qb, kb, vb = ({1: plain, **dict(zip(DILATIONS, views[a * n_d:(a + 1) * n_d]))} for a, plain in enumerate((qb, kb, vb)))

    ya, lse_a = _attn_fwd(qs, k_maj, vt, tq, tk)

    ob, lb, tiles, full = [], [], [], {}
    for (_, dil) in DILATED_PATTERNS:
        t = min(BAND_TILE, s_len // dil)
        bias, bias_t = _bias_tiles(small["rel_bias"], dil, t)
        tiles.append((t, bias, bias_t, _band_buckets(dil, t)))
        o, l, gathered = _band_fwd(qb[dil], kb[dil], vb[dil], bias, dil, t, (GATHER, [shards[n] for n in _RIDERS[dil]]))
        ob.append(o)
        lb.append(l)
        full.update({n: _assemble(g, axis_of[n]) for n, g in zip(_RIDERS[dil], gathered)})
    w_out, w_ff1, w_ff2, w_gate, w_proj = (full[n] for n in ("w_out", "w_ff1", "w_ff2", "w_ple_gate", "w_ple_proj"))

    h1, yo, yb, lse_b, ycat = _attn_out_fwd(ya, ob, lb, x, small["g_out_a"], small["g_out_b"], w_out,
                                            small["g_attn_post"], tm)
    h2, fo, xn2 = _mlp_fwd(h1, small["g_mlp_pre"], w_ff1, w_ff2, small["g_mlp_post"], tm)
    dh2, loss_part, dg_ple, xn3, dgl, dpp, pb = _ple_fwd_bwd(h2, p, target, small["g_ple"], w_gate, w_proj, tm)
    dh1, dfo, du, f, dg_mlp_post, dg_mlp_pre = _mlp_bwd(h1, dh2, fo, xn2, small["g_mlp_pre"], w_ff1, w_ff2,
                                                          small["g_mlp_post"], tm)
    dyo, dya, dyb, delta_a, delta_b, dg_attn_post, dg_out_a, dg_out_b, *dyb_views = _attn_out_bwd(
        dh1, yo, ya, yb, small["g_out_a"], small["g_out_b"], w_out, small["g_attn_post"], tm)
    dyb = {1: dyb, **dict(zip(DILATIONS, dyb_views))}

    dqs, dk_t, dv_t = _attn_bwd(qs, dya, lse_a, delta_a, kt, k_maj, vt, tq, tk)

    tg = min(GRAD_ROW_TILE, s_len)
    grads = {
        "w_out": _weight_grad(ycat, dyo, "grad_w_out", D_MODEL, D_MODEL, tg),
        "w_ff1": _weight_grad(xn2, du, "grad_w_ff1", D_MODEL, 1024, tg),
        "w_ff2": _weight_grad(f, dfo, "grad_w_ff2", 1024, D_MODEL, tg),
        "w_ple_gate": _weight_grad(xn3, dgl, "grad_w_ple_gate", D_MODEL, D_MODEL, tg),
        "w_ple_proj": _weight_grad(pb, dpp, "grad_w_ple_proj", D_PLE, D_MODEL, tg),
    }

    dqb, dkb, dvb, dbias, received = {}, {}, {}, [], {}
    for (_, dil), (t, bias, bias_t, _) in zip(DILATED_PATTERNS, tiles):
        rider = (EXCHANGE, [_cut(grads[n], axis_of[n]) for n in _GRAD_RIDERS[dil]])
        dqb[dil], db, got = _band_dq(qb[dil], dyb[dil], lse_b, delta_b, kb[dil], vb[dil], bias, dil, t, rider)
        dkb[dil], dvb[dil] = _band_dkv(kb[dil], vb[dil], qb[dil], dyb[dil], lse_b, delta_b, bias_t, dil, t)
        dbias.append(db)
        received.update(zip(_GRAD_RIDERS[dil], got))
    d_rel = _dbias_reduce(dbias, [jnp.asarray(tl[3]) for tl in tiles])[:, :N_HEADS_B]

    parts_b = [part[d] for part in (dqb, dkb, dvb) for d in (1,) + DILATIONS]
    dx, dproj, dg_attn_pre, dgq_lanes, dgk_lanes = _inproj_bwd(
        dqs, dk_t, dv_t, parts_b, qa_raw, ka_raw, x, dh1, small["g_attn_pre"], w_in, cos, sin, gq, gk, tm)
    grad_w_in = _weight_grad(xn1, dproj, "grad_w_in", D_MODEL, 768, tg)
    small_grads = {
        "g_attn_pre": dg_attn_pre, "g_q": dgq_lanes.reshape(N_HEADS_A, HEAD_DIM).sum(0, keepdims=True),
        "g_k": dgk_lanes.reshape(N_KV_A, HEAD_DIM).sum(0, keepdims=True), "g_out_a": dg_out_a, "g_out_b": dg_out_b,
        "g_attn_post": dg_attn_post, "rel_bias": d_rel, "g_mlp_pre": dg_mlp_pre, "g_mlp_post": dg_mlp_post,
        "g_ple": dg_ple,
    }
    return loss_part[0, 0], dx, grad_w_in, received, small_grads


_BIG = (("w_in", 1), ("w_out", 0), ("w_ff1", 1), ("w_ff2", 0), ("w_ple_gate", 0), ("w_ple_proj", 1))


def _assemble(gathered, axis):
    if axis == 0:
        return gathered.reshape(-1, gathered.shape[2])
    return gathered.transpose(1, 0, 2).reshape(gathered.shape[1], -1)


def _cut(full, axis):
    if axis == 0:
        return full.reshape(N_DEV, full.shape[0] // N_DEV, full.shape[1])
    return full.reshape(full.shape[0], N_DEV, full.shape[1] // N_DEV).transpose(1, 0, 2)


def kernel(x, p, w_in, g_attn_pre, g_q, g_k, g_out_a, g_out_b, w_out, g_attn_post, rel_bias, g_mlp_pre, w_ff1, w_ff2, g_mlp_post, g_ple, w_ple_gate, w_ple_proj, loss_target, m_w_in, m_g_attn_pre, m_g_q, m_g_k, m_g_out_a, m_g_out_b, m_w_out, m_g_attn_post, m_rel_bias, m_g_mlp_pre, m_w_ff1, m_w_ff2, m_g_mlp_post, m_g_ple, m_w_ple_gate, m_w_ple_proj, v_w_in, v_g_attn_pre, v_g_q, v_g_k, v_g_out_a, v_g_out_b, v_w_out, v_g_attn_post, v_rel_bias, v_g_mlp_pre, v_w_ff1, v_w_ff2, v_g_mlp_post, v_g_ple, v_w_ple_gate, v_w_ple_proj):
    given = dict(locals())
    small_names = [n for n, _ in _SMALL]
    small = {n: given[n] for n in small_names}
    shards = {n: given[n][0] for n, _ in _BIG}

    bf16_shards = {n: shards[n].astype(BF16) for n, _ in _BIG}
    (gathered_w_in,) = _collective(GATHER, [bf16_shards["w_in"]], "gather_w_in")

    loss_part, dx, grad_w_in, received, small_grads = _local_step(
        x[0], p[0, 0], loss_target[0], _assemble(gathered_w_in, 1), bf16_shards, small)

    loss_at = np.zeros((_SLAB_ROWS, 128), bool)
    loss_at[-1, 0] = True
    slab = jnp.where(loss_at, loss_part * (0.5 / D_MODEL), _pack_small(small_grads))
    slab_parts = jnp.broadcast_to(slab[None], (N_DEV,) + slab.shape)
    received["w_in"], slabs = _collective(EXCHANGE, [_cut(grad_w_in, 1), slab_parts], "exchange_w_in_grads")

    out_g, out_d, out_m, out_v = {}, {}, {}, {}
    for n, _ in _BIG:
        rows = shards[n].shape[0]
        g, d, nm, nv = _sum_adamw(received[n], shards[n], given["m_" + n][0], given["v_" + n][0], "adamw_" + n,
                                  min(rows, 128))
        out_g[n], out_d[n], out_m[n], out_v[n] = g[None], d[None], nm[None], nv[None]
    g, d, nm, nv = _sum_adamw(slabs, _pack_small(small), _pack_small({n: given["m_" + n] for n in small_names}),
                              _pack_small({n: given["v_" + n] for n in small_names}), "adamw_small", _SLAB_ROWS)
    shapes = {n: given[n].shape for n in small_names}
    for dst, slab_out in ((out_g, g), (out_d, d), (out_m, nm), (out_v, nv)):
        dst.update(_unpack_small(slab_out, shapes))

    loss = g[-1, 0]
    order = ["w_in", "g_attn_pre", "g_q", "g_k", "g_out_a", "g_out_b", "w_out", "g_attn_post", "rel_bias", "g_mlp_pre",
             "w_ff1", "w_ff2", "g_mlp_post", "g_ple", "w_ple_gate", "w_ple_proj"]
    return (loss, dx[None], *[out_g[n] for n in order], *[out_d[n] for n in order], *[out_m[n] for n in order],
            *[out_v[n] for n in order])
```

```python
import functools
import math

import jax
import jax.numpy as jnp
import numpy as np
from jax import lax
from jax.experimental import pallas as pl
from jax.experimental.pallas import tpu as pltpu

F32 = jnp.float32
BF16 = jnp.bfloat16
SDS = jax.ShapeDtypeStruct

D_MODEL = 1024
HEAD_DIM = 64
N_HEADS_A = 8
N_KV_A = 2
GROUP_A = N_HEADS_A // N_KV_A
N_HEADS_B = 8
D_A = N_HEADS_A * HEAD_DIM
D_KV_A = N_KV_A * HEAD_DIM
D_B = N_HEADS_B * HEAD_DIM
D_IN = D_A + 2 * D_KV_A + 3 * D_B
D_FF = 4 * D_MODEL
D_PLE = 256
GRID_W = 64
ROPE_THETA = 10000.0
ROPE_HALF = HEAD_DIM // 2
DILATED_PATTERNS = ((128, 1), (512, 4), (2048, 16))
BAND_HALF = 64
N_BUCKETS = 32
MAX_DISTANCE = 1024
EPS = 1e-6
NEG_BIG = -1e30
SCORE_SCALE = HEAD_DIM ** -0.5

ADAM_LR = 0.001
ADAM_B1 = 0.9
ADAM_B2 = 0.999
ADAM_EPS = 1e-08
ADAM_WD = 0.01
ADAM_STEP = 10

N_DEV = 8
V7X_VMEM_LIMIT = 56 * 1024 * 1024

OFF_QA, OFF_KA, OFF_VA, OFF_QB, OFF_KB, OFF_VB = 0, 512, 640, 768, 1280, 1792


def _params(n_axes, vmem=None):
    return pltpu.CompilerParams(dimension_semantics=("arbitrary",) * n_axes, vmem_limit_bytes=vmem)


def _dot(a, b):
    return jnp.dot(a, b, preferred_element_type=F32)


def _dot_nt(a, b):
    return lax.dot_general(a, b, (((1,), (1,)), ((), ())), preferred_element_type=F32)


def _dot_tn(a, b):
    return lax.dot_general(a, b, (((0,), (0,)), ((), ())), preferred_element_type=F32)


def _rms_fwd(x, g):
    r = lax.rsqrt(jnp.mean(x * x, axis=-1, keepdims=True) + EPS)
    return x * r * g, r


def _rms_bwd(x, g, dy):
    r = lax.rsqrt(jnp.mean(x * x, axis=-1, keepdims=True) + EPS)
    xh = x * r
    dxh = dy * g
    dx = r * (dxh - xh * jnp.mean(dxh * xh, axis=-1, keepdims=True))
    return dx, jnp.sum(dy * xh, axis=0, keepdims=True)


def _head_sum(v):
    head = lax.broadcasted_iota(jnp.int32, v.shape, 1) >> 6
    out = jnp.zeros_like(v)
    for h in range(v.shape[1] // HEAD_DIM):
        msk = head == h
        s = jnp.sum(jnp.where(msk, v, 0.0), axis=-1, keepdims=True)
        out = jnp.where(msk, s, out)
    return out


def _swap_halves(v):
    w = v.shape[1]
    lane = lax.broadcasted_iota(jnp.int32, v.shape, 1)
    first_half = (lane & (HEAD_DIM - 1)) < ROPE_HALF
    return jnp.where(first_half, pltpu.roll(v, w - ROPE_HALF, 1), pltpu.roll(v, ROPE_HALF, 1))


def _head_norm_rope(v, g, cos, sin_signed):
    r = lax.rsqrt(_head_sum(v * v) * (1.0 / HEAD_DIM) + EPS)
    y = v * r * g
    return y * cos + _swap_halves(y) * sin_signed


def _head_norm_rope_bwd(v, g, cos, sin_signed, dout):
    dy = dout * cos - _swap_halves(dout) * sin_signed
    r = lax.rsqrt(_head_sum(v * v) * (1.0 / HEAD_DIM) + EPS)
    xh = v * r
    dxh = dy * g
    dv = r * (dxh - xh * (_head_sum(dxh * xh) * (1.0 / HEAD_DIM)))
    return dv, jnp.sum(dy * xh, axis=0, keepdims=True)


def _row_spec(tm, n):
    return pl.BlockSpec((tm, n), lambda i: (i, 0))


def _full_spec(shape):
    nd = len(shape)
    return pl.BlockSpec(shape, lambda *_: (0,) * nd)


def _const_spec(shape):
    nd = len(shape)
    return pl.BlockSpec(shape, lambda *_: (0,) * nd, pipeline_mode=pl.Buffered(1))


def _zero_at_first(first, *refs):
    @pl.when(first)
    def _():
        for ref in refs:
            ref[...] = jnp.zeros_like(ref)


DILATIONS = tuple(d for _, d in DILATED_PATTERNS if d > 1)


def _dilation_perm(tm, dil):
    rows = np.arange(tm)
    perm = np.zeros((tm, tm), np.float32)
    perm[(rows % dil) * (tm // dil) + rows // dil, rows] = 1.0
    return jnp.asarray(perm, BF16)


def _store_dilated(ref, perm_ref, val, dil):
    per = val.shape[0] // dil
    sorted_rows = _dot(perm_ref[...], val).astype(BF16)
    for r in range(dil):
        ref[:, r * D_B:(r + 1) * D_B] = sorted_rows[r * per:(r + 1) * per, :]


def _dilated_specs(s_len, tm, dtype):
    specs = [pl.BlockSpec((tm // d, d * D_B), lambda i: (i, 0)) for d in DILATIONS]
    return specs, [SDS((s_len // d, d * D_B), dtype) for d in DILATIONS]


ONES_ROWS = 16


def _inproj_fwd(x, g_pre, w_in, cos, sin, gq, gk, tm, tk):
    s_len = x.shape[0]
    n_d = len(DILATIONS)

    def body(x_ref, g_ref, w_ref, cos_ref, sin_ref, gq_ref, gk_ref, *rest):
        perms, views = rest[:n_d], rest[n_d + 10:]
        qa_raw, ka_raw, qs, k_maj, kt, vt, qb, kb, vb, xn_out = rest[n_d:n_d + 10]
        xn, _ = _rms_fwd(x_ref[...], g_ref[...])
        xn = xn.astype(BF16)
        xn_out[...] = xn
        qa = _dot(xn, w_ref[:, OFF_QA:OFF_KA])
        qa_raw[...] = qa
        cos, sin = cos_ref[...], sin_ref[...]
        qs[...] = (_head_norm_rope(qa, gq_ref[...], jnp.tile(cos, (1, D_A // D_KV_A)), jnp.tile(sin, (1, D_A // D_KV_A)))
                   * SCORE_SCALE).astype(BF16)
        ka = _dot(xn, w_ref[:, OFF_KA:OFF_VA])
        ka_raw[...] = ka
        kn = _head_norm_rope(ka, gk_ref[...], cos, sin).astype(BF16)
        for kv in range(N_KV_A):
            k_maj[kv] = kn[:, HEAD_DIM * kv:HEAD_DIM * (kv + 1)]
        kt[...] = kn.T.reshape(N_KV_A, HEAD_DIM, tm)
        va = _dot(xn, w_ref[:, OFF_VA:OFF_QB]).astype(BF16)
        vt[:, :HEAD_DIM, :] = va.T.reshape(N_KV_A, HEAD_DIM, tm)
        vt[:, HEAD_DIM:, :] = jnp.ones((N_KV_A, ONES_ROWS, tm), BF16)
        mixer_b = ((_dot(xn, w_ref[:, OFF_QB:OFF_KB]) * SCORE_SCALE).astype(BF16),
                   _dot(xn, w_ref[:, OFF_KB:OFF_VB]).astype(BF16), _dot(xn, w_ref[:, OFF_VB:D_IN]).astype(BF16))
        for a, (val, plain) in enumerate(zip(mixer_b, (qb, kb, vb))):
            plain[...] = val
            for j, d in enumerate(DILATIONS):
                _store_dilated(views[a * n_d + j], perms[j], val, d)

    view_specs, view_shapes = _dilated_specs(s_len, tm, BF16)
    per_chunk = tk // tm

    def chunk_t(rows):
        return pl.BlockSpec((N_KV_A, None, rows, tm), lambda i: (0, i // per_chunk, 0, i % per_chunk))

    return pl.pallas_call(
        body, name="inproj_fwd", grid=(s_len // tm,),
        in_specs=[_row_spec(tm, D_MODEL), _full_spec((1, D_MODEL)), _const_spec((D_MODEL, D_IN)),
                  _row_spec(tm, D_KV_A), _row_spec(tm, D_KV_A), _full_spec((1, D_A)), _full_spec((1, D_KV_A))]
                 + [_full_spec((tm, tm))] * n_d,
        out_specs=[_row_spec(tm, D_A), _row_spec(tm, D_KV_A), _row_spec(tm, D_A),
                   pl.BlockSpec((N_KV_A, tm, HEAD_DIM), lambda i: (0, i, 0)), chunk_t(HEAD_DIM),
                   chunk_t(HEAD_DIM + ONES_ROWS), _row_spec(tm, D_B), _row_spec(tm, D_B), _row_spec(tm, D_B),
                   _row_spec(tm, D_MODEL)] + view_specs * 3,
        out_shape=[SDS((s_len, D_A), F32), SDS((s_len, D_KV_A), F32), SDS((s_len, D_A), BF16),
                   SDS((N_KV_A, s_len, HEAD_DIM), BF16), SDS((N_KV_A, s_len // tk, HEAD_DIM, tk), BF16),
                   SDS((N_KV_A, s_len // tk, HEAD_DIM + ONES_ROWS, tk), BF16), SDS((s_len, D_B), BF16),
                   SDS((s_len, D_B), BF16), SDS((s_len, D_B), BF16), SDS((s_len, D_MODEL), BF16)] + view_shapes * 3,
        compiler_params=_params(1, V7X_VMEM_LIMIT),
    )(x, g_pre, w_in, cos, sin, gq, gk, *[_dilation_perm(tm, d) for d in DILATIONS])


def _stack_heads(ref, tq):
    return jnp.concatenate([ref[:, HEAD_DIM * g:HEAD_DIM * (g + 1)] for g in range(GROUP_A)], axis=0)


def _stack_cols(ref, tq):
    return jnp.concatenate([ref[:, HEAD_DIM * g:HEAD_DIM * g + 1] for g in range(GROUP_A)], axis=0)


def _attn_fwd(qs, k, vt, tq, tk):
    s_len = qs.shape[0]
    nk = s_len // tk
    assert nk % 2 == 0
    gw = GROUP_A * HEAD_DIM
    rows = GROUP_A * tq
    vrows = vt.shape[2]

    def body(q_ref, k_ref, vt_ref, o_ref, lse_ref, s_buf):
        qt = _stack_heads(q_ref, tq).T

        def scores(j, slot):
            kj = k_ref[pl.ds(pl.multiple_of(j * tk, tk), tk), :]
            s_buf[slot] = _dot(kj, qt)

        def consume(j, slot, carry):
            m, acc = carry
            st = s_buf[slot]
            m_new = jnp.maximum(m, jnp.max(st, axis=0, keepdims=True))
            pt = jnp.exp(st - m_new)
            acc = jnp.exp(m - m_new) * acc + _dot(vt_ref[j], pt.astype(BF16))
            return m_new, acc

        scores(0, 0)

        def pair(j, carry, more):
            scores(j + 1, 1)
            carry = consume(j, 0, carry)
            if more:
                scores(j + 2, 0)
            return consume(j + 1, 1, carry)

        carry = (jnp.full((1, rows), NEG_BIG, F32), jnp.zeros((vrows, rows), F32))
        carry = lax.fori_loop(0, nk // 2 - 1, lambda jj, c: pair(2 * jj, c, True), carry)
        m, acc = pair(nk - 2, carry, False)
        l = acc[HEAD_DIM:HEAD_DIM + 1]
        o = (acc[:HEAD_DIM] / l).T
        lse = jnp.broadcast_to(m + jnp.log(l), (HEAD_DIM, rows)).T
        for g in range(GROUP_A):
            o_ref[:, HEAD_DIM * g:HEAD_DIM * (g + 1)] = o[g * tq:(g + 1) * tq]
            lse_ref[:, HEAD_DIM * g:HEAD_DIM * (g + 1)] = lse[g * tq:(g + 1) * tq]

    tile = pl.BlockSpec((tq, gw), lambda kv, i: (i, kv))
    return pl.pallas_call(
        body, name="attn_fwd", grid=(N_KV_A, s_len // tq),
        in_specs=[tile, pl.BlockSpec((None, s_len, HEAD_DIM), lambda kv, i: (kv, 0, 0)),
                  pl.BlockSpec((None, nk, vrows, tk), lambda kv, i: (kv, 0, 0, 0))],
        out_specs=[tile, tile],
        out_shape=(SDS((s_len, D_A), F32), SDS((s_len, D_A), F32)),
        scratch_shapes=[pltpu.VMEM((2, tk, rows), F32)],
        compiler_params=_params(2, V7X_VMEM_LIMIT),
    )(qs, k, vt)


def _attn_bwd(qs, do, lse, delta, kt, k, vt, tq, tk):
    s_len = qs.shape[0]
    nk = s_len // tk
    assert nk % 2 == 0
    nq = s_len // tq
    gw = GROUP_A * HEAD_DIM
    rows = GROUP_A * tq

    def body(q_ref, do_ref, lse_ref, delta_ref, kt_ref, k_ref, vt_ref, dq_ref, dk_hbm, dv_hbm,
             dk_acc, dv_acc, s_buf, dp_buf):
        kv = pl.program_id(0)
        i = pl.program_id(1)

        @pl.when(i == 0)
        def _():
            dk_acc[...] = jnp.zeros_like(dk_acc)
            dv_acc[...] = jnp.zeros_like(dv_acc)

        q = _stack_heads(q_ref, tq)
        dout = _stack_heads(do_ref, tq)
        qt = q.T
        doutt = dout.T
        row_lse = _stack_cols(lse_ref, tq)
        row_delta = _stack_cols(delta_ref, tq)

        def scores(j, slot):
            s_buf[slot] = _dot(q, kt_ref[j])
            dp_buf[slot] = _dot(dout, vt_ref[j, :HEAD_DIM, :])

        def consume(j, slot, dq):
            p = jnp.exp(s_buf[slot] - row_lse)
            ds = (p * (dp_buf[slot] - row_delta)).astype(BF16)
            dv_acc[j] += _dot(doutt, p.astype(BF16))
            dk_acc[j] += _dot(qt, ds)
            return dq + _dot(ds, k_ref[pl.ds(pl.multiple_of(j * tk, tk), tk), :])

        scores(0, 0)

        def pair(j, dq, more):
            scores(j + 1, 1)
            dq = consume(j, 0, dq)
            if more:
                scores(j + 2, 0)
            return consume(j + 1, 1, dq)

        dq = lax.fori_loop(0, nk // 2 - 1, lambda jj, c: pair(2 * jj, c, True), jnp.zeros((rows, HEAD_DIM), F32))
        dq = pair(nk - 2, dq, False) * SCORE_SCALE
        for g in range(GROUP_A):
            dq_ref[:, HEAD_DIM * g:HEAD_DIM * (g + 1)] = dq[g * tq:(g + 1) * tq]

        @pl.when(i == nq - 1)
        def _():
            pltpu.sync_copy(dk_acc, dk_hbm.at[kv])
            pltpu.sync_copy(dv_acc, dv_hbm.at[kv])

    tile = pl.BlockSpec((tq, gw), lambda kv, i: (i, kv))
    chunks = pl.BlockSpec((None, nk, HEAD_DIM, tk), lambda kv, i: (kv, 0, 0, 0))
    grad_t = SDS((N_KV_A, nk, HEAD_DIM, tk), F32)
    return pl.pallas_call(
        body, name="attn_bwd", grid=(N_KV_A, nq),
        in_specs=[tile, tile, tile, tile, chunks,
                  pl.BlockSpec((None, s_len, HEAD_DIM), lambda kv, i: (kv, 0, 0)),
                  pl.BlockSpec((None, nk, vt.shape[2], tk), lambda kv, i: (kv, 0, 0, 0))],
        out_specs=[tile, pl.BlockSpec(memory_space=pl.ANY), pl.BlockSpec(memory_space=pl.ANY)],
        out_shape=(SDS((s_len, D_A), F32), grad_t, grad_t),
        scratch_shapes=[pltpu.VMEM((nk, HEAD_DIM, tk), F32), pltpu.VMEM((nk, HEAD_DIM, tk), F32),
                        pltpu.VMEM((2, rows, tk), F32), pltpu.VMEM((2, rows, tk), F32)],
        compiler_params=_params(2, V7X_VMEM_LIMIT),
    )(qs, do, lse, delta, kt, k, vt)


STAT_W = 128


def _band_specs(length, t, width=D_B):
    hb = t // BAND_HALF
    last = length // BAND_HALF - 1
    main = pl.BlockSpec((t, width), lambda r, i: (i, r))
    prev = pl.BlockSpec((BAND_HALF, width), lambda r, i: (jnp.maximum(i * hb - 1, 0), r))
    nxt = pl.BlockSpec((BAND_HALF, width), lambda r, i: (jnp.minimum((i + 1) * hb, last), r))
    return main, [prev, main, nxt]


def _pack_heads(cols):
    lane = lax.broadcasted_iota(jnp.int32, (cols[0].shape[0], STAT_W), 1)
    out = jnp.zeros((cols[0].shape[0], STAT_W), F32)
    for h, c in enumerate(cols):
        out = jnp.where(lane == h, c, out)
    return out


def _spread_heads(stat):
    head = lax.broadcasted_iota(jnp.int32, (stat.shape[0], D_B), 1) >> 6
    out = jnp.zeros((stat.shape[0], D_B), F32)
    for h in range(N_HEADS_B):
        out = jnp.where(head == h, stat[:, h:h + 1], out)
    return out


def _window(refs):
    return jnp.concatenate([r[...] for r in refs], axis=0)


def _head(v, h):
    return v[:, HEAD_DIM * h:HEAD_DIM * (h + 1)]


def _call_with_rider(body, name, grid, in_specs, out_specs, out_shape, scratch_shapes, args, rider):
    n_in, n_out = len(in_specs), len(out_specs)
    if rider is None:
        res = pl.pallas_call(body, name=name, grid=grid, in_specs=in_specs, out_specs=out_specs, out_shape=out_shape,
                             scratch_shapes=scratch_shapes, compiler_params=_params(len(grid), V7X_VMEM_LIMIT))(*args)
        return res, []
    kind, arrays = rider
    n = len(arrays)

    def with_rider(*refs):
        ins, c_ins = refs[:n_in], refs[n_in:n_in + n]
        outs, c_outs = refs[n_in + n:n_in + n + n_out], refs[n_in + n + n_out:n_in + 2 * n + n_out]
        rest = refs[n_in + 2 * n + n_out:]
        scratch, sems = rest[:-3], rest[-3:]
        ids = [pl.program_id(a) for a in range(len(grid))]
        first = functools.reduce(jnp.logical_and, [i == 0 for i in ids])
        last = functools.reduce(jnp.logical_and, [i == g - 1 for i, g in zip(ids, grid)])

        @pl.when(first)
        def _():
            _comm_start(kind, c_ins, c_outs, sems)

        body(*ins, *outs, *scratch)

        @pl.when(last)
        def _():
            _comm_wait(kind, c_ins, c_outs, sems)

    res = pl.pallas_call(
        with_rider, name=name, grid=grid, in_specs=list(in_specs) + [ANY_SPEC] * n,
        out_specs=list(out_specs) + [ANY_SPEC] * n, out_shape=list(out_shape) + _comm_out_shapes(kind, arrays),
        scratch_shapes=list(scratch_shapes) + _comm_sems(n), compiler_params=_params(len(grid), V7X_VMEM_LIMIT),
    )(*args, *arrays)
    return res[:n_out], res[n_out:]


def _band_fwd(q, k, v, bias, dil, t, rider=None):
    s_len = q.size // D_B
    length = s_len // dil
    w = t + 2 * BAND_HALF
    view = lambda a: a.reshape(length, dil * D_B)
    main, win = _band_specs(length, t)
    stat, _ = _band_specs(length, t, STAT_W)

    def body(q_ref, k0, k1, k2, v0, v1, v2, bias_ref, o_ref, lse_ref, s_buf, kt_buf, o_buf):
        lses = []
        i = pl.program_id(1)
        kt_buf[...] = _window((k0, k1, k2)).T
        vw = _window((v0, v1, v2))
        pos = i * t - BAND_HALF + lax.broadcasted_iota(jnp.int32, (1, w), 1)
        valid = (pos >= 0) & (pos < length)
        for h in range(N_HEADS_B):
            s_buf[h] = _dot(q_ref[:, HEAD_DIM * h:HEAD_DIM * (h + 1)], kt_buf[HEAD_DIM * h:HEAD_DIM * (h + 1), :])
        for h in range(N_HEADS_B):
            s = jnp.where(valid, s_buf[h] + bias_ref[h], NEG_BIG)
            m = jnp.max(s, axis=-1, keepdims=True)
            e = jnp.exp(s - m)
            den = jnp.sum(e, axis=-1, keepdims=True)
            o_buf[:, HEAD_DIM * h:HEAD_DIM * (h + 1)] = _dot(e.astype(BF16), _head(vw, h)) / den
            lses.append(m + jnp.log(den))
        o_ref[...] = o_buf[...].astype(BF16)
        lse_ref[...] = _pack_heads(lses)

    (o, lse), rode = _call_with_rider(
        body, f"band_fwd_d{dil}", (dil, length // t),
        [main] + win + win + [_full_spec((N_HEADS_B, t, w))], [main, stat],
        [SDS((length, dil * D_B), BF16), SDS((length, dil * STAT_W), F32)],
        [pltpu.VMEM((N_HEADS_B, t, w), F32), pltpu.VMEM((D_B, w), BF16), pltpu.VMEM((t, D_B), F32)],
        (view(q), view(k), view(k), view(k), view(v), view(v), view(v), bias), rider)
    return o, lse.reshape(s_len, STAT_W), rode


def _band_dq(q, do, lse, delta, k, v, bias, dil, t, rider=None):
    s_len = q.size // D_B
    length = s_len // dil
    w = t + 2 * BAND_HALF
    view = lambda a: a.reshape(length, dil * D_B)
    main, win = _band_specs(length, t)

    def body(q_ref, do_ref, lse_ref, delta_ref, k0, k1, k2, v0, v1, v2, bias_ref, dq_ref, dbias_ref,
             s_buf, dp_buf, kt_buf, vt_buf, dq_buf):
        i = pl.program_id(1)
        @pl.when((pl.program_id(0) == 0) & (i == 0))
        def _():
            dbias_ref[...] = jnp.zeros_like(dbias_ref)

        kw = _window((k0, k1, k2))
        kt_buf[...] = kw.T
        vt_buf[...] = _window((v0, v1, v2)).T
        pos = i * t - BAND_HALF + lax.broadcasted_iota(jnp.int32, (1, w), 1)
        valid = (pos >= 0) & (pos < length)
        for h in range(N_HEADS_B):
            cols = slice(HEAD_DIM * h, HEAD_DIM * (h + 1))
            s_buf[h] = _dot(q_ref[:, cols], kt_buf[cols, :])
            dp_buf[h] = _dot(do_ref[:, cols], vt_buf[cols, :])
        for h in range(N_HEADS_B):
            cols = slice(HEAD_DIM * h, HEAD_DIM * (h + 1))
            s = jnp.where(valid, s_buf[h] + bias_ref[h], NEG_BIG)
            p = jnp.exp(s - lse_ref[:, h:h + 1])
            ds = p * (dp_buf[h] - delta_ref[:, h:h + 1])
            dq_buf[:, cols] = _dot(ds.astype(BF16), _head(kw, h)) * SCORE_SCALE
            dbias_ref[h] += ds
        dq_ref[...] = dq_buf[...].astype(BF16)

    stat, _ = _band_specs(length, t, STAT_W)
    sview = lambda a: a.reshape(length, dil * STAT_W)
    (dq, dbias), rode = _call_with_rider(
        body, f"band_dq_d{dil}", (dil, length // t),
        [main, main, stat, stat] + win + win + [_full_spec((N_HEADS_B, t, w))],
        [main, _full_spec((N_HEADS_B, t, w))],
        [SDS((length, dil * D_B), BF16), SDS((N_HEADS_B, t, w), F32)],
        [pltpu.VMEM((N_HEADS_B, t, w), F32), pltpu.VMEM((N_HEADS_B, t, w), F32),
         pltpu.VMEM((D_B, w), BF16), pltpu.VMEM((D_B, w), BF16), pltpu.VMEM((t, D_B), F32)],
        (view(q), view(do), sview(lse), sview(delta), view(k), view(k), view(k), view(v), view(v), view(v), bias),
        rider)
    return dq, dbias, rode


def _band_dkv(k, v, q, do, lse, delta, bias_t, dil, t):
    s_len = q.size // D_B
    length = s_len // dil
    w = t + 2 * BAND_HALF
    view = lambda a: a.reshape(length, dil * D_B)
    main, win = _band_specs(length, t)

    def body(k_ref, v_ref, q0, q1, q2, d0, d1, d2, l0, l1, l2, e0, e1, e2, bias_ref, dk_ref, dv_ref,
             s_buf, dp_buf, kt_buf, vt_buf):
        i = pl.program_id(1)
        qw = _window((q0, q1, q2))
        dow = _window((d0, d1, d2))
        lsew = _window((l0, l1, l2))
        deltaw = _window((e0, e1, e2))
        pos = i * t - BAND_HALF + lax.broadcasted_iota(jnp.int32, (w, 1), 0)
        valid = (pos >= 0) & (pos < length)
        kt_buf[...] = k_ref[...].T
        vt_buf[...] = v_ref[...].T
        for h in range(N_HEADS_B):
            cols = slice(HEAD_DIM * h, HEAD_DIM * (h + 1))
            s_buf[h] = _dot(_head(qw, h), kt_buf[cols, :])
            dp_buf[h] = _dot(_head(dow, h), vt_buf[cols, :])
        qwt = qw.T
        dowt = dow.T
        dkt, dvt = [], []
        for h in range(N_HEADS_B):
            rows = slice(HEAD_DIM * h, HEAD_DIM * (h + 1))
            s = jnp.where(valid, s_buf[h] + bias_ref[h], NEG_BIG)
            p = jnp.exp(s - lsew[:, h:h + 1])
            ds = p * (dp_buf[h] - deltaw[:, h:h + 1])
            dvt.append(_dot(dowt[rows, :], p.astype(BF16)))
            dkt.append(_dot(qwt[rows, :], ds.astype(BF16)))
        dv_ref[...] = jnp.concatenate(dvt, axis=0).T.astype(BF16)
        dk_ref[...] = jnp.concatenate(dkt, axis=0).T.astype(BF16)

    _, swin = _band_specs(length, t, STAT_W)
    sview = lambda a: a.reshape(length, dil * STAT_W)
    dk, dv = pl.pallas_call(
        body, name=f"band_dkv_d{dil}", grid=(dil, length // t),
        in_specs=[main, main] + win + win + swin + swin + [_full_spec((N_HEADS_B, w, t))],
        out_specs=[main, main],
        out_shape=(SDS((length, dil * D_B), BF16), SDS((length, dil * D_B), BF16)),
        scratch_shapes=[pltpu.VMEM((N_HEADS_B, w, t), F32), pltpu.VMEM((N_HEADS_B, w, t), F32),
                        pltpu.VMEM((D_B, t), BF16), pltpu.VMEM((D_B, t), BF16)],
        compiler_params=_params(2, V7X_VMEM_LIMIT),
    )(view(k), view(v), view(q), view(q), view(q), view(do), view(do), view(do), sview(lse), sview(lse), sview(lse),
      sview(delta), sview(delta), sview(delta), bias_t)
    return dk, dv


def _t5_bucket_np(rel):
    nb = N_BUCKETS // 2
    max_exact = nb // 2
    side = np.where(rel > 0, nb, 0)
    n = np.abs(rel)
    ratio = np.maximum(n, max_exact).astype(np.float32) / np.float32(max_exact)
    large = max_exact + (np.log(ratio) / np.float32(math.log(MAX_DISTANCE / max_exact))
                         * np.float32(nb - max_exact)).astype(np.int32)
    large = np.minimum(large, nb - 1)
    return (side + np.where(n < max_exact, n, large)).astype(np.int32)


def _band_buckets(dil, t):
    rel = np.arange(t + 2 * BAND_HALF)[None, :] - BAND_HALF - np.arange(t)[:, None]
    bucket = _t5_bucket_np(np.clip(rel, -BAND_HALF, BAND_HALF) * dil)
    return np.where(np.abs(rel) <= BAND_HALF, bucket, -1).astype(np.int32)


def _toeplitz(vals, rows, cols):
    heads = vals.shape[0]
    period = -(-(rows + cols) // 128) * 128 + 1
    gap = period - (rows + cols - 1)
    vec = jnp.concatenate([vals[:, rows - 1:], jnp.zeros((heads, gap), vals.dtype), vals[:, :rows - 1]], axis=1)
    flat = jnp.broadcast_to(vec[:, None, :], (heads, rows, period)).reshape(heads, rows * period)
    return flat[:, :rows * (period - 1)].reshape(heads, rows, period - 1)[:, :, :cols]


def _bias_tiles(rel_bias, dil, t):
    w = t + 2 * BAND_HALF
    rel = np.arange(-BAND_HALF, BAND_HALF + 1)
    bucket = _t5_bucket_np(rel * dil)
    runs, start = [], 0
    for i in range(1, len(bucket) + 1):
        if i == len(bucket) or bucket[i] != bucket[start]:
            b = int(bucket[start])
            runs.append(jnp.broadcast_to(rel_bias[b:b + 1], (i - start, N_HEADS_B)))
            start = i
    per_rel = jnp.concatenate(runs, axis=0).T

    def diagonals(lo, hi):
        left = jnp.full((N_HEADS_B, max(0, -BAND_HALF - lo)), NEG_BIG, F32)
        right = jnp.full((N_HEADS_B, max(0, hi - BAND_HALF)), NEG_BIG, F32)
        return jnp.concatenate([left, per_rel, right], axis=1)

    tile = _toeplitz(diagonals(-(t - 1) - BAND_HALF, w - 1 - BAND_HALF), t, w)
    twin = _toeplitz(diagonals(-(w - 1) + BAND_HALF, t - 1 + BAND_HALF), w, t)
    return tile, twin


def _dbias_reduce(dbias, buckets):
    n = len(dbias)

    def body(*refs):
        db_refs, bk_refs, o_ref = refs[:n], refs[n:2 * n], refs[2 * n]
        row = lax.broadcasted_iota(jnp.int32, (N_BUCKETS, 128), 0)
        lane = lax.broadcasted_iota(jnp.int32, (N_BUCKETS, 128), 1)

        def per_bucket(b, out):
            for pat in range(n):
                msk = bk_refs[pat][...] == b
                for h in range(N_HEADS_B):
                    tot = jnp.sum(jnp.where(msk, db_refs[pat][h], 0.0), axis=-1, keepdims=True)
                    tot = jnp.sum(tot, axis=0, keepdims=True)
                    out = out + jnp.where((row == b) & (lane == h), tot, 0.0)
            return out

        o_ref[...] = lax.fori_loop(0, N_BUCKETS, per_bucket, jnp.zeros((N_BUCKETS, 128), F32))

    return pl.pallas_call(
        body, name="dbias_reduce", out_shape=SDS((N_BUCKETS, 128), F32),
        compiler_params=pltpu.CompilerParams(vmem_limit_bytes=V7X_VMEM_LIMIT),
    )(*dbias, *buckets)


def _attn_out_fwd(ya, ob, lb, x, g_a, g_b, w_out, g_post, tm):
    s_len = ya.shape[0]
    assert len(ob) == 1 + len(DILATIONS) == 3

    def body(ya_ref, o0, o1, o2, l0, l1, l2, x_ref, ga_ref, gb_ref, w_ref, gp_ref, unsort1, unsort2,
             h1_ref, yo_ref, yb_ref, lse_ref, ycat_ref):
        def plain_rows(o_ref, unsort_ref, dil):
            sorted_rows = jnp.concatenate([o_ref[:, r * D_B:(r + 1) * D_B] for r in range(dil)], axis=0)
            return _dot(unsort_ref[...], sorted_rows)

        m = jnp.maximum(jnp.maximum(l0[...], l1[...]), l2[...])
        w0, w1, w2 = jnp.exp(l0[...] - m), jnp.exp(l1[...] - m), jnp.exp(l2[...] - m)
        wsum = w0 + w1 + w2
        yb = (_spread_heads(w0 / wsum) * o0[...].astype(F32)
              + _spread_heads(w1 / wsum) * plain_rows(o1, unsort1, DILATIONS[0])
              + _spread_heads(w2 / wsum) * plain_rows(o2, unsort2, DILATIONS[1]))
        yb_ref[...] = yb
        lse_ref[...] = m + jnp.log(wsum)
        yan, _ = _rms_fwd(ya_ref[...], ga_ref[...])
        ybn, _ = _rms_fwd(yb, gb_ref[...])
        yan, ybn = yan.astype(BF16), ybn.astype(BF16)
        ycat_ref[:, :D_A] = yan
        ycat_ref[:, D_A:] = ybn
        yo = _dot(yan, w_ref[:D_A, :]) + _dot(ybn, w_ref[D_A:, :])
        yo_ref[...] = yo
        post, _ = _rms_fwd(yo, gp_ref[...])
        h1_ref[...] = x_ref[...] + post

    half, full, stat = _row_spec(tm, D_A), _row_spec(tm, D_MODEL), _row_spec(tm, STAT_W)
    view_specs, _ = _dilated_specs(s_len, tm, BF16)
    return pl.pallas_call(
        body, name="attn_out_fwd", grid=(s_len // tm,),
        in_specs=[half] * 2 + view_specs + [stat] * 3
                 + [full, _full_spec((1, D_A)), _full_spec((1, D_B)), _full_spec((D_MODEL, D_MODEL)),
                    _full_spec((1, D_MODEL))] + [_full_spec((tm, tm))] * len(DILATIONS),
        out_specs=[full, full, half, stat, full],
        out_shape=(SDS((s_len, D_MODEL), F32), SDS((s_len, D_MODEL), F32), SDS((s_len, D_B), F32),
                   SDS((s_len, STAT_W), F32), SDS((s_len, D_MODEL), BF16)),
        compiler_params=_params(1, V7X_VMEM_LIMIT),
    )(ya, ob[0], ob[1], ob[2], lb[0], lb[1], lb[2], x, g_a, g_b, w_out, g_post,
      *[_dilation_perm(tm, d).T for d in DILATIONS])


def _attn_out_bwd(dh1, yo, ya, yb, g_a, g_b, w_out, g_post, tm):
    s_len = ya.shape[0]
    n_d = len(DILATIONS)

    def body(dh1_ref, yo_ref, ya_ref, yb_ref, ga_ref, gb_ref, w_ref, gp_ref, *rest):
        perms, views = rest[:n_d], rest[n_d + 8:]
        dyo_ref, dya_ref, dyb_ref, dela_ref, delb_ref, dgp_ref, dga_ref, dgb_ref = rest[n_d:n_d + 8]
        _zero_at_first(pl.program_id(0) == 0, dgp_ref, dga_ref, dgb_ref)
        dyo, dgp = _rms_bwd(yo_ref[...], gp_ref[...], dh1_ref[...])
        dyo = dyo.astype(BF16)
        dyo_ref[...] = dyo
        dgp_ref[...] += dgp
        dya_n = _dot_nt(dyo, w_ref[:D_A, :])
        dyb_n = _dot_nt(dyo, w_ref[D_A:, :])
        ya, yb = ya_ref[...], yb_ref[...]
        dya, dga = _rms_bwd(ya, ga_ref[...], dya_n)
        dyb, dgb = _rms_bwd(yb, gb_ref[...], dyb_n)
        dga_ref[...] += dga
        dgb_ref[...] += dgb
        dya_ref[...] = dya.astype(BF16)
        dyb_ref[...] = dyb.astype(BF16)
        for j, d in enumerate(DILATIONS):
            _store_dilated(views[j], perms[j], dyb.astype(BF16), d)
        dela_ref[...] = _head_sum(dya * ya)
        prod = dyb * yb
        head = lax.broadcasted_iota(jnp.int32, prod.shape, 1) >> 6
        delb_ref[...] = _pack_heads([jnp.sum(jnp.where(head == h, prod, 0.0), axis=-1, keepdims=True)
                                     for h in range(N_HEADS_B)])

    half, full = _row_spec(tm, D_A), _row_spec(tm, D_MODEL)
    view_specs, view_shapes = _dilated_specs(s_len, tm, BF16)
    return pl.pallas_call(
        body, name="attn_out_bwd", grid=(s_len // tm,),
        in_specs=[full, full, half, half, _full_spec((1, D_A)), _full_spec((1, D_B)),
                  _full_spec((D_MODEL, D_MODEL)), _full_spec((1, D_MODEL))] + [_full_spec((tm, tm))] * n_d,
        out_specs=[full, half, half, half, _row_spec(tm, STAT_W), _full_spec((1, D_MODEL)), _full_spec((1, D_A)),
                   _full_spec((1, D_B))] + view_specs,
        out_shape=[SDS((s_len, D_MODEL), BF16), SDS((s_len, D_A), BF16), SDS((s_len, D_B), BF16),
                   SDS((s_len, D_A), F32), SDS((s_len, STAT_W), F32), SDS((1, D_MODEL), F32), SDS((1, D_A), F32),
                   SDS((1, D_B), F32)] + view_shapes,
        compiler_params=_params(1, V7X_VMEM_LIMIT),
    )(dh1, yo, ya, yb, g_a, g_b, w_out, g_post, *[_dilation_perm(tm, d) for d in DILATIONS])


FF_CHUNK = 1024


def _mlp_fwd(h1, g_pre, w1, w2, g_post, tm):
    s_len = h1.shape[0]

    def body(h1_ref, gpre_ref, w1_ref, w2_ref, gpost_ref, h2_ref, fo_ref, xn_ref):
        h1v = h1_ref[...]
        xn, _ = _rms_fwd(h1v, gpre_ref[...])
        xn = xn.astype(BF16)
        xn_ref[...] = xn
        fo = jnp.zeros((tm, D_MODEL), F32)
        for c in range(D_FF // FF_CHUNK):
            cols = slice(c * FF_CHUNK, (c + 1) * FF_CHUNK)
            u = jnp.maximum(_dot(xn, w1_ref[:, cols]), 0.0)
            fo = fo + _dot((u * u).astype(BF16), w2_ref[cols, :])
        fo_ref[...] = fo
        post, _ = _rms_fwd(fo, gpost_ref[...])
        h2_ref[...] = h1v + post

    full = _row_spec(tm, D_MODEL)
    return pl.pallas_call(
        body, name="mlp_fwd", grid=(s_len // tm,),
        in_specs=[full, _full_spec((1, D_MODEL)), _const_spec((D_MODEL, D_FF)), _const_spec((D_FF, D_MODEL)),
                  _full_spec((1, D_MODEL))],
        out_specs=[full, full, full],
        out_shape=(SDS((s_len, D_MODEL), F32), SDS((s_len, D_MODEL), F32), SDS((s_len, D_MODEL), BF16)),
        compiler_params=_params(1, V7X_VMEM_LIMIT),
    )(h1, g_pre, w1, w2, g_post)


def _mlp_bwd(h1, dh2, fo, xn, g_pre, w1, w2, g_post, tm):
    s_len = h1.shape[0]

    def body(h1_ref, dh2_ref, fo_ref, xn_ref, gpre_ref, w1_ref, w2_ref, gpost_ref,
             dh1_ref, dfo_ref, du_ref, f_ref, dgpost_ref, dgpre_ref):
        _zero_at_first(pl.program_id(0) == 0, dgpost_ref, dgpre_ref)
        dh2 = dh2_ref[...]
        dfo, dgpost = _rms_bwd(fo_ref[...], gpost_ref[...], dh2)
        dfo = dfo.astype(BF16)
        dfo_ref[...] = dfo
        dgpost_ref[...] += dgpost
        xn = xn_ref[...]
        dxn = jnp.zeros((tm, D_MODEL), F32)
        for c in range(D_FF // FF_CHUNK):
            cols = slice(c * FF_CHUNK, (c + 1) * FF_CHUNK)
            u = jnp.maximum(_dot(xn, w1_ref[:, cols]), 0.0)
            f_ref[:, cols] = (u * u).astype(BF16)
            du = (_dot_nt(dfo, w2_ref[cols, :]) * (2.0 * u)).astype(BF16)
            du_ref[:, cols] = du
            dxn = dxn + _dot_nt(du, w1_ref[:, cols])
        dx, dgpre = _rms_bwd(h1_ref[...], gpre_ref[...], dxn)
        dgpre_ref[...] += dgpre
        dh1_ref[...] = dh2 + dx

    full, wide = _row_spec(tm, D_MODEL), _row_spec(tm, D_FF)
    return pl.pallas_call(
        body, name="mlp_bwd", grid=(s_len // tm,),
        in_specs=[full, full, full, full, _full_spec((1, D_MODEL)), _const_spec((D_MODEL, D_FF)),
                  _const_spec((D_FF, D_MODEL)), _full_spec((1, D_MODEL))],
        out_specs=[full, full, wide, wide, _full_spec((1, D_MODEL)), _full_spec((1, D_MODEL))],
        out_shape=(SDS((s_len, D_MODEL), F32), SDS((s_len, D_MODEL), BF16), SDS((s_len, D_FF), BF16),
                   SDS((s_len, D_FF), BF16), SDS((1, D_MODEL), F32), SDS((1, D_MODEL), F32)),
        compiler_params=_params(1, V7X_VMEM_LIMIT),
    )(h1, dh2, fo, xn, g_pre, w1, w2, g_post)


def _ple_fwd_bwd(h2, p, target, g_ple, w_gate, w_proj, tm):
    s_len = h2.shape[0]

    def body(h2_ref, p_ref, t_ref, g_ref, wg_ref, wp_ref, dh2_ref, loss_ref, dg_ref, xn_ref, dgl_ref, dpp_ref, pb_ref):
        _zero_at_first(pl.program_id(0) == 0, loss_ref, dg_ref)
        h2 = h2_ref[...]
        g = g_ref[...]
        xn, _ = _rms_fwd(h2, g)
        xn = xn.astype(BF16)
        xn_ref[...] = xn
        gate = 1.0 / (1.0 + jnp.exp(-_dot(xn, wg_ref[...])))
        pb = p_ref[...].astype(BF16)
        pb_ref[...] = pb
        pp = _dot(pb, wp_ref[...])
        diff = h2 + gate * pp - t_ref[...]
        loss_ref[...] += jnp.full((8, 128), jnp.sum(diff * diff), F32)
        dh3 = diff * (1.0 / D_MODEL)
        dpp_ref[...] = (dh3 * gate).astype(BF16)
        dgl = (dh3 * pp * gate * (1.0 - gate)).astype(BF16)
        dgl_ref[...] = dgl
        dx, dg = _rms_bwd(h2, g, _dot_nt(dgl, wg_ref[...]))
        dg_ref[...] += dg
        dh2_ref[...] = dh3 + dx

    full = _row_spec(tm, D_MODEL)
    return pl.pallas_call(
        body, name="ple_fwd_bwd", grid=(s_len // tm,),
        in_specs=[full, _row_spec(tm, D_PLE), full, _full_spec((1, D_MODEL)), _full_spec((D_MODEL, D_MODEL)),
                  _full_spec((D_PLE, D_MODEL))],
        out_specs=[full, _full_spec((8, 128)), _full_spec((1, D_MODEL)), full, full, full, _row_spec(tm, D_PLE)],
        out_shape=(SDS((s_len, D_MODEL), F32), SDS((8, 128), F32), SDS((1, D_MODEL), F32), SDS((s_len, D_MODEL), BF16),
                   SDS((s_len, D_MODEL), BF16), SDS((s_len, D_MODEL), BF16), SDS((s_len, D_PLE), BF16)),
        compiler_params=_params(1, V7X_VMEM_LIMIT),
    )(h2, p, target, g_ple, w_gate, w_proj)


def _inproj_bwd(dqs, dk_t, dv_t, parts_b, qa_raw, ka_raw, x, dh1, g_pre, w_in, cos, sin, gq, gk, tm):
    s_len = x.shape[0]
    tk = dk_t.shape[3]

    n_d = len(DILATIONS)
    n_parts = 3 * (1 + n_d)

    def body(dqs_ref, dkt_ref, dvt_ref, *rest):
        parts, rest = rest[:n_parts], rest[n_parts:]
        (qa_ref, ka_ref, x_ref, dh1_ref, g_ref, w_ref, cos_ref, sin_ref, gq_ref, gk_ref), rest = rest[:10], rest[10:]
        unsort, (dx_ref, dproj_ref, dg_ref, dgq_ref, dgk_ref) = rest[:n_d], rest[n_d:]

        def mixer_b_grad(a):
            own = parts[a * (1 + n_d):(a + 1) * (1 + n_d)]
            total = own[0][...].astype(F32)
            for j, d in enumerate(DILATIONS):
                sorted_rows = jnp.concatenate([own[1 + j][:, r * D_B:(r + 1) * D_B] for r in range(d)], axis=0)
                total = total + _dot(unsort[j][...], sorted_rows)
            return total.astype(BF16)

        _zero_at_first(pl.program_id(0) == 0, dg_ref, dgq_ref, dgk_ref)
        cos, sin = cos_ref[...], sin_ref[...]
        dkn = dkt_ref[...].reshape(D_KV_A, tm).T
        dva = dvt_ref[...].reshape(D_KV_A, tm).T
        dqa, dgq = _head_norm_rope_bwd(qa_ref[...], gq_ref[...], jnp.tile(cos, (1, D_A // D_KV_A)),
                                       jnp.tile(sin, (1, D_A // D_KV_A)), dqs_ref[...])
        dka, dgk = _head_norm_rope_bwd(ka_ref[...], gk_ref[...], cos, sin, dkn)
        dgq_ref[...] += dgq
        dgk_ref[...] += dgk
        dproj_ref[:, OFF_QA:OFF_KA] = dqa.astype(BF16)
        dproj_ref[:, OFF_KA:OFF_VA] = dka.astype(BF16)
        dproj_ref[:, OFF_VA:OFF_QB] = dva.astype(BF16)
        dproj_ref[:, OFF_QB:OFF_KB] = mixer_b_grad(0)
        dproj_ref[:, OFF_KB:OFF_VB] = mixer_b_grad(1)
        dproj_ref[:, OFF_VB:D_IN] = mixer_b_grad(2)
        dxn = _dot_nt(dproj_ref[...], w_ref[...])
        dx, dg = _rms_bwd(x_ref[...], g_ref[...], dxn)
        dg_ref[...] += dg
        dx_ref[...] = dh1_ref[...] + dx

    half, kvw, full = _row_spec(tm, D_A), _row_spec(tm, D_KV_A), _row_spec(tm, D_MODEL)
    per_chunk = tk // tm
    chunk_t = pl.BlockSpec((N_KV_A, None, HEAD_DIM, tm), lambda i: (0, i // per_chunk, 0, i % per_chunk))
    view_specs, _ = _dilated_specs(s_len, tm, BF16)
    return pl.pallas_call(
        body, name="inproj_bwd", grid=(s_len // tm,),
        in_specs=[half, chunk_t, chunk_t] + ([half] + view_specs) * 3
                 + [half, kvw, full, full, _full_spec((1, D_MODEL)), _const_spec((D_MODEL, D_IN)), kvw, kvw,
                    _full_spec((1, D_A)), _full_spec((1, D_KV_A))] + [_full_spec((tm, tm))] * n_d,
        out_specs=[full, _row_spec(tm, D_IN), _full_spec((1, D_MODEL)), _full_spec((1, D_A)), _full_spec((1, D_KV_A))],
        out_shape=(SDS((s_len, D_MODEL), F32), SDS((s_len, D_IN), BF16), SDS((1, D_MODEL), F32), SDS((1, D_A), F32),
                   SDS((1, D_KV_A), F32)),
        compiler_params=_params(1, V7X_VMEM_LIMIT),
    )(dqs, dk_t, dv_t, *parts_b, qa_raw, ka_raw, x, dh1, g_pre, w_in, cos, sin, gq, gk,
      *[_dilation_perm(tm, d).T for d in DILATIONS])


def _weight_grad(a, b, name, tk1, tn, tm):
    s_len, k1 = a.shape
    n = b.shape[1]
    steps = s_len // tm

    def body(a_ref, b_ref, o_ref, acc):
        r = pl.program_id(2)
        _zero_at_first(r == 0, acc)
        acc[...] += _dot_tn(a_ref[...], b_ref[...])

        @pl.when(r == steps - 1)
        def _():
            o_ref[...] = acc[...].astype(BF16)

    return pl.pallas_call(
        body, name=name, grid=(k1 // tk1, n // tn, steps),
        in_specs=[pl.BlockSpec((tm, tk1), lambda i, j, r: (r, i)), pl.BlockSpec((tm, tn), lambda i, j, r: (r, j))],
        out_specs=pl.BlockSpec((tk1, tn), lambda i, j, r: (i, j)),
        out_shape=SDS((k1, n), BF16),
        scratch_shapes=[pltpu.VMEM((tk1, tn), F32)],
        compiler_params=_params(3, V7X_VMEM_LIMIT),
    )(a, b)


def _my_index():
    return 4 * lax.axis_index("x") + 2 * lax.axis_index("y") + lax.axis_index("c")


def _peer(k):
    f = k + 1
    x, y, c = lax.axis_index("x"), lax.axis_index("y"), lax.axis_index("c")
    return (x ^ ((f >> 2) & 1), y ^ ((f >> 1) & 1), c ^ (f & 1))


GATHER, EXCHANGE = "gather", "exchange"
ANY_SPEC = pl.BlockSpec(memory_space=pl.ANY)


def _comm_out_shapes(kind, arrays):
    return [SDS((N_DEV,) + a.shape if kind == GATHER else a.shape, a.dtype) for a in arrays]


def _comm_sems(n):
    return [pltpu.SemaphoreType.DMA((n, N_DEV - 1)), pltpu.SemaphoreType.DMA((n, N_DEV - 1)), pltpu.SemaphoreType.DMA((n,))]


def _comm_copies(kind, ins, outs, send_sems, recv_sems, local_sems):
    me = _my_index()
    local, remote = [], []
    for a, (src, dst) in enumerate(zip(ins, outs)):
        local.append(pltpu.make_async_copy(src if kind == GATHER else src.at[me], dst.at[me], local_sems.at[a]))
        for k in range(N_DEV - 1):
            px, py, pc = _peer(k)
            remote.append(pltpu.make_async_remote_copy(
                src_ref=src if kind == GATHER else src.at[4 * px + 2 * py + pc], dst_ref=dst.at[me],
                send_sem=send_sems.at[a, k], recv_sem=recv_sems.at[a, k],
                device_id=(px, py, pc), device_id_type=pl.DeviceIdType.MESH))
    return local, remote


def _comm_start(kind, ins, outs, sems):
    local, remote = _comm_copies(kind, ins, outs, *sems)
    for cp in local + remote:
        cp.start()


def _comm_wait(kind, ins, outs, sems):
    local, remote = _comm_copies(kind, ins, outs, *sems)
    for cp in local:
        cp.wait()
    for cp in remote:
        cp.wait_send()
    for cp in remote:
        cp.wait_recv()


def _collective(kind, arrays, name):
    n = len(arrays)

    def body(*refs):
        ins, outs, sems = refs[:n], refs[n:2 * n], refs[2 * n:]
        _comm_start(kind, ins, outs, sems)
        _comm_wait(kind, ins, outs, sems)

    return pl.pallas_call(
        body, name=name, in_specs=[ANY_SPEC] * n, out_specs=[ANY_SPEC] * n,
        out_shape=_comm_out_shapes(kind, arrays), scratch_shapes=_comm_sems(n),
    )(*arrays)


def _sum_adamw(parts, w, m, v, name, tr):
    rows, cols = w.shape
    c1 = 1.0 / (1.0 - ADAM_B1 ** ADAM_STEP)
    c2 = 1.0 / (1.0 - ADAM_B2 ** ADAM_STEP)

    def body(p_ref, w_ref, m_ref, v_ref, g_ref, d_ref, nm_ref, nv_ref):
        g = p_ref[0].astype(F32)
        for j in range(1, N_DEV):
            g = g + p_ref[j].astype(F32)
        g_ref[...] = g
        nm = ADAM_B1 * m_ref[...] + (1.0 - ADAM_B1) * g
        nv = ADAM_B2 * v_ref[...] + (1.0 - ADAM_B2) * (g * g)
        nm_ref[...] = nm
        nv_ref[...] = nv
        d_ref[...] = -ADAM_LR * ((nm * c1) / (jnp.sqrt(nv * c2) + ADAM_EPS) + ADAM_WD * w_ref[...])

    blk = pl.BlockSpec((tr, cols), lambda i: (i, 0))
    return pl.pallas_call(
        body, name=name, grid=(rows // tr,),
        in_specs=[pl.BlockSpec((N_DEV, tr, cols), lambda i: (0, i, 0)), blk, blk, blk],
        out_specs=[blk] * 4, out_shape=[SDS((rows, cols), F32)] * 4,
        compiler_params=_params(1, V7X_VMEM_LIMIT),
    )(parts, w, m, v)


_SMALL = (("g_attn_pre", 1024), ("g_q", 64), ("g_k", 64), ("g_out_a", 512), ("g_out_b", 512), ("g_attn_post", 1024),
          ("rel_bias", 256), ("g_mlp_pre", 1024), ("g_mlp_post", 1024), ("g_ple", 1024))
_SLAB_ROWS = 56


def _pack_small(vals):
    rows = []
    for (name, size) in _SMALL:
        flat = vals[name].reshape(-1).astype(F32)
        padded = -(-size // 128) * 128
        rows.append(jnp.pad(flat, (0, padded - size)).reshape(padded // 128, 128))
    slab = jnp.concatenate(rows, axis=0)
    return jnp.pad(slab, ((0, _SLAB_ROWS - slab.shape[0]), (0, 0)))


def _unpack_small(slab, shapes):
    out, row = {}, 0
    for (name, size) in _SMALL:
        nrow = -(-size // 128)
        out[name] = slab[row:row + nrow].reshape(-1)[:size].reshape(shapes[name])
        row += nrow
    return out


def _rope_tables(s_len):
    rows = s_len // GRID_W
    row = jnp.broadcast_to(jnp.arange(rows)[:, None], (rows, GRID_W)).reshape(-1).astype(F32)
    col = jnp.broadcast_to(jnp.arange(GRID_W)[None, :], (rows, GRID_W)).reshape(-1).astype(F32)
    n_axis = ROPE_HALF // 2
    inv_freq = ROPE_THETA ** (-jnp.arange(n_axis, dtype=F32) / n_axis)
    ang = jnp.concatenate([row[:, None] * inv_freq, col[:, None] * inv_freq], axis=-1)
    cos, sin = jnp.cos(ang), jnp.sin(ang)
    cos = jnp.tile(jnp.concatenate([cos, cos], axis=-1), (1, N_KV_A))
    sin = jnp.tile(jnp.concatenate([-sin, sin], axis=-1), (1, N_KV_A))
    return cos, sin


_RIDERS = {1: ("w_out", "w_ff1"), 4: ("w_ff2",), 16: ("w_ple_gate", "w_ple_proj")}
_GRAD_RIDERS = {1: ("w_ff1",), 4: ("w_ff2",), 16: ("w_ple_gate", "w_ple_proj", "w_out")}


ROW_TILE = 256
ATTN_Q_TILE = 128
ATTN_K_CHUNK = 2048
BAND_TILE = 256
GRAD_ROW_TILE = 1024


def _local_step(x, p, target, w_in, shards, small):
    axis_of = dict(_BIG)
    s_len = x.shape[0]
    tm = min(ROW_TILE, s_len)
    tq = min(ATTN_Q_TILE, s_len)
    tk = min(ATTN_K_CHUNK, s_len // 2)
    cos, sin = _rope_tables(s_len)
    gq = jnp.tile(small["g_q"], (1, N_HEADS_A))
    gk = jnp.tile(small["g_k"], (1, N_KV_A))

    qa_raw, ka_raw, qs, k_maj, kt, vt, qb, kb, vb, xn1, *views = _inproj_fwd(
        x, small["g_attn_pre"], w_in, cos, sin, gq, gk, tm, tk)
    n_d = len(DILATIONS)
    qb, kb, vb = ({1: plain, **dict(zip(DILATIONS, views[a * n_d:(a + 1) * n_d]))} for a, plain in enumerate((qb, kb, vb)))

    ya, lse_a = _attn_fwd(qs, k_maj, vt, tq, tk)

    ob, lb, tiles, full = [], [], [], {}
    for (_, dil) in DILATED_PATTERNS:
        t = min(BAND_TILE, s_len // dil)
        bias, bias_t = _bias_tiles(small["rel_bias"], dil, t)
        tiles.append((t, bias, bias_t, _band_buckets(dil, t)))
        o, l, gathered = _band_fwd(qb[dil], kb[dil], vb[dil], bias, dil, t, (GATHER, [shards[n] for n in _RIDERS[dil]]))
        ob.append(o)
        lb.append(l)
        full.update({n: _assemble(g, axis_of[n]) for n, g in zip(_RIDERS[dil], gathered)})
    w_out, w_ff1, w_ff2, w_gate, w_proj = (full[n] for n in ("w_out", "w_ff1", "w_ff2", "w_ple_gate", "w_ple_proj"))

    h1, yo, yb, lse_b, ycat = _attn_out_fwd(ya, ob, lb, x, small["g_out_a"], small["g_out_b"], w_out,
                                            small["g_attn_post"], tm)
    h2, fo, xn2 = _mlp_fwd(h1, small["g_mlp_pre"], w_ff1, w_ff2, small["g_mlp_post"], tm)
    dh2, loss_part, dg_ple, xn3, dgl, dpp, pb = _ple_fwd_bwd(h2, p, target, small["g_ple"], w_gate, w_proj, tm)
    dh1, dfo, du, f, dg_mlp_post, dg_mlp_pre = _mlp_bwd(h1, dh2, fo, xn2, small["g_mlp_pre"], w_ff1, w_ff2,
                                                          small["g_mlp_post"], tm)
    dyo, dya, dyb, delta_a, delta_b, dg_attn_post, dg_out_a, dg_out_b, *dyb_views = _attn_out_bwd(
        dh1, yo, ya, yb, small["g_out_a"], small["g_out_b"], w_out, small["g_attn_post"], tm)
    dyb = {1: dyb, **dict(zip(DILATIONS, dyb_views))}

    dqs, dk_t, dv_t = _attn_bwd(qs, dya, lse_a, delta_a, kt, k_maj, vt, tq, tk)

    tg = min(GRAD_ROW_TILE, s_len)
    grads = {
        "w_out": _weight_grad(ycat, dyo, "grad_w_out", D_MODEL, D_MODEL, tg),
        "w_ff1": _weight_grad(xn2, du, "grad_w_ff1", D_MODEL, 1024, tg),
        "w_ff2": _weight_grad(f, dfo, "grad_w_ff2", 1024, D_MODEL, tg),
        "w_ple_gate": _weight_grad(xn3, dgl, "grad_w_ple_gate", D_MODEL, D_MODEL, tg),
        "w_ple_proj": _weight_grad(pb, dpp, "grad_w_ple_proj", D_PLE, D_MODEL, tg),
    }

    dqb, dkb, dvb, dbias, received = {}, {}, {}, [], {}
    for (_, dil), (t, bias, bias_t, _) in zip(DILATED_PATTERNS, tiles):
        rider = (EXCHANGE, [_cut(grads[n], axis_of[n]) for n in _GRAD_RIDERS[dil]])
        dqb[dil], db, got = _band_dq(qb[dil], dyb[dil], lse_b, delta_b, kb[dil], vb[dil], bias, dil, t, rider)
        dkb[dil], dvb[dil] = _band_dkv(kb[dil], vb[dil], qb[dil], dyb[dil], lse_b, delta_b, bias_t, dil, t)
        dbias.append(db)
        received.update(zip(_GRAD_RIDERS[dil], got))
    d_rel = _dbias_reduce(dbias, [jnp.asarray(tl[3]) for tl in tiles])[:, :N_HEADS_B]

    parts_b = [part[d] for part in (dqb, dkb, dvb) for d in (1,) + DILATIONS]
    dx, dproj, dg_attn_pre, dgq_lanes, dgk_lanes = _inproj_bwd(
        dqs, dk_t, dv_t, parts_b, qa_raw, ka_raw, x, dh1, small["g_attn_pre"], w_in, cos, sin, gq, gk, tm)
    grad_w_in = _weight_grad(xn1, dproj, "grad_w_in", D_MODEL, 768, tg)
    small_grads = {
        "g_attn_pre": dg_attn_pre, "g_q": dgq_lanes.reshape(N_HEADS_A, HEAD_DIM).sum(0, keepdims=True),
        "g_k": dgk_lanes.reshape(N_KV_A, HEAD_DIM).sum(0, keepdims=True), "g_out_a": dg_out_a, "g_out_b": dg_out_b,
        "g_attn_post": dg_attn_post, "rel_bias": d_rel, "g_mlp_pre": dg_mlp_pre, "g_mlp_post": dg_mlp_post,
        "g_ple": dg_ple,
    }
    return loss_part[0, 0], dx, grad_w_in, received, small_grads


_BIG = (("w_in", 1), ("w_out", 0), ("w_ff1", 1), ("w_ff2", 0), ("w_ple_gate", 0), ("w_ple_proj", 1))


def _assemble(gathered, axis):
    if axis == 0:
        return gathered.reshape(-1, gathered.shape[2])
    return gathered.transpose(1, 0, 2).reshape(gathered.shape[1], -1)


def _cut(full, axis):
    if axis == 0:
        return full.reshape(N_DEV, full.shape[0] // N_DEV, full.shape[1])
    return full.reshape(full.shape[0], N_DEV, full.shape[1] // N_DEV).transpose(1, 0, 2)


def kernel(x, p, w_in, g_attn_pre, g_q, g_k, g_out_a, g_out_b, w_out, g_attn_post, rel_bias, g_mlp_pre, w_ff1, w_ff2, g_mlp_post, g_ple, w_ple_gate, w_ple_proj, loss_target, m_w_in, m_g_attn_pre, m_g_q, m_g_k, m_g_out_a, m_g_out_b, m_w_out, m_g_attn_post, m_rel_bias, m_g_mlp_pre, m_w_ff1, m_w_ff2, m_g_mlp_post, m_g_ple, m_w_ple_gate, m_w_ple_proj, v_w_in, v_g_attn_pre, v_g_q, v_g_k, v_g_out_a, v_g_out_b, v_w_out, v_g_attn_post, v_rel_bias, v_g_mlp_pre, v_w_ff1, v_w_ff2, v_g_mlp_post, v_g_ple, v_w_ple_gate, v_w_ple_proj):
    given = dict(locals())
    small_names = [n for n, _ in _SMALL]
    small = {n: given[n] for n in small_names}
    shards = {n: given[n][0] for n, _ in _BIG}

    bf16_shards = {n: shards[n].astype(BF16) for n, _ in _BIG}
    (gathered_w_in,) = _collective(GATHER, [bf16_shards["w_in"]], "gather_w_in")

    loss_part, dx, grad_w_in, received, small_grads = _local_step(
        x[0], p[0, 0], loss_target[0], _assemble(gathered_w_in, 1), bf16_shards, small)

    loss_at = np.zeros((_SLAB_ROWS, 128), bool)
    loss_at[-1, 0] = True
    slab = jnp.where(loss_at, loss_part * (0.5 / D_MODEL), _pack_small(small_grads))
    slab_parts = jnp.broadcast_to(slab[None], (N_DEV,) + slab.shape)
    received["w_in"], slabs = _collective(EXCHANGE, [_cut(grad_w_in, 1), slab_parts], "exchange_w_in_grads")

    out_g, out_d, out_m, out_v = {}, {}, {}, {}
    for n, _ in _BIG:
        rows = shards[n].shape[0]
        g, d, nm, nv = _sum_adamw(received[n], shards[n], given["m_" + n][0], given["v_" + n][0], "adamw_" + n,
                                  min(rows, 128))
        out_g[n], out_d[n], out_m[n], out_v[n] = g[None], d[None], nm[None], nv[None]
    g, d, nm, nv = _sum_adamw(slabs, _pack_small(small), _pack_small({n: given["m_" + n] for n in small_names}),
                              _pack_small({n: given["v_" + n] for n in small_names}), "adamw_small", _SLAB_ROWS)
    shapes = {n: given[n].shape for n in small_names}
    for dst, slab_out in ((out_g, g), (out_d, d), (out_m, nm), (out_v, nv)):
        dst.update(_unpack_small(slab_out, shapes))

    loss = g[-1, 0]
    order = ["w_in", "g_attn_pre", "g_q", "g_k", "g_out_a", "g_out_b", "w_out", "g_attn_post", "rel_bias", "g_mlp_pre",
             "w_ff1", "w_ff2", "g_mlp_post", "g_ple", "w_ple_gate", "w_ple_proj"]
    return (loss, dx[None], *[out_g[n] for n in order], *[out_d[n] for n in order], *[out_m[n] for n in order],
            *[out_v[n] for n in order])
```

```python
import functools
import math

import jax
import jax.numpy as jnp
import numpy as np
from jax import lax
from jax.experimental import pallas as pl
from jax.experimental.pallas import tpu as pltpu

F32 = jnp.float32
BF16 = jnp.bfloat16
SDS = jax.ShapeDtypeStruct

D_MODEL = 1024
HEAD_DIM = 64
N_HEADS_A = 8
N_KV_A = 2
GROUP_A = N_HEADS_A // N_KV_A
N_HEADS_B = 8
D_A = N_HEADS_A * HEAD_DIM
D_KV_A = N_KV_A * HEAD_DIM
D_B = N_HEADS_B * HEAD_DIM
D_IN = D_A + 2 * D_KV_A + 3 * D_B
D_FF = 4 * D_MODEL
D_PLE = 256
GRID_W = 64
ROPE_THETA = 10000.0
ROPE_HALF = HEAD_DIM // 2
DILATED_PATTERNS = ((128, 1), (512, 4), (2048, 16))
BAND_HALF = 64
N_BUCKETS = 32
MAX_DISTANCE = 1024
EPS = 1e-6
NEG_BIG = -1e30
SCORE_SCALE = HEAD_DIM ** -0.5

ADAM_LR = 0.001
ADAM_B1 = 0.9
ADAM_B2 = 0.999
ADAM_EPS = 1e-08
ADAM_WD = 0.01
ADAM_STEP = 10

N_DEV = 8
V7X_VMEM_LIMIT = 56 * 1024 * 1024

OFF_QA, OFF_KA, OFF_VA, OFF_QB, OFF_KB, OFF_VB = 0, 512, 640, 768, 1280, 1792


def _params(n_axes, vmem=None):
    return pltpu.CompilerParams(dimension_semantics=("arbitrary",) * n_axes, vmem_limit_bytes=vmem)


def _dot(a, b):
    return jnp.dot(a, b, preferred_element_type=F32)


def _dot_nt(a, b):
    return lax.dot_general(a, b, (((1,), (1,)), ((), ())), preferred_element_type=F32)


def _dot_tn(a, b):
    return lax.dot_general(a, b, (((0,), (0,)), ((), ())), preferred_element_type=F32)


def _rms_fwd(x, g):
    r = lax.rsqrt(jnp.mean(x * x, axis=-1, keepdims=True) + EPS)
    return x * r * g, r


def _rms_bwd(x, g, dy):
    r = lax.rsqrt(jnp.mean(x * x, axis=-1, keepdims=True) + EPS)
    xh = x * r
    dxh = dy * g
    dx = r * (dxh - xh * jnp.mean(dxh * xh, axis=-1, keepdims=True))
    return dx, jnp.sum(dy * xh, axis=0, keepdims=True)


def _head_sum(v):
    head = lax.broadcasted_iota(jnp.int32, v.shape, 1) >> 6
    out = jnp.zeros_like(v)
    for h in range(v.shape[1] // HEAD_DIM):
        msk = head == h
        s = jnp.sum(jnp.where(msk, v, 0.0), axis=-1, keepdims=True)
        out = jnp.where(msk, s, out)
    return out


def _swap_halves(v):
    w = v.shape[1]
    lane = lax.broadcasted_iota(jnp.int32, v.shape, 1)
    first_half = (lane & (HEAD_DIM - 1)) < ROPE_HALF
    return jnp.where(first_half, pltpu.roll(v, w - ROPE_HALF, 1), pltpu.roll(v, ROPE_HALF, 1))


def _head_norm_rope(v, g, cos, sin_signed):
    r = lax.rsqrt(_head_sum(v * v) * (1.0 / HEAD_DIM) + EPS)
    y = v * r * g
    return y * cos + _swap_halves(y) * sin_signed


def _head_norm_rope_bwd(v, g, cos, sin_signed, dout):
    dy = dout * cos - _swap_halves(dout) * sin_signed
    r = lax.rsqrt(_head_sum(v * v) * (1.0 / HEAD_DIM) + EPS)
    xh = v * r
    dxh = dy * g
    dv = r * (dxh - xh * (_head_sum(dxh * xh) * (1.0 / HEAD_DIM)))
    return dv, jnp.sum(dy * xh, axis=0, keepdims=True)


def _row_spec(tm, n):
    return pl.BlockSpec((tm, n), lambda i: (i, 0))


def _full_spec(shape):
    nd = len(shape)
    return pl.BlockSpec(shape, lambda *_: (0,) * nd)


def _const_spec(shape):
    nd = len(shape)
    return pl.BlockSpec(shape, lambda *_: (0,) * nd, pipeline_mode=pl.Buffered(1))


def _zero_at_first(first, *refs):
    @pl.when(first)
    def _():
        for ref in refs:
            ref[...] = jnp.zeros_like(ref)


DILATIONS = tuple(d for _, d in DILATED_PATTERNS if d > 1)


def _dilation_perm(tm, dil):
    rows = np.arange(tm)
    perm = np.zeros((tm, tm), np.float32)
    perm[(rows % dil) * (tm // dil) + rows // dil, rows] = 1.0
    return jnp.asarray(perm, BF16)


def _store_dilated(ref, perm_ref, val, dil):
    per = val.shape[0] // dil
    sorted_rows = _dot(perm_ref[...], val).astype(BF16)
    for r in range(dil):
        ref[:, r * D_B:(r + 1) * D_B] = sorted_rows[r * per:(r + 1) * per, :]


def _dilated_specs(s_len, tm, dtype):
    specs = [pl.BlockSpec((tm // d, d * D_B), lambda i: (i, 0)) for d in DILATIONS]
    return specs, [SDS((s_len // d, d * D_B), dtype) for d in DILATIONS]


ONES_ROWS = 16


def _inproj_fwd(x, g_pre, w_in, cos, sin, gq, gk, tm, tk):
    s_len = x.shape[0]
    n_d = len(DILATIONS)

    def body(x_ref, g_ref, w_ref, cos_ref, sin_ref, gq_ref, gk_ref, *rest):
        perms, views = rest[:n_d], rest[n_d + 10:]
        qa_raw, ka_raw, qs, k_maj, kt, vt, qb, kb, vb, xn_out = rest[n_d:n_d + 10]
        xn, _ = _rms_fwd(x_ref[...], g_ref[...])
        xn = xn.astype(BF16)
        xn_out[...] = xn
        qa = _dot(xn, w_ref[:, OFF_QA:OFF_KA])
        qa_raw[...] = qa
        cos, sin = cos_ref[...], sin_ref[...]
        qs[...] = (_head_norm_rope(qa, gq_ref[...], jnp.tile(cos, (1, D_A // D_KV_A)), jnp.tile(sin, (1, D_A // D_KV_A)))
                   * SCORE_SCALE).astype(BF16)
        ka = _dot(xn, w_ref[:, OFF_KA:OFF_VA])
        ka_raw[...] = ka
        kn = _head_norm_rope(ka, gk_ref[...], cos, sin).astype(BF16)
        for kv in range(N_KV_A):
            k_maj[kv] = kn[:, HEAD_DIM * kv:HEAD_DIM * (kv + 1)]
        kt[...] = kn.T.reshape(N_KV_A, HEAD_DIM, tm)
        va = _dot(xn, w_ref[:, OFF_VA:OFF_QB]).astype(BF16)
        vt[:, :HEAD_DIM, :] = va.T.reshape(N_KV_A, HEAD_DIM, tm)
        vt[:, HEAD_DIM:, :] = jnp.ones((N_KV_A, ONES_ROWS, tm), BF16)
        mixer_b = ((_dot(xn, w_ref[:, OFF_QB:OFF_KB]) * SCORE_SCALE).astype(BF16),
                   _dot(xn, w_ref[:, OFF_KB:OFF_VB]).astype(BF16), _dot(xn, w_ref[:, OFF_VB:D_IN]).astype(BF16))
        for a, (val, plain) in enumerate(zip(mixer_b, (qb, kb, vb))):
            plain[...] = val
            for j, d in enumerate(DILATIONS):
                _store_dilated(views[a * n_d + j], perms[j], val, d)

    view_specs, view_shapes = _dilated_specs(s_len, tm, BF16)
    per_chunk = tk // tm

    def chunk_t(rows):
        return pl.BlockSpec((N_KV_A, None, rows, tm), lambda i: (0, i // per_chunk, 0, i % per_chunk))

    return pl.pallas_call(
        body, name="inproj_fwd", grid=(s_len // tm,),
        in_specs=[_row_spec(tm, D_MODEL), _full_spec((1, D_MODEL)), _const_spec((D_MODEL, D_IN)),
                  _row_spec(tm, D_KV_A), _row_spec(tm, D_KV_A), _full_spec((1, D_A)), _full_spec((1, D_KV_A))]
                 + [_full_spec((tm, tm))] * n_d,
        out_specs=[_row_spec(tm, D_A), _row_spec(tm, D_KV_A), _row_spec(tm, D_A),
                   pl.BlockSpec((N_KV_A, tm, HEAD_DIM), lambda i: (0, i, 0)), chunk_t(HEAD_DIM),
                   chunk_t(HEAD_DIM + ONES_ROWS), _row_spec(tm, D_B), _row_spec(tm, D_B), _row_spec(tm, D_B),
                   _row_spec(tm, D_MODEL)] + view_specs * 3,
        out_shape=[SDS((s_len, D_A), F32), SDS((s_len, D_KV_A), F32), SDS((s_len, D_A), BF16),
                   SDS((N_KV_A, s_len, HEAD_DIM), BF16), SDS((N_KV_A, s_len // tk, HEAD_DIM, tk), BF16),
                   SDS((N_KV_A, s_len // tk, HEAD_DIM + ONES_ROWS, tk), BF16), SDS((s_len, D_B), BF16),
                   SDS((s_len, D_B), BF16), SDS((s_len, D_B), BF16), SDS((s_len, D_MODEL), BF16)] + view_shapes * 3,
        compiler_params=_params(1, V7X_VMEM_LIMIT),
    )(x, g_pre, w_in, cos, sin, gq, gk, *[_dilation_perm(tm, d) for d in DILATIONS])


def _stack_heads(ref, tq):
    return jnp.concatenate([ref[:, HEAD_DIM * g:HEAD_DIM * (g + 1)] for g in range(GROUP_A)], axis=0)


def _stack_cols(ref, tq):
    return jnp.concatenate([ref[:, HEAD_DIM * g:HEAD_DIM * g + 1] for g in range(GROUP_A)], axis=0)


def _attn_fwd(qs, k, vt, tq, tk):
    s_len = qs.shape[0]
    nk = s_len // tk
    assert nk % 2 == 0
    gw = GROUP_A * HEAD_DIM
    rows = GROUP_A * tq
    vrows = vt.shape[2]

    def body(q_ref, k_ref, vt_ref, o_ref, lse_ref, s_buf):
        qt = _stack_heads(q_ref, tq).T

        def scores(j, slot):
            kj = k_ref[pl.ds(pl.multiple_of(j * tk, tk), tk), :]
            s_buf[slot] = _dot(kj, qt)

        def consume(j, slot, carry):
            m, acc = carry
            st = s_buf[slot]
            m_new = jnp.maximum(m, jnp.max(st, axis=0, keepdims=True))
            pt = jnp.exp(st - m_new)
            acc = jnp.exp(m - m_new) * acc + _dot(vt_ref[j], pt.astype(BF16))
            return m_new, acc

        scores(0, 0)

        def pair(j, carry, more):
            scores(j + 1, 1)
            carry = consume(j, 0, carry)
            if more:
                scores(j + 2, 0)
            return consume(j + 1, 1, carry)

        carry = (jnp.full((1, rows), NEG_BIG, F32), jnp.zeros((vrows, rows), F32))
        carry = lax.fori_loop(0, nk // 2 - 1, lambda jj, c: pair(2 * jj, c, True), carry)
        m, acc = pair(nk - 2, carry, False)
        l = acc[HEAD_DIM:HEAD_DIM + 1]
        o = (acc[:HEAD_DIM] / l).T
        lse = jnp.broadcast_to(m + jnp.log(l), (HEAD_DIM, rows)).T
        for g in range(GROUP_A):
            o_ref[:, HEAD_DIM * g:HEAD_DIM * (g + 1)] = o[g * tq:(g + 1) * tq]
            lse_ref[:, HEAD_DIM * g:HEAD_DIM * (g + 1)] = lse[g * tq:(g + 1) * tq]

    tile = pl.BlockSpec((tq, gw), lambda kv, i: (i, kv))
    return pl.pallas_call(
        body, name="attn_fwd", grid=(N_KV_A, s_len // tq),
        in_specs=[tile, pl.BlockSpec((None, s_len, HEAD_DIM), lambda kv, i: (kv, 0, 0)),
                  pl.BlockSpec((None, nk, vrows, tk), lambda kv, i: (kv, 0, 0, 0))],
        out_specs=[tile, tile],
        out_shape=(SDS((s_len, D_A), F32), SDS((s_len, D_A), F32)),
        scratch_shapes=[pltpu.VMEM((2, tk, rows), F32)],
        compiler_params=_params(2, V7X_VMEM_LIMIT),
    )(qs, k, vt)


def _attn_bwd(qs, do, lse, delta, kt, k, vt, tq, tk):
    s_len = qs.shape[0]
    nk = s_len // tk
    assert nk % 2 == 0
    nq = s_len // tq
    gw = GROUP_A * HEAD_DIM
    rows = GROUP_A * tq

    def body(q_ref, do_ref, lse_ref, delta_ref, kt_ref, k_ref, vt_ref, dq_ref, dk_hbm, dv_hbm,
             dk_acc, dv_acc, s_buf, dp_buf):
        kv = pl.program_id(0)
        i = pl.program_id(1)

        @pl.when(i == 0)
        def _():
            dk_acc[...] = jnp.zeros_like(dk_acc)
            dv_acc[...] = jnp.zeros_like(dv_acc)

        q = _stack_heads(q_ref, tq)
        dout = _stack_heads(do_ref, tq)
        qt = q.T
        doutt = dout.T
        row_lse = _stack_cols(lse_ref, tq)
        row_delta = _stack_cols(delta_ref, tq)

        def scores(j, slot):
            s_buf[slot] = _dot(q, kt_ref[j])
            dp_buf[slot] = _dot(dout, vt_ref[j, :HEAD_DIM, :])

        def consume(j, slot, dq):
            p = jnp.exp(s_buf[slot] - row_lse)
            ds = (p * (dp_buf[slot] - row_delta)).astype(BF16)
            dv_acc[j] += _dot(doutt, p.astype(BF16))
            dk_acc[j] += _dot(qt, ds)
            return dq + _dot(ds, k_ref[pl.ds(pl.multiple_of(j * tk, tk), tk), :])

        scores(0, 0)

        def pair(j, dq, more):
            scores(j + 1, 1)
            dq = consume(j, 0, dq)
            if more:
                scores(j + 2, 0)
            return consume(j + 1, 1, dq)

        dq = lax.fori_loop(0, nk // 2 - 1, lambda jj, c: pair(2 * jj, c, True), jnp.zeros((rows, HEAD_DIM), F32))
        dq = pair(nk - 2, dq, False) * SCORE_SCALE
        for g in range(GROUP_A):
            dq_ref[:, HEAD_DIM * g:HEAD_DIM * (g + 1)] = dq[g * tq:(g + 1) * tq]

        @pl.when(i == nq - 1)
        def _():
            pltpu.sync_copy(dk_acc, dk_hbm.at[kv])
            pltpu.sync_copy(dv_acc, dv_hbm.at[kv])

    tile = pl.BlockSpec((tq, gw), lambda kv, i: (i, kv))
    chunks = pl.BlockSpec((None, nk, HEAD_DIM, tk), lambda kv, i: (kv, 0, 0, 0))
    grad_t = SDS((N_KV_A, nk, HEAD_DIM, tk), F32)
    return pl.pallas_call(
        body, name="attn_bwd", grid=(N_KV_A, nq),
        in_specs=[tile, tile, tile, tile, chunks,
                  pl.BlockSpec((None, s_len, HEAD_DIM), lambda kv, i: (kv, 0, 0)),
                  pl.BlockSpec((None, nk, vt.shape[2], tk), lambda kv, i: (kv, 0, 0, 0))],
        out_specs=[tile, pl.BlockSpec(memory_space=pl.ANY), pl.BlockSpec(memory_space=pl.ANY)],
        out_shape=(SDS((s_len, D_A), F32), grad_t, grad_t),
        scratch_shapes=[pltpu.VMEM((nk, HEAD_DIM, tk), F32), pltpu.VMEM((nk, HEAD_DIM, tk), F32),
                        pltpu.VMEM((2, rows, tk), F32), pltpu.VMEM((2, rows, tk), F32)],
        compiler_params=_params(2, V7X_VMEM_LIMIT),
    )(qs, do, lse, delta, kt, k, vt)


STAT_W = 128


def _band_specs(length, t, width=D_B):
    hb = t // BAND_HALF
    last = length // BAND_HALF - 1
    main = pl.BlockSpec((t, width), lambda r, i: (i, r))
    prev = pl.BlockSpec((BAND_HALF, width), lambda r, i: (jnp.maximum(i * hb - 1, 0), r))
    nxt = pl.BlockSpec((BAND_HALF, width), lambda r, i: (jnp.minimum((i + 1) * hb, last), r))
    return main, [prev, main, nxt]


def _pack_heads(cols):
    lane = lax.broadcasted_iota(jnp.int32, (cols[0].shape[0], STAT_W), 1)
    out = jnp.zeros((cols[0].shape[0], STAT_W), F32)
    for h, c in enumerate(cols):
        out = jnp.where(lane == h, c, out)
    return out


def _spread_heads(stat):
    head = lax.broadcasted_iota(jnp.int32, (stat.shape[0], D_B), 1) >> 6
    out = jnp.zeros((stat.shape[0], D_B), F32)
    for h in range(N_HEADS_B):
        out = jnp.where(head == h, stat[:, h:h + 1], out)
    return out


def _window(refs):
    return jnp.concatenate([r[...] for r in refs], axis=0)


def _head(v, h):
    return v[:, HEAD_DIM * h:HEAD_DIM * (h + 1)]


def _call_with_rider(body, name, grid, in_specs, out_specs, out_shape, scratch_shapes, args, rider):
    n_in, n_out = len(in_specs), len(out_specs)
    if rider is None:
        res = pl.pallas_call(body, name=name, grid=grid, in_specs=in_specs, out_specs=out_specs, out_shape=out_shape,
                             scratch_shapes=scratch_shapes, compiler_params=_params(len(grid), V7X_VMEM_LIMIT))(*args)
        return res, []
    kind, arrays = rider
    n = len(arrays)

    def with_rider(*refs):
        ins, c_ins = refs[:n_in], refs[n_in:n_in + n]
        outs, c_outs = refs[n_in + n:n_in + n + n_out], refs[n_in + n + n_out:n_in + 2 * n + n_out]
        rest = refs[n_in + 2 * n + n_out:]
        scratch, sems = rest[:-3], rest[-3:]
        ids = [pl.program_id(a) for a in range(len(grid))]
        first = functools.reduce(jnp.logical_and, [i == 0 for i in ids])
        last = functools.reduce(jnp.logical_and, [i == g - 1 for i, g in zip(ids, grid)])

        @pl.when(first)
        def _():
            _comm_start(kind, c_ins, c_outs, sems)

        body(*ins, *outs, *scratch)

        @pl.when(last)
        def _():
            _comm_wait(kind, c_ins, c_outs, sems)

    res = pl.pallas_call(
        with_rider, name=name, grid=grid, in_specs=list(in_specs) + [ANY_SPEC] * n,
        out_specs=list(out_specs) + [ANY_SPEC] * n, out_shape=list(out_shape) + _comm_out_shapes(kind, arrays),
        scratch_shapes=list(scratch_shapes) + _comm_sems(n), compiler_params=_params(len(grid), V7X_VMEM_LIMIT),
    )(*args, *arrays)
    return res[:n_out], res[n_out:]


def _band_fwd(q, k, v, bias, dil, t, rider=None):
    s_len = q.size // D_B
    length = s_len // dil
    w = t + 2 * BAND_HALF
    view = lambda a: a.reshape(length, dil * D_B)
    main, win = _band_specs(length, t)
    stat, _ = _band_specs(length, t, STAT_W)

    def body(q_ref, k0, k1, k2, v0, v1, v2, bias_ref, o_ref, lse_ref, s_buf, kt_buf, o_buf):
        lses = []
        i = pl.program_id(1)
        kt_buf[...] = _window((k0, k1, k2)).T
        vw = _window((v0, v1, v2))
        pos = i * t - BAND_HALF + lax.broadcasted_iota(jnp.int32, (1, w), 1)
        valid = (pos >= 0) & (pos < length)
        for h in range(N_HEADS_B):
            s_buf[h] = _dot(q_ref[:, HEAD_DIM * h:HEAD_DIM * (h + 1)], kt_buf[HEAD_DIM * h:HEAD_DIM * (h + 1), :])
        for h in range(N_HEADS_B):
            s = jnp.where(valid, s_buf[h] + bias_ref[h], NEG_BIG)
            m = jnp.max(s, axis=-1, keepdims=True)
            e = jnp.exp(s - m)
            den = jnp.sum(e, axis=-1, keepdims=True)
            o_buf[:, HEAD_DIM * h:HEAD_DIM * (h + 1)] = _dot(e.astype(BF16), _head(vw, h)) / den
            lses.append(m + jnp.log(den))
        o_ref[...] = o_buf[...].astype(BF16)
        lse_ref[...] = _pack_heads(lses)

    (o, lse), rode = _call_with_rider(
        body, f"band_fwd_d{dil}", (dil, length // t),
        [main] + win + win + [_full_spec((N_HEADS_B, t, w))], [main, stat],
        [SDS((length, dil * D_B), BF16), SDS((length, dil * STAT_W), F32)],
        [pltpu.VMEM((N_HEADS_B, t, w), F32), pltpu.VMEM((D_B, w), BF16), pltpu.VMEM((t, D_B), F32)],
        (view(q), view(k), view(k), view(k), view(v), view(v), view(v), bias), rider)
    return o, lse.reshape(s_len, STAT_W), rode


def _band_dq(q, do, lse, delta, k, v, bias, dil, t, rider=None):
    s_len = q.size // D_B
    length = s_len // dil
    w = t + 2 * BAND_HALF
    view = lambda a: a.reshape(length, dil * D_B)
    main, win = _band_specs(length, t)

    def body(q_ref, do_ref, lse_ref, delta_ref, k0, k1, k2, v0, v1, v2, bias_ref, dq_ref, dbias_ref,
             s_buf, dp_buf, kt_buf, vt_buf, dq_buf):
        i = pl.program_id(1)
        @pl.when((pl.program_id(0) == 0) & (i == 0))
        def _():
            dbias_ref[...] = jnp.zeros_like(dbias_ref)

        kw = _window((k0, k1, k2))
        kt_buf[...] = kw.T
        vt_buf[...] = _window((v0, v1, v2)).T
        pos = i * t - BAND_HALF + lax.broadcasted_iota(jnp.int32, (1, w), 1)
        valid = (pos >= 0) & (pos < length)
        for h in range(N_HEADS_B):
            cols = slice(HEAD_DIM * h, HEAD_DIM * (h + 1))
            s_buf[h] = _dot(q_ref[:, cols], kt_buf[cols, :])
            dp_buf[h] = _dot(do_ref[:, cols], vt_buf[cols, :])
        for h in range(N_HEADS_B):
            cols = slice(HEAD_DIM * h, HEAD_DIM * (h + 1))
            s = jnp.where(valid, s_buf[h] + bias_ref[h], NEG_BIG)
            p = jnp.exp(s - lse_ref[:, h:h + 1])
            ds = p * (dp_buf[h] - delta_ref[:, h:h + 1])
            dq_buf[:, cols] = _dot(ds.astype(BF16), _head(kw, h)) * SCORE_SCALE
            dbias_ref[h] += ds
        dq_ref[...] = dq_buf[...].astype(BF16)

    stat, _ = _band_specs(length, t, STAT_W)
    sview = lambda a: a.reshape(length, dil * STAT_W)
    (dq, dbias), rode = _call_with_rider(
        body, f"band_dq_d{dil}", (dil, length // t),
        [main, main, stat, stat] + win + win + [_full_spec((N_HEADS_B, t, w))],
        [main, _full_spec((N_HEADS_B, t, w))],
        [SDS((length, dil * D_B), BF16), SDS((N_HEADS_B, t, w), F32)],
        [pltpu.VMEM((N_HEADS_B, t, w), F32), pltpu.VMEM((N_HEADS_B, t, w), F32),
         pltpu.VMEM((D_B, w), BF16), pltpu.VMEM((D_B, w), BF16), pltpu.VMEM((t, D_B), F32)],
        (view(q), view(do), sview(lse), sview(delta), view(k), view(k), view(k), view(v), view(v), view(v), bias),
        rider)
    return dq, dbias, rode


def _band_dkv(k, v, q, do, lse, delta, bias_t, dil, t):
    s_len = q.size // D_B
    length = s_len // dil
    w = t + 2 * BAND_HALF
    view = lambda a: a.reshape(length, dil * D_B)
    main, win = _band_specs(length, t)

    def body(k_ref, v_ref, q0, q1, q2, d0, d1, d2, l0, l1, l2, e0, e1, e2, bias_ref, dk_ref, dv_ref,
             s_buf, dp_buf, kt_buf, vt_buf):
        i = pl.program_id(1)
        qw = _window((q0, q1, q2))
        dow = _window((d0, d1, d2))
        lsew = _window((l0, l1, l2))
        deltaw = _window((e0, e1, e2))
        pos = i * t - BAND_HALF + lax.broadcasted_iota(jnp.int32, (w, 1), 0)
        valid = (pos >= 0) & (pos < length)
        kt_buf[...] = k_ref[...].T
        vt_buf[...] = v_ref[...].T
        for h in range(N_HEADS_B):
            cols = slice(HEAD_DIM * h, HEAD_DIM * (h + 1))
            s_buf[h] = _dot(_head(qw, h), kt_buf[cols, :])
            dp_buf[h] = _dot(_head(dow, h), vt_buf[cols, :])
        qwt = qw.T
        dowt = dow.T
        dkt, dvt = [], []
        for h in range(N_HEADS_B):
            rows = slice(HEAD_DIM * h, HEAD_DIM * (h + 1))
            s = jnp.where(valid, s_buf[h] + bias_ref[h], NEG_BIG)
            p = jnp.exp(s - lsew[:, h:h + 1])
            ds = p * (dp_buf[h] - deltaw[:, h:h + 1])
            dvt.append(_dot(dowt[rows, :], p.astype(BF16)))
            dkt.append(_dot(qwt[rows, :], ds.astype(BF16)))
        dv_ref[...] = jnp.concatenate(dvt, axis=0).T.astype(BF16)
        dk_ref[...] = jnp.concatenate(dkt, axis=0).T.astype(BF16)

    _, swin = _band_specs(length, t, STAT_W)
    sview = lambda a: a.reshape(length, dil * STAT_W)
    dk, dv = pl.pallas_call(
        body, name=f"band_dkv_d{dil}", grid=(dil, length // t),
        in_specs=[main, main] + win + win + swin + swin + [_full_spec((N_HEADS_B, w, t))],
        out_specs=[main, main],
        out_shape=(SDS((length, dil * D_B), BF16), SDS((length, dil * D_B), BF16)),
        scratch_shapes=[pltpu.VMEM((N_HEADS_B, w, t), F32), pltpu.VMEM((N_HEADS_B, w, t), F32),
                        pltpu.VMEM((D_B, t), BF16), pltpu.VMEM((D_B, t), BF16)],
        compiler_params=_params(2, V7X_VMEM_LIMIT),
    )(view(k), view(v), view(q), view(q), view(q), view(do), view(do), view(do), sview(lse), sview(lse), sview(lse),
      sview(delta), sview(delta), sview(delta), bias_t)
    return dk, dv


def _t5_bucket_np(rel):
    nb = N_BUCKETS // 2
    max_exact = nb // 2
    side = np.where(rel > 0, nb, 0)
    n = np.abs(rel)
    ratio = np.maximum(n, max_exact).astype(np.float32) / np.float32(max_exact)
    large = max_exact + (np.log(ratio) / np.float32(math.log(MAX_DISTANCE / max_exact))
                         * np.float32(nb - max_exact)).astype(np.int32)
    large = np.minimum(large, nb - 1)
    return (side + np.where(n < max_exact, n, large)).astype(np.int32)


def _band_buckets(dil, t):
    rel = np.arange(t + 2 * BAND_HALF)[None, :] - BAND_HALF - np.arange(t)[:, None]
    bucket = _t5_bucket_np(np.clip(rel, -BAND_HALF, BAND_HALF) * dil)
    return np.where(np.abs(rel) <= BAND_HALF, bucket, -1).astype(np.int32)


def _toeplitz(vals, rows, cols):
    heads = vals.shape[0]
    period = -(-(rows + cols) // 128) * 128 + 1
    gap = period - (rows + cols - 1)
    vec = jnp.concatenate([vals[:, rows - 1:], jnp.zeros((heads, gap), vals.dtype), vals[:, :rows - 1]], axis=1)
    flat = jnp.broadcast_to(vec[:, None, :], (heads, rows, period)).reshape(heads, rows * period)
    return flat[:, :rows * (period - 1)].reshape(heads, rows, period - 1)[:, :, :cols]


def _bias_tiles(rel_bias, dil, t):
    w = t + 2 * BAND_HALF
    rel = np.arange(-BAND_HALF, BAND_HALF + 1)
    bucket = _t5_bucket_np(rel * dil)
    runs, start = [], 0
    for i in range(1, len(bucket) + 1):
        if i == len(bucket) or bucket[i] != bucket[start]:
            b = int(bucket[start])
            runs.append(jnp.broadcast_to(rel_bias[b:b + 1], (i - start, N_HEADS_B)))
            start = i
    per_rel = jnp.concatenate(runs, axis=0).T

    def diagonals(lo, hi):
        left = jnp.full((N_HEADS_B, max(0, -BAND_HALF - lo)), NEG_BIG, F32)
        right = jnp.full((N_HEADS_B, max(0, hi - BAND_HALF)), NEG_BIG, F32)
        return jnp.concatenate([left, per_rel, right], axis=1)

    tile = _toeplitz(diagonals(-(t - 1) - BAND_HALF, w - 1 - BAND_HALF), t, w)
    twin = _toeplitz(diagonals(-(w - 1) + BAND_HALF, t - 1 + BAND_HALF), w, t)
    return tile, twin


def _dbias_reduce(dbias, buckets):
    n = len(dbias)

    def body(*refs):
        db_refs, bk_refs, o_ref = refs[:n], refs[n:2 * n], refs[2 * n]
        row = lax.broadcasted_iota(jnp.int32, (N_BUCKETS, 128), 0)
        lane = lax.broadcasted_iota(jnp.int32, (N_BUCKETS, 128), 1)

        def per_bucket(b, out):
            for pat in range(n):
                msk = bk_refs[pat][...] == b
                for h in range(N_HEADS_B):
                    tot = jnp.sum(jnp.where(msk, db_refs[pat][h], 0.0), axis=-1, keepdims=True)
                    tot = jnp.sum(tot, axis=0, keepdims=True)
                    out = out + jnp.where((row == b) & (lane == h), tot, 0.0)
            return out

        o_ref[...] = lax.fori_loop(0, N_BUCKETS, per_bucket, jnp.zeros((N_BUCKETS, 128), F32))

    return pl.pallas_call(
        body, name="dbias_reduce", out_shape=SDS((N_BUCKETS, 128), F32),
        compiler_params=pltpu.CompilerParams(vmem_limit_bytes=V7X_VMEM_LIMIT),
    )(*dbias, *buckets)


def _attn_out_fwd(ya, ob, lb, x, g_a, g_b, w_out, g_post, tm):
    s_len = ya.shape[0]
    assert len(ob) == 1 + len(DILATIONS) == 3

    def body(ya_ref, o0, o1, o2, l0, l1, l2, x_ref, ga_ref, gb_ref, w_ref, gp_ref, unsort1, unsort2,
             h1_ref, yo_ref, yb_ref, lse_ref, ycat_ref):
        def plain_rows(o_ref, unsort_ref, dil):
            sorted_rows = jnp.concatenate([o_ref[:, r * D_B:(r + 1) * D_B] for r in range(dil)], axis=0)
            return _dot(unsort_ref[...], sorted_rows)

        m = jnp.maximum(jnp.maximum(l0[...], l1[...]), l2[...])
        w0, w1, w2 = jnp.exp(l0[...] - m), jnp.exp(l1[...] - m), jnp.exp(l2[...] - m)
        wsum = w0 + w1 + w2
        yb = (_spread_heads(w0 / wsum) * o0[...].astype(F32)
              + _spread_heads(w1 / wsum) * plain_rows(o1, unsort1, DILATIONS[0])
              + _spread_heads(w2 / wsum) * plain_rows(o2, unsort2, DILATIONS[1]))
        yb_ref[...] = yb
        lse_ref[...] = m + jnp.log(wsum)
        yan, _ = _rms_fwd(ya_ref[...], ga_ref[...])
        ybn, _ = _rms_fwd(yb, gb_ref[...])
        yan, ybn = yan.astype(BF16), ybn.astype(BF16)
        ycat_ref[:, :D_A] = yan
        ycat_ref[:, D_A:] = ybn
        yo = _dot(yan, w_ref[:D_A, :]) + _dot(ybn, w_ref[D_A:, :])
        yo_ref[...] = yo
        post, _ = _rms_fwd(yo, gp_ref[...])
        h1_ref[...] = x_ref[...] + post

    half, full, stat = _row_spec(tm, D_A), _row_spec(tm, D_MODEL), _row_spec(tm, STAT_W)
    view_specs, _ = _dilated_specs(s_len, tm, BF16)
    return pl.pallas_call(
        body, name="attn_out_fwd", grid=(s_len // tm,),
        in_specs=[half] * 2 + view_specs + [stat] * 3
                 + [full, _full_spec((1, D_A)), _full_spec((1, D_B)), _full_spec((D_MODEL, D_MODEL)),
                    _full_spec((1, D_MODEL))] + [_full_spec((tm, tm))] * len(DILATIONS),
        out_specs=[full, full, half, stat, full],
        out_shape=(SDS((s_len, D_MODEL), F32), SDS((s_len, D_MODEL), F32), SDS((s_len, D_B), F32),
                   SDS((s_len, STAT_W), F32), SDS((s_len, D_MODEL), BF16)),
        compiler_params=_params(1, V7X_VMEM_LIMIT),
    )(ya, ob[0], ob[1], ob[2], lb[0], lb[1], lb[2], x, g_a, g_b, w_out, g_post,
      *[_dilation_perm(tm, d).T for d in DILATIONS])


def _attn_out_bwd(dh1, yo, ya, yb, g_a, g_b, w_out, g_post, tm):
    s_len = ya.shape[0]
    n_d = len(DILATIONS)

    def body(dh1_ref, yo_ref, ya_ref, yb_ref, ga_ref, gb_ref, w_ref, gp_ref, *rest):
        perms, views = rest[:n_d], rest[n_d + 8:]
        dyo_ref, dya_ref, dyb_ref, dela_ref, delb_ref, dgp_ref, dga_ref, dgb_ref = rest[n_d:n_d + 8]
        _zero_at_first(pl.program_id(0) == 0, dgp_ref, dga_ref, dgb_ref)
        dyo, dgp = _rms_bwd(yo_ref[...], gp_ref[...], dh1_ref[...])
        dyo = dyo.astype(BF16)
        dyo_ref[...] = dyo
        dgp_ref[...] += dgp
        dya_n = _dot_nt(dyo, w_ref[:D_A, :])
        dyb_n = _dot_nt(dyo, w_ref[D_A:, :])
        ya, yb = ya_ref[...], yb_ref[...]
        dya, dga = _rms_bwd(ya, ga_ref[...], dya_n)
        dyb, dgb = _rms_bwd(yb, gb_ref[...], dyb_n)
        dga_ref[...] += dga
        dgb_ref[...] += dgb
        dya_ref[...] = dya.astype(BF16)
        dyb_ref[...] = dyb.astype(BF16)
        for j, d in enumerate(DILATIONS):
            _store_dilated(views[j], perms[j], dyb.astype(BF16), d)
        dela_ref[...] = _head_sum(dya * ya)
        prod = dyb * yb
        head = lax.broadcasted_iota(jnp.int32, prod.shape, 1) >> 6
        delb_ref[...] = _pack_heads([jnp.sum(jnp.where(head == h, prod, 0.0), axis=-1, keepdims=True)
                                     for h in range(N_HEADS_B)])

    half, full = _row_spec(tm, D_A), _row_spec(tm, D_MODEL)
    view_specs, view_shapes = _dilated_specs(s_len, tm, BF16)
    return pl.pallas_call(
        body, name="attn_out_bwd", grid=(s_len // tm,),
        in_specs=[full, full, half, half, _full_spec((1, D_A)), _full_spec((1, D_B)),
                  _full_spec((D_MODEL, D_MODEL)), _full_spec((1, D_MODEL))] + [_full_spec((tm, tm))] * n_d,
        out_specs=[full, half, half, half, _row_spec(tm, STAT_W), _full_spec((1, D_MODEL)), _full_spec((1, D_A)),
                   _full_spec((1, D_B))] + view_specs,
        out_shape=[SDS((s_len, D_MODEL), BF16), SDS((s_len, D_A), BF16), SDS((s_len, D_B), BF16),
                   SDS((s_len, D_A), F32), SDS((s_len, STAT_W), F32), SDS((1, D_MODEL), F32), SDS((1, D_A), F32),
                   SDS((1, D_B), F32)] + view_shapes,
        compiler_params=_params(1, V7X_VMEM_LIMIT),
    )(dh1, yo, ya, yb, g_a, g_b, w_out, g_post, *[_dilation_perm(tm, d) for d in DILATIONS])


FF_CHUNK = 1024


def _mlp_fwd(h1, g_pre, w1, w2, g_post, tm):
    s_len = h1.shape[0]

    def body(h1_ref, gpre_ref, w1_ref, w2_ref, gpost_ref, h2_ref, fo_ref, xn_ref):
        h1v = h1_ref[...]
        xn, _ = _rms_fwd(h1v, gpre_ref[...])
        xn = xn.astype(BF16)
        xn_ref[...] = xn
        fo = jnp.zeros((tm, D_MODEL), F32)
        for c in range(D_FF // FF_CHUNK):
            cols = slice(c * FF_CHUNK, (c + 1) * FF_CHUNK)
            u = jnp.maximum(_dot(xn, w1_ref[:, cols]), 0.0)
            fo = fo + _dot((u * u).astype(BF16), w2_ref[cols, :])
        fo_ref[...] = fo
        post, _ = _rms_fwd(fo, gpost_ref[...])
        h2_ref[...] = h1v + post

    full = _row_spec(tm, D_MODEL)
    return pl.pallas_call(
        body, name="mlp_fwd", grid=(s_len // tm,),
        in_specs=[full, _full_spec((1, D_MODEL)), _const_spec((D_MODEL, D_FF)), _const_spec((D_FF, D_MODEL)),
                  _full_spec((1, D_MODEL))],
        out_specs=[full, full, full],
        out_shape=(SDS((s_len, D_MODEL), F32), SDS((s_len, D_MODEL), F32), SDS((s_len, D_MODEL), BF16)),
        compiler_params=_params(1, V7X_VMEM_LIMIT),
    )(h1, g_pre, w1, w2, g_post)


def _mlp_bwd(h1, dh2, fo, xn, g_pre, w1, w2, g_post, tm):
    s_len = h1.shape[0]

    def body(h1_ref, dh2_ref, fo_ref, xn_ref, gpre_ref, w1_ref, w2_ref, gpost_ref,
             dh1_ref, dfo_ref, du_ref, f_ref, dgpost_ref, dgpre_ref):
        _zero_at_first(pl.program_id(0) == 0, dgpost_ref, dgpre_ref)
        dh2 = dh2_ref[...]
        dfo, dgpost = _rms_bwd(fo_ref[...], gpost_ref[...], dh2)
        dfo = dfo.astype(BF16)
        dfo_ref[...] = dfo
        dgpost_ref[...] += dgpost
        xn = xn_ref[...]
        dxn = jnp.zeros((tm, D_MODEL), F32)
        for c in range(D_FF // FF_CHUNK):
            cols = slice(c * FF_CHUNK, (c + 1) * FF_CHUNK)
            u = jnp.maximum(_dot(xn, w1_ref[:, cols]), 0.0)
            f_ref[:, cols] = (u * u).astype(BF16)
            du = (_dot_nt(dfo, w2_ref[cols, :]) * (2.0 * u)).astype(BF16)
            du_ref[:, cols] = du
            dxn = dxn + _dot_nt(du, w1_ref[:, cols])
        dx, dgpre = _rms_bwd(h1_ref[...], gpre_ref[...], dxn)
        dgpre_ref[...] += dgpre
        dh1_ref[...] = dh2 + dx

    full, wide = _row_spec(tm, D_MODEL), _row_spec(tm, D_FF)
    return pl.pallas_call(
        body, name="mlp_bwd", grid=(s_len // tm,),
        in_specs=[full, full, full, full, _full_spec((1, D_MODEL)), _const_spec((D_MODEL, D_FF)),
                  _const_spec((D_FF, D_MODEL)), _full_spec((1, D_MODEL))],
        out_specs=[full, full, wide, wide, _full_spec((1, D_MODEL)), _full_spec((1, D_MODEL))],
        out_shape=(SDS((s_len, D_MODEL), F32), SDS((s_len, D_MODEL), BF16), SDS((s_len, D_FF), BF16),
                   SDS((s_len, D_FF), BF16), SDS((1, D_MODEL), F32), SDS((1, D_MODEL), F32)),
        compiler_params=_params(1, V7X_VMEM_LIMIT),
    )(h1, dh2, fo, xn, g_pre, w1, w2, g_post)


def _ple_fwd_bwd(h2, p, target, g_ple, w_gate, w_proj, tm):
    s_len = h2.shape[0]

    def body(h2_ref, p_ref, t_ref, g_ref, wg_ref, wp_ref, dh2_ref, loss_ref, dg_ref, xn_ref, dgl_ref, dpp_ref, pb_ref):
        _zero_at_first(pl.program_id(0) == 0, loss_ref, dg_ref)
        h2 = h2_ref[...]
        g = g_ref[...]
        xn, _ = _rms_fwd(h2, g)
        xn = xn.astype(BF16)
        xn_ref[...] = xn
        gate = 1.0 / (1.0 + jnp.exp(-_dot(xn, wg_ref[...])))
        pb = p_ref[...].astype(BF16)
        pb_ref[...] = pb
        pp = _dot(pb, wp_ref[...])
        diff = h2 + gate * pp - t_ref[...]
        loss_ref[...] += jnp.full((8, 128), jnp.sum(diff * diff), F32)
        dh3 = diff * (1.0 / D_MODEL)
        dpp_ref[...] = (dh3 * gate).astype(BF16)
        dgl = (dh3 * pp * gate * (1.0 - gate)).astype(BF16)
        dgl_ref[...] = dgl
        dx, dg = _rms_bwd(h2, g, _dot_nt(dgl, wg_ref[...]))
        dg_ref[...] += dg
        dh2_ref[...] = dh3 + dx

    full = _row_spec(tm, D_MODEL)
    return pl.pallas_call(
        body, name="ple_fwd_bwd", grid=(s_len // tm,),
        in_specs=[full, _row_spec(tm, D_PLE), full, _full_spec((1, D_MODEL)), _full_spec((D_MODEL, D_MODEL)),
                  _full_spec((D_PLE, D_MODEL))],
        out_specs=[full, _full_spec((8, 128)), _full_spec((1, D_MODEL)), full, full, full, _row_spec(tm, D_PLE)],
        out_shape=(SDS((s_len, D_MODEL), F32), SDS((8, 128), F32), SDS((1, D_MODEL), F32), SDS((s_len, D_MODEL), BF16),
                   SDS((s_len, D_MODEL), BF16), SDS((s_len, D_MODEL), BF16), SDS((s_len, D_PLE), BF16)),
        compiler_params=_params(1, V7X_VMEM_LIMIT),
    )(h2, p, target, g_ple, w_gate, w_proj)


def _inproj_bwd(dqs, dk_t, dv_t, parts_b, qa_raw, ka_raw, x, dh1, g_pre, w_in, cos, sin, gq, gk, tm):
    s_len = x.shape[0]
    tk = dk_t.shape[3]

    n_d = len(DILATIONS)
    n_parts = 3 * (1 + n_d)

    def body(dqs_ref, dkt_ref, dvt_ref, *rest):
        parts, rest = rest[:n_parts], rest[n_parts:]
        (qa_ref, ka_ref, x_ref, dh1_ref, g_ref, w_ref, cos_ref, sin_ref, gq_ref, gk_ref), rest = rest[:10], rest[10:]
        unsort, (dx_ref, dproj_ref, dg_ref, dgq_ref, dgk_ref) = rest[:n_d], rest[n_d:]

        def mixer_b_grad(a):
            own = parts[a * (1 + n_d):(a + 1) * (1 + n_d)]
            total = own[0][...].astype(F32)
            for j, d in enumerate(DILATIONS):
                sorted_rows = jnp.concatenate([own[1 + j][:, r * D_B:(r + 1) * D_B] for r in range(d)], axis=0)
                total = total + _dot(unsort[j][...], sorted_rows)
            return total.astype(BF16)

        _zero_at_first(pl.program_id(0) == 0, dg_ref, dgq_ref, dgk_ref)
        cos, sin = cos_ref[...], sin_ref[...]
        dkn = dkt_ref[...].reshape(D_KV_A, tm).T
        dva = dvt_ref[...].reshape(D_KV_A, tm).T
        dqa, dgq = _head_norm_rope_bwd(qa_ref[...], gq_ref[...], jnp.tile(cos, (1, D_A // D_KV_A)),
                                       jnp.tile(sin, (1, D_A // D_KV_A)), dqs_ref[...])
        dka, dgk = _head_norm_rope_bwd(ka_ref[...], gk_ref[...], cos, sin, dkn)
        dgq_ref[...] += dgq
        dgk_ref[...] += dgk
        dproj_ref[:, OFF_QA:OFF_KA] = dqa.astype(BF16)
        dproj_ref[:, OFF_KA:OFF_VA] = dka.astype(BF16)
        dproj_ref[:, OFF_VA:OFF_QB] = dva.astype(BF16)
        dproj_ref[:, OFF_QB:OFF_KB] = mixer_b_grad(0)
        dproj_ref[:, OFF_KB:OFF_VB] = mixer_b_grad(1)
        dproj_ref[:, OFF_VB:D_IN] = mixer_b_grad(2)
        dxn = _dot_nt(dproj_ref[...], w_ref[...])
        dx, dg = _rms_bwd(x_ref[...], g_ref[...], dxn)
        dg_ref[...] += dg
        dx_ref[...] = dh1_ref[...] + dx

    half, kvw, full = _row_spec(tm, D_A), _row_spec(tm, D_KV_A), _row_spec(tm, D_MODEL)
    per_chunk = tk // tm
    chunk_t = pl.BlockSpec((N_KV_A, None, HEAD_DIM, tm), lambda i: (0, i // per_chunk, 0, i % per_chunk))
    view_specs, _ = _dilated_specs(s_len, tm, BF16)
    return pl.pallas_call(
        body, name="inproj_bwd", grid=(s_len // tm,),
        in_specs=[half, chunk_t, chunk_t] + ([half] + view_specs) * 3
                 + [half, kvw, full, full, _full_spec((1, D_MODEL)), _const_spec((D_MODEL, D_IN)), kvw, kvw,
                    _full_spec((1, D_A)), _full_spec((1, D_KV_A))] + [_full_spec((tm, tm))] * n_d,
        out_specs=[full, _row_spec(tm, D_IN), _full_spec((1, D_MODEL)), _full_spec((1, D_A)), _full_spec((1, D_KV_A))],
        out_shape=(SDS((s_len, D_MODEL), F32), SDS((s_len, D_IN), BF16), SDS((1, D_MODEL), F32), SDS((1, D_A), F32),
                   SDS((1, D_KV_A), F32)),
        compiler_params=_params(1, V7X_VMEM_LIMIT),
    )(dqs, dk_t, dv_t, *parts_b, qa_raw, ka_raw, x, dh1, g_pre, w_in, cos, sin, gq, gk,
      *[_dilation_perm(tm, d).T for d in DILATIONS])


def _weight_grad(a, b, name, tk1, tn, tm):
    s_len, k1 = a.shape
    n = b.shape[1]
    steps = s_len // tm

    def body(a_ref, b_ref, o_ref, acc):
        r = pl.program_id(2)
        _zero_at_first(r == 0, acc)
        acc[...] += _dot_tn(a_ref[...], b_ref[...])

        @pl.when(r == steps - 1)
        def _():
            o_ref[...] = acc[...].astype(BF16)

    return pl.pallas_call(
        body, name=name, grid=(k1 // tk1, n // tn, steps),
        in_specs=[pl.BlockSpec((tm, tk1), lambda i, j, r: (r, i)), pl.BlockSpec((tm, tn), lambda i, j, r: (r, j))],
        out_specs=pl.BlockSpec((tk1, tn), lambda i, j, r: (i, j)),
        out_shape=SDS((k1, n), BF16),
        scratch_shapes=[pltpu.VMEM((tk1, tn), F32)],
        compiler_params=_params(3, V7X_VMEM_LIMIT),
    )(a, b)


def _my_index():
    return 4 * lax.axis_index("x") + 2 * lax.axis_index("y") + lax.axis_index("c")


def _peer(k):
    f = k + 1
    x, y, c = lax.axis_index("x"), lax.axis_index("y"), lax.axis_index("c")
    return (x ^ ((f >> 2) & 1), y ^ ((f >> 1) & 1), c ^ (f & 1))


GATHER, EXCHANGE = "gather", "exchange"
ANY_SPEC = pl.BlockSpec(memory_space=pl.ANY)


def _comm_out_shapes(kind, arrays):
    return [SDS((N_DEV,) + a.shape if kind == GATHER else a.shape, a.dtype) for a in arrays]


def _comm_sems(n):
    return [pltpu.SemaphoreType.DMA((n, N_DEV - 1)), pltpu.SemaphoreType.DMA((n, N_DEV - 1)), pltpu.SemaphoreType.DMA((n,))]


def _comm_copies(kind, ins, outs, send_sems, recv_sems, local_sems):
    me = _my_index()
    local, remote = [], []
    for a, (src, dst) in enumerate(zip(ins, outs)):
        local.append(pltpu.make_async_copy(src if kind == GATHER else src.at[me], dst.at[me], local_sems.at[a]))
        for k in range(N_DEV - 1):
            px, py, pc = _peer(k)
            remote.append(pltpu.make_async_remote_copy(
                src_ref=src if kind == GATHER else src.at[4 * px + 2 * py + pc], dst_ref=dst.at[me],
                send_sem=send_sems.at[a, k], recv_sem=recv_sems.at[a, k],
                device_id=(px, py, pc), device_id_type=pl.DeviceIdType.MESH))
    return local, remote


def _comm_start(kind, ins, outs, sems):
    local, remote = _comm_copies(kind, ins, outs, *sems)
    for cp in local + remote:
        cp.start()


def _comm_wait(kind, ins, outs, sems):
    local, remote = _comm_copies(kind, ins, outs, *sems)
    for cp in local:
        cp.wait()
    for cp in remote:
        cp.wait_send()
    for cp in remote:
        cp.wait_recv()


def _collective(kind, arrays, name):
    n = len(arrays)

    def body(*refs):
        ins, outs, sems = refs[:n], refs[n:2 * n], refs[2 * n:]
        _comm_start(kind, ins, outs, sems)
        _comm_wait(kind, ins, outs, sems)

    return pl.pallas_call(
        body, name=name, in_specs=[ANY_SPEC] * n, out_specs=[ANY_SPEC] * n,
        out_shape=_comm_out_shapes(kind, arrays), scratch_shapes=_comm_sems(n),
    )(*arrays)


def _sum_adamw(parts, w, m, v, name, tr):
    rows, cols = w.shape
    c1 = 1.0 / (1.0 - ADAM_B1 ** ADAM_STEP)
    c2 = 1.0 / (1.0 - ADAM_B2 ** ADAM_STEP)

    def body(p_ref, w_ref, m_ref, v_ref, g_ref, d_ref, nm_ref, nv_ref):
        g = p_ref[0].astype(F32)
        for j in range(1, N_DEV):
            g = g + p_ref[j].astype(F32)
        g_ref[...] = g
        nm = ADAM_B1 * m_ref[...] + (1.0 - ADAM_B1) * g
        nv = ADAM_B2 * v_ref[...] + (1.0 - ADAM_B2) * (g * g)
        nm_ref[...] = nm
        nv_ref[...] = nv
        d_ref[...] = -ADAM_LR * ((nm * c1) / (jnp.sqrt(nv * c2) + ADAM_EPS) + ADAM_WD * w_ref[...])

    blk = pl.BlockSpec((tr, cols), lambda i: (i, 0))
    return pl.pallas_call(
        body, name=name, grid=(rows // tr,),
        in_specs=[pl.BlockSpec((N_DEV, tr, cols), lambda i: (0, i, 0)), blk, blk, blk],
        out_specs=[blk] * 4, out_shape=[SDS((rows, cols), F32)] * 4,
        compiler_params=_params(1, V7X_VMEM_LIMIT),
    )(parts, w, m, v)


_SMALL = (("g_attn_pre", 1024), ("g_q", 64), ("g_k", 64), ("g_out_a", 512), ("g_out_b", 512), ("g_attn_post", 1024),
          ("rel_bias", 256), ("g_mlp_pre", 1024), ("g_mlp_post", 1024), ("g_ple", 1024))
_SLAB_ROWS = 56


def _pack_small(vals):
    rows = []
    for (name, size) in _SMALL:
        flat = vals[name].reshape(-1).astype(F32)
        padded = -(-size // 128) * 128
        rows.append(jnp.pad(flat, (0, padded - size)).reshape(padded // 128, 128))
    slab = jnp.concatenate(rows, axis=0)
    return jnp.pad(slab, ((0, _SLAB_ROWS - slab.shape[0]), (0, 0)))


def _unpack_small(slab, shapes):
    out, row = {}, 0
    for (name, size) in _SMALL:
        nrow = -(-size // 128)
        out[name] = slab[row:row + nrow].reshape(-1)[:size].reshape(shapes[name])
        row += nrow
    return out


def _rope_tables(s_len):
    rows = s_len // GRID_W
    row = jnp.broadcast_to(jnp.arange(rows)[:, None], (rows, GRID_W)).reshape(-1).astype(F32)
    col = jnp.broadcast_to(jnp.arange(GRID_W)[None, :], (rows, GRID_W)).reshape(-1).astype(F32)
    n_axis = ROPE_HALF // 2
    inv_freq = ROPE_THETA ** (-jnp.arange(n_axis, dtype=F32) / n_axis)
    ang = jnp.concatenate([row[:, None] * inv_freq, col[:, None] * inv_freq], axis=-1)
    cos, sin = jnp.cos(ang), jnp.sin(ang)
    cos = jnp.tile(jnp.concatenate([cos, cos], axis=-1), (1, N_KV_A))
    sin = jnp.tile(jnp.concatenate([-sin, sin], axis=-1), (1, N_KV_A))
    return cos, sin


_RIDERS = {1: ("w_out", "w_ff1"), 4: ("w_ff2",), 16: ("w_ple_gate", "w_ple_proj")}
_GRAD_RIDERS = {1: ("w_ff1",), 4: ("w_ff2",), 16: ("w_ple_gate", "w_ple_proj", "w_out")}


ROW_TILE = 512
MLP_ROW_TILE = 256
ATTN_Q_TILE = 128
ATTN_K_CHUNK = 2048
BAND_TILE = 256
GRAD_ROW_TILE = 1024


def _local_step(x, p, target, w_in, shards, small):
    axis_of = dict(_BIG)
    s_len = x.shape[0]
    tm = min(ROW_TILE, s_len)
    tq = min(ATTN_Q_TILE, s_len)
    tk = min(ATTN_K_CHUNK, s_len // 2)
    cos, sin = _rope_tables(s_len)
    gq = jnp.tile(small["g_q"], (1, N_HEADS_A))
    gk = jnp.tile(small["g_k"], (1, N_KV_A))

    qa_raw, ka_raw, qs, k_maj, kt, vt, qb, kb, vb, xn1, *views = _inproj_fwd(
        x, small["g_attn_pre"], w_in, cos, sin, gq, gk, tm, tk)
    n_d = len(DILATIONS)
    qb, kb, vb = ({1: plain, **dict(zip(DILATIONS, views[a * n_d:(a + 1) * n_d]))} for a, plain in enumerate((qb, kb, vb)))

    ya, lse_a = _attn_fwd(qs, k_maj, vt, tq, tk)

    ob, lb, tiles, full = [], [], [], {}
    for (_, dil) in DILATED_PATTERNS:
        t = min(BAND_TILE, s_len // dil)
        bias, bias_t = _bias_tiles(small["rel_bias"], dil, t)
        tiles.append((t, bias, bias_t, _band_buckets(dil, t)))
        o, l, gathered = _band_fwd(qb[dil], kb[dil], vb[dil], bias, dil, t, (GATHER, [shards[n] for n in _RIDERS[dil]]))
        ob.append(o)
        lb.append(l)
        full.update({n: _assemble(g, axis_of[n]) for n, g in zip(_RIDERS[dil], gathered)})
    w_out, w_ff1, w_ff2, w_gate, w_proj = (full[n] for n in ("w_out", "w_ff1", "w_ff2", "w_ple_gate", "w_ple_proj"))

    h1, yo, yb, lse_b, ycat = _attn_out_fwd(ya, ob, lb, x, small["g_out_a"], small["g_out_b"], w_out,
                                            small["g_attn_post"], tm)
    h2, fo, xn2 = _mlp_fwd(h1, small["g_mlp_pre"], w_ff1, w_ff2, small["g_mlp_post"], min(MLP_ROW_TILE, s_len))
    dh2, loss_part, dg_ple, xn3, dgl, dpp, pb = _ple_fwd_bwd(h2, p, target, small["g_ple"], w_gate, w_proj, tm)
    dh1, dfo, du, f, dg_mlp_post, dg_mlp_pre = _mlp_bwd(h1, dh2, fo, xn2, small["g_mlp_pre"], w_ff1, w_ff2,
                                                          small["g_mlp_post"], min(MLP_ROW_TILE, s_len))
    dyo, dya, dyb, delta_a, delta_b, dg_attn_post, dg_out_a, dg_out_b, *dyb_views = _attn_out_bwd(
        dh1, yo, ya, yb, small["g_out_a"], small["g_out_b"], w_out, small["g_attn_post"], tm)
    dyb = {1: dyb, **dict(zip(DILATIONS, dyb_views))}

    dqs, dk_t, dv_t = _attn_bwd(qs, dya, lse_a, delta_a, kt, k_maj, vt, tq, tk)

    tg = min(GRAD_ROW_TILE, s_len)
    grads = {
        "w_out": _weight_grad(ycat, dyo, "grad_w_out", D_MODEL, D_MODEL, tg),
        "w_ff1": _weight_grad(xn2, du, "grad_w_ff1", D_MODEL, 1024, tg),
        "w_ff2": _weight_grad(f, dfo, "grad_w_ff2", 1024, D_MODEL, tg),
        "w_ple_gate": _weight_grad(xn3, dgl, "grad_w_ple_gate", D_MODEL, D_MODEL, tg),
        "w_ple_proj": _weight_grad(pb, dpp, "grad_w_ple_proj", D_PLE, D_MODEL, tg),
    }

    dqb, dkb, dvb, dbias, received = {}, {}, {}, [], {}
    for (_, dil), (t, bias, bias_t, _) in zip(DILATED_PATTERNS, tiles):
        rider = (EXCHANGE, [_cut(grads[n], axis_of[n]) for n in _GRAD_RIDERS[dil]])
        dqb[dil], db, got = _band_dq(qb[dil], dyb[dil], lse_b, delta_b, kb[dil], vb[dil], bias, dil, t, rider)
        dkb[dil], dvb[dil] = _band_dkv(kb[dil], vb[dil], qb[dil], dyb[dil], lse_b, delta_b, bias_t, dil, t)
        dbias.append(db)
        received.update(zip(_GRAD_RIDERS[dil], got))
    d_rel = _dbias_reduce(dbias, [jnp.asarray(tl[3]) for tl in tiles])[:, :N_HEADS_B]

    parts_b = [part[d] for part in (dqb, dkb, dvb) for d in (1,) + DILATIONS]
    dx, dproj, dg_attn_pre, dgq_lanes, dgk_lanes = _inproj_bwd(
        dqs, dk_t, dv_t, parts_b, qa_raw, ka_raw, x, dh1, small["g_attn_pre"], w_in, cos, sin, gq, gk, tm)
    grad_w_in = _weight_grad(xn1, dproj, "grad_w_in", D_MODEL, 768, tg)
    small_grads = {
        "g_attn_pre": dg_attn_pre, "g_q": dgq_lanes.reshape(N_HEADS_A, HEAD_DIM).sum(0, keepdims=True),
        "g_k": dgk_lanes.reshape(N_KV_A, HEAD_DIM).sum(0, keepdims=True), "g_out_a": dg_out_a, "g_out_b": dg_out_b,
        "g_attn_post": dg_attn_post, "rel_bias": d_rel, "g_mlp_pre": dg_mlp_pre, "g_mlp_post": dg_mlp_post,
        "g_ple": dg_ple,
    }
    return loss_part[0, 0], dx, grad_w_in, received, small_grads


_BIG = (("w_in", 1), ("w_out", 0), ("w_ff1", 1), ("w_ff2", 0), ("w_ple_gate", 0), ("w_ple_proj", 1))


def _assemble(gathered, axis):
    if axis == 0:
        return gathered.reshape(-1, gathered.shape[2])
    return gathered.transpose(1, 0, 2).reshape(gathered.shape[1], -1)


def _cut(full, axis):
    if axis == 0:
        return full.reshape(N_DEV, full.shape[0] // N_DEV, full.shape[1])
    return full.reshape(full.shape[0], N_DEV, full.shape[1] // N_DEV).transpose(1, 0, 2)


def kernel(x, p, w_in, g_attn_pre, g_q, g_k, g_out_a, g_out_b, w_out, g_attn_post, rel_bias, g_mlp_pre, w_ff1, w_ff2, g_mlp_post, g_ple, w_ple_gate, w_ple_proj, loss_target, m_w_in, m_g_attn_pre, m_g_q, m_g_k, m_g_out_a, m_g_out_b, m_w_out, m_g_attn_post, m_rel_bias, m_g_mlp_pre, m_w_ff1, m_w_ff2, m_g_mlp_post, m_g_ple, m_w_ple_gate, m_w_ple_proj, v_w_in, v_g_attn_pre, v_g_q, v_g_k, v_g_out_a, v_g_out_b, v_w_out, v_g_attn_post, v_rel_bias, v_g_mlp_pre, v_w_ff1, v_w_ff2, v_g_mlp_post, v_g_ple, v_w_ple_gate, v_w_ple_proj):
    given = dict(locals())
    small_names = [n for n, _ in _SMALL]
    small = {n: given[n] for n in small_names}
    shards = {n: given[n][0] for n, _ in _BIG}

    bf16_shards = {n: shards[n].astype(BF16) for n, _ in _BIG}
    (gathered_w_in,) = _collective(GATHER, [bf16_shards["w_in"]], "gather_w_in")

    loss_part, dx, grad_w_in, received, small_grads = _local_step(
        x[0], p[0, 0], loss_target[0], _assemble(gathered_w_in, 1), bf16_shards, small)

    loss_at = np.zeros((_SLAB_ROWS, 128), bool)
    loss_at[-1, 0] = True
    slab = jnp.where(loss_at, loss_part * (0.5 / D_MODEL), _pack_small(small_grads))
    slab_parts = jnp.broadcast_to(slab[None], (N_DEV,) + slab.shape)
    received["w_in"], slabs = _collective(EXCHANGE, [_cut(grad_w_in, 1), slab_parts], "exchange_w_in_grads")

    out_g, out_d, out_m, out_v = {}, {}, {}, {}
    for n, _ in _BIG:
        rows = shards[n].shape[0]
        g, d, nm, nv = _sum_adamw(received[n], shards[n], given["m_" + n][0], given["v_" + n][0], "adamw_" + n,
                                  min(rows, 128))
        out_g[n], out_d[n], out_m[n], out_v[n] = g[None], d[None], nm[None], nv[None]
    g, d, nm, nv = _sum_adamw(slabs, _pack_small(small), _pack_small({n: given["m_" + n] for n in small_names}),
                              _pack_small({n: given["v_" + n] for n in small_names}), "adamw_small", _SLAB_ROWS)
    shapes = {n: given[n].shape for n in small_names}
    for dst, slab_out in ((out_g, g), (out_d, d), (out_m, nm), (out_v, nv)):
        dst.update(_unpack_small(slab_out, shapes))

    loss = g[-1, 0]
    order = ["w_in", "g_attn_pre", "g_q", "g_k", "g_out_a", "g_out_b", "w_out", "g_attn_post", "rel_bias", "g_mlp_pre",
             "w_ff1", "w_ff2", "g_mlp_post", "g_ple", "w_ple_gate", "w_ple_proj"]
    return (loss, dx[None], *[out_g[n] for n in order], *[out_d[n] for n in order], *[out_m[n] for n in order],
            *[out_v[n] for n in order])
```

```python
import functools
import math

import jax
import jax.numpy as jnp
import numpy as np
from jax import lax
from jax.experimental import pallas as pl
from jax.experimental.pallas import tpu as pltpu

F32 = jnp.float32
BF16 = jnp.bfloat16
SDS = jax.ShapeDtypeStruct

D_MODEL = 1024
HEAD_DIM = 64
N_HEADS_A = 8
N_KV_A = 2
GROUP_A = N_HEADS_A // N_KV_A
N_HEADS_B = 8
D_A = N_HEADS_A * HEAD_DIM
D_KV_A = N_KV_A * HEAD_DIM
D_B = N_HEADS_B * HEAD_DIM
D_IN = D_A + 2 * D_KV_A + 3 * D_B
D_FF = 4 * D_MODEL
D_PLE = 256
GRID_W = 64
ROPE_THETA = 10000.0
ROPE_HALF = HEAD_DIM // 2
DILATED_PATTERNS = ((128, 1), (512, 4), (2048, 16))
BAND_HALF = 64
N_BUCKETS = 32
MAX_DISTANCE = 1024
EPS = 1e-6
NEG_BIG = -1e30
SCORE_SCALE = HEAD_DIM ** -0.5

ADAM_LR = 0.001
ADAM_B1 = 0.9
ADAM_B2 = 0.999
ADAM_EPS = 1e-08
ADAM_WD = 0.01
ADAM_STEP = 10

N_DEV = 8
V7X_VMEM_LIMIT = 56 * 1024 * 1024

OFF_QA, OFF_KA, OFF_VA, OFF_QB, OFF_KB, OFF_VB = 0, 512, 640, 768, 1280, 1792


def _params(n_axes, vmem=None):
    return pltpu.CompilerParams(dimension_semantics=("arbitrary",) * n_axes, vmem_limit_bytes=vmem)


def _dot(a, b):
    return jnp.dot(a, b, preferred_element_type=F32)


def _dot_nt(a, b):
    return lax.dot_general(a, b, (((1,), (1,)), ((), ())), preferred_element_type=F32)


def _dot_tn(a, b):
    return lax.dot_general(a, b, (((0,), (0,)), ((), ())), preferred_element_type=F32)


def _rms_fwd(x, g):
    r = lax.rsqrt(jnp.mean(x * x, axis=-1, keepdims=True) + EPS)
    return x * r * g, r


def _rms_bwd(x, g, dy):
    r = lax.rsqrt(jnp.mean(x * x, axis=-1, keepdims=True) + EPS)
    xh = x * r
    dxh = dy * g
    dx = r * (dxh - xh * jnp.mean(dxh * xh, axis=-1, keepdims=True))
    return dx, jnp.sum(dy * xh, axis=0, keepdims=True)


def _head_sum(v):
    head = lax.broadcasted_iota(jnp.int32, v.shape, 1) >> 6
    out = jnp.zeros_like(v)
    for h in range(v.shape[1] // HEAD_DIM):
        msk = head == h
        s = jnp.sum(jnp.where(msk, v, 0.0), axis=-1, keepdims=True)
        out = jnp.where(msk, s, out)
    return out


def _swap_halves(v):
    w = v.shape[1]
    lane = lax.broadcasted_iota(jnp.int32, v.shape, 1)
    first_half = (lane & (HEAD_DIM - 1)) < ROPE_HALF
    return jnp.where(first_half, pltpu.roll(v, w - ROPE_HALF, 1), pltpu.roll(v, ROPE_HALF, 1))


def _head_norm_rope(v, g, cos, sin_signed):
    r = lax.rsqrt(_head_sum(v * v) * (1.0 / HEAD_DIM) + EPS)
    y = v * r * g
    return y * cos + _swap_halves(y) * sin_signed


def _head_norm_rope_bwd(v, g, cos, sin_signed, dout):
    dy = dout * cos - _swap_halves(dout) * sin_signed
    r = lax.rsqrt(_head_sum(v * v) * (1.0 / HEAD_DIM) + EPS)
    xh = v * r
    dxh = dy * g
    dv = r * (dxh - xh * (_head_sum(dxh * xh) * (1.0 / HEAD_DIM)))
    return dv, jnp.sum(dy * xh, axis=0, keepdims=True)


def _row_spec(tm, n):
    return pl.BlockSpec((tm, n), lambda i: (i, 0))


def _full_spec(shape):
    nd = len(shape)
    return pl.BlockSpec(shape, lambda *_: (0,) * nd)


def _const_spec(shape):
    nd = len(shape)
    return pl.BlockSpec(shape, lambda *_: (0,) * nd, pipeline_mode=pl.Buffered(1))


def _zero_at_first(first, *refs):
    @pl.when(first)
    def _():
        for ref in refs:
            ref[...] = jnp.zeros_like(ref)


DILATIONS = tuple(d for _, d in DILATED_PATTERNS if d > 1)


def _dilation_perm(tm, dil):
    rows = np.arange(tm)
    perm = np.zeros((tm, tm), np.float32)
    perm[(rows % dil) * (tm // dil) + rows // dil, rows] = 1.0
    return jnp.asarray(perm, BF16)


def _store_dilated(ref, perm_ref, val, dil):
    per = val.shape[0] // dil
    sorted_rows = _dot(perm_ref[...], val).astype(BF16)
    for r in range(dil):
        ref[:, r * D_B:(r + 1) * D_B] = sorted_rows[r * per:(r + 1) * per, :]


def _dilated_specs(s_len, tm, dtype):
    specs = [pl.BlockSpec((tm // d, d * D_B), lambda i: (i, 0)) for d in DILATIONS]
    return specs, [SDS((s_len // d, d * D_B), dtype) for d in DILATIONS]


ONES_ROWS = 16


def _inproj_fwd(x, g_pre, w_in, cos, sin, gq, gk, tm, tk):
    s_len = x.shape[0]
    n_d = len(DILATIONS)

    def body(x_ref, g_ref, w_ref, cos_ref, sin_ref, gq_ref, gk_ref, *rest):
        perms, views = rest[:n_d], rest[n_d + 10:]
        qa_raw, ka_raw, qs, k_maj, kt, vt, qb, kb, vb, xn_out = rest[n_d:n_d + 10]
        xn, _ = _rms_fwd(x_ref[...], g_ref[...])
        xn = xn.astype(BF16)
        xn_out[...] = xn
        qa = _dot(xn, w_ref[:, OFF_QA:OFF_KA])
        qa_raw[...] = qa
        cos, sin = cos_ref[...], sin_ref[...]
        qs[...] = (_head_norm_rope(qa, gq_ref[...], jnp.tile(cos, (1, D_A // D_KV_A)), jnp.tile(sin, (1, D_A // D_KV_A)))
                   * SCORE_SCALE).astype(BF16)
        ka = _dot(xn, w_ref[:, OFF_KA:OFF_VA])
        ka_raw[...] = ka
        kn = _head_norm_rope(ka, gk_ref[...], cos, sin).astype(BF16)
        for kv in range(N_KV_A):
            k_maj[kv] = kn[:, HEAD_DIM * kv:HEAD_DIM * (kv + 1)]
        kt[...] = kn.T.reshape(N_KV_A, HEAD_DIM, tm)
        va = _dot(xn, w_ref[:, OFF_VA:OFF_QB]).astype(BF16)
        vt[:, :HEAD_DIM, :] = va.T.reshape(N_KV_A, HEAD_DIM, tm)
        vt[:, HEAD_DIM:, :] = jnp.ones((N_KV_A, ONES_ROWS, tm), BF16)
        mixer_b = ((_dot(xn, w_ref[:, OFF_QB:OFF_KB]) * SCORE_SCALE).astype(BF16),
                   _dot(xn, w_ref[:, OFF_KB:OFF_VB]).astype(BF16), _dot(xn, w_ref[:, OFF_VB:D_IN]).astype(BF16))
        for a, (val, plain) in enumerate(zip(mixer_b, (qb, kb, vb))):
            plain[...] = val
            for j, d in enumerate(DILATIONS):
                _store_dilated(views[a * n_d + j], perms[j], val, d)

    view_specs, view_shapes = _dilated_specs(s_len, tm, BF16)
    per_chunk = tk // tm

    def chunk_t(rows):
        return pl.BlockSpec((N_KV_A, None, rows, tm), lambda i: (0, i // per_chunk, 0, i % per_chunk))

    return pl.pallas_call(
        body, name="inproj_fwd", grid=(s_len // tm,),
        in_specs=[_row_spec(tm, D_MODEL), _full_spec((1, D_MODEL)), _const_spec((D_MODEL, D_IN)),
                  _row_spec(tm, D_KV_A), _row_spec(tm, D_KV_A), _full_spec((1, D_A)), _full_spec((1, D_KV_A))]
                 + [_full_spec((tm, tm))] * n_d,
        out_specs=[_row_spec(tm, D_A), _row_spec(tm, D_KV_A), _row_spec(tm, D_A),
                   pl.BlockSpec((N_KV_A, tm, HEAD_DIM), lambda i: (0, i, 0)), chunk_t(HEAD_DIM),
                   chunk_t(HEAD_DIM + ONES_ROWS), _row_spec(tm, D_B), _row_spec(tm, D_B), _row_spec(tm, D_B),
                   _row_spec(tm, D_MODEL)] + view_specs * 3,
        out_shape=[SDS((s_len, D_A), F32), SDS((s_len, D_KV_A), F32), SDS((s_len, D_A), BF16),
                   SDS((N_KV_A, s_len, HEAD_DIM), BF16), SDS((N_KV_A, s_len // tk, HEAD_DIM, tk), BF16),
                   SDS((N_KV_A, s_len // tk, HEAD_DIM + ONES_ROWS, tk), BF16), SDS((s_len, D_B), BF16),
                   SDS((s_len, D_B), BF16), SDS((s_len, D_B), BF16), SDS((s_len, D_MODEL), BF16)] + view_shapes * 3,
        compiler_params=_params(1, V7X_VMEM_LIMIT),
    )(x, g_pre, w_in, cos, sin, gq, gk, *[_dilation_perm(tm, d) for d in DILATIONS])


def _stack_heads(ref, tq):
    return jnp.concatenate([ref[:, HEAD_DIM * g:HEAD_DIM * (g + 1)] for g in range(GROUP_A)], axis=0)


def _stack_cols(ref, tq):
    return jnp.concatenate([ref[:, HEAD_DIM * g:HEAD_DIM * g + 1] for g in range(GROUP_A)], axis=0)


def _attn_fwd(qs, k, vt, tq, tk):
    s_len = qs.shape[0]
    nk = s_len // tk
    assert nk % 2 == 0
    gw = GROUP_A * HEAD_DIM
    rows = GROUP_A * tq
    vrows = vt.shape[2]

    def body(q_ref, k_ref, vt_ref, o_ref, lse_ref, s_buf):
        qt = _stack_heads(q_ref, tq).T

        def scores(j, slot):
            kj = k_ref[pl.ds(pl.multiple_of(j * tk, tk), tk), :]
            s_buf[slot] = _dot(kj, qt)

        def consume(j, slot, carry):
            m, acc = carry
            st = s_buf[slot]
            m_new = jnp.maximum(m, jnp.max(st, axis=0, keepdims=True))
            pt = jnp.exp(st - m_new)
            acc = jnp.exp(m - m_new) * acc + _dot(vt_ref[j], pt.astype(BF16))
            return m_new, acc

        scores(0, 0)

        def pair(j, carry, more):
            scores(j + 1, 1)
            carry = consume(j, 0, carry)
            if more:
                scores(j + 2, 0)
            return consume(j + 1, 1, carry)

        carry = (jnp.full((1, rows), NEG_BIG, F32), jnp.zeros((vrows, rows), F32))
        carry = lax.fori_loop(0, nk // 2 - 1, lambda jj, c: pair(2 * jj, c, True), carry)
        m, acc = pair(nk - 2, carry, False)
        l = acc[HEAD_DIM:HEAD_DIM + 1]
        o = (acc[:HEAD_DIM] / l).T
        lse = jnp.broadcast_to(m + jnp.log(l), (HEAD_DIM, rows)).T
        for g in range(GROUP_A):
            o_ref[:, HEAD_DIM * g:HEAD_DIM * (g + 1)] = o[g * tq:(g + 1) * tq]
            lse_ref[:, HEAD_DIM * g:HEAD_DIM * (g + 1)] = lse[g * tq:(g + 1) * tq]

    tile = pl.BlockSpec((tq, gw), lambda kv, i: (i, kv))
    return pl.pallas_call(
        body, name="attn_fwd", grid=(N_KV_A, s_len // tq),
        in_specs=[tile, pl.BlockSpec((None, s_len, HEAD_DIM), lambda kv, i: (kv, 0, 0)),
                  pl.BlockSpec((None, nk, vrows, tk), lambda kv, i: (kv, 0, 0, 0))],
        out_specs=[tile, tile],
        out_shape=(SDS((s_len, D_A), F32), SDS((s_len, D_A), F32)),
        scratch_shapes=[pltpu.VMEM((2, tk, rows), F32)],
        compiler_params=_params(2, V7X_VMEM_LIMIT),
    )(qs, k, vt)


def _attn_bwd(qs, do, lse, delta, kt, k, vt, tq, tk):
    s_len = qs.shape[0]
    nk = s_len // tk
    assert nk % 2 == 0
    nq = s_len // tq
    gw = GROUP_A * HEAD_DIM
    rows = GROUP_A * tq

    def body(q_ref, do_ref, lse_ref, delta_ref, kt_ref, k_ref, vt_ref, dq_ref, dk_hbm, dv_hbm,
             dk_acc, dv_acc, s_buf, dp_buf):
        kv = pl.program_id(0)
        i = pl.program_id(1)

        @pl.when(i == 0)
        def _():
            dk_acc[...] = jnp.zeros_like(dk_acc)
            dv_acc[...] = jnp.zeros_like(dv_acc)

        q = _stack_heads(q_ref, tq)
        dout = _stack_heads(do_ref, tq)
        qt = q.T
        doutt = dout.T
        row_lse = _stack_cols(lse_ref, tq)
        row_delta = _stack_cols(delta_ref, tq)

        def scores(j, slot):
            s_buf[slot] = _dot(q, kt_ref[j])
            dp_buf[slot] = _dot(dout, vt_ref[j, :HEAD_DIM, :])

        def consume(j, slot, dq):
            p = jnp.exp(s_buf[slot] - row_lse)
            ds = (p * (dp_buf[slot] - row_delta)).astype(BF16)
            dv_acc[j] += _dot(doutt, p.astype(BF16))
            dk_acc[j] += _dot(qt, ds)
            return dq + _dot(ds, k_ref[pl.ds(pl.multiple_of(j * tk, tk), tk), :])

        scores(0, 0)

        def pair(j, dq, more):
            scores(j + 1, 1)
            dq = consume(j, 0, dq)
            if more:
                scores(j + 2, 0)
            return consume(j + 1, 1, dq)

        dq = lax.fori_loop(0, nk // 2 - 1, lambda jj, c: pair(2 * jj, c, True), jnp.zeros((rows, HEAD_DIM), F32))
        dq = pair(nk - 2, dq, False) * SCORE_SCALE
        for g in range(GROUP_A):
            dq_ref[:, HEAD_DIM * g:HEAD_DIM * (g + 1)] = dq[g * tq:(g + 1) * tq]

        @pl.when(i == nq - 1)
        def _():
            pltpu.sync_copy(dk_acc, dk_hbm.at[kv])
            pltpu.sync_copy(dv_acc, dv_hbm.at[kv])

    tile = pl.BlockSpec((tq, gw), lambda kv, i: (i, kv))
    chunks = pl.BlockSpec((None, nk, HEAD_DIM, tk), lambda kv, i: (kv, 0, 0, 0))
    grad_t = SDS((N_KV_A, nk, HEAD_DIM, tk), F32)
    return pl.pallas_call(
        body, name="attn_bwd", grid=(N_KV_A, nq),
        in_specs=[tile, tile, tile, tile, chunks,
                  pl.BlockSpec((None, s_len, HEAD_DIM), lambda kv, i: (kv, 0, 0)),
                  pl.BlockSpec((None, nk, vt.shape[2], tk), lambda kv, i: (kv, 0, 0, 0))],
        out_specs=[tile, pl.BlockSpec(memory_space=pl.ANY), pl.BlockSpec(memory_space=pl.ANY)],
        out_shape=(SDS((s_len, D_A), F32), grad_t, grad_t),
        scratch_shapes=[pltpu.VMEM((nk, HEAD_DIM, tk), F32), pltpu.VMEM((nk, HEAD_DIM, tk), F32),
                        pltpu.VMEM((2, rows, tk), F32), pltpu.VMEM((2, rows, tk), F32)],
        compiler_params=_params(2, V7X_VMEM_LIMIT),
    )(qs, do, lse, delta, kt, k, vt)


STAT_W = 128


def _band_specs(length, t, width=D_B):
    hb = t // BAND_HALF
    last = length // BAND_HALF - 1
    main = pl.BlockSpec((t, width), lambda r, i: (i, r))
    prev = pl.BlockSpec((BAND_HALF, width), lambda r, i: (jnp.maximum(i * hb - 1, 0), r))
    nxt = pl.BlockSpec((BAND_HALF, width), lambda r, i: (jnp.minimum((i + 1) * hb, last), r))
    return main, [prev, main, nxt]


def _pack_heads(cols):
    lane = lax.broadcasted_iota(jnp.int32, (cols[0].shape[0], STAT_W), 1)
    out = jnp.zeros((cols[0].shape[0], STAT_W), F32)
    for h, c in enumerate(cols):
        out = jnp.where(lane == h, c, out)
    return out


def _spread_heads(stat):
    head = lax.broadcasted_iota(jnp.int32, (stat.shape[0], D_B), 1) >> 6
    out = jnp.zeros((stat.shape[0], D_B), F32)
    for h in range(N_HEADS_B):
        out = jnp.where(head == h, stat[:, h:h + 1], out)
    return out


def _window(refs):
    return jnp.concatenate([r[...] for r in refs], axis=0)


def _head(v, h):
    return v[:, HEAD_DIM * h:HEAD_DIM * (h + 1)]


def _call_with_rider(body, name, grid, in_specs, out_specs, out_shape, scratch_shapes, args, rider):
    n_in, n_out = len(in_specs), len(out_specs)
    if rider is None:
        res = pl.pallas_call(body, name=name, grid=grid, in_specs=in_specs, out_specs=out_specs, out_shape=out_shape,
                             scratch_shapes=scratch_shapes, compiler_params=_params(len(grid), V7X_VMEM_LIMIT))(*args)
        return res, []
    kind, arrays = rider
    n = len(arrays)

    def with_rider(*refs):
        ins, c_ins = refs[:n_in], refs[n_in:n_in + n]
        outs, c_outs = refs[n_in + n:n_in + n + n_out], refs[n_in + n + n_out:n_in + 2 * n + n_out]
        rest = refs[n_in + 2 * n + n_out:]
        scratch, sems = rest[:-3], rest[-3:]
        ids = [pl.program_id(a) for a in range(len(grid))]
        first = functools.reduce(jnp.logical_and, [i == 0 for i in ids])
        last = functools.reduce(jnp.logical_and, [i == g - 1 for i, g in zip(ids, grid)])

        @pl.when(first)
        def _():
            _comm_start(kind, c_ins, c_outs, sems)

        body(*ins, *outs, *scratch)

        @pl.when(last)
        def _():
            _comm_wait(kind, c_ins, c_outs, sems)

    res = pl.pallas_call(
        with_rider, name=name, grid=grid, in_specs=list(in_specs) + [ANY_SPEC] * n,
        out_specs=list(out_specs) + [ANY_SPEC] * n, out_shape=list(out_shape) + _comm_out_shapes(kind, arrays),
        scratch_shapes=list(scratch_shapes) + _comm_sems(n), compiler_params=_params(len(grid), V7X_VMEM_LIMIT),
    )(*args, *arrays)
    return res[:n_out], res[n_out:]


def _band_fwd(q, k, v, bias, dil, t, rider=None):
    s_len = q.size // D_B
    length = s_len // dil
    w = t + 2 * BAND_HALF
    view = lambda a: a.reshape(length, dil * D_B)
    main, win = _band_specs(length, t)
    stat, _ = _band_specs(length, t, STAT_W)

    def body(q_ref, k0, k1, k2, v0, v1, v2, bias_ref, o_ref, lse_ref, s_buf, kt_buf, o_buf):
        lses = []
        i = pl.program_id(1)
        kt_buf[...] = _window((k0, k1, k2)).T
        vw = _window((v0, v1, v2))
        pos = i * t - BAND_HALF + lax.broadcasted_iota(jnp.int32, (1, w), 1)
        valid = (pos >= 0) & (pos < length)
        for h in range(N_HEADS_B):
            s_buf[h] = _dot(q_ref[:, HEAD_DIM * h:HEAD_DIM * (h + 1)], kt_buf[HEAD_DIM * h:HEAD_DIM * (h + 1), :])
        for h in range(N_HEADS_B):
            s = jnp.where(valid, s_buf[h] + bias_ref[h], NEG_BIG)
            m = jnp.max(s, axis=-1, keepdims=True)
            e = jnp.exp(s - m)
            den = jnp.sum(e, axis=-1, keepdims=True)
            o_buf[:, HEAD_DIM * h:HEAD_DIM * (h + 1)] = _dot(e.astype(BF16), _head(vw, h)) / den
            lses.append(m + jnp.log(den))
        o_ref[...] = o_buf[...].astype(BF16)
        lse_ref[...] = _pack_heads(lses)

    (o, lse), rode = _call_with_rider(
        body, f"band_fwd_d{dil}", (dil, length // t),
        [main] + win + win + [_full_spec((N_HEADS_B, t, w))], [main, stat],
        [SDS((length, dil * D_B), BF16), SDS((length, dil * STAT_W), F32)],
        [pltpu.VMEM((N_HEADS_B, t, w), F32), pltpu.VMEM((D_B, w), BF16), pltpu.VMEM((t, D_B), F32)],
        (view(q), view(k), view(k), view(k), view(v), view(v), view(v), bias), rider)
    return o, lse.reshape(s_len, STAT_W), rode


def _band_dq(q, do, lse, delta, k, v, bias, dil, t, rider=None):
    s_len = q.size // D_B
    length = s_len // dil
    w = t + 2 * BAND_HALF
    view = lambda a: a.reshape(length, dil * D_B)
    main, win = _band_specs(length, t)

    def body(q_ref, do_ref, lse_ref, delta_ref, k0, k1, k2, v0, v1, v2, bias_ref, dq_ref, dbias_ref,
             s_buf, dp_buf, kt_buf, vt_buf, dq_buf):
        i = pl.program_id(1)
        @pl.when((pl.program_id(0) == 0) & (i == 0))
        def _():
            dbias_ref[...] = jnp.zeros_like(dbias_ref)

        kw = _window((k0, k1, k2))
        kt_buf[...] = kw.T
        vt_buf[...] = _window((v0, v1, v2)).T
        pos = i * t - BAND_HALF + lax.broadcasted_iota(jnp.int32, (1, w), 1)
        valid = (pos >= 0) & (pos < length)
        for h in range(N_HEADS_B):
            cols = slice(HEAD_DIM * h, HEAD_DIM * (h + 1))
            s_buf[h] = _dot(q_ref[:, cols], kt_buf[cols, :])
            dp_buf[h] = _dot(do_ref[:, cols], vt_buf[cols, :])
        for h in range(N_HEADS_B):
            cols = slice(HEAD_DIM * h, HEAD_DIM * (h + 1))
            s = jnp.where(valid, s_buf[h] + bias_ref[h], NEG_BIG)
            p = jnp.exp(s - lse_ref[:, h:h + 1])
            ds = p * (dp_buf[h] - delta_ref[:, h:h + 1])
            dq_buf[:, cols] = _dot(ds.astype(BF16), _head(kw, h)) * SCORE_SCALE
            dbias_ref[h] += ds
        dq_ref[...] = dq_buf[...].astype(BF16)

    stat, _ = _band_specs(length, t, STAT_W)
    sview = lambda a: a.reshape(length, dil * STAT_W)
    (dq, dbias), rode = _call_with_rider(
        body, f"band_dq_d{dil}", (dil, length // t),
        [main, main, stat, stat] + win + win + [_full_spec((N_HEADS_B, t, w))],
        [main, _full_spec((N_HEADS_B, t, w))],
        [SDS((length, dil * D_B), BF16), SDS((N_HEADS_B, t, w), F32)],
        [pltpu.VMEM((N_HEADS_B, t, w), F32), pltpu.VMEM((N_HEADS_B, t, w), F32),
         pltpu.VMEM((D_B, w), BF16), pltpu.VMEM((D_B, w), BF16), pltpu.VMEM((t, D_B), F32)],
        (view(q), view(do), sview(lse), sview(delta), view(k), view(k), view(k), view(v), view(v), view(v), bias),
        rider)
    return dq, dbias, rode


def _band_dkv(k, v, q, do, lse, delta, bias_t, dil, t):
    s_len = q.size // D_B
    length = s_len // dil
    w = t + 2 * BAND_HALF
    view = lambda a: a.reshape(length, dil * D_B)
    main, win = _band_specs(length, t)

    def body(k_ref, v_ref, q0, q1, q2, d0, d1, d2, l0, l1, l2, e0, e1, e2, bias_ref, dk_ref, dv_ref,
             s_buf, dp_buf, kt_buf, vt_buf):
        i = pl.program_id(1)
        qw = _window((q0, q1, q2))
        dow = _window((d0, d1, d2))
        lsew = _window((l0, l1, l2))
        deltaw = _window((e0, e1, e2))
        pos = i * t - BAND_HALF + lax.broadcasted_iota(jnp.int32, (w, 1), 0)
        valid = (pos >= 0) & (pos < length)
        kt_buf[...] = k_ref[...].T
        vt_buf[...] = v_ref[...].T
        for h in range(N_HEADS_B):
            cols = slice(HEAD_DIM * h, HEAD_DIM * (h + 1))
            s_buf[h] = _dot(_head(qw, h), kt_buf[cols, :])
            dp_buf[h] = _dot(_head(dow, h), vt_buf[cols, :])
        qwt = qw.T
        dowt = dow.T
        dkt, dvt = [], []
        for h in range(N_HEADS_B):
            rows = slice(HEAD_DIM * h, HEAD_DIM * (h + 1))
            s = jnp.where(valid, s_buf[h] + bias_ref[h], NEG_BIG)
            p = jnp.exp(s - lsew[:, h:h + 1])
            ds = p * (dp_buf[h] - deltaw[:, h:h + 1])
            dvt.append(_dot(dowt[rows, :], p.astype(BF16)))
            dkt.append(_dot(qwt[rows, :], ds.astype(BF16)))
        dv_ref[...] = jnp.concatenate(dvt, axis=0).T.astype(BF16)
        dk_ref[...] = jnp.concatenate(dkt, axis=0).T.astype(BF16)

    _, swin = _band_specs(length, t, STAT_W)
    sview = lambda a: a.reshape(length, dil * STAT_W)
    dk, dv = pl.pallas_call(
        body, name=f"band_dkv_d{dil}", grid=(dil, length // t),
        in_specs=[main, main] + win + win + swin + swin + [_full_spec((N_HEADS_B, w, t))],
        out_specs=[main, main],
        out_shape=(SDS((length, dil * D_B), BF16), SDS((length, dil * D_B), BF16)),
        scratch_shapes=[pltpu.VMEM((N_HEADS_B, w, t), F32), pltpu.VMEM((N_HEADS_B, w, t), F32),
                        pltpu.VMEM((D_B, t), BF16), pltpu.VMEM((D_B, t), BF16)],
        compiler_params=_params(2, V7X_VMEM_LIMIT),
    )(view(k), view(v), view(q), view(q), view(q), view(do), view(do), view(do), sview(lse), sview(lse), sview(lse),
      sview(delta), sview(delta), sview(delta), bias_t)
    return dk, dv


def _t5_bucket_np(rel):
    nb = N_BUCKETS // 2
    max_exact = nb // 2
    side = np.where(rel > 0, nb, 0)
    n = np.abs(rel)
    ratio = np.maximum(n, max_exact).astype(np.float32) / np.float32(max_exact)
    large = max_exact + (np.log(ratio) / np.float32(math.log(MAX_DISTANCE / max_exact))
                         * np.float32(nb - max_exact)).astype(np.int32)
    large = np.minimum(large, nb - 1)
    return (side + np.where(n < max_exact, n, large)).astype(np.int32)


def _band_buckets(dil, t):
    rel = np.arange(t + 2 * BAND_HALF)[None, :] - BAND_HALF - np.arange(t)[:, None]
    bucket = _t5_bucket_np(np.clip(rel, -BAND_HALF, BAND_HALF) * dil)
    return np.where(np.abs(rel) <= BAND_HALF, bucket, -1).astype(np.int32)


def _toeplitz(vals, rows, cols):
    heads = vals.shape[0]
    period = -(-(rows + cols) // 128) * 128 + 1
    gap = period - (rows + cols - 1)
    vec = jnp.concatenate([vals[:, rows - 1:], jnp.zeros((heads, gap), vals.dtype), vals[:, :rows - 1]], axis=1)
    flat = jnp.broadcast_to(vec[:, None, :], (heads, rows, period)).reshape(heads, rows * period)
    return flat[:, :rows * (period - 1)].reshape(heads, rows, period - 1)[:, :, :cols]


def _bias_tiles(rel_bias, dil, t):
    w = t + 2 * BAND_HALF
    rel = np.arange(-BAND_HALF, BAND_HALF + 1)
    bucket = _t5_bucket_np(rel * dil)
    runs, start = [], 0
    for i in range(1, len(bucket) + 1):
        if i == len(bucket) or bucket[i] != bucket[start]:
            b = int(bucket[start])
            runs.append(jnp.broadcast_to(rel_bias[b:b + 1], (i - start, N_HEADS_B)))
            start = i
    per_rel = jnp.concatenate(runs, axis=0).T

    def diagonals(lo, hi):
        left = jnp.full((N_HEADS_B, max(0, -BAND_HALF - lo)), NEG_BIG, F32)
        right = jnp.full((N_HEADS_B, max(0, hi - BAND_HALF)), NEG_BIG, F32)
        return jnp.concatenate([left, per_rel, right], axis=1)

    tile = _toeplitz(diagonals(-(t - 1) - BAND_HALF, w - 1 - BAND_HALF), t, w)
    twin = _toeplitz(diagonals(-(w - 1) + BAND_HALF, t - 1 + BAND_HALF), w, t)
    return tile, twin


def _dbias_reduce(dbias, buckets):
    n = len(dbias)

    def body(*refs):
        db_refs, bk_refs, o_ref = refs[:n], refs[n:2 * n], refs[2 * n]
        row = lax.broadcasted_iota(jnp.int32, (N_BUCKETS, 128), 0)
        lane = lax.broadcasted_iota(jnp.int32, (N_BUCKETS, 128), 1)

        def per_bucket(b, out):
            for pat in range(n):
                msk = bk_refs[pat][...] == b
                for h in range(N_HEADS_B):
                    tot = jnp.sum(jnp.where(msk, db_refs[pat][h], 0.0), axis=-1, keepdims=True)
                    tot = jnp.sum(tot, axis=0, keepdims=True)
                    out = out + jnp.where((row == b) & (lane == h), tot, 0.0)
            return out

        o_ref[...] = lax.fori_loop(0, N_BUCKETS, per_bucket, jnp.zeros((N_BUCKETS, 128), F32))

    return pl.pallas_call(
        body, name="dbias_reduce", out_shape=SDS((N_BUCKETS, 128), F32),
        compiler_params=pltpu.CompilerParams(vmem_limit_bytes=V7X_VMEM_LIMIT),
    )(*dbias, *buckets)


def _attn_out_fwd(ya, ob, lb, x, g_a, g_b, w_out, g_post, tm):
    s_len = ya.shape[0]
    assert len(ob) == 1 + len(DILATIONS) == 3

    def body(ya_ref, o0, o1, o2, l0, l1, l2, x_ref, ga_ref, gb_ref, w_ref, gp_ref, unsort1, unsort2,
             h1_ref, yo_ref, yb_ref, lse_ref, ycat_ref):
        def plain_rows(o_ref, unsort_ref, dil):
            sorted_rows = jnp.concatenate([o_ref[:, r * D_B:(r + 1) * D_B] for r in range(dil)], axis=0)
            return _dot(unsort_ref[...], sorted_rows)

        m = jnp.maximum(jnp.maximum(l0[...], l1[...]), l2[...])
        w0, w1, w2 = jnp.exp(l0[...] - m), jnp.exp(l1[...] - m), jnp.exp(l2[...] - m)
        wsum = w0 + w1 + w2
        yb = (_spread_heads(w0 / wsum) * o0[...].astype(F32)
              + _spread_heads(w1 / wsum) * plain_rows(o1, unsort1, DILATIONS[0])
              + _spread_heads(w2 / wsum) * plain_rows(o2, unsort2, DILATIONS[1]))
        yb_ref[...] = yb
        lse_ref[...] = m + jnp.log(wsum)
        yan, _ = _rms_fwd(ya_ref[...], ga_ref[...])
        ybn, _ = _rms_fwd(yb, gb_ref[...])
        yan, ybn = yan.astype(BF16), ybn.astype(BF16)
        ycat_ref[:, :D_A] = yan
        ycat_ref[:, D_A:] = ybn
        yo = _dot(yan, w_ref[:D_A, :]) + _dot(ybn, w_ref[D_A:, :])
        yo_ref[...] = yo
        post, _ = _rms_fwd(yo, gp_ref[...])
        h1_ref[...] = x_ref[...] + post

    half, full, stat = _row_spec(tm, D_A), _row_spec(tm, D_MODEL), _row_spec(tm, STAT_W)
    view_specs, _ = _dilated_specs(s_len, tm, BF16)
    return pl.pallas_call(
        body, name="attn_out_fwd", grid=(s_len // tm,),
        in_specs=[half] * 2 + view_specs + [stat] * 3
                 + [full, _full_spec((1, D_A)), _full_spec((1, D_B)), _full_spec((D_MODEL, D_MODEL)),
                    _full_spec((1, D_MODEL))] + [_full_spec((tm, tm))] * len(DILATIONS),
        out_specs=[full, full, half, stat, full],
        out_shape=(SDS((s_len, D_MODEL), F32), SDS((s_len, D_MODEL), F32), SDS((s_len, D_B), F32),
                   SDS((s_len, STAT_W), F32), SDS((s_len, D_MODEL), BF16)),
        compiler_params=_params(1, V7X_VMEM_LIMIT),
    )(ya, ob[0], ob[1], ob[2], lb[0], lb[1], lb[2], x, g_a, g_b, w_out, g_post,
      *[_dilation_perm(tm, d).T for d in DILATIONS])


def _attn_out_bwd(dh1, yo, ya, yb, g_a, g_b, w_out, g_post, tm):
    s_len = ya.shape[0]
    n_d = len(DILATIONS)

    def body(dh1_ref, yo_ref, ya_ref, yb_ref, ga_ref, gb_ref, w_ref, gp_ref, *rest):
        perms, views = rest[:n_d], rest[n_d + 8:]
        dyo_ref, dya_ref, dyb_ref, dela_ref, delb_ref, dgp_ref, dga_ref, dgb_ref = rest[n_d:n_d + 8]
        _zero_at_first(pl.program_id(0) == 0, dgp_ref, dga_ref, dgb_ref)
        dyo, dgp = _rms_bwd(yo_ref[...], gp_ref[...], dh1_ref[...])
        dyo = dyo.astype(BF16)
        dyo_ref[...] = dyo
        dgp_ref[...] += dgp
        dya_n = _dot_nt(dyo, w_ref[:D_A, :])
        dyb_n = _dot_nt(dyo, w_ref[D_A:, :])
        ya, yb = ya_ref[...], yb_ref[...]
        dya, dga = _rms_bwd(ya, ga_ref[...], dya_n)
        dyb, dgb = _rms_bwd(yb, gb_ref[...], dyb_n)
        dga_ref[...] += dga
        dgb_ref[...] += dgb
        dya_ref[...] = dya.astype(BF16)
        dyb_ref[...] = dyb.astype(BF16)
        for j, d in enumerate(DILATIONS):
            _store_dilated(views[j], perms[j], dyb.astype(BF16), d)
        dela_ref[...] = _head_sum(dya * ya)
        prod = dyb * yb
        head = lax.broadcasted_iota(jnp.int32, prod.shape, 1) >> 6
        delb_ref[...] = _pack_heads([jnp.sum(jnp.where(head == h, prod, 0.0), axis=-1, keepdims=True)
                                     for h in range(N_HEADS_B)])

    half, full = _row_spec(tm, D_A), _row_spec(tm, D_MODEL)
    view_specs, view_shapes = _dilated_specs(s_len, tm, BF16)
    return pl.pallas_call(
        body, name="attn_out_bwd", grid=(s_len // tm,),
        in_specs=[full, full, half, half, _full_spec((1, D_A)), _full_spec((1, D_B)),
                  _full_spec((D_MODEL, D_MODEL)), _full_spec((1, D_MODEL))] + [_full_spec((tm, tm))] * n_d,
        out_specs=[full, half, half, half, _row_spec(tm, STAT_W), _full_spec((1, D_MODEL)), _full_spec((1, D_A)),
                   _full_spec((1, D_B))] + view_specs,
        out_shape=[SDS((s_len, D_MODEL), BF16), SDS((s_len, D_A), BF16), SDS((s_len, D_B), BF16),
                   SDS((s_len, D_A), F32), SDS((s_len, STAT_W), F32), SDS((1, D_MODEL), F32), SDS((1, D_A), F32),
                   SDS((1, D_B), F32)] + view_shapes,
        compiler_params=_params(1, V7X_VMEM_LIMIT),
    )(dh1, yo, ya, yb, g_a, g_b, w_out, g_post, *[_dilation_perm(tm, d) for d in DILATIONS])


FF_CHUNK = 1024


def _mlp_fwd(h1, g_pre, w1, w2, g_post, tm):
    s_len = h1.shape[0]

    def body(h1_ref, gpre_ref, w1_ref, w2_ref, gpost_ref, h2_ref, fo_ref, xn_ref):
        h1v = h1_ref[...]
        xn, _ = _rms_fwd(h1v, gpre_ref[...])
        xn = xn.astype(BF16)
        xn_ref[...] = xn
        fo = jnp.zeros((tm, D_MODEL), F32)
        for c in range(D_FF // FF_CHUNK):
            cols = slice(c * FF_CHUNK, (c + 1) * FF_CHUNK)
            u = jnp.maximum(_dot(xn, w1_ref[:, cols]), 0.0)
            fo = fo + _dot((u * u).astype(BF16), w2_ref[cols, :])
        fo_ref[...] = fo
        post, _ = _rms_fwd(fo, gpost_ref[...])
        h2_ref[...] = h1v + post

    full = _row_spec(tm, D_MODEL)
    return pl.pallas_call(
        body, name="mlp_fwd", grid=(s_len // tm,),
        in_specs=[full, _full_spec((1, D_MODEL)), _const_spec((D_MODEL, D_FF)), _const_spec((D_FF, D_MODEL)),
                  _full_spec((1, D_MODEL))],
        out_specs=[full, full, full],
        out_shape=(SDS((s_len, D_MODEL), F32), SDS((s_len, D_MODEL), F32), SDS((s_len, D_MODEL), BF16)),
        compiler_params=_params(1, V7X_VMEM_LIMIT),
    )(h1, g_pre, w1, w2, g_post)


def _mlp_bwd(h1, dh2, fo, xn, g_pre, w1, w2, g_post, tm):
    s_len = h1.shape[0]

    def body(h1_ref, dh2_ref, fo_ref, xn_ref, gpre_ref, w1_ref, w2_ref, gpost_ref,
             dh1_ref, dfo_ref, du_ref, f_ref, dgpost_ref, dgpre_ref):
        _zero_at_first(pl.program_id(0) == 0, dgpost_ref, dgpre_ref)
        dh2 = dh2_ref[...]
        dfo, dgpost = _rms_bwd(fo_ref[...], gpost_ref[...], dh2)
        dfo = dfo.astype(BF16)
        dfo_ref[...] = dfo
        dgpost_ref[...] += dgpost
        xn = xn_ref[...]
        dxn = jnp.zeros((tm, D_MODEL), F32)
        for c in range(D_FF // FF_CHUNK):
            cols = slice(c * FF_CHUNK, (c + 1) * FF_CHUNK)
            u = jnp.maximum(_dot(xn, w1_ref[:, cols]), 0.0)
            f_ref[:, cols] = (u * u).astype(BF16)
            du = (_dot_nt(dfo, w2_ref[cols, :]) * (2.0 * u)).astype(BF16)
            du_ref[:, cols] = du
            dxn = dxn + _dot_nt(du, w1_ref[:, cols])
        dx, dgpre = _rms_bwd(h1_ref[...], gpre_ref[...], dxn)
        dgpre_ref[...] += dgpre
        dh1_ref[...] = dh2 + dx

    full, wide = _row_spec(tm, D_MODEL), _row_spec(tm, D_FF)
    return pl.pallas_call(
        body, name="mlp_bwd", grid=(s_len // tm,),
        in_specs=[full, full, full, full, _full_spec((1, D_MODEL)), _const_spec((D_MODEL, D_FF)),
                  _const_spec((D_FF, D_MODEL)), _full_spec((1, D_MODEL))],
        out_specs=[full, full, wide, wide, _full_spec((1, D_MODEL)), _full_spec((1, D_MODEL))],
        out_shape=(SDS((s_len, D_MODEL), F32), SDS((s_len, D_MODEL), BF16), SDS((s_len, D_FF), BF16),
                   SDS((s_len, D_FF), BF16), SDS((1, D_MODEL), F32), SDS((1, D_MODEL), F32)),
        compiler_params=_params(1, V7X_VMEM_LIMIT),
    )(h1, dh2, fo, xn, g_pre, w1, w2, g_post)


def _ple_fwd_bwd(h2, p, target, g_ple, w_gate, w_proj, tm):
    s_len = h2.shape[0]

    def body(h2_ref, p_ref, t_ref, g_ref, wg_ref, wp_ref, dh2_ref, loss_ref, dg_ref, xn_ref, dgl_ref, dpp_ref, pb_ref):
        _zero_at_first(pl.program_id(0) == 0, loss_ref, dg_ref)
        h2 = h2_ref[...]
        g = g_ref[...]
        xn, _ = _rms_fwd(h2, g)
        xn = xn.astype(BF16)
        xn_ref[...] = xn
        gate = 1.0 / (1.0 + jnp.exp(-_dot(xn, wg_ref[...])))
        pb = p_ref[...].astype(BF16)
        pb_ref[...] = pb
        pp = _dot(pb, wp_ref[...])
        diff = h2 + gate * pp - t_ref[...]
        loss_ref[...] += jnp.full((8, 128), jnp.sum(diff * diff), F32)
        dh3 = diff * (1.0 / D_MODEL)
        dpp_ref[...] = (dh3 * gate).astype(BF16)
        dgl = (dh3 * pp * gate * (1.0 - gate)).astype(BF16)
        dgl_ref[...] = dgl
        dx, dg = _rms_bwd(h2, g, _dot_nt(dgl, wg_ref[...]))
        dg_ref[...] += dg
        dh2_ref[...] = dh3 + dx

    full = _row_spec(tm, D_MODEL)
    return pl.pallas_call(
        body, name="ple_fwd_bwd", grid=(s_len // tm,),
        in_specs=[full, _row_spec(tm, D_PLE), full, _full_spec((1, D_MODEL)), _full_spec((D_MODEL, D_MODEL)),
                  _full_spec((D_PLE, D_MODEL))],
        out_specs=[full, _full_spec((8, 128)), _full_spec((1, D_MODEL)), full, full, full, _row_spec(tm, D_PLE)],
        out_shape=(SDS((s_len, D_MODEL), F32), SDS((8, 128), F32), SDS((1, D_MODEL), F32), SDS((s_len, D_MODEL), BF16),
                   SDS((s_len, D_MODEL), BF16), SDS((s_len, D_MODEL), BF16), SDS((s_len, D_PLE), BF16)),
        compiler_params=_params(1, V7X_VMEM_LIMIT),
    )(h2, p, target, g_ple, w_gate, w_proj)


def _inproj_bwd(dqs, dk_t, dv_t, parts_b, qa_raw, ka_raw, x, dh1, g_pre, w_in, cos, sin, gq, gk, tm):
    s_len = x.shape[0]
    tk = dk_t.shape[3]

    n_d = len(DILATIONS)
    n_parts = 3 * (1 + n_d)

    def body(dqs_ref, dkt_ref, dvt_ref, *rest):
        parts, rest = rest[:n_parts], rest[n_parts:]
        (qa_ref, ka_ref, x_ref, dh1_ref, g_ref, w_ref, cos_ref, sin_ref, gq_ref, gk_ref), rest = rest[:10], rest[10:]
        unsort, (dx_ref, dproj_ref, dg_ref, dgq_ref, dgk_ref) = rest[:n_d], rest[n_d:]

        def mixer_b_grad(a):
            own = parts[a * (1 + n_d):(a + 1) * (1 + n_d)]
            total = own[0][...].astype(F32)
            for j, d in enumerate(DILATIONS):
                sorted_rows = jnp.concatenate([own[1 + j][:, r * D_B:(r + 1) * D_B] for r in range(d)], axis=0)
                total = total + _dot(unsort[j][...], sorted_rows)
            return total.astype(BF16)

        _zero_at_first(pl.program_id(0) == 0, dg_ref, dgq_ref, dgk_ref)
        cos, sin = cos_ref[...], sin_ref[...]
        dkn = dkt_ref[...].reshape(D_KV_A, tm).T
        dva = dvt_ref[...].reshape(D_KV_A, tm).T
        dqa, dgq = _head_norm_rope_bwd(qa_ref[...], gq_ref[...], jnp.tile(cos, (1, D_A // D_KV_A)),
                                       jnp.tile(sin, (1, D_A // D_KV_A)), dqs_ref[...])
        dka, dgk = _head_norm_rope_bwd(ka_ref[...], gk_ref[...], cos, sin, dkn)
        dgq_ref[...] += dgq
        dgk_ref[...] += dgk
        dproj_ref[:, OFF_QA:OFF_KA] = dqa.astype(BF16)
        dproj_ref[:, OFF_KA:OFF_VA] = dka.astype(BF16)
        dproj_ref[:, OFF_VA:OFF_QB] = dva.astype(BF16)
        dproj_ref[:, OFF_QB:OFF_KB] = mixer_b_grad(0)
        dproj_ref[:, OFF_KB:OFF_VB] = mixer_b_grad(1)
        dproj_ref[:, OFF_VB:D_IN] = mixer_b_grad(2)
        dxn = _dot_nt(dproj_ref[...], w_ref[...])
        dx, dg = _rms_bwd(x_ref[...], g_ref[...], dxn)
        dg_ref[...] += dg
        dx_ref[...] = dh1_ref[...] + dx

    half, kvw, full = _row_spec(tm, D_A), _row_spec(tm, D_KV_A), _row_spec(tm, D_MODEL)
    per_chunk = tk // tm
    chunk_t = pl.BlockSpec((N_KV_A, None, HEAD_DIM, tm), lambda i: (0, i // per_chunk, 0, i % per_chunk))
    view_specs, _ = _dilated_specs(s_len, tm, BF16)
    return pl.pallas_call(
        body, name="inproj_bwd", grid=(s_len // tm,),
        in_specs=[half, chunk_t, chunk_t] + ([half] + view_specs) * 3
                 + [half, kvw, full, full, _full_spec((1, D_MODEL)), _const_spec((D_MODEL, D_IN)), kvw, kvw,
                    _full_spec((1, D_A)), _full_spec((1, D_KV_A))] + [_full_spec((tm, tm))] * n_d,
        out_specs=[full, _row_spec(tm, D_IN), _full_spec((1, D_MODEL)), _full_spec((1, D_A)), _full_spec((1, D_KV_A))],
        out_shape=(SDS((s_len, D_MODEL), F32), SDS((s_len, D_IN), BF16), SDS((1, D_MODEL), F32), SDS((1, D_A), F32),
                   SDS((1, D_KV_A), F32)),
        compiler_params=_params(1, V7X_VMEM_LIMIT),
    )(dqs, dk_t, dv_t, *parts_b, qa_raw, ka_raw, x, dh1, g_pre, w_in, cos, sin, gq, gk,
      *[_dilation_perm(tm, d).T for d in DILATIONS])


def _weight_grad(a, b, name, tk1, tn, tm):
    s_len, k1 = a.shape
    n = b.shape[1]
    steps = s_len // tm

    def body(a_ref, b_ref, o_ref, acc):
        r = pl.program_id(2)
        _zero_at_first(r == 0, acc)
        acc[...] += _dot_tn(a_ref[...], b_ref[...])

        @pl.when(r == steps - 1)
        def _():
            o_ref[...] = acc[...].astype(BF16)

    return pl.pallas_call(
        body, name=name, grid=(k1 // tk1, n // tn, steps),
        in_specs=[pl.BlockSpec((tm, tk1), lambda i, j, r: (r, i)), pl.BlockSpec((tm, tn), lambda i, j, r: (r, j))],
        out_specs=pl.BlockSpec((tk1, tn), lambda i, j, r: (i, j)),
        out_shape=SDS((k1, n), BF16),
        scratch_shapes=[pltpu.VMEM((tk1, tn), F32)],
        compiler_params=_params(3, V7X_VMEM_LIMIT),
    )(a, b)


def _my_index():
    return 4 * lax.axis_index("x") + 2 * lax.axis_index("y") + lax.axis_index("c")


def _peer(k):
    f = k + 1
    x, y, c = lax.axis_index("x"), lax.axis_index("y"), lax.axis_index("c")
    return (x ^ ((f >> 2) & 1), y ^ ((f >> 1) & 1), c ^ (f & 1))


GATHER, EXCHANGE = "gather", "exchange"
ANY_SPEC = pl.BlockSpec(memory_space=pl.ANY)


def _comm_out_shapes(kind, arrays):
    return [SDS((N_DEV,) + a.shape if kind == GATHER else a.shape, a.dtype) for a in arrays]


def _comm_sems(n):
    return [pltpu.SemaphoreType.DMA((n, N_DEV - 1)), pltpu.SemaphoreType.DMA((n, N_DEV - 1)), pltpu.SemaphoreType.DMA((n,))]


def _comm_copies(kind, ins, outs, send_sems, recv_sems, local_sems):
    me = _my_index()
    local, remote = [], []
    for a, (src, dst) in enumerate(zip(ins, outs)):
        local.append(pltpu.make_async_copy(src if kind == GATHER else src.at[me], dst.at[me], local_sems.at[a]))
        for k in range(N_DEV - 1):
            px, py, pc = _peer(k)
            remote.append(pltpu.make_async_remote_copy(
                src_ref=src if kind == GATHER else src.at[4 * px + 2 * py + pc], dst_ref=dst.at[me],
                send_sem=send_sems.at[a, k], recv_sem=recv_sems.at[a, k],
                device_id=(px, py, pc), device_id_type=pl.DeviceIdType.MESH))
    return local, remote


def _comm_start(kind, ins, outs, sems):
    local, remote = _comm_copies(kind, ins, outs, *sems)
    for cp in local + remote:
        cp.start()


def _comm_wait(kind, ins, outs, sems):
    local, remote = _comm_copies(kind, ins, outs, *sems)
    for cp in local:
        cp.wait()
    for cp in remote:
        cp.wait_send()
    for cp in remote:
        cp.wait_recv()


def _collective(kind, arrays, name):
    n = len(arrays)

    def body(*refs):
        ins, outs, sems = refs[:n], refs[n:2 * n], refs[2 * n:]
        _comm_start(kind, ins, outs, sems)
        _comm_wait(kind, ins, outs, sems)

    return pl.pallas_call(
        body, name=name, in_specs=[ANY_SPEC] * n, out_specs=[ANY_SPEC] * n,
        out_shape=_comm_out_shapes(kind, arrays), scratch_shapes=_comm_sems(n),
    )(*arrays)


def _sum_adamw(parts, w, m, v, name, tr):
    rows, cols = w.shape
    c1 = 1.0 / (1.0 - ADAM_B1 ** ADAM_STEP)
    c2 = 1.0 / (1.0 - ADAM_B2 ** ADAM_STEP)

    def body(p_ref, w_ref, m_ref, v_ref, g_ref, d_ref, nm_ref, nv_ref):
        g = p_ref[0].astype(F32)
        for j in range(1, N_DEV):
            g = g + p_ref[j].astype(F32)
        g_ref[...] = g
        nm = ADAM_B1 * m_ref[...] + (1.0 - ADAM_B1) * g
        nv = ADAM_B2 * v_ref[...] + (1.0 - ADAM_B2) * (g * g)
        nm_ref[...] = nm
        nv_ref[...] = nv
        d_ref[...] = -ADAM_LR * ((nm * c1) / (jnp.sqrt(nv * c2) + ADAM_EPS) + ADAM_WD * w_ref[...])

    blk = pl.BlockSpec((tr, cols), lambda i: (i, 0))
    return pl.pallas_call(
        body, name=name, grid=(rows // tr,),
        in_specs=[pl.BlockSpec((N_DEV, tr, cols), lambda i: (0, i, 0)), blk, blk, blk],
        out_specs=[blk] * 4, out_shape=[SDS((rows, cols), F32)] * 4,
        compiler_params=_params(1, V7X_VMEM_LIMIT),
    )(parts, w, m, v)


_SMALL = (("g_attn_pre", 1024), ("g_q", 64), ("g_k", 64), ("g_out_a", 512), ("g_out_b", 512), ("g_attn_post", 1024),
          ("rel_bias", 256), ("g_mlp_pre", 1024), ("g_mlp_post", 1024), ("g_ple", 1024))
_SLAB_ROWS = 56


def _pack_small(vals):
    rows = []
    for (name, size) in _SMALL:
        flat = vals[name].reshape(-1).astype(F32)
        padded = -(-size // 128) * 128
        rows.append(jnp.pad(flat, (0, padded - size)).reshape(padded // 128, 128))
    slab = jnp.concatenate(rows, axis=0)
    return jnp.pad(slab, ((0, _SLAB_ROWS - slab.shape[0]), (0, 0)))


def _unpack_small(slab, shapes):
    out, row = {}, 0
    for (name, size) in _SMALL:
        nrow = -(-size // 128)
        out[name] = slab[row:row + nrow].reshape(-1)[:size].reshape(shapes[name])
        row += nrow
    return out


def _rope_tables(s_len):
    rows = s_len // GRID_W
    row = jnp.broadcast_to(jnp.arange(rows)[:, None], (rows, GRID_W)).reshape(-1).astype(F32)
    col = jnp.broadcast_to(jnp.arange(GRID_W)[None, :], (rows, GRID_W)).reshape(-1).astype(F32)
    n_axis = ROPE_HALF // 2
    inv_freq = ROPE_THETA ** (-jnp.arange(n_axis, dtype=F32) / n_axis)
    ang = jnp.concatenate([row[:, None] * inv_freq, col[:, None] * inv_freq], axis=-1)
    cos, sin = jnp.cos(ang), jnp.sin(ang)
    cos = jnp.tile(jnp.concatenate([cos, cos], axis=-1), (1, N_KV_A))
    sin = jnp.tile(jnp.concatenate([-sin, sin], axis=-1), (1, N_KV_A))
    return cos, sin


_RIDERS = {1: ("w_out", "w_ff1"), 4: ("w_ff2",), 16: ("w_ple_gate", "w_ple_proj")}
_GRAD_RIDERS = {1: ("w_ff1",), 4: ("w_ff2",), 16: ("w_ple_gate", "w_ple_proj", "w_out")}


ROW_TILE = 512
MLP_ROW_TILE = 256
SORT_ROW_TILE = 256
ATTN_Q_TILE = 128
ATTN_K_CHUNK = 2048
BAND_TILE = 256
GRAD_ROW_TILE = 2048


def _local_step(x, p, target, w_in, shards, small):
    axis_of = dict(_BIG)
    s_len = x.shape[0]
    tm = min(ROW_TILE, s_len)
    tq = min(ATTN_Q_TILE, s_len)
    tk = min(ATTN_K_CHUNK, s_len // 2)
    cos, sin = _rope_tables(s_len)
    gq = jnp.tile(small["g_q"], (1, N_HEADS_A))
    gk = jnp.tile(small["g_k"], (1, N_KV_A))

    qa_raw, ka_raw, qs, k_maj, kt, vt, qb, kb, vb, xn1, *views = _inproj_fwd(
        x, small["g_attn_pre"], w_in, cos, sin, gq, gk, min(SORT_ROW_TILE, s_len), tk)
    n_d = len(DILATIONS)
    qb, kb, vb = ({1: plain, **dict(zip(DILATIONS, views[a * n_d:(a + 1) * n_d]))} for a, plain in enumerate((qb, kb, vb)))

    ya, lse_a = _attn_fwd(qs, k_maj, vt, tq, tk)

    ob, lb, tiles, full = [], [], [], {}
    for (_, dil) in DILATED_PATTERNS:
        t = min(BAND_TILE, s_len // dil)
        bias, bias_t = _bias_tiles(small["rel_bias"], dil, t)
        tiles.append((t, bias, bias_t, _band_buckets(dil, t)))
        o, l, gathered = _band_fwd(qb[dil], kb[dil], vb[dil], bias, dil, t, (GATHER, [shards[n] for n in _RIDERS[dil]]))
        ob.append(o)
        lb.append(l)
        full.update({n: _assemble(g, axis_of[n]) for n, g in zip(_RIDERS[dil], gathered)})
    w_out, w_ff1, w_ff2, w_gate, w_proj = (full[n] for n in ("w_out", "w_ff1", "w_ff2", "w_ple_gate", "w_ple_proj"))

    h1, yo, yb, lse_b, ycat = _attn_out_fwd(ya, ob, lb, x, small["g_out_a"], small["g_out_b"], w_out,
                                            small["g_attn_post"], tm)
    h2, fo, xn2 = _mlp_fwd(h1, small["g_mlp_pre"], w_ff1, w_ff2, small["g_mlp_post"], min(MLP_ROW_TILE, s_len))
    dh2, loss_part, dg_ple, xn3, dgl, dpp, pb = _ple_fwd_bwd(h2, p, target, small["g_ple"], w_gate, w_proj, tm)
    dh1, dfo, du, f, dg_mlp_post, dg_mlp_pre = _mlp_bwd(h1, dh2, fo, xn2, small["g_mlp_pre"], w_ff1, w_ff2,
                                                          small["g_mlp_post"], min(MLP_ROW_TILE, s_len))
    dyo, dya, dyb, delta_a, delta_b, dg_attn_post, dg_out_a, dg_out_b, *dyb_views = _attn_out_bwd(
        dh1, yo, ya, yb, small["g_out_a"], small["g_out_b"], w_out, small["g_attn_post"], tm)
    dyb = {1: dyb, **dict(zip(DILATIONS, dyb_views))}

    dqs, dk_t, dv_t = _attn_bwd(qs, dya, lse_a, delta_a, kt, k_maj, vt, tq, tk)

    tg = min(GRAD_ROW_TILE, s_len)
    grads = {
        "w_out": _weight_grad(ycat, dyo, "grad_w_out", D_MODEL, D_MODEL, tg),
        "w_ff1": _weight_grad(xn2, du, "grad_w_ff1", D_MODEL, 1024, tg),
        "w_ff2": _weight_grad(f, dfo, "grad_w_ff2", 1024, D_MODEL, tg),
        "w_ple_gate": _weight_grad(xn3, dgl, "grad_w_ple_gate", D_MODEL, D_MODEL, tg),
        "w_ple_proj": _weight_grad(pb, dpp, "grad_w_ple_proj", D_PLE, D_MODEL, tg),
    }

    dqb, dkb, dvb, dbias, received = {}, {}, {}, [], {}
    for (_, dil), (t, bias, bias_t, _) in zip(DILATED_PATTERNS, tiles):
        rider = (EXCHANGE, [_cut(grads[n], axis_of[n]) for n in _GRAD_RIDERS[dil]])
        dqb[dil], db, got = _band_dq(qb[dil], dyb[dil], lse_b, delta_b, kb[dil], vb[dil], bias, dil, t, rider)
        dkb[dil], dvb[dil] = _band_dkv(kb[dil], vb[dil], qb[dil], dyb[dil], lse_b, delta_b, bias_t, dil, t)
        dbias.append(db)
        received.update(zip(_GRAD_RIDERS[dil], got))
    d_rel = _dbias_reduce(dbias, [jnp.asarray(tl[3]) for tl in tiles])[:, :N_HEADS_B]

    parts_b = [part[d] for part in (dqb, dkb, dvb) for d in (1,) + DILATIONS]
    dx, dproj, dg_attn_pre, dgq_lanes, dgk_lanes = _inproj_bwd(
        dqs, dk_t, dv_t, parts_b, qa_raw, ka_raw, x, dh1, small["g_attn_pre"], w_in, cos, sin, gq, gk, tm)
    grad_w_in = _weight_grad(xn1, dproj, "grad_w_in", D_MODEL, 768, tg)
    small_grads = {
        "g_attn_pre": dg_attn_pre, "g_q": dgq_lanes.reshape(N_HEADS_A, HEAD_DIM).sum(0, keepdims=True),
        "g_k": dgk_lanes.reshape(N_KV_A, HEAD_DIM).sum(0, keepdims=True), "g_out_a": dg_out_a, "g_out_b": dg_out_b,
        "g_attn_post": dg_attn_post, "rel_bias": d_rel, "g_mlp_pre": dg_mlp_pre, "g_mlp_post": dg_mlp_post,
        "g_ple": dg_ple,
    }
    return loss_part[0, 0], dx, grad_w_in, received, small_grads


_BIG = (("w_in", 1), ("w_out", 0), ("w_ff1", 1), ("w_ff2", 0), ("w_ple_gate", 0), ("w_ple_proj", 1))


def _assemble(gathered, axis):
    if axis == 0:
        return gathered.reshape(-1, gathered.shape[2])
    return gathered.transpose(1, 0, 2).reshape(gathered.shape[1], -1)


def _cut(full, axis):
    if axis == 0:
        return full.reshape(N_DEV, full.shape[0] // N_DEV, full.shape[1])
    return full.reshape(full.shape[0], N_DEV, full.shape[1] // N_DEV).transpose(1, 0, 2)


def kernel(x, p, w_in, g_attn_pre, g_q, g_k, g_out_a, g_out_b, w_out, g_attn_post, rel_bias, g_mlp_pre, w_ff1, w_ff2, g_mlp_post, g_ple, w_ple_gate, w_ple_proj, loss_target, m_w_in, m_g_attn_pre, m_g_q, m_g_k, m_g_out_a, m_g_out_b, m_w_out, m_g_attn_post, m_rel_bias, m_g_mlp_pre, m_w_ff1, m_w_ff2, m_g_mlp_post, m_g_ple, m_w_ple_gate, m_w_ple_proj, v_w_in, v_g_attn_pre, v_g_q, v_g_k, v_g_out_a, v_g_out_b, v_w_out, v_g_attn_post, v_rel_bias, v_g_mlp_pre, v_w_ff1, v_w_ff2, v_g_mlp_post, v_g_ple, v_w_ple_gate, v_w_ple_proj):
    given = dict(locals())
    small_names = [n for n, _ in _SMALL]
    small = {n: given[n] for n in small_names}
    shards = {n: given[n][0] for n, _ in _BIG}

    bf16_shards = {n: shards[n].astype(BF16) for n, _ in _BIG}
    (gathered_w_in,) = _collective(GATHER, [bf16_shards["w_in"]], "gather_w_in")

    loss_part, dx, grad_w_in, received, small_grads = _local_step(
        x[0], p[0, 0], loss_target[0], _assemble(gathered_w_in, 1), bf16_shards, small)

    loss_at = np.zeros((_SLAB_ROWS, 128), bool)
    loss_at[-1, 0] = True
    slab = jnp.where(loss_at, loss_part * (0.5 / D_MODEL), _pack_small(small_grads))
    slab_parts = jnp.broadcast_to(slab[None], (N_DEV,) + slab.shape)
    received["w_in"], slabs = _collective(EXCHANGE, [_cut(grad_w_in, 1), slab_parts], "exchange_w_in_grads")

    out_g, out_d, out_m, out_v = {}, {}, {}, {}
    for n, _ in _BIG:
        rows = shards[n].shape[0]
        g, d, nm, nv = _sum_adamw(received[n], shards[n], given["m_" + n][0], given["v_" + n][0], "adamw_" + n,
                                  min(rows, 128))
        out_g[n], out_d[n], out_m[n], out_v[n] = g[None], d[None], nm[None], nv[None]
    g, d, nm, nv = _sum_adamw(slabs, _pack_small(small), _pack_small({n: given["m_" + n] for n in small_names}),
                              _pack_small({n: given["v_" + n] for n in small_names}), "adamw_small", _SLAB_ROWS)
    shapes = {n: given[n].shape for n in small_names}
    for dst, slab_out in ((out_g, g), (out_d, d), (out_m, nm), (out_v, nv)):
        dst.update(_unpack_small(slab_out, shapes))

    loss = g[-1, 0]
    order = ["w_in", "g_attn_pre", "g_q", "g_k", "g_out_a", "g_out_b", "w_out", "g_attn_post", "rel_bias", "g_mlp_pre",
             "w_ff1", "w_ff2", "g_mlp_post", "g_ple", "w_ple_gate", "w_ple_proj"]
    return (loss, dx[None], *[out_g[n] for n in order], *[out_d[n] for n in order], *[out_m[n] for n in order],
            *[out_v[n] for n in order])
```
